```python
import math
import jax, jax.numpy as jnp
from jax import lax
import numpy as np

D_MODEL = 1024
BATCH = 8
SEQ = 4096
DEPTH = 2

HEAD_DIM = 64
ATTN_SCALE = HEAD_DIM ** -0.5
QBLK = 128
A_HEADS = 8
A_LATENT = 128
IDX_HEADS = 8
IDX_DIM = 64
IDX_SCALE = (IDX_HEADS * IDX_DIM) ** -0.5
TOPK_MAX = 256
B_HEADS = 4
C_GROUPS = ((128, 1), (512, 4), (2048, 16))
C_HEADS_PER_GROUP = 4
C_HEADS = C_HEADS_PER_GROUP * len(C_GROUPS)
N_BUCKETS = 32
MAX_DISTANCE = 2048
N_BIAS_HEADS = A_HEADS + B_HEADS + C_HEADS
D_FF = -(-8 * D_MODEL // (3 * 256)) * 256
N_BRANCH = 3
IN_SPLITS = (A_HEADS * HEAD_DIM, A_LATENT, IDX_HEADS * IDX_DIM, IDX_DIM, IDX_HEADS,
             B_HEADS * 2 * HEAD_DIM, B_HEADS * 2 * HEAD_DIM, B_HEADS * 2 * HEAD_DIM,
             C_HEADS * HEAD_DIM, C_HEADS * HEAD_DIM, C_HEADS * HEAD_DIM,
             D_MODEL, D_MODEL, D_MODEL)
IN_WIDTH = sum(IN_SPLITS)
EPS = 1e-6

kernel_name = 'hybrid_dsa_diff_dilated_gated_block'


def rmsnorm(x, g):
    xf = x.astype(jnp.float32)
    y = xf * lax.rsqrt(jnp.mean(xf * xf, axis=-1, keepdims=True) + EPS)
    return (y * g.astype(jnp.float32)).astype(x.dtype)


def split_cols(t, widths):
    offs, acc = [], 0
    for w in widths[:-1]:
        acc += w
        offs.append(acc)
    return jnp.split(t, offs, axis=-1)


def rel_bucket(dist):
    max_exact = N_BUCKETS // 2
    n = jnp.maximum(dist, 0)
    nf = jnp.maximum(n, 1).astype(jnp.float32)
    large = max_exact + (jnp.log(nf / max_exact) / math.log(MAX_DISTANCE / max_exact)
                         * (N_BUCKETS - max_exact)).astype(jnp.int32)
    large = jnp.minimum(large, N_BUCKETS - 1)
    return jnp.where(n < max_exact, n, large)


def dsa_attention(q_lat, kv_lat, q_idx, k_idx, w_idx, bias_tab):
    B, S = kv_lat.shape[:2]
    topk = min(TOPK_MAX, S // 4)
    key_pos = jnp.arange(S)

    def block(i):
        start = i * QBLK
        qi = lax.dynamic_slice_in_dim(q_idx, start, QBLK, axis=1)
        wi = lax.dynamic_slice_in_dim(w_idx, start, QBLK, axis=1)
        ql = lax.dynamic_slice_in_dim(q_lat, start, QBLK, axis=1)
        q_pos = start + jnp.arange(QBLK)
        score = jnp.einsum('bqh,bqhs->bqs', wi,
                           jax.nn.relu(jnp.einsum('bqhd,bsd->bqhs', qi, k_idx))).astype(jnp.float32)
        score = jnp.where((key_pos[None, :] <= q_pos[:, None])[None], score, -jnp.inf)
        _, sel = lax.top_k(score, topk)
        kv_sel = jax.vmap(lambda kv, ix: kv[ix])(kv_lat, sel)
        dist = q_pos[None, :, None] - sel
        bias = jnp.moveaxis(bias_tab[rel_bucket(dist)], -1, 2)
        logits = jnp.einsum('bqhc,bqkc->bqhk', ql, kv_sel) * ATTN_SCALE + bias
        logits = jnp.where((dist >= 0)[:, :, None, :], logits.astype(jnp.float32), -jnp.inf)
        p = jax.nn.softmax(logits, axis=-1).astype(kv_sel.dtype)
        return jnp.einsum('bqhk,bqkc->bqhc', p, kv_sel)

    out = lax.map(block, jnp.arange(S // QBLK))
    return jnp.moveaxis(out, 0, 1).reshape(B, S, A_HEADS, -1)


def diff_attention(q, k, v, lam, bias_tab):
    B, S = q.shape[:2]
    key_pos = jnp.arange(S)

    def block(i):
        start = i * QBLK
        qb = lax.dynamic_slice_in_dim(q, start, QBLK, axis=1)
        q_pos = start + jnp.arange(QBLK)
        dist = q_pos[:, None] - key_pos[None, :]
        bias = jnp.moveaxis(bias_tab[rel_bucket(dist)], -1, 0)
        logits = jnp.einsum('bqhmd,bshmd->bmhqs', qb, k) * ATTN_SCALE + bias
        logits = jnp.where(dist >= 0, logits.astype(jnp.float32), -jnp.inf)
        p = jax.nn.softmax(logits, axis=-1)
        attn = (p[:, 0] - lam * p[:, 1]).astype(v.dtype)
        return jnp.einsum('bhqs,bshe->bqhe', attn, v)

    out = lax.map(block, jnp.arange(S // QBLK))
    return jnp.moveaxis(out, 0, 1).reshape(B, S, B_HEADS, -1)


def dilated_group(q, k, v, dil, n_back, bias_tab):
    B, S, H, E = q.shape
    M = S // dil
    nb = -(-M // n_back)
    Mp = nb * n_back

    def by_residue(t):
        return t.reshape(B, M, dil, H, E).transpose(0, 2, 1, 3, 4)

    qr = jnp.pad(by_residue(q), ((0, 0), (0, 0), (0, Mp - M), (0, 0), (0, 0))).reshape(B, dil, nb, n_back, H, E)

    def band(t):
        tp = jnp.pad(by_residue(t), ((0, 0), (0, 0), (n_back, Mp - M), (0, 0), (0, 0)))
        tp = tp.reshape(B, dil, nb + 1, n_back, H, E)
        return jnp.concatenate([tp[:, :, :-1], tp[:, :, 1:]], axis=3)

    kb, vb = band(k), band(v)
    i = jnp.arange(n_back)[:, None]
    j = jnp.arange(2 * n_back)[None, :]
    rel = i - j + n_back
    key_ok = (jnp.arange(nb)[:, None] * n_back + j - n_back) >= 0
    mask = ((rel >= 0) & (rel <= n_back))[None] & key_ok[:, None, :]
    bias = jnp.moveaxis(bias_tab[rel_bucket(rel * dil)], -1, 0)
    logits = jnp.einsum('bdcqhe,bdckhe->bdchqk', qr, kb) * ATTN_SCALE + bias
    logits = jnp.where(mask[:, None], logits.astype(jnp.float32), -jnp.inf)
    lse = jax.nn.logsumexp(logits, axis=-1)
    p = jnp.exp(logits - lse[..., None]).astype(v.dtype)
    out = jnp.einsum('bdchqk,bdckhe->bdcqhe', p, vb)
    out = out.reshape(B, dil, Mp, H, E)[:, :, :M].transpose(0, 2, 1, 3, 4).reshape(B, S, H, E)
    lse = lse.transpose(0, 1, 2, 4, 3).reshape(B, dil, Mp, H)[:, :, :M].transpose(0, 2, 1, 3).reshape(B, S, H)
    return out, lse


def dilated_attention(q, k, v, bias_tab):
    outs, lses = [], []
    for g, (window, dil) in enumerate(C_GROUPS):
        hs = slice(g * C_HEADS_PER_GROUP, (g + 1) * C_HEADS_PER_GROUP)
        o, s = dilated_group(q[:, :, hs], k[:, :, hs], v[:, :, hs], dil, window // dil, bias_tab[:, hs])
        outs.append(o)
        lses.append(s)
    alpha = jax.nn.softmax(jnp.stack(lses), axis=0).astype(q.dtype)
    return jnp.einsum('gbsh,gbshe->bshe', alpha, jnp.stack(outs))


def setup_inputs(seed: int = 0) -> dict:
    key = jax.random.key(seed)
    ks = jax.random.split(key, 24)
    f32 = jnp.float32
    nrm = lambda k, shape, s: jax.random.normal(k, shape, f32) * s
    gain = lambda k, shape: 1.0 + 0.02 * jax.random.normal(k, shape, f32)
    L, D = DEPTH, D_MODEL
    return {
        'x': nrm(ks[0], (BATCH, SEQ, D), 1.0),
        'c': nrm(ks[1], (BATCH, D), 1.0),
        'w_ada': nrm(ks[2], (L, D, 6 * D), D ** -0.5),
        'b_ada': nrm(ks[3], (L, 6 * D), 0.02),
        'g_norm1': gain(ks[4], (L, D)),
        'w_in': nrm(ks[5], (L, D, IN_WIDTH), D ** -0.5),
        'w_uk': nrm(ks[6], (L, A_HEADS, HEAD_DIM, A_LATENT), HEAD_DIM ** -0.5),
        'w_uv': nrm(ks[7], (L, A_HEADS, A_LATENT, HEAD_DIM), A_LATENT ** -0.5),
        'g_kv': gain(ks[8], (L, A_LATENT)),
        'lam_q1': nrm(ks[9], (L, HEAD_DIM), 0.1),
        'lam_k1': nrm(ks[10], (L, HEAD_DIM), 0.1),
        'lam_q2': nrm(ks[11], (L, HEAD_DIM), 0.1),
        'lam_k2': nrm(ks[12], (L, HEAD_DIM), 0.1),
        'g_subln': gain(ks[13], (L, 2 * HEAD_DIM)),
        'w_branch_a': nrm(ks[14], (L, A_HEADS * HEAD_DIM, D), (A_HEADS * HEAD_DIM) ** -0.5),
        'w_branch_b': nrm(ks[15], (L, B_HEADS * 2 * HEAD_DIM, D), (B_HEADS * 2 * HEAD_DIM) ** -0.5),
        'w_branch_c': nrm(ks[16], (L, C_HEADS_PER_GROUP * HEAD_DIM, D), (C_HEADS_PER_GROUP * HEAD_DIM) ** -0.5),
        'w_out': nrm(ks[17], (L, D, D), D ** -0.5),
        'g_norm2': gain(ks[18], (L, D)),
        'w_gate_up': nrm(ks[19], (L, D, 2 * D_FF), D ** -0.5),
        'w_down': nrm(ks[20], (L, D_FF, D), D_FF ** -0.5),
        'rel_bias': nrm(ks[21], (N_BUCKETS, N_BIAS_HEADS), 0.5),
        'g_final': gain(ks[22], (D,)),
    }


def reference(x, c, w_ada, b_ada, g_norm1, w_in, w_uk, w_uv, g_kv, lam_q1, lam_k1, lam_q2, lam_k2,
              g_subln, w_branch_a, w_branch_b, w_branch_c, w_out, g_norm2, w_gate_up, w_down,
              rel_bias, g_final):
    B, S, D = x.shape
    bias_a = rel_bias[:, :A_HEADS]
    bias_b = rel_bias[:, A_HEADS:A_HEADS + B_HEADS]
    bias_c = rel_bias[:, A_HEADS + B_HEADS:]
    c_act = jax.nn.silu(c)
    for l in range(DEPTH):
        mod = (c_act @ w_ada[l] + b_ada[l])[:, None, :]
        sh1, sc1, gt1, sh2, sc2, gt2 = jnp.split(mod, 6, axis=-1)
        h = rmsnorm(x, g_norm1[l]) * (1 + sc1) + sh1
        (a_q, a_kv, i_q, i_k, i_w, b_q, b_k, b_v, c_q, c_k, c_v,
         z_a, z_b, z_c) = split_cols(h @ w_in[l], IN_SPLITS)
        q_lat = jnp.einsum('bshd,hdc->bshc', a_q.reshape(B, S, A_HEADS, HEAD_DIM), w_uk[l])
        kv_lat = rmsnorm(a_kv, g_kv[l])
        o_lat = dsa_attention(q_lat, kv_lat, i_q.reshape(B, S, IDX_HEADS, IDX_DIM), i_k,
                              i_w * IDX_SCALE, bias_a)
        o_a = jnp.einsum('bshc,hce->bshe', o_lat, w_uv[l]).reshape(B, S, -1)
        lam_init = 0.8 - 0.6 * math.exp(-0.3 * l)
        lam = (jnp.exp(jnp.sum(lam_q1[l] * lam_k1[l]).astype(jnp.float32))
               - jnp.exp(jnp.sum(lam_q2[l] * lam_k2[l]).astype(jnp.float32)) + lam_init)
        o_b = diff_attention(b_q.reshape(B, S, B_HEADS, 2, HEAD_DIM), b_k.reshape(B, S, B_HEADS, 2, HEAD_DIM),
                             b_v.reshape(B, S, B_HEADS, 2 * HEAD_DIM), lam, bias_b)
        o_b = (rmsnorm(o_b, g_subln[l]) * (1 - lam_init)).reshape(B, S, -1)
        o_c = dilated_attention(c_q.reshape(B, S, C_HEADS, HEAD_DIM), c_k.reshape(B, S, C_HEADS, HEAD_DIM),
                                c_v.reshape(B, S, C_HEADS, HEAD_DIM), bias_c).reshape(B, S, -1)
        merged = (jax.nn.sigmoid(z_a) * (o_a @ w_branch_a[l])
                  + jax.nn.sigmoid(z_b) * (o_b @ w_branch_b[l])
                  + jax.nn.sigmoid(z_c) * (o_c @ w_branch_c[l]))
        x = x + gt1 * (merged @ w_out[l])
        h2 = rmsnorm(x, g_norm2[l]) * (1 + sc2) + sh2
        f_g, f_u = jnp.split(h2 @ w_gate_up[l], 2, axis=-1)
        x = x + gt2 * ((jax.nn.silu(f_g) * f_u) @ w_down[l])
    return rmsnorm(x, g_final)
```

```python
import functools
import math

import numpy as np
import jax
import jax.numpy as jnp
from jax import lax
from jax.experimental import pallas as pl
from jax.experimental.pallas import tpu as pltpu

D_MODEL = 1024
HEAD_DIM = 64
ATTN_SCALE = HEAD_DIM ** -0.5
A_HEADS = 8
A_LATENT = 128
IDX_HEADS = 8
IDX_DIM = 64
IDX_SCALE = (IDX_HEADS * IDX_DIM) ** -0.5
TOPK_MAX = 256
B_HEADS = 4
C_GROUPS = ((128, 1), (512, 4), (2048, 16))
C_HPG = 4
C_HEADS = C_HPG * len(C_GROUPS)
N_BUCKETS = 32
MAX_DISTANCE = 2048
N_BIAS_HEADS = A_HEADS + B_HEADS + C_HEADS
D_FF = -(-8 * D_MODEL // (3 * 256)) * 256
EPS = 1e-6

MM_DTYPE = jnp.bfloat16
F32 = jnp.float32
I32 = jnp.int32

ATT_TILE = 256
N_OFFSETS = 8
C_BAND = 128
ROW_TILE = 512
NEG = -1e30
INT_MIN = -2 ** 31
VMEM_LIMIT = 56 * 1024 * 1024


def _cparams(n_axes, vmem=VMEM_LIMIT):
    return pltpu.CompilerParams(dimension_semantics=("arbitrary",) * n_axes,
                                vmem_limit_bytes=vmem)


def _const_spec(shape):
    nd = len(shape)
    return pl.BlockSpec(shape, lambda *_: (0,) * nd, pipeline_mode=pl.Buffered(1))


def _bucket_thresholds(max_dist):
    n = np.arange(max_dist + 1)
    max_exact = N_BUCKETS // 2
    nf = np.maximum(n, 1).astype(np.float32)
    large = max_exact + (np.log(nf / np.float32(max_exact))
                         / np.float32(math.log(MAX_DISTANCE / max_exact))
                         * np.float32(N_BUCKETS - max_exact)).astype(np.int32)
    large = np.minimum(large, N_BUCKETS - 1)
    bucket = np.where(n < max_exact, n, large)
    assert np.all(np.diff(bucket) >= 0)
    thr = []
    for k in range(1, N_BUCKETS):
        idx = np.nonzero(bucket >= k)[0]
        thr.append(int(idx[0]) if idx.size else None)
    return thr


def _bias_from_dist(dist, tab_ref, col, thresholds):
    b = jnp.full(dist.shape, tab_ref[col], F32)
    for k, thr in enumerate(thresholds, start=1):
        if thr is None:
            break
        b = jnp.where(dist >= thr, tab_ref[k * N_BIAS_HEADS + col], b)
    return b


def _bias_att_kernel(tab_ref, out_ref, *, head0, tile, thresholds):
    h = pl.program_id(0)
    o = pl.program_id(1)
    row = lax.broadcasted_iota(I32, (tile, tile), 0)
    colq = lax.broadcasted_iota(I32, (tile, tile), 1)
    dist = jnp.maximum(o * tile + colq - row, 0)
    out_ref[...] = _bias_from_dist(dist, tab_ref, head0 + h, thresholds)


def _bias_att_tiles(tab, head0, n_heads, tile, thresholds):
    return pl.pallas_call(
        functools.partial(_bias_att_kernel, head0=head0, tile=tile, thresholds=thresholds),
        out_shape=jax.ShapeDtypeStruct((n_heads, N_OFFSETS, tile, tile), F32),
        grid=(n_heads, N_OFFSETS),
        in_specs=[pl.BlockSpec(memory_space=pltpu.SMEM)],
        out_specs=pl.BlockSpec((None, None, tile, tile), lambda h, o: (h, o, 0, 0)),
        compiler_params=_cparams(2),
        name="bias_att_tiles",
    )(tab)


def _bias_dil_kernel(tab_ref, out_ref, *, head0, thresholds):
    h = pl.program_id(0)
    g = h // C_HPG
    dil = jnp.where(g == 0, C_GROUPS[0][1], jnp.where(g == 1, C_GROUPS[1][1], C_GROUPS[2][1]))
    i = lax.broadcasted_iota(I32, (C_BAND, 2 * C_BAND), 0)
    j = lax.broadcasted_iota(I32, (C_BAND, 2 * C_BAND), 1)
    dist = jnp.maximum((i - j + C_BAND) * dil, 0)
    out_ref[...] = _bias_from_dist(dist, tab_ref, head0 + h, thresholds)


def _bias_dil_tiles(tab, head0, thresholds):
    return pl.pallas_call(
        functools.partial(_bias_dil_kernel, head0=head0, thresholds=thresholds),
        out_shape=jax.ShapeDtypeStruct((C_HEADS, C_BAND, 2 * C_BAND), F32),
        grid=(C_HEADS,),
        in_specs=[pl.BlockSpec(memory_space=pltpu.SMEM)],
        out_specs=pl.BlockSpec((None, C_BAND, 2 * C_BAND), lambda h: (h, 0, 0)),
        compiler_params=_cparams(1),
        name="bias_dil_tiles",
    )(tab)


def _mod_kernel(c_ref, w_ref, b_ref, out_ref):
    c = c_ref[...]
    ca = (c * jax.nn.sigmoid(c)).astype(MM_DTYPE)
    out_ref[...] = jnp.dot(ca, w_ref[...].astype(MM_DTYPE), preferred_element_type=F32) + b_ref[...]


def _modulation(c, w_ada, b_ada):
    depth, d, wid = w_ada.shape
    bsz = c.shape[0]
    tn = 1536
    assert wid % tn == 0
    return pl.pallas_call(
        _mod_kernel,
        out_shape=jax.ShapeDtypeStruct((depth, bsz, wid), F32),
        grid=(depth, wid // tn),
        in_specs=[pl.BlockSpec((bsz, d), lambda l, j: (0, 0)),
                  pl.BlockSpec((None, d, tn), lambda l, j: (l, 0, j)),
                  pl.BlockSpec((None, 1, tn), lambda l, j: (l, 0, j))],
        out_specs=pl.BlockSpec((None, bsz, tn), lambda l, j: (l, 0, j)),
        compiler_params=_cparams(2),
        name="adaln_modulation",
    )(c, w_ada, b_ada.reshape(depth, 1, wid))


def _rms(x):
    return x * lax.rsqrt(jnp.mean(x * x, axis=-1, keepdims=True) + EPS)


def _in_kernel(x_ref, mod_ref, g1_ref, gkv_ref,
               w_aq, w_iq, w_ik, w_iw, w_kv, w_bq, w_bk, w_bv, w_cq, w_ck, w_cv,
               o_aq, o_iq, o_ik, o_iw, o_kv, o_bq, o_bk, o_bv, o_cq, o_ck, o_cv):
    h = _rms(x_ref[...]) * g1_ref[...]
    h = h * (1.0 + mod_ref[1:2, :]) + mod_ref[0:1, :]
    hb = h.astype(MM_DTYPE)

    def mm(w):
        return jnp.dot(hb, w[...], preferred_element_type=F32)

    for w, o in ((w_aq, o_aq), (w_iq, o_iq), (w_ik, o_ik), (w_bq, o_bq), (w_bk, o_bk),
                 (w_bv, o_bv), (w_cq, o_cq), (w_ck, o_ck), (w_cv, o_cv)):
        o[...] = mm(w).astype(o.dtype)
    o_iw[...] = mm(w_iw) * IDX_SCALE
    o_kv[...] = (_rms(mm(w_kv)) * gkv_ref[...]).astype(o_kv.dtype)


def _in_proj(x2d, mod_l, g1, gkv, ws, seq):
    n, d = x2d.shape
    tm = ROW_TILE
    per_b = seq // tm
    widths = [w.shape[1] for w in ws]
    dts = [MM_DTYPE, MM_DTYPE, MM_DTYPE, F32, MM_DTYPE] + [MM_DTYPE] * 6
    in_specs = [pl.BlockSpec((tm, d), lambda i: (i, 0)),
                pl.BlockSpec((None, 6, d), lambda i: (i // per_b, 0, 0)),
                _const_spec((1, d)), _const_spec((1, A_LATENT))]
    in_specs += [_const_spec(w.shape) for w in ws]
    out_specs = [pl.BlockSpec((tm, wd), lambda i: (i, 0)) for wd in widths]
    out_shape = [jax.ShapeDtypeStruct((n, wd), dt) for wd, dt in zip(widths, dts)]
    return pl.pallas_call(
        _in_kernel, out_shape=out_shape, grid=(n // tm,),
        in_specs=in_specs, out_specs=out_specs,
        compiler_params=_cparams(1), name="in_proj",
    )(x2d, mod_l, g1, gkv, *ws)


def _dsa_kernel(aqT_ref, iqT_ref, iwT_ref, kidx_ref, kv_ref, kvT_ref, bias_ref, wukT_ref, wuvT_ref,
                out_ref, keys_ref, qlat_ref, acc_ref, m_ref, l_ref, *, topk, idx_bits):
    T = ATT_TILE
    qi = pl.program_id(1)
    nk = qi + 1

    for h in range(A_HEADS):
        q = jnp.dot(wukT_ref[h], aqT_ref[h * HEAD_DIM:(h + 1) * HEAD_DIM, :],
                    preferred_element_type=F32) * ATTN_SCALE
        qlat_ref[h] = q.astype(qlat_ref.dtype)

    s_loc = lax.broadcasted_iota(I32, (T, T), 0)
    t_loc = lax.broadcasted_iota(I32, (T, T), 1)

    def score_tile(kj, carry):
        kt = kidx_ref[kj]
        acc = jnp.zeros((T, T), F32)
        for h in range(IDX_HEADS):
            s = jnp.dot(kt, iqT_ref[h * IDX_DIM:(h + 1) * IDX_DIM, :], preferred_element_type=F32)
            acc = acc + jnp.maximum(s, 0.0) * iwT_ref[h:h + 1, :]
        bits = lax.bitcast_convert_type(acc, I32)
        key = jnp.where(bits < 0, bits ^ jnp.int32(0x7FFFFFFF), bits)
        visible = (kj * T + s_loc) <= (qi * T + t_loc)
        keys_ref[kj] = jnp.where(visible, key, jnp.int32(INT_MIN))
        return carry

    lax.fori_loop(0, nk, score_tile, 0)

    one = jnp.int32(1)
    nil = jnp.int32(0)

    def count(hit_fn):
        def body(kj, c):
            hit = hit_fn(keys_ref[kj], kj)
            return c + jnp.sum(hit.reshape(T // 8, 8, T), axis=0)
        c = lax.fori_loop(0, nk, body, jnp.zeros((8, T), I32))
        return jnp.sum(c, axis=0, keepdims=True)

    zero = jnp.zeros((1, T), I32)
    kth = jnp.where(count(lambda kk, kj: jnp.where(kk >= zero, one, nil)) >= topk,
                    nil, jnp.int32(INT_MIN))

    def bit_step(i, kth):
        cand = kth | jnp.left_shift(one, jnp.int32(30) - i)
        return jnp.where(count(lambda kk, kj: jnp.where(kk >= cand, one, nil)) >= topk, cand, kth)

    kth = lax.fori_loop(0, 31, bit_step, kth)
    kth = jnp.maximum(kth, jnp.int32(INT_MIN + 1))
    n_ge = count(lambda kk, kj: jnp.where(kk >= kth, one, nil))

    @pl.when(jnp.max(n_ge) > topk)
    def _():
        need = topk - count(lambda kk, kj: jnp.where(kk > kth, one, nil))

        def pos_step(i, cut):
            cand = cut | jnp.left_shift(one, jnp.int32(idx_bits - 1) - i)
            c = count(lambda kk, kj: jnp.where(
                kk == kth, jnp.where((kj * T + s_loc) < cand, one, nil), nil))
            return jnp.where(c < need, cand, cut)

        cut = lax.fori_loop(0, idx_bits, pos_step, zero)

        def demote(kj, carry):
            kk = keys_ref[kj]
            lowered = jnp.where((kj * T + s_loc) > cut, kth - one, kk)
            keys_ref[kj] = jnp.where(kk == kth, lowered, kk)
            return carry

        lax.fori_loop(0, nk, demote, 0)

    m_ref[...] = jnp.full(m_ref.shape, NEG, F32)
    l_ref[...] = jnp.zeros(l_ref.shape, F32)
    acc_ref[...] = jnp.zeros(acc_ref.shape, F32)

    def attend(kj, carry):
        sel = keys_ref[kj] >= kth
        kvt = kv_ref[kj]
        kvTt = kvT_ref[kj]
        off = jnp.minimum(qi - kj, N_OFFSETS - 1)
        for h in range(A_HEADS):
            s = jnp.dot(kvt, qlat_ref[h], preferred_element_type=F32) + bias_ref[h, off]
            s = jnp.where(sel, s, NEG)
            m_old = m_ref[h:h + 1, :]
            m_new = jnp.maximum(m_old, jnp.max(s, axis=0, keepdims=True))
            p = jnp.exp(s - m_new)
            alpha = jnp.exp(m_old - m_new)
            l_ref[h:h + 1, :] = alpha * l_ref[h:h + 1, :] + jnp.sum(p, axis=0, keepdims=True)
            acc_ref[h] = alpha * acc_ref[h] + jnp.dot(kvTt, p.astype(MM_DTYPE),
                                                      preferred_element_type=F32)
            m_ref[h:h + 1, :] = m_new
        return carry

    lax.fori_loop(0, nk, attend, 0)

    for h in range(A_HEADS):
        o_lat = (acc_ref[h] / l_ref[h:h + 1, :]).astype(MM_DTYPE)
        out_ref[h * HEAD_DIM:(h + 1) * HEAD_DIM, :] = jnp.dot(
            wuvT_ref[h], o_lat, preferred_element_type=F32).astype(out_ref.dtype)


def _dsa(aqT, iqT, iwT, kidx, kv, kvT, bias_a, wukT, wuvT):
    bsz, _, seq = aqT.shape
    T = ATT_TILE
    nk = seq // T
    topk = min(TOPK_MAX, seq // 4)
    idx_bits = int(math.log2(seq))
    assert 2 ** idx_bits == seq
    qspec = lambda rows: pl.BlockSpec((None, rows, T), lambda b, i: (b, 0, i))
    kspec = lambda a, c: pl.BlockSpec((None, nk, a, c), lambda b, i: (b, 0, 0, 0))
    return pl.pallas_call(
        functools.partial(_dsa_kernel, topk=topk, idx_bits=idx_bits),
        out_shape=jax.ShapeDtypeStruct((bsz, A_HEADS * HEAD_DIM, seq), MM_DTYPE),
        grid=(bsz, nk),
        in_specs=[qspec(A_HEADS * HEAD_DIM), qspec(IDX_HEADS * IDX_DIM), qspec(IDX_HEADS),
                  kspec(T, IDX_DIM), kspec(T, A_LATENT), kspec(A_LATENT, T),
                  _const_spec(bias_a.shape), _const_spec(wukT.shape), _const_spec(wuvT.shape)],
        out_specs=qspec(A_HEADS * HEAD_DIM),
        scratch_shapes=[pltpu.VMEM((nk, T, T), I32),
                        pltpu.VMEM((A_HEADS, A_LATENT, T), MM_DTYPE),
                        pltpu.VMEM((A_HEADS, A_LATENT, T), F32),
                        pltpu.VMEM((A_HEADS, T), F32),
                        pltpu.VMEM((A_HEADS, T), F32)],
        compiler_params=_cparams(2), name="dsa_attention",
    )(aqT, iqT, iwT, kidx, kv, kvT, bias_a, wukT, wuvT)


def _diff_kernel(qT_ref, k_ref, vT_ref, bias_ref, lam_ref, gsub_ref, out_ref,
                 acc_ref, m_ref, l_ref, *, lam_init):
    T = ATT_TILE
    qi = pl.program_id(2)
    q = qT_ref[...].astype(F32) * ATTN_SCALE
    half = lax.broadcasted_iota(I32, q.shape, 0) < HEAD_DIM
    qz = (jnp.where(half, q, 0.0).astype(MM_DTYPE), jnp.where(half, 0.0, q).astype(MM_DTYPE))
    s_loc = lax.broadcasted_iota(I32, (T, T), 0)
    t_loc = lax.broadcasted_iota(I32, (T, T), 1)

    m_ref[...] = jnp.full(m_ref.shape, NEG, F32)
    l_ref[...] = jnp.zeros(l_ref.shape, F32)
    acc_ref[...] = jnp.zeros(acc_ref.shape, F32)

    def step(kj, diagonal):
        kt = k_ref[kj]
        vt = vT_ref[kj]
        b = bias_ref[jnp.minimum(qi - kj, N_OFFSETS - 1)]
        for m in range(2):
            s = jnp.dot(kt, qz[m], preferred_element_type=F32) + b
            if diagonal:
                s = jnp.where(s_loc <= t_loc, s, NEG)
            m_old = m_ref[m:m + 1, :]
            m_new = jnp.maximum(m_old, jnp.max(s, axis=0, keepdims=True))
            p = jnp.exp(s - m_new)
            alpha = jnp.exp(m_old - m_new)
            l_ref[m:m + 1, :] = alpha * l_ref[m:m + 1, :] + jnp.sum(p, axis=0, keepdims=True)
            acc_ref[m] = alpha * acc_ref[m] + jnp.dot(vt, p.astype(MM_DTYPE),
                                                      preferred_element_type=F32)
            m_ref[m:m + 1, :] = m_new

    def body(kj, carry):
        step(kj, False)
        return carry

    lax.fori_loop(0, qi, body, 0)
    step(qi, True)

    lr = lam_ref[...]
    lam = (jnp.exp(jnp.sum(lr[0:1, :] * lr[1:2, :], axis=1, keepdims=True))
           - jnp.exp(jnp.sum(lr[2:3, :] * lr[3:4, :], axis=1, keepdims=True)) + lam_init)
    attn = acc_ref[0] / l_ref[0:1, :] - lam * (acc_ref[1] / l_ref[1:2, :])
    y = attn * lax.rsqrt(jnp.mean(attn * attn, axis=0, keepdims=True) + EPS)
    out_ref[...] = (y * gsub_ref[...] * (1.0 - lam_init)).astype(out_ref.dtype)


def _diff(bqT, bk, bvT, bias_b, lam_rows, gsub, lam_init):
    bsz, _, seq = bqT.shape
    T = ATT_TILE
    nk = seq // T
    dv = 2 * HEAD_DIM
    return pl.pallas_call(
        functools.partial(_diff_kernel, lam_init=lam_init),
        out_shape=jax.ShapeDtypeStruct((bsz, B_HEADS * dv, seq), MM_DTYPE),
        grid=(bsz, B_HEADS, nk),
        in_specs=[pl.BlockSpec((None, dv, T), lambda b, h, i: (b, h, i)),
                  pl.BlockSpec((None, nk, T, dv), lambda b, h, i: (b, 0, 0, h)),
                  pl.BlockSpec((None, None, nk, dv, T), lambda b, h, i: (b, h, 0, 0, 0)),
                  pl.BlockSpec((None, N_OFFSETS, T, T), lambda b, h, i: (h, 0, 0, 0)),
                  pl.BlockSpec((4, HEAD_DIM), lambda b, h, i: (0, 0)),
                  pl.BlockSpec((dv, 1), lambda b, h, i: (0, 0))],
        out_specs=pl.BlockSpec((None, dv, T), lambda b, h, i: (b, h, i)),
        scratch_shapes=[pltpu.VMEM((2, dv, T), F32),
                        pltpu.VMEM((2, T), F32),
                        pltpu.VMEM((2, T), F32)],
        compiler_params=_cparams(3), name="diff_attention",
    )(bqT, bk, bvT, bias_b, lam_rows, gsub)


def _dil_kernel(q_ref, k_ref, v_ref, kh_ref, vh_ref, bias_ref, out_ref, lse_ref, *, tq):
    n = C_BAND
    wid = C_HPG * HEAD_DIM
    halo_lo = jnp.where(pl.program_id(2) == 0, jnp.int32(n), jnp.int32(0))
    i = lax.broadcasted_iota(I32, (n, 2 * n), 0)
    j = lax.broadcasted_iota(I32, (n, 2 * n), 1)
    lane_head = lax.broadcasted_iota(I32, (n, wid), 1) // HEAD_DIM
    for c in range(tq // n):
        q = q_ref[c * n:(c + 1) * n, :].astype(F32) * ATTN_SCALE
        if c == 0:
            keys = jnp.concatenate([kh_ref[...], k_ref[0:n, :]], axis=0)
            vals = jnp.concatenate([vh_ref[...], v_ref[0:n, :]], axis=0)
            lo = jnp.maximum(i, halo_lo)
        else:
            keys = k_ref[(c - 1) * n:(c + 1) * n, :]
            vals = v_ref[(c - 1) * n:(c + 1) * n, :]
            lo = i
        out = jnp.zeros((n, wid), F32)
        lse = jnp.zeros((n, wid), F32)
        for h in range(C_HPG):
            in_head = lane_head == h
            qh = jnp.where(in_head, q, 0.0).astype(MM_DTYPE)
            s = lax.dot_general(qh, keys, (((1,), (1,)), ((), ())),
                                preferred_element_type=F32) + bias_ref[h]
            s = jnp.where(j >= lo, jnp.where(j <= i + n, s, NEG), NEG)
            m = jnp.max(s, axis=1, keepdims=True)
            p = jnp.exp(s - m)
            den = jnp.sum(p, axis=1, keepdims=True)
            o = jnp.dot(p.astype(MM_DTYPE), vals, preferred_element_type=F32) / den
            out = jnp.where(in_head, o, out)
            lse = jnp.where(in_head, m + jnp.log(den), lse)
        out_ref[c * n:(c + 1) * n, :] = out
        lse_ref[c * n:(c + 1) * n, :] = lse


def _dilated_group(cq, ck, cv, bias_g, g, dil):
    bsz, seq, cw = cq.shape
    wid = C_HPG * HEAD_DIM
    n = C_BAND
    m = seq // dil
    assert m % n == 0
    tq = min(m, 512)
    ncol = cw // wid
    view = lambda t: t.reshape(bsz, m, dil * cw)
    cur = pl.BlockSpec((None, tq, wid), lambda b, r, i: (b, i, r * ncol + g))
    halo = pl.BlockSpec((None, n, wid),
                        lambda b, r, i: (b, jnp.maximum(i * (tq // n) - 1, 0), r * ncol + g))
    outspec = pl.BlockSpec((None, tq, wid), lambda b, r, i: (b, i, r))
    out, lse = pl.pallas_call(
        functools.partial(_dil_kernel, tq=tq),
        out_shape=[jax.ShapeDtypeStruct((bsz, m, dil * wid), F32)] * 2,
        grid=(bsz, dil, m // tq),
        in_specs=[cur, cur, cur, halo, halo,
                  pl.BlockSpec((C_HPG, n, 2 * n), lambda b, r, i: (0, 0, 0))],
        out_specs=[outspec, outspec],
        compiler_params=_cparams(3), name=f"dilated_group{g}",
    )(view(cq), view(ck), view(cv), view(ck), view(cv), bias_g)
    return out.reshape(bsz, seq, wid), lse.reshape(bsz, seq, wid)


def _merge_kernel(x_ref, mod_ref, g1_ref, oa_ref, ob_ref,
                  c0_ref, c1_ref, c2_ref, s0_ref, s1_ref, s2_ref,
                  wza, wzb, wzc, wba, wbb, wbc, wo, out_ref):
    x = x_ref[...]
    h = _rms(x) * g1_ref[...]
    h = h * (1.0 + mod_ref[1:2, :]) + mod_ref[0:1, :]
    hb = h.astype(MM_DTYPE)

    s0, s1, s2 = s0_ref[...], s1_ref[...], s2_ref[...]
    mx = jnp.maximum(jnp.maximum(s0, s1), s2)
    e0, e1, e2 = jnp.exp(s0 - mx), jnp.exp(s1 - mx), jnp.exp(s2 - mx)
    oc = (e0 * c0_ref[...] + e1 * c1_ref[...] + e2 * c2_ref[...]) / (e0 + e1 + e2)

    def gated(wz, o, wb):
        z = jnp.dot(hb, wz[...], preferred_element_type=F32)
        return jax.nn.sigmoid(z) * jnp.dot(o, wb[...], preferred_element_type=F32)

    merged = (gated(wza, oa_ref[...], wba) + gated(wzb, ob_ref[...], wbb)
              + gated(wzc, oc.astype(MM_DTYPE), wbc))
    y = jnp.dot(merged.astype(MM_DTYPE), wo[...], preferred_element_type=F32)
    out_ref[...] = x + mod_ref[2:3, :] * y


def _merge(x2d, mod_l, g1, oa, ob, ocs, lses, ws, seq):
    n, d = x2d.shape
    tm = ROW_TILE
    per_b = seq // tm
    row = lambda wd: pl.BlockSpec((tm, wd), lambda i: (i, 0))
    wid = C_HPG * HEAD_DIM
    in_specs = [row(d), pl.BlockSpec((None, 6, d), lambda i: (i // per_b, 0, 0)), _const_spec((1, d)),
                row(oa.shape[1]), row(ob.shape[1])] + [row(wid)] * 6
    in_specs += [_const_spec(w.shape) for w in ws]
    return pl.pallas_call(
        _merge_kernel, out_shape=jax.ShapeDtypeStruct((n, d), F32), grid=(n // tm,),
        in_specs=in_specs, out_specs=row(d),
        compiler_params=_cparams(1), name="gated_merge",
    )(x2d, mod_l, g1, oa, ob, *ocs, *lses, *ws)


def _ffn_kernel(x_ref, mod_ref, g2_ref, gf_ref, wgu, wd, out_ref, *, chunk, final_norm):
    x = x_ref[...]
    h = _rms(x) * g2_ref[...]
    h = h * (1.0 + mod_ref[4:5, :]) + mod_ref[3:4, :]
    hb = h.astype(MM_DTYPE)
    acc = jnp.zeros(x.shape, F32)
    for c in range(D_FF // chunk):
        fg = jnp.dot(hb, wgu[:, c * chunk:(c + 1) * chunk], preferred_element_type=F32)
        fu = jnp.dot(hb, wgu[:, D_FF + c * chunk:D_FF + (c + 1) * chunk],
                     preferred_element_type=F32)
        act = (fg * jax.nn.sigmoid(fg) * fu).astype(MM_DTYPE)
        acc = acc + jnp.dot(act, wd[c * chunk:(c + 1) * chunk, :], preferred_element_type=F32)
    y = x + mod_ref[5:6, :] * acc
    if final_norm:
        y = _rms(y) * gf_ref[...]
    out_ref[...] = y


def _ffn(x2d, mod_l, g2, gf, wgu, wd, seq, final_norm):
    n, d = x2d.shape
    tm = ROW_TILE
    per_b = seq // tm
    row = pl.BlockSpec((tm, d), lambda i: (i, 0))
    return pl.pallas_call(
        functools.partial(_ffn_kernel, chunk=256, final_norm=final_norm),
        out_shape=jax.ShapeDtypeStruct((n, d), F32), grid=(n // tm,),
        in_specs=[row, pl.BlockSpec((None, 6, d), lambda i: (i // per_b, 0, 0)),
                  _const_spec((1, d)), _const_spec((1, d)),
                  _const_spec(wgu.shape), _const_spec(wd.shape)],
        out_specs=row,
        compiler_params=_cparams(1), name="swiglu_ffn",
    )(x2d, mod_l, g2, gf, wgu, wd)


def kernel(x, c, w_ada, b_ada, g_norm1, w_in, w_uk, w_uv, g_kv, lam_q1, lam_k1, lam_q2, lam_k2,
           g_subln, w_branch_a, w_branch_b, w_branch_c, w_out, g_norm2, w_gate_up, w_down,
           rel_bias, g_final):
    bsz, seq, d = x.shape
    depth = w_ada.shape[0]
    T = ATT_TILE
    nk = seq // T
    assert d == D_MODEL and seq % T == 0 and seq % ROW_TILE == 0
    n = bsz * seq
    cast = lambda w: w.astype(MM_DTYPE)

    thresholds = _bucket_thresholds(seq + 2 * C_BAND * C_GROUPS[-1][1])
    assert seq <= N_OFFSETS * T or (N_OFFSETS - 2) * T + 1 >= thresholds[-1]
    tab = rel_bias.reshape(-1)
    bias_a = _bias_att_tiles(tab, 0, A_HEADS, T, thresholds)
    bias_b = _bias_att_tiles(tab, A_HEADS, B_HEADS, T, thresholds)
    bias_c = _bias_dil_tiles(tab, A_HEADS + B_HEADS, thresholds)

    mod = _modulation(c, w_ada, b_ada).reshape(depth, bsz, 6, d)

    splits = (A_HEADS * HEAD_DIM, A_LATENT, IDX_HEADS * IDX_DIM, IDX_DIM, IDX_HEADS,
              B_HEADS * 2 * HEAD_DIM, B_HEADS * 2 * HEAD_DIM, B_HEADS * 2 * HEAD_DIM,
              C_HEADS * HEAD_DIM, C_HEADS * HEAD_DIM, C_HEADS * HEAD_DIM, d, d, d)
    offs = np.concatenate([[0], np.cumsum(splits)])
    seg = lambda w, k: w[:, int(offs[k]):int(offs[k + 1])]

    x2d = x.reshape(n, d)
    for l in range(depth):
        wl = w_in[l]
        w_iw = jnp.pad(seg(wl, 4), ((0, 0), (0, 128 - IDX_HEADS)))
        ws_in = [cast(seg(wl, 0)), cast(seg(wl, 2)), cast(seg(wl, 3)), cast(w_iw), cast(seg(wl, 1)),
                 cast(seg(wl, 5)), cast(seg(wl, 6)), cast(seg(wl, 7)),
                 cast(seg(wl, 8)), cast(seg(wl, 9)), cast(seg(wl, 10))]
        g1 = g_norm1[l].reshape(1, d)
        (aq, iq, ik, iw, kv, bq, bk, bv, cq, ck, cv) = _in_proj(
            x2d, mod[l], g1, g_kv[l].reshape(1, A_LATENT), ws_in, seq)

        tq = lambda t: t.reshape(bsz, seq, -1).transpose(0, 2, 1)
        iwT = tq(iw[:, :IDX_HEADS])
        kvT = kv.reshape(bsz, nk, T, A_LATENT).transpose(0, 1, 3, 2)
        o_aT = _dsa(tq(aq), tq(iq), iwT,
                    ik.reshape(bsz, nk, T, IDX_DIM), kv.reshape(bsz, nk, T, A_LATENT), kvT,
                    bias_a, cast(w_uk[l].transpose(0, 2, 1)), cast(w_uv[l].transpose(0, 2, 1)))
        o_a = o_aT.transpose(0, 2, 1).reshape(n, -1)

        lam_init = 0.8 - 0.6 * math.exp(-0.3 * l)
        lam_rows = jnp.stack([lam_q1[l], lam_k1[l], lam_q2[l], lam_k2[l]])
        dv = 2 * HEAD_DIM
        bvT = bv.reshape(bsz, nk, T, B_HEADS, dv).transpose(0, 3, 1, 4, 2)
        o_bT = _diff(tq(bq), bk.reshape(bsz, nk, T, B_HEADS * dv), bvT, bias_b, lam_rows,
                     g_subln[l].reshape(dv, 1), lam_init)
        o_b = o_bT.transpose(0, 2, 1).reshape(n, -1)

        cq3, ck3, cv3 = (t.reshape(bsz, seq, -1) for t in (cq, ck, cv))
        ocs, lses = [], []
        for g, (window, dil) in enumerate(C_GROUPS):
            assert window // dil == C_BAND
            o, s = _dilated_group(cq3, ck3, cv3, bias_c[g * C_HPG:(g + 1) * C_HPG], g, dil)
            ocs.append(o.reshape(n, -1))
            lses.append(s.reshape(n, -1))

        ws_merge = [cast(seg(wl, 11)), cast(seg(wl, 12)), cast(seg(wl, 13)),
                    cast(w_branch_a[l]), cast(w_branch_b[l]), cast(w_branch_c[l]), cast(w_out[l])]
        x2d = _merge(x2d, mod[l], g1, o_a, o_b, ocs, lses, ws_merge, seq)
        x2d = _ffn(x2d, mod[l], g_norm2[l].reshape(1, d), g_final.reshape(1, d),
                   cast(w_gate_up[l]), cast(w_down[l]), seq, final_norm=(l == depth - 1))
    return x2d.reshape(bsz, seq, d)
```

```python
import functools
import math

import numpy as np
import jax
import jax.numpy as jnp
from jax import lax
from jax.experimental import pallas as pl
from jax.experimental.pallas import tpu as pltpu

D_MODEL = 1024
HEAD_DIM = 64
ATTN_SCALE = HEAD_DIM ** -0.5
A_HEADS = 8
A_LATENT = 128
IDX_HEADS = 8
IDX_DIM = 64
IDX_SCALE = (IDX_HEADS * IDX_DIM) ** -0.5
TOPK_MAX = 256
B_HEADS = 4
C_GROUPS = ((128, 1), (512, 4), (2048, 16))
C_HPG = 4
C_HEADS = C_HPG * len(C_GROUPS)
N_BUCKETS = 32
MAX_DISTANCE = 2048
N_BIAS_HEADS = A_HEADS + B_HEADS + C_HEADS
D_FF = -(-8 * D_MODEL // (3 * 256)) * 256
EPS = 1e-6

MM_DTYPE = jnp.bfloat16
F32 = jnp.float32
I32 = jnp.int32

ATT_TILE = 256
N_OFFSETS = 8
C_BAND = 128
ROW_TILE = 512
NEG = -1e30
INT_MIN = -2 ** 31
VMEM_LIMIT = 56 * 1024 * 1024


def _cparams(n_axes, vmem=VMEM_LIMIT):
    return pltpu.CompilerParams(dimension_semantics=("arbitrary",) * n_axes,
                                vmem_limit_bytes=vmem)


def _const_spec(shape):
    nd = len(shape)
    return pl.BlockSpec(shape, lambda *_: (0,) * nd, pipeline_mode=pl.Buffered(1))


def _bucket_thresholds(max_dist):
    n = np.arange(max_dist + 1)
    max_exact = N_BUCKETS // 2
    nf = np.maximum(n, 1).astype(np.float32)
    large = max_exact + (np.log(nf / np.float32(max_exact))
                         / np.float32(math.log(MAX_DISTANCE / max_exact))
                         * np.float32(N_BUCKETS - max_exact)).astype(np.int32)
    large = np.minimum(large, N_BUCKETS - 1)
    bucket = np.where(n < max_exact, n, large)
    assert np.all(np.diff(bucket) >= 0)
    thr = []
    for k in range(1, N_BUCKETS):
        idx = np.nonzero(bucket >= k)[0]
        thr.append(int(idx[0]) if idx.size else None)
    return thr


def _bias_from_dist(dist, tab_ref, col, thresholds):
    b = jnp.full(dist.shape, tab_ref[col], F32)
    for k, thr in enumerate(thresholds, start=1):
        if thr is None:
            break
        b = jnp.where(dist >= thr, tab_ref[k * N_BIAS_HEADS + col], b)
    return b


def _bias_att_kernel(tab_ref, out_ref, *, head0, tile, thresholds):
    h = pl.program_id(0)
    o = pl.program_id(1)
    row = lax.broadcasted_iota(I32, (tile, tile), 0)
    colq = lax.broadcasted_iota(I32, (tile, tile), 1)
    dist = jnp.maximum(o * tile + colq - row, 0)
    out_ref[...] = _bias_from_dist(dist, tab_ref, head0 + h, thresholds)


def _bias_att_tiles(tab, head0, n_heads, tile, thresholds):
    return pl.pallas_call(
        functools.partial(_bias_att_kernel, head0=head0, tile=tile, thresholds=thresholds),
        out_shape=jax.ShapeDtypeStruct((n_heads, N_OFFSETS, tile, tile), F32),
        grid=(n_heads, N_OFFSETS),
        in_specs=[pl.BlockSpec(memory_space=pltpu.SMEM)],
        out_specs=pl.BlockSpec((None, None, tile, tile), lambda h, o: (h, o, 0, 0)),
        compiler_params=_cparams(2),
        name="bias_att_tiles",
    )(tab)


def _bias_dil_kernel(tab_ref, out_ref, *, head0, thresholds):
    h = pl.program_id(0)
    g = h // C_HPG
    dil = jnp.where(g == 0, C_GROUPS[0][1], jnp.where(g == 1, C_GROUPS[1][1], C_GROUPS[2][1]))
    i = lax.broadcasted_iota(I32, (C_BAND, 2 * C_BAND), 0)
    j = lax.broadcasted_iota(I32, (C_BAND, 2 * C_BAND), 1)
    dist = jnp.maximum((i - j + C_BAND) * dil, 0)
    out_ref[...] = _bias_from_dist(dist, tab_ref, head0 + h, thresholds)


def _bias_dil_tiles(tab, head0, thresholds):
    return pl.pallas_call(
        functools.partial(_bias_dil_kernel, head0=head0, thresholds=thresholds),
        out_shape=jax.ShapeDtypeStruct((C_HEADS, C_BAND, 2 * C_BAND), F32),
        grid=(C_HEADS,),
        in_specs=[pl.BlockSpec(memory_space=pltpu.SMEM)],
        out_specs=pl.BlockSpec((None, C_BAND, 2 * C_BAND), lambda h: (h, 0, 0)),
        compiler_params=_cparams(1),
        name="bias_dil_tiles",
    )(tab)


def _mod_kernel(c_ref, w_ref, b_ref, out_ref):
    c = c_ref[...]
    ca = (c * jax.nn.sigmoid(c)).astype(MM_DTYPE)
    out_ref[...] = jnp.dot(ca, w_ref[...].astype(MM_DTYPE), preferred_element_type=F32) + b_ref[...]


def _modulation(c, w_ada, b_ada):
    depth, d, wid = w_ada.shape
    bsz = c.shape[0]
    tn = 1536
    assert wid % tn == 0
    return pl.pallas_call(
        _mod_kernel,
        out_shape=jax.ShapeDtypeStruct((depth, bsz, wid), F32),
        grid=(depth, wid // tn),
        in_specs=[pl.BlockSpec((bsz, d), lambda l, j: (0, 0)),
                  pl.BlockSpec((None, d, tn), lambda l, j: (l, 0, j)),
                  pl.BlockSpec((None, 1, tn), lambda l, j: (l, 0, j))],
        out_specs=pl.BlockSpec((None, bsz, tn), lambda l, j: (l, 0, j)),
        compiler_params=_cparams(2),
        name="adaln_modulation",
    )(c, w_ada, b_ada.reshape(depth, 1, wid))


def _rms(x):
    return x * lax.rsqrt(jnp.mean(x * x, axis=-1, keepdims=True) + EPS)


def _in_kernel(x_ref, mod_ref, g1_ref, gkv_ref,
               w_aq, w_iq, w_ik, w_iw, w_kv, w_bq, w_bk, w_bv, w_cq, w_ck, w_cv,
               o_aq, o_iq, o_ik, o_iw, o_kv, o_bq, o_bk, o_bv, o_cq, o_ck, o_cv):
    h = _rms(x_ref[...]) * g1_ref[...]
    h = h * (1.0 + mod_ref[1:2, :]) + mod_ref[0:1, :]
    hb = h.astype(MM_DTYPE)

    def mm(w):
        return jnp.dot(hb, w[...], preferred_element_type=F32)

    for w, o in ((w_aq, o_aq), (w_iq, o_iq), (w_ik, o_ik), (w_bq, o_bq), (w_bk, o_bk),
                 (w_bv, o_bv), (w_cq, o_cq), (w_ck, o_ck), (w_cv, o_cv)):
        o[...] = mm(w).astype(o.dtype)
    o_iw[...] = mm(w_iw) * IDX_SCALE
    o_kv[...] = (_rms(mm(w_kv)) * gkv_ref[...]).astype(o_kv.dtype)


def _in_proj(x2d, mod_l, g1, gkv, ws, seq):
    n, d = x2d.shape
    tm = ROW_TILE
    per_b = seq // tm
    widths = [w.shape[1] for w in ws]
    dts = [MM_DTYPE, MM_DTYPE, MM_DTYPE, F32, MM_DTYPE] + [MM_DTYPE] * 6
    in_specs = [pl.BlockSpec((tm, d), lambda i: (i, 0)),
                pl.BlockSpec((None, 6, d), lambda i: (i // per_b, 0, 0)),
                _const_spec((1, d)), _const_spec((1, A_LATENT))]
    in_specs += [_const_spec(w.shape) for w in ws]
    out_specs = [pl.BlockSpec((tm, wd), lambda i: (i, 0)) for wd in widths]
    out_shape = [jax.ShapeDtypeStruct((n, wd), dt) for wd, dt in zip(widths, dts)]
    return pl.pallas_call(
        _in_kernel, out_shape=out_shape, grid=(n // tm,),
        in_specs=in_specs, out_specs=out_specs,
        compiler_params=_cparams(1), name="in_proj",
    )(x2d, mod_l, g1, gkv, *ws)


def _softmax_step(s, m_old, l_old):
    m_new = jnp.maximum(m_old, jnp.max(s, axis=0, keepdims=True))
    p = jnp.exp(s - m_new)
    alpha = jnp.exp(m_old - m_new)
    l_new = alpha * l_old + jnp.sum(p, axis=0, keepdims=True)
    return p.astype(MM_DTYPE), alpha, m_new, l_new


def _dsa_kernel(aqT_ref, iqT_ref, iwT_ref, kidx_ref, kv_ref, kvT_ref, bias_ref, wukT_ref, wuvT_ref,
                out_ref, keys_ref, qlat_ref, acc_ref, s_ref, *, topk, idx_bits):
    T = ATT_TILE
    qi = pl.program_id(1)
    nk = qi + 1

    for h in range(A_HEADS):
        q = jnp.dot(wukT_ref[h], aqT_ref[h * HEAD_DIM:(h + 1) * HEAD_DIM, :],
                    preferred_element_type=F32) * ATTN_SCALE
        qlat_ref[h] = q.astype(qlat_ref.dtype)

    s_loc = lax.broadcasted_iota(I32, (T, T), 0)
    t_loc = lax.broadcasted_iota(I32, (T, T), 1)

    def score_tile(kj, carry):
        kt = kidx_ref[kj]
        acc = jnp.zeros((T, T), F32)
        for h in range(IDX_HEADS):
            s = jnp.dot(kt, iqT_ref[h * IDX_DIM:(h + 1) * IDX_DIM, :], preferred_element_type=F32)
            acc = acc + jnp.maximum(s, 0.0) * iwT_ref[h:h + 1, :]
        bits = lax.bitcast_convert_type(acc, I32)
        key = jnp.where(bits < 0, bits ^ jnp.int32(0x7FFFFFFF), bits)
        visible = (kj * T + s_loc) <= (qi * T + t_loc)
        keys_ref[kj] = jnp.where(visible, key, jnp.int32(INT_MIN))
        return carry

    lax.fori_loop(0, nk, score_tile, 0)

    one = jnp.int32(1)
    nil = jnp.int32(0)

    def count(hit_fn):
        def body(kj, c):
            hit = hit_fn(keys_ref[kj], kj)
            return c + jnp.sum(hit.reshape(T // 8, 8, T), axis=0)
        c = lax.fori_loop(0, nk, body, jnp.zeros((8, T), I32))
        return jnp.sum(c, axis=0, keepdims=True)

    zero = jnp.zeros((1, T), I32)
    kth = jnp.where(count(lambda kk, kj: jnp.where(kk >= zero, one, nil)) >= topk,
                    nil, jnp.int32(INT_MIN))

    def bit_step(i, kth):
        cand = kth | jnp.left_shift(one, jnp.int32(30) - i)
        return jnp.where(count(lambda kk, kj: jnp.where(kk >= cand, one, nil)) >= topk, cand, kth)

    kth = lax.fori_loop(0, 31, bit_step, kth)
    kth = jnp.maximum(kth, jnp.int32(INT_MIN + 1))
    n_ge = count(lambda kk, kj: jnp.where(kk >= kth, one, nil))

    @pl.when(jnp.max(n_ge) > topk)
    def _():
        need = topk - count(lambda kk, kj: jnp.where(kk > kth, one, nil))

        def pos_step(i, cut):
            cand = cut | jnp.left_shift(one, jnp.int32(idx_bits - 1) - i)
            c = count(lambda kk, kj: jnp.where(
                kk == kth, jnp.where((kj * T + s_loc) < cand, one, nil), nil))
            return jnp.where(c < need, cand, cut)

        cut = lax.fori_loop(0, idx_bits, pos_step, zero)

        def demote(kj, carry):
            kk = keys_ref[kj]
            lowered = jnp.where((kj * T + s_loc) > cut, kth - one, kk)
            keys_ref[kj] = jnp.where(kk == kth, lowered, kk)
            return carry

        lax.fori_loop(0, nk, demote, 0)

    acc_ref[...] = jnp.zeros(acc_ref.shape, F32)

    def logits_to_scratch(kj):
        kvt = kv_ref[kj]
        for h in range(A_HEADS):
            s_ref[h] = jnp.dot(kvt, qlat_ref[h], preferred_element_type=F32)

    logits_to_scratch(0)

    def attend(kj, carry):
        ms, ls = carry
        sel = keys_ref[kj] >= kth
        kvTt = kvT_ref[kj]
        kv_next = kv_ref[jnp.minimum(kj + 1, nk - 1)]
        off = jnp.minimum(qi - kj, N_OFFSETS - 1)
        new_m, new_l = [], []
        for h in range(A_HEADS):
            s = jnp.where(sel, s_ref[h] + bias_ref[h, off], NEG)
            p, alpha, m_new, l_new = _softmax_step(s, ms[h], ls[h])
            s_ref[h] = jnp.dot(kv_next, qlat_ref[h], preferred_element_type=F32)
            acc_ref[h] = alpha * acc_ref[h] + jnp.dot(kvTt, p, preferred_element_type=F32)
            new_m.append(m_new)
            new_l.append(l_new)
        return tuple(new_m), tuple(new_l)

    init = (tuple(jnp.full((1, T), NEG, F32) for _ in range(A_HEADS)),
            tuple(jnp.zeros((1, T), F32) for _ in range(A_HEADS)))
    _, ls = lax.fori_loop(0, nk, attend, init)

    for h in range(A_HEADS):
        o_lat = (acc_ref[h] / ls[h]).astype(MM_DTYPE)
        out_ref[h * HEAD_DIM:(h + 1) * HEAD_DIM, :] = jnp.dot(
            wuvT_ref[h], o_lat, preferred_element_type=F32).astype(out_ref.dtype)


def _dsa(aqT, iqT, iwT, kidx, kv, kvT, bias_a, wukT, wuvT):
    bsz, _, seq = aqT.shape
    T = ATT_TILE
    nk = seq // T
    topk = min(TOPK_MAX, seq // 4)
    idx_bits = int(math.log2(seq))
    assert 2 ** idx_bits == seq
    qspec = lambda rows: pl.BlockSpec((None, rows, T), lambda b, i: (b, 0, i))
    kspec = lambda a, c: pl.BlockSpec((None, nk, a, c), lambda b, i: (b, 0, 0, 0))
    return pl.pallas_call(
        functools.partial(_dsa_kernel, topk=topk, idx_bits=idx_bits),
        out_shape=jax.ShapeDtypeStruct((bsz, A_HEADS * HEAD_DIM, seq), MM_DTYPE),
        grid=(bsz, nk),
        in_specs=[qspec(A_HEADS * HEAD_DIM), qspec(IDX_HEADS * IDX_DIM), qspec(IDX_HEADS),
                  kspec(T, IDX_DIM), kspec(T, A_LATENT), kspec(A_LATENT, T),
                  _const_spec(bias_a.shape), _const_spec(wukT.shape), _const_spec(wuvT.shape)],
        out_specs=qspec(A_HEADS * HEAD_DIM),
        scratch_shapes=[pltpu.VMEM((nk, T, T), I32),
                        pltpu.VMEM((A_HEADS, A_LATENT, T), MM_DTYPE),
                        pltpu.VMEM((A_HEADS, A_LATENT, T), F32),
                        pltpu.VMEM((A_HEADS, T, T), F32)],
        compiler_params=_cparams(2), name="dsa_attention",
    )(aqT, iqT, iwT, kidx, kv, kvT, bias_a, wukT, wuvT)


def _diff_kernel(qT_ref, k_ref, vT_ref, bias_ref, lam_ref, gsub_ref, out_ref,
                 qz_ref, acc_ref, s_ref, *, lam_init):
    T = ATT_TILE
    dv = 2 * HEAD_DIM
    n_chain = 2 * B_HEADS
    qi = pl.program_id(1)
    half = lax.broadcasted_iota(I32, (dv, T), 0) < HEAD_DIM
    for h in range(B_HEADS):
        q = qT_ref[h * dv:(h + 1) * dv, :].astype(F32) * ATTN_SCALE
        qz_ref[2 * h] = jnp.where(half, q, 0.0).astype(MM_DTYPE)
        qz_ref[2 * h + 1] = jnp.where(half, 0.0, q).astype(MM_DTYPE)
    s_loc = lax.broadcasted_iota(I32, (T, T), 0)
    t_loc = lax.broadcasted_iota(I32, (T, T), 1)
    acc_ref[...] = jnp.zeros(acc_ref.shape, F32)

    def logits(kj, c):
        h = c // 2
        return jnp.dot(k_ref[kj, :, h * dv:(h + 1) * dv], qz_ref[c], preferred_element_type=F32)

    for c in range(n_chain):
        s_ref[c] = logits(0, c)

    def step(kj, carry, diagonal):
        ms, ls = carry
        off = jnp.minimum(qi - kj, N_OFFSETS - 1)
        new_m, new_l = [], []
        for c in range(n_chain):
            h = c // 2
            s = s_ref[c] + bias_ref[h, off]
            if diagonal:
                s = jnp.where(s_loc <= t_loc, s, NEG)
            p, alpha, m_new, l_new = _softmax_step(s, ms[c], ls[c])
            if not diagonal:
                s_ref[c] = logits(kj + 1, c)
            acc_ref[c] = alpha * acc_ref[c] + jnp.dot(vT_ref[h, kj], p, preferred_element_type=F32)
            new_m.append(m_new)
            new_l.append(l_new)
        return tuple(new_m), tuple(new_l)

    init = (tuple(jnp.full((1, T), NEG, F32) for _ in range(n_chain)),
            tuple(jnp.zeros((1, T), F32) for _ in range(n_chain)))
    carry = lax.fori_loop(0, qi, lambda kj, cr: step(kj, cr, False), init)
    _, ls = step(qi, carry, True)

    lr = lam_ref[...]
    lam = (jnp.exp(jnp.sum(lr[0:1, :] * lr[1:2, :], axis=1, keepdims=True))
           - jnp.exp(jnp.sum(lr[2:3, :] * lr[3:4, :], axis=1, keepdims=True)) + lam_init)
    for h in range(B_HEADS):
        attn = acc_ref[2 * h] / ls[2 * h] - lam * (acc_ref[2 * h + 1] / ls[2 * h + 1])
        y = attn * lax.rsqrt(jnp.mean(attn * attn, axis=0, keepdims=True) + EPS)
        out_ref[h * dv:(h + 1) * dv, :] = (y * gsub_ref[...] * (1.0 - lam_init)).astype(out_ref.dtype)


def _diff(bqT, bk, bvT, bias_b, lam_rows, gsub, lam_init):
    bsz, _, seq = bqT.shape
    T = ATT_TILE
    nk = seq // T
    dv = 2 * HEAD_DIM
    qspec = pl.BlockSpec((None, B_HEADS * dv, T), lambda b, i: (b, 0, i))
    return pl.pallas_call(
        functools.partial(_diff_kernel, lam_init=lam_init),
        out_shape=jax.ShapeDtypeStruct((bsz, B_HEADS * dv, seq), MM_DTYPE),
        grid=(bsz, nk),
        in_specs=[qspec,
                  pl.BlockSpec((None, nk, T, B_HEADS * dv), lambda b, i: (b, 0, 0, 0)),
                  pl.BlockSpec((None, B_HEADS, nk, dv, T), lambda b, i: (b, 0, 0, 0, 0)),
                  _const_spec(bias_b.shape), _const_spec((4, HEAD_DIM)), _const_spec((dv, 1))],
        out_specs=qspec,
        scratch_shapes=[pltpu.VMEM((2 * B_HEADS, dv, T), MM_DTYPE),
                        pltpu.VMEM((2 * B_HEADS, dv, T), F32),
                        pltpu.VMEM((2 * B_HEADS, T, T), F32)],
        compiler_params=_cparams(2), name="diff_attention",
    )(bqT, bk, bvT, bias_b, lam_rows, gsub)


def _dil_kernel(q_ref, k_ref, v_ref, kh_ref, vh_ref, bias_ref, out_ref, lse_ref, *, tq):
    n = C_BAND
    wid = C_HPG * HEAD_DIM
    halo_lo = jnp.where(pl.program_id(2) == 0, jnp.int32(n), jnp.int32(0))
    i = lax.broadcasted_iota(I32, (n, 2 * n), 0)
    j = lax.broadcasted_iota(I32, (n, 2 * n), 1)
    lane_head = lax.broadcasted_iota(I32, (n, wid), 1) // HEAD_DIM
    for c in range(tq // n):
        q = q_ref[c * n:(c + 1) * n, :].astype(F32) * ATTN_SCALE
        if c == 0:
            keys = jnp.concatenate([kh_ref[...], k_ref[0:n, :]], axis=0)
            vals = jnp.concatenate([vh_ref[...], v_ref[0:n, :]], axis=0)
            lo = jnp.maximum(i, halo_lo)
        else:
            keys = k_ref[(c - 1) * n:(c + 1) * n, :]
            vals = v_ref[(c - 1) * n:(c + 1) * n, :]
            lo = i
        out = jnp.zeros((n, wid), F32)
        lse = jnp.zeros((n, wid), F32)
        for h in range(C_HPG):
            in_head = lane_head == h
            qh = jnp.where(in_head, q, 0.0).astype(MM_DTYPE)
            s = lax.dot_general(qh, keys, (((1,), (1,)), ((), ())),
                                preferred_element_type=F32) + bias_ref[h]
            s = jnp.where(j >= lo, jnp.where(j <= i + n, s, NEG), NEG)
            m = jnp.max(s, axis=1, keepdims=True)
            p = jnp.exp(s - m)
            den = jnp.sum(p, axis=1, keepdims=True)
            o = jnp.dot(p.astype(MM_DTYPE), vals, preferred_element_type=F32) / den
            out = jnp.where(in_head, o, out)
            lse = jnp.where(in_head, m + jnp.log(den), lse)
        out_ref[c * n:(c + 1) * n, :] = out
        lse_ref[c * n:(c + 1) * n, :] = lse


def _dilated_group(cq, ck, cv, bias_g, g, dil):
    bsz, seq, cw = cq.shape
    wid = C_HPG * HEAD_DIM
    n = C_BAND
    m = seq // dil
    assert m % n == 0
    tq = min(m, 512)
    ncol = cw // wid
    view = lambda t: t.reshape(bsz, m, dil * cw)
    cur = pl.BlockSpec((None, tq, wid), lambda b, r, i: (b, i, r * ncol + g))
    halo = pl.BlockSpec((None, n, wid),
                        lambda b, r, i: (b, jnp.maximum(i * (tq // n) - 1, 0), r * ncol + g))
    outspec = pl.BlockSpec((None, tq, wid), lambda b, r, i: (b, i, r))
    out, lse = pl.pallas_call(
        functools.partial(_dil_kernel, tq=tq),
        out_shape=[jax.ShapeDtypeStruct((bsz, m, dil * wid), F32)] * 2,
        grid=(bsz, dil, m // tq),
        in_specs=[cur, cur, cur, halo, halo,
                  pl.BlockSpec((C_HPG, n, 2 * n), lambda b, r, i: (0, 0, 0))],
        out_specs=[outspec, outspec],
        compiler_params=_cparams(3), name=f"dilated_group{g}",
    )(view(cq), view(ck), view(cv), view(ck), view(cv), bias_g)
    return out.reshape(bsz, seq, wid), lse.reshape(bsz, seq, wid)


def _merge_kernel(x_ref, mod_ref, g1_ref, oa_ref, ob_ref,
                  c0_ref, c1_ref, c2_ref, s0_ref, s1_ref, s2_ref,
                  wza, wzb, wzc, wba, wbb, wbc, wo, out_ref):
    x = x_ref[...]
    h = _rms(x) * g1_ref[...]
    h = h * (1.0 + mod_ref[1:2, :]) + mod_ref[0:1, :]
    hb = h.astype(MM_DTYPE)

    s0, s1, s2 = s0_ref[...], s1_ref[...], s2_ref[...]
    mx = jnp.maximum(jnp.maximum(s0, s1), s2)
    e0, e1, e2 = jnp.exp(s0 - mx), jnp.exp(s1 - mx), jnp.exp(s2 - mx)
    oc = (e0 * c0_ref[...] + e1 * c1_ref[...] + e2 * c2_ref[...]) / (e0 + e1 + e2)

    def gated(wz, o, wb):
        z = jnp.dot(hb, wz[...], preferred_element_type=F32)
        return jax.nn.sigmoid(z) * jnp.dot(o, wb[...], preferred_element_type=F32)

    merged = (gated(wza, oa_ref[...], wba) + gated(wzb, ob_ref[...], wbb)
              + gated(wzc, oc.astype(MM_DTYPE), wbc))
    y = jnp.dot(merged.astype(MM_DTYPE), wo[...], preferred_element_type=F32)
    out_ref[...] = x + mod_ref[2:3, :] * y


def _merge(x2d, mod_l, g1, oa, ob, ocs, lses, ws, seq):
    n, d = x2d.shape
    tm = ROW_TILE
    per_b = seq // tm
    row = lambda wd: pl.BlockSpec((tm, wd), lambda i: (i, 0))
    wid = C_HPG * HEAD_DIM
    in_specs = [row(d), pl.BlockSpec((None, 6, d), lambda i: (i // per_b, 0, 0)), _const_spec((1, d)),
                row(oa.shape[1]), row(ob.shape[1])] + [row(wid)] * 6
    in_specs += [_const_spec(w.shape) for w in ws]
    return pl.pallas_call(
        _merge_kernel, out_shape=jax.ShapeDtypeStruct((n, d), F32), grid=(n // tm,),
        in_specs=in_specs, out_specs=row(d),
        compiler_params=_cparams(1), name="gated_merge",
    )(x2d, mod_l, g1, oa, ob, *ocs, *lses, *ws)


def _ffn_kernel(x_ref, mod_ref, g2_ref, gf_ref, wgu, wd, out_ref, *, chunk, final_norm):
    x = x_ref[...]
    h = _rms(x) * g2_ref[...]
    h = h * (1.0 + mod_ref[4:5, :]) + mod_ref[3:4, :]
    hb = h.astype(MM_DTYPE)
    acc = jnp.zeros(x.shape, F32)
    for c in range(D_FF // chunk):
        fg = jnp.dot(hb, wgu[:, c * chunk:(c + 1) * chunk], preferred_element_type=F32)
        fu = jnp.dot(hb, wgu[:, D_FF + c * chunk:D_FF + (c + 1) * chunk],
                     preferred_element_type=F32)
        act = (fg * jax.nn.sigmoid(fg) * fu).astype(MM_DTYPE)
        acc = acc + jnp.dot(act, wd[c * chunk:(c + 1) * chunk, :], preferred_element_type=F32)
    y = x + mod_ref[5:6, :] * acc
    if final_norm:
        y = _rms(y) * gf_ref[...]
    out_ref[...] = y


def _ffn(x2d, mod_l, g2, gf, wgu, wd, seq, final_norm):
    n, d = x2d.shape
    tm = ROW_TILE
    per_b = seq // tm
    row = pl.BlockSpec((tm, d), lambda i: (i, 0))
    return pl.pallas_call(
        functools.partial(_ffn_kernel, chunk=256, final_norm=final_norm),
        out_shape=jax.ShapeDtypeStruct((n, d), F32), grid=(n // tm,),
        in_specs=[row, pl.BlockSpec((None, 6, d), lambda i: (i // per_b, 0, 0)),
                  _const_spec((1, d)), _const_spec((1, d)),
                  _const_spec(wgu.shape), _const_spec(wd.shape)],
        out_specs=row,
        compiler_params=_cparams(1), name="swiglu_ffn",
    )(x2d, mod_l, g2, gf, wgu, wd)


def kernel(x, c, w_ada, b_ada, g_norm1, w_in, w_uk, w_uv, g_kv, lam_q1, lam_k1, lam_q2, lam_k2,
           g_subln, w_branch_a, w_branch_b, w_branch_c, w_out, g_norm2, w_gate_up, w_down,
           rel_bias, g_final):
    bsz, seq, d = x.shape
    depth = w_ada.shape[0]
    T = ATT_TILE
    nk = seq // T
    assert d == D_MODEL and seq % T == 0 and seq % ROW_TILE == 0
    n = bsz * seq
    cast = lambda w: w.astype(MM_DTYPE)

    thresholds = _bucket_thresholds(seq + 2 * C_BAND * C_GROUPS[-1][1])
    assert seq <= N_OFFSETS * T or (N_OFFSETS - 2) * T + 1 >= thresholds[-1]
    tab = rel_bias.reshape(-1)
    bias_a = _bias_att_tiles(tab, 0, A_HEADS, T, thresholds)
    bias_b = _bias_att_tiles(tab, A_HEADS, B_HEADS, T, thresholds)
    bias_c = _bias_dil_tiles(tab, A_HEADS + B_HEADS, thresholds)

    mod = _modulation(c, w_ada, b_ada).reshape(depth, bsz, 6, d)

    splits = (A_HEADS * HEAD_DIM, A_LATENT, IDX_HEADS * IDX_DIM, IDX_DIM, IDX_HEADS,
              B_HEADS * 2 * HEAD_DIM, B_HEADS * 2 * HEAD_DIM, B_HEADS * 2 * HEAD_DIM,
              C_HEADS * HEAD_DIM, C_HEADS * HEAD_DIM, C_HEADS * HEAD_DIM, d, d, d)
    offs = np.concatenate([[0], np.cumsum(splits)])
    seg = lambda w, k: w[:, int(offs[k]):int(offs[k + 1])]

    x2d = x.reshape(n, d)
    for l in range(depth):
        wl = w_in[l]
        w_iw = jnp.pad(seg(wl, 4), ((0, 0), (0, 128 - IDX_HEADS)))
        ws_in = [cast(seg(wl, 0)), cast(seg(wl, 2)), cast(seg(wl, 3)), cast(w_iw), cast(seg(wl, 1)),
                 cast(seg(wl, 5)), cast(seg(wl, 6)), cast(seg(wl, 7)),
                 cast(seg(wl, 8)), cast(seg(wl, 9)), cast(seg(wl, 10))]
        g1 = g_norm1[l].reshape(1, d)
        (aq, iq, ik, iw, kv, bq, bk, bv, cq, ck, cv) = _in_proj(
            x2d, mod[l], g1, g_kv[l].reshape(1, A_LATENT), ws_in, seq)

        tq = lambda t: t.reshape(bsz, seq, -1).transpose(0, 2, 1)
        iwT = tq(iw[:, :IDX_HEADS])
        kvT = kv.reshape(bsz, nk, T, A_LATENT).transpose(0, 1, 3, 2)
        o_aT = _dsa(tq(aq), tq(iq), iwT,
                    ik.reshape(bsz, nk, T, IDX_DIM), kv.reshape(bsz, nk, T, A_LATENT), kvT,
                    bias_a, cast(w_uk[l].transpose(0, 2, 1)), cast(w_uv[l].transpose(0, 2, 1)))
        o_a = o_aT.transpose(0, 2, 1).reshape(n, -1)

        lam_init = 0.8 - 0.6 * math.exp(-0.3 * l)
        lam_rows = jnp.stack([lam_q1[l], lam_k1[l], lam_q2[l], lam_k2[l]])
        dv = 2 * HEAD_DIM
        bvT = bv.reshape(bsz, nk, T, B_HEADS, dv).transpose(0, 3, 1, 4, 2)
        o_bT = _diff(tq(bq), bk.reshape(bsz, nk, T, B_HEADS * dv), bvT, bias_b, lam_rows,
                     g_subln[l].reshape(dv, 1), lam_init)
        o_b = o_bT.transpose(0, 2, 1).reshape(n, -1)

        cq3, ck3, cv3 = (t.reshape(bsz, seq, -1) for t in (cq, ck, cv))
        ocs, lses = [], []
        for g, (window, dil) in enumerate(C_GROUPS):
            assert window // dil == C_BAND
            o, s = _dilated_group(cq3, ck3, cv3, bias_c[g * C_HPG:(g + 1) * C_HPG], g, dil)
            ocs.append(o.reshape(n, -1))
            lses.append(s.reshape(n, -1))

        ws_merge = [cast(seg(wl, 11)), cast(seg(wl, 12)), cast(seg(wl, 13)),
                    cast(w_branch_a[l]), cast(w_branch_b[l]), cast(w_branch_c[l]), cast(w_out[l])]
        x2d = _merge(x2d, mod[l], g1, o_a, o_b, ocs, lses, ws_merge, seq)
        x2d = _ffn(x2d, mod[l], g_norm2[l].reshape(1, d), g_final.reshape(1, d),
                   cast(w_gate_up[l]), cast(w_down[l]), seq, final_norm=(l == depth - 1))
    return x2d.reshape(bsz, seq, d)
```

```python
import functools
import math

import numpy as np
import jax
import jax.numpy as jnp
from jax import lax
from jax.experimental import pallas as pl
from jax.experimental.pallas import tpu as pltpu

D_MODEL = 1024
HEAD_DIM = 64
ATTN_SCALE = HEAD_DIM ** -0.5
A_HEADS = 8
A_LATENT = 128
IDX_HEADS = 8
IDX_DIM = 64
IDX_SCALE = (IDX_HEADS * IDX_DIM) ** -0.5
TOPK_MAX = 256
B_HEADS = 4
C_GROUPS = ((128, 1), (512, 4), (2048, 16))
C_HPG = 4
C_HEADS = C_HPG * len(C_GROUPS)
N_BUCKETS = 32
MAX_DISTANCE = 2048
N_BIAS_HEADS = A_HEADS + B_HEADS + C_HEADS
D_FF = -(-8 * D_MODEL // (3 * 256)) * 256
EPS = 1e-6

MM_DTYPE = jnp.bfloat16
F32 = jnp.float32
I32 = jnp.int32

ATT_TILE = 256
N_OFFSETS = 8
C_BAND = 128
ROW_TILE = 512
NEG = -1e30
INT_MIN = -2 ** 31
VMEM_LIMIT = 56 * 1024 * 1024


def _cparams(n_axes, vmem=VMEM_LIMIT):
    return pltpu.CompilerParams(dimension_semantics=("arbitrary",) * n_axes,
                                vmem_limit_bytes=vmem)


def _const_spec(shape):
    nd = len(shape)
    return pl.BlockSpec(shape, lambda *_: (0,) * nd, pipeline_mode=pl.Buffered(1))


def _bucket_thresholds(max_dist):
    n = np.arange(max_dist + 1)
    max_exact = N_BUCKETS // 2
    nf = np.maximum(n, 1).astype(np.float32)
    large = max_exact + (np.log(nf / np.float32(max_exact))
                         / np.float32(math.log(MAX_DISTANCE / max_exact))
                         * np.float32(N_BUCKETS - max_exact)).astype(np.int32)
    large = np.minimum(large, N_BUCKETS - 1)
    bucket = np.where(n < max_exact, n, large)
    assert np.all(np.diff(bucket) >= 0)
    thr = []
    for k in range(1, N_BUCKETS):
        idx = np.nonzero(bucket >= k)[0]
        thr.append(int(idx[0]) if idx.size else None)
    return thr


def _bias_from_dist(dist, tab_ref, col, thresholds):
    b = jnp.full(dist.shape, tab_ref[col], F32)
    for k, thr in enumerate(thresholds, start=1):
        if thr is None:
            break
        b = jnp.where(dist >= thr, tab_ref[k * N_BIAS_HEADS + col], b)
    return b


def _bias_att_kernel(tab_ref, out_ref, *, head0, tile, thresholds):
    h = pl.program_id(0)
    o = pl.program_id(1)
    row = lax.broadcasted_iota(I32, (tile, tile), 0)
    colq = lax.broadcasted_iota(I32, (tile, tile), 1)
    dist = jnp.maximum(o * tile + colq - row, 0)
    out_ref[...] = _bias_from_dist(dist, tab_ref, head0 + h, thresholds)


def _bias_att_tiles(tab, head0, n_heads, tile, thresholds):
    return pl.pallas_call(
        functools.partial(_bias_att_kernel, head0=head0, tile=tile, thresholds=thresholds),
        out_shape=jax.ShapeDtypeStruct((n_heads, N_OFFSETS, tile, tile), F32),
        grid=(n_heads, N_OFFSETS),
        in_specs=[pl.BlockSpec(memory_space=pltpu.SMEM)],
        out_specs=pl.BlockSpec((None, None, tile, tile), lambda h, o: (h, o, 0, 0)),
        compiler_params=_cparams(2),
        name="bias_att_tiles",
    )(tab)


def _bias_dil_kernel(tab_ref, out_ref, *, head0, thresholds):
    h = pl.program_id(0)
    g = h // C_HPG
    dil = jnp.where(g == 0, C_GROUPS[0][1], jnp.where(g == 1, C_GROUPS[1][1], C_GROUPS[2][1]))
    i = lax.broadcasted_iota(I32, (C_BAND, 2 * C_BAND), 0)
    j = lax.broadcasted_iota(I32, (C_BAND, 2 * C_BAND), 1)
    dist = jnp.maximum((i - j + C_BAND) * dil, 0)
    out_ref[...] = _bias_from_dist(dist, tab_ref, head0 + h, thresholds)


def _bias_dil_tiles(tab, head0, thresholds):
    return pl.pallas_call(
        functools.partial(_bias_dil_kernel, head0=head0, thresholds=thresholds),
        out_shape=jax.ShapeDtypeStruct((C_HEADS, C_BAND, 2 * C_BAND), F32),
        grid=(C_HEADS,),
        in_specs=[pl.BlockSpec(memory_space=pltpu.SMEM)],
        out_specs=pl.BlockSpec((None, C_BAND, 2 * C_BAND), lambda h: (h, 0, 0)),
        compiler_params=_cparams(1),
        name="bias_dil_tiles",
    )(tab)


def _mod_kernel(c_ref, w_ref, b_ref, out_ref):
    c = c_ref[...]
    ca = (c * jax.nn.sigmoid(c)).astype(MM_DTYPE)
    out_ref[...] = jnp.dot(ca, w_ref[...].astype(MM_DTYPE), preferred_element_type=F32) + b_ref[...]


def _modulation(c, w_ada, b_ada):
    depth, d, wid = w_ada.shape
    bsz = c.shape[0]
    tn = 1536
    assert wid % tn == 0
    return pl.pallas_call(
        _mod_kernel,
        out_shape=jax.ShapeDtypeStruct((depth, bsz, wid), F32),
        grid=(depth, wid // tn),
        in_specs=[pl.BlockSpec((bsz, d), lambda l, j: (0, 0)),
                  pl.BlockSpec((None, d, tn), lambda l, j: (l, 0, j)),
                  pl.BlockSpec((None, 1, tn), lambda l, j: (l, 0, j))],
        out_specs=pl.BlockSpec((None, bsz, tn), lambda l, j: (l, 0, j)),
        compiler_params=_cparams(2),
        name="adaln_modulation",
    )(c, w_ada, b_ada.reshape(depth, 1, wid))


def _rms(x):
    return x * lax.rsqrt(jnp.mean(x * x, axis=-1, keepdims=True) + EPS)


def _in_kernel(x_ref, mod_ref, g1_ref, gkv_ref,
               wT_aq, wT_iq, wT_iw, wT_bq, wT_bv, w_ik, w_kv, w_bk, w_c,
               o_aqT, o_iqT, o_iwT, o_bqT, o_bvT, o_ik, o_kv, o_kvT, o_bk, o_c0, o_c1, o_c2,
               c_scr):
    T = ATT_TILE
    tm = x_ref.shape[0]
    h = _rms(x_ref[...]) * g1_ref[...]
    h = h * (1.0 + mod_ref[1:2, :]) + mod_ref[0:1, :]
    hb = h.astype(MM_DTYPE)

    def mm(w):
        return jnp.dot(hb, w[...], preferred_element_type=F32)

    def mm_t(wT):
        return lax.dot_general(wT[...], hb, (((1,), (1,)), ((), ())), preferred_element_type=F32)

    o_aqT[...] = mm_t(wT_aq).astype(o_aqT.dtype)
    o_iqT[...] = mm_t(wT_iq).astype(o_iqT.dtype)
    o_bqT[...] = mm_t(wT_bq).astype(o_bqT.dtype)
    o_iwT[...] = (mm_t(wT_iw) * IDX_SCALE)[:IDX_HEADS]
    bvT = mm_t(wT_bv).astype(o_bvT.dtype)
    kv = _rms(mm(w_kv)) * gkv_ref[...]
    kvT = kv.T.astype(o_kvT.dtype)
    for j in range(tm // T):
        o_bvT[j] = bvT[:, j * T:(j + 1) * T]
        o_kvT[j] = kvT[:, j * T:(j + 1) * T]
    o_kv[...] = kv.astype(o_kv.dtype)
    o_ik[...] = mm(w_ik).astype(o_ik.dtype)
    o_bk[...] = mm(w_bk).astype(o_bk.dtype)

    yc = mm(w_c)
    n_chunk = yc.shape[1] // 128
    for j in range(n_chunk):
        c_scr[j] = yc[:, j * 128:(j + 1) * 128]
    per_group = n_chunk // len(C_GROUPS)
    for g, o_c in enumerate((o_c0, o_c1, o_c2)):
        dil = C_GROUPS[g][1]
        for r in range(dil):
            for jj in range(per_group):
                o_c[r, :, jj * 128:(jj + 1) * 128] = c_scr[
                    g * per_group + jj, pl.ds(r, tm // dil, stride=dil), :].astype(o_c.dtype)


def _in_proj(x2d, mod_l, g1, gkv, ws, bsz, seq):
    n, d = x2d.shape
    tm = ROW_TILE
    T = ATT_TILE
    per_b = seq // tm
    nk = seq // T
    hd = A_HEADS * HEAD_DIM
    bw = B_HEADS * 2 * HEAD_DIM
    gw = 3 * C_HPG * HEAD_DIM
    in_specs = [pl.BlockSpec((tm, d), lambda i: (i, 0)),
                pl.BlockSpec((None, 6, d), lambda i: (i // per_b, 0, 0)),
                _const_spec((1, d)), _const_spec((1, A_LATENT))]
    in_specs += [_const_spec(w.shape) for w in ws]

    def tspec(rows):
        return pl.BlockSpec((None, rows, tm), lambda i: (i // per_b, 0, i % per_b))

    def tile_tspec(rows):
        return pl.BlockSpec((None, tm // T, rows, T), lambda i: (i // per_b, i % per_b, 0, 0))

    def rspec(wd):
        return pl.BlockSpec((tm, wd), lambda i: (i, 0))

    def cspec(dil):
        return pl.BlockSpec((None, dil, tm // dil, gw), lambda i: (i // per_b, 0, i % per_b, 0))

    sds = jax.ShapeDtypeStruct
    out_specs = [tspec(hd), tspec(IDX_HEADS * IDX_DIM), tspec(IDX_HEADS), tspec(bw),
                 tile_tspec(bw), rspec(IDX_DIM), rspec(A_LATENT), tile_tspec(A_LATENT), rspec(bw)]
    out_shape = [sds((bsz, hd, seq), MM_DTYPE), sds((bsz, IDX_HEADS * IDX_DIM, seq), MM_DTYPE),
                 sds((bsz, IDX_HEADS, seq), F32), sds((bsz, bw, seq), MM_DTYPE),
                 sds((bsz, nk, bw, T), MM_DTYPE), sds((n, IDX_DIM), MM_DTYPE),
                 sds((n, A_LATENT), MM_DTYPE), sds((bsz, nk, A_LATENT, T), MM_DTYPE),
                 sds((n, bw), MM_DTYPE)]
    for _, dil in C_GROUPS:
        out_specs.append(cspec(dil))
        out_shape.append(sds((bsz, dil, seq // dil, gw), MM_DTYPE))
    return pl.pallas_call(
        _in_kernel, out_shape=out_shape, grid=(n // tm,),
        in_specs=in_specs, out_specs=out_specs,
        scratch_shapes=[pltpu.VMEM((3 * gw // 128, tm, 128), F32)],
        compiler_params=_cparams(1), name="in_proj",
    )(x2d, mod_l, g1, gkv, *ws)


def _softmax_step(s, m_old, l_old):
    m_new = jnp.maximum(m_old, jnp.max(s, axis=0, keepdims=True))
    p = jnp.exp(s - m_new)
    alpha = jnp.exp(m_old - m_new)
    l_new = alpha * l_old + jnp.sum(p, axis=0, keepdims=True)
    return p.astype(MM_DTYPE), alpha, m_new, l_new


def _dsa_kernel(aqT_ref, iqT_ref, iwT_ref, kidx_ref, kv_ref, kvT_ref, bias_ref, wukT_ref, wuvT_ref,
                out_ref, keys_ref, qlat_ref, acc_ref, s_ref, oT_ref, *, topk, idx_bits):
    T = ATT_TILE
    qi = pl.program_id(1)
    nk = qi + 1

    for h in range(A_HEADS):
        q = jnp.dot(wukT_ref[h], aqT_ref[h * HEAD_DIM:(h + 1) * HEAD_DIM, :],
                    preferred_element_type=F32) * ATTN_SCALE
        qlat_ref[h] = q.astype(qlat_ref.dtype)

    s_loc = lax.broadcasted_iota(I32, (T, T), 0)
    t_loc = lax.broadcasted_iota(I32, (T, T), 1)

    def score_tile(kj, carry):
        kt = kidx_ref[kj]
        acc = jnp.zeros((T, T), F32)
        for h in range(IDX_HEADS):
            s = jnp.dot(kt, iqT_ref[h * IDX_DIM:(h + 1) * IDX_DIM, :], preferred_element_type=F32)
            acc = acc + jnp.maximum(s, 0.0) * iwT_ref[h:h + 1, :]
        bits = lax.bitcast_convert_type(acc, I32)
        key = jnp.where(bits < 0, bits ^ jnp.int32(0x7FFFFFFF), bits)
        visible = (kj * T + s_loc) <= (qi * T + t_loc)
        keys_ref[kj] = jnp.where(visible, key, jnp.int32(INT_MIN))
        return carry

    lax.fori_loop(0, nk, score_tile, 0)

    one = jnp.int32(1)
    nil = jnp.int32(0)

    def count(hit_fn):
        def body(kj, c):
            hit = hit_fn(keys_ref[kj], kj)
            return c + jnp.sum(hit.reshape(T // 8, 8, T), axis=0)
        c = lax.fori_loop(0, nk, body, jnp.zeros((8, T), I32))
        return jnp.sum(c, axis=0, keepdims=True)

    zero = jnp.zeros((1, T), I32)
    kth = jnp.where(count(lambda kk, kj: jnp.where(kk >= zero, one, nil)) >= topk,
                    nil, jnp.int32(INT_MIN))

    def bit_step(i, kth):
        cand = kth | jnp.left_shift(one, jnp.int32(30) - i)
        return jnp.where(count(lambda kk, kj: jnp.where(kk >= cand, one, nil)) >= topk, cand, kth)

    kth = lax.fori_loop(0, 31, bit_step, kth)
    kth = jnp.maximum(kth, jnp.int32(INT_MIN + 1))
    n_ge = count(lambda kk, kj: jnp.where(kk >= kth, one, nil))

    @pl.when(jnp.max(n_ge) > topk)
    def _():
        need = topk - count(lambda kk, kj: jnp.where(kk > kth, one, nil))

        def pos_step(i, cut):
            cand = cut | jnp.left_shift(one, jnp.int32(idx_bits - 1) - i)
            c = count(lambda kk, kj: jnp.where(
                kk == kth, jnp.where((kj * T + s_loc) < cand, one, nil), nil))
            return jnp.where(c < need, cand, cut)

        cut = lax.fori_loop(0, idx_bits, pos_step, zero)

        def demote(kj, carry):
            kk = keys_ref[kj]
            lowered = jnp.where((kj * T + s_loc) > cut, kth - one, kk)
            keys_ref[kj] = jnp.where(kk == kth, lowered, kk)
            return carry

        lax.fori_loop(0, nk, demote, 0)

    acc_ref[...] = jnp.zeros(acc_ref.shape, F32)

    def logits_to_scratch(kj):
        kvt = kv_ref[kj]
        for h in range(A_HEADS):
            s_ref[h] = jnp.dot(kvt, qlat_ref[h], preferred_element_type=F32)

    logits_to_scratch(0)

    def attend(kj, carry):
        ms, ls = carry
        sel = keys_ref[kj] >= kth
        kvTt = kvT_ref[kj]
        kv_next = kv_ref[jnp.minimum(kj + 1, nk - 1)]
        off = jnp.minimum(qi - kj, N_OFFSETS - 1)
        new_m, new_l = [], []
        for h in range(A_HEADS):
            s = jnp.where(sel, s_ref[h] + bias_ref[h, off], NEG)
            p, alpha, m_new, l_new = _softmax_step(s, ms[h], ls[h])
            s_ref[h] = jnp.dot(kv_next, qlat_ref[h], preferred_element_type=F32)
            acc_ref[h] = alpha * acc_ref[h] + jnp.dot(kvTt, p, preferred_element_type=F32)
            new_m.append(m_new)
            new_l.append(l_new)
        return tuple(new_m), tuple(new_l)

    init = (tuple(jnp.full((1, T), NEG, F32) for _ in range(A_HEADS)),
            tuple(jnp.zeros((1, T), F32) for _ in range(A_HEADS)))
    _, ls = lax.fori_loop(0, nk, attend, init)

    for h in range(A_HEADS):
        o_lat = (acc_ref[h] / ls[h]).astype(MM_DTYPE)
        oT_ref[h * HEAD_DIM:(h + 1) * HEAD_DIM, :] = jnp.dot(
            wuvT_ref[h], o_lat, preferred_element_type=F32)
    out_ref[...] = oT_ref[...].T.astype(out_ref.dtype)


def _dsa(aqT, iqT, iwT, kidx, kv, kvT, bias_a, wukT, wuvT):
    bsz, _, seq = aqT.shape
    T = ATT_TILE
    nk = seq // T
    topk = min(TOPK_MAX, seq // 4)
    idx_bits = int(math.log2(seq))
    assert 2 ** idx_bits == seq
    qspec = lambda rows: pl.BlockSpec((None, rows, T), lambda b, i: (b, 0, i))
    kspec = lambda a, c: pl.BlockSpec((None, nk, a, c), lambda b, i: (b, 0, 0, 0))
    return pl.pallas_call(
        functools.partial(_dsa_kernel, topk=topk, idx_bits=idx_bits),
        out_shape=jax.ShapeDtypeStruct((bsz, seq, A_HEADS * HEAD_DIM), MM_DTYPE),
        grid=(bsz, nk),
        in_specs=[qspec(A_HEADS * HEAD_DIM), qspec(IDX_HEADS * IDX_DIM), qspec(IDX_HEADS),
                  kspec(T, IDX_DIM), kspec(T, A_LATENT), kspec(A_LATENT, T),
                  _const_spec(bias_a.shape), _const_spec(wukT.shape), _const_spec(wuvT.shape)],
        out_specs=pl.BlockSpec((None, T, A_HEADS * HEAD_DIM), lambda b, i: (b, i, 0)),
        scratch_shapes=[pltpu.VMEM((nk, T, T), I32),
                        pltpu.VMEM((A_HEADS, A_LATENT, T), MM_DTYPE),
                        pltpu.VMEM((A_HEADS, A_LATENT, T), F32),
                        pltpu.VMEM((A_HEADS, T, T), F32),
                        pltpu.VMEM((A_HEADS * HEAD_DIM, T), F32)],
        compiler_params=_cparams(2), name="dsa_attention",
    )(aqT, iqT, iwT, kidx, kv, kvT, bias_a, wukT, wuvT)


def _diff_kernel(qT_ref, k_ref, vT_ref, bias_ref, lam_ref, gsub_ref, out_ref,
                 qz_ref, acc_ref, s_ref, oT_ref, *, lam_init):
    T = ATT_TILE
    dv = 2 * HEAD_DIM
    n_chain = 2 * B_HEADS
    qi = pl.program_id(1)
    half = lax.broadcasted_iota(I32, (dv, T), 0) < HEAD_DIM
    for h in range(B_HEADS):
        q = qT_ref[h * dv:(h + 1) * dv, :].astype(F32) * ATTN_SCALE
        qz_ref[2 * h] = jnp.where(half, q, 0.0).astype(MM_DTYPE)
        qz_ref[2 * h + 1] = jnp.where(half, 0.0, q).astype(MM_DTYPE)
    s_loc = lax.broadcasted_iota(I32, (T, T), 0)
    t_loc = lax.broadcasted_iota(I32, (T, T), 1)
    acc_ref[...] = jnp.zeros(acc_ref.shape, F32)

    def logits(kj, c):
        h = c // 2
        return jnp.dot(k_ref[kj, :, h * dv:(h + 1) * dv], qz_ref[c], preferred_element_type=F32)

    for c in range(n_chain):
        s_ref[c] = logits(0, c)

    def step(kj, carry, diagonal):
        ms, ls = carry
        off = jnp.minimum(qi - kj, N_OFFSETS - 1)
        new_m, new_l = [], []
        for c in range(n_chain):
            h = c // 2
            s = s_ref[c] + bias_ref[h, off]
            if diagonal:
                s = jnp.where(s_loc <= t_loc, s, NEG)
            p, alpha, m_new, l_new = _softmax_step(s, ms[c], ls[c])
            if not diagonal:
                s_ref[c] = logits(kj + 1, c)
            acc_ref[c] = alpha * acc_ref[c] + jnp.dot(vT_ref[kj, h * dv:(h + 1) * dv, :], p,
                                                      preferred_element_type=F32)
            new_m.append(m_new)
            new_l.append(l_new)
        return tuple(new_m), tuple(new_l)

    init = (tuple(jnp.full((1, T), NEG, F32) for _ in range(n_chain)),
            tuple(jnp.zeros((1, T), F32) for _ in range(n_chain)))
    carry = lax.fori_loop(0, qi, lambda kj, cr: step(kj, cr, False), init)
    _, ls = step(qi, carry, True)

    lr = lam_ref[...]
    lam = (jnp.exp(jnp.sum(lr[0:1, :] * lr[1:2, :], axis=1, keepdims=True))
           - jnp.exp(jnp.sum(lr[2:3, :] * lr[3:4, :], axis=1, keepdims=True)) + lam_init)
    for h in range(B_HEADS):
        attn = acc_ref[2 * h] / ls[2 * h] - lam * (acc_ref[2 * h + 1] / ls[2 * h + 1])
        y = attn * lax.rsqrt(jnp.mean(attn * attn, axis=0, keepdims=True) + EPS)
        oT_ref[h * dv:(h + 1) * dv, :] = y * gsub_ref[...] * (1.0 - lam_init)
    out_ref[...] = oT_ref[...].T.astype(out_ref.dtype)


def _diff(bqT, bk, bvT, bias_b, lam_rows, gsub, lam_init):
    bsz, _, seq = bqT.shape
    T = ATT_TILE
    nk = seq // T
    dv = 2 * HEAD_DIM
    qspec = pl.BlockSpec((None, B_HEADS * dv, T), lambda b, i: (b, 0, i))
    return pl.pallas_call(
        functools.partial(_diff_kernel, lam_init=lam_init),
        out_shape=jax.ShapeDtypeStruct((bsz, seq, B_HEADS * dv), MM_DTYPE),
        grid=(bsz, nk),
        in_specs=[qspec,
                  pl.BlockSpec((None, nk, T, B_HEADS * dv), lambda b, i: (b, 0, 0, 0)),
                  pl.BlockSpec((None, nk, B_HEADS * dv, T), lambda b, i: (b, 0, 0, 0)),
                  _const_spec(bias_b.shape), _const_spec((4, HEAD_DIM)), _const_spec((dv, 1))],
        out_specs=pl.BlockSpec((None, T, B_HEADS * dv), lambda b, i: (b, i, 0)),
        scratch_shapes=[pltpu.VMEM((2 * B_HEADS, dv, T), MM_DTYPE),
                        pltpu.VMEM((2 * B_HEADS, dv, T), F32),
                        pltpu.VMEM((2 * B_HEADS, T, T), F32),
                        pltpu.VMEM((B_HEADS * dv, T), F32)],
        compiler_params=_cparams(2), name="diff_attention",
    )(bqT, bk, bvT, bias_b, lam_rows, gsub)


def _dil_kernel(cur_ref, halo_ref, bias_ref, out_ref, lse_ref, *, tq):
    n = C_BAND
    wid = C_HPG * HEAD_DIM
    halo_lo = jnp.where(pl.program_id(2) == 0, jnp.int32(n), jnp.int32(0))
    i = lax.broadcasted_iota(I32, (n, 2 * n), 0)
    j = lax.broadcasted_iota(I32, (n, 2 * n), 1)
    lane_head = lax.broadcasted_iota(I32, (n, wid), 1) // HEAD_DIM
    for c in range(tq // n):
        q = cur_ref[c * n:(c + 1) * n, 0:wid].astype(F32) * ATTN_SCALE
        if c == 0:
            keys = jnp.concatenate([halo_ref[:, wid:2 * wid], cur_ref[0:n, wid:2 * wid]], axis=0)
            vals = jnp.concatenate([halo_ref[:, 2 * wid:3 * wid], cur_ref[0:n, 2 * wid:3 * wid]], axis=0)
            lo = jnp.maximum(i, halo_lo)
        else:
            keys = cur_ref[(c - 1) * n:(c + 1) * n, wid:2 * wid]
            vals = cur_ref[(c - 1) * n:(c + 1) * n, 2 * wid:3 * wid]
            lo = i
        out = jnp.zeros((n, wid), F32)
        lse = jnp.zeros((n, wid), F32)
        for h in range(C_HPG):
            in_head = lane_head == h
            qh = jnp.where(in_head, q, 0.0).astype(MM_DTYPE)
            s = lax.dot_general(qh, keys, (((1,), (1,)), ((), ())),
                                preferred_element_type=F32) + bias_ref[h]
            s = jnp.where(j >= lo, jnp.where(j <= i + n, s, NEG), NEG)
            m = jnp.max(s, axis=1, keepdims=True)
            p = jnp.exp(s - m)
            den = jnp.sum(p, axis=1, keepdims=True)
            o = jnp.dot(p.astype(MM_DTYPE), vals, preferred_element_type=F32) / den
            out = jnp.where(in_head, o, out)
            lse = jnp.where(in_head, m + jnp.log(den), lse)
        out_ref[c * n:(c + 1) * n, :] = out
        lse_ref[c * n:(c + 1) * n, :] = lse


def _dilated_group(cg, bias_g, g):
    bsz, dil, m, gw = cg.shape
    wid = C_HPG * HEAD_DIM
    n = C_BAND
    assert m % n == 0 and gw == 3 * wid
    tq = min(m, 512)
    cur = pl.BlockSpec((None, None, tq, gw), lambda b, r, i: (b, r, i, 0))
    halo = pl.BlockSpec((None, None, n, gw),
                        lambda b, r, i: (b, r, jnp.maximum(i * (tq // n) - 1, 0), 0))
    outspec = pl.BlockSpec((None, None, tq, wid), lambda b, r, i: (b, r, i, 0))
    return pl.pallas_call(
        functools.partial(_dil_kernel, tq=tq),
        out_shape=[jax.ShapeDtypeStruct((bsz, dil, m, wid), F32)] * 2,
        grid=(bsz, dil, m // tq),
        in_specs=[cur, halo, pl.BlockSpec((C_HPG, n, 2 * n), lambda b, r, i: (0, 0, 0))],
        out_specs=[outspec, outspec],
        compiler_params=_cparams(3), name=f"dilated_group{g}",
    )(cg, cg, bias_g)


def _merge_kernel(x_ref, mod_ref, g1_ref, oa_ref, ob_ref,
                  c0_ref, c1_ref, c2_ref, s0_ref, s1_ref, s2_ref,
                  wza, wzb, wzc, wba, wbb, wbc, wo, out_ref, tok_ref):
    x = x_ref[...]
    tm = x.shape[0]
    h = _rms(x) * g1_ref[...]
    h = h * (1.0 + mod_ref[1:2, :]) + mod_ref[0:1, :]
    hb = h.astype(MM_DTYPE)

    def token_order(k, ref):
        dil, _, w = ref.shape
        if dil == 1:
            return ref[0]
        n_chunk = w // 128
        for r in range(dil):
            for j in range(n_chunk):
                tok_ref[k * n_chunk + j, pl.ds(r, tm // dil, stride=dil), :] = ref[
                    r, :, j * 128:(j + 1) * 128]
        return jnp.concatenate([tok_ref[k * n_chunk + j] for j in range(n_chunk)], axis=1)

    s0, s1, s2 = s0_ref[0], token_order(0, s1_ref), token_order(1, s2_ref)
    c0, c1, c2 = c0_ref[0], token_order(2, c1_ref), token_order(3, c2_ref)
    mx = jnp.maximum(jnp.maximum(s0, s1), s2)
    e0, e1, e2 = jnp.exp(s0 - mx), jnp.exp(s1 - mx), jnp.exp(s2 - mx)
    oc = (e0 * c0 + e1 * c1 + e2 * c2) / (e0 + e1 + e2)

    def gated(wz, o, wb):
        z = jnp.dot(hb, wz[...], preferred_element_type=F32)
        return jax.nn.sigmoid(z) * jnp.dot(o, wb[...], preferred_element_type=F32)

    merged = (gated(wza, oa_ref[...], wba) + gated(wzb, ob_ref[...], wbb)
              + gated(wzc, oc.astype(MM_DTYPE), wbc))
    y = jnp.dot(merged.astype(MM_DTYPE), wo[...], preferred_element_type=F32)
    out_ref[...] = x + mod_ref[2:3, :] * y


def _merge(x2d, mod_l, g1, oa, ob, ocs, lses, ws, seq):
    n, d = x2d.shape
    tm = ROW_TILE
    per_b = seq // tm
    row = lambda wd: pl.BlockSpec((tm, wd), lambda i: (i, 0))
    wid = C_HPG * HEAD_DIM
    res = [pl.BlockSpec((None, dil, tm // dil, wid), lambda i: (i // per_b, 0, i % per_b, 0))
           for _, dil in C_GROUPS]
    in_specs = [row(d), pl.BlockSpec((None, 6, d), lambda i: (i // per_b, 0, 0)), _const_spec((1, d)),
                row(oa.shape[1]), row(ob.shape[1])] + res + res
    in_specs += [_const_spec(w.shape) for w in ws]
    return pl.pallas_call(
        _merge_kernel, out_shape=jax.ShapeDtypeStruct((n, d), F32), grid=(n // tm,),
        in_specs=in_specs, out_specs=row(d),
        scratch_shapes=[pltpu.VMEM((4 * wid // 128, tm, 128), F32)],
        compiler_params=_cparams(1), name="gated_merge",
    )(x2d, mod_l, g1, oa, ob, *ocs, *lses, *ws)


def _ffn_kernel(x_ref, mod_ref, g2_ref, gf_ref, wgu, wd, out_ref, *, chunk, final_norm):
    x = x_ref[...]
    h = _rms(x) * g2_ref[...]
    h = h * (1.0 + mod_ref[4:5, :]) + mod_ref[3:4, :]
    hb = h.astype(MM_DTYPE)
    acc = jnp.zeros(x.shape, F32)
    for c in range(D_FF // chunk):
        fg = jnp.dot(hb, wgu[:, c * chunk:(c + 1) * chunk], preferred_element_type=F32)
        fu = jnp.dot(hb, wgu[:, D_FF + c * chunk:D_FF + (c + 1) * chunk],
                     preferred_element_type=F32)
        act = (fg * jax.nn.sigmoid(fg) * fu).astype(MM_DTYPE)
        acc = acc + jnp.dot(act, wd[c * chunk:(c + 1) * chunk, :], preferred_element_type=F32)
    y = x + mod_ref[5:6, :] * acc
    if final_norm:
        y = _rms(y) * gf_ref[...]
    out_ref[...] = y


def _ffn(x2d, mod_l, g2, gf, wgu, wd, seq, final_norm):
    n, d = x2d.shape
    tm = ROW_TILE
    per_b = seq // tm
    row = pl.BlockSpec((tm, d), lambda i: (i, 0))
    return pl.pallas_call(
        functools.partial(_ffn_kernel, chunk=256, final_norm=final_norm),
        out_shape=jax.ShapeDtypeStruct((n, d), F32), grid=(n // tm,),
        in_specs=[row, pl.BlockSpec((None, 6, d), lambda i: (i // per_b, 0, 0)),
                  _const_spec((1, d)), _const_spec((1, d)),
                  _const_spec(wgu.shape), _const_spec(wd.shape)],
        out_specs=row,
        compiler_params=_cparams(1), name="swiglu_ffn",
    )(x2d, mod_l, g2, gf, wgu, wd)


def kernel(x, c, w_ada, b_ada, g_norm1, w_in, w_uk, w_uv, g_kv, lam_q1, lam_k1, lam_q2, lam_k2,
           g_subln, w_branch_a, w_branch_b, w_branch_c, w_out, g_norm2, w_gate_up, w_down,
           rel_bias, g_final):
    bsz, seq, d = x.shape
    depth = w_ada.shape[0]
    T = ATT_TILE
    nk = seq // T
    assert d == D_MODEL and seq % T == 0 and seq % ROW_TILE == 0
    n = bsz * seq
    cast = lambda w: w.astype(MM_DTYPE)

    thresholds = _bucket_thresholds(seq + 2 * C_BAND * C_GROUPS[-1][1])
    assert seq <= N_OFFSETS * T or (N_OFFSETS - 2) * T + 1 >= thresholds[-1]
    tab = rel_bias.reshape(-1)
    bias_a = _bias_att_tiles(tab, 0, A_HEADS, T, thresholds)
    bias_b = _bias_att_tiles(tab, A_HEADS, B_HEADS, T, thresholds)
    bias_c = _bias_dil_tiles(tab, A_HEADS + B_HEADS, thresholds)

    mod = _modulation(c, w_ada, b_ada).reshape(depth, bsz, 6, d)

    splits = (A_HEADS * HEAD_DIM, A_LATENT, IDX_HEADS * IDX_DIM, IDX_DIM, IDX_HEADS,
              B_HEADS * 2 * HEAD_DIM, B_HEADS * 2 * HEAD_DIM, B_HEADS * 2 * HEAD_DIM,
              C_HEADS * HEAD_DIM, C_HEADS * HEAD_DIM, C_HEADS * HEAD_DIM, d, d, d)
    offs = np.concatenate([[0], np.cumsum(splits)])
    seg = lambda w, k: w[:, int(offs[k]):int(offs[k + 1])]

    x2d = x.reshape(n, d)
    for l in range(depth):
        wl = w_in[l]
        castT = lambda w: w.T.astype(MM_DTYPE)
        wT_iw = jnp.pad(seg(wl, 4).T, ((0, 16 - IDX_HEADS), (0, 0)))
        wid = C_HPG * HEAD_DIM
        w_c = jnp.concatenate([seg(wl, k)[:, g * wid:(g + 1) * wid]
                               for g in range(len(C_GROUPS)) for k in (8, 9, 10)], axis=1)
        ws_in = [castT(seg(wl, 0)), castT(seg(wl, 2)), cast(wT_iw), castT(seg(wl, 5)), castT(seg(wl, 7)),
                 cast(seg(wl, 3)), cast(seg(wl, 1)), cast(seg(wl, 6)), cast(w_c)]
        g1 = g_norm1[l].reshape(1, d)
        (aqT, iqT, iwT, bqT, bvT, ik, kv, kvT, bk, cg0, cg1, cg2) = _in_proj(
            x2d, mod[l], g1, g_kv[l].reshape(1, A_LATENT), ws_in, bsz, seq)

        o_a = _dsa(aqT, iqT, iwT,
                   ik.reshape(bsz, nk, T, IDX_DIM), kv.reshape(bsz, nk, T, A_LATENT), kvT,
                   bias_a, cast(w_uk[l].transpose(0, 2, 1)), cast(w_uv[l].transpose(0, 2, 1)))
        o_a = o_a.reshape(n, -1)

        lam_init = 0.8 - 0.6 * math.exp(-0.3 * l)
        lam_rows = jnp.stack([lam_q1[l], lam_k1[l], lam_q2[l], lam_k2[l]])
        dv = 2 * HEAD_DIM
        o_b = _diff(bqT, bk.reshape(bsz, nk, T, B_HEADS * dv), bvT, bias_b, lam_rows,
                    g_subln[l].reshape(dv, 1), lam_init)
        o_b = o_b.reshape(n, -1)

        ocs, lses = [], []
        for g, (cg, (window, dil)) in enumerate(zip((cg0, cg1, cg2), C_GROUPS)):
            assert window // dil == C_BAND
            o, s = _dilated_group(cg, bias_c[g * C_HPG:(g + 1) * C_HPG], g)
            ocs.append(o)
            lses.append(s)

        ws_merge = [cast(seg(wl, 11)), cast(seg(wl, 12)), cast(seg(wl, 13)),
                    cast(w_branch_a[l]), cast(w_branch_b[l]), cast(w_branch_c[l]), cast(w_out[l])]
        x2d = _merge(x2d, mod[l], g1, o_a, o_b, ocs, lses, ws_merge, seq)
        x2d = _ffn(x2d, mod[l], g_norm2[l].reshape(1, d), g_final.reshape(1, d),
                   cast(w_gate_up[l]), cast(w_down[l]), seq, final_norm=(l == depth - 1))
    return x2d.reshape(bsz, seq, d)
```

```python
import functools
import math

import numpy as np
import jax
import jax.numpy as jnp
from jax import lax
from jax.experimental import pallas as pl
from jax.experimental.pallas import tpu as pltpu

D_MODEL = 1024
HEAD_DIM = 64
ATTN_SCALE = HEAD_DIM ** -0.5
A_HEADS = 8
A_LATENT = 128
IDX_HEADS = 8
IDX_DIM = 64
IDX_SCALE = (IDX_HEADS * IDX_DIM) ** -0.5
TOPK_MAX = 256
B_HEADS = 4
C_GROUPS = ((128, 1), (512, 4), (2048, 16))
C_HPG = 4
C_HEADS = C_HPG * len(C_GROUPS)
N_BUCKETS = 32
MAX_DISTANCE = 2048
N_BIAS_HEADS = A_HEADS + B_HEADS + C_HEADS
D_FF = -(-8 * D_MODEL // (3 * 256)) * 256
EPS = 1e-6

MM_DTYPE = jnp.bfloat16
F32 = jnp.float32
I32 = jnp.int32

ATT_TILE = 256
N_OFFSETS = 8
C_BAND = 128
ROW_TILE = 512
NEG = -1e30
INT_MIN = -2 ** 31
VMEM_LIMIT = 56 * 1024 * 1024


def _cparams(n_axes, vmem=VMEM_LIMIT):
    return pltpu.CompilerParams(dimension_semantics=("arbitrary",) * n_axes,
                                vmem_limit_bytes=vmem)


def _const_spec(shape):
    nd = len(shape)
    return pl.BlockSpec(shape, lambda *_: (0,) * nd, pipeline_mode=pl.Buffered(1))


def _bucket_thresholds(max_dist):
    n = np.arange(max_dist + 1)
    max_exact = N_BUCKETS // 2
    nf = np.maximum(n, 1).astype(np.float32)
    large = max_exact + (np.log(nf / np.float32(max_exact))
                         / np.float32(math.log(MAX_DISTANCE / max_exact))
                         * np.float32(N_BUCKETS - max_exact)).astype(np.int32)
    large = np.minimum(large, N_BUCKETS - 1)
    bucket = np.where(n < max_exact, n, large)
    assert np.all(np.diff(bucket) >= 0)
    thr = []
    for k in range(1, N_BUCKETS):
        idx = np.nonzero(bucket >= k)[0]
        thr.append(int(idx[0]) if idx.size else None)
    return thr


def _bias_from_dist(dist, tab_ref, col, thresholds):
    b = jnp.full(dist.shape, tab_ref[col], F32)
    for k, thr in enumerate(thresholds, start=1):
        if thr is None:
            break
        b = jnp.where(dist >= thr, tab_ref[k * N_BIAS_HEADS + col], b)
    return b


def _bias_att_kernel(tab_ref, out_ref, *, head0, tile, thresholds):
    h = pl.program_id(0)
    o = pl.program_id(1)
    row = lax.broadcasted_iota(I32, (tile, tile), 0)
    colq = lax.broadcasted_iota(I32, (tile, tile), 1)
    dist = jnp.maximum(o * tile + colq - row, 0)
    out_ref[...] = _bias_from_dist(dist, tab_ref, head0 + h, thresholds)


def _bias_att_tiles(tab, head0, n_heads, tile, thresholds):
    return pl.pallas_call(
        functools.partial(_bias_att_kernel, head0=head0, tile=tile, thresholds=thresholds),
        out_shape=jax.ShapeDtypeStruct((n_heads, N_OFFSETS, tile, tile), F32),
        grid=(n_heads, N_OFFSETS),
        in_specs=[pl.BlockSpec(memory_space=pltpu.SMEM)],
        out_specs=pl.BlockSpec((None, None, tile, tile), lambda h, o: (h, o, 0, 0)),
        compiler_params=_cparams(2),
        name="bias_att_tiles",
    )(tab)


def _bias_dil_kernel(tab_ref, out_ref, *, head0, thresholds):
    h = pl.program_id(0)
    g = h // C_HPG
    dil = jnp.where(g == 0, C_GROUPS[0][1], jnp.where(g == 1, C_GROUPS[1][1], C_GROUPS[2][1]))
    i = lax.broadcasted_iota(I32, (C_BAND, 2 * C_BAND), 0)
    j = lax.broadcasted_iota(I32, (C_BAND, 2 * C_BAND), 1)
    dist = jnp.maximum((i - j + C_BAND) * dil, 0)
    out_ref[...] = _bias_from_dist(dist, tab_ref, head0 + h, thresholds)


def _bias_dil_tiles(tab, head0, thresholds):
    return pl.pallas_call(
        functools.partial(_bias_dil_kernel, head0=head0, thresholds=thresholds),
        out_shape=jax.ShapeDtypeStruct((C_HEADS, C_BAND, 2 * C_BAND), F32),
        grid=(C_HEADS,),
        in_specs=[pl.BlockSpec(memory_space=pltpu.SMEM)],
        out_specs=pl.BlockSpec((None, C_BAND, 2 * C_BAND), lambda h: (h, 0, 0)),
        compiler_params=_cparams(1),
        name="bias_dil_tiles",
    )(tab)


def _mod_kernel(c_ref, w_ref, b_ref, out_ref):
    c = c_ref[...]
    ca = (c * jax.nn.sigmoid(c)).astype(MM_DTYPE)
    out_ref[...] = jnp.dot(ca, w_ref[...].astype(MM_DTYPE), preferred_element_type=F32) + b_ref[...]


def _modulation(c, w_ada, b_ada):
    depth, d, wid = w_ada.shape
    bsz = c.shape[0]
    tn = 1536
    assert wid % tn == 0
    return pl.pallas_call(
        _mod_kernel,
        out_shape=jax.ShapeDtypeStruct((depth, bsz, wid), F32),
        grid=(depth, wid // tn),
        in_specs=[pl.BlockSpec((bsz, d), lambda l, j: (0, 0)),
                  pl.BlockSpec((None, d, tn), lambda l, j: (l, 0, j)),
                  pl.BlockSpec((None, 1, tn), lambda l, j: (l, 0, j))],
        out_specs=pl.BlockSpec((None, bsz, tn), lambda l, j: (l, 0, j)),
        compiler_params=_cparams(2),
        name="adaln_modulation",
    )(c, w_ada, b_ada.reshape(depth, 1, wid))


def _rms(x):
    return x * lax.rsqrt(jnp.mean(x * x, axis=-1, keepdims=True) + EPS)


def _in_kernel(x_ref, mod_ref, g1_ref, gkv_ref,
               wT_aq, wT_iq, wT_iw, wT_bq, wT_bv, w_ik, w_kv, w_bk, w_c,
               o_aqT, o_iqT, o_iwT, o_bqT, o_bvT, o_ik, o_kv, o_kvT, o_bk, o_c0, o_c1, o_c2,
               c_scr):
    T = ATT_TILE
    tm = x_ref.shape[0]
    h = _rms(x_ref[...]) * g1_ref[...]
    h = h * (1.0 + mod_ref[1:2, :]) + mod_ref[0:1, :]
    hb = h.astype(MM_DTYPE)

    def mm(w):
        return jnp.dot(hb, w[...], preferred_element_type=F32)

    def mm_t(wT):
        return lax.dot_general(wT[...], hb, (((1,), (1,)), ((), ())), preferred_element_type=F32)

    o_aqT[...] = mm_t(wT_aq).astype(o_aqT.dtype)
    o_iqT[...] = mm_t(wT_iq).astype(o_iqT.dtype)
    o_bqT[...] = mm_t(wT_bq).astype(o_bqT.dtype)
    o_iwT[...] = (mm_t(wT_iw) * IDX_SCALE)[:IDX_HEADS]
    bvT = mm_t(wT_bv).astype(o_bvT.dtype)
    kv = _rms(mm(w_kv)) * gkv_ref[...]
    kvT = kv.T.astype(o_kvT.dtype)
    for j in range(tm // T):
        o_bvT[j] = bvT[:, j * T:(j + 1) * T]
        o_kvT[j] = kvT[:, j * T:(j + 1) * T]
    o_kv[...] = kv.astype(o_kv.dtype)
    o_ik[...] = mm(w_ik).astype(o_ik.dtype)
    o_bk[...] = mm(w_bk).astype(o_bk.dtype)

    yc = mm(w_c)
    n_chunk = yc.shape[1] // 128
    for j in range(n_chunk):
        c_scr[j] = yc[:, j * 128:(j + 1) * 128]
    per_group = n_chunk // len(C_GROUPS)
    for g, o_c in enumerate((o_c0, o_c1, o_c2)):
        dil = C_GROUPS[g][1]
        for r in range(dil):
            for jj in range(per_group):
                o_c[r, :, jj * 128:(jj + 1) * 128] = c_scr[
                    g * per_group + jj, pl.ds(r, tm // dil, stride=dil), :].astype(o_c.dtype)


def _in_proj(x2d, mod_l, g1, gkv, ws, bsz, seq):
    n, d = x2d.shape
    tm = ROW_TILE
    T = ATT_TILE
    per_b = seq // tm
    nk = seq // T
    hd = A_HEADS * HEAD_DIM
    bw = B_HEADS * 2 * HEAD_DIM
    gw = 3 * C_HPG * HEAD_DIM
    in_specs = [pl.BlockSpec((tm, d), lambda i: (i, 0)),
                pl.BlockSpec((None, 6, d), lambda i: (i // per_b, 0, 0)),
                _const_spec((1, d)), _const_spec((1, A_LATENT))]
    in_specs += [_const_spec(w.shape) for w in ws]

    def tspec(rows):
        return pl.BlockSpec((None, rows, tm), lambda i: (i // per_b, 0, i % per_b))

    def tile_tspec(rows):
        return pl.BlockSpec((None, tm // T, rows, T), lambda i: (i // per_b, i % per_b, 0, 0))

    def rspec(wd):
        return pl.BlockSpec((tm, wd), lambda i: (i, 0))

    def cspec(dil):
        return pl.BlockSpec((None, dil, tm // dil, gw), lambda i: (i // per_b, 0, i % per_b, 0))

    sds = jax.ShapeDtypeStruct
    out_specs = [tspec(hd), tspec(IDX_HEADS * IDX_DIM), tspec(IDX_HEADS), tspec(bw),
                 tile_tspec(bw), rspec(IDX_DIM), rspec(A_LATENT), tile_tspec(A_LATENT), rspec(bw)]
    out_shape = [sds((bsz, hd, seq), MM_DTYPE), sds((bsz, IDX_HEADS * IDX_DIM, seq), MM_DTYPE),
                 sds((bsz, IDX_HEADS, seq), F32), sds((bsz, bw, seq), MM_DTYPE),
                 sds((bsz, nk, bw, T), MM_DTYPE), sds((n, IDX_DIM), MM_DTYPE),
                 sds((n, A_LATENT), MM_DTYPE), sds((bsz, nk, A_LATENT, T), MM_DTYPE),
                 sds((n, bw), MM_DTYPE)]
    for _, dil in C_GROUPS:
        out_specs.append(cspec(dil))
        out_shape.append(sds((bsz, dil, seq // dil, gw), MM_DTYPE))
    return pl.pallas_call(
        _in_kernel, out_shape=out_shape, grid=(n // tm,),
        in_specs=in_specs, out_specs=out_specs,
        scratch_shapes=[pltpu.VMEM((3 * gw // 128, tm, 128), F32)],
        compiler_params=_cparams(1), name="in_proj",
    )(x2d, mod_l, g1, gkv, *ws)


def _softmax_step(s, m_old, l_old):
    m_new = jnp.maximum(m_old, jnp.max(s, axis=0, keepdims=True))
    p = jnp.exp(s - m_new)
    alpha = jnp.exp(m_old - m_new)
    l_new = alpha * l_old + jnp.sum(p, axis=0, keepdims=True)
    return p.astype(MM_DTYPE), alpha, m_new, l_new


def _bit_transpose32(words):
    a = list(words)
    j, mask = 16, 0x0000FFFF
    while j:
        k = 0
        while k < 32:
            t = (a[k] ^ lax.shift_right_logical(a[k + j], jnp.int32(j))) & jnp.int32(mask)
            a[k] = a[k] ^ t
            a[k + j] = a[k + j] ^ lax.shift_left(t, jnp.int32(j))
            k = (k + j + 1) & ~j
        j >>= 1
        mask = (mask ^ (mask << j)) & 0xFFFFFFFF
    return a


def _dsa_kernel(aqT_ref, iqT_ref, iwT_ref, kidx_ref, kv_ref, kvT_ref, bias_ref, wukT_ref, wuvT_ref,
                out_ref, keys_ref, planes_ref, qlat_ref, acc_ref, s_ref, oT_ref, *, topk, idx_bits):
    T = ATT_TILE
    qi = pl.program_id(1)
    nk = qi + 1

    for h in range(A_HEADS):
        q = jnp.dot(wukT_ref[h], aqT_ref[h * HEAD_DIM:(h + 1) * HEAD_DIM, :],
                    preferred_element_type=F32) * ATTN_SCALE
        qlat_ref[h] = q.astype(qlat_ref.dtype)

    s_loc = lax.broadcasted_iota(I32, (T, T), 0)
    t_loc = lax.broadcasted_iota(I32, (T, T), 1)

    def score_tile(kj, carry):
        kt = kidx_ref[kj]
        acc = jnp.zeros((T, T), F32)
        for h in range(IDX_HEADS):
            s = jnp.dot(kt, iqT_ref[h * IDX_DIM:(h + 1) * IDX_DIM, :], preferred_element_type=F32)
            acc = acc + jnp.maximum(s, 0.0) * iwT_ref[h:h + 1, :]
        bits = lax.bitcast_convert_type(acc, I32)
        key = jnp.where(bits < 0, bits ^ jnp.int32(0x7FFFFFFF), bits)
        visible = (kj * T + s_loc) <= (qi * T + t_loc)
        key = jnp.where(visible, key, jnp.int32(INT_MIN))
        keys_ref[kj] = key
        planes = _bit_transpose32([key[8 * i:8 * (i + 1), :] ^ jnp.int32(INT_MIN) for i in range(32)])
        rows = pl.ds(pl.multiple_of(kj * 8, 8), 8)
        for b in range(32):
            planes_ref[b, rows, :] = planes[b]
        return carry

    lax.fori_loop(0, nk, score_tile, 0)

    def clear_planes(kj, carry):
        rows = pl.ds(pl.multiple_of(kj * 8, 8), 8)
        for b in range(32):
            planes_ref[b, rows, :] = jnp.zeros((8, T), I32)
        return carry

    lax.fori_loop(nk, planes_ref.shape[1] // 8, clear_planes, 0)

    one = jnp.int32(1)
    nil = jnp.int32(0)
    zero = jnp.zeros((1, T), I32)

    n_rows = planes_ref.shape[1]
    tile_of_row = lax.broadcasted_iota(I32, (n_rows, T), 0) // 8
    tied0 = jnp.where(tile_of_row < nk, jnp.int32(-1), nil)

    def bit_step(b, carry):
        tied, n_gt, kth_u = carry
        ones = tied & planes_ref[b]
        n1 = jnp.sum(lax.population_count(ones), axis=0, keepdims=True)
        take = (n_gt + n1) >= topk
        tied = jnp.where(take, ones, tied ^ ones)
        n_gt = jnp.where(take, n_gt, n_gt + n1)
        kth_u = jnp.where(take, kth_u | lax.shift_left(one, jnp.int32(31) - b), kth_u)
        return tied, n_gt, kth_u

    tied, n_gt, kth_u = lax.fori_loop(0, 32, bit_step, (tied0, zero, zero))
    n_eq = jnp.sum(lax.population_count(tied), axis=0, keepdims=True)
    n_ge = jnp.where(kth_u == nil, n_gt, n_gt + n_eq)
    kth = jnp.maximum(kth_u ^ jnp.int32(INT_MIN), jnp.int32(INT_MIN + 1))

    def count(hit_fn):
        def body(kj, c):
            hit = hit_fn(keys_ref[kj], kj)
            return c + jnp.sum(hit.reshape(T // 8, 8, T), axis=0)
        c = lax.fori_loop(0, nk, body, jnp.zeros((8, T), I32))
        return jnp.sum(c, axis=0, keepdims=True)

    @pl.when(jnp.max(n_ge) > topk)
    def _():
        need = topk - n_gt

        def pos_step(i, cut):
            cand = cut | jnp.left_shift(one, jnp.int32(idx_bits - 1) - i)
            c = count(lambda kk, kj: jnp.where(
                kk == kth, jnp.where((kj * T + s_loc) < cand, one, nil), nil))
            return jnp.where(c < need, cand, cut)

        cut = lax.fori_loop(0, idx_bits, pos_step, zero)

        def demote(kj, carry):
            kk = keys_ref[kj]
            lowered = jnp.where((kj * T + s_loc) > cut, kth - one, kk)
            keys_ref[kj] = jnp.where(kk == kth, lowered, kk)
            return carry

        lax.fori_loop(0, nk, demote, 0)

    acc_ref[...] = jnp.zeros(acc_ref.shape, F32)

    def logits_to_scratch(kj):
        kvt = kv_ref[kj]
        for h in range(A_HEADS):
            s_ref[h] = jnp.dot(kvt, qlat_ref[h], preferred_element_type=F32)

    logits_to_scratch(0)

    def attend(kj, carry):
        ms, ls = carry
        sel = keys_ref[kj] >= kth
        kvTt = kvT_ref[kj]
        kv_next = kv_ref[jnp.minimum(kj + 1, nk - 1)]
        off = jnp.minimum(qi - kj, N_OFFSETS - 1)
        new_m, new_l = [], []
        for h in range(A_HEADS):
            s = jnp.where(sel, s_ref[h] + bias_ref[h, off], NEG)
            p, alpha, m_new, l_new = _softmax_step(s, ms[h], ls[h])
            s_ref[h] = jnp.dot(kv_next, qlat_ref[h], preferred_element_type=F32)
            acc_ref[h] = alpha * acc_ref[h] + jnp.dot(kvTt, p, preferred_element_type=F32)
            new_m.append(m_new)
            new_l.append(l_new)
        return tuple(new_m), tuple(new_l)

    init = (tuple(jnp.full((1, T), NEG, F32) for _ in range(A_HEADS)),
            tuple(jnp.zeros((1, T), F32) for _ in range(A_HEADS)))
    _, ls = lax.fori_loop(0, nk, attend, init)

    for h in range(A_HEADS):
        o_lat = (acc_ref[h] / ls[h]).astype(MM_DTYPE)
        oT_ref[h * HEAD_DIM:(h + 1) * HEAD_DIM, :] = jnp.dot(
            wuvT_ref[h], o_lat, preferred_element_type=F32)
    out_ref[...] = oT_ref[...].T.astype(out_ref.dtype)


def _dsa(aqT, iqT, iwT, kidx, kv, kvT, bias_a, wukT, wuvT):
    bsz, _, seq = aqT.shape
    T = ATT_TILE
    nk = seq // T
    topk = min(TOPK_MAX, seq // 4)
    idx_bits = int(math.log2(seq))
    assert 2 ** idx_bits == seq
    qspec = lambda rows: pl.BlockSpec((None, rows, T), lambda b, i: (b, 0, i))
    kspec = lambda a, c: pl.BlockSpec((None, nk, a, c), lambda b, i: (b, 0, 0, 0))
    return pl.pallas_call(
        functools.partial(_dsa_kernel, topk=topk, idx_bits=idx_bits),
        out_shape=jax.ShapeDtypeStruct((bsz, seq, A_HEADS * HEAD_DIM), MM_DTYPE),
        grid=(bsz, nk),
        in_specs=[qspec(A_HEADS * HEAD_DIM), qspec(IDX_HEADS * IDX_DIM), qspec(IDX_HEADS),
                  kspec(T, IDX_DIM), kspec(T, A_LATENT), kspec(A_LATENT, T),
                  _const_spec(bias_a.shape), _const_spec(wukT.shape), _const_spec(wuvT.shape)],
        out_specs=pl.BlockSpec((None, T, A_HEADS * HEAD_DIM), lambda b, i: (b, i, 0)),
        scratch_shapes=[pltpu.VMEM((nk, T, T), I32),
                        pltpu.VMEM((32, nk * (T // 32), T), I32),
                        pltpu.VMEM((A_HEADS, A_LATENT, T), MM_DTYPE),
                        pltpu.VMEM((A_HEADS, A_LATENT, T), F32),
                        pltpu.VMEM((A_HEADS, T, T), F32),
                        pltpu.VMEM((A_HEADS * HEAD_DIM, T), F32)],
        compiler_params=_cparams(2), name="dsa_attention",
    )(aqT, iqT, iwT, kidx, kv, kvT, bias_a, wukT, wuvT)


def _diff_kernel(qT_ref, k_ref, vT_ref, bias_ref, lam_ref, gsub_ref, out_ref,
                 qz_ref, acc_ref, s_ref, oT_ref, *, lam_init):
    T = ATT_TILE
    dv = 2 * HEAD_DIM
    n_chain = 2 * B_HEADS
    qi = pl.program_id(1)
    half = lax.broadcasted_iota(I32, (dv, T), 0) < HEAD_DIM
    for h in range(B_HEADS):
        q = qT_ref[h * dv:(h + 1) * dv, :].astype(F32) * ATTN_SCALE
        qz_ref[2 * h] = jnp.where(half, q, 0.0).astype(MM_DTYPE)
        qz_ref[2 * h + 1] = jnp.where(half, 0.0, q).astype(MM_DTYPE)
    s_loc = lax.broadcasted_iota(I32, (T, T), 0)
    t_loc = lax.broadcasted_iota(I32, (T, T), 1)
    acc_ref[...] = jnp.zeros(acc_ref.shape, F32)

    def logits(kj, c):
        h = c // 2
        return jnp.dot(k_ref[kj, :, h * dv:(h + 1) * dv], qz_ref[c], preferred_element_type=F32)

    for c in range(n_chain):
        s_ref[c] = logits(0, c)

    def step(kj, carry, diagonal):
        ms, ls = carry
        off = jnp.minimum(qi - kj, N_OFFSETS - 1)
        new_m, new_l = [], []
        for c in range(n_chain):
            h = c // 2
            s = s_ref[c] + bias_ref[h, off]
            if diagonal:
                s = jnp.where(s_loc <= t_loc, s, NEG)
            p, alpha, m_new, l_new = _softmax_step(s, ms[c], ls[c])
            if not diagonal:
                s_ref[c] = logits(kj + 1, c)
            acc_ref[c] = alpha * acc_ref[c] + jnp.dot(vT_ref[kj, h * dv:(h + 1) * dv, :], p,
                                                      preferred_element_type=F32)
            new_m.append(m_new)
            new_l.append(l_new)
        return tuple(new_m), tuple(new_l)

    init = (tuple(jnp.full((1, T), NEG, F32) for _ in range(n_chain)),
            tuple(jnp.zeros((1, T), F32) for _ in range(n_chain)))
    carry = lax.fori_loop(0, qi, lambda kj, cr: step(kj, cr, False), init)
    _, ls = step(qi, carry, True)

    lr = lam_ref[...]
    lam = (jnp.exp(jnp.sum(lr[0:1, :] * lr[1:2, :], axis=1, keepdims=True))
           - jnp.exp(jnp.sum(lr[2:3, :] * lr[3:4, :], axis=1, keepdims=True)) + lam_init)
    for h in range(B_HEADS):
        attn = acc_ref[2 * h] / ls[2 * h] - lam * (acc_ref[2 * h + 1] / ls[2 * h + 1])
        y = attn * lax.rsqrt(jnp.mean(attn * attn, axis=0, keepdims=True) + EPS)
        oT_ref[h * dv:(h + 1) * dv, :] = y * gsub_ref[...] * (1.0 - lam_init)
    out_ref[...] = oT_ref[...].T.astype(out_ref.dtype)


def _diff(bqT, bk, bvT, bias_b, lam_rows, gsub, lam_init):
    bsz, _, seq = bqT.shape
    T = ATT_TILE
    nk = seq // T
    dv = 2 * HEAD_DIM
    qspec = pl.BlockSpec((None, B_HEADS * dv, T), lambda b, i: (b, 0, i))
    return pl.pallas_call(
        functools.partial(_diff_kernel, lam_init=lam_init),
        out_shape=jax.ShapeDtypeStruct((bsz, seq, B_HEADS * dv), MM_DTYPE),
        grid=(bsz, nk),
        in_specs=[qspec,
                  pl.BlockSpec((None, nk, T, B_HEADS * dv), lambda b, i: (b, 0, 0, 0)),
                  pl.BlockSpec((None, nk, B_HEADS * dv, T), lambda b, i: (b, 0, 0, 0)),
                  _const_spec(bias_b.shape), _const_spec((4, HEAD_DIM)), _const_spec((dv, 1))],
        out_specs=pl.BlockSpec((None, T, B_HEADS * dv), lambda b, i: (b, i, 0)),
        scratch_shapes=[pltpu.VMEM((2 * B_HEADS, dv, T), MM_DTYPE),
                        pltpu.VMEM((2 * B_HEADS, dv, T), F32),
                        pltpu.VMEM((2 * B_HEADS, T, T), F32),
                        pltpu.VMEM((B_HEADS * dv, T), F32)],
        compiler_params=_cparams(2), name="diff_attention",
    )(bqT, bk, bvT, bias_b, lam_rows, gsub)


def _dil_kernel(cur_ref, halo_ref, bias_ref, out_ref, lse_ref, *, tq):
    n = C_BAND
    wid = C_HPG * HEAD_DIM
    halo_lo = jnp.where(pl.program_id(2) == 0, jnp.int32(n), jnp.int32(0))
    i = lax.broadcasted_iota(I32, (n, 2 * n), 0)
    j = lax.broadcasted_iota(I32, (n, 2 * n), 1)
    lane_head = lax.broadcasted_iota(I32, (n, wid), 1) // HEAD_DIM
    for c in range(tq // n):
        q = cur_ref[c * n:(c + 1) * n, 0:wid].astype(F32) * ATTN_SCALE
        if c == 0:
            keys = jnp.concatenate([halo_ref[:, wid:2 * wid], cur_ref[0:n, wid:2 * wid]], axis=0)
            vals = jnp.concatenate([halo_ref[:, 2 * wid:3 * wid], cur_ref[0:n, 2 * wid:3 * wid]], axis=0)
            lo = jnp.maximum(i, halo_lo)
        else:
            keys = cur_ref[(c - 1) * n:(c + 1) * n, wid:2 * wid]
            vals = cur_ref[(c - 1) * n:(c + 1) * n, 2 * wid:3 * wid]
            lo = i
        out = jnp.zeros((n, wid), F32)
        lse = jnp.zeros((n, wid), F32)
        for h in range(C_HPG):
            in_head = lane_head == h
            qh = jnp.where(in_head, q, 0.0).astype(MM_DTYPE)
            s = lax.dot_general(qh, keys, (((1,), (1,)), ((), ())),
                                preferred_element_type=F32) + bias_ref[h]
            s = jnp.where(j >= lo, jnp.where(j <= i + n, s, NEG), NEG)
            m = jnp.max(s, axis=1, keepdims=True)
            p = jnp.exp(s - m)
            den = jnp.sum(p, axis=1, keepdims=True)
            o = jnp.dot(p.astype(MM_DTYPE), vals, preferred_element_type=F32) / den
            out = jnp.where(in_head, o, out)
            lse = jnp.where(in_head, m + jnp.log(den), lse)
        out_ref[c * n:(c + 1) * n, :] = out
        lse_ref[c * n:(c + 1) * n, :] = lse


def _dilated_group(cg, bias_g, g):
    bsz, dil, m, gw = cg.shape
    wid = C_HPG * HEAD_DIM
    n = C_BAND
    assert m % n == 0 and gw == 3 * wid
    tq = min(m, 512)
    cur = pl.BlockSpec((None, None, tq, gw), lambda b, r, i: (b, r, i, 0))
    halo = pl.BlockSpec((None, None, n, gw),
                        lambda b, r, i: (b, r, jnp.maximum(i * (tq // n) - 1, 0), 0))
    outspec = pl.BlockSpec((None, None, tq, wid), lambda b, r, i: (b, r, i, 0))
    return pl.pallas_call(
        functools.partial(_dil_kernel, tq=tq),
        out_shape=[jax.ShapeDtypeStruct((bsz, dil, m, wid), F32)] * 2,
        grid=(bsz, dil, m // tq),
        in_specs=[cur, halo, pl.BlockSpec((C_HPG, n, 2 * n), lambda b, r, i: (0, 0, 0))],
        out_specs=[outspec, outspec],
        compiler_params=_cparams(3), name=f"dilated_group{g}",
    )(cg, cg, bias_g)


def _merge_kernel(x_ref, mod_ref, g1_ref, oa_ref, ob_ref,
                  c0_ref, c1_ref, c2_ref, s0_ref, s1_ref, s2_ref,
                  wza, wzb, wzc, wba, wbb, wbc, wo, out_ref, tok_ref):
    x = x_ref[...]
    tm = x.shape[0]
    h = _rms(x) * g1_ref[...]
    h = h * (1.0 + mod_ref[1:2, :]) + mod_ref[0:1, :]
    hb = h.astype(MM_DTYPE)

    def token_order(k, ref):
        dil, _, w = ref.shape
        if dil == 1:
            return ref[0]
        n_chunk = w // 128
        for r in range(dil):
            for j in range(n_chunk):
                tok_ref[k * n_chunk + j, pl.ds(r, tm // dil, stride=dil), :] = ref[
                    r, :, j * 128:(j + 1) * 128]
        return jnp.concatenate([tok_ref[k * n_chunk + j] for j in range(n_chunk)], axis=1)

    s0, s1, s2 = s0_ref[0], token_order(0, s1_ref), token_order(1, s2_ref)
    c0, c1, c2 = c0_ref[0], token_order(2, c1_ref), token_order(3, c2_ref)
    mx = jnp.maximum(jnp.maximum(s0, s1), s2)
    e0, e1, e2 = jnp.exp(s0 - mx), jnp.exp(s1 - mx), jnp.exp(s2 - mx)
    oc = (e0 * c0 + e1 * c1 + e2 * c2) / (e0 + e1 + e2)

    def gated(wz, o, wb):
        z = jnp.dot(hb, wz[...], preferred_element_type=F32)
        return jax.nn.sigmoid(z) * jnp.dot(o, wb[...], preferred_element_type=F32)

    merged = (gated(wza, oa_ref[...], wba) + gated(wzb, ob_ref[...], wbb)
              + gated(wzc, oc.astype(MM_DTYPE), wbc))
    y = jnp.dot(merged.astype(MM_DTYPE), wo[...], preferred_element_type=F32)
    out_ref[...] = x + mod_ref[2:3, :] * y


def _merge(x2d, mod_l, g1, oa, ob, ocs, lses, ws, seq):
    n, d = x2d.shape
    tm = ROW_TILE
    per_b = seq // tm
    row = lambda wd: pl.BlockSpec((tm, wd), lambda i: (i, 0))
    wid = C_HPG * HEAD_DIM
    res = [pl.BlockSpec((None, dil, tm // dil, wid), lambda i: (i // per_b, 0, i % per_b, 0))
           for _, dil in C_GROUPS]
    in_specs = [row(d), pl.BlockSpec((None, 6, d), lambda i: (i // per_b, 0, 0)), _const_spec((1, d)),
                row(oa.shape[1]), row(ob.shape[1])] + res + res
    in_specs += [_const_spec(w.shape) for w in ws]
    return pl.pallas_call(
        _merge_kernel, out_shape=jax.ShapeDtypeStruct((n, d), F32), grid=(n // tm,),
        in_specs=in_specs, out_specs=row(d),
        scratch_shapes=[pltpu.VMEM((4 * wid // 128, tm, 128), F32)],
        compiler_params=_cparams(1), name="gated_merge",
    )(x2d, mod_l, g1, oa, ob, *ocs, *lses, *ws)


def _ffn_kernel(x_ref, mod_ref, g2_ref, gf_ref, wgu, wd, out_ref, *, chunk, final_norm):
    x = x_ref[...]
    h = _rms(x) * g2_ref[...]
    h = h * (1.0 + mod_ref[4:5, :]) + mod_ref[3:4, :]
    hb = h.astype(MM_DTYPE)
    acc = jnp.zeros(x.shape, F32)
    for c in range(D_FF // chunk):
        fg = jnp.dot(hb, wgu[:, c * chunk:(c + 1) * chunk], preferred_element_type=F32)
        fu = jnp.dot(hb, wgu[:, D_FF + c * chunk:D_FF + (c + 1) * chunk],
                     preferred_element_type=F32)
        act = (fg * jax.nn.sigmoid(fg) * fu).astype(MM_DTYPE)
        acc = acc + jnp.dot(act, wd[c * chunk:(c + 1) * chunk, :], preferred_element_type=F32)
    y = x + mod_ref[5:6, :] * acc
    if final_norm:
        y = _rms(y) * gf_ref[...]
    out_ref[...] = y


def _ffn(x2d, mod_l, g2, gf, wgu, wd, seq, final_norm):
    n, d = x2d.shape
    tm = ROW_TILE
    per_b = seq // tm
    row = pl.BlockSpec((tm, d), lambda i: (i, 0))
    return pl.pallas_call(
        functools.partial(_ffn_kernel, chunk=256, final_norm=final_norm),
        out_shape=jax.ShapeDtypeStruct((n, d), F32), grid=(n // tm,),
        in_specs=[row, pl.BlockSpec((None, 6, d), lambda i: (i // per_b, 0, 0)),
                  _const_spec((1, d)), _const_spec((1, d)),
                  _const_spec(wgu.shape), _const_spec(wd.shape)],
        out_specs=row,
        compiler_params=_cparams(1), name="swiglu_ffn",
    )(x2d, mod_l, g2, gf, wgu, wd)


def kernel(x, c, w_ada, b_ada, g_norm1, w_in, w_uk, w_uv, g_kv, lam_q1, lam_k1, lam_q2, lam_k2,
           g_subln, w_branch_a, w_branch_b, w_branch_c, w_out, g_norm2, w_gate_up, w_down,
           rel_bias, g_final):
    bsz, seq, d = x.shape
    depth = w_ada.shape[0]
    T = ATT_TILE
    nk = seq // T
    assert d == D_MODEL and seq % T == 0 and seq % ROW_TILE == 0
    n = bsz * seq
    cast = lambda w: w.astype(MM_DTYPE)

    thresholds = _bucket_thresholds(seq + 2 * C_BAND * C_GROUPS[-1][1])
    assert seq <= N_OFFSETS * T or (N_OFFSETS - 2) * T + 1 >= thresholds[-1]
    tab = rel_bias.reshape(-1)
    bias_a = _bias_att_tiles(tab, 0, A_HEADS, T, thresholds)
    bias_b = _bias_att_tiles(tab, A_HEADS, B_HEADS, T, thresholds)
    bias_c = _bias_dil_tiles(tab, A_HEADS + B_HEADS, thresholds)

    mod = _modulation(c, w_ada, b_ada).reshape(depth, bsz, 6, d)

    splits = (A_HEADS * HEAD_DIM, A_LATENT, IDX_HEADS * IDX_DIM, IDX_DIM, IDX_HEADS,
              B_HEADS * 2 * HEAD_DIM, B_HEADS * 2 * HEAD_DIM, B_HEADS * 2 * HEAD_DIM,
              C_HEADS * HEAD_DIM, C_HEADS * HEAD_DIM, C_HEADS * HEAD_DIM, d, d, d)
    offs = np.concatenate([[0], np.cumsum(splits)])
    seg = lambda w, k: w[:, int(offs[k]):int(offs[k + 1])]

    x2d = x.reshape(n, d)
    for l in range(depth):
        wl = w_in[l]
        castT = lambda w: w.T.astype(MM_DTYPE)
        wT_iw = jnp.pad(seg(wl, 4).T, ((0, 16 - IDX_HEADS), (0, 0)))
        wid = C_HPG * HEAD_DIM
        w_c = jnp.concatenate([seg(wl, k)[:, g * wid:(g + 1) * wid]
                               for g in range(len(C_GROUPS)) for k in (8, 9, 10)], axis=1)
        ws_in = [castT(seg(wl, 0)), castT(seg(wl, 2)), cast(wT_iw), castT(seg(wl, 5)), castT(seg(wl, 7)),
                 cast(seg(wl, 3)), cast(seg(wl, 1)), cast(seg(wl, 6)), cast(w_c)]
        g1 = g_norm1[l].reshape(1, d)
        (aqT, iqT, iwT, bqT, bvT, ik, kv, kvT, bk, cg0, cg1, cg2) = _in_proj(
            x2d, mod[l], g1, g_kv[l].reshape(1, A_LATENT), ws_in, bsz, seq)

        o_a = _dsa(aqT, iqT, iwT,
                   ik.reshape(bsz, nk, T, IDX_DIM), kv.reshape(bsz, nk, T, A_LATENT), kvT,
                   bias_a, cast(w_uk[l].transpose(0, 2, 1)), cast(w_uv[l].transpose(0, 2, 1)))
        o_a = o_a.reshape(n, -1)

        lam_init = 0.8 - 0.6 * math.exp(-0.3 * l)
        lam_rows = jnp.stack([lam_q1[l], lam_k1[l], lam_q2[l], lam_k2[l]])
        dv = 2 * HEAD_DIM
        o_b = _diff(bqT, bk.reshape(bsz, nk, T, B_HEADS * dv), bvT, bias_b, lam_rows,
                    g_subln[l].reshape(dv, 1), lam_init)
        o_b = o_b.reshape(n, -1)

        ocs, lses = [], []
        for g, (cg, (window, dil)) in enumerate(zip((cg0, cg1, cg2), C_GROUPS)):
            assert window // dil == C_BAND
            o, s = _dilated_group(cg, bias_c[g * C_HPG:(g + 1) * C_HPG], g)
            ocs.append(o)
            lses.append(s)

        ws_merge = [cast(seg(wl, 11)), cast(seg(wl, 12)), cast(seg(wl, 13)),
                    cast(w_branch_a[l]), cast(w_branch_b[l]), cast(w_branch_c[l]), cast(w_out[l])]
        x2d = _merge(x2d, mod[l], g1, o_a, o_b, ocs, lses, ws_merge, seq)
        x2d = _ffn(x2d, mod[l], g_norm2[l].reshape(1, d), g_final.reshape(1, d),
                   cast(w_gate_up[l]), cast(w_down[l]), seq, final_norm=(l == depth - 1))
    return x2d.reshape(bsz, seq, d)
```

```python
import functools
import math

import numpy as np
import jax
import jax.numpy as jnp
from jax import lax
from jax.experimental import pallas as pl
from jax.experimental.pallas import tpu as pltpu

D_MODEL = 1024
HEAD_DIM = 64
ATTN_SCALE = HEAD_DIM ** -0.5
A_HEADS = 8
A_LATENT = 128
IDX_HEADS = 8
IDX_DIM = 64
IDX_SCALE = (IDX_HEADS * IDX_DIM) ** -0.5
TOPK_MAX = 256
B_HEADS = 4
C_GROUPS = ((128, 1), (512, 4), (2048, 16))
C_HPG = 4
C_HEADS = C_HPG * len(C_GROUPS)
N_BUCKETS = 32
MAX_DISTANCE = 2048
N_BIAS_HEADS = A_HEADS + B_HEADS + C_HEADS
D_FF = -(-8 * D_MODEL // (3 * 256)) * 256
EPS = 1e-6

MM_DTYPE = jnp.bfloat16
F32 = jnp.float32
I32 = jnp.int32

ATT_TILE = 256
N_OFFSETS = 8
C_BAND = 128
ROW_TILE = 512
NEG = -1e30
INT_MIN = -2 ** 31
VMEM_LIMIT = 56 * 1024 * 1024


def _cparams(n_axes, vmem=VMEM_LIMIT):
    return pltpu.CompilerParams(dimension_semantics=("arbitrary",) * n_axes,
                                vmem_limit_bytes=vmem)


def _const_spec(shape):
    nd = len(shape)
    return pl.BlockSpec(shape, lambda *_: (0,) * nd, pipeline_mode=pl.Buffered(1))


def _bucket_thresholds(max_dist):
    n = np.arange(max_dist + 1)
    max_exact = N_BUCKETS // 2
    nf = np.maximum(n, 1).astype(np.float32)
    large = max_exact + (np.log(nf / np.float32(max_exact))
                         / np.float32(math.log(MAX_DISTANCE / max_exact))
                         * np.float32(N_BUCKETS - max_exact)).astype(np.int32)
    large = np.minimum(large, N_BUCKETS - 1)
    bucket = np.where(n < max_exact, n, large)
    assert np.all(np.diff(bucket) >= 0)
    thr = []
    for k in range(1, N_BUCKETS):
        idx = np.nonzero(bucket >= k)[0]
        thr.append(int(idx[0]) if idx.size else None)
    return thr


def _bias_from_dist(dist, tab_ref, col, thresholds):
    b = jnp.full(dist.shape, tab_ref[col], F32)
    for k, thr in enumerate(thresholds, start=1):
        if thr is None:
            break
        b = jnp.where(dist >= thr, tab_ref[k * N_BIAS_HEADS + col], b)
    return b


def _bias_att_kernel(tab_ref, out_ref, *, head0, tile, thresholds):
    h = pl.program_id(0)
    o = pl.program_id(1)
    row = lax.broadcasted_iota(I32, (tile, tile), 0)
    colq = lax.broadcasted_iota(I32, (tile, tile), 1)
    dist = jnp.maximum(o * tile + colq - row, 0)
    out_ref[...] = _bias_from_dist(dist, tab_ref, head0 + h, thresholds)


def _bias_att_tiles(tab, head0, n_heads, tile, thresholds):
    return pl.pallas_call(
        functools.partial(_bias_att_kernel, head0=head0, tile=tile, thresholds=thresholds),
        out_shape=jax.ShapeDtypeStruct((n_heads, N_OFFSETS, tile, tile), F32),
        grid=(n_heads, N_OFFSETS),
        in_specs=[pl.BlockSpec(memory_space=pltpu.SMEM)],
        out_specs=pl.BlockSpec((None, None, tile, tile), lambda h, o: (h, o, 0, 0)),
        compiler_params=_cparams(2),
        name="bias_att_tiles",
    )(tab)


def _bias_dil_kernel(tab_ref, out_ref, *, head0, thresholds):
    h = pl.program_id(0)
    g = h // C_HPG
    dil = jnp.where(g == 0, C_GROUPS[0][1], jnp.where(g == 1, C_GROUPS[1][1], C_GROUPS[2][1]))
    i = lax.broadcasted_iota(I32, (C_BAND, 2 * C_BAND), 0)
    j = lax.broadcasted_iota(I32, (C_BAND, 2 * C_BAND), 1)
    dist = jnp.maximum((i - j + C_BAND) * dil, 0)
    out_ref[...] = _bias_from_dist(dist, tab_ref, head0 + h, thresholds)


def _bias_dil_tiles(tab, head0, thresholds):
    return pl.pallas_call(
        functools.partial(_bias_dil_kernel, head0=head0, thresholds=thresholds),
        out_shape=jax.ShapeDtypeStruct((C_HEADS, C_BAND, 2 * C_BAND), F32),
        grid=(C_HEADS,),
        in_specs=[pl.BlockSpec(memory_space=pltpu.SMEM)],
        out_specs=pl.BlockSpec((None, C_BAND, 2 * C_BAND), lambda h: (h, 0, 0)),
        compiler_params=_cparams(1),
        name="bias_dil_tiles",
    )(tab)


def _mod_kernel(c_ref, w_ref, b_ref, out_ref):
    c = c_ref[...]
    ca = (c * jax.nn.sigmoid(c)).astype(MM_DTYPE)
    out_ref[...] = jnp.dot(ca, w_ref[...].astype(MM_DTYPE), preferred_element_type=F32) + b_ref[...]


def _modulation(c, w_ada, b_ada):
    depth, d, wid = w_ada.shape
    bsz = c.shape[0]
    tn = 1536
    assert wid % tn == 0
    return pl.pallas_call(
        _mod_kernel,
        out_shape=jax.ShapeDtypeStruct((depth, bsz, wid), F32),
        grid=(depth, wid // tn),
        in_specs=[pl.BlockSpec((bsz, d), lambda l, j: (0, 0)),
                  pl.BlockSpec((None, d, tn), lambda l, j: (l, 0, j)),
                  pl.BlockSpec((None, 1, tn), lambda l, j: (l, 0, j))],
        out_specs=pl.BlockSpec((None, bsz, tn), lambda l, j: (l, 0, j)),
        compiler_params=_cparams(2),
        name="adaln_modulation",
    )(c, w_ada, b_ada.reshape(depth, 1, wid))


def _rms(x):
    return x * lax.rsqrt(jnp.mean(x * x, axis=-1, keepdims=True) + EPS)


def _in_kernel(x_ref, mod_ref, g1_ref, gkv_ref,
               wT_aq, wT_iq, wT_iw, wT_bq, wT_bv, w_ik, w_kv, w_bk, w_c,
               o_aqT, o_iqT, o_iwT, o_bqT, o_bvT, o_ik, o_kv, o_kvT, o_bk, o_c0, o_c1, o_c2,
               c_scr):
    T = ATT_TILE
    tm = x_ref.shape[0]
    h = _rms(x_ref[...]) * g1_ref[...]
    h = h * (1.0 + mod_ref[1:2, :]) + mod_ref[0:1, :]
    hb = h.astype(MM_DTYPE)

    def mm(w):
        return jnp.dot(hb, w[...], preferred_element_type=F32)

    def mm_t(wT):
        return lax.dot_general(wT[...], hb, (((1,), (1,)), ((), ())), preferred_element_type=F32)

    o_aqT[...] = mm_t(wT_aq).astype(o_aqT.dtype)
    o_iqT[...] = mm_t(wT_iq).astype(o_iqT.dtype)
    o_bqT[...] = mm_t(wT_bq).astype(o_bqT.dtype)
    o_iwT[...] = (mm_t(wT_iw) * IDX_SCALE)[:IDX_HEADS]
    bvT = mm_t(wT_bv).astype(o_bvT.dtype)
    kv = _rms(mm(w_kv)) * gkv_ref[...]
    kvT = kv.T.astype(o_kvT.dtype)
    for j in range(tm // T):
        o_bvT[j] = bvT[:, j * T:(j + 1) * T]
        o_kvT[j] = kvT[:, j * T:(j + 1) * T]
    o_kv[...] = kv.astype(o_kv.dtype)
    o_ik[...] = mm(w_ik).astype(o_ik.dtype)
    o_bk[...] = mm(w_bk).astype(o_bk.dtype)

    yc = mm(w_c)
    n_chunk = yc.shape[1] // 128
    for j in range(n_chunk):
        c_scr[j] = yc[:, j * 128:(j + 1) * 128]
    per_group = n_chunk // len(C_GROUPS)
    for g, o_c in enumerate((o_c0, o_c1, o_c2)):
        dil = C_GROUPS[g][1]
        for r in range(dil):
            for jj in range(per_group):
                o_c[r, :, jj * 128:(jj + 1) * 128] = c_scr[
                    g * per_group + jj, pl.ds(r, tm // dil, stride=dil), :].astype(o_c.dtype)


def _in_proj(x2d, mod_l, g1, gkv, ws, bsz, seq):
    n, d = x2d.shape
    tm = ROW_TILE
    T = ATT_TILE
    per_b = seq // tm
    nk = seq // T
    hd = A_HEADS * HEAD_DIM
    bw = B_HEADS * 2 * HEAD_DIM
    gw = 3 * C_HPG * HEAD_DIM
    in_specs = [pl.BlockSpec((tm, d), lambda i: (i, 0)),
                pl.BlockSpec((None, 6, d), lambda i: (i // per_b, 0, 0)),
                _const_spec((1, d)), _const_spec((1, A_LATENT))]
    in_specs += [_const_spec(w.shape) for w in ws]

    def tspec(rows):
        return pl.BlockSpec((None, rows, tm), lambda i: (i // per_b, 0, i % per_b))

    def tile_tspec(rows):
        return pl.BlockSpec((None, tm // T, rows, T), lambda i: (i // per_b, i % per_b, 0, 0))

    def rspec(wd):
        return pl.BlockSpec((tm, wd), lambda i: (i, 0))

    def cspec(dil):
        return pl.BlockSpec((None, dil, tm // dil, gw), lambda i: (i // per_b, 0, i % per_b, 0))

    sds = jax.ShapeDtypeStruct
    out_specs = [tspec(hd), tspec(IDX_HEADS * IDX_DIM), tspec(IDX_HEADS), tspec(bw),
                 tile_tspec(bw), rspec(IDX_DIM), rspec(A_LATENT), tile_tspec(A_LATENT), rspec(bw)]
    out_shape = [sds((bsz, hd, seq), MM_DTYPE), sds((bsz, IDX_HEADS * IDX_DIM, seq), MM_DTYPE),
                 sds((bsz, IDX_HEADS, seq), F32), sds((bsz, bw, seq), MM_DTYPE),
                 sds((bsz, nk, bw, T), MM_DTYPE), sds((n, IDX_DIM), MM_DTYPE),
                 sds((n, A_LATENT), MM_DTYPE), sds((bsz, nk, A_LATENT, T), MM_DTYPE),
                 sds((n, bw), MM_DTYPE)]
    for _, dil in C_GROUPS:
        out_specs.append(cspec(dil))
        out_shape.append(sds((bsz, dil, seq // dil, gw), MM_DTYPE))
    return pl.pallas_call(
        _in_kernel, out_shape=out_shape, grid=(n // tm,),
        in_specs=in_specs, out_specs=out_specs,
        scratch_shapes=[pltpu.VMEM((3 * gw // 128, tm, 128), F32)],
        compiler_params=_cparams(1), name="in_proj",
    )(x2d, mod_l, g1, gkv, *ws)


def _softmax_step(s, m_old, l_old):
    m_new = jnp.maximum(m_old, jnp.max(s, axis=0, keepdims=True))
    p = jnp.exp(s - m_new)
    alpha = jnp.exp(m_old - m_new)
    l_new = alpha * l_old + jnp.sum(p, axis=0, keepdims=True)
    return p.astype(MM_DTYPE), alpha, m_new, l_new


def _bit_transpose32(words):
    a = list(words)
    j, mask = 16, 0x0000FFFF
    while j:
        k = 0
        while k < 32:
            t = (a[k] ^ lax.shift_right_logical(a[k + j], jnp.int32(j))) & jnp.int32(mask)
            a[k] = a[k] ^ t
            a[k + j] = a[k + j] ^ lax.shift_left(t, jnp.int32(j))
            k = (k + j + 1) & ~j
        j >>= 1
        mask = (mask ^ (mask << j)) & 0xFFFFFFFF
    return a


def _dsa_kernel(aqT_ref, iqT_ref, iwT_ref, kidx_ref, kv_ref, kvT_ref, bias_ref, wukT_ref, wuvT_ref,
                out_ref, keys_ref, planes_ref, qlat_ref, acc_ref, s_ref, oT_ref, *, topk, idx_bits):
    T = ATT_TILE
    qi = pl.program_id(1)
    nk = qi + 1

    for h in range(A_HEADS):
        q = jnp.dot(wukT_ref[h], aqT_ref[h * HEAD_DIM:(h + 1) * HEAD_DIM, :],
                    preferred_element_type=F32) * ATTN_SCALE
        qlat_ref[h] = q.astype(qlat_ref.dtype)

    s_loc = lax.broadcasted_iota(I32, (T, T), 0)
    t_loc = lax.broadcasted_iota(I32, (T, T), 1)

    def score_tile(kj, carry):
        kt = kidx_ref[kj]
        acc = jnp.zeros((T, T), F32)
        for h in range(IDX_HEADS):
            s = jnp.dot(kt, iqT_ref[h * IDX_DIM:(h + 1) * IDX_DIM, :], preferred_element_type=F32)
            acc = acc + jnp.maximum(s, 0.0) * iwT_ref[h:h + 1, :]
        bits = lax.bitcast_convert_type(acc, I32)
        key = jnp.where(bits < 0, bits ^ jnp.int32(0x7FFFFFFF), bits)
        visible = (kj * T + s_loc) <= (qi * T + t_loc)
        key = jnp.where(visible, key, jnp.int32(INT_MIN))
        keys_ref[kj] = key
        planes = _bit_transpose32([key[8 * i:8 * (i + 1), :] ^ jnp.int32(INT_MIN) for i in range(32)])
        rows = pl.ds(pl.multiple_of(kj * 8, 8), 8)
        for b in range(32):
            planes_ref[b, rows, :] = planes[b]
        return carry

    lax.fori_loop(0, nk, score_tile, 0)

    def clear_planes(kj, carry):
        rows = pl.ds(pl.multiple_of(kj * 8, 8), 8)
        for b in range(32):
            planes_ref[b, rows, :] = jnp.zeros((8, T), I32)
        return carry

    lax.fori_loop(nk, planes_ref.shape[1] // 8, clear_planes, 0)

    one = jnp.int32(1)
    nil = jnp.int32(0)
    zero = jnp.zeros((1, T), I32)

    n_rows = planes_ref.shape[1]
    tile_of_row = lax.broadcasted_iota(I32, (n_rows, T), 0) // 8
    tied0 = jnp.where(tile_of_row < nk, jnp.int32(-1), nil)

    def bit_step(b, carry):
        tied, n_gt, kth_u = carry
        ones = tied & planes_ref[b]
        n1 = jnp.sum(lax.population_count(ones), axis=0, keepdims=True)
        take = (n_gt + n1) >= topk
        tied = jnp.where(take, ones, tied ^ ones)
        n_gt = jnp.where(take, n_gt, n_gt + n1)
        kth_u = jnp.where(take, kth_u | lax.shift_left(one, jnp.int32(31) - b), kth_u)
        return tied, n_gt, kth_u

    tied, n_gt, kth_u = lax.fori_loop(0, 32, bit_step, (tied0, zero, zero))
    n_eq = jnp.sum(lax.population_count(tied), axis=0, keepdims=True)
    n_ge = jnp.where(kth_u == nil, n_gt, n_gt + n_eq)
    kth = jnp.maximum(kth_u ^ jnp.int32(INT_MIN), jnp.int32(INT_MIN + 1))

    def count(hit_fn):
        def body(kj, c):
            hit = hit_fn(keys_ref[kj], kj)
            return c + jnp.sum(hit.reshape(T // 8, 8, T), axis=0)
        c = lax.fori_loop(0, nk, body, jnp.zeros((8, T), I32))
        return jnp.sum(c, axis=0, keepdims=True)

    @pl.when(jnp.max(n_ge) > topk)
    def _():
        need = topk - n_gt

        def pos_step(i, cut):
            cand = cut | jnp.left_shift(one, jnp.int32(idx_bits - 1) - i)
            c = count(lambda kk, kj: jnp.where(
                kk == kth, jnp.where((kj * T + s_loc) < cand, one, nil), nil))
            return jnp.where(c < need, cand, cut)

        cut = lax.fori_loop(0, idx_bits, pos_step, zero)

        def demote(kj, carry):
            kk = keys_ref[kj]
            lowered = jnp.where((kj * T + s_loc) > cut, kth - one, kk)
            keys_ref[kj] = jnp.where(kk == kth, lowered, kk)
            return carry

        lax.fori_loop(0, nk, demote, 0)

    acc_ref[...] = jnp.zeros(acc_ref.shape, F32)

    def logits_to_scratch(kj):
        kvt = kv_ref[kj]
        for h in range(A_HEADS):
            s_ref[h] = jnp.dot(kvt, qlat_ref[h], preferred_element_type=F32)

    logits_to_scratch(0)

    def attend(kj, carry):
        ms, ls = carry
        sel = keys_ref[kj] >= kth
        kvTt = kvT_ref[kj]
        kv_next = kv_ref[jnp.minimum(kj + 1, nk - 1)]
        off = jnp.minimum(qi - kj, N_OFFSETS - 1)
        new_m, new_l = [], []
        for h in range(A_HEADS):
            s = jnp.where(sel, s_ref[h] + bias_ref[h, off], NEG)
            p, alpha, m_new, l_new = _softmax_step(s, ms[h], ls[h])
            s_ref[h] = jnp.dot(kv_next, qlat_ref[h], preferred_element_type=F32)
            acc_ref[h] = alpha * acc_ref[h] + jnp.dot(kvTt, p, preferred_element_type=F32)
            new_m.append(m_new)
            new_l.append(l_new)
        return tuple(new_m), tuple(new_l)

    init = (tuple(jnp.full((1, T), NEG, F32) for _ in range(A_HEADS)),
            tuple(jnp.zeros((1, T), F32) for _ in range(A_HEADS)))
    _, ls = lax.fori_loop(0, nk, attend, init)

    for h in range(A_HEADS):
        o_lat = (acc_ref[h] / ls[h]).astype(MM_DTYPE)
        oT_ref[h * HEAD_DIM:(h + 1) * HEAD_DIM, :] = jnp.dot(
            wuvT_ref[h], o_lat, preferred_element_type=F32)
    out_ref[...] = oT_ref[...].T.astype(out_ref.dtype)


def _dsa(aqT, iqT, iwT, kidx, kv, kvT, bias_a, wukT, wuvT):
    bsz, _, seq = aqT.shape
    T = ATT_TILE
    nk = seq // T
    topk = min(TOPK_MAX, seq // 4)
    idx_bits = int(math.log2(seq))
    assert 2 ** idx_bits == seq
    qspec = lambda rows: pl.BlockSpec((None, rows, T), lambda b, i: (b, 0, i))
    kspec = lambda a, c: pl.BlockSpec((None, nk, a, c), lambda b, i: (b, 0, 0, 0))
    return pl.pallas_call(
        functools.partial(_dsa_kernel, topk=topk, idx_bits=idx_bits),
        out_shape=jax.ShapeDtypeStruct((bsz, seq, A_HEADS * HEAD_DIM), MM_DTYPE),
        grid=(bsz, nk),
        in_specs=[qspec(A_HEADS * HEAD_DIM), qspec(IDX_HEADS * IDX_DIM), qspec(IDX_HEADS),
                  kspec(T, IDX_DIM), kspec(T, A_LATENT), kspec(A_LATENT, T),
                  _const_spec(bias_a.shape), _const_spec(wukT.shape), _const_spec(wuvT.shape)],
        out_specs=pl.BlockSpec((None, T, A_HEADS * HEAD_DIM), lambda b, i: (b, i, 0)),
        scratch_shapes=[pltpu.VMEM((nk, T, T), I32),
                        pltpu.VMEM((32, nk * (T // 32), T), I32),
                        pltpu.VMEM((A_HEADS, A_LATENT, T), MM_DTYPE),
                        pltpu.VMEM((A_HEADS, A_LATENT, T), F32),
                        pltpu.VMEM((A_HEADS, T, T), F32),
                        pltpu.VMEM((A_HEADS * HEAD_DIM, T), F32)],
        compiler_params=_cparams(2), name="dsa_attention",
    )(aqT, iqT, iwT, kidx, kv, kvT, bias_a, wukT, wuvT)


def _diff_kernel(qT_ref, k_ref, vT_ref, bias_ref, lam_ref, gsub_ref, out_ref,
                 qz_ref, acc_ref, s_ref, oT_ref, *, lam_init):
    T = ATT_TILE
    dv = 2 * HEAD_DIM
    n_chain = 2 * B_HEADS
    qi = pl.program_id(1)
    half = lax.broadcasted_iota(I32, (dv, T), 0) < HEAD_DIM
    for h in range(B_HEADS):
        q = qT_ref[h * dv:(h + 1) * dv, :].astype(F32) * ATTN_SCALE
        qz_ref[2 * h] = jnp.where(half, q, 0.0).astype(MM_DTYPE)
        qz_ref[2 * h + 1] = jnp.where(half, 0.0, q).astype(MM_DTYPE)
    s_loc = lax.broadcasted_iota(I32, (T, T), 0)
    t_loc = lax.broadcasted_iota(I32, (T, T), 1)
    acc_ref[...] = jnp.zeros(acc_ref.shape, F32)

    def logits(kj, c):
        h = c // 2
        return jnp.dot(k_ref[kj, :, h * dv:(h + 1) * dv], qz_ref[c], preferred_element_type=F32)

    for c in range(n_chain):
        s_ref[c] = logits(0, c)

    def step(kj, carry, diagonal):
        ms, ls = carry
        off = jnp.minimum(qi - kj, N_OFFSETS - 1)
        new_m, new_l = [], []
        for c in range(n_chain):
            h = c // 2
            s = s_ref[c] + bias_ref[h, off]
            if diagonal:
                s = jnp.where(s_loc <= t_loc, s, NEG)
            p, alpha, m_new, l_new = _softmax_step(s, ms[c], ls[c])
            if not diagonal:
                s_ref[c] = logits(kj + 1, c)
            acc_ref[c] = alpha * acc_ref[c] + jnp.dot(vT_ref[kj, h * dv:(h + 1) * dv, :], p,
                                                      preferred_element_type=F32)
            new_m.append(m_new)
            new_l.append(l_new)
        return tuple(new_m), tuple(new_l)

    init = (tuple(jnp.full((1, T), NEG, F32) for _ in range(n_chain)),
            tuple(jnp.zeros((1, T), F32) for _ in range(n_chain)))
    carry = lax.fori_loop(0, qi, lambda kj, cr: step(kj, cr, False), init)
    _, ls = step(qi, carry, True)

    lr = lam_ref[...]
    lam = (jnp.exp(jnp.sum(lr[0:1, :] * lr[1:2, :], axis=1, keepdims=True))
           - jnp.exp(jnp.sum(lr[2:3, :] * lr[3:4, :], axis=1, keepdims=True)) + lam_init)
    for h in range(B_HEADS):
        attn = acc_ref[2 * h] / ls[2 * h] - lam * (acc_ref[2 * h + 1] / ls[2 * h + 1])
        y = attn * lax.rsqrt(jnp.mean(attn * attn, axis=0, keepdims=True) + EPS)
        oT_ref[h * dv:(h + 1) * dv, :] = y * gsub_ref[...] * (1.0 - lam_init)
    out_ref[...] = oT_ref[...].T.astype(out_ref.dtype)


def _diff(bqT, bk, bvT, bias_b, lam_rows, gsub, lam_init):
    bsz, _, seq = bqT.shape
    T = ATT_TILE
    nk = seq // T
    dv = 2 * HEAD_DIM
    qspec = pl.BlockSpec((None, B_HEADS * dv, T), lambda b, i: (b, 0, i))
    return pl.pallas_call(
        functools.partial(_diff_kernel, lam_init=lam_init),
        out_shape=jax.ShapeDtypeStruct((bsz, seq, B_HEADS * dv), MM_DTYPE),
        grid=(bsz, nk),
        in_specs=[qspec,
                  pl.BlockSpec((None, nk, T, B_HEADS * dv), lambda b, i: (b, 0, 0, 0)),
                  pl.BlockSpec((None, nk, B_HEADS * dv, T), lambda b, i: (b, 0, 0, 0)),
                  _const_spec(bias_b.shape), _const_spec((4, HEAD_DIM)), _const_spec((dv, 1))],
        out_specs=pl.BlockSpec((None, T, B_HEADS * dv), lambda b, i: (b, i, 0)),
        scratch_shapes=[pltpu.VMEM((2 * B_HEADS, dv, T), MM_DTYPE),
                        pltpu.VMEM((2 * B_HEADS, dv, T), F32),
                        pltpu.VMEM((2 * B_HEADS, T, T), F32),
                        pltpu.VMEM((B_HEADS * dv, T), F32)],
        compiler_params=_cparams(2), name="diff_attention",
    )(bqT, bk, bvT, bias_b, lam_rows, gsub)


def _dil_kernel(cur_ref, halo_ref, bias_ref, out_ref, lse_ref, *, tq):
    n = C_BAND
    wid = C_HPG * HEAD_DIM
    halo_lo = jnp.where(pl.program_id(2) == 0, jnp.int32(n), jnp.int32(0))
    i = lax.broadcasted_iota(I32, (n, 2 * n), 0)
    j = lax.broadcasted_iota(I32, (n, 2 * n), 1)
    lane_head = lax.broadcasted_iota(I32, (n, wid), 1) // HEAD_DIM
    in_head = [lane_head == h for h in range(C_HPG)]
    band = jnp.where(j >= i, jnp.where(j <= i + n, 0.0, NEG), NEG)
    band0 = jnp.where(j >= jnp.maximum(i, halo_lo), jnp.where(j <= i + n, 0.0, NEG), NEG)
    bias = [bias_ref[h] + band for h in range(C_HPG)]
    bias0 = [bias_ref[h] + band0 for h in range(C_HPG)]

    def band_rows(c, lo, hi):
        if c == 0:
            return jnp.concatenate([halo_ref[:, lo:hi], cur_ref[0:n, lo:hi]], axis=0)
        return cur_ref[(c - 1) * n:(c + 1) * n, lo:hi]

    logits = []
    for c in range(tq // n):
        q = cur_ref[c * n:(c + 1) * n, 0:wid].astype(F32) * ATTN_SCALE
        keys = band_rows(c, wid, 2 * wid)
        for h in range(C_HPG):
            qh = jnp.where(in_head[h], q, 0.0).astype(MM_DTYPE)
            s = lax.dot_general(qh, keys, (((1,), (1,)), ((), ())), preferred_element_type=F32)
            logits.append(s + (bias0 if c == 0 else bias)[h])
    for c in range(tq // n):
        vals = band_rows(c, 2 * wid, 3 * wid)
        out = jnp.zeros((n, wid), F32)
        lse = jnp.zeros((n, wid), F32)
        for h in range(C_HPG):
            s = logits[c * C_HPG + h]
            m = jnp.max(s, axis=1, keepdims=True)
            p = jnp.exp(s - m)
            den = jnp.sum(p, axis=1, keepdims=True)
            o = jnp.dot(p.astype(MM_DTYPE), vals, preferred_element_type=F32) * (1.0 / den)
            out = jnp.where(in_head[h], o, out)
            lse = jnp.where(in_head[h], m + jnp.log(den), lse)
        out_ref[c * n:(c + 1) * n, :] = out
        lse_ref[c * n:(c + 1) * n, :] = lse


def _dilated_group(cg, bias_g, g):
    bsz, dil, m, gw = cg.shape
    wid = C_HPG * HEAD_DIM
    n = C_BAND
    assert m % n == 0 and gw == 3 * wid
    tq = min(m, 512)
    cur = pl.BlockSpec((None, None, tq, gw), lambda b, r, i: (b, r, i, 0))
    halo = pl.BlockSpec((None, None, n, gw),
                        lambda b, r, i: (b, r, jnp.maximum(i * (tq // n) - 1, 0), 0))
    outspec = pl.BlockSpec((None, None, tq, wid), lambda b, r, i: (b, r, i, 0))
    return pl.pallas_call(
        functools.partial(_dil_kernel, tq=tq),
        out_shape=[jax.ShapeDtypeStruct((bsz, dil, m, wid), F32)] * 2,
        grid=(bsz, dil, m // tq),
        in_specs=[cur, halo, pl.BlockSpec((C_HPG, n, 2 * n), lambda b, r, i: (0, 0, 0))],
        out_specs=[outspec, outspec],
        compiler_params=_cparams(3), name=f"dilated_group{g}",
    )(cg, cg, bias_g)


def _merge_kernel(x_ref, mod_ref, g1_ref, oa_ref, ob_ref,
                  c0_ref, c1_ref, c2_ref, s0_ref, s1_ref, s2_ref,
                  wza, wzb, wzc, wba, wbb, wbc, wo, out_ref, tok_ref):
    x = x_ref[...]
    tm = x.shape[0]
    h = _rms(x) * g1_ref[...]
    h = h * (1.0 + mod_ref[1:2, :]) + mod_ref[0:1, :]
    hb = h.astype(MM_DTYPE)

    def token_order(k, ref):
        dil, _, w = ref.shape
        if dil == 1:
            return ref[0]
        n_chunk = w // 128
        for r in range(dil):
            for j in range(n_chunk):
                tok_ref[k * n_chunk + j, pl.ds(r, tm // dil, stride=dil), :] = ref[
                    r, :, j * 128:(j + 1) * 128]
        return jnp.concatenate([tok_ref[k * n_chunk + j] for j in range(n_chunk)], axis=1)

    s0, s1, s2 = s0_ref[0], token_order(0, s1_ref), token_order(1, s2_ref)
    c0, c1, c2 = c0_ref[0], token_order(2, c1_ref), token_order(3, c2_ref)
    mx = jnp.maximum(jnp.maximum(s0, s1), s2)
    e0, e1, e2 = jnp.exp(s0 - mx), jnp.exp(s1 - mx), jnp.exp(s2 - mx)
    oc = (e0 * c0 + e1 * c1 + e2 * c2) / (e0 + e1 + e2)

    def gated(wz, o, wb):
        z = jnp.dot(hb, wz[...], preferred_element_type=F32)
        return jax.nn.sigmoid(z) * jnp.dot(o, wb[...], preferred_element_type=F32)

    merged = (gated(wza, oa_ref[...], wba) + gated(wzb, ob_ref[...], wbb)
              + gated(wzc, oc.astype(MM_DTYPE), wbc))
    y = jnp.dot(merged.astype(MM_DTYPE), wo[...], preferred_element_type=F32)
    out_ref[...] = x + mod_ref[2:3, :] * y


def _merge(x2d, mod_l, g1, oa, ob, ocs, lses, ws, seq):
    n, d = x2d.shape
    tm = ROW_TILE
    per_b = seq // tm
    row = lambda wd: pl.BlockSpec((tm, wd), lambda i: (i, 0))
    wid = C_HPG * HEAD_DIM
    res = [pl.BlockSpec((None, dil, tm // dil, wid), lambda i: (i // per_b, 0, i % per_b, 0))
           for _, dil in C_GROUPS]
    in_specs = [row(d), pl.BlockSpec((None, 6, d), lambda i: (i // per_b, 0, 0)), _const_spec((1, d)),
                row(oa.shape[1]), row(ob.shape[1])] + res + res
    in_specs += [_const_spec(w.shape) for w in ws]
    return pl.pallas_call(
        _merge_kernel, out_shape=jax.ShapeDtypeStruct((n, d), F32), grid=(n // tm,),
        in_specs=in_specs, out_specs=row(d),
        scratch_shapes=[pltpu.VMEM((4 * wid // 128, tm, 128), F32)],
        compiler_params=_cparams(1), name="gated_merge",
    )(x2d, mod_l, g1, oa, ob, *ocs, *lses, *ws)


def _ffn_kernel(x_ref, mod_ref, g2_ref, gf_ref, wgu, wd, out_ref, *, chunk, final_norm):
    x = x_ref[...]
    h = _rms(x) * g2_ref[...]
    h = h * (1.0 + mod_ref[4:5, :]) + mod_ref[3:4, :]
    hb = h.astype(MM_DTYPE)
    acc = jnp.zeros(x.shape, F32)
    for c in range(D_FF // chunk):
        fg = jnp.dot(hb, wgu[:, c * chunk:(c + 1) * chunk], preferred_element_type=F32)
        fu = jnp.dot(hb, wgu[:, D_FF + c * chunk:D_FF + (c + 1) * chunk],
                     preferred_element_type=F32)
        act = (fg * jax.nn.sigmoid(fg) * fu).astype(MM_DTYPE)
        acc = acc + jnp.dot(act, wd[c * chunk:(c + 1) * chunk, :], preferred_element_type=F32)
    y = x + mod_ref[5:6, :] * acc
    if final_norm:
        y = _rms(y) * gf_ref[...]
    out_ref[...] = y


def _ffn(x2d, mod_l, g2, gf, wgu, wd, seq, final_norm):
    n, d = x2d.shape
    tm = ROW_TILE
    per_b = seq // tm
    row = pl.BlockSpec((tm, d), lambda i: (i, 0))
    return pl.pallas_call(
        functools.partial(_ffn_kernel, chunk=256, final_norm=final_norm),
        out_shape=jax.ShapeDtypeStruct((n, d), F32), grid=(n // tm,),
        in_specs=[row, pl.BlockSpec((None, 6, d), lambda i: (i // per_b, 0, 0)),
                  _const_spec((1, d)), _const_spec((1, d)),
                  _const_spec(wgu.shape), _const_spec(wd.shape)],
        out_specs=row,
        compiler_params=_cparams(1), name="swiglu_ffn",
    )(x2d, mod_l, g2, gf, wgu, wd)


def kernel(x, c, w_ada, b_ada, g_norm1, w_in, w_uk, w_uv, g_kv, lam_q1, lam_k1, lam_q2, lam_k2,
           g_subln, w_branch_a, w_branch_b, w_branch_c, w_out, g_norm2, w_gate_up, w_down,
           rel_bias, g_final):
    bsz, seq, d = x.shape
    depth = w_ada.shape[0]
    T = ATT_TILE
    nk = seq // T
    assert d == D_MODEL and seq % T == 0 and seq % ROW_TILE == 0
    n = bsz * seq
    cast = lambda w: w.astype(MM_DTYPE)

    thresholds = _bucket_thresholds(seq + 2 * C_BAND * C_GROUPS[-1][1])
    assert seq <= N_OFFSETS * T or (N_OFFSETS - 2) * T + 1 >= thresholds[-1]
    tab = rel_bias.reshape(-1)
    bias_a = _bias_att_tiles(tab, 0, A_HEADS, T, thresholds)
    bias_b = _bias_att_tiles(tab, A_HEADS, B_HEADS, T, thresholds)
    bias_c = _bias_dil_tiles(tab, A_HEADS + B_HEADS, thresholds)

    mod = _modulation(c, w_ada, b_ada).reshape(depth, bsz, 6, d)

    splits = (A_HEADS * HEAD_DIM, A_LATENT, IDX_HEADS * IDX_DIM, IDX_DIM, IDX_HEADS,
              B_HEADS * 2 * HEAD_DIM, B_HEADS * 2 * HEAD_DIM, B_HEADS * 2 * HEAD_DIM,
              C_HEADS * HEAD_DIM, C_HEADS * HEAD_DIM, C_HEADS * HEAD_DIM, d, d, d)
    offs = np.concatenate([[0], np.cumsum(splits)])
    seg = lambda w, k: w[:, int(offs[k]):int(offs[k + 1])]

    x2d = x.reshape(n, d)
    for l in range(depth):
        wl = w_in[l]
        castT = lambda w: w.T.astype(MM_DTYPE)
        wT_iw = jnp.pad(seg(wl, 4).T, ((0, 16 - IDX_HEADS), (0, 0)))
        wid = C_HPG * HEAD_DIM
        w_c = jnp.concatenate([seg(wl, k)[:, g * wid:(g + 1) * wid]
                               for g in range(len(C_GROUPS)) for k in (8, 9, 10)], axis=1)
        ws_in = [castT(seg(wl, 0)), castT(seg(wl, 2)), cast(wT_iw), castT(seg(wl, 5)), castT(seg(wl, 7)),
                 cast(seg(wl, 3)), cast(seg(wl, 1)), cast(seg(wl, 6)), cast(w_c)]
        g1 = g_norm1[l].reshape(1, d)
        (aqT, iqT, iwT, bqT, bvT, ik, kv, kvT, bk, cg0, cg1, cg2) = _in_proj(
            x2d, mod[l], g1, g_kv[l].reshape(1, A_LATENT), ws_in, bsz, seq)

        o_a = _dsa(aqT, iqT, iwT,
                   ik.reshape(bsz, nk, T, IDX_DIM), kv.reshape(bsz, nk, T, A_LATENT), kvT,
                   bias_a, cast(w_uk[l].transpose(0, 2, 1)), cast(w_uv[l].transpose(0, 2, 1)))
        o_a = o_a.reshape(n, -1)

        lam_init = 0.8 - 0.6 * math.exp(-0.3 * l)
        lam_rows = jnp.stack([lam_q1[l], lam_k1[l], lam_q2[l], lam_k2[l]])
        dv = 2 * HEAD_DIM
        o_b = _diff(bqT, bk.reshape(bsz, nk, T, B_HEADS * dv), bvT, bias_b, lam_rows,
                    g_subln[l].reshape(dv, 1), lam_init)
        o_b = o_b.reshape(n, -1)

        ocs, lses = [], []
        for g, (cg, (window, dil)) in enumerate(zip((cg0, cg1, cg2), C_GROUPS)):
            assert window // dil == C_BAND
            o, s = _dilated_group(cg, bias_c[g * C_HPG:(g + 1) * C_HPG], g)
            ocs.append(o)
            lses.append(s)

        ws_merge = [cast(seg(wl, 11)), cast(seg(wl, 12)), cast(seg(wl, 13)),
                    cast(w_branch_a[l]), cast(w_branch_b[l]), cast(w_branch_c[l]), cast(w_out[l])]
        x2d = _merge(x2d, mod[l], g1, o_a, o_b, ocs, lses, ws_merge, seq)
        x2d = _ffn(x2d, mod[l], g_norm2[l].reshape(1, d), g_final.reshape(1, d),
                   cast(w_gate_up[l]), cast(w_down[l]), seq, final_norm=(l == depth - 1))
    return x2d.reshape(bsz, seq, d)
```

```python
import functools
import math

import numpy as np
import jax
import jax.numpy as jnp
from jax import lax
from jax.experimental import pallas as pl
from jax.experimental.pallas import tpu as pltpu

D_MODEL = 1024
HEAD_DIM = 64
ATTN_SCALE = HEAD_DIM ** -0.5
LOG2E = math.log2(math.e)
A_HEADS = 8
A_LATENT = 128
IDX_HEADS = 8
IDX_DIM = 64
IDX_SCALE = (IDX_HEADS * IDX_DIM) ** -0.5
TOPK_MAX = 256
B_HEADS = 4
C_GROUPS = ((128, 1), (512, 4), (2048, 16))
C_HPG = 4
C_HEADS = C_HPG * len(C_GROUPS)
N_BUCKETS = 32
MAX_DISTANCE = 2048
N_BIAS_HEADS = A_HEADS + B_HEADS + C_HEADS
D_FF = -(-8 * D_MODEL // (3 * 256)) * 256
EPS = 1e-6

MM_DTYPE = jnp.bfloat16
F32 = jnp.float32
I32 = jnp.int32

ATT_TILE = 256
N_OFFSETS = 8
C_BAND = 128
ROW_TILE = 512
NEG = -1e30
INT_MIN = -2 ** 31
VMEM_LIMIT = 56 * 1024 * 1024


def _cparams(n_axes, vmem=VMEM_LIMIT):
    return pltpu.CompilerParams(dimension_semantics=("arbitrary",) * n_axes,
                                vmem_limit_bytes=vmem)


def _const_spec(shape):
    nd = len(shape)
    return pl.BlockSpec(shape, lambda *_: (0,) * nd, pipeline_mode=pl.Buffered(1))


def _bucket_thresholds(max_dist):
    n = np.arange(max_dist + 1)
    max_exact = N_BUCKETS // 2
    nf = np.maximum(n, 1).astype(np.float32)
    large = max_exact + (np.log(nf / np.float32(max_exact))
                         / np.float32(math.log(MAX_DISTANCE / max_exact))
                         * np.float32(N_BUCKETS - max_exact)).astype(np.int32)
    large = np.minimum(large, N_BUCKETS - 1)
    bucket = np.where(n < max_exact, n, large)
    assert np.all(np.diff(bucket) >= 0)
    thr = []
    for k in range(1, N_BUCKETS):
        idx = np.nonzero(bucket >= k)[0]
        thr.append(int(idx[0]) if idx.size else None)
    return thr


def _bias_from_dist(dist, tab_ref, col, thresholds):
    b = jnp.full(dist.shape, tab_ref[col], F32)
    for k, thr in enumerate(thresholds, start=1):
        if thr is None:
            break
        b = jnp.where(dist >= thr, tab_ref[k * N_BIAS_HEADS + col], b)
    return b


def _bias_att_kernel(tab_ref, out_ref, *, head0, tile, thresholds):
    h = pl.program_id(0)
    o = pl.program_id(1)
    row = lax.broadcasted_iota(I32, (tile, tile), 0)
    colq = lax.broadcasted_iota(I32, (tile, tile), 1)
    dist = jnp.maximum(o * tile + colq - row, 0)
    out_ref[...] = _bias_from_dist(dist, tab_ref, head0 + h, thresholds) * LOG2E


def _bias_att_tiles(tab, head0, n_heads, tile, thresholds):
    return pl.pallas_call(
        functools.partial(_bias_att_kernel, head0=head0, tile=tile, thresholds=thresholds),
        out_shape=jax.ShapeDtypeStruct((n_heads, N_OFFSETS, tile, tile), F32),
        grid=(n_heads, N_OFFSETS),
        in_specs=[pl.BlockSpec(memory_space=pltpu.SMEM)],
        out_specs=pl.BlockSpec((None, None, tile, tile), lambda h, o: (h, o, 0, 0)),
        compiler_params=_cparams(2),
        name="bias_att_tiles",
    )(tab)


def _bias_dil_kernel(tab_ref, out_ref, *, head0, thresholds):
    h = pl.program_id(0)
    g = h // C_HPG
    dil = jnp.where(g == 0, C_GROUPS[0][1], jnp.where(g == 1, C_GROUPS[1][1], C_GROUPS[2][1]))
    i = lax.broadcasted_iota(I32, (C_BAND, 2 * C_BAND), 0)
    j = lax.broadcasted_iota(I32, (C_BAND, 2 * C_BAND), 1)
    dist = jnp.maximum((i - j + C_BAND) * dil, 0)
    out_ref[...] = _bias_from_dist(dist, tab_ref, head0 + h, thresholds)


def _bias_dil_tiles(tab, head0, thresholds):
    return pl.pallas_call(
        functools.partial(_bias_dil_kernel, head0=head0, thresholds=thresholds),
        out_shape=jax.ShapeDtypeStruct((C_HEADS, C_BAND, 2 * C_BAND), F32),
        grid=(C_HEADS,),
        in_specs=[pl.BlockSpec(memory_space=pltpu.SMEM)],
        out_specs=pl.BlockSpec((None, C_BAND, 2 * C_BAND), lambda h: (h, 0, 0)),
        compiler_params=_cparams(1),
        name="bias_dil_tiles",
    )(tab)


def _mod_kernel(c_ref, w_ref, b_ref, out_ref):
    c = c_ref[...]
    ca = (c * jax.nn.sigmoid(c)).astype(MM_DTYPE)
    out_ref[...] = jnp.dot(ca, w_ref[...].astype(MM_DTYPE), preferred_element_type=F32) + b_ref[...]


def _modulation(c, w_ada, b_ada):
    depth, d, wid = w_ada.shape
    bsz = c.shape[0]
    tn = 1536
    assert wid % tn == 0
    return pl.pallas_call(
        _mod_kernel,
        out_shape=jax.ShapeDtypeStruct((depth, bsz, wid), F32),
        grid=(depth, wid // tn),
        in_specs=[pl.BlockSpec((bsz, d), lambda l, j: (0, 0)),
                  pl.BlockSpec((None, d, tn), lambda l, j: (l, 0, j)),
                  pl.BlockSpec((None, 1, tn), lambda l, j: (l, 0, j))],
        out_specs=pl.BlockSpec((None, bsz, tn), lambda l, j: (l, 0, j)),
        compiler_params=_cparams(2),
        name="adaln_modulation",
    )(c, w_ada, b_ada.reshape(depth, 1, wid))


def _rms(x):
    return x * lax.rsqrt(jnp.mean(x * x, axis=-1, keepdims=True) + EPS)


def _in_kernel(x_ref, mod_ref, g1_ref, gkv_ref,
               wT_aq, wT_iq, wT_iw, wT_bq, wT_bv, w_ik, w_kv, w_bk, w_c,
               o_aqT, o_iqT, o_iwT, o_bqT, o_bvT, o_ik, o_kv, o_kvT, o_bk, o_c0, o_c1, o_c2,
               c_scr):
    T = ATT_TILE
    tm = x_ref.shape[0]
    h = _rms(x_ref[...]) * g1_ref[...]
    h = h * (1.0 + mod_ref[1:2, :]) + mod_ref[0:1, :]
    hb = h.astype(MM_DTYPE)

    def mm(w):
        return jnp.dot(hb, w[...], preferred_element_type=F32)

    def mm_t(wT):
        return lax.dot_general(wT[...], hb, (((1,), (1,)), ((), ())), preferred_element_type=F32)

    o_aqT[...] = mm_t(wT_aq).astype(o_aqT.dtype)
    o_iqT[...] = mm_t(wT_iq).astype(o_iqT.dtype)
    o_bqT[...] = (mm_t(wT_bq) * (ATTN_SCALE * LOG2E)).astype(o_bqT.dtype)
    o_iwT[...] = (mm_t(wT_iw) * IDX_SCALE)[:IDX_HEADS]
    bvT = mm_t(wT_bv).astype(o_bvT.dtype)
    kv = _rms(mm(w_kv)) * gkv_ref[...]
    kvT = kv.T.astype(o_kvT.dtype)
    for j in range(tm // T):
        o_bvT[j] = bvT[:, j * T:(j + 1) * T]
        o_kvT[j] = kvT[:, j * T:(j + 1) * T]
    o_kv[...] = kv.astype(o_kv.dtype)
    o_ik[...] = mm(w_ik).astype(o_ik.dtype)
    o_bk[...] = mm(w_bk).astype(o_bk.dtype)

    yc = mm(w_c)
    n_chunk = yc.shape[1] // 128
    for j in range(n_chunk):
        c_scr[j] = yc[:, j * 128:(j + 1) * 128]
    per_group = n_chunk // len(C_GROUPS)
    for g, o_c in enumerate((o_c0, o_c1, o_c2)):
        dil = C_GROUPS[g][1]
        for r in range(dil):
            for jj in range(per_group):
                o_c[r, :, jj * 128:(jj + 1) * 128] = c_scr[
                    g * per_group + jj, pl.ds(r, tm // dil, stride=dil), :].astype(o_c.dtype)


def _in_proj(x2d, mod_l, g1, gkv, ws, bsz, seq):
    n, d = x2d.shape
    tm = ROW_TILE
    T = ATT_TILE
    per_b = seq // tm
    nk = seq // T
    hd = A_HEADS * HEAD_DIM
    bw = B_HEADS * 2 * HEAD_DIM
    gw = 3 * C_HPG * HEAD_DIM
    in_specs = [pl.BlockSpec((tm, d), lambda i: (i, 0)),
                pl.BlockSpec((None, 6, d), lambda i: (i // per_b, 0, 0)),
                _const_spec((1, d)), _const_spec((1, A_LATENT))]
    in_specs += [_const_spec(w.shape) for w in ws]

    def tspec(rows):
        return pl.BlockSpec((None, rows, tm), lambda i: (i // per_b, 0, i % per_b))

    def tile_tspec(rows):
        return pl.BlockSpec((None, tm // T, rows, T), lambda i: (i // per_b, i % per_b, 0, 0))

    def rspec(wd):
        return pl.BlockSpec((tm, wd), lambda i: (i, 0))

    def cspec(dil):
        return pl.BlockSpec((None, dil, tm // dil, gw), lambda i: (i // per_b, 0, i % per_b, 0))

    sds = jax.ShapeDtypeStruct
    out_specs = [tspec(hd), tspec(IDX_HEADS * IDX_DIM), tspec(IDX_HEADS), tspec(bw),
                 tile_tspec(bw), rspec(IDX_DIM), rspec(A_LATENT), tile_tspec(A_LATENT), rspec(bw)]
    out_shape = [sds((bsz, hd, seq), MM_DTYPE), sds((bsz, IDX_HEADS * IDX_DIM, seq), MM_DTYPE),
                 sds((bsz, IDX_HEADS, seq), F32), sds((bsz, bw, seq), MM_DTYPE),
                 sds((bsz, nk, bw, T), MM_DTYPE), sds((n, IDX_DIM), MM_DTYPE),
                 sds((n, A_LATENT), MM_DTYPE), sds((bsz, nk, A_LATENT, T), MM_DTYPE),
                 sds((n, bw), MM_DTYPE)]
    for _, dil in C_GROUPS:
        out_specs.append(cspec(dil))
        out_shape.append(sds((bsz, dil, seq // dil, gw), MM_DTYPE))
    return pl.pallas_call(
        _in_kernel, out_shape=out_shape, grid=(n // tm,),
        in_specs=in_specs, out_specs=out_specs,
        scratch_shapes=[pltpu.VMEM((3 * gw // 128, tm, 128), F32)],
        compiler_params=_cparams(1), name="in_proj",
    )(x2d, mod_l, g1, gkv, *ws)


def _softmax_step(s, m_old):
    m_new = jnp.maximum(m_old, jnp.max(s, axis=0, keepdims=True))
    p = jnp.exp2(s - m_new)
    alpha = jnp.exp2(m_old - m_new)
    return p.astype(MM_DTYPE), alpha, m_new


def _initial_max(n_chains, tile):
    return tuple(jnp.full((1, tile), NEG, F32) for _ in range(n_chains))


SUM_ROWS = 16


def _with_ones_row(vT):
    row = lax.broadcasted_iota(I32, (SUM_ROWS, vT.shape[1]), 0)
    ones = jnp.where(row == 0, 1.0, 0.0).astype(vT.dtype)
    return jnp.concatenate([vT, ones], axis=0)


def _bit_transpose32(words):
    a = list(words)
    j, mask = 16, 0x0000FFFF
    while j:
        k = 0
        while k < 32:
            t = (a[k] ^ lax.shift_right_logical(a[k + j], jnp.int32(j))) & jnp.int32(mask)
            a[k] = a[k] ^ t
            a[k + j] = a[k + j] ^ lax.shift_left(t, jnp.int32(j))
            k = (k + j + 1) & ~j
        j >>= 1
        mask = (mask ^ (mask << j)) & 0xFFFFFFFF
    return a


def _dsa_kernel(aqT_ref, iqT_ref, iwT_ref, kidx_ref, kv_ref, kvT_ref, bias_ref, wukT_ref, wuvT_ref,
                out_ref, keys_ref, planes_ref, qlat_ref, acc_ref, s_ref, oT_ref, *, topk, idx_bits):
    T = ATT_TILE
    qi = pl.program_id(1)
    nk = qi + 1

    for h in range(A_HEADS):
        q = jnp.dot(wukT_ref[h], aqT_ref[h * HEAD_DIM:(h + 1) * HEAD_DIM, :],
                    preferred_element_type=F32) * (ATTN_SCALE * LOG2E)
        qlat_ref[h] = q.astype(qlat_ref.dtype)

    s_loc = lax.broadcasted_iota(I32, (T, T), 0)
    t_loc = lax.broadcasted_iota(I32, (T, T), 1)

    def score_tile(kj, carry):
        kt = kidx_ref[kj]
        acc = jnp.zeros((T, T), F32)
        for h in range(IDX_HEADS):
            s = jnp.dot(kt, iqT_ref[h * IDX_DIM:(h + 1) * IDX_DIM, :], preferred_element_type=F32)
            acc = acc + jnp.maximum(s, 0.0) * iwT_ref[h:h + 1, :]
        bits = lax.bitcast_convert_type(acc, I32)
        key = jnp.where(bits < 0, bits ^ jnp.int32(0x7FFFFFFF), bits)
        visible = (kj * T + s_loc) <= (qi * T + t_loc)
        key = jnp.where(visible, key, jnp.int32(INT_MIN))
        keys_ref[kj] = key
        planes = _bit_transpose32([key[8 * i:8 * (i + 1), :] ^ jnp.int32(INT_MIN) for i in range(32)])
        rows = pl.ds(pl.multiple_of(kj * 8, 8), 8)
        for b in range(32):
            planes_ref[b, rows, :] = planes[b]
        return carry

    lax.fori_loop(0, nk, score_tile, 0)

    def clear_planes(kj, carry):
        rows = pl.ds(pl.multiple_of(kj * 8, 8), 8)
        for b in range(32):
            planes_ref[b, rows, :] = jnp.zeros((8, T), I32)
        return carry

    lax.fori_loop(nk, planes_ref.shape[1] // 8, clear_planes, 0)

    one = jnp.int32(1)
    nil = jnp.int32(0)
    zero = jnp.zeros((1, T), I32)

    n_rows = planes_ref.shape[1]
    tile_of_row = lax.broadcasted_iota(I32, (n_rows, T), 0) // 8
    tied0 = jnp.where(tile_of_row < nk, jnp.int32(-1), nil)

    def bit_step(b, carry):
        tied, n_gt, kth_u = carry
        ones = tied & planes_ref[b]
        n1 = jnp.sum(lax.population_count(ones), axis=0, keepdims=True)
        take = (n_gt + n1) >= topk
        tied = jnp.where(take, ones, tied ^ ones)
        n_gt = jnp.where(take, n_gt, n_gt + n1)
        kth_u = jnp.where(take, kth_u | lax.shift_left(one, jnp.int32(31) - b), kth_u)
        return tied, n_gt, kth_u

    tied, n_gt, kth_u = lax.fori_loop(0, 32, bit_step, (tied0, zero, zero))
    n_eq = jnp.sum(lax.population_count(tied), axis=0, keepdims=True)
    n_ge = jnp.where(kth_u == nil, n_gt, n_gt + n_eq)
    kth = jnp.maximum(kth_u ^ jnp.int32(INT_MIN), jnp.int32(INT_MIN + 1))

    def count(hit_fn):
        def body(kj, c):
            hit = hit_fn(keys_ref[kj], kj)
            return c + jnp.sum(hit.reshape(T // 8, 8, T), axis=0)
        c = lax.fori_loop(0, nk, body, jnp.zeros((8, T), I32))
        return jnp.sum(c, axis=0, keepdims=True)

    @pl.when(jnp.max(n_ge) > topk)
    def _():
        need = topk - n_gt

        def pos_step(i, cut):
            cand = cut | jnp.left_shift(one, jnp.int32(idx_bits - 1) - i)
            c = count(lambda kk, kj: jnp.where(
                kk == kth, jnp.where((kj * T + s_loc) < cand, one, nil), nil))
            return jnp.where(c < need, cand, cut)

        cut = lax.fori_loop(0, idx_bits, pos_step, zero)

        def demote(kj, carry):
            kk = keys_ref[kj]
            lowered = jnp.where((kj * T + s_loc) > cut, kth - one, kk)
            keys_ref[kj] = jnp.where(kk == kth, lowered, kk)
            return carry

        lax.fori_loop(0, nk, demote, 0)

    acc_ref[...] = jnp.zeros(acc_ref.shape, F32)

    def logits_to_scratch(kj):
        kvt = kv_ref[kj]
        for h in range(A_HEADS):
            s_ref[h] = jnp.dot(kvt, qlat_ref[h], preferred_element_type=F32)

    logits_to_scratch(0)

    def attend(kj, ms):
        sel = keys_ref[kj] >= kth
        kvTt = _with_ones_row(kvT_ref[kj])
        kv_next = kv_ref[jnp.minimum(kj + 1, nk - 1)]
        off = jnp.minimum(qi - kj, N_OFFSETS - 1)
        new_m = []
        for h in range(A_HEADS):
            s = jnp.where(sel, s_ref[h] + bias_ref[h, off], NEG)
            p, alpha, m_new = _softmax_step(s, ms[h])
            s_ref[h] = jnp.dot(kv_next, qlat_ref[h], preferred_element_type=F32)
            acc_ref[h] = alpha * acc_ref[h] + jnp.dot(kvTt, p, preferred_element_type=F32)
            new_m.append(m_new)
        return tuple(new_m)

    lax.fori_loop(0, nk, attend, _initial_max(A_HEADS, T))

    for h in range(A_HEADS):
        o_lat = (acc_ref[h, 0:A_LATENT, :] / acc_ref[h, A_LATENT:A_LATENT + 1, :]).astype(MM_DTYPE)
        oT_ref[h * HEAD_DIM:(h + 1) * HEAD_DIM, :] = jnp.dot(
            wuvT_ref[h], o_lat, preferred_element_type=F32)
    out_ref[...] = oT_ref[...].T.astype(out_ref.dtype)


def _dsa(aqT, iqT, iwT, kidx, kv, kvT, bias_a, wukT, wuvT):
    bsz, _, seq = aqT.shape
    T = ATT_TILE
    nk = seq // T
    topk = min(TOPK_MAX, seq // 4)
    idx_bits = int(math.log2(seq))
    assert 2 ** idx_bits == seq
    qspec = lambda rows: pl.BlockSpec((None, rows, T), lambda b, i: (b, 0, i))
    kspec = lambda a, c: pl.BlockSpec((None, nk, a, c), lambda b, i: (b, 0, 0, 0))
    return pl.pallas_call(
        functools.partial(_dsa_kernel, topk=topk, idx_bits=idx_bits),
        out_shape=jax.ShapeDtypeStruct((bsz, seq, A_HEADS * HEAD_DIM), MM_DTYPE),
        grid=(bsz, nk),
        in_specs=[qspec(A_HEADS * HEAD_DIM), qspec(IDX_HEADS * IDX_DIM), qspec(IDX_HEADS),
                  kspec(T, IDX_DIM), kspec(T, A_LATENT), kspec(A_LATENT, T),
                  _const_spec(bias_a.shape), _const_spec(wukT.shape), _const_spec(wuvT.shape)],
        out_specs=pl.BlockSpec((None, T, A_HEADS * HEAD_DIM), lambda b, i: (b, i, 0)),
        scratch_shapes=[pltpu.VMEM((nk, T, T), I32),
                        pltpu.VMEM((32, nk * (T // 32), T), I32),
                        pltpu.VMEM((A_HEADS, A_LATENT, T), MM_DTYPE),
                        pltpu.VMEM((A_HEADS, A_LATENT + SUM_ROWS, T), F32),
                        pltpu.VMEM((A_HEADS, T, T), F32),
                        pltpu.VMEM((A_HEADS * HEAD_DIM, T), F32)],
        compiler_params=_cparams(2), name="dsa_attention",
    )(aqT, iqT, iwT, kidx, kv, kvT, bias_a, wukT, wuvT)


def _diff_kernel(qT_ref, k_ref, vT_ref, bias_ref, lam_ref, gsub_ref, out_ref,
                 qz_ref, acc_ref, s_ref, oT_ref, *, lam_init):
    T = ATT_TILE
    dv = 2 * HEAD_DIM
    n_chain = 2 * B_HEADS
    qi = pl.program_id(1)
    half = lax.broadcasted_iota(I32, (dv, T), 0) < HEAD_DIM
    for h in range(B_HEADS):
        q = qT_ref[h * dv:(h + 1) * dv, :].astype(F32)
        qz_ref[2 * h] = jnp.where(half, q, 0.0).astype(MM_DTYPE)
        qz_ref[2 * h + 1] = jnp.where(half, 0.0, q).astype(MM_DTYPE)
    s_loc = lax.broadcasted_iota(I32, (T, T), 0)
    t_loc = lax.broadcasted_iota(I32, (T, T), 1)
    acc_ref[...] = jnp.zeros(acc_ref.shape, F32)

    def logits(kj, c):
        h = c // 2
        return jnp.dot(k_ref[kj, :, h * dv:(h + 1) * dv], qz_ref[c], preferred_element_type=F32)

    for c in range(n_chain):
        s_ref[c] = logits(0, c)

    def step(kj, ms, diagonal):
        off = jnp.minimum(qi - kj, N_OFFSETS - 1)
        new_m = []
        for c in range(n_chain):
            h = c // 2
            s = s_ref[c] + bias_ref[h, off]
            if diagonal:
                s = jnp.where(s_loc <= t_loc, s, NEG)
            p, alpha, m_new = _softmax_step(s, ms[c])
            if not diagonal:
                s_ref[c] = logits(kj + 1, c)
            vT = _with_ones_row(vT_ref[kj, h * dv:(h + 1) * dv, :])
            acc_ref[c] = alpha * acc_ref[c] + jnp.dot(vT, p, preferred_element_type=F32)
            new_m.append(m_new)
        return tuple(new_m)

    ms = lax.fori_loop(0, qi, lambda kj, cr: step(kj, cr, False), _initial_max(n_chain, T))
    step(qi, ms, True)

    lr = lam_ref[...]
    lam = (jnp.exp(jnp.sum(lr[0:1, :] * lr[1:2, :], axis=1, keepdims=True))
           - jnp.exp(jnp.sum(lr[2:3, :] * lr[3:4, :], axis=1, keepdims=True)) + lam_init)

    def normalised(c):
        return acc_ref[c, 0:dv, :] / acc_ref[c, dv:dv + 1, :]

    for h in range(B_HEADS):
        attn = normalised(2 * h) - lam * normalised(2 * h + 1)
        y = attn * lax.rsqrt(jnp.mean(attn * attn, axis=0, keepdims=True) + EPS)
        oT_ref[h * dv:(h + 1) * dv, :] = y * gsub_ref[...] * (1.0 - lam_init)
    out_ref[...] = oT_ref[...].T.astype(out_ref.dtype)


def _diff(bqT, bk, bvT, bias_b, lam_rows, gsub, lam_init):
    bsz, _, seq = bqT.shape
    T = ATT_TILE
    nk = seq // T
    dv = 2 * HEAD_DIM
    qspec = pl.BlockSpec((None, B_HEADS * dv, T), lambda b, i: (b, 0, i))
    return pl.pallas_call(
        functools.partial(_diff_kernel, lam_init=lam_init),
        out_shape=jax.ShapeDtypeStruct((bsz, seq, B_HEADS * dv), MM_DTYPE),
        grid=(bsz, nk),
        in_specs=[qspec,
                  pl.BlockSpec((None, nk, T, B_HEADS * dv), lambda b, i: (b, 0, 0, 0)),
                  pl.BlockSpec((None, nk, B_HEADS * dv, T), lambda b, i: (b, 0, 0, 0)),
                  _const_spec(bias_b.shape), _const_spec((4, HEAD_DIM)), _const_spec((dv, 1))],
        out_specs=pl.BlockSpec((None, T, B_HEADS * dv), lambda b, i: (b, i, 0)),
        scratch_shapes=[pltpu.VMEM((2 * B_HEADS, dv, T), MM_DTYPE),
                        pltpu.VMEM((2 * B_HEADS, dv + SUM_ROWS, T), F32),
                        pltpu.VMEM((2 * B_HEADS, T, T), F32),
                        pltpu.VMEM((B_HEADS * dv, T), F32)],
        compiler_params=_cparams(2), name="diff_attention",
    )(bqT, bk, bvT, bias_b, lam_rows, gsub)


def _dil_kernel(cur_ref, halo_ref, bias_ref, out_ref, lse_ref, *, tq):
    n = C_BAND
    wid = C_HPG * HEAD_DIM
    halo_lo = jnp.where(pl.program_id(2) == 0, jnp.int32(n), jnp.int32(0))
    i = lax.broadcasted_iota(I32, (n, 2 * n), 0)
    j = lax.broadcasted_iota(I32, (n, 2 * n), 1)
    lane_head = lax.broadcasted_iota(I32, (n, wid), 1) // HEAD_DIM
    in_head = [lane_head == h for h in range(C_HPG)]
    band = jnp.where(j >= i, jnp.where(j <= i + n, 0.0, NEG), NEG)
    band0 = jnp.where(j >= jnp.maximum(i, halo_lo), jnp.where(j <= i + n, 0.0, NEG), NEG)
    bias = [bias_ref[h] + band for h in range(C_HPG)]
    bias0 = [bias_ref[h] + band0 for h in range(C_HPG)]

    def band_rows(c, lo, hi):
        if c == 0:
            return jnp.concatenate([halo_ref[:, lo:hi], cur_ref[0:n, lo:hi]], axis=0)
        return cur_ref[(c - 1) * n:(c + 1) * n, lo:hi]

    logits = []
    for c in range(tq // n):
        q = cur_ref[c * n:(c + 1) * n, 0:wid].astype(F32) * ATTN_SCALE
        keys = band_rows(c, wid, 2 * wid)
        for h in range(C_HPG):
            qh = jnp.where(in_head[h], q, 0.0).astype(MM_DTYPE)
            s = lax.dot_general(qh, keys, (((1,), (1,)), ((), ())), preferred_element_type=F32)
            logits.append(s + (bias0 if c == 0 else bias)[h])
    for c in range(tq // n):
        vals = band_rows(c, 2 * wid, 3 * wid)
        out = jnp.zeros((n, wid), F32)
        lse = jnp.zeros((n, wid), F32)
        for h in range(C_HPG):
            s = logits[c * C_HPG + h]
            m = jnp.max(s, axis=1, keepdims=True)
            p = jnp.exp(s - m)
            den = jnp.sum(p, axis=1, keepdims=True)
            o = jnp.dot(p.astype(MM_DTYPE), vals, preferred_element_type=F32) * (1.0 / den)
            out = jnp.where(in_head[h], o, out)
            lse = jnp.where(in_head[h], m + jnp.log(den), lse)
        out_ref[c * n:(c + 1) * n, :] = out
        lse_ref[c * n:(c + 1) * n, :] = lse


def _dilated_group(cg, bias_g, g):
    bsz, dil, m, gw = cg.shape
    wid = C_HPG * HEAD_DIM
    n = C_BAND
    assert m % n == 0 and gw == 3 * wid
    tq = min(m, 512)
    cur = pl.BlockSpec((None, None, tq, gw), lambda b, r, i: (b, r, i, 0))
    halo = pl.BlockSpec((None, None, n, gw),
                        lambda b, r, i: (b, r, jnp.maximum(i * (tq // n) - 1, 0), 0))
    outspec = pl.BlockSpec((None, None, tq, wid), lambda b, r, i: (b, r, i, 0))
    return pl.pallas_call(
        functools.partial(_dil_kernel, tq=tq),
        out_shape=[jax.ShapeDtypeStruct((bsz, dil, m, wid), F32)] * 2,
        grid=(bsz, dil, m // tq),
        in_specs=[cur, halo, pl.BlockSpec((C_HPG, n, 2 * n), lambda b, r, i: (0, 0, 0))],
        out_specs=[outspec, outspec],
        compiler_params=_cparams(3), name=f"dilated_group{g}",
    )(cg, cg, bias_g)


def _merge_kernel(x_ref, mod_ref, g1_ref, oa_ref, ob_ref,
                  c0_ref, c1_ref, c2_ref, s0_ref, s1_ref, s2_ref,
                  wza, wzb, wzc, wba, wbb, wbc, wo, out_ref, tok_ref):
    x = x_ref[...]
    tm = x.shape[0]
    h = _rms(x) * g1_ref[...]
    h = h * (1.0 + mod_ref[1:2, :]) + mod_ref[0:1, :]
    hb = h.astype(MM_DTYPE)

    def token_order(k, ref):
        dil, _, w = ref.shape
        if dil == 1:
            return ref[0]
        n_chunk = w // 128
        for r in range(dil):
            for j in range(n_chunk):
                tok_ref[k * n_chunk + j, pl.ds(r, tm // dil, stride=dil), :] = ref[
                    r, :, j * 128:(j + 1) * 128]
        return jnp.concatenate([tok_ref[k * n_chunk + j] for j in range(n_chunk)], axis=1)

    s0, s1, s2 = s0_ref[0], token_order(0, s1_ref), token_order(1, s2_ref)
    c0, c1, c2 = c0_ref[0], token_order(2, c1_ref), token_order(3, c2_ref)
    mx = jnp.maximum(jnp.maximum(s0, s1), s2)
    e0, e1, e2 = jnp.exp(s0 - mx), jnp.exp(s1 - mx), jnp.exp(s2 - mx)
    oc = (e0 * c0 + e1 * c1 + e2 * c2) / (e0 + e1 + e2)

    def gated(wz, o, wb):
        z = jnp.dot(hb, wz[...], preferred_element_type=F32)
        return jax.nn.sigmoid(z) * jnp.dot(o, wb[...], preferred_element_type=F32)

    merged = (gated(wza, oa_ref[...], wba) + gated(wzb, ob_ref[...], wbb)
              + gated(wzc, oc.astype(MM_DTYPE), wbc))
    y = jnp.dot(merged.astype(MM_DTYPE), wo[...], preferred_element_type=F32)
    out_ref[...] = x + mod_ref[2:3, :] * y


def _merge(x2d, mod_l, g1, oa, ob, ocs, lses, ws, seq):
    n, d = x2d.shape
    tm = ROW_TILE
    per_b = seq // tm
    row = lambda wd: pl.BlockSpec((tm, wd), lambda i: (i, 0))
    wid = C_HPG * HEAD_DIM
    res = [pl.BlockSpec((None, dil, tm // dil, wid), lambda i: (i // per_b, 0, i % per_b, 0))
           for _, dil in C_GROUPS]
    in_specs = [row(d), pl.BlockSpec((None, 6, d), lambda i: (i // per_b, 0, 0)), _const_spec((1, d)),
                row(oa.shape[1]), row(ob.shape[1])] + res + res
    in_specs += [_const_spec(w.shape) for w in ws]
    return pl.pallas_call(
        _merge_kernel, out_shape=jax.ShapeDtypeStruct((n, d), F32), grid=(n // tm,),
        in_specs=in_specs, out_specs=row(d),
        scratch_shapes=[pltpu.VMEM((4 * wid // 128, tm, 128), F32)],
        compiler_params=_cparams(1), name="gated_merge",
    )(x2d, mod_l, g1, oa, ob, *ocs, *lses, *ws)


def _ffn_kernel(x_ref, mod_ref, g2_ref, gf_ref, wgu, wd, out_ref, *, chunk, final_norm):
    x = x_ref[...]
    h = _rms(x) * g2_ref[...]
    h = h * (1.0 + mod_ref[4:5, :]) + mod_ref[3:4, :]
    hb = h.astype(MM_DTYPE)
    acc = jnp.zeros(x.shape, F32)
    for c in range(D_FF // chunk):
        fg = jnp.dot(hb, wgu[:, c * chunk:(c + 1) * chunk], preferred_element_type=F32)
        fu = jnp.dot(hb, wgu[:, D_FF + c * chunk:D_FF + (c + 1) * chunk],
                     preferred_element_type=F32)
        act = (fg * jax.nn.sigmoid(fg) * fu).astype(MM_DTYPE)
        acc = acc + jnp.dot(act, wd[c * chunk:(c + 1) * chunk, :], preferred_element_type=F32)
    y = x + mod_ref[5:6, :] * acc
    if final_norm:
        y = _rms(y) * gf_ref[...]
    out_ref[...] = y


def _ffn(x2d, mod_l, g2, gf, wgu, wd, seq, final_norm):
    n, d = x2d.shape
    tm = ROW_TILE
    per_b = seq // tm
    row = pl.BlockSpec((tm, d), lambda i: (i, 0))
    return pl.pallas_call(
        functools.partial(_ffn_kernel, chunk=256, final_norm=final_norm),
        out_shape=jax.ShapeDtypeStruct((n, d), F32), grid=(n // tm,),
        in_specs=[row, pl.BlockSpec((None, 6, d), lambda i: (i // per_b, 0, 0)),
                  _const_spec((1, d)), _const_spec((1, d)),
                  _const_spec(wgu.shape), _const_spec(wd.shape)],
        out_specs=row,
        compiler_params=_cparams(1), name="swiglu_ffn",
    )(x2d, mod_l, g2, gf, wgu, wd)


def kernel(x, c, w_ada, b_ada, g_norm1, w_in, w_uk, w_uv, g_kv, lam_q1, lam_k1, lam_q2, lam_k2,
           g_subln, w_branch_a, w_branch_b, w_branch_c, w_out, g_norm2, w_gate_up, w_down,
           rel_bias, g_final):
    bsz, seq, d = x.shape
    depth = w_ada.shape[0]
    T = ATT_TILE
    nk = seq // T
    assert d == D_MODEL and seq % T == 0 and seq % ROW_TILE == 0
    n = bsz * seq
    cast = lambda w: w.astype(MM_DTYPE)

    thresholds = _bucket_thresholds(seq + 2 * C_BAND * C_GROUPS[-1][1])
    assert seq <= N_OFFSETS * T or (N_OFFSETS - 2) * T + 1 >= thresholds[-1]
    tab = rel_bias.reshape(-1)
    bias_a = _bias_att_tiles(tab, 0, A_HEADS, T, thresholds)
    bias_b = _bias_att_tiles(tab, A_HEADS, B_HEADS, T, thresholds)
    bias_c = _bias_dil_tiles(tab, A_HEADS + B_HEADS, thresholds)

    mod = _modulation(c, w_ada, b_ada).reshape(depth, bsz, 6, d)

    splits = (A_HEADS * HEAD_DIM, A_LATENT, IDX_HEADS * IDX_DIM, IDX_DIM, IDX_HEADS,
              B_HEADS * 2 * HEAD_DIM, B_HEADS * 2 * HEAD_DIM, B_HEADS * 2 * HEAD_DIM,
              C_HEADS * HEAD_DIM, C_HEADS * HEAD_DIM, C_HEADS * HEAD_DIM, d, d, d)
    offs = np.concatenate([[0], np.cumsum(splits)])
    seg = lambda w, k: w[:, int(offs[k]):int(offs[k + 1])]

    x2d = x.reshape(n, d)
    for l in range(depth):
        wl = w_in[l]
        castT = lambda w: w.T.astype(MM_DTYPE)
        wT_iw = jnp.pad(seg(wl, 4).T, ((0, 16 - IDX_HEADS), (0, 0)))
        wid = C_HPG * HEAD_DIM
        w_c = jnp.concatenate([seg(wl, k)[:, g * wid:(g + 1) * wid]
                               for g in range(len(C_GROUPS)) for k in (8, 9, 10)], axis=1)
        ws_in = [castT(seg(wl, 0)), castT(seg(wl, 2)), cast(wT_iw), castT(seg(wl, 5)), castT(seg(wl, 7)),
                 cast(seg(wl, 3)), cast(seg(wl, 1)), cast(seg(wl, 6)), cast(w_c)]
        g1 = g_norm1[l].reshape(1, d)
        (aqT, iqT, iwT, bqT, bvT, ik, kv, kvT, bk, cg0, cg1, cg2) = _in_proj(
            x2d, mod[l], g1, g_kv[l].reshape(1, A_LATENT), ws_in, bsz, seq)

        o_a = _dsa(aqT, iqT, iwT,
                   ik.reshape(bsz, nk, T, IDX_DIM), kv.reshape(bsz, nk, T, A_LATENT), kvT,
                   bias_a, cast(w_uk[l].transpose(0, 2, 1)), cast(w_uv[l].transpose(0, 2, 1)))
        o_a = o_a.reshape(n, -1)

        lam_init = 0.8 - 0.6 * math.exp(-0.3 * l)
        lam_rows = jnp.stack([lam_q1[l], lam_k1[l], lam_q2[l], lam_k2[l]])
        dv = 2 * HEAD_DIM
        o_b = _diff(bqT, bk.reshape(bsz, nk, T, B_HEADS * dv), bvT, bias_b, lam_rows,
                    g_subln[l].reshape(dv, 1), lam_init)
        o_b = o_b.reshape(n, -1)

        ocs, lses = [], []
        for g, (cg, (window, dil)) in enumerate(zip((cg0, cg1, cg2), C_GROUPS)):
            assert window // dil == C_BAND
            o, s = _dilated_group(cg, bias_c[g * C_HPG:(g + 1) * C_HPG], g)
            ocs.append(o)
            lses.append(s)

        ws_merge = [cast(seg(wl, 11)), cast(seg(wl, 12)), cast(seg(wl, 13)),
                    cast(w_branch_a[l]), cast(w_branch_b[l]), cast(w_branch_c[l]), cast(w_out[l])]
        x2d = _merge(x2d, mod[l], g1, o_a, o_b, ocs, lses, ws_merge, seq)
        x2d = _ffn(x2d, mod[l], g_norm2[l].reshape(1, d), g_final.reshape(1, d),
                   cast(w_gate_up[l]), cast(w_down[l]), seq, final_norm=(l == depth - 1))
    return x2d.reshape(bsz, seq, d)
```

```python
import functools
import math

import numpy as np
import jax
import jax.numpy as jnp
from jax import lax
from jax.experimental import pallas as pl
from jax.experimental.pallas import tpu as pltpu

D_MODEL = 1024
HEAD_DIM = 64
ATTN_SCALE = HEAD_DIM ** -0.5
LOG2E = math.log2(math.e)
A_HEADS = 8
A_LATENT = 128
IDX_HEADS = 8
IDX_DIM = 64
IDX_SCALE = (IDX_HEADS * IDX_DIM) ** -0.5
TOPK_MAX = 256
B_HEADS = 4
C_GROUPS = ((128, 1), (512, 4), (2048, 16))
C_HPG = 4
C_HEADS = C_HPG * len(C_GROUPS)
N_BUCKETS = 32
MAX_DISTANCE = 2048
N_BIAS_HEADS = A_HEADS + B_HEADS + C_HEADS
D_FF = -(-8 * D_MODEL // (3 * 256)) * 256
EPS = 1e-6

MM_DTYPE = jnp.bfloat16
F32 = jnp.float32
I32 = jnp.int32

ATT_TILE = 256
N_OFFSETS = 8
C_BAND = 128
ROW_TILE = 512
NEG = -1e30
INT_MIN = -2 ** 31
VMEM_LIMIT = 56 * 1024 * 1024


def _cparams(n_axes, vmem=VMEM_LIMIT):
    return pltpu.CompilerParams(dimension_semantics=("arbitrary",) * n_axes,
                                vmem_limit_bytes=vmem)


def _const_spec(shape):
    nd = len(shape)
    return pl.BlockSpec(shape, lambda *_: (0,) * nd, pipeline_mode=pl.Buffered(1))


def _bucket_thresholds(max_dist):
    n = np.arange(max_dist + 1)
    max_exact = N_BUCKETS // 2
    nf = np.maximum(n, 1).astype(np.float32)
    large = max_exact + (np.log(nf / np.float32(max_exact))
                         / np.float32(math.log(MAX_DISTANCE / max_exact))
                         * np.float32(N_BUCKETS - max_exact)).astype(np.int32)
    large = np.minimum(large, N_BUCKETS - 1)
    bucket = np.where(n < max_exact, n, large)
    assert np.all(np.diff(bucket) >= 0)
    thr = []
    for k in range(1, N_BUCKETS):
        idx = np.nonzero(bucket >= k)[0]
        thr.append(int(idx[0]) if idx.size else None)
    return thr


def _bias_from_dist(dist, tab_ref, col, thresholds):
    b = jnp.full(dist.shape, tab_ref[col], F32)
    for k, thr in enumerate(thresholds, start=1):
        if thr is None:
            break
        b = jnp.where(dist >= thr, tab_ref[k * N_BIAS_HEADS + col], b)
    return b


def _bias_att_kernel(tab_ref, out_ref, *, head0, tile, thresholds):
    h = pl.program_id(0)
    o = pl.program_id(1)
    row = lax.broadcasted_iota(I32, (tile, tile), 0)
    colq = lax.broadcasted_iota(I32, (tile, tile), 1)
    dist = jnp.maximum(o * tile + colq - row, 0)
    out_ref[...] = _bias_from_dist(dist, tab_ref, head0 + h, thresholds) * LOG2E


def _bias_att_tiles(tab, head0, n_heads, tile, thresholds):
    return pl.pallas_call(
        functools.partial(_bias_att_kernel, head0=head0, tile=tile, thresholds=thresholds),
        out_shape=jax.ShapeDtypeStruct((n_heads, N_OFFSETS, tile, tile), F32),
        grid=(n_heads, N_OFFSETS),
        in_specs=[pl.BlockSpec(memory_space=pltpu.SMEM)],
        out_specs=pl.BlockSpec((None, None, tile, tile), lambda h, o: (h, o, 0, 0)),
        compiler_params=_cparams(2),
        name="bias_att_tiles",
    )(tab)


def _bias_dil_kernel(tab_ref, out_ref, *, head0, thresholds):
    h = pl.program_id(0)
    g = h // C_HPG
    dil = jnp.where(g == 0, C_GROUPS[0][1], jnp.where(g == 1, C_GROUPS[1][1], C_GROUPS[2][1]))
    i = lax.broadcasted_iota(I32, (C_BAND, 2 * C_BAND), 0)
    j = lax.broadcasted_iota(I32, (C_BAND, 2 * C_BAND), 1)
    dist = jnp.maximum((i - j + C_BAND) * dil, 0)
    out_ref[...] = _bias_from_dist(dist, tab_ref, head0 + h, thresholds)


def _bias_dil_tiles(tab, head0, thresholds):
    return pl.pallas_call(
        functools.partial(_bias_dil_kernel, head0=head0, thresholds=thresholds),
        out_shape=jax.ShapeDtypeStruct((C_HEADS, C_BAND, 2 * C_BAND), F32),
        grid=(C_HEADS,),
        in_specs=[pl.BlockSpec(memory_space=pltpu.SMEM)],
        out_specs=pl.BlockSpec((None, C_BAND, 2 * C_BAND), lambda h: (h, 0, 0)),
        compiler_params=_cparams(1),
        name="bias_dil_tiles",
    )(tab)


def _mod_kernel(c_ref, w_ref, b_ref, out_ref):
    c = c_ref[...]
    ca = (c * jax.nn.sigmoid(c)).astype(MM_DTYPE)
    out_ref[...] = jnp.dot(ca, w_ref[...].astype(MM_DTYPE), preferred_element_type=F32) + b_ref[...]


def _modulation(c, w_ada, b_ada):
    depth, d, wid = w_ada.shape
    bsz = c.shape[0]
    tn = 1536
    assert wid % tn == 0
    return pl.pallas_call(
        _mod_kernel,
        out_shape=jax.ShapeDtypeStruct((depth, bsz, wid), F32),
        grid=(depth, wid // tn),
        in_specs=[pl.BlockSpec((bsz, d), lambda l, j: (0, 0)),
                  pl.BlockSpec((None, d, tn), lambda l, j: (l, 0, j)),
                  pl.BlockSpec((None, 1, tn), lambda l, j: (l, 0, j))],
        out_specs=pl.BlockSpec((None, bsz, tn), lambda l, j: (l, 0, j)),
        compiler_params=_cparams(2),
        name="adaln_modulation",
    )(c, w_ada, b_ada.reshape(depth, 1, wid))


def _rms(x):
    return x * lax.rsqrt(jnp.mean(x * x, axis=-1, keepdims=True) + EPS)


def _in_kernel(x_ref, mod_ref, g1_ref, gkv_ref,
               wT_aq, wT_iq, wT_iw, wT_bq, wT_bv, w_ik, w_kv, w_bk, w_c,
               o_aqT, o_iqT, o_iwT, o_bqT, o_bvT, o_ik, o_kv, o_kvT, o_bk, o_c0, o_c1, o_c2,
               c_scr):
    T = ATT_TILE
    tm = x_ref.shape[0]
    h = _rms(x_ref[...]) * g1_ref[...]
    h = h * (1.0 + mod_ref[1:2, :]) + mod_ref[0:1, :]
    hb = h.astype(MM_DTYPE)

    def mm(w):
        return jnp.dot(hb, w[...], preferred_element_type=F32)

    def mm_t(wT):
        return lax.dot_general(wT[...], hb, (((1,), (1,)), ((), ())), preferred_element_type=F32)

    o_aqT[...] = mm_t(wT_aq).astype(o_aqT.dtype)
    o_iqT[...] = mm_t(wT_iq).astype(o_iqT.dtype)
    o_bqT[...] = (mm_t(wT_bq) * (ATTN_SCALE * LOG2E)).astype(o_bqT.dtype)
    o_iwT[...] = (mm_t(wT_iw) * IDX_SCALE)[:IDX_HEADS]
    bvT = mm_t(wT_bv).astype(o_bvT.dtype)
    kv = _rms(mm(w_kv)) * gkv_ref[...]
    kvT = kv.T.astype(o_kvT.dtype)
    for j in range(tm // T):
        o_bvT[j] = bvT[:, j * T:(j + 1) * T]
        o_kvT[j] = kvT[:, j * T:(j + 1) * T]
    o_kv[...] = kv.astype(o_kv.dtype)
    o_ik[...] = mm(w_ik).astype(o_ik.dtype)
    o_bk[...] = mm(w_bk).astype(o_bk.dtype)

    yc = mm(w_c)
    n_chunk = yc.shape[1] // 128
    for j in range(n_chunk):
        c_scr[j] = yc[:, j * 128:(j + 1) * 128]
    per_group = n_chunk // len(C_GROUPS)
    for g, o_c in enumerate((o_c0, o_c1, o_c2)):
        dil = C_GROUPS[g][1]
        for r in range(dil):
            for jj in range(per_group):
                o_c[r, :, jj * 128:(jj + 1) * 128] = c_scr[
                    g * per_group + jj, pl.ds(r, tm // dil, stride=dil), :].astype(o_c.dtype)


def _in_proj(x2d, mod_l, g1, gkv, ws, bsz, seq):
    n, d = x2d.shape
    tm = ROW_TILE
    T = ATT_TILE
    per_b = seq // tm
    nk = seq // T
    hd = A_HEADS * HEAD_DIM
    bw = B_HEADS * 2 * HEAD_DIM
    gw = 3 * C_HPG * HEAD_DIM
    in_specs = [pl.BlockSpec((tm, d), lambda i: (i, 0)),
                pl.BlockSpec((None, 6, d), lambda i: (i // per_b, 0, 0)),
                _const_spec((1, d)), _const_spec((1, A_LATENT))]
    in_specs += [_const_spec(w.shape) for w in ws]

    def tspec(rows):
        return pl.BlockSpec((None, rows, tm), lambda i: (i // per_b, 0, i % per_b))

    def tile_tspec(rows):
        return pl.BlockSpec((None, tm // T, rows, T), lambda i: (i // per_b, i % per_b, 0, 0))

    def rspec(wd):
        return pl.BlockSpec((tm, wd), lambda i: (i, 0))

    def cspec(dil):
        return pl.BlockSpec((None, dil, tm // dil, gw), lambda i: (i // per_b, 0, i % per_b, 0))

    sds = jax.ShapeDtypeStruct
    out_specs = [tspec(hd), tspec(IDX_HEADS * IDX_DIM), tspec(IDX_HEADS), tspec(bw),
                 tile_tspec(bw), rspec(IDX_DIM), rspec(A_LATENT), tile_tspec(A_LATENT), rspec(bw)]
    out_shape = [sds((bsz, hd, seq), MM_DTYPE), sds((bsz, IDX_HEADS * IDX_DIM, seq), MM_DTYPE),
                 sds((bsz, IDX_HEADS, seq), F32), sds((bsz, bw, seq), MM_DTYPE),
                 sds((bsz, nk, bw, T), MM_DTYPE), sds((n, IDX_DIM), MM_DTYPE),
                 sds((n, A_LATENT), MM_DTYPE), sds((bsz, nk, A_LATENT, T), MM_DTYPE),
                 sds((n, bw), MM_DTYPE)]
    for _, dil in C_GROUPS:
        out_specs.append(cspec(dil))
        out_shape.append(sds((bsz, dil, seq // dil, gw), MM_DTYPE))
    return pl.pallas_call(
        _in_kernel, out_shape=out_shape, grid=(n // tm,),
        in_specs=in_specs, out_specs=out_specs,
        scratch_shapes=[pltpu.VMEM((3 * gw // 128, tm, 128), F32)],
        compiler_params=_cparams(1), name="in_proj",
    )(x2d, mod_l, g1, gkv, *ws)


def _softmax_step(s, m_old):
    m_new = jnp.maximum(m_old, jnp.max(s, axis=0, keepdims=True))
    p = jnp.exp2(s - m_new)
    alpha = jnp.exp2(m_old - m_new)
    return p.astype(MM_DTYPE), alpha, m_new


def _initial_max(n_chains, tile):
    return tuple(jnp.full((1, tile), NEG, F32) for _ in range(n_chains))


SUM_ROWS = 16


def _with_ones_row(vT):
    row = lax.broadcasted_iota(I32, (SUM_ROWS, vT.shape[1]), 0)
    ones = jnp.where(row == 0, 1.0, 0.0).astype(vT.dtype)
    return jnp.concatenate([vT, ones], axis=0)


def _bit_transpose32(words):
    a = list(words)
    j, mask = 16, 0x0000FFFF
    while j:
        k = 0
        while k < 32:
            t = (a[k] ^ lax.shift_right_logical(a[k + j], jnp.int32(j))) & jnp.int32(mask)
            a[k] = a[k] ^ t
            a[k + j] = a[k + j] ^ lax.shift_left(t, jnp.int32(j))
            k = (k + j + 1) & ~j
        j >>= 1
        mask = (mask ^ (mask << j)) & 0xFFFFFFFF
    return a


def _dsa_kernel(aqT_ref, iqT_ref, iwT_ref, kidx_ref, kv_ref, kvT_ref, bias_ref, wukT_ref, wuvT_ref,
                out_ref, keys_ref, planes_ref, qlat_ref, acc_ref, s_ref, oT_ref, *, topk, idx_bits):
    T = ATT_TILE
    qi = pl.program_id(1)
    nk = qi + 1

    for h in range(A_HEADS):
        q = jnp.dot(wukT_ref[h], aqT_ref[h * HEAD_DIM:(h + 1) * HEAD_DIM, :],
                    preferred_element_type=F32) * (ATTN_SCALE * LOG2E)
        qlat_ref[h] = q.astype(qlat_ref.dtype)

    s_loc = lax.broadcasted_iota(I32, (T, T), 0)
    t_loc = lax.broadcasted_iota(I32, (T, T), 1)

    def score_tile(kj, carry):
        kt = kidx_ref[kj]
        acc = jnp.zeros((T, T), F32)
        for h in range(IDX_HEADS):
            s = jnp.dot(kt, iqT_ref[h * IDX_DIM:(h + 1) * IDX_DIM, :], preferred_element_type=F32)
            acc = acc + jnp.maximum(s, 0.0) * iwT_ref[h:h + 1, :]
        bits = lax.bitcast_convert_type(acc, I32)
        key = jnp.where(bits < 0, bits ^ jnp.int32(0x7FFFFFFF), bits)
        visible = (kj * T + s_loc) <= (qi * T + t_loc)
        key = jnp.where(visible, key, jnp.int32(INT_MIN))
        keys_ref[kj] = key
        planes = _bit_transpose32([key[8 * i:8 * (i + 1), :] ^ jnp.int32(INT_MIN) for i in range(32)])
        rows = pl.ds(pl.multiple_of(kj * 8, 8), 8)
        for b in range(32):
            planes_ref[b, rows, :] = planes[b]
        return carry

    lax.fori_loop(0, nk, score_tile, 0)

    def clear_planes(kj, carry):
        rows = pl.ds(pl.multiple_of(kj * 8, 8), 8)
        for b in range(32):
            planes_ref[b, rows, :] = jnp.zeros((8, T), I32)
        return carry

    lax.fori_loop(nk, planes_ref.shape[1] // 8, clear_planes, 0)

    one = jnp.int32(1)
    nil = jnp.int32(0)
    zero = jnp.zeros((1, T), I32)

    n_rows = planes_ref.shape[1]
    tile_of_row = lax.broadcasted_iota(I32, (n_rows, T), 0) // 8
    tied0 = jnp.where(tile_of_row < nk, jnp.int32(-1), nil)

    def bit_step(b, carry):
        tied, n_gt, kth_u = carry
        ones = tied & planes_ref[b]
        n1 = jnp.sum(lax.population_count(ones), axis=0, keepdims=True)
        take = (n_gt + n1) >= topk
        tied = jnp.where(take, ones, tied ^ ones)
        n_gt = jnp.where(take, n_gt, n_gt + n1)
        kth_u = jnp.where(take, kth_u | lax.shift_left(one, jnp.int32(31) - b), kth_u)
        return tied, n_gt, kth_u

    tied, n_gt, kth_u = lax.fori_loop(0, 32, bit_step, (tied0, zero, zero))
    n_eq = jnp.sum(lax.population_count(tied), axis=0, keepdims=True)
    n_ge = jnp.where(kth_u == nil, n_gt, n_gt + n_eq)
    kth = jnp.maximum(kth_u ^ jnp.int32(INT_MIN), jnp.int32(INT_MIN + 1))

    def count(hit_fn):
        def body(kj, c):
            hit = hit_fn(keys_ref[kj], kj)
            return c + jnp.sum(hit.reshape(T // 8, 8, T), axis=0)
        c = lax.fori_loop(0, nk, body, jnp.zeros((8, T), I32))
        return jnp.sum(c, axis=0, keepdims=True)

    @pl.when(jnp.max(n_ge) > topk)
    def _():
        need = topk - n_gt

        def pos_step(i, cut):
            cand = cut | jnp.left_shift(one, jnp.int32(idx_bits - 1) - i)
            c = count(lambda kk, kj: jnp.where(
                kk == kth, jnp.where((kj * T + s_loc) < cand, one, nil), nil))
            return jnp.where(c < need, cand, cut)

        cut = lax.fori_loop(0, idx_bits, pos_step, zero)

        def demote(kj, carry):
            kk = keys_ref[kj]
            lowered = jnp.where((kj * T + s_loc) > cut, kth - one, kk)
            keys_ref[kj] = jnp.where(kk == kth, lowered, kk)
            return carry

        lax.fori_loop(0, nk, demote, 0)

    acc_ref[...] = jnp.zeros(acc_ref.shape, F32)

    def logits(kj, h):
        off = jnp.minimum(qi - kj, N_OFFSETS - 1)
        return bias_ref[h, off] + jnp.dot(kv_ref[kj], qlat_ref[h], preferred_element_type=F32)

    for h in range(A_HEADS):
        s_ref[h] = logits(0, h)

    def attend(kj, ms):
        unselected = jnp.where(keys_ref[kj] >= kth, 0.0, NEG)
        kvTt = _with_ones_row(kvT_ref[kj])
        kj_next = jnp.minimum(kj + 1, nk - 1)
        new_m = []
        for h in range(A_HEADS):
            s = s_ref[h] + unselected
            p, alpha, m_new = _softmax_step(s, ms[h])
            s_ref[h] = logits(kj_next, h)
            acc_ref[h] = alpha * acc_ref[h] + jnp.dot(kvTt, p, preferred_element_type=F32)
            new_m.append(m_new)
        return tuple(new_m)

    lax.fori_loop(0, nk, attend, _initial_max(A_HEADS, T))

    for h in range(A_HEADS):
        o_lat = (acc_ref[h, 0:A_LATENT, :] / acc_ref[h, A_LATENT:A_LATENT + 1, :]).astype(MM_DTYPE)
        oT_ref[h * HEAD_DIM:(h + 1) * HEAD_DIM, :] = jnp.dot(
            wuvT_ref[h], o_lat, preferred_element_type=F32)
    out_ref[...] = oT_ref[...].T.astype(out_ref.dtype)


def _dsa(aqT, iqT, iwT, kidx, kv, kvT, bias_a, wukT, wuvT):
    bsz, _, seq = aqT.shape
    T = ATT_TILE
    nk = seq // T
    topk = min(TOPK_MAX, seq // 4)
    idx_bits = int(math.log2(seq))
    assert 2 ** idx_bits == seq
    qspec = lambda rows: pl.BlockSpec((None, rows, T), lambda b, i: (b, 0, i))
    kspec = lambda a, c: pl.BlockSpec((None, nk, a, c), lambda b, i: (b, 0, 0, 0))
    return pl.pallas_call(
        functools.partial(_dsa_kernel, topk=topk, idx_bits=idx_bits),
        out_shape=jax.ShapeDtypeStruct((bsz, seq, A_HEADS * HEAD_DIM), MM_DTYPE),
        grid=(bsz, nk),
        in_specs=[qspec(A_HEADS * HEAD_DIM), qspec(IDX_HEADS * IDX_DIM), qspec(IDX_HEADS),
                  kspec(T, IDX_DIM), kspec(T, A_LATENT), kspec(A_LATENT, T),
                  _const_spec(bias_a.shape), _const_spec(wukT.shape), _const_spec(wuvT.shape)],
        out_specs=pl.BlockSpec((None, T, A_HEADS * HEAD_DIM), lambda b, i: (b, i, 0)),
        scratch_shapes=[pltpu.VMEM((nk, T, T), I32),
                        pltpu.VMEM((32, nk * (T // 32), T), I32),
                        pltpu.VMEM((A_HEADS, A_LATENT, T), MM_DTYPE),
                        pltpu.VMEM((A_HEADS, A_LATENT + SUM_ROWS, T), F32),
                        pltpu.VMEM((A_HEADS, T, T), F32),
                        pltpu.VMEM((A_HEADS * HEAD_DIM, T), F32)],
        compiler_params=_cparams(2), name="dsa_attention",
    )(aqT, iqT, iwT, kidx, kv, kvT, bias_a, wukT, wuvT)


def _diff_kernel(qT_ref, k_ref, vT_ref, bias_ref, lam_ref, gsub_ref, out_ref,
                 qz_ref, acc_ref, s_ref, oT_ref, *, lam_init):
    T = ATT_TILE
    dv = 2 * HEAD_DIM
    n_chain = 2 * B_HEADS
    qi = pl.program_id(1)
    half = lax.broadcasted_iota(I32, (dv, T), 0) < HEAD_DIM
    for h in range(B_HEADS):
        q = qT_ref[h * dv:(h + 1) * dv, :].astype(F32)
        qz_ref[2 * h] = jnp.where(half, q, 0.0).astype(MM_DTYPE)
        qz_ref[2 * h + 1] = jnp.where(half, 0.0, q).astype(MM_DTYPE)
    s_loc = lax.broadcasted_iota(I32, (T, T), 0)
    t_loc = lax.broadcasted_iota(I32, (T, T), 1)
    acc_ref[...] = jnp.zeros(acc_ref.shape, F32)

    def logits(kj, c):
        h = c // 2
        off = jnp.minimum(qi - kj, N_OFFSETS - 1)
        return bias_ref[h, off] + jnp.dot(k_ref[kj, :, h * dv:(h + 1) * dv], qz_ref[c],
                                          preferred_element_type=F32)

    for c in range(n_chain):
        s_ref[c] = logits(0, c)

    def step(kj, ms, diagonal):
        new_m = []
        for c in range(n_chain):
            h = c // 2
            s = s_ref[c]
            if diagonal:
                s = jnp.where(s_loc <= t_loc, s, NEG)
            p, alpha, m_new = _softmax_step(s, ms[c])
            if not diagonal:
                s_ref[c] = logits(kj + 1, c)
            vT = _with_ones_row(vT_ref[kj, h * dv:(h + 1) * dv, :])
            acc_ref[c] = alpha * acc_ref[c] + jnp.dot(vT, p, preferred_element_type=F32)
            new_m.append(m_new)
        return tuple(new_m)

    ms = lax.fori_loop(0, qi, lambda kj, cr: step(kj, cr, False), _initial_max(n_chain, T))
    step(qi, ms, True)

    lr = lam_ref[...]
    lam = (jnp.exp(jnp.sum(lr[0:1, :] * lr[1:2, :], axis=1, keepdims=True))
           - jnp.exp(jnp.sum(lr[2:3, :] * lr[3:4, :], axis=1, keepdims=True)) + lam_init)

    def normalised(c):
        return acc_ref[c, 0:dv, :] / acc_ref[c, dv:dv + 1, :]

    for h in range(B_HEADS):
        attn = normalised(2 * h) - lam * normalised(2 * h + 1)
        y = attn * lax.rsqrt(jnp.mean(attn * attn, axis=0, keepdims=True) + EPS)
        oT_ref[h * dv:(h + 1) * dv, :] = y * gsub_ref[...] * (1.0 - lam_init)
    out_ref[...] = oT_ref[...].T.astype(out_ref.dtype)


def _diff(bqT, bk, bvT, bias_b, lam_rows, gsub, lam_init):
    bsz, _, seq = bqT.shape
    T = ATT_TILE
    nk = seq // T
    dv = 2 * HEAD_DIM
    qspec = pl.BlockSpec((None, B_HEADS * dv, T), lambda b, i: (b, 0, i))
    return pl.pallas_call(
        functools.partial(_diff_kernel, lam_init=lam_init),
        out_shape=jax.ShapeDtypeStruct((bsz, seq, B_HEADS * dv), MM_DTYPE),
        grid=(bsz, nk),
        in_specs=[qspec,
                  pl.BlockSpec((None, nk, T, B_HEADS * dv), lambda b, i: (b, 0, 0, 0)),
                  pl.BlockSpec((None, nk, B_HEADS * dv, T), lambda b, i: (b, 0, 0, 0)),
                  _const_spec(bias_b.shape), _const_spec((4, HEAD_DIM)), _const_spec((dv, 1))],
        out_specs=pl.BlockSpec((None, T, B_HEADS * dv), lambda b, i: (b, i, 0)),
        scratch_shapes=[pltpu.VMEM((2 * B_HEADS, dv, T), MM_DTYPE),
                        pltpu.VMEM((2 * B_HEADS, dv + SUM_ROWS, T), F32),
                        pltpu.VMEM((2 * B_HEADS, T, T), F32),
                        pltpu.VMEM((B_HEADS * dv, T), F32)],
        compiler_params=_cparams(2), name="diff_attention",
    )(bqT, bk, bvT, bias_b, lam_rows, gsub)


def _dil_kernel(cur_ref, halo_ref, bias_ref, out_ref, lse_ref, *, tq):
    n = C_BAND
    wid = C_HPG * HEAD_DIM
    halo_lo = jnp.where(pl.program_id(2) == 0, jnp.int32(n), jnp.int32(0))
    i = lax.broadcasted_iota(I32, (n, 2 * n), 0)
    j = lax.broadcasted_iota(I32, (n, 2 * n), 1)
    lane_head = lax.broadcasted_iota(I32, (n, wid), 1) // HEAD_DIM
    in_head = [lane_head == h for h in range(C_HPG)]
    band = jnp.where(j >= i, jnp.where(j <= i + n, 0.0, NEG), NEG)
    band0 = jnp.where(j >= jnp.maximum(i, halo_lo), jnp.where(j <= i + n, 0.0, NEG), NEG)
    bias = [bias_ref[h] + band for h in range(C_HPG)]
    bias0 = [bias_ref[h] + band0 for h in range(C_HPG)]

    def band_rows(c, lo, hi):
        if c == 0:
            return jnp.concatenate([halo_ref[:, lo:hi], cur_ref[0:n, lo:hi]], axis=0)
        return cur_ref[(c - 1) * n:(c + 1) * n, lo:hi]

    logits = []
    for c in range(tq // n):
        q = cur_ref[c * n:(c + 1) * n, 0:wid].astype(F32) * ATTN_SCALE
        keys = band_rows(c, wid, 2 * wid)
        for h in range(C_HPG):
            qh = jnp.where(in_head[h], q, 0.0).astype(MM_DTYPE)
            s = lax.dot_general(qh, keys, (((1,), (1,)), ((), ())), preferred_element_type=F32)
            logits.append(s + (bias0 if c == 0 else bias)[h])
    for c in range(tq // n):
        vals = band_rows(c, 2 * wid, 3 * wid)
        out = jnp.zeros((n, wid), F32)
        lse = jnp.zeros((n, wid), F32)
        for h in range(C_HPG):
            s = logits[c * C_HPG + h]
            m = jnp.max(s, axis=1, keepdims=True)
            p = jnp.exp(s - m)
            den = jnp.sum(p, axis=1, keepdims=True)
            o = jnp.dot(p.astype(MM_DTYPE), vals, preferred_element_type=F32) * (1.0 / den)
            out = jnp.where(in_head[h], o, out)
            lse = jnp.where(in_head[h], m + jnp.log(den), lse)
        out_ref[c * n:(c + 1) * n, :] = out
        lse_ref[c * n:(c + 1) * n, :] = lse


def _dilated_group(cg, bias_g, g):
    bsz, dil, m, gw = cg.shape
    wid = C_HPG * HEAD_DIM
    n = C_BAND
    assert m % n == 0 and gw == 3 * wid
    tq = min(m, 512)
    cur = pl.BlockSpec((None, None, tq, gw), lambda b, r, i: (b, r, i, 0))
    halo = pl.BlockSpec((None, None, n, gw),
                        lambda b, r, i: (b, r, jnp.maximum(i * (tq // n) - 1, 0), 0))
    outspec = pl.BlockSpec((None, None, tq, wid), lambda b, r, i: (b, r, i, 0))
    return pl.pallas_call(
        functools.partial(_dil_kernel, tq=tq),
        out_shape=[jax.ShapeDtypeStruct((bsz, dil, m, wid), F32)] * 2,
        grid=(bsz, dil, m // tq),
        in_specs=[cur, halo, pl.BlockSpec((C_HPG, n, 2 * n), lambda b, r, i: (0, 0, 0))],
        out_specs=[outspec, outspec],
        compiler_params=_cparams(3), name=f"dilated_group{g}",
    )(cg, cg, bias_g)


def _merge_kernel(x_ref, mod_ref, g1_ref, oa_ref, ob_ref,
                  c0_ref, c1_ref, c2_ref, s0_ref, s1_ref, s2_ref,
                  wza, wzb, wzc, wba, wbb, wbc, wo, out_ref, tok_ref):
    x = x_ref[...]
    tm = x.shape[0]
    h = _rms(x) * g1_ref[...]
    h = h * (1.0 + mod_ref[1:2, :]) + mod_ref[0:1, :]
    hb = h.astype(MM_DTYPE)

    def token_order(k, ref):
        dil, _, w = ref.shape
        if dil == 1:
            return ref[0]
        n_chunk = w // 128
        for r in range(dil):
            for j in range(n_chunk):
                tok_ref[k * n_chunk + j, pl.ds(r, tm // dil, stride=dil), :] = ref[
                    r, :, j * 128:(j + 1) * 128]
        return jnp.concatenate([tok_ref[k * n_chunk + j] for j in range(n_chunk)], axis=1)

    s0, s1, s2 = s0_ref[0], token_order(0, s1_ref), token_order(1, s2_ref)
    c0, c1, c2 = c0_ref[0], token_order(2, c1_ref), token_order(3, c2_ref)
    mx = jnp.maximum(jnp.maximum(s0, s1), s2)
    e0, e1, e2 = jnp.exp(s0 - mx), jnp.exp(s1 - mx), jnp.exp(s2 - mx)
    oc = (e0 * c0 + e1 * c1 + e2 * c2) / (e0 + e1 + e2)

    def gated(wz, o, wb):
        z = jnp.dot(hb, wz[...], preferred_element_type=F32)
        return jax.nn.sigmoid(z) * jnp.dot(o, wb[...], preferred_element_type=F32)

    merged = (gated(wza, oa_ref[...], wba) + gated(wzb, ob_ref[...], wbb)
              + gated(wzc, oc.astype(MM_DTYPE), wbc))
    y = jnp.dot(merged.astype(MM_DTYPE), wo[...], preferred_element_type=F32)
    out_ref[...] = x + mod_ref[2:3, :] * y


def _merge(x2d, mod_l, g1, oa, ob, ocs, lses, ws, seq):
    n, d = x2d.shape
    tm = ROW_TILE
    per_b = seq // tm
    row = lambda wd: pl.BlockSpec((tm, wd), lambda i: (i, 0))
    wid = C_HPG * HEAD_DIM
    res = [pl.BlockSpec((None, dil, tm // dil, wid), lambda i: (i // per_b, 0, i % per_b, 0))
           for _, dil in C_GROUPS]
    in_specs = [row(d), pl.BlockSpec((None, 6, d), lambda i: (i // per_b, 0, 0)), _const_spec((1, d)),
                row(oa.shape[1]), row(ob.shape[1])] + res + res
    in_specs += [_const_spec(w.shape) for w in ws]
    return pl.pallas_call(
        _merge_kernel, out_shape=jax.ShapeDtypeStruct((n, d), F32), grid=(n // tm,),
        in_specs=in_specs, out_specs=row(d),
        scratch_shapes=[pltpu.VMEM((4 * wid // 128, tm, 128), F32)],
        compiler_params=_cparams(1), name="gated_merge",
    )(x2d, mod_l, g1, oa, ob, *ocs, *lses, *ws)


def _ffn_kernel(x_ref, mod_ref, g2_ref, gf_ref, wgu, wd, out_ref, *, chunk, final_norm):
    x = x_ref[...]
    h = _rms(x) * g2_ref[...]
    h = h * (1.0 + mod_ref[4:5, :]) + mod_ref[3:4, :]
    hb = h.astype(MM_DTYPE)
    acc = jnp.zeros(x.shape, F32)
    for c in range(D_FF // chunk):
        fg = jnp.dot(hb, wgu[:, c * chunk:(c + 1) * chunk], preferred_element_type=F32)
        fu = jnp.dot(hb, wgu[:, D_FF + c * chunk:D_FF + (c + 1) * chunk],
                     preferred_element_type=F32)
        act = (fg * jax.nn.sigmoid(fg) * fu).astype(MM_DTYPE)
        acc = acc + jnp.dot(act, wd[c * chunk:(c + 1) * chunk, :], preferred_element_type=F32)
    y = x + mod_ref[5:6, :] * acc
    if final_norm:
        y = _rms(y) * gf_ref[...]
    out_ref[...] = y


def _ffn(x2d, mod_l, g2, gf, wgu, wd, seq, final_norm):
    n, d = x2d.shape
    tm = ROW_TILE
    per_b = seq // tm
    row = pl.BlockSpec((tm, d), lambda i: (i, 0))
    return pl.pallas_call(
        functools.partial(_ffn_kernel, chunk=256, final_norm=final_norm),
        out_shape=jax.ShapeDtypeStruct((n, d), F32), grid=(n // tm,),
        in_specs=[row, pl.BlockSpec((None, 6, d), lambda i: (i // per_b, 0, 0)),
                  _const_spec((1, d)), _const_spec((1, d)),
                  _const_spec(wgu.shape), _const_spec(wd.shape)],
        out_specs=row,
        compiler_params=_cparams(1), name="swiglu_ffn",
    )(x2d, mod_l, g2, gf, wgu, wd)


def kernel(x, c, w_ada, b_ada, g_norm1, w_in, w_uk, w_uv, g_kv, lam_q1, lam_k1, lam_q2, lam_k2,
           g_subln, w_branch_a, w_branch_b, w_branch_c, w_out, g_norm2, w_gate_up, w_down,
           rel_bias, g_final):
    bsz, seq, d = x.shape
    depth = w_ada.shape[0]
    T = ATT_TILE
    nk = seq // T
    assert d == D_MODEL and seq % T == 0 and seq % ROW_TILE == 0
    n = bsz * seq
    cast = lambda w: w.astype(MM_DTYPE)

    thresholds = _bucket_thresholds(seq + 2 * C_BAND * C_GROUPS[-1][1])
    assert seq <= N_OFFSETS * T or (N_OFFSETS - 2) * T + 1 >= thresholds[-1]
    tab = rel_bias.reshape(-1)
    bias_a = _bias_att_tiles(tab, 0, A_HEADS, T, thresholds)
    bias_b = _bias_att_tiles(tab, A_HEADS, B_HEADS, T, thresholds)
    bias_c = _bias_dil_tiles(tab, A_HEADS + B_HEADS, thresholds)

    mod = _modulation(c, w_ada, b_ada).reshape(depth, bsz, 6, d)

    splits = (A_HEADS * HEAD_DIM, A_LATENT, IDX_HEADS * IDX_DIM, IDX_DIM, IDX_HEADS,
              B_HEADS * 2 * HEAD_DIM, B_HEADS * 2 * HEAD_DIM, B_HEADS * 2 * HEAD_DIM,
              C_HEADS * HEAD_DIM, C_HEADS * HEAD_DIM, C_HEADS * HEAD_DIM, d, d, d)
    offs = np.concatenate([[0], np.cumsum(splits)])
    seg = lambda w, k: w[:, int(offs[k]):int(offs[k + 1])]

    x2d = x.reshape(n, d)
    for l in range(depth):
        wl = w_in[l]
        castT = lambda w: w.T.astype(MM_DTYPE)
        wT_iw = jnp.pad(seg(wl, 4).T, ((0, 16 - IDX_HEADS), (0, 0)))
        wid = C_HPG * HEAD_DIM
        w_c = jnp.concatenate([seg(wl, k)[:, g * wid:(g + 1) * wid]
                               for g in range(len(C_GROUPS)) for k in (8, 9, 10)], axis=1)
        ws_in = [castT(seg(wl, 0)), castT(seg(wl, 2)), cast(wT_iw), castT(seg(wl, 5)), castT(seg(wl, 7)),
                 cast(seg(wl, 3)), cast(seg(wl, 1)), cast(seg(wl, 6)), cast(w_c)]
        g1 = g_norm1[l].reshape(1, d)
        (aqT, iqT, iwT, bqT, bvT, ik, kv, kvT, bk, cg0, cg1, cg2) = _in_proj(
            x2d, mod[l], g1, g_kv[l].reshape(1, A_LATENT), ws_in, bsz, seq)

        o_a = _dsa(aqT, iqT, iwT,
                   ik.reshape(bsz, nk, T, IDX_DIM), kv.reshape(bsz, nk, T, A_LATENT), kvT,
                   bias_a, cast(w_uk[l].transpose(0, 2, 1)), cast(w_uv[l].transpose(0, 2, 1)))
        o_a = o_a.reshape(n, -1)

        lam_init = 0.8 - 0.6 * math.exp(-0.3 * l)
        lam_rows = jnp.stack([lam_q1[l], lam_k1[l], lam_q2[l], lam_k2[l]])
        dv = 2 * HEAD_DIM
        o_b = _diff(bqT, bk.reshape(bsz, nk, T, B_HEADS * dv), bvT, bias_b, lam_rows,
                    g_subln[l].reshape(dv, 1), lam_init)
        o_b = o_b.reshape(n, -1)

        ocs, lses = [], []
        for g, (cg, (window, dil)) in enumerate(zip((cg0, cg1, cg2), C_GROUPS)):
            assert window // dil == C_BAND
            o, s = _dilated_group(cg, bias_c[g * C_HPG:(g + 1) * C_HPG], g)
            ocs.append(o)
            lses.append(s)

        ws_merge = [cast(seg(wl, 11)), cast(seg(wl, 12)), cast(seg(wl, 13)),
                    cast(w_branch_a[l]), cast(w_branch_b[l]), cast(w_branch_c[l]), cast(w_out[l])]
        x2d = _merge(x2d, mod[l], g1, o_a, o_b, ocs, lses, ws_merge, seq)
        x2d = _ffn(x2d, mod[l], g_norm2[l].reshape(1, d), g_final.reshape(1, d),
                   cast(w_gate_up[l]), cast(w_down[l]), seq, final_norm=(l == depth - 1))
    return x2d.reshape(bsz, seq, d)
```

```python
import functools
import math

import numpy as np
import jax
import jax.numpy as jnp
from jax import lax
from jax.experimental import pallas as pl
from jax.experimental.pallas import tpu as pltpu

D_MODEL = 1024
HEAD_DIM = 64
ATTN_SCALE = HEAD_DIM ** -0.5
LOG2E = math.log2(math.e)
A_HEADS = 8
A_LATENT = 128
IDX_HEADS = 8
IDX_DIM = 64
IDX_SCALE = (IDX_HEADS * IDX_DIM) ** -0.5
TOPK_MAX = 256
B_HEADS = 4
C_GROUPS = ((128, 1), (512, 4), (2048, 16))
C_HPG = 4
C_HEADS = C_HPG * len(C_GROUPS)
N_BUCKETS = 32
MAX_DISTANCE = 2048
N_BIAS_HEADS = A_HEADS + B_HEADS + C_HEADS
D_FF = -(-8 * D_MODEL // (3 * 256)) * 256
EPS = 1e-6

MM_DTYPE = jnp.bfloat16
F32 = jnp.float32
I32 = jnp.int32

ATT_TILE = 256
N_OFFSETS = 8
C_BAND = 128
ROW_TILE = 512
NEG = -1e30
INT_MIN = -2 ** 31
VMEM_LIMIT = 56 * 1024 * 1024


def _cparams(n_axes, vmem=VMEM_LIMIT):
    return pltpu.CompilerParams(dimension_semantics=("arbitrary",) * n_axes,
                                vmem_limit_bytes=vmem)


def _const_spec(shape):
    nd = len(shape)
    return pl.BlockSpec(shape, lambda *_: (0,) * nd, pipeline_mode=pl.Buffered(1))


def _bucket_thresholds(max_dist):
    n = np.arange(max_dist + 1)
    max_exact = N_BUCKETS // 2
    nf = np.maximum(n, 1).astype(np.float32)
    large = max_exact + (np.log(nf / np.float32(max_exact))
                         / np.float32(math.log(MAX_DISTANCE / max_exact))
                         * np.float32(N_BUCKETS - max_exact)).astype(np.int32)
    large = np.minimum(large, N_BUCKETS - 1)
    bucket = np.where(n < max_exact, n, large)
    assert np.all(np.diff(bucket) >= 0)
    thr = []
    for k in range(1, N_BUCKETS):
        idx = np.nonzero(bucket >= k)[0]
        thr.append(int(idx[0]) if idx.size else None)
    return thr


def _bias_from_dist(dist, tab_ref, col, thresholds):
    b = jnp.full(dist.shape, tab_ref[col], F32)
    for k, thr in enumerate(thresholds, start=1):
        if thr is None:
            break
        b = jnp.where(dist >= thr, tab_ref[k * N_BIAS_HEADS + col], b)
    return b


def _bias_att_kernel(tab_ref, out_ref, *, head0, tile, thresholds):
    h = pl.program_id(0)
    o = pl.program_id(1)
    row = lax.broadcasted_iota(I32, (tile, tile), 0)
    colq = lax.broadcasted_iota(I32, (tile, tile), 1)
    dist = jnp.maximum(o * tile + colq - row, 0)
    out_ref[...] = _bias_from_dist(dist, tab_ref, head0 + h, thresholds) * LOG2E


def _bias_att_tiles(tab, head0, n_heads, tile, thresholds):
    return pl.pallas_call(
        functools.partial(_bias_att_kernel, head0=head0, tile=tile, thresholds=thresholds),
        out_shape=jax.ShapeDtypeStruct((n_heads, N_OFFSETS, tile, tile), F32),
        grid=(n_heads, N_OFFSETS),
        in_specs=[pl.BlockSpec(memory_space=pltpu.SMEM)],
        out_specs=pl.BlockSpec((None, None, tile, tile), lambda h, o: (h, o, 0, 0)),
        compiler_params=_cparams(2),
        name="bias_att_tiles",
    )(tab)


def _bias_dil_kernel(tab_ref, out_ref, *, head0, thresholds):
    h = pl.program_id(0)
    g = h // C_HPG
    dil = jnp.where(g == 0, C_GROUPS[0][1], jnp.where(g == 1, C_GROUPS[1][1], C_GROUPS[2][1]))
    i = lax.broadcasted_iota(I32, (C_BAND, 2 * C_BAND), 0)
    j = lax.broadcasted_iota(I32, (C_BAND, 2 * C_BAND), 1)
    dist = jnp.maximum((i - j + C_BAND) * dil, 0)
    out_ref[...] = _bias_from_dist(dist, tab_ref, head0 + h, thresholds)


def _bias_dil_tiles(tab, head0, thresholds):
    return pl.pallas_call(
        functools.partial(_bias_dil_kernel, head0=head0, thresholds=thresholds),
        out_shape=jax.ShapeDtypeStruct((C_HEADS, C_BAND, 2 * C_BAND), F32),
        grid=(C_HEADS,),
        in_specs=[pl.BlockSpec(memory_space=pltpu.SMEM)],
        out_specs=pl.BlockSpec((None, C_BAND, 2 * C_BAND), lambda h: (h, 0, 0)),
        compiler_params=_cparams(1),
        name="bias_dil_tiles",
    )(tab)


def _mod_kernel(c_ref, w_ref, b_ref, out_ref):
    c = c_ref[...]
    ca = (c * jax.nn.sigmoid(c)).astype(MM_DTYPE)
    out_ref[...] = jnp.dot(ca, w_ref[...].astype(MM_DTYPE), preferred_element_type=F32) + b_ref[...]


def _modulation(c, w_ada, b_ada):
    depth, d, wid = w_ada.shape
    bsz = c.shape[0]
    tn = 1536
    assert wid % tn == 0
    return pl.pallas_call(
        _mod_kernel,
        out_shape=jax.ShapeDtypeStruct((depth, bsz, wid), F32),
        grid=(depth, wid // tn),
        in_specs=[pl.BlockSpec((bsz, d), lambda l, j: (0, 0)),
                  pl.BlockSpec((None, d, tn), lambda l, j: (l, 0, j)),
                  pl.BlockSpec((None, 1, tn), lambda l, j: (l, 0, j))],
        out_specs=pl.BlockSpec((None, bsz, tn), lambda l, j: (l, 0, j)),
        compiler_params=_cparams(2),
        name="adaln_modulation",
    )(c, w_ada, b_ada.reshape(depth, 1, wid))


def _rms(x):
    return x * lax.rsqrt(jnp.mean(x * x, axis=-1, keepdims=True) + EPS)


def _in_kernel(x_ref, mod_ref, g1_ref, gkv_ref,
               wT_aq, wT_iq, wT_iw, wT_bq, wT_bv, w_ik, w_kv, w_bk, w_c,
               o_aqT, o_iqT, o_iwT, o_bqT, o_bvT, o_ik, o_kv, o_kvT, o_bk, o_c0, o_c1, o_c2,
               c_scr):
    T = ATT_TILE
    tm = x_ref.shape[0]
    h = _rms(x_ref[...]) * g1_ref[...]
    h = h * (1.0 + mod_ref[1:2, :]) + mod_ref[0:1, :]
    hb = h.astype(MM_DTYPE)

    def mm(w):
        return jnp.dot(hb, w[...], preferred_element_type=F32)

    def mm_t(wT):
        return lax.dot_general(wT[...], hb, (((1,), (1,)), ((), ())), preferred_element_type=F32)

    o_aqT[...] = mm_t(wT_aq).astype(o_aqT.dtype)
    o_iqT[...] = mm_t(wT_iq).astype(o_iqT.dtype)
    o_bqT[...] = (mm_t(wT_bq) * (ATTN_SCALE * LOG2E)).astype(o_bqT.dtype)
    o_iwT[...] = (mm_t(wT_iw) * IDX_SCALE)[:IDX_HEADS]
    bvT = mm_t(wT_bv).astype(o_bvT.dtype)
    kv = _rms(mm(w_kv)) * gkv_ref[...]
    kvT = kv.T.astype(o_kvT.dtype)
    for j in range(tm // T):
        o_bvT[j] = bvT[:, j * T:(j + 1) * T]
        o_kvT[j] = kvT[:, j * T:(j + 1) * T]
    o_kv[...] = kv.astype(o_kv.dtype)
    o_ik[...] = mm(w_ik).astype(o_ik.dtype)
    o_bk[...] = mm(w_bk).astype(o_bk.dtype)

    yc = mm(w_c)
    n_chunk = yc.shape[1] // 128
    for j in range(n_chunk):
        c_scr[j] = yc[:, j * 128:(j + 1) * 128]
    per_group = n_chunk // len(C_GROUPS)
    for g, o_c in enumerate((o_c0, o_c1, o_c2)):
        dil = C_GROUPS[g][1]
        for r in range(dil):
            for jj in range(per_group):
                o_c[r, :, jj * 128:(jj + 1) * 128] = c_scr[
                    g * per_group + jj, pl.ds(r, tm // dil, stride=dil), :].astype(o_c.dtype)


def _in_proj(x2d, mod_l, g1, gkv, ws, bsz, seq):
    n, d = x2d.shape
    tm = ROW_TILE
    T = ATT_TILE
    per_b = seq // tm
    nk = seq // T
    hd = A_HEADS * HEAD_DIM
    bw = B_HEADS * 2 * HEAD_DIM
    gw = 3 * C_HPG * HEAD_DIM
    in_specs = [pl.BlockSpec((tm, d), lambda i: (i, 0)),
                pl.BlockSpec((None, 6, d), lambda i: (i // per_b, 0, 0)),
                _const_spec((1, d)), _const_spec((1, A_LATENT))]
    in_specs += [_const_spec(w.shape) for w in ws]

    def tspec(rows):
        return pl.BlockSpec((None, rows, tm), lambda i: (i // per_b, 0, i % per_b))

    def tile_tspec(rows):
        return pl.BlockSpec((None, tm // T, rows, T), lambda i: (i // per_b, i % per_b, 0, 0))

    def rspec(wd):
        return pl.BlockSpec((tm, wd), lambda i: (i, 0))

    def cspec(dil):
        return pl.BlockSpec((None, dil, tm // dil, gw), lambda i: (i // per_b, 0, i % per_b, 0))

    sds = jax.ShapeDtypeStruct
    out_specs = [tspec(hd), tspec(IDX_HEADS * IDX_DIM), tspec(IDX_HEADS), tspec(bw),
                 tile_tspec(bw), rspec(IDX_DIM), rspec(A_LATENT), tile_tspec(A_LATENT), rspec(bw)]
    out_shape = [sds((bsz, hd, seq), MM_DTYPE), sds((bsz, IDX_HEADS * IDX_DIM, seq), MM_DTYPE),
                 sds((bsz, IDX_HEADS, seq), F32), sds((bsz, bw, seq), MM_DTYPE),
                 sds((bsz, nk, bw, T), MM_DTYPE), sds((n, IDX_DIM), MM_DTYPE),
                 sds((n, A_LATENT), MM_DTYPE), sds((bsz, nk, A_LATENT, T), MM_DTYPE),
                 sds((n, bw), MM_DTYPE)]
    for _, dil in C_GROUPS:
        out_specs.append(cspec(dil))
        out_shape.append(sds((bsz, dil, seq // dil, gw), MM_DTYPE))
    return pl.pallas_call(
        _in_kernel, out_shape=out_shape, grid=(n // tm,),
        in_specs=in_specs, out_specs=out_specs,
        scratch_shapes=[pltpu.VMEM((3 * gw // 128, tm, 128), F32)],
        compiler_params=_cparams(1), name="in_proj",
    )(x2d, mod_l, g1, gkv, *ws)


def _softmax_step(s, m_old):
    m_new = jnp.maximum(m_old, jnp.max(s, axis=0, keepdims=True))
    p = jnp.exp2(s - m_new)
    alpha = jnp.exp2(m_old - m_new)
    return p.astype(MM_DTYPE), alpha, m_new


def _initial_max(n_chains, tile):
    return tuple(jnp.full((1, tile), NEG, F32) for _ in range(n_chains))


SUM_ROWS = 16


def _with_ones_row(vT):
    row = lax.broadcasted_iota(I32, (SUM_ROWS, vT.shape[1]), 0)
    ones = jnp.where(row == 0, 1.0, 0.0).astype(vT.dtype)
    return jnp.concatenate([vT, ones], axis=0)


def _bit_transpose32(words):
    a = list(words)
    j, mask = 16, 0x0000FFFF
    while j:
        k = 0
        while k < 32:
            t = (a[k] ^ lax.shift_right_logical(a[k + j], jnp.int32(j))) & jnp.int32(mask)
            a[k] = a[k] ^ t
            a[k + j] = a[k + j] ^ lax.shift_left(t, jnp.int32(j))
            k = (k + j + 1) & ~j
        j >>= 1
        mask = (mask ^ (mask << j)) & 0xFFFFFFFF
    return a


def _dsa_kernel(aqT_ref, iqT_ref, iwT_ref, kidx_ref, kv_ref, kvT_ref, bias_ref, wukT_ref, wuvT_ref,
                out_ref, keys_ref, planes_ref, qlat_ref, acc_ref, s_ref, oT_ref, *, topk, idx_bits):
    T = ATT_TILE
    qi = pl.program_id(1)
    nk = qi + 1

    for h in range(A_HEADS):
        q = jnp.dot(wukT_ref[h], aqT_ref[h * HEAD_DIM:(h + 1) * HEAD_DIM, :],
                    preferred_element_type=F32) * (ATTN_SCALE * LOG2E)
        qlat_ref[h] = q.astype(qlat_ref.dtype)

    s_loc = lax.broadcasted_iota(I32, (T, T), 0)
    t_loc = lax.broadcasted_iota(I32, (T, T), 1)

    def score_tile(kj, carry):
        kt = kidx_ref[kj]
        acc = jnp.zeros((T, T), F32)
        for h in range(IDX_HEADS):
            s = jnp.dot(kt, iqT_ref[h * IDX_DIM:(h + 1) * IDX_DIM, :], preferred_element_type=F32)
            acc = acc + jnp.maximum(s, 0.0) * iwT_ref[h:h + 1, :]
        bits = lax.bitcast_convert_type(acc, I32)
        key = jnp.where(bits < 0, bits ^ jnp.int32(0x7FFFFFFF), bits)
        visible = (kj * T + s_loc) <= (qi * T + t_loc)
        key = jnp.where(visible, key, jnp.int32(INT_MIN))
        keys_ref[kj] = key
        planes = _bit_transpose32([key[8 * i:8 * (i + 1), :] ^ jnp.int32(INT_MIN) for i in range(32)])
        for b in range(32):
            planes_ref[kj, b] = planes[b]
        return carry

    lax.fori_loop(0, nk, score_tile, 0)

    n_tiles = planes_ref.shape[0]

    def clear_planes(kj, carry):
        for b in range(32):
            planes_ref[kj, b] = jnp.zeros((8, T), I32)
        return carry

    lax.fori_loop(nk, n_tiles, clear_planes, 0)

    one = jnp.int32(1)
    nil = jnp.int32(0)
    zero = jnp.zeros((1, T), I32)

    n_rows = n_tiles * 8
    tile_of_row = lax.broadcasted_iota(I32, (n_rows, T), 0) // 8
    tied0 = jnp.where(tile_of_row < nk, jnp.int32(-1), nil)

    def bit_step(b, carry):
        tied, n_gt, kth_u = carry
        ones = tied & planes_ref[:, b].reshape(n_rows, T)
        n1 = jnp.sum(lax.population_count(ones), axis=0, keepdims=True)
        take = (n_gt + n1) >= topk
        tied = jnp.where(take, ones, tied ^ ones)
        n_gt = jnp.where(take, n_gt, n_gt + n1)
        kth_u = jnp.where(take, kth_u | lax.shift_left(one, jnp.int32(31) - b), kth_u)
        return tied, n_gt, kth_u

    tied, n_gt, kth_u = lax.fori_loop(0, 32, bit_step, (tied0, zero, zero))
    n_eq = jnp.sum(lax.population_count(tied), axis=0, keepdims=True)
    n_ge = jnp.where(kth_u == nil, n_gt, n_gt + n_eq)
    kth = jnp.maximum(kth_u ^ jnp.int32(INT_MIN), jnp.int32(INT_MIN + 1))

    def count(hit_fn):
        def body(kj, c):
            hit = hit_fn(keys_ref[kj], kj)
            return c + jnp.sum(hit.reshape(T // 8, 8, T), axis=0)
        c = lax.fori_loop(0, nk, body, jnp.zeros((8, T), I32))
        return jnp.sum(c, axis=0, keepdims=True)

    @pl.when(jnp.max(n_ge) > topk)
    def _():
        need = topk - n_gt

        def pos_step(i, cut):
            cand = cut | jnp.left_shift(one, jnp.int32(idx_bits - 1) - i)
            c = count(lambda kk, kj: jnp.where(
                kk == kth, jnp.where((kj * T + s_loc) < cand, one, nil), nil))
            return jnp.where(c < need, cand, cut)

        cut = lax.fori_loop(0, idx_bits, pos_step, zero)

        def demote(kj, carry):
            kk = keys_ref[kj]
            lowered = jnp.where((kj * T + s_loc) > cut, kth - one, kk)
            keys_ref[kj] = jnp.where(kk == kth, lowered, kk)
            return carry

        lax.fori_loop(0, nk, demote, 0)

    acc_ref[...] = jnp.zeros(acc_ref.shape, F32)

    def logits(kj, h):
        off = jnp.minimum(qi - kj, N_OFFSETS - 1)
        return bias_ref[h, off] + jnp.dot(kv_ref[kj], qlat_ref[h], preferred_element_type=F32)

    for h in range(A_HEADS):
        s_ref[h] = logits(0, h)

    def attend(kj, ms):
        unselected = jnp.where(keys_ref[kj] >= kth, 0.0, NEG)
        kvTt = _with_ones_row(kvT_ref[kj])
        kj_next = jnp.minimum(kj + 1, nk - 1)
        new_m = []
        for h in range(A_HEADS):
            s = s_ref[h] + unselected
            p, alpha, m_new = _softmax_step(s, ms[h])
            s_ref[h] = logits(kj_next, h)
            acc_ref[h] = alpha * acc_ref[h] + jnp.dot(kvTt, p, preferred_element_type=F32)
            new_m.append(m_new)
        return tuple(new_m)

    lax.fori_loop(0, nk, attend, _initial_max(A_HEADS, T))

    for h in range(A_HEADS):
        o_lat = (acc_ref[h, 0:A_LATENT, :] / acc_ref[h, A_LATENT:A_LATENT + 1, :]).astype(MM_DTYPE)
        oT_ref[h * HEAD_DIM:(h + 1) * HEAD_DIM, :] = jnp.dot(
            wuvT_ref[h], o_lat, preferred_element_type=F32)
    out_ref[...] = oT_ref[...].T.astype(out_ref.dtype)


def _dsa(aqT, iqT, iwT, kidx, kv, kvT, bias_a, wukT, wuvT):
    bsz, _, seq = aqT.shape
    T = ATT_TILE
    nk = seq // T
    topk = min(TOPK_MAX, seq // 4)
    idx_bits = int(math.log2(seq))
    assert 2 ** idx_bits == seq
    qspec = lambda rows: pl.BlockSpec((None, rows, T), lambda b, i: (b, 0, i))
    kspec = lambda a, c: pl.BlockSpec((None, nk, a, c), lambda b, i: (b, 0, 0, 0))
    return pl.pallas_call(
        functools.partial(_dsa_kernel, topk=topk, idx_bits=idx_bits),
        out_shape=jax.ShapeDtypeStruct((bsz, seq, A_HEADS * HEAD_DIM), MM_DTYPE),
        grid=(bsz, nk),
        in_specs=[qspec(A_HEADS * HEAD_DIM), qspec(IDX_HEADS * IDX_DIM), qspec(IDX_HEADS),
                  kspec(T, IDX_DIM), kspec(T, A_LATENT), kspec(A_LATENT, T),
                  _const_spec(bias_a.shape), _const_spec(wukT.shape), _const_spec(wuvT.shape)],
        out_specs=pl.BlockSpec((None, T, A_HEADS * HEAD_DIM), lambda b, i: (b, i, 0)),
        scratch_shapes=[pltpu.VMEM((nk, T, T), I32),
                        pltpu.VMEM((nk, 33, T // 32, T), I32),
                        pltpu.VMEM((A_HEADS, A_LATENT, T), MM_DTYPE),
                        pltpu.VMEM((A_HEADS, A_LATENT + SUM_ROWS, T), F32),
                        pltpu.VMEM((A_HEADS, T, T), F32),
                        pltpu.VMEM((A_HEADS * HEAD_DIM, T), F32)],
        compiler_params=_cparams(2), name="dsa_attention",
    )(aqT, iqT, iwT, kidx, kv, kvT, bias_a, wukT, wuvT)


def _diff_kernel(qT_ref, k_ref, vT_ref, bias_ref, lam_ref, gsub_ref, out_ref,
                 qz_ref, acc_ref, s_ref, oT_ref, *, lam_init):
    T = ATT_TILE
    dv = 2 * HEAD_DIM
    n_chain = 2 * B_HEADS
    qi = pl.program_id(1)
    half = lax.broadcasted_iota(I32, (dv, T), 0) < HEAD_DIM
    for h in range(B_HEADS):
        q = qT_ref[h * dv:(h + 1) * dv, :].astype(F32)
        qz_ref[2 * h] = jnp.where(half, q, 0.0).astype(MM_DTYPE)
        qz_ref[2 * h + 1] = jnp.where(half, 0.0, q).astype(MM_DTYPE)
    s_loc = lax.broadcasted_iota(I32, (T, T), 0)
    t_loc = lax.broadcasted_iota(I32, (T, T), 1)
    acc_ref[...] = jnp.zeros(acc_ref.shape, F32)

    def logits(kj, c):
        h = c // 2
        off = jnp.minimum(qi - kj, N_OFFSETS - 1)
        return bias_ref[h, off] + jnp.dot(k_ref[kj, :, h * dv:(h + 1) * dv], qz_ref[c],
                                          preferred_element_type=F32)

    for c in range(n_chain):
        s_ref[c] = logits(0, c)

    def step(kj, ms, diagonal):
        new_m = []
        for c in range(n_chain):
            h = c // 2
            s = s_ref[c]
            if diagonal:
                s = jnp.where(s_loc <= t_loc, s, NEG)
            p, alpha, m_new = _softmax_step(s, ms[c])
            if not diagonal:
                s_ref[c] = logits(kj + 1, c)
            vT = _with_ones_row(vT_ref[kj, h * dv:(h + 1) * dv, :])
            acc_ref[c] = alpha * acc_ref[c] + jnp.dot(vT, p, preferred_element_type=F32)
            new_m.append(m_new)
        return tuple(new_m)

    ms = lax.fori_loop(0, qi, lambda kj, cr: step(kj, cr, False), _initial_max(n_chain, T))
    step(qi, ms, True)

    lr = lam_ref[...]
    lam = (jnp.exp(jnp.sum(lr[0:1, :] * lr[1:2, :], axis=1, keepdims=True))
           - jnp.exp(jnp.sum(lr[2:3, :] * lr[3:4, :], axis=1, keepdims=True)) + lam_init)

    def normalised(c):
        return acc_ref[c, 0:dv, :] / acc_ref[c, dv:dv + 1, :]

    for h in range(B_HEADS):
        attn = normalised(2 * h) - lam * normalised(2 * h + 1)
        y = attn * lax.rsqrt(jnp.mean(attn * attn, axis=0, keepdims=True) + EPS)
        oT_ref[h * dv:(h + 1) * dv, :] = y * gsub_ref[...] * (1.0 - lam_init)
    out_ref[...] = oT_ref[...].T.astype(out_ref.dtype)


def _diff(bqT, bk, bvT, bias_b, lam_rows, gsub, lam_init):
    bsz, _, seq = bqT.shape
    T = ATT_TILE
    nk = seq // T
    dv = 2 * HEAD_DIM
    qspec = pl.BlockSpec((None, B_HEADS * dv, T), lambda b, i: (b, 0, i))
    return pl.pallas_call(
        functools.partial(_diff_kernel, lam_init=lam_init),
        out_shape=jax.ShapeDtypeStruct((bsz, seq, B_HEADS * dv), MM_DTYPE),
        grid=(bsz, nk),
        in_specs=[qspec,
                  pl.BlockSpec((None, nk, T, B_HEADS * dv), lambda b, i: (b, 0, 0, 0)),
                  pl.BlockSpec((None, nk, B_HEADS * dv, T), lambda b, i: (b, 0, 0, 0)),
                  _const_spec(bias_b.shape), _const_spec((4, HEAD_DIM)), _const_spec((dv, 1))],
        out_specs=pl.BlockSpec((None, T, B_HEADS * dv), lambda b, i: (b, i, 0)),
        scratch_shapes=[pltpu.VMEM((2 * B_HEADS, dv, T), MM_DTYPE),
                        pltpu.VMEM((2 * B_HEADS, dv + SUM_ROWS, T), F32),
                        pltpu.VMEM((2 * B_HEADS, T, T), F32),
                        pltpu.VMEM((B_HEADS * dv, T), F32)],
        compiler_params=_cparams(2), name="diff_attention",
    )(bqT, bk, bvT, bias_b, lam_rows, gsub)


def _dil_kernel(cur_ref, halo_ref, bias_ref, out_ref, lse_ref, *, tq):
    n = C_BAND
    wid = C_HPG * HEAD_DIM
    halo_lo = jnp.where(pl.program_id(2) == 0, jnp.int32(n), jnp.int32(0))
    i = lax.broadcasted_iota(I32, (n, 2 * n), 0)
    j = lax.broadcasted_iota(I32, (n, 2 * n), 1)
    lane_head = lax.broadcasted_iota(I32, (n, wid), 1) // HEAD_DIM
    in_head = [lane_head == h for h in range(C_HPG)]
    band = jnp.where(j >= i, jnp.where(j <= i + n, 0.0, NEG), NEG)
    band0 = jnp.where(j >= jnp.maximum(i, halo_lo), jnp.where(j <= i + n, 0.0, NEG), NEG)
    bias = [bias_ref[h] + band for h in range(C_HPG)]
    bias0 = [bias_ref[h] + band0 for h in range(C_HPG)]

    def band_rows(c, lo, hi):
        if c == 0:
            return jnp.concatenate([halo_ref[:, lo:hi], cur_ref[0:n, lo:hi]], axis=0)
        return cur_ref[(c - 1) * n:(c + 1) * n, lo:hi]

    logits = []
    for c in range(tq // n):
        q = cur_ref[c * n:(c + 1) * n, 0:wid].astype(F32) * ATTN_SCALE
        keys = band_rows(c, wid, 2 * wid)
        for h in range(C_HPG):
            qh = jnp.where(in_head[h], q, 0.0).astype(MM_DTYPE)
            s = lax.dot_general(qh, keys, (((1,), (1,)), ((), ())), preferred_element_type=F32)
            logits.append(s + (bias0 if c == 0 else bias)[h])
    for c in range(tq // n):
        vals = band_rows(c, 2 * wid, 3 * wid)
        out = jnp.zeros((n, wid), F32)
        lse = jnp.zeros((n, wid), F32)
        for h in range(C_HPG):
            s = logits[c * C_HPG + h]
            m = jnp.max(s, axis=1, keepdims=True)
            p = jnp.exp(s - m)
            den = jnp.sum(p, axis=1, keepdims=True)
            o = jnp.dot(p.astype(MM_DTYPE), vals, preferred_element_type=F32) * (1.0 / den)
            out = jnp.where(in_head[h], o, out)
            lse = jnp.where(in_head[h], m + jnp.log(den), lse)
        out_ref[c * n:(c + 1) * n, :] = out
        lse_ref[c * n:(c + 1) * n, :] = lse


def _dilated_group(cg, bias_g, g):
    bsz, dil, m, gw = cg.shape
    wid = C_HPG * HEAD_DIM
    n = C_BAND
    assert m % n == 0 and gw == 3 * wid
    tq = min(m, 512)
    cur = pl.BlockSpec((None, None, tq, gw), lambda b, r, i: (b, r, i, 0))
    halo = pl.BlockSpec((None, None, n, gw),
                        lambda b, r, i: (b, r, jnp.maximum(i * (tq // n) - 1, 0), 0))
    outspec = pl.BlockSpec((None, None, tq, wid), lambda b, r, i: (b, r, i, 0))
    return pl.pallas_call(
        functools.partial(_dil_kernel, tq=tq),
        out_shape=[jax.ShapeDtypeStruct((bsz, dil, m, wid), F32)] * 2,
        grid=(bsz, dil, m // tq),
        in_specs=[cur, halo, pl.BlockSpec((C_HPG, n, 2 * n), lambda b, r, i: (0, 0, 0))],
        out_specs=[outspec, outspec],
        compiler_params=_cparams(3), name=f"dilated_group{g}",
    )(cg, cg, bias_g)


def _merge_kernel(x_ref, mod_ref, g1_ref, oa_ref, ob_ref,
                  c0_ref, c1_ref, c2_ref, s0_ref, s1_ref, s2_ref,
                  wza, wzb, wzc, wba, wbb, wbc, wo, out_ref, tok_ref):
    x = x_ref[...]
    tm = x.shape[0]
    h = _rms(x) * g1_ref[...]
    h = h * (1.0 + mod_ref[1:2, :]) + mod_ref[0:1, :]
    hb = h.astype(MM_DTYPE)

    def token_order(k, ref):
        dil, _, w = ref.shape
        if dil == 1:
            return ref[0]
        n_chunk = w // 128
        for r in range(dil):
            for j in range(n_chunk):
                tok_ref[k * n_chunk + j, pl.ds(r, tm // dil, stride=dil), :] = ref[
                    r, :, j * 128:(j + 1) * 128]
        return jnp.concatenate([tok_ref[k * n_chunk + j] for j in range(n_chunk)], axis=1)

    s0, s1, s2 = s0_ref[0], token_order(0, s1_ref), token_order(1, s2_ref)
    c0, c1, c2 = c0_ref[0], token_order(2, c1_ref), token_order(3, c2_ref)
    mx = jnp.maximum(jnp.maximum(s0, s1), s2)
    e0, e1, e2 = jnp.exp(s0 - mx), jnp.exp(s1 - mx), jnp.exp(s2 - mx)
    oc = (e0 * c0 + e1 * c1 + e2 * c2) / (e0 + e1 + e2)

    def gated(wz, o, wb):
        z = jnp.dot(hb, wz[...], preferred_element_type=F32)
        return jax.nn.sigmoid(z) * jnp.dot(o, wb[...], preferred_element_type=F32)

    merged = (gated(wza, oa_ref[...], wba) + gated(wzb, ob_ref[...], wbb)
              + gated(wzc, oc.astype(MM_DTYPE), wbc))
    y = jnp.dot(merged.astype(MM_DTYPE), wo[...], preferred_element_type=F32)
    out_ref[...] = x + mod_ref[2:3, :] * y


def _merge(x2d, mod_l, g1, oa, ob, ocs, lses, ws, seq):
    n, d = x2d.shape
    tm = ROW_TILE
    per_b = seq // tm
    row = lambda wd: pl.BlockSpec((tm, wd), lambda i: (i, 0))
    wid = C_HPG * HEAD_DIM
    res = [pl.BlockSpec((None, dil, tm // dil, wid), lambda i: (i // per_b, 0, i % per_b, 0))
           for _, dil in C_GROUPS]
    in_specs = [row(d), pl.BlockSpec((None, 6, d), lambda i: (i // per_b, 0, 0)), _const_spec((1, d)),
                row(oa.shape[1]), row(ob.shape[1])] + res + res
    in_specs += [_const_spec(w.shape) for w in ws]
    return pl.pallas_call(
        _merge_kernel, out_shape=jax.ShapeDtypeStruct((n, d), F32), grid=(n // tm,),
        in_specs=in_specs, out_specs=row(d),
        scratch_shapes=[pltpu.VMEM((4 * wid // 128, tm, 128), F32)],
        compiler_params=_cparams(1), name="gated_merge",
    )(x2d, mod_l, g1, oa, ob, *ocs, *lses, *ws)


def _ffn_kernel(x_ref, mod_ref, g2_ref, gf_ref, wgu, wd, out_ref, *, chunk, final_norm):
    x = x_ref[...]
    h = _rms(x) * g2_ref[...]
    h = h * (1.0 + mod_ref[4:5, :]) + mod_ref[3:4, :]
    hb = h.astype(MM_DTYPE)
    acc = jnp.zeros(x.shape, F32)
    for c in range(D_FF // chunk):
        fg = jnp.dot(hb, wgu[:, c * chunk:(c + 1) * chunk], preferred_element_type=F32)
        fu = jnp.dot(hb, wgu[:, D_FF + c * chunk:D_FF + (c + 1) * chunk],
                     preferred_element_type=F32)
        act = (fg * jax.nn.sigmoid(fg) * fu).astype(MM_DTYPE)
        acc = acc + jnp.dot(act, wd[c * chunk:(c + 1) * chunk, :], preferred_element_type=F32)
    y = x + mod_ref[5:6, :] * acc
    if final_norm:
        y = _rms(y) * gf_ref[...]
    out_ref[...] = y


def _ffn(x2d, mod_l, g2, gf, wgu, wd, seq, final_norm):
    n, d = x2d.shape
    tm = ROW_TILE
    per_b = seq // tm
    row = pl.BlockSpec((tm, d), lambda i: (i, 0))
    return pl.pallas_call(
        functools.partial(_ffn_kernel, chunk=256, final_norm=final_norm),
        out_shape=jax.ShapeDtypeStruct((n, d), F32), grid=(n // tm,),
        in_specs=[row, pl.BlockSpec((None, 6, d), lambda i: (i // per_b, 0, 0)),
                  _const_spec((1, d)), _const_spec((1, d)),
                  _const_spec(wgu.shape), _const_spec(wd.shape)],
        out_specs=row,
        compiler_params=_cparams(1), name="swiglu_ffn",
    )(x2d, mod_l, g2, gf, wgu, wd)


def kernel(x, c, w_ada, b_ada, g_norm1, w_in, w_uk, w_uv, g_kv, lam_q1, lam_k1, lam_q2, lam_k2,
           g_subln, w_branch_a, w_branch_b, w_branch_c, w_out, g_norm2, w_gate_up, w_down,
           rel_bias, g_final):
    bsz, seq, d = x.shape
    depth = w_ada.shape[0]
    T = ATT_TILE
    nk = seq // T
    assert d == D_MODEL and seq % T == 0 and seq % ROW_TILE == 0
    n = bsz * seq
    cast = lambda w: w.astype(MM_DTYPE)

    thresholds = _bucket_thresholds(seq + 2 * C_BAND * C_GROUPS[-1][1])
    assert seq <= N_OFFSETS * T or (N_OFFSETS - 2) * T + 1 >= thresholds[-1]
    tab = rel_bias.reshape(-1)
    bias_a = _bias_att_tiles(tab, 0, A_HEADS, T, thresholds)
    bias_b = _bias_att_tiles(tab, A_HEADS, B_HEADS, T, thresholds)
    bias_c = _bias_dil_tiles(tab, A_HEADS + B_HEADS, thresholds)

    mod = _modulation(c, w_ada, b_ada).reshape(depth, bsz, 6, d)

    splits = (A_HEADS * HEAD_DIM, A_LATENT, IDX_HEADS * IDX_DIM, IDX_DIM, IDX_HEADS,
              B_HEADS * 2 * HEAD_DIM, B_HEADS * 2 * HEAD_DIM, B_HEADS * 2 * HEAD_DIM,
              C_HEADS * HEAD_DIM, C_HEADS * HEAD_DIM, C_HEADS * HEAD_DIM, d, d, d)
    offs = np.concatenate([[0], np.cumsum(splits)])
    seg = lambda w, k: w[:, int(offs[k]):int(offs[k + 1])]

    x2d = x.reshape(n, d)
    for l in range(depth):
        wl = w_in[l]
        castT = lambda w: w.T.astype(MM_DTYPE)
        wT_iw = jnp.pad(seg(wl, 4).T, ((0, 16 - IDX_HEADS), (0, 0)))
        wid = C_HPG * HEAD_DIM
        w_c = jnp.concatenate([seg(wl, k)[:, g * wid:(g + 1) * wid]
                               for g in range(len(C_GROUPS)) for k in (8, 9, 10)], axis=1)
        ws_in = [castT(seg(wl, 0)), castT(seg(wl, 2)), cast(wT_iw), castT(seg(wl, 5)), castT(seg(wl, 7)),
                 cast(seg(wl, 3)), cast(seg(wl, 1)), cast(seg(wl, 6)), cast(w_c)]
        g1 = g_norm1[l].reshape(1, d)
        (aqT, iqT, iwT, bqT, bvT, ik, kv, kvT, bk, cg0, cg1, cg2) = _in_proj(
            x2d, mod[l], g1, g_kv[l].reshape(1, A_LATENT), ws_in, bsz, seq)

        o_a = _dsa(aqT, iqT, iwT,
                   ik.reshape(bsz, nk, T, IDX_DIM), kv.reshape(bsz, nk, T, A_LATENT), kvT,
                   bias_a, cast(w_uk[l].transpose(0, 2, 1)), cast(w_uv[l].transpose(0, 2, 1)))
        o_a = o_a.reshape(n, -1)

        lam_init = 0.8 - 0.6 * math.exp(-0.3 * l)
        lam_rows = jnp.stack([lam_q1[l], lam_k1[l], lam_q2[l], lam_k2[l]])
        dv = 2 * HEAD_DIM
        o_b = _diff(bqT, bk.reshape(bsz, nk, T, B_HEADS * dv), bvT, bias_b, lam_rows,
                    g_subln[l].reshape(dv, 1), lam_init)
        o_b = o_b.reshape(n, -1)

        ocs, lses = [], []
        for g, (cg, (window, dil)) in enumerate(zip((cg0, cg1, cg2), C_GROUPS)):
            assert window // dil == C_BAND
            o, s = _dilated_group(cg, bias_c[g * C_HPG:(g + 1) * C_HPG], g)
            ocs.append(o)
            lses.append(s)

        ws_merge = [cast(seg(wl, 11)), cast(seg(wl, 12)), cast(seg(wl, 13)),
                    cast(w_branch_a[l]), cast(w_branch_b[l]), cast(w_branch_c[l]), cast(w_out[l])]
        x2d = _merge(x2d, mod[l], g1, o_a, o_b, ocs, lses, ws_merge, seq)
        x2d = _ffn(x2d, mod[l], g_norm2[l].reshape(1, d), g_final.reshape(1, d),
                   cast(w_gate_up[l]), cast(w_down[l]), seq, final_norm=(l == depth - 1))
    return x2d.reshape(bsz, seq, d)
```

```python
import functools
import math

import numpy as np
import jax
import jax.numpy as jnp
from jax import lax
from jax.experimental import pallas as pl
from jax.experimental.pallas import tpu as pltpu

D_MODEL = 1024
HEAD_DIM = 64
ATTN_SCALE = HEAD_DIM ** -0.5
LOG2E = math.log2(math.e)
A_HEADS = 8
A_LATENT = 128
IDX_HEADS = 8
IDX_DIM = 64
IDX_SCALE = (IDX_HEADS * IDX_DIM) ** -0.5
TOPK_MAX = 256
B_HEADS = 4
C_GROUPS = ((128, 1), (512, 4), (2048, 16))
C_HPG = 4
C_HEADS = C_HPG * len(C_GROUPS)
N_BUCKETS = 32
MAX_DISTANCE = 2048
N_BIAS_HEADS = A_HEADS + B_HEADS + C_HEADS
D_FF = -(-8 * D_MODEL // (3 * 256)) * 256
EPS = 1e-6

MM_DTYPE = jnp.bfloat16
F32 = jnp.float32
I32 = jnp.int32

ATT_TILE = 256
N_OFFSETS = 8
C_BAND = 128
ROW_TILE = 512
NEG = -1e30
INT_MIN = -2 ** 31
VMEM_LIMIT = 56 * 1024 * 1024


def _cparams(n_axes, vmem=VMEM_LIMIT):
    return pltpu.CompilerParams(dimension_semantics=("arbitrary",) * n_axes,
                                vmem_limit_bytes=vmem)


def _const_spec(shape):
    nd = len(shape)
    return pl.BlockSpec(shape, lambda *_: (0,) * nd, pipeline_mode=pl.Buffered(1))


def _bucket_thresholds(max_dist):
    n = np.arange(max_dist + 1)
    max_exact = N_BUCKETS // 2
    nf = np.maximum(n, 1).astype(np.float32)
    large = max_exact + (np.log(nf / np.float32(max_exact))
                         / np.float32(math.log(MAX_DISTANCE / max_exact))
                         * np.float32(N_BUCKETS - max_exact)).astype(np.int32)
    large = np.minimum(large, N_BUCKETS - 1)
    bucket = np.where(n < max_exact, n, large)
    assert np.all(np.diff(bucket) >= 0)
    thr = []
    for k in range(1, N_BUCKETS):
        idx = np.nonzero(bucket >= k)[0]
        thr.append(int(idx[0]) if idx.size else None)
    return thr


def _bias_from_dist(dist, tab_ref, col, thresholds):
    b = jnp.full(dist.shape, tab_ref[col], F32)
    for k, thr in enumerate(thresholds, start=1):
        if thr is None:
            break
        b = jnp.where(dist >= thr, tab_ref[k * N_BIAS_HEADS + col], b)
    return b


def _bias_att_kernel(tab_ref, out_ref, *, head0, tile, thresholds):
    h = pl.program_id(0)
    o = pl.program_id(1)
    row = lax.broadcasted_iota(I32, (tile, tile), 0)
    colq = lax.broadcasted_iota(I32, (tile, tile), 1)
    dist = jnp.maximum(o * tile + colq - row, 0)
    out_ref[...] = _bias_from_dist(dist, tab_ref, head0 + h, thresholds) * LOG2E


def _bias_att_tiles(tab, head0, n_heads, tile, thresholds):
    return pl.pallas_call(
        functools.partial(_bias_att_kernel, head0=head0, tile=tile, thresholds=thresholds),
        out_shape=jax.ShapeDtypeStruct((n_heads, N_OFFSETS, tile, tile), F32),
        grid=(n_heads, N_OFFSETS),
        in_specs=[pl.BlockSpec(memory_space=pltpu.SMEM)],
        out_specs=pl.BlockSpec((None, None, tile, tile), lambda h, o: (h, o, 0, 0)),
        compiler_params=_cparams(2),
        name="bias_att_tiles",
    )(tab)


def _bias_dil_kernel(tab_ref, out_ref, *, head0, thresholds):
    h = pl.program_id(0)
    g = h // C_HPG
    dil = jnp.where(g == 0, C_GROUPS[0][1], jnp.where(g == 1, C_GROUPS[1][1], C_GROUPS[2][1]))
    i = lax.broadcasted_iota(I32, (C_BAND, 2 * C_BAND), 0)
    j = lax.broadcasted_iota(I32, (C_BAND, 2 * C_BAND), 1)
    dist = jnp.maximum((i - j + C_BAND) * dil, 0)
    out_ref[...] = _bias_from_dist(dist, tab_ref, head0 + h, thresholds)


def _bias_dil_tiles(tab, head0, thresholds):
    return pl.pallas_call(
        functools.partial(_bias_dil_kernel, head0=head0, thresholds=thresholds),
        out_shape=jax.ShapeDtypeStruct((C_HEADS, C_BAND, 2 * C_BAND), F32),
        grid=(C_HEADS,),
        in_specs=[pl.BlockSpec(memory_space=pltpu.SMEM)],
        out_specs=pl.BlockSpec((None, C_BAND, 2 * C_BAND), lambda h: (h, 0, 0)),
        compiler_params=_cparams(1),
        name="bias_dil_tiles",
    )(tab)


def _mod_kernel(c_ref, w_ref, b_ref, out_ref):
    c = c_ref[...]
    ca = (c * jax.nn.sigmoid(c)).astype(MM_DTYPE)
    out_ref[...] = jnp.dot(ca, w_ref[...].astype(MM_DTYPE), preferred_element_type=F32) + b_ref[...]


def _modulation(c, w_ada, b_ada):
    depth, d, wid = w_ada.shape
    bsz = c.shape[0]
    tn = 1536
    assert wid % tn == 0
    return pl.pallas_call(
        _mod_kernel,
        out_shape=jax.ShapeDtypeStruct((depth, bsz, wid), F32),
        grid=(depth, wid // tn),
        in_specs=[pl.BlockSpec((bsz, d), lambda l, j: (0, 0)),
                  pl.BlockSpec((None, d, tn), lambda l, j: (l, 0, j)),
                  pl.BlockSpec((None, 1, tn), lambda l, j: (l, 0, j))],
        out_specs=pl.BlockSpec((None, bsz, tn), lambda l, j: (l, 0, j)),
        compiler_params=_cparams(2),
        name="adaln_modulation",
    )(c, w_ada, b_ada.reshape(depth, 1, wid))


def _rms(x):
    return x * lax.rsqrt(jnp.mean(x * x, axis=-1, keepdims=True) + EPS)


def _in_kernel(x_ref, mod_ref, g1_ref, gkv_ref,
               wT_aq, wT_iq, wT_iw, wT_bq, wT_bv, w_ik, w_kv, w_bk, w_c,
               o_aqT, o_iqT, o_iwT, o_bqT, o_bvT, o_ik, o_kv, o_kvT, o_bk, o_c0, o_c1, o_c2,
               c_scr):
    T = ATT_TILE
    tm = x_ref.shape[0]
    h = _rms(x_ref[...]) * g1_ref[...]
    h = h * (1.0 + mod_ref[1:2, :]) + mod_ref[0:1, :]
    hb = h.astype(MM_DTYPE)

    def mm(w):
        return jnp.dot(hb, w[...], preferred_element_type=F32)

    def mm_t(wT):
        return lax.dot_general(wT[...], hb, (((1,), (1,)), ((), ())), preferred_element_type=F32)

    o_aqT[...] = mm_t(wT_aq).astype(o_aqT.dtype)
    o_iqT[...] = mm_t(wT_iq).astype(o_iqT.dtype)
    o_bqT[...] = (mm_t(wT_bq) * (ATTN_SCALE * LOG2E)).astype(o_bqT.dtype)
    o_iwT[...] = (mm_t(wT_iw) * IDX_SCALE)[:IDX_HEADS]
    bvT = mm_t(wT_bv).astype(o_bvT.dtype)
    kv = _rms(mm(w_kv)) * gkv_ref[...]
    kvT = kv.T.astype(o_kvT.dtype)
    for j in range(tm // T):
        o_bvT[j] = bvT[:, j * T:(j + 1) * T]
        o_kvT[j] = kvT[:, j * T:(j + 1) * T]
    o_kv[...] = kv.astype(o_kv.dtype)
    o_ik[...] = mm(w_ik).astype(o_ik.dtype)
    o_bk[...] = mm(w_bk).astype(o_bk.dtype)

    yc = mm(w_c)
    n_chunk = yc.shape[1] // 128
    for j in range(n_chunk):
        c_scr[j] = yc[:, j * 128:(j + 1) * 128]
    per_group = n_chunk // len(C_GROUPS)
    for g, o_c in enumerate((o_c0, o_c1, o_c2)):
        dil = C_GROUPS[g][1]
        for r in range(dil):
            for jj in range(per_group):
                o_c[r, :, jj * 128:(jj + 1) * 128] = c_scr[
                    g * per_group + jj, pl.ds(r, tm // dil, stride=dil), :].astype(o_c.dtype)


def _in_proj(x2d, mod_l, g1, gkv, ws, bsz, seq):
    n, d = x2d.shape
    tm = ROW_TILE
    T = ATT_TILE
    per_b = seq // tm
    nk = seq // T
    hd = A_HEADS * HEAD_DIM
    bw = B_HEADS * 2 * HEAD_DIM
    gw = 3 * C_HPG * HEAD_DIM
    in_specs = [pl.BlockSpec((tm, d), lambda i: (i, 0)),
                pl.BlockSpec((None, 6, d), lambda i: (i // per_b, 0, 0)),
                _const_spec((1, d)), _const_spec((1, A_LATENT))]
    in_specs += [_const_spec(w.shape) for w in ws]

    def tspec(rows):
        return pl.BlockSpec((None, rows, tm), lambda i: (i // per_b, 0, i % per_b))

    def tile_tspec(rows):
        return pl.BlockSpec((None, tm // T, rows, T), lambda i: (i // per_b, i % per_b, 0, 0))

    def rspec(wd):
        return pl.BlockSpec((tm, wd), lambda i: (i, 0))

    def cspec(dil):
        return pl.BlockSpec((None, dil, tm // dil, gw), lambda i: (i // per_b, 0, i % per_b, 0))

    sds = jax.ShapeDtypeStruct
    out_specs = [tspec(hd), tspec(IDX_HEADS * IDX_DIM), tspec(IDX_HEADS), tspec(bw),
                 tile_tspec(bw), rspec(IDX_DIM), rspec(A_LATENT), tile_tspec(A_LATENT), rspec(bw)]
    out_shape = [sds((bsz, hd, seq), MM_DTYPE), sds((bsz, IDX_HEADS * IDX_DIM, seq), MM_DTYPE),
                 sds((bsz, IDX_HEADS, seq), F32), sds((bsz, bw, seq), MM_DTYPE),
                 sds((bsz, nk, bw, T), MM_DTYPE), sds((n, IDX_DIM), MM_DTYPE),
                 sds((n, A_LATENT), MM_DTYPE), sds((bsz, nk, A_LATENT, T), MM_DTYPE),
                 sds((n, bw), MM_DTYPE)]
    for _, dil in C_GROUPS:
        out_specs.append(cspec(dil))
        out_shape.append(sds((bsz, dil, seq // dil, gw), MM_DTYPE))
    return pl.pallas_call(
        _in_kernel, out_shape=out_shape, grid=(n // tm,),
        in_specs=in_specs, out_specs=out_specs,
        scratch_shapes=[pltpu.VMEM((3 * gw // 128, tm, 128), F32)],
        compiler_params=_cparams(1), name="in_proj",
    )(x2d, mod_l, g1, gkv, *ws)


def _softmax_step(s, m_old):
    m_new = jnp.maximum(m_old, jnp.max(s, axis=0, keepdims=True))
    p = jnp.exp2(s - m_new)
    alpha = jnp.exp2(m_old - m_new)
    return p.astype(MM_DTYPE), alpha, m_new


def _initial_max(n_chains, tile):
    return tuple(jnp.full((1, tile), NEG, F32) for _ in range(n_chains))


SUM_ROWS = 16


def _with_ones_row(vT):
    row = lax.broadcasted_iota(I32, (SUM_ROWS, vT.shape[1]), 0)
    ones = jnp.where(row == 0, 1.0, 0.0).astype(vT.dtype)
    return jnp.concatenate([vT, ones], axis=0)


def _bit_transpose32(words):
    a = list(words)
    j, mask = 16, 0x0000FFFF
    while j:
        k = 0
        while k < 32:
            t = (a[k] ^ lax.shift_right_logical(a[k + j], jnp.int32(j))) & jnp.int32(mask)
            a[k] = a[k] ^ t
            a[k + j] = a[k + j] ^ lax.shift_left(t, jnp.int32(j))
            k = (k + j + 1) & ~j
        j >>= 1
        mask = (mask ^ (mask << j)) & 0xFFFFFFFF
    return a


def _dsa_kernel(aqT_ref, iqT_ref, iwT_ref, kidx_ref, kv_ref, kvT_ref, bias_ref, wukT_ref, wuvT_ref,
                out_ref, keys_ref, planes_ref, qlat_ref, acc_ref, s_ref, oT_ref, *, topk, idx_bits):
    T = ATT_TILE
    qi = pl.program_id(1)
    nk = qi + 1

    for h in range(A_HEADS):
        q = jnp.dot(wukT_ref[h], aqT_ref[h * HEAD_DIM:(h + 1) * HEAD_DIM, :],
                    preferred_element_type=F32) * (ATTN_SCALE * LOG2E)
        qlat_ref[h] = q.astype(qlat_ref.dtype)

    s_loc = lax.broadcasted_iota(I32, (T, T), 0)
    t_loc = lax.broadcasted_iota(I32, (T, T), 1)

    def score_tile(kj, diagonal):
        kt = kidx_ref[kj]
        acc = jnp.zeros((T, T), F32)
        for h in range(IDX_HEADS):
            s = jnp.dot(kt, iqT_ref[h * IDX_DIM:(h + 1) * IDX_DIM, :], preferred_element_type=F32)
            acc = acc + jnp.maximum(s, 0.0) * iwT_ref[h:h + 1, :]
        bits = lax.bitcast_convert_type(acc, I32)
        key = jnp.where(bits < 0, bits ^ jnp.int32(0x7FFFFFFF), bits)
        if diagonal:
            key = jnp.where(s_loc <= t_loc, key, jnp.int32(INT_MIN))
        keys_ref[kj] = key
        planes = _bit_transpose32([key[8 * i:8 * (i + 1), :] ^ jnp.int32(INT_MIN) for i in range(32)])
        for b in range(32):
            planes_ref[kj, b] = planes[b]

    def off_diagonal(kj, carry):
        score_tile(kj, False)
        return carry

    lax.fori_loop(0, qi, off_diagonal, 0)
    score_tile(qi, True)

    n_tiles = planes_ref.shape[0]

    @pl.when(qi == 0)
    def _():
        def clear_planes(kj, carry):
            for b in range(32):
                planes_ref[kj, b] = jnp.zeros((8, T), I32)
            return carry

        lax.fori_loop(1, n_tiles, clear_planes, 0)

    one = jnp.int32(1)
    nil = jnp.int32(0)
    zero = jnp.zeros((1, T), I32)

    n_rows = n_tiles * 8
    tile_of_row = lax.broadcasted_iota(I32, (n_rows, T), 0) // 8
    tied0 = jnp.where(tile_of_row < nk, jnp.int32(-1), nil)

    def bit_step(b, carry):
        tied, n_gt, kth_u = carry
        ones = tied & planes_ref[:, b].reshape(n_rows, T)
        n1 = jnp.sum(lax.population_count(ones), axis=0, keepdims=True)
        take = (n_gt + n1) >= topk
        tied = jnp.where(take, ones, tied ^ ones)
        n_gt = jnp.where(take, n_gt, n_gt + n1)
        kth_u = jnp.where(take, kth_u | lax.shift_left(one, jnp.int32(31) - b), kth_u)
        return tied, n_gt, kth_u

    tied, n_gt, kth_u = lax.fori_loop(0, 32, bit_step, (tied0, zero, zero))
    n_eq = jnp.sum(lax.population_count(tied), axis=0, keepdims=True)
    n_ge = jnp.where(kth_u == nil, n_gt, n_gt + n_eq)
    kth = jnp.maximum(kth_u ^ jnp.int32(INT_MIN), jnp.int32(INT_MIN + 1))

    def count(hit_fn):
        def body(kj, c):
            hit = hit_fn(keys_ref[kj], kj)
            return c + jnp.sum(hit.reshape(T // 8, 8, T), axis=0)
        c = lax.fori_loop(0, nk, body, jnp.zeros((8, T), I32))
        return jnp.sum(c, axis=0, keepdims=True)

    @pl.when(jnp.max(n_ge) > topk)
    def _():
        need = topk - n_gt

        def pos_step(i, cut):
            cand = cut | jnp.left_shift(one, jnp.int32(idx_bits - 1) - i)
            c = count(lambda kk, kj: jnp.where(
                kk == kth, jnp.where((kj * T + s_loc) < cand, one, nil), nil))
            return jnp.where(c < need, cand, cut)

        cut = lax.fori_loop(0, idx_bits, pos_step, zero)

        def demote(kj, carry):
            kk = keys_ref[kj]
            lowered = jnp.where((kj * T + s_loc) > cut, kth - one, kk)
            keys_ref[kj] = jnp.where(kk == kth, lowered, kk)
            return carry

        lax.fori_loop(0, nk, demote, 0)

    acc_ref[...] = jnp.zeros(acc_ref.shape, F32)

    def logits(kj, h):
        off = jnp.minimum(qi - kj, N_OFFSETS - 1)
        return bias_ref[h, off] + jnp.dot(kv_ref[kj], qlat_ref[h], preferred_element_type=F32)

    for h in range(A_HEADS):
        s_ref[h] = logits(0, h)

    def attend(kj, ms):
        unselected = jnp.where(keys_ref[kj] >= kth, 0.0, NEG)
        kvTt = _with_ones_row(kvT_ref[kj])
        kj_next = jnp.minimum(kj + 1, nk - 1)
        new_m = []
        for h in range(A_HEADS):
            s = s_ref[h] + unselected
            p, alpha, m_new = _softmax_step(s, ms[h])
            s_ref[h] = logits(kj_next, h)
            acc_ref[h] = alpha * acc_ref[h] + jnp.dot(kvTt, p, preferred_element_type=F32)
            new_m.append(m_new)
        return tuple(new_m)

    lax.fori_loop(0, nk, attend, _initial_max(A_HEADS, T))

    for h in range(A_HEADS):
        o_lat = (acc_ref[h, 0:A_LATENT, :] / acc_ref[h, A_LATENT:A_LATENT + 1, :]).astype(MM_DTYPE)
        oT_ref[h * HEAD_DIM:(h + 1) * HEAD_DIM, :] = jnp.dot(
            wuvT_ref[h], o_lat, preferred_element_type=F32)
    out_ref[...] = oT_ref[...].T.astype(out_ref.dtype)


def _dsa(aqT, iqT, iwT, kidx, kv, kvT, bias_a, wukT, wuvT):
    bsz, _, seq = aqT.shape
    T = ATT_TILE
    nk = seq // T
    topk = min(TOPK_MAX, seq // 4)
    idx_bits = int(math.log2(seq))
    assert 2 ** idx_bits == seq
    qspec = lambda rows: pl.BlockSpec((None, rows, T), lambda b, i: (b, 0, i))
    kspec = lambda a, c: pl.BlockSpec((None, nk, a, c), lambda b, i: (b, 0, 0, 0))
    return pl.pallas_call(
        functools.partial(_dsa_kernel, topk=topk, idx_bits=idx_bits),
        out_shape=jax.ShapeDtypeStruct((bsz, seq, A_HEADS * HEAD_DIM), MM_DTYPE),
        grid=(bsz, nk),
        in_specs=[qspec(A_HEADS * HEAD_DIM), qspec(IDX_HEADS * IDX_DIM), qspec(IDX_HEADS),
                  kspec(T, IDX_DIM), kspec(T, A_LATENT), kspec(A_LATENT, T),
                  _const_spec(bias_a.shape), _const_spec(wukT.shape), _const_spec(wuvT.shape)],
        out_specs=pl.BlockSpec((None, T, A_HEADS * HEAD_DIM), lambda b, i: (b, i, 0)),
        scratch_shapes=[pltpu.VMEM((nk, T, T), I32),
                        pltpu.VMEM((nk, 33, T // 32, T), I32),
                        pltpu.VMEM((A_HEADS, A_LATENT, T), MM_DTYPE),
                        pltpu.VMEM((A_HEADS, A_LATENT + SUM_ROWS, T), F32),
                        pltpu.VMEM((A_HEADS, T, T), F32),
                        pltpu.VMEM((A_HEADS * HEAD_DIM, T), F32)],
        compiler_params=_cparams(2), name="dsa_attention",
    )(aqT, iqT, iwT, kidx, kv, kvT, bias_a, wukT, wuvT)


def _diff_kernel(qT_ref, k_ref, vT_ref, bias_ref, lam_ref, gsub_ref, out_ref,
                 qz_ref, acc_ref, s_ref, oT_ref, *, lam_init):
    T = ATT_TILE
    dv = 2 * HEAD_DIM
    n_chain = 2 * B_HEADS
    qi = pl.program_id(1)
    half = lax.broadcasted_iota(I32, (dv, T), 0) < HEAD_DIM
    for h in range(B_HEADS):
        q = qT_ref[h * dv:(h + 1) * dv, :].astype(F32)
        qz_ref[2 * h] = jnp.where(half, q, 0.0).astype(MM_DTYPE)
        qz_ref[2 * h + 1] = jnp.where(half, 0.0, q).astype(MM_DTYPE)
    s_loc = lax.broadcasted_iota(I32, (T, T), 0)
    t_loc = lax.broadcasted_iota(I32, (T, T), 1)
    acc_ref[...] = jnp.zeros(acc_ref.shape, F32)

    def logits(kj, c):
        h = c // 2
        off = jnp.minimum(qi - kj, N_OFFSETS - 1)
        return bias_ref[h, off] + jnp.dot(k_ref[kj, :, h * dv:(h + 1) * dv], qz_ref[c],
                                          preferred_element_type=F32)

    for c in range(n_chain):
        s_ref[c] = logits(0, c)

    def step(kj, ms, diagonal):
        new_m = []
        for c in range(n_chain):
            h = c // 2
            s = s_ref[c]
            if diagonal:
                s = jnp.where(s_loc <= t_loc, s, NEG)
            p, alpha, m_new = _softmax_step(s, ms[c])
            if not diagonal:
                s_ref[c] = logits(kj + 1, c)
            vT = _with_ones_row(vT_ref[kj, h * dv:(h + 1) * dv, :])
            acc_ref[c] = alpha * acc_ref[c] + jnp.dot(vT, p, preferred_element_type=F32)
            new_m.append(m_new)
        return tuple(new_m)

    ms = lax.fori_loop(0, qi, lambda kj, cr: step(kj, cr, False), _initial_max(n_chain, T))
    step(qi, ms, True)

    lr = lam_ref[...]
    lam = (jnp.exp(jnp.sum(lr[0:1, :] * lr[1:2, :], axis=1, keepdims=True))
           - jnp.exp(jnp.sum(lr[2:3, :] * lr[3:4, :], axis=1, keepdims=True)) + lam_init)

    def normalised(c):
        return acc_ref[c, 0:dv, :] / acc_ref[c, dv:dv + 1, :]

    for h in range(B_HEADS):
        attn = normalised(2 * h) - lam * normalised(2 * h + 1)
        y = attn * lax.rsqrt(jnp.mean(attn * attn, axis=0, keepdims=True) + EPS)
        oT_ref[h * dv:(h + 1) * dv, :] = y * gsub_ref[...] * (1.0 - lam_init)
    out_ref[...] = oT_ref[...].T.astype(out_ref.dtype)


def _diff(bqT, bk, bvT, bias_b, lam_rows, gsub, lam_init):
    bsz, _, seq = bqT.shape
    T = ATT_TILE
    nk = seq // T
    dv = 2 * HEAD_DIM
    qspec = pl.BlockSpec((None, B_HEADS * dv, T), lambda b, i: (b, 0, i))
    return pl.pallas_call(
        functools.partial(_diff_kernel, lam_init=lam_init),
        out_shape=jax.ShapeDtypeStruct((bsz, seq, B_HEADS * dv), MM_DTYPE),
        grid=(bsz, nk),
        in_specs=[qspec,
                  pl.BlockSpec((None, nk, T, B_HEADS * dv), lambda b, i: (b, 0, 0, 0)),
                  pl.BlockSpec((None, nk, B_HEADS * dv, T), lambda b, i: (b, 0, 0, 0)),
                  _const_spec(bias_b.shape), _const_spec((4, HEAD_DIM)), _const_spec((dv, 1))],
        out_specs=pl.BlockSpec((None, T, B_HEADS * dv), lambda b, i: (b, i, 0)),
        scratch_shapes=[pltpu.VMEM((2 * B_HEADS, dv, T), MM_DTYPE),
                        pltpu.VMEM((2 * B_HEADS, dv + SUM_ROWS, T), F32),
                        pltpu.VMEM((2 * B_HEADS, T, T), F32),
                        pltpu.VMEM((B_HEADS * dv, T), F32)],
        compiler_params=_cparams(2), name="diff_attention",
    )(bqT, bk, bvT, bias_b, lam_rows, gsub)


def _dil_kernel(cur_ref, halo_ref, bias_ref, out_ref, lse_ref, *, tq):
    n = C_BAND
    wid = C_HPG * HEAD_DIM
    halo_lo = jnp.where(pl.program_id(2) == 0, jnp.int32(n), jnp.int32(0))
    i = lax.broadcasted_iota(I32, (n, 2 * n), 0)
    j = lax.broadcasted_iota(I32, (n, 2 * n), 1)
    lane_head = lax.broadcasted_iota(I32, (n, wid), 1) // HEAD_DIM
    in_head = [lane_head == h for h in range(C_HPG)]
    band = jnp.where(j >= i, jnp.where(j <= i + n, 0.0, NEG), NEG)
    band0 = jnp.where(j >= jnp.maximum(i, halo_lo), jnp.where(j <= i + n, 0.0, NEG), NEG)
    bias = [bias_ref[h] + band for h in range(C_HPG)]
    bias0 = [bias_ref[h] + band0 for h in range(C_HPG)]

    def band_rows(r, c, lo, hi):
        if c == 0:
            return jnp.concatenate([halo_ref[r, :, lo:hi], cur_ref[r, 0:n, lo:hi]], axis=0)
        return cur_ref[r, (c - 1) * n:(c + 1) * n, lo:hi]

    blocks = [(r, c) for r in range(cur_ref.shape[0]) for c in range(tq // n)]
    logits = []
    for r, c in blocks:
        q = cur_ref[r, c * n:(c + 1) * n, 0:wid].astype(F32) * ATTN_SCALE
        keys = band_rows(r, c, wid, 2 * wid)
        for h in range(C_HPG):
            qh = jnp.where(in_head[h], q, 0.0).astype(MM_DTYPE)
            s = lax.dot_general(qh, keys, (((1,), (1,)), ((), ())), preferred_element_type=F32)
            logits.append(s + (bias0 if c == 0 else bias)[h])
    for k, (r, c) in enumerate(blocks):
        vals = band_rows(r, c, 2 * wid, 3 * wid)
        out = jnp.zeros((n, wid), F32)
        lse = jnp.zeros((n, wid), F32)
        for h in range(C_HPG):
            s = logits[k * C_HPG + h]
            m = jnp.max(s, axis=1, keepdims=True)
            p = jnp.exp(s - m)
            den = jnp.sum(p, axis=1, keepdims=True)
            o = jnp.dot(p.astype(MM_DTYPE), vals, preferred_element_type=F32) * (1.0 / den)
            out = jnp.where(in_head[h], o, out)
            lse = jnp.where(in_head[h], m + jnp.log(den), lse)
        out_ref[r, c * n:(c + 1) * n, :] = out
        lse_ref[r, c * n:(c + 1) * n, :] = lse


def _dilated_group(cg, bias_g, g):
    bsz, dil, m, gw = cg.shape
    wid = C_HPG * HEAD_DIM
    n = C_BAND
    assert m % n == 0 and gw == 3 * wid
    tq = min(m, 512)
    n_res = max(1, min(dil, 512 // tq))
    cur = pl.BlockSpec((None, n_res, tq, gw), lambda b, r, i: (b, r, i, 0))
    halo = pl.BlockSpec((None, n_res, n, gw),
                        lambda b, r, i: (b, r, jnp.maximum(i * (tq // n) - 1, 0), 0))
    outspec = pl.BlockSpec((None, n_res, tq, wid), lambda b, r, i: (b, r, i, 0))
    return pl.pallas_call(
        functools.partial(_dil_kernel, tq=tq),
        out_shape=[jax.ShapeDtypeStruct((bsz, dil, m, wid), F32)] * 2,
        grid=(bsz, dil // n_res, m // tq),
        in_specs=[cur, halo, pl.BlockSpec((C_HPG, n, 2 * n), lambda b, r, i: (0, 0, 0))],
        out_specs=[outspec, outspec],
        compiler_params=_cparams(3), name=f"dilated_group{g}",
    )(cg, cg, bias_g)


def _merge_kernel(x_ref, mod_ref, g1_ref, oa_ref, ob_ref,
                  c0_ref, c1_ref, c2_ref, s0_ref, s1_ref, s2_ref,
                  wza, wzb, wzc, wba, wbb, wbc, wo, out_ref, tok_ref):
    x = x_ref[...]
    tm = x.shape[0]
    h = _rms(x) * g1_ref[...]
    h = h * (1.0 + mod_ref[1:2, :]) + mod_ref[0:1, :]
    hb = h.astype(MM_DTYPE)

    def token_order(k, ref):
        dil, _, w = ref.shape
        if dil == 1:
            return ref[0]
        n_chunk = w // 128
        for r in range(dil):
            for j in range(n_chunk):
                tok_ref[k * n_chunk + j, pl.ds(r, tm // dil, stride=dil), :] = ref[
                    r, :, j * 128:(j + 1) * 128]
        return jnp.concatenate([tok_ref[k * n_chunk + j] for j in range(n_chunk)], axis=1)

    s0, s1, s2 = s0_ref[0], token_order(0, s1_ref), token_order(1, s2_ref)
    c0, c1, c2 = c0_ref[0], token_order(2, c1_ref), token_order(3, c2_ref)
    mx = jnp.maximum(jnp.maximum(s0, s1), s2)
    e0, e1, e2 = jnp.exp(s0 - mx), jnp.exp(s1 - mx), jnp.exp(s2 - mx)
    oc = (e0 * c0 + e1 * c1 + e2 * c2) / (e0 + e1 + e2)

    def gated(wz, o, wb):
        z = jnp.dot(hb, wz[...], preferred_element_type=F32)
        return jax.nn.sigmoid(z) * jnp.dot(o, wb[...], preferred_element_type=F32)

    merged = (gated(wza, oa_ref[...], wba) + gated(wzb, ob_ref[...], wbb)
              + gated(wzc, oc.astype(MM_DTYPE), wbc))
    y = jnp.dot(merged.astype(MM_DTYPE), wo[...], preferred_element_type=F32)
    out_ref[...] = x + mod_ref[2:3, :] * y


def _merge(x2d, mod_l, g1, oa, ob, ocs, lses, ws, seq):
    n, d = x2d.shape
    tm = ROW_TILE
    per_b = seq // tm
    row = lambda wd: pl.BlockSpec((tm, wd), lambda i: (i, 0))
    wid = C_HPG * HEAD_DIM
    res = [pl.BlockSpec((None, dil, tm // dil, wid), lambda i: (i // per_b, 0, i % per_b, 0))
           for _, dil in C_GROUPS]
    in_specs = [row(d), pl.BlockSpec((None, 6, d), lambda i: (i // per_b, 0, 0)), _const_spec((1, d)),
                row(oa.shape[1]), row(ob.shape[1])] + res + res
    in_specs += [_const_spec(w.shape) for w in ws]
    return pl.pallas_call(
        _merge_kernel, out_shape=jax.ShapeDtypeStruct((n, d), F32), grid=(n // tm,),
        in_specs=in_specs, out_specs=row(d),
        scratch_shapes=[pltpu.VMEM((4 * wid // 128, tm, 128), F32)],
        compiler_params=_cparams(1), name="gated_merge",
    )(x2d, mod_l, g1, oa, ob, *ocs, *lses, *ws)


def _ffn_kernel(x_ref, mod_ref, g2_ref, gf_ref, wgu, wd, out_ref, *, chunk, final_norm):
    x = x_ref[...]
    h = _rms(x) * g2_ref[...]
    h = h * (1.0 + mod_ref[4:5, :]) + mod_ref[3:4, :]
    hb = h.astype(MM_DTYPE)
    acc = jnp.zeros(x.shape, F32)
    for c in range(D_FF // chunk):
        fg = jnp.dot(hb, wgu[:, c * chunk:(c + 1) * chunk], preferred_element_type=F32)
        fu = jnp.dot(hb, wgu[:, D_FF + c * chunk:D_FF + (c + 1) * chunk],
                     preferred_element_type=F32)
        act = (fg * jax.nn.sigmoid(fg) * fu).astype(MM_DTYPE)
        acc = acc + jnp.dot(act, wd[c * chunk:(c + 1) * chunk, :], preferred_element_type=F32)
    y = x + mod_ref[5:6, :] * acc
    if final_norm:
        y = _rms(y) * gf_ref[...]
    out_ref[...] = y


def _ffn(x2d, mod_l, g2, gf, wgu, wd, seq, final_norm):
    n, d = x2d.shape
    tm = ROW_TILE
    per_b = seq // tm
    row = pl.BlockSpec((tm, d), lambda i: (i, 0))
    return pl.pallas_call(
        functools.partial(_ffn_kernel, chunk=256, final_norm=final_norm),
        out_shape=jax.ShapeDtypeStruct((n, d), F32), grid=(n // tm,),
        in_specs=[row, pl.BlockSpec((None, 6, d), lambda i: (i // per_b, 0, 0)),
                  _const_spec((1, d)), _const_spec((1, d)),
                  _const_spec(wgu.shape), _const_spec(wd.shape)],
        out_specs=row,
        compiler_params=_cparams(1), name="swiglu_ffn",
    )(x2d, mod_l, g2, gf, wgu, wd)


def kernel(x, c, w_ada, b_ada, g_norm1, w_in, w_uk, w_uv, g_kv, lam_q1, lam_k1, lam_q2, lam_k2,
           g_subln, w_branch_a, w_branch_b, w_branch_c, w_out, g_norm2, w_gate_up, w_down,
           rel_bias, g_final):
    bsz, seq, d = x.shape
    depth = w_ada.shape[0]
    T = ATT_TILE
    nk = seq // T
    assert d == D_MODEL and seq % T == 0 and seq % ROW_TILE == 0
    n = bsz * seq
    cast = lambda w: w.astype(MM_DTYPE)

    thresholds = _bucket_thresholds(seq + 2 * C_BAND * C_GROUPS[-1][1])
    assert seq <= N_OFFSETS * T or (N_OFFSETS - 2) * T + 1 >= thresholds[-1]
    tab = rel_bias.reshape(-1)
    bias_a = _bias_att_tiles(tab, 0, A_HEADS, T, thresholds)
    bias_b = _bias_att_tiles(tab, A_HEADS, B_HEADS, T, thresholds)
    bias_c = _bias_dil_tiles(tab, A_HEADS + B_HEADS, thresholds)

    mod = _modulation(c, w_ada, b_ada).reshape(depth, bsz, 6, d)

    splits = (A_HEADS * HEAD_DIM, A_LATENT, IDX_HEADS * IDX_DIM, IDX_DIM, IDX_HEADS,
              B_HEADS * 2 * HEAD_DIM, B_HEADS * 2 * HEAD_DIM, B_HEADS * 2 * HEAD_DIM,
              C_HEADS * HEAD_DIM, C_HEADS * HEAD_DIM, C_HEADS * HEAD_DIM, d, d, d)
    offs = np.concatenate([[0], np.cumsum(splits)])
    seg = lambda w, k: w[:, int(offs[k]):int(offs[k + 1])]

    x2d = x.reshape(n, d)
    for l in range(depth):
        wl = w_in[l]
        castT = lambda w: w.T.astype(MM_DTYPE)
        wT_iw = jnp.pad(seg(wl, 4).T, ((0, 16 - IDX_HEADS), (0, 0)))
        wid = C_HPG * HEAD_DIM
        w_c = jnp.concatenate([seg(wl, k)[:, g * wid:(g + 1) * wid]
                               for g in range(len(C_GROUPS)) for k in (8, 9, 10)], axis=1)
        ws_in = [castT(seg(wl, 0)), castT(seg(wl, 2)), cast(wT_iw), castT(seg(wl, 5)), castT(seg(wl, 7)),
                 cast(seg(wl, 3)), cast(seg(wl, 1)), cast(seg(wl, 6)), cast(w_c)]
        g1 = g_norm1[l].reshape(1, d)
        (aqT, iqT, iwT, bqT, bvT, ik, kv, kvT, bk, cg0, cg1, cg2) = _in_proj(
            x2d, mod[l], g1, g_kv[l].reshape(1, A_LATENT), ws_in, bsz, seq)

        o_a = _dsa(aqT, iqT, iwT,
                   ik.reshape(bsz, nk, T, IDX_DIM), kv.reshape(bsz, nk, T, A_LATENT), kvT,
                   bias_a, cast(w_uk[l].transpose(0, 2, 1)), cast(w_uv[l].transpose(0, 2, 1)))
        o_a = o_a.reshape(n, -1)

        lam_init = 0.8 - 0.6 * math.exp(-0.3 * l)
        lam_rows = jnp.stack([lam_q1[l], lam_k1[l], lam_q2[l], lam_k2[l]])
        dv = 2 * HEAD_DIM
        o_b = _diff(bqT, bk.reshape(bsz, nk, T, B_HEADS * dv), bvT, bias_b, lam_rows,
                    g_subln[l].reshape(dv, 1), lam_init)
        o_b = o_b.reshape(n, -1)

        ocs, lses = [], []
        for g, (cg, (window, dil)) in enumerate(zip((cg0, cg1, cg2), C_GROUPS)):
            assert window // dil == C_BAND
            o, s = _dilated_group(cg, bias_c[g * C_HPG:(g + 1) * C_HPG], g)
            ocs.append(o)
            lses.append(s)

        ws_merge = [cast(seg(wl, 11)), cast(seg(wl, 12)), cast(seg(wl, 13)),
                    cast(w_branch_a[l]), cast(w_branch_b[l]), cast(w_branch_c[l]), cast(w_out[l])]
        x2d = _merge(x2d, mod[l], g1, o_a, o_b, ocs, lses, ws_merge, seq)
        x2d = _ffn(x2d, mod[l], g_norm2[l].reshape(1, d), g_final.reshape(1, d),
                   cast(w_gate_up[l]), cast(w_down[l]), seq, final_norm=(l == depth - 1))
    return x2d.reshape(bsz, seq, d)
```

```python
import functools
import math

import numpy as np
import jax
import jax.numpy as jnp
from jax import lax
from jax.experimental import pallas as pl
from jax.experimental.pallas import tpu as pltpu

D_MODEL = 1024
HEAD_DIM = 64
ATTN_SCALE = HEAD_DIM ** -0.5
LOG2E = math.log2(math.e)
A_HEADS = 8
A_LATENT = 128
IDX_HEADS = 8
IDX_DIM = 64
IDX_SCALE = (IDX_HEADS * IDX_DIM) ** -0.5
TOPK_MAX = 256
B_HEADS = 4
C_GROUPS = ((128, 1), (512, 4), (2048, 16))
C_HPG = 4
C_HEADS = C_HPG * len(C_GROUPS)
N_BUCKETS = 32
MAX_DISTANCE = 2048
N_BIAS_HEADS = A_HEADS + B_HEADS + C_HEADS
D_FF = -(-8 * D_MODEL // (3 * 256)) * 256
EPS = 1e-6

MM_DTYPE = jnp.bfloat16
F32 = jnp.float32
I32 = jnp.int32

ATT_TILE = 256
N_OFFSETS = 8
C_BAND = 128
ROW_TILE = 512
FFN_CHUNK = 256
NEG = -1e30
INT_MIN = -2 ** 31
VMEM_LIMIT = 56 * 1024 * 1024


def _cparams(n_axes, vmem=VMEM_LIMIT):
    return pltpu.CompilerParams(dimension_semantics=("arbitrary",) * n_axes,
                                vmem_limit_bytes=vmem)


def _const_spec(shape):
    nd = len(shape)
    return pl.BlockSpec(shape, lambda *_: (0,) * nd, pipeline_mode=pl.Buffered(1))


def _bucket_thresholds(max_dist):
    n = np.arange(max_dist + 1)
    max_exact = N_BUCKETS // 2
    nf = np.maximum(n, 1).astype(np.float32)
    large = max_exact + (np.log(nf / np.float32(max_exact))
                         / np.float32(math.log(MAX_DISTANCE / max_exact))
                         * np.float32(N_BUCKETS - max_exact)).astype(np.int32)
    large = np.minimum(large, N_BUCKETS - 1)
    bucket = np.where(n < max_exact, n, large)
    assert np.all(np.diff(bucket) >= 0)
    thr = []
    for k in range(1, N_BUCKETS):
        idx = np.nonzero(bucket >= k)[0]
        thr.append(int(idx[0]) if idx.size else None)
    return thr


def _bias_from_dist(dist, tab_ref, col, thresholds):
    b = jnp.full(dist.shape, tab_ref[col], F32)
    for k, thr in enumerate(thresholds, start=1):
        if thr is None:
            break
        b = jnp.where(dist >= thr, tab_ref[k * N_BIAS_HEADS + col], b)
    return b


def _bias_att_kernel(tab_ref, out_ref, *, head0, tile, thresholds):
    h = pl.program_id(0)
    o = pl.program_id(1)
    row = lax.broadcasted_iota(I32, (tile, tile), 0)
    colq = lax.broadcasted_iota(I32, (tile, tile), 1)
    dist = jnp.maximum(o * tile + colq - row, 0)
    out_ref[...] = _bias_from_dist(dist, tab_ref, head0 + h, thresholds) * LOG2E


def _bias_att_tiles(tab, head0, n_heads, tile, thresholds):
    return pl.pallas_call(
        functools.partial(_bias_att_kernel, head0=head0, tile=tile, thresholds=thresholds),
        out_shape=jax.ShapeDtypeStruct((n_heads, N_OFFSETS, tile, tile), F32),
        grid=(n_heads, N_OFFSETS),
        in_specs=[pl.BlockSpec(memory_space=pltpu.SMEM)],
        out_specs=pl.BlockSpec((None, None, tile, tile), lambda h, o: (h, o, 0, 0)),
        compiler_params=_cparams(2),
        name="bias_att_tiles",
    )(tab)


def _bias_dil_kernel(tab_ref, out_ref, *, head0, thresholds):
    h = pl.program_id(0)
    g = h // C_HPG
    dil = jnp.where(g == 0, C_GROUPS[0][1], jnp.where(g == 1, C_GROUPS[1][1], C_GROUPS[2][1]))
    i = lax.broadcasted_iota(I32, (C_BAND, 2 * C_BAND), 0)
    j = lax.broadcasted_iota(I32, (C_BAND, 2 * C_BAND), 1)
    dist = jnp.maximum((i - j + C_BAND) * dil, 0)
    out_ref[...] = _bias_from_dist(dist, tab_ref, head0 + h, thresholds)


def _bias_dil_tiles(tab, head0, thresholds):
    return pl.pallas_call(
        functools.partial(_bias_dil_kernel, head0=head0, thresholds=thresholds),
        out_shape=jax.ShapeDtypeStruct((C_HEADS, C_BAND, 2 * C_BAND), F32),
        grid=(C_HEADS,),
        in_specs=[pl.BlockSpec(memory_space=pltpu.SMEM)],
        out_specs=pl.BlockSpec((None, C_BAND, 2 * C_BAND), lambda h: (h, 0, 0)),
        compiler_params=_cparams(1),
        name="bias_dil_tiles",
    )(tab)


def _mod_kernel(c_ref, w_ref, b_ref, out_ref):
    c = c_ref[...]
    ca = (c * jax.nn.sigmoid(c)).astype(MM_DTYPE)
    out_ref[...] = jnp.dot(ca, w_ref[...].astype(MM_DTYPE), preferred_element_type=F32) + b_ref[...]


def _modulation(c, w_ada, b_ada):
    depth, d, wid = w_ada.shape
    bsz = c.shape[0]
    tn = 1536
    assert wid % tn == 0
    return pl.pallas_call(
        _mod_kernel,
        out_shape=jax.ShapeDtypeStruct((depth, bsz, wid), F32),
        grid=(depth, wid // tn),
        in_specs=[pl.BlockSpec((bsz, d), lambda l, j: (0, 0)),
                  pl.BlockSpec((None, d, tn), lambda l, j: (l, 0, j)),
                  pl.BlockSpec((None, 1, tn), lambda l, j: (l, 0, j))],
        out_specs=pl.BlockSpec((None, bsz, tn), lambda l, j: (l, 0, j)),
        compiler_params=_cparams(2),
        name="adaln_modulation",
    )(c, w_ada, b_ada.reshape(depth, 1, wid))


def _rms(x):
    return x * lax.rsqrt(jnp.mean(x * x, axis=-1, keepdims=True) + EPS)


def _in_kernel(x_ref, mod_ref, g1_ref, gkv_ref,
               wT_aq, wT_iq, wT_iw, wT_bq, wT_bv, w_ik, w_kv, w_bk, w_c,
               o_aqT, o_iqT, o_iwT, o_bqT, o_bvT, o_ik, o_kv, o_kvT, o_bk, o_c0, o_c1, o_c2,
               c_scr):
    T = ATT_TILE
    tm = x_ref.shape[0]
    h = _rms(x_ref[...]) * g1_ref[...]
    h = h * (1.0 + mod_ref[1:2, :]) + mod_ref[0:1, :]
    hb = h.astype(MM_DTYPE)

    def mm(w):
        return jnp.dot(hb, w[...], preferred_element_type=F32)

    def mm_t(wT):
        return lax.dot_general(wT[...], hb, (((1,), (1,)), ((), ())), preferred_element_type=F32)

    o_aqT[...] = mm_t(wT_aq).astype(o_aqT.dtype)
    o_iqT[...] = mm_t(wT_iq).astype(o_iqT.dtype)
    o_bqT[...] = (mm_t(wT_bq) * (ATTN_SCALE * LOG2E)).astype(o_bqT.dtype)
    o_iwT[...] = (mm_t(wT_iw) * IDX_SCALE)[:IDX_HEADS]
    bvT = mm_t(wT_bv).astype(o_bvT.dtype)
    kv = _rms(mm(w_kv)) * gkv_ref[...]
    kvT = kv.T.astype(o_kvT.dtype)
    for j in range(tm // T):
        o_bvT[j] = bvT[:, j * T:(j + 1) * T]
        o_kvT[j] = kvT[:, j * T:(j + 1) * T]
    o_kv[...] = kv.astype(o_kv.dtype)
    o_ik[...] = mm(w_ik).astype(o_ik.dtype)
    o_bk[...] = mm(w_bk).astype(o_bk.dtype)

    yc = mm(w_c)
    n_chunk = yc.shape[1] // 128
    for j in range(n_chunk):
        c_scr[j] = yc[:, j * 128:(j + 1) * 128]
    per_group = n_chunk // len(C_GROUPS)
    for g, o_c in enumerate((o_c0, o_c1, o_c2)):
        dil = C_GROUPS[g][1]
        for r in range(dil):
            for jj in range(per_group):
                o_c[r, :, jj * 128:(jj + 1) * 128] = c_scr[
                    g * per_group + jj, pl.ds(r, tm // dil, stride=dil), :].astype(o_c.dtype)


def _in_proj(x2d, mod_l, g1, gkv, ws, bsz, seq):
    n, d = x2d.shape
    tm = ROW_TILE
    T = ATT_TILE
    per_b = seq // tm
    nk = seq // T
    hd = A_HEADS * HEAD_DIM
    bw = B_HEADS * 2 * HEAD_DIM
    gw = 3 * C_HPG * HEAD_DIM
    in_specs = [pl.BlockSpec((tm, d), lambda i: (i, 0)),
                pl.BlockSpec((None, 6, d), lambda i: (i // per_b, 0, 0)),
                _const_spec((1, d)), _const_spec((1, A_LATENT))]
    in_specs += [_const_spec(w.shape) for w in ws]

    def tspec(rows):
        return pl.BlockSpec((None, rows, tm), lambda i: (i // per_b, 0, i % per_b))

    def tile_tspec(rows):
        return pl.BlockSpec((None, tm // T, rows, T), lambda i: (i // per_b, i % per_b, 0, 0))

    def rspec(wd):
        return pl.BlockSpec((tm, wd), lambda i: (i, 0))

    def cspec(dil):
        return pl.BlockSpec((None, dil, tm // dil, gw), lambda i: (i // per_b, 0, i % per_b, 0))

    sds = jax.ShapeDtypeStruct
    out_specs = [tspec(hd), tspec(IDX_HEADS * IDX_DIM), tspec(IDX_HEADS), tspec(bw),
                 tile_tspec(bw), rspec(IDX_DIM), rspec(A_LATENT), tile_tspec(A_LATENT), rspec(bw)]
    out_shape = [sds((bsz, hd, seq), MM_DTYPE), sds((bsz, IDX_HEADS * IDX_DIM, seq), MM_DTYPE),
                 sds((bsz, IDX_HEADS, seq), F32), sds((bsz, bw, seq), MM_DTYPE),
                 sds((bsz, nk, bw, T), MM_DTYPE), sds((n, IDX_DIM), MM_DTYPE),
                 sds((n, A_LATENT), MM_DTYPE), sds((bsz, nk, A_LATENT, T), MM_DTYPE),
                 sds((n, bw), MM_DTYPE)]
    for _, dil in C_GROUPS:
        out_specs.append(cspec(dil))
        out_shape.append(sds((bsz, dil, seq // dil, gw), MM_DTYPE))
    return pl.pallas_call(
        _in_kernel, out_shape=out_shape, grid=(n // tm,),
        in_specs=in_specs, out_specs=out_specs,
        scratch_shapes=[pltpu.VMEM((3 * gw // 128, tm, 128), F32)],
        compiler_params=_cparams(1), name="in_proj",
    )(x2d, mod_l, g1, gkv, *ws)


def _softmax_step(s, m_old):
    m_new = jnp.maximum(m_old, jnp.max(s, axis=0, keepdims=True))
    p = jnp.exp2(s - m_new)
    alpha = jnp.exp2(m_old - m_new)
    return p.astype(MM_DTYPE), alpha, m_new


def _initial_max(n_chains, tile):
    return tuple(jnp.full((1, tile), NEG, F32) for _ in range(n_chains))


SUM_ROWS = 16


def _with_ones_row(vT):
    row = lax.broadcasted_iota(I32, (SUM_ROWS, vT.shape[1]), 0)
    ones = jnp.where(row == 0, 1.0, 0.0).astype(vT.dtype)
    return jnp.concatenate([vT, ones], axis=0)


def _bit_transpose32(words):
    a = list(words)
    j, mask = 16, 0x0000FFFF
    while j:
        k = 0
        while k < 32:
            t = (a[k] ^ lax.shift_right_logical(a[k + j], jnp.int32(j))) & jnp.int32(mask)
            a[k] = a[k] ^ t
            a[k + j] = a[k + j] ^ lax.shift_left(t, jnp.int32(j))
            k = (k + j + 1) & ~j
        j >>= 1
        mask = (mask ^ (mask << j)) & 0xFFFFFFFF
    return a


def _dsa_kernel(aqT_ref, iqT_ref, iwT_ref, kidx_ref, kv_ref, kvT_ref, bias_ref, wukT_ref, wuvT_ref,
                out_ref, keys_ref, planes_ref, qlat_ref, acc_ref, s_ref, oT_ref, *, topk, idx_bits):
    T = ATT_TILE
    qi = pl.program_id(1)
    nk = qi + 1

    for h in range(A_HEADS):
        q = jnp.dot(wukT_ref[h], aqT_ref[h * HEAD_DIM:(h + 1) * HEAD_DIM, :],
                    preferred_element_type=F32) * (ATTN_SCALE * LOG2E)
        qlat_ref[h] = q.astype(qlat_ref.dtype)

    s_loc = lax.broadcasted_iota(I32, (T, T), 0)
    t_loc = lax.broadcasted_iota(I32, (T, T), 1)

    def score_tile(kj, diagonal):
        kt = kidx_ref[kj]
        acc = jnp.zeros((T, T), F32)
        for h in range(IDX_HEADS):
            s = jnp.dot(kt, iqT_ref[h * IDX_DIM:(h + 1) * IDX_DIM, :], preferred_element_type=F32)
            acc = acc + jnp.maximum(s, 0.0) * iwT_ref[h:h + 1, :]
        bits = lax.bitcast_convert_type(acc, I32)
        key = jnp.where(bits < 0, bits ^ jnp.int32(0x7FFFFFFF), bits)
        if diagonal:
            key = jnp.where(s_loc <= t_loc, key, jnp.int32(INT_MIN))
        keys_ref[kj] = key
        planes = _bit_transpose32([key[8 * i:8 * (i + 1), :] ^ jnp.int32(INT_MIN) for i in range(32)])
        for b in range(32):
            planes_ref[kj, b] = planes[b]

    def off_diagonal(kj, carry):
        score_tile(kj, False)
        return carry

    lax.fori_loop(0, qi, off_diagonal, 0)
    score_tile(qi, True)

    n_tiles = planes_ref.shape[0]

    @pl.when(qi == 0)
    def _():
        def clear_planes(kj, carry):
            for b in range(32):
                planes_ref[kj, b] = jnp.zeros((8, T), I32)
            return carry

        lax.fori_loop(1, n_tiles, clear_planes, 0)

    one = jnp.int32(1)
    nil = jnp.int32(0)
    zero = jnp.zeros((1, T), I32)

    n_rows = n_tiles * 8
    tile_of_row = lax.broadcasted_iota(I32, (n_rows, T), 0) // 8
    tied0 = jnp.where(tile_of_row < nk, jnp.int32(-1), nil)

    def bit_step(b, carry):
        tied, n_gt, kth_u = carry
        ones = tied & planes_ref[:, b].reshape(n_rows, T)
        n1 = jnp.sum(lax.population_count(ones), axis=0, keepdims=True)
        take = (n_gt + n1) >= topk
        tied = jnp.where(take, ones, tied ^ ones)
        n_gt = jnp.where(take, n_gt, n_gt + n1)
        kth_u = jnp.where(take, kth_u | lax.shift_left(one, jnp.int32(31) - b), kth_u)
        return tied, n_gt, kth_u

    tied, n_gt, kth_u = lax.fori_loop(0, 32, bit_step, (tied0, zero, zero))
    n_eq = jnp.sum(lax.population_count(tied), axis=0, keepdims=True)
    n_ge = jnp.where(kth_u == nil, n_gt, n_gt + n_eq)
    kth = jnp.maximum(kth_u ^ jnp.int32(INT_MIN), jnp.int32(INT_MIN + 1))

    def count(hit_fn):
        def body(kj, c):
            hit = hit_fn(keys_ref[kj], kj)
            return c + jnp.sum(hit.reshape(T // 8, 8, T), axis=0)
        c = lax.fori_loop(0, nk, body, jnp.zeros((8, T), I32))
        return jnp.sum(c, axis=0, keepdims=True)

    @pl.when(jnp.max(n_ge) > topk)
    def _():
        need = topk - n_gt

        def pos_step(i, cut):
            cand = cut | jnp.left_shift(one, jnp.int32(idx_bits - 1) - i)
            c = count(lambda kk, kj: jnp.where(
                kk == kth, jnp.where((kj * T + s_loc) < cand, one, nil), nil))
            return jnp.where(c < need, cand, cut)

        cut = lax.fori_loop(0, idx_bits, pos_step, zero)

        def demote(kj, carry):
            kk = keys_ref[kj]
            lowered = jnp.where((kj * T + s_loc) > cut, kth - one, kk)
            keys_ref[kj] = jnp.where(kk == kth, lowered, kk)
            return carry

        lax.fori_loop(0, nk, demote, 0)

    acc_ref[...] = jnp.zeros(acc_ref.shape, F32)

    def logits(kj, h):
        off = jnp.minimum(qi - kj, N_OFFSETS - 1)
        return bias_ref[h, off] + jnp.dot(kv_ref[kj], qlat_ref[h], preferred_element_type=F32)

    for h in range(A_HEADS):
        s_ref[h] = logits(0, h)

    def attend(kj, ms):
        unselected = jnp.where(keys_ref[kj] >= kth, 0.0, NEG)
        kvTt = _with_ones_row(kvT_ref[kj])
        kj_next = jnp.minimum(kj + 1, nk - 1)
        new_m = []
        for h in range(A_HEADS):
            s = s_ref[h] + unselected
            p, alpha, m_new = _softmax_step(s, ms[h])
            s_ref[h] = logits(kj_next, h)
            acc_ref[h] = alpha * acc_ref[h] + jnp.dot(kvTt, p, preferred_element_type=F32)
            new_m.append(m_new)
        return tuple(new_m)

    lax.fori_loop(0, nk, attend, _initial_max(A_HEADS, T))

    for h in range(A_HEADS):
        o_lat = (acc_ref[h, 0:A_LATENT, :] / acc_ref[h, A_LATENT:A_LATENT + 1, :]).astype(MM_DTYPE)
        oT_ref[h * HEAD_DIM:(h + 1) * HEAD_DIM, :] = jnp.dot(
            wuvT_ref[h], o_lat, preferred_element_type=F32)
    out_ref[...] = oT_ref[...].T.astype(out_ref.dtype)


def _dsa(aqT, iqT, iwT, kidx, kv, kvT, bias_a, wukT, wuvT):
    bsz, _, seq = aqT.shape
    T = ATT_TILE
    nk = seq // T
    topk = min(TOPK_MAX, seq // 4)
    idx_bits = int(math.log2(seq))
    assert 2 ** idx_bits == seq
    qspec = lambda rows: pl.BlockSpec((None, rows, T), lambda b, i: (b, 0, i))
    kspec = lambda a, c: pl.BlockSpec((None, nk, a, c), lambda b, i: (b, 0, 0, 0))
    return pl.pallas_call(
        functools.partial(_dsa_kernel, topk=topk, idx_bits=idx_bits),
        out_shape=jax.ShapeDtypeStruct((bsz, seq, A_HEADS * HEAD_DIM), MM_DTYPE),
        grid=(bsz, nk),
        in_specs=[qspec(A_HEADS * HEAD_DIM), qspec(IDX_HEADS * IDX_DIM), qspec(IDX_HEADS),
                  kspec(T, IDX_DIM), kspec(T, A_LATENT), kspec(A_LATENT, T),
                  _const_spec(bias_a.shape), _const_spec(wukT.shape), _const_spec(wuvT.shape)],
        out_specs=pl.BlockSpec((None, T, A_HEADS * HEAD_DIM), lambda b, i: (b, i, 0)),
        scratch_shapes=[pltpu.VMEM((nk, T, T), I32),
                        pltpu.VMEM((nk, 33, T // 32, T), I32),
                        pltpu.VMEM((A_HEADS, A_LATENT, T), MM_DTYPE),
                        pltpu.VMEM((A_HEADS, A_LATENT + SUM_ROWS, T), F32),
                        pltpu.VMEM((A_HEADS, T, T), F32),
                        pltpu.VMEM((A_HEADS * HEAD_DIM, T), F32)],
        compiler_params=_cparams(2), name="dsa_attention",
    )(aqT, iqT, iwT, kidx, kv, kvT, bias_a, wukT, wuvT)


def _diff_kernel(qT_ref, k_ref, vT_ref, bias_ref, lam_ref, gsub_ref, out_ref,
                 qz_ref, acc_ref, s_ref, oT_ref, *, lam_init):
    T = ATT_TILE
    dv = 2 * HEAD_DIM
    n_chain = 2 * B_HEADS
    qi = pl.program_id(1)
    half = lax.broadcasted_iota(I32, (dv, T), 0) < HEAD_DIM
    for h in range(B_HEADS):
        q = qT_ref[h * dv:(h + 1) * dv, :].astype(F32)
        qz_ref[2 * h] = jnp.where(half, q, 0.0).astype(MM_DTYPE)
        qz_ref[2 * h + 1] = jnp.where(half, 0.0, q).astype(MM_DTYPE)
    s_loc = lax.broadcasted_iota(I32, (T, T), 0)
    t_loc = lax.broadcasted_iota(I32, (T, T), 1)
    acc_ref[...] = jnp.zeros(acc_ref.shape, F32)

    def logits(kj, c):
        h = c // 2
        off = jnp.minimum(qi - kj, N_OFFSETS - 1)
        return bias_ref[h, off] + jnp.dot(k_ref[kj, :, h * dv:(h + 1) * dv], qz_ref[c],
                                          preferred_element_type=F32)

    for c in range(n_chain):
        s_ref[c] = logits(0, c)

    def step(kj, ms, diagonal):
        new_m = []
        for c in range(n_chain):
            h = c // 2
            s = s_ref[c]
            if diagonal:
                s = jnp.where(s_loc <= t_loc, s, NEG)
            p, alpha, m_new = _softmax_step(s, ms[c])
            if not diagonal:
                s_ref[c] = logits(kj + 1, c)
            vT = _with_ones_row(vT_ref[kj, h * dv:(h + 1) * dv, :])
            acc_ref[c] = alpha * acc_ref[c] + jnp.dot(vT, p, preferred_element_type=F32)
            new_m.append(m_new)
        return tuple(new_m)

    ms = lax.fori_loop(0, qi, lambda kj, cr: step(kj, cr, False), _initial_max(n_chain, T))
    step(qi, ms, True)

    lr = lam_ref[...]
    lam = (jnp.exp(jnp.sum(lr[0:1, :] * lr[1:2, :], axis=1, keepdims=True))
           - jnp.exp(jnp.sum(lr[2:3, :] * lr[3:4, :], axis=1, keepdims=True)) + lam_init)

    def normalised(c):
        return acc_ref[c, 0:dv, :] / acc_ref[c, dv:dv + 1, :]

    for h in range(B_HEADS):
        attn = normalised(2 * h) - lam * normalised(2 * h + 1)
        y = attn * lax.rsqrt(jnp.mean(attn * attn, axis=0, keepdims=True) + EPS)
        oT_ref[h * dv:(h + 1) * dv, :] = y * gsub_ref[...] * (1.0 - lam_init)
    out_ref[...] = oT_ref[...].T.astype(out_ref.dtype)


def _diff(bqT, bk, bvT, bias_b, lam_rows, gsub, lam_init):
    bsz, _, seq = bqT.shape
    T = ATT_TILE
    nk = seq // T
    dv = 2 * HEAD_DIM
    qspec = pl.BlockSpec((None, B_HEADS * dv, T), lambda b, i: (b, 0, i))
    return pl.pallas_call(
        functools.partial(_diff_kernel, lam_init=lam_init),
        out_shape=jax.ShapeDtypeStruct((bsz, seq, B_HEADS * dv), MM_DTYPE),
        grid=(bsz, nk),
        in_specs=[qspec,
                  pl.BlockSpec((None, nk, T, B_HEADS * dv), lambda b, i: (b, 0, 0, 0)),
                  pl.BlockSpec((None, nk, B_HEADS * dv, T), lambda b, i: (b, 0, 0, 0)),
                  _const_spec(bias_b.shape), _const_spec((4, HEAD_DIM)), _const_spec((dv, 1))],
        out_specs=pl.BlockSpec((None, T, B_HEADS * dv), lambda b, i: (b, i, 0)),
        scratch_shapes=[pltpu.VMEM((2 * B_HEADS, dv, T), MM_DTYPE),
                        pltpu.VMEM((2 * B_HEADS, dv + SUM_ROWS, T), F32),
                        pltpu.VMEM((2 * B_HEADS, T, T), F32),
                        pltpu.VMEM((B_HEADS * dv, T), F32)],
        compiler_params=_cparams(2), name="diff_attention",
    )(bqT, bk, bvT, bias_b, lam_rows, gsub)


def _dil_kernel(cur_ref, halo_ref, bias_ref, out_ref, lse_ref, *, tq):
    n = C_BAND
    wid = C_HPG * HEAD_DIM
    halo_lo = jnp.where(pl.program_id(2) == 0, jnp.int32(n), jnp.int32(0))
    i = lax.broadcasted_iota(I32, (n, 2 * n), 0)
    j = lax.broadcasted_iota(I32, (n, 2 * n), 1)
    lane_head = lax.broadcasted_iota(I32, (n, wid), 1) // HEAD_DIM
    in_head = [lane_head == h for h in range(C_HPG)]
    band = jnp.where(j >= i, jnp.where(j <= i + n, 0.0, NEG), NEG)
    band0 = jnp.where(j >= jnp.maximum(i, halo_lo), jnp.where(j <= i + n, 0.0, NEG), NEG)
    bias = [bias_ref[h] + band for h in range(C_HPG)]
    bias0 = [bias_ref[h] + band0 for h in range(C_HPG)]

    def band_rows(r, c, lo, hi):
        if c == 0:
            return jnp.concatenate([halo_ref[r, :, lo:hi], cur_ref[r, 0:n, lo:hi]], axis=0)
        return cur_ref[r, (c - 1) * n:(c + 1) * n, lo:hi]

    blocks = [(r, c) for r in range(cur_ref.shape[0]) for c in range(tq // n)]
    logits = []
    for r, c in blocks:
        q = cur_ref[r, c * n:(c + 1) * n, 0:wid].astype(F32) * ATTN_SCALE
        keys = band_rows(r, c, wid, 2 * wid)
        for h in range(C_HPG):
            qh = jnp.where(in_head[h], q, 0.0).astype(MM_DTYPE)
            s = lax.dot_general(qh, keys, (((1,), (1,)), ((), ())), preferred_element_type=F32)
            logits.append(s + (bias0 if c == 0 else bias)[h])
    for k, (r, c) in enumerate(blocks):
        vals = band_rows(r, c, 2 * wid, 3 * wid)
        out = jnp.zeros((n, wid), F32)
        lse = jnp.zeros((n, wid), F32)
        for h in range(C_HPG):
            s = logits[k * C_HPG + h]
            m = jnp.max(s, axis=1, keepdims=True)
            p = jnp.exp(s - m)
            den = jnp.sum(p, axis=1, keepdims=True)
            o = jnp.dot(p.astype(MM_DTYPE), vals, preferred_element_type=F32) * (1.0 / den)
            out = jnp.where(in_head[h], o, out)
            lse = jnp.where(in_head[h], m + jnp.log(den), lse)
        out_ref[r, c * n:(c + 1) * n, :] = out
        lse_ref[r, c * n:(c + 1) * n, :] = lse


def _dilated_group(cg, bias_g, g):
    bsz, dil, m, gw = cg.shape
    wid = C_HPG * HEAD_DIM
    n = C_BAND
    assert m % n == 0 and gw == 3 * wid
    tq = min(m, 512)
    n_res = max(1, min(dil, 512 // tq))
    cur = pl.BlockSpec((None, n_res, tq, gw), lambda b, r, i: (b, r, i, 0))
    halo = pl.BlockSpec((None, n_res, n, gw),
                        lambda b, r, i: (b, r, jnp.maximum(i * (tq // n) - 1, 0), 0))
    outspec = pl.BlockSpec((None, n_res, tq, wid), lambda b, r, i: (b, r, i, 0))
    return pl.pallas_call(
        functools.partial(_dil_kernel, tq=tq),
        out_shape=[jax.ShapeDtypeStruct((bsz, dil, m, wid), F32)] * 2,
        grid=(bsz, dil // n_res, m // tq),
        in_specs=[cur, halo, pl.BlockSpec((C_HPG, n, 2 * n), lambda b, r, i: (0, 0, 0))],
        out_specs=[outspec, outspec],
        compiler_params=_cparams(3), name=f"dilated_group{g}",
    )(cg, cg, bias_g)


def _merge_ffn_kernel(x_ref, mod_ref, g1_ref, g2_ref, gf_ref, oa_ref, ob_ref,
                      c0_ref, c1_ref, c2_ref, s0_ref, s1_ref, s2_ref,
                      wza, wzb, wzc, wba, wbb, wbc, wo, wgu, wd, out_ref, tok_ref, *, final_norm):
    x = x_ref[...]
    tm = x.shape[0]
    h = _rms(x) * g1_ref[...]
    h = h * (1.0 + mod_ref[1:2, :]) + mod_ref[0:1, :]
    hb = h.astype(MM_DTYPE)

    def token_order(k, ref):
        dil, _, w = ref.shape
        if dil == 1:
            return ref[0]
        n_chunk = w // 128
        for r in range(dil):
            for j in range(n_chunk):
                tok_ref[k * n_chunk + j, pl.ds(r, tm // dil, stride=dil), :] = ref[
                    r, :, j * 128:(j + 1) * 128]
        return jnp.concatenate([tok_ref[k * n_chunk + j] for j in range(n_chunk)], axis=1)

    s0, s1, s2 = s0_ref[0], token_order(0, s1_ref), token_order(1, s2_ref)
    c0, c1, c2 = c0_ref[0], token_order(2, c1_ref), token_order(3, c2_ref)
    mx = jnp.maximum(jnp.maximum(s0, s1), s2)
    e0, e1, e2 = jnp.exp(s0 - mx), jnp.exp(s1 - mx), jnp.exp(s2 - mx)
    oc = (e0 * c0 + e1 * c1 + e2 * c2) / (e0 + e1 + e2)

    def gated(wz, o, wb):
        z = jnp.dot(hb, wz[...], preferred_element_type=F32)
        return jax.nn.sigmoid(z) * jnp.dot(o, wb[...], preferred_element_type=F32)

    merged = (gated(wza, oa_ref[...], wba) + gated(wzb, ob_ref[...], wbb)
              + gated(wzc, oc.astype(MM_DTYPE), wbc))
    y = jnp.dot(merged.astype(MM_DTYPE), wo[...], preferred_element_type=F32)
    x = x + mod_ref[2:3, :] * y

    h = _rms(x) * g2_ref[...]
    h = h * (1.0 + mod_ref[4:5, :]) + mod_ref[3:4, :]
    hb = h.astype(MM_DTYPE)
    acc = jnp.zeros(x.shape, F32)
    for c in range(D_FF // FFN_CHUNK):
        cols = slice(c * FFN_CHUNK, (c + 1) * FFN_CHUNK)
        fg = jnp.dot(hb, wgu[:, cols], preferred_element_type=F32)
        fu = jnp.dot(hb, wgu[:, D_FF + c * FFN_CHUNK:D_FF + (c + 1) * FFN_CHUNK],
                     preferred_element_type=F32)
        act = (fg * jax.nn.sigmoid(fg) * fu).astype(MM_DTYPE)
        acc = acc + jnp.dot(act, wd[cols, :], preferred_element_type=F32)
    y = x + mod_ref[5:6, :] * acc
    if final_norm:
        y = _rms(y) * gf_ref[...]
    out_ref[...] = y


def _merge_ffn(x2d, mod_l, g1, g2, gf, oa, ob, ocs, lses, ws, seq, final_norm):
    n, d = x2d.shape
    tm = ROW_TILE
    per_b = seq // tm
    row = lambda wd: pl.BlockSpec((tm, wd), lambda i: (i, 0))
    vec = _const_spec((1, d))
    wid = C_HPG * HEAD_DIM
    res = [pl.BlockSpec((None, dil, tm // dil, wid), lambda i: (i // per_b, 0, i % per_b, 0))
           for _, dil in C_GROUPS]
    in_specs = [row(d), pl.BlockSpec((None, 6, d), lambda i: (i // per_b, 0, 0)), vec, vec, vec,
                row(oa.shape[1]), row(ob.shape[1])] + res + res
    in_specs += [_const_spec(w.shape) for w in ws]
    return pl.pallas_call(
        functools.partial(_merge_ffn_kernel, final_norm=final_norm),
        out_shape=jax.ShapeDtypeStruct((n, d), F32), grid=(n // tm,),
        in_specs=in_specs, out_specs=row(d),
        scratch_shapes=[pltpu.VMEM((4 * wid // 128, tm, 128), F32)],
        compiler_params=_cparams(1), name="merge_ffn",
    )(x2d, mod_l, g1, g2, gf, oa, ob, *ocs, *lses, *ws)


def kernel(x, c, w_ada, b_ada, g_norm1, w_in, w_uk, w_uv, g_kv, lam_q1, lam_k1, lam_q2, lam_k2,
           g_subln, w_branch_a, w_branch_b, w_branch_c, w_out, g_norm2, w_gate_up, w_down,
           rel_bias, g_final):
    bsz, seq, d = x.shape
    depth = w_ada.shape[0]
    T = ATT_TILE
    nk = seq // T
    assert d == D_MODEL and seq % T == 0 and seq % ROW_TILE == 0
    n = bsz * seq
    cast = lambda w: w.astype(MM_DTYPE)

    thresholds = _bucket_thresholds(seq + 2 * C_BAND * C_GROUPS[-1][1])
    assert seq <= N_OFFSETS * T or (N_OFFSETS - 2) * T + 1 >= thresholds[-1]
    tab = rel_bias.reshape(-1)
    bias_a = _bias_att_tiles(tab, 0, A_HEADS, T, thresholds)
    bias_b = _bias_att_tiles(tab, A_HEADS, B_HEADS, T, thresholds)
    bias_c = _bias_dil_tiles(tab, A_HEADS + B_HEADS, thresholds)

    mod = _modulation(c, w_ada, b_ada).reshape(depth, bsz, 6, d)

    splits = (A_HEADS * HEAD_DIM, A_LATENT, IDX_HEADS * IDX_DIM, IDX_DIM, IDX_HEADS,
              B_HEADS * 2 * HEAD_DIM, B_HEADS * 2 * HEAD_DIM, B_HEADS * 2 * HEAD_DIM,
              C_HEADS * HEAD_DIM, C_HEADS * HEAD_DIM, C_HEADS * HEAD_DIM, d, d, d)
    offs = np.concatenate([[0], np.cumsum(splits)])
    seg = lambda w, k: w[:, int(offs[k]):int(offs[k + 1])]

    x2d = x.reshape(n, d)
    for l in range(depth):
        wl = w_in[l]
        castT = lambda w: w.T.astype(MM_DTYPE)
        wT_iw = jnp.pad(seg(wl, 4).T, ((0, 16 - IDX_HEADS), (0, 0)))
        wid = C_HPG * HEAD_DIM
        w_c = jnp.concatenate([seg(wl, k)[:, g * wid:(g + 1) * wid]
                               for g in range(len(C_GROUPS)) for k in (8, 9, 10)], axis=1)
        ws_in = [castT(seg(wl, 0)), castT(seg(wl, 2)), cast(wT_iw), castT(seg(wl, 5)), castT(seg(wl, 7)),
                 cast(seg(wl, 3)), cast(seg(wl, 1)), cast(seg(wl, 6)), cast(w_c)]
        g1 = g_norm1[l].reshape(1, d)
        (aqT, iqT, iwT, bqT, bvT, ik, kv, kvT, bk, cg0, cg1, cg2) = _in_proj(
            x2d, mod[l], g1, g_kv[l].reshape(1, A_LATENT), ws_in, bsz, seq)

        o_a = _dsa(aqT, iqT, iwT,
                   ik.reshape(bsz, nk, T, IDX_DIM), kv.reshape(bsz, nk, T, A_LATENT), kvT,
                   bias_a, cast(w_uk[l].transpose(0, 2, 1)), cast(w_uv[l].transpose(0, 2, 1)))
        o_a = o_a.reshape(n, -1)

        lam_init = 0.8 - 0.6 * math.exp(-0.3 * l)
        lam_rows = jnp.stack([lam_q1[l], lam_k1[l], lam_q2[l], lam_k2[l]])
        dv = 2 * HEAD_DIM
        o_b = _diff(bqT, bk.reshape(bsz, nk, T, B_HEADS * dv), bvT, bias_b, lam_rows,
                    g_subln[l].reshape(dv, 1), lam_init)
        o_b = o_b.reshape(n, -1)

        ocs, lses = [], []
        for g, (cg, (window, dil)) in enumerate(zip((cg0, cg1, cg2), C_GROUPS)):
            assert window // dil == C_BAND
            o, s = _dilated_group(cg, bias_c[g * C_HPG:(g + 1) * C_HPG], g)
            ocs.append(o)
            lses.append(s)

        ws_out = [cast(seg(wl, 11)), cast(seg(wl, 12)), cast(seg(wl, 13)),
                  cast(w_branch_a[l]), cast(w_branch_b[l]), cast(w_branch_c[l]), cast(w_out[l]),
                  cast(w_gate_up[l]), cast(w_down[l])]
        x2d = _merge_ffn(x2d, mod[l], g1, g_norm2[l].reshape(1, d), g_final.reshape(1, d),
                         o_a, o_b, ocs, lses, ws_out, seq, final_norm=(l == depth - 1))
    return x2d.reshape(bsz, seq, d)
```

```python
import functools
import math

import numpy as np
import jax
import jax.numpy as jnp
from jax import lax
from jax.experimental import pallas as pl
from jax.experimental.pallas import tpu as pltpu

D_MODEL = 1024
HEAD_DIM = 64
ATTN_SCALE = HEAD_DIM ** -0.5
LOG2E = math.log2(math.e)
A_HEADS = 8
A_LATENT = 128
IDX_HEADS = 8
IDX_DIM = 64
IDX_SCALE = (IDX_HEADS * IDX_DIM) ** -0.5
TOPK_MAX = 256
B_HEADS = 4
C_GROUPS = ((128, 1), (512, 4), (2048, 16))
C_HPG = 4
C_HEADS = C_HPG * len(C_GROUPS)
N_BUCKETS = 32
MAX_DISTANCE = 2048
N_BIAS_HEADS = A_HEADS + B_HEADS + C_HEADS
D_FF = -(-8 * D_MODEL // (3 * 256)) * 256
EPS = 1e-6

MM_DTYPE = jnp.bfloat16
F32 = jnp.float32
I32 = jnp.int32

ATT_TILE = 256
N_OFFSETS = 8
C_BAND = 128
ROW_TILE = 512
FFN_CHUNK = 256
MOD_COLS = 1536
NEG = -1e30
INT_MIN = -2 ** 31
LANES = 128
SUBLANES = 8
KEY_BITS = 32
VMEM_LIMIT = 56 * 1024 * 1024


def _cparams(n_axes, vmem=VMEM_LIMIT):
    return pltpu.CompilerParams(dimension_semantics=("arbitrary",) * n_axes,
                                vmem_limit_bytes=vmem)


def _const_spec(shape):
    nd = len(shape)
    return pl.BlockSpec(shape, lambda *_: (0,) * nd, pipeline_mode=pl.Buffered(1))


def _bucket_thresholds(max_dist):
    n = np.arange(max_dist + 1)
    max_exact = N_BUCKETS // 2
    nf = np.maximum(n, 1).astype(np.float32)
    large = max_exact + (np.log(nf / np.float32(max_exact))
                         / np.float32(math.log(MAX_DISTANCE / max_exact))
                         * np.float32(N_BUCKETS - max_exact)).astype(np.int32)
    large = np.minimum(large, N_BUCKETS - 1)
    bucket = np.where(n < max_exact, n, large)
    assert np.all(np.diff(bucket) >= 0)
    thr = []
    for k in range(1, N_BUCKETS):
        idx = np.nonzero(bucket >= k)[0]
        thr.append(int(idx[0]) if idx.size else None)
    return thr


def _bias_from_dist(dist, tab_ref, col, thresholds, lo=0, hi=None):
    reached = [k for k, thr in enumerate(thresholds, start=1) if thr is not None]
    base = max([0] + [k for k in reached if thresholds[k - 1] <= lo])
    b = jnp.full(dist.shape, tab_ref[base * N_BIAS_HEADS + col], F32)
    for k in reached:
        thr = thresholds[k - 1]
        if thr > lo and (hi is None or thr <= hi):
            b = jnp.where(dist >= thr, tab_ref[k * N_BIAS_HEADS + col], b)
    return b


def _bias_att_kernel(tab_ref, out_ref, *, head0, thresholds):
    h = pl.program_id(0)
    n_off, tile, _ = out_ref.shape
    row = lax.broadcasted_iota(I32, (tile, tile), 0)
    colq = lax.broadcasted_iota(I32, (tile, tile), 1)
    for o in range(n_off):
        dist = jnp.maximum(o * tile + colq - row, 0)
        lo, hi = max(o * tile - (tile - 1), 0), o * tile + tile - 1
        out_ref[o] = _bias_from_dist(dist, tab_ref, head0 + h, thresholds, lo, hi) * LOG2E


def _bias_att_tiles(tab, head0, n_heads, tile, thresholds):
    return pl.pallas_call(
        functools.partial(_bias_att_kernel, head0=head0, thresholds=thresholds),
        out_shape=jax.ShapeDtypeStruct((n_heads, N_OFFSETS, tile, tile), F32),
        grid=(n_heads,),
        in_specs=[pl.BlockSpec(memory_space=pltpu.SMEM)],
        out_specs=pl.BlockSpec((None, N_OFFSETS, tile, tile), lambda h: (h, 0, 0, 0)),
        compiler_params=_cparams(1),
        name="bias_att_tiles",
    )(tab)


def _bias_dil_kernel(tab_ref, out_ref, *, head0, thresholds):
    h = pl.program_id(0)
    g = h // C_HPG
    dil = jnp.where(g == 0, C_GROUPS[0][1], jnp.where(g == 1, C_GROUPS[1][1], C_GROUPS[2][1]))
    i = lax.broadcasted_iota(I32, (C_BAND, 2 * C_BAND), 0)
    j = lax.broadcasted_iota(I32, (C_BAND, 2 * C_BAND), 1)
    dist = jnp.maximum((i - j + C_BAND) * dil, 0)
    out_ref[...] = _bias_from_dist(dist, tab_ref, head0 + h, thresholds)


def _bias_dil_tiles(tab, head0, thresholds):
    return pl.pallas_call(
        functools.partial(_bias_dil_kernel, head0=head0, thresholds=thresholds),
        out_shape=jax.ShapeDtypeStruct((C_HEADS, C_BAND, 2 * C_BAND), F32),
        grid=(C_HEADS,),
        in_specs=[pl.BlockSpec(memory_space=pltpu.SMEM)],
        out_specs=pl.BlockSpec((None, C_BAND, 2 * C_BAND), lambda h: (h, 0, 0)),
        compiler_params=_cparams(1),
        name="bias_dil_tiles",
    )(tab)


def _mod_kernel(c_ref, w_ref, b_ref, out_ref):
    c = c_ref[...]
    ca = (c * jax.nn.sigmoid(c)).astype(MM_DTYPE)
    out_ref[...] = jnp.dot(ca, w_ref[...].astype(MM_DTYPE), preferred_element_type=F32) + b_ref[...]


def _modulation(c, w_ada, b_ada):
    depth, d, wid = w_ada.shape
    bsz = c.shape[0]
    tn = MOD_COLS
    assert wid % tn == 0
    return pl.pallas_call(
        _mod_kernel,
        out_shape=jax.ShapeDtypeStruct((depth, bsz, wid), F32),
        grid=(depth, wid // tn),
        in_specs=[pl.BlockSpec((bsz, d), lambda l, j: (0, 0)),
                  pl.BlockSpec((None, d, tn), lambda l, j: (l, 0, j)),
                  pl.BlockSpec((None, 1, tn), lambda l, j: (l, 0, j))],
        out_specs=pl.BlockSpec((None, bsz, tn), lambda l, j: (l, 0, j)),
        compiler_params=_cparams(2),
        name="adaln_modulation",
    )(c, w_ada, b_ada.reshape(depth, 1, wid))


def _rms(x):
    return x * lax.rsqrt(jnp.mean(x * x, axis=-1, keepdims=True) + EPS)


def _in_kernel(x_ref, mod_ref, g1_ref, gkv_ref,
               wT_aq, wT_iq, wT_iw, wT_bq, wT_bv, w_ik, w_kv, w_bk, w_c,
               o_aqT, o_iqT, o_iwT, o_bqT, o_bvT, o_ik, o_kv, o_kvT, o_bk, o_c0, o_c1, o_c2,
               c_scr):
    T = ATT_TILE
    tm = x_ref.shape[0]
    h = _rms(x_ref[...]) * g1_ref[...]
    h = h * (1.0 + mod_ref[1:2, :]) + mod_ref[0:1, :]
    hb = h.astype(MM_DTYPE)

    def mm(w):
        return jnp.dot(hb, w[...], preferred_element_type=F32)

    def mm_t(wT):
        return lax.dot_general(wT[...], hb, (((1,), (1,)), ((), ())), preferred_element_type=F32)

    o_aqT[...] = mm_t(wT_aq).astype(o_aqT.dtype)
    o_iqT[...] = mm_t(wT_iq).astype(o_iqT.dtype)
    o_bqT[...] = (mm_t(wT_bq) * (ATTN_SCALE * LOG2E)).astype(o_bqT.dtype)
    o_iwT[...] = (mm_t(wT_iw) * IDX_SCALE)[:IDX_HEADS]
    bvT = mm_t(wT_bv).astype(o_bvT.dtype)
    kv = _rms(mm(w_kv)) * gkv_ref[...]
    kvT = kv.T.astype(o_kvT.dtype)
    for j in range(tm // T):
        o_bvT[j] = bvT[:, j * T:(j + 1) * T]
        o_kvT[j] = kvT[:, j * T:(j + 1) * T]
    o_kv[...] = kv.astype(o_kv.dtype)
    o_ik[...] = mm(w_ik).astype(o_ik.dtype)
    o_bk[...] = mm(w_bk).astype(o_bk.dtype)

    yc = mm(w_c)
    n_chunk = yc.shape[1] // LANES
    for j in range(n_chunk):
        c_scr[j] = yc[:, j * LANES:(j + 1) * LANES]
    per_group = n_chunk // len(C_GROUPS)
    for g, o_c in enumerate((o_c0, o_c1, o_c2)):
        dil = C_GROUPS[g][1]
        for r in range(dil):
            for jj in range(per_group):
                o_c[r, :, jj * LANES:(jj + 1) * LANES] = c_scr[
                    g * per_group + jj, pl.ds(r, tm // dil, stride=dil), :].astype(o_c.dtype)


def _in_proj(x2d, mod_l, g1, gkv, ws, bsz, seq):
    n, d = x2d.shape
    tm = ROW_TILE
    T = ATT_TILE
    per_b = seq // tm
    nk = seq // T
    hd = A_HEADS * HEAD_DIM
    bw = B_HEADS * 2 * HEAD_DIM
    gw = 3 * C_HPG * HEAD_DIM
    in_specs = [pl.BlockSpec((tm, d), lambda i: (i, 0)),
                pl.BlockSpec((None, 6, d), lambda i: (i // per_b, 0, 0)),
                _const_spec((1, d)), _const_spec((1, A_LATENT))]
    in_specs += [_const_spec(w.shape) for w in ws]

    def tspec(rows):
        return pl.BlockSpec((None, rows, tm), lambda i: (i // per_b, 0, i % per_b))

    def tile_tspec(rows):
        return pl.BlockSpec((None, tm // T, rows, T), lambda i: (i // per_b, i % per_b, 0, 0))

    def rspec(wd):
        return pl.BlockSpec((tm, wd), lambda i: (i, 0))

    def cspec(dil):
        return pl.BlockSpec((None, dil, tm // dil, gw), lambda i: (i // per_b, 0, i % per_b, 0))

    sds = jax.ShapeDtypeStruct
    out_specs = [tspec(hd), tspec(IDX_HEADS * IDX_DIM), tspec(IDX_HEADS), tspec(bw),
                 tile_tspec(bw), rspec(IDX_DIM), rspec(A_LATENT), tile_tspec(A_LATENT), rspec(bw)]
    out_shape = [sds((bsz, hd, seq), MM_DTYPE), sds((bsz, IDX_HEADS * IDX_DIM, seq), MM_DTYPE),
                 sds((bsz, IDX_HEADS, seq), F32), sds((bsz, bw, seq), MM_DTYPE),
                 sds((bsz, nk, bw, T), MM_DTYPE), sds((n, IDX_DIM), MM_DTYPE),
                 sds((n, A_LATENT), MM_DTYPE), sds((bsz, nk, A_LATENT, T), MM_DTYPE),
                 sds((n, bw), MM_DTYPE)]
    for _, dil in C_GROUPS:
        out_specs.append(cspec(dil))
        out_shape.append(sds((bsz, dil, seq // dil, gw), MM_DTYPE))
    return pl.pallas_call(
        _in_kernel, out_shape=out_shape, grid=(n // tm,),
        in_specs=in_specs, out_specs=out_specs,
        scratch_shapes=[pltpu.VMEM((3 * gw // LANES, tm, LANES), F32)],
        compiler_params=_cparams(1), name="in_proj",
    )(x2d, mod_l, g1, gkv, *ws)


def _softmax_step(s, m_old):
    m_new = jnp.maximum(m_old, jnp.max(s, axis=0, keepdims=True))
    p = jnp.exp2(s - m_new)
    alpha = jnp.exp2(m_old - m_new)
    return p.astype(MM_DTYPE), alpha, m_new


def _initial_max(n_chains, tile):
    return tuple(jnp.full((1, tile), NEG, F32) for _ in range(n_chains))


SUM_ROWS = 16


def _with_ones_row(vT):
    row = lax.broadcasted_iota(I32, (SUM_ROWS, vT.shape[1]), 0)
    ones = jnp.where(row == 0, 1.0, 0.0).astype(vT.dtype)
    return jnp.concatenate([vT, ones], axis=0)


def _bit_transpose32(words):
    a = list(words)
    j, mask = 16, 0x0000FFFF
    while j:
        k = 0
        while k < 32:
            t = (a[k] ^ lax.shift_right_logical(a[k + j], jnp.int32(j))) & jnp.int32(mask)
            a[k] = a[k] ^ t
            a[k + j] = a[k + j] ^ lax.shift_left(t, jnp.int32(j))
            k = (k + j + 1) & ~j
        j >>= 1
        mask = (mask ^ (mask << j)) & 0xFFFFFFFF
    return a


def _dsa_kernel(aqT_ref, iqT_ref, iwT_ref, kidx_ref, kv_ref, kvT_ref, bias_ref, wukT_ref, wuvT_ref,
                out_ref, keys_ref, planes_ref, qlat_ref, acc_ref, s_ref, oT_ref, *, topk, idx_bits):
    T = ATT_TILE
    qi = pl.program_id(1)
    nk = qi + 1

    for h in range(A_HEADS):
        q = jnp.dot(wukT_ref[h], aqT_ref[h * HEAD_DIM:(h + 1) * HEAD_DIM, :],
                    preferred_element_type=F32) * (ATTN_SCALE * LOG2E)
        qlat_ref[h] = q.astype(qlat_ref.dtype)

    s_loc = lax.broadcasted_iota(I32, (T, T), 0)
    t_loc = lax.broadcasted_iota(I32, (T, T), 1)

    def score_tile(kj, diagonal):
        kt = kidx_ref[kj]
        acc = jnp.zeros((T, T), F32)
        for h in range(IDX_HEADS):
            s = jnp.dot(kt, iqT_ref[h * IDX_DIM:(h + 1) * IDX_DIM, :], preferred_element_type=F32)
            acc = acc + jnp.maximum(s, 0.0) * iwT_ref[h:h + 1, :]
        bits = lax.bitcast_convert_type(acc, I32)
        key = jnp.where(bits < 0, bits ^ jnp.int32(0x7FFFFFFF), bits)
        if diagonal:
            key = jnp.where(s_loc <= t_loc, key, jnp.int32(INT_MIN))
        keys_ref[kj] = key
        planes = _bit_transpose32([key[SUBLANES * i:SUBLANES * (i + 1), :] ^ jnp.int32(INT_MIN)
                                   for i in range(KEY_BITS)])
        for b in range(KEY_BITS):
            planes_ref[kj, b] = planes[b]

    def off_diagonal(kj, carry):
        score_tile(kj, False)
        return carry

    lax.fori_loop(0, qi, off_diagonal, 0)
    score_tile(qi, True)

    n_tiles = planes_ref.shape[0]

    @pl.when(qi == 0)
    def _():
        def clear_planes(kj, carry):
            for b in range(KEY_BITS):
                planes_ref[kj, b] = jnp.zeros((SUBLANES, T), I32)
            return carry

        lax.fori_loop(1, n_tiles, clear_planes, 0)

    one = jnp.int32(1)
    nil = jnp.int32(0)
    zero = jnp.zeros((1, T), I32)

    n_rows = n_tiles * SUBLANES
    tile_of_row = lax.broadcasted_iota(I32, (n_rows, T), 0) // SUBLANES
    tied0 = jnp.where(tile_of_row < nk, jnp.int32(-1), nil)

    def bit_step(b, carry):
        tied, n_gt, kth_u = carry
        ones = tied & planes_ref[:, b].reshape(n_rows, T)
        n1 = jnp.sum(lax.population_count(ones), axis=0, keepdims=True)
        take = (n_gt + n1) >= topk
        tied = jnp.where(take, ones, tied ^ ones)
        n_gt = jnp.where(take, n_gt, n_gt + n1)
        kth_u = jnp.where(take, kth_u | lax.shift_left(one, jnp.int32(KEY_BITS - 1) - b), kth_u)
        return tied, n_gt, kth_u

    tied, n_gt, kth_u = lax.fori_loop(0, KEY_BITS, bit_step, (tied0, zero, zero))
    n_eq = jnp.sum(lax.population_count(tied), axis=0, keepdims=True)
    n_ge = jnp.where(kth_u == nil, n_gt, n_gt + n_eq)
    kth = jnp.maximum(kth_u ^ jnp.int32(INT_MIN), jnp.int32(INT_MIN + 1))

    def count(hit_fn):
        def body(kj, c):
            hit = hit_fn(keys_ref[kj], kj)
            return c + jnp.sum(hit.reshape(T // SUBLANES, SUBLANES, T), axis=0)
        c = lax.fori_loop(0, nk, body, jnp.zeros((SUBLANES, T), I32))
        return jnp.sum(c, axis=0, keepdims=True)

    @pl.when(jnp.max(n_ge) > topk)
    def _():
        need = topk - n_gt

        def pos_step(i, cut):
            cand = cut | jnp.left_shift(one, jnp.int32(idx_bits - 1) - i)
            c = count(lambda kk, kj: jnp.where(
                kk == kth, jnp.where((kj * T + s_loc) < cand, one, nil), nil))
            return jnp.where(c < need, cand, cut)

        cut = lax.fori_loop(0, idx_bits, pos_step, zero)

        def demote(kj, carry):
            kk = keys_ref[kj]
            lowered = jnp.where((kj * T + s_loc) > cut, kth - one, kk)
            keys_ref[kj] = jnp.where(kk == kth, lowered, kk)
            return carry

        lax.fori_loop(0, nk, demote, 0)

    acc_ref[...] = jnp.zeros(acc_ref.shape, F32)

    def logits(kj, h):
        off = jnp.minimum(qi - kj, N_OFFSETS - 1)
        return bias_ref[h, off] + jnp.dot(kv_ref[kj], qlat_ref[h], preferred_element_type=F32)

    for h in range(A_HEADS):
        s_ref[h] = logits(0, h)

    def attend(kj, ms):
        unselected = jnp.where(keys_ref[kj] >= kth, 0.0, NEG)
        kvTt = _with_ones_row(kvT_ref[kj])
        kj_next = jnp.minimum(kj + 1, nk - 1)
        new_m = []
        for h in range(A_HEADS):
            s = s_ref[h] + unselected
            p, alpha, m_new = _softmax_step(s, ms[h])
            s_ref[h] = logits(kj_next, h)
            acc_ref[h] = alpha * acc_ref[h] + jnp.dot(kvTt, p, preferred_element_type=F32)
            new_m.append(m_new)
        return tuple(new_m)

    lax.fori_loop(0, nk, attend, _initial_max(A_HEADS, T))

    for h in range(A_HEADS):
        o_lat = (acc_ref[h, 0:A_LATENT, :] / acc_ref[h, A_LATENT:A_LATENT + 1, :]).astype(MM_DTYPE)
        oT_ref[h * HEAD_DIM:(h + 1) * HEAD_DIM, :] = jnp.dot(
            wuvT_ref[h], o_lat, preferred_element_type=F32)
    out_ref[...] = oT_ref[...].T.astype(out_ref.dtype)


def _dsa(aqT, iqT, iwT, kidx, kv, kvT, bias_a, wukT, wuvT):
    bsz, _, seq = aqT.shape
    T = ATT_TILE
    nk = seq // T
    topk = min(TOPK_MAX, seq // 4)
    idx_bits = int(math.log2(seq))
    assert 2 ** idx_bits == seq
    assert T == KEY_BITS * SUBLANES
    qspec = lambda rows: pl.BlockSpec((None, rows, T), lambda b, i: (b, 0, i))
    kspec = lambda a, c: pl.BlockSpec((None, nk, a, c), lambda b, i: (b, 0, 0, 0))
    return pl.pallas_call(
        functools.partial(_dsa_kernel, topk=topk, idx_bits=idx_bits),
        out_shape=jax.ShapeDtypeStruct((bsz, seq, A_HEADS * HEAD_DIM), MM_DTYPE),
        grid=(bsz, nk),
        in_specs=[qspec(A_HEADS * HEAD_DIM), qspec(IDX_HEADS * IDX_DIM), qspec(IDX_HEADS),
                  kspec(T, IDX_DIM), kspec(T, A_LATENT), kspec(A_LATENT, T),
                  _const_spec(bias_a.shape), _const_spec(wukT.shape), _const_spec(wuvT.shape)],
        out_specs=pl.BlockSpec((None, T, A_HEADS * HEAD_DIM), lambda b, i: (b, i, 0)),
        scratch_shapes=[pltpu.VMEM((nk, T, T), I32),
                        pltpu.VMEM((nk, KEY_BITS + 1, SUBLANES, T), I32),
                        pltpu.VMEM((A_HEADS, A_LATENT, T), MM_DTYPE),
                        pltpu.VMEM((A_HEADS, A_LATENT + SUM_ROWS, T), F32),
                        pltpu.VMEM((A_HEADS, T, T), F32),
                        pltpu.VMEM((A_HEADS * HEAD_DIM, T), F32)],
        compiler_params=_cparams(2), name="dsa_attention",
    )(aqT, iqT, iwT, kidx, kv, kvT, bias_a, wukT, wuvT)


def _diff_kernel(qT_ref, k_ref, vT_ref, bias_ref, lam_ref, gsub_ref, out_ref,
                 qz_ref, acc_ref, s_ref, oT_ref, *, lam_init):
    T = ATT_TILE
    dv = 2 * HEAD_DIM
    n_chain = 2 * B_HEADS
    qi = pl.program_id(1)
    half = lax.broadcasted_iota(I32, (dv, T), 0) < HEAD_DIM
    for h in range(B_HEADS):
        q = qT_ref[h * dv:(h + 1) * dv, :].astype(F32)
        qz_ref[2 * h] = jnp.where(half, q, 0.0).astype(MM_DTYPE)
        qz_ref[2 * h + 1] = jnp.where(half, 0.0, q).astype(MM_DTYPE)
    s_loc = lax.broadcasted_iota(I32, (T, T), 0)
    t_loc = lax.broadcasted_iota(I32, (T, T), 1)
    acc_ref[...] = jnp.zeros(acc_ref.shape, F32)

    def logits(kj, c):
        h = c // 2
        off = jnp.minimum(qi - kj, N_OFFSETS - 1)
        return bias_ref[h, off] + jnp.dot(k_ref[kj, :, h * dv:(h + 1) * dv], qz_ref[c],
                                          preferred_element_type=F32)

    for c in range(n_chain):
        s_ref[c] = logits(0, c)

    def step(kj, ms, diagonal):
        new_m = []
        for c in range(n_chain):
            h = c // 2
            s = s_ref[c]
            if diagonal:
                s = jnp.where(s_loc <= t_loc, s, NEG)
            p, alpha, m_new = _softmax_step(s, ms[c])
            if not diagonal:
                s_ref[c] = logits(kj + 1, c)
            vT = _with_ones_row(vT_ref[kj, h * dv:(h + 1) * dv, :])
            acc_ref[c] = alpha * acc_ref[c] + jnp.dot(vT, p, preferred_element_type=F32)
            new_m.append(m_new)
        return tuple(new_m)

    ms = lax.fori_loop(0, qi, lambda kj, cr: step(kj, cr, False), _initial_max(n_chain, T))
    step(qi, ms, True)

    lr = lam_ref[...]
    lam = (jnp.exp(jnp.sum(lr[0:1, :] * lr[1:2, :], axis=1, keepdims=True))
           - jnp.exp(jnp.sum(lr[2:3, :] * lr[3:4, :], axis=1, keepdims=True)) + lam_init)

    def normalised(c):
        return acc_ref[c, 0:dv, :] / acc_ref[c, dv:dv + 1, :]

    for h in range(B_HEADS):
        attn = normalised(2 * h) - lam * normalised(2 * h + 1)
        y = attn * lax.rsqrt(jnp.mean(attn * attn, axis=0, keepdims=True) + EPS)
        oT_ref[h * dv:(h + 1) * dv, :] = y * gsub_ref[...] * (1.0 - lam_init)
    out_ref[...] = oT_ref[...].T.astype(out_ref.dtype)


def _diff(bqT, bk, bvT, bias_b, lam_rows, gsub, lam_init):
    bsz, _, seq = bqT.shape
    T = ATT_TILE
    nk = seq // T
    dv = 2 * HEAD_DIM
    qspec = pl.BlockSpec((None, B_HEADS * dv, T), lambda b, i: (b, 0, i))
    return pl.pallas_call(
        functools.partial(_diff_kernel, lam_init=lam_init),
        out_shape=jax.ShapeDtypeStruct((bsz, seq, B_HEADS * dv), MM_DTYPE),
        grid=(bsz, nk),
        in_specs=[qspec,
                  pl.BlockSpec((None, nk, T, B_HEADS * dv), lambda b, i: (b, 0, 0, 0)),
                  pl.BlockSpec((None, nk, B_HEADS * dv, T), lambda b, i: (b, 0, 0, 0)),
                  _const_spec(bias_b.shape), _const_spec((4, HEAD_DIM)), _const_spec((dv, 1))],
        out_specs=pl.BlockSpec((None, T, B_HEADS * dv), lambda b, i: (b, i, 0)),
        scratch_shapes=[pltpu.VMEM((2 * B_HEADS, dv, T), MM_DTYPE),
                        pltpu.VMEM((2 * B_HEADS, dv + SUM_ROWS, T), F32),
                        pltpu.VMEM((2 * B_HEADS, T, T), F32),
                        pltpu.VMEM((B_HEADS * dv, T), F32)],
        compiler_params=_cparams(2), name="diff_attention",
    )(bqT, bk, bvT, bias_b, lam_rows, gsub)


def _dil_kernel(cur_ref, halo_ref, bias_ref, out_ref, lse_ref, *, tq):
    n = C_BAND
    wid = C_HPG * HEAD_DIM
    halo_lo = jnp.where(pl.program_id(2) == 0, jnp.int32(n), jnp.int32(0))
    i = lax.broadcasted_iota(I32, (n, 2 * n), 0)
    j = lax.broadcasted_iota(I32, (n, 2 * n), 1)
    lane_head = lax.broadcasted_iota(I32, (n, wid), 1) // HEAD_DIM
    in_head = [lane_head == h for h in range(C_HPG)]
    band = jnp.where(j >= i, jnp.where(j <= i + n, 0.0, NEG), NEG)
    band0 = jnp.where(j >= jnp.maximum(i, halo_lo), jnp.where(j <= i + n, 0.0, NEG), NEG)
    bias = [bias_ref[h] + band for h in range(C_HPG)]
    bias0 = [bias_ref[h] + band0 for h in range(C_HPG)]

    def band_rows(r, c, lo, hi):
        if c == 0:
            return jnp.concatenate([halo_ref[r, :, lo:hi], cur_ref[r, 0:n, lo:hi]], axis=0)
        return cur_ref[r, (c - 1) * n:(c + 1) * n, lo:hi]

    blocks = [(r, c) for r in range(cur_ref.shape[0]) for c in range(tq // n)]
    logits = []
    for r, c in blocks:
        q = cur_ref[r, c * n:(c + 1) * n, 0:wid].astype(F32) * ATTN_SCALE
        keys = band_rows(r, c, wid, 2 * wid)
        for h in range(C_HPG):
            qh = jnp.where(in_head[h], q, 0.0).astype(MM_DTYPE)
            s = lax.dot_general(qh, keys, (((1,), (1,)), ((), ())), preferred_element_type=F32)
            logits.append(s + (bias0 if c == 0 else bias)[h])
    for k, (r, c) in enumerate(blocks):
        vals = band_rows(r, c, 2 * wid, 3 * wid)
        out = jnp.zeros((n, wid), F32)
        lse = jnp.zeros((n, wid), F32)
        for h in range(C_HPG):
            s = logits[k * C_HPG + h]
            m = jnp.max(s, axis=1, keepdims=True)
            p = jnp.exp(s - m)
            den = jnp.sum(p, axis=1, keepdims=True)
            o = jnp.dot(p.astype(MM_DTYPE), vals, preferred_element_type=F32) * (1.0 / den)
            out = jnp.where(in_head[h], o, out)
            lse = jnp.where(in_head[h], m + jnp.log(den), lse)
        out_ref[r, c * n:(c + 1) * n, :] = out
        lse_ref[r, c * n:(c + 1) * n, :] = lse


def _dilated_group(cg, bias_g, g):
    bsz, dil, m, gw = cg.shape
    wid = C_HPG * HEAD_DIM
    n = C_BAND
    assert m % n == 0 and gw == 3 * wid
    tq = min(m, 512)
    n_res = max(1, min(dil, 512 // tq))
    cur = pl.BlockSpec((None, n_res, tq, gw), lambda b, r, i: (b, r, i, 0))
    halo = pl.BlockSpec((None, n_res, n, gw),
                        lambda b, r, i: (b, r, jnp.maximum(i * (tq // n) - 1, 0), 0))
    outspec = pl.BlockSpec((None, n_res, tq, wid), lambda b, r, i: (b, r, i, 0))
    return pl.pallas_call(
        functools.partial(_dil_kernel, tq=tq),
        out_shape=[jax.ShapeDtypeStruct((bsz, dil, m, wid), F32)] * 2,
        grid=(bsz, dil // n_res, m // tq),
        in_specs=[cur, halo, pl.BlockSpec((C_HPG, n, 2 * n), lambda b, r, i: (0, 0, 0))],
        out_specs=[outspec, outspec],
        compiler_params=_cparams(3), name=f"dilated_group{g}",
    )(cg, cg, bias_g)


def _merge_ffn_kernel(x_ref, mod_ref, g1_ref, g2_ref, gf_ref, oa_ref, ob_ref,
                      c0_ref, c1_ref, c2_ref, s0_ref, s1_ref, s2_ref,
                      wza, wzb, wzc, wba, wbb, wbc, wo, wgu, wd, out_ref, tok_ref, *, final_norm):
    x = x_ref[...]
    tm = x.shape[0]
    h = _rms(x) * g1_ref[...]
    h = h * (1.0 + mod_ref[1:2, :]) + mod_ref[0:1, :]
    hb = h.astype(MM_DTYPE)

    def token_order(k, ref):
        dil, _, w = ref.shape
        if dil == 1:
            return ref[0]
        n_chunk = w // LANES
        for r in range(dil):
            for j in range(n_chunk):
                tok_ref[k * n_chunk + j, pl.ds(r, tm // dil, stride=dil), :] = ref[
                    r, :, j * LANES:(j + 1) * LANES]
        return jnp.concatenate([tok_ref[k * n_chunk + j] for j in range(n_chunk)], axis=1)

    s0, s1, s2 = s0_ref[0], token_order(0, s1_ref), token_order(1, s2_ref)
    c0, c1, c2 = c0_ref[0], token_order(2, c1_ref), token_order(3, c2_ref)
    mx = jnp.maximum(jnp.maximum(s0, s1), s2)
    e0, e1, e2 = jnp.exp(s0 - mx), jnp.exp(s1 - mx), jnp.exp(s2 - mx)
    oc = (e0 * c0 + e1 * c1 + e2 * c2) / (e0 + e1 + e2)

    def gated(wz, o, wb):
        z = jnp.dot(hb, wz[...], preferred_element_type=F32)
        return jax.nn.sigmoid(z) * jnp.dot(o, wb[...], preferred_element_type=F32)

    merged = (gated(wza, oa_ref[...], wba) + gated(wzb, ob_ref[...], wbb)
              + gated(wzc, oc.astype(MM_DTYPE), wbc))
    y = jnp.dot(merged.astype(MM_DTYPE), wo[...], preferred_element_type=F32)
    x = x + mod_ref[2:3, :] * y

    h = _rms(x) * g2_ref[...]
    h = h * (1.0 + mod_ref[4:5, :]) + mod_ref[3:4, :]
    hb = h.astype(MM_DTYPE)
    acc = jnp.zeros(x.shape, F32)
    for c in range(D_FF // FFN_CHUNK):
        cols = slice(c * FFN_CHUNK, (c + 1) * FFN_CHUNK)
        fg = jnp.dot(hb, wgu[:, cols], preferred_element_type=F32)
        fu = jnp.dot(hb, wgu[:, D_FF + c * FFN_CHUNK:D_FF + (c + 1) * FFN_CHUNK],
                     preferred_element_type=F32)
        act = (fg * jax.nn.sigmoid(fg) * fu).astype(MM_DTYPE)
        acc = acc + jnp.dot(act, wd[cols, :], preferred_element_type=F32)
    y = x + mod_ref[5:6, :] * acc
    if final_norm:
        y = _rms(y) * gf_ref[...]
    out_ref[...] = y


def _merge_ffn(x2d, mod_l, g1, g2, gf, oa, ob, ocs, lses, ws, seq, final_norm):
    n, d = x2d.shape
    tm = ROW_TILE
    per_b = seq // tm
    row = lambda wd: pl.BlockSpec((tm, wd), lambda i: (i, 0))
    vec = _const_spec((1, d))
    wid = C_HPG * HEAD_DIM
    res = [pl.BlockSpec((None, dil, tm // dil, wid), lambda i: (i // per_b, 0, i % per_b, 0))
           for _, dil in C_GROUPS]
    in_specs = [row(d), pl.BlockSpec((None, 6, d), lambda i: (i // per_b, 0, 0)), vec, vec, vec,
                row(oa.shape[1]), row(ob.shape[1])] + res + res
    in_specs += [_const_spec(w.shape) for w in ws]
    return pl.pallas_call(
        functools.partial(_merge_ffn_kernel, final_norm=final_norm),
        out_shape=jax.ShapeDtypeStruct((n, d), F32), grid=(n // tm,),
        in_specs=in_specs, out_specs=row(d),
        scratch_shapes=[pltpu.VMEM((4 * wid // LANES, tm, LANES), F32)],
        compiler_params=_cparams(1), name="merge_ffn",
    )(x2d, mod_l, g1, g2, gf, oa, ob, *ocs, *lses, *ws)


def kernel(x, c, w_ada, b_ada, g_norm1, w_in, w_uk, w_uv, g_kv, lam_q1, lam_k1, lam_q2, lam_k2,
           g_subln, w_branch_a, w_branch_b, w_branch_c, w_out, g_norm2, w_gate_up, w_down,
           rel_bias, g_final):
    bsz, seq, d = x.shape
    depth = w_ada.shape[0]
    T = ATT_TILE
    nk = seq // T
    assert d == D_MODEL and seq % T == 0 and seq % ROW_TILE == 0
    n = bsz * seq
    cast = lambda w: w.astype(MM_DTYPE)

    thresholds = _bucket_thresholds(seq + 2 * C_BAND * C_GROUPS[-1][1])
    assert seq <= N_OFFSETS * T or (N_OFFSETS - 2) * T + 1 >= thresholds[-1]
    tab = rel_bias.reshape(-1)
    bias_a = _bias_att_tiles(tab, 0, A_HEADS, T, thresholds)
    bias_b = _bias_att_tiles(tab, A_HEADS, B_HEADS, T, thresholds)
    bias_c = _bias_dil_tiles(tab, A_HEADS + B_HEADS, thresholds)

    mod = _modulation(c, w_ada, b_ada).reshape(depth, bsz, 6, d)

    splits = (A_HEADS * HEAD_DIM, A_LATENT, IDX_HEADS * IDX_DIM, IDX_DIM, IDX_HEADS,
              B_HEADS * 2 * HEAD_DIM, B_HEADS * 2 * HEAD_DIM, B_HEADS * 2 * HEAD_DIM,
              C_HEADS * HEAD_DIM, C_HEADS * HEAD_DIM, C_HEADS * HEAD_DIM, d, d, d)
    offs = np.concatenate([[0], np.cumsum(splits)])
    seg = lambda w, k: w[:, int(offs[k]):int(offs[k + 1])]

    x2d = x.reshape(n, d)
    for l in range(depth):
        wl = w_in[l]
        castT = lambda w: w.T.astype(MM_DTYPE)
        wT_iw = jnp.pad(seg(wl, 4).T, ((0, 16 - IDX_HEADS), (0, 0)))
        wid = C_HPG * HEAD_DIM
        w_c = jnp.concatenate([seg(wl, k)[:, g * wid:(g + 1) * wid]
                               for g in range(len(C_GROUPS)) for k in (8, 9, 10)], axis=1)
        ws_in = [castT(seg(wl, 0)), castT(seg(wl, 2)), cast(wT_iw), castT(seg(wl, 5)), castT(seg(wl, 7)),
                 cast(seg(wl, 3)), cast(seg(wl, 1)), cast(seg(wl, 6)), cast(w_c)]
        g1 = g_norm1[l].reshape(1, d)
        (aqT, iqT, iwT, bqT, bvT, ik, kv, kvT, bk, cg0, cg1, cg2) = _in_proj(
            x2d, mod[l], g1, g_kv[l].reshape(1, A_LATENT), ws_in, bsz, seq)

        o_a = _dsa(aqT, iqT, iwT,
                   ik.reshape(bsz, nk, T, IDX_DIM), kv.reshape(bsz, nk, T, A_LATENT), kvT,
                   bias_a, cast(w_uk[l].transpose(0, 2, 1)), cast(w_uv[l].transpose(0, 2, 1)))
        o_a = o_a.reshape(n, -1)

        lam_init = 0.8 - 0.6 * math.exp(-0.3 * l)
        lam_rows = jnp.stack([lam_q1[l], lam_k1[l], lam_q2[l], lam_k2[l]])
        dv = 2 * HEAD_DIM
        o_b = _diff(bqT, bk.reshape(bsz, nk, T, B_HEADS * dv), bvT, bias_b, lam_rows,
                    g_subln[l].reshape(dv, 1), lam_init)
        o_b = o_b.reshape(n, -1)

        ocs, lses = [], []
        for g, (cg, (window, dil)) in enumerate(zip((cg0, cg1, cg2), C_GROUPS)):
            assert window // dil == C_BAND
            o, s = _dilated_group(cg, bias_c[g * C_HPG:(g + 1) * C_HPG], g)
            ocs.append(o)
            lses.append(s)

        ws_out = [cast(seg(wl, 11)), cast(seg(wl, 12)), cast(seg(wl, 13)),
                  cast(w_branch_a[l]), cast(w_branch_b[l]), cast(w_branch_c[l]), cast(w_out[l]),
                  cast(w_gate_up[l]), cast(w_down[l])]
        x2d = _merge_ffn(x2d, mod[l], g1, g_norm2[l].reshape(1, d), g_final.reshape(1, d),
                         o_a, o_b, ocs, lses, ws_out, seq, final_norm=(l == depth - 1))
    return x2d.reshape(bsz, seq, d)
```

```python
import functools
import math

import numpy as np
import jax
import jax.numpy as jnp
from jax import lax
from jax.experimental import pallas as pl
from jax.experimental.pallas import tpu as pltpu

D_MODEL = 1024
HEAD_DIM = 64
ATTN_SCALE = HEAD_DIM ** -0.5
LOG2E = math.log2(math.e)
A_HEADS = 8
A_LATENT = 128
IDX_HEADS = 8
IDX_DIM = 64
IDX_SCALE = (IDX_HEADS * IDX_DIM) ** -0.5
TOPK_MAX = 256
B_HEADS = 4
C_GROUPS = ((128, 1), (512, 4), (2048, 16))
C_HPG = 4
C_HEADS = C_HPG * len(C_GROUPS)
N_BUCKETS = 32
MAX_DISTANCE = 2048
N_BIAS_HEADS = A_HEADS + B_HEADS + C_HEADS
D_FF = -(-8 * D_MODEL // (3 * 256)) * 256
EPS = 1e-6

MM_DTYPE = jnp.bfloat16
F32 = jnp.float32
I32 = jnp.int32

ATT_TILE = 256
N_OFFSETS = 8
C_BAND = 128
ROW_TILE = 512
FFN_CHUNK = 256
MOD_COLS = 1536
NEG = -1e30
INT_MIN = -2 ** 31
LANES = 128
SUBLANES = 8
KEY_BITS = 32
VMEM_LIMIT = 56 * 1024 * 1024


def _cparams(n_axes, vmem=VMEM_LIMIT):
    return pltpu.CompilerParams(dimension_semantics=("arbitrary",) * n_axes,
                                vmem_limit_bytes=vmem)


def _const_spec(shape):
    nd = len(shape)
    return pl.BlockSpec(shape, lambda *_: (0,) * nd, pipeline_mode=pl.Buffered(1))


def _bucket_thresholds(max_dist):
    n = np.arange(max_dist + 1)
    max_exact = N_BUCKETS // 2
    nf = np.maximum(n, 1).astype(np.float32)
    large = max_exact + (np.log(nf / np.float32(max_exact))
                         / np.float32(math.log(MAX_DISTANCE / max_exact))
                         * np.float32(N_BUCKETS - max_exact)).astype(np.int32)
    large = np.minimum(large, N_BUCKETS - 1)
    bucket = np.where(n < max_exact, n, large)
    assert np.all(np.diff(bucket) >= 0)
    thr = []
    for k in range(1, N_BUCKETS):
        idx = np.nonzero(bucket >= k)[0]
        thr.append(int(idx[0]) if idx.size else None)
    return thr


def _bias_from_dist(dist, tab_ref, col, thresholds, lo=0, hi=None):
    reached = [k for k, thr in enumerate(thresholds, start=1) if thr is not None]
    base = max([0] + [k for k in reached if thresholds[k - 1] <= lo])
    b = jnp.full(dist.shape, tab_ref[base * N_BIAS_HEADS + col], F32)
    for k in reached:
        thr = thresholds[k - 1]
        if thr > lo and (hi is None or thr <= hi):
            b = jnp.where(dist >= thr, tab_ref[k * N_BIAS_HEADS + col], b)
    return b


def _bias_att_kernel(tab_ref, out_ref, *, head0, thresholds):
    h = pl.program_id(0)
    n_off, tile, _ = out_ref.shape
    row = lax.broadcasted_iota(I32, (tile, tile), 0)
    colq = lax.broadcasted_iota(I32, (tile, tile), 1)
    for o in range(n_off):
        dist = jnp.maximum(o * tile + colq - row, 0)
        lo, hi = max(o * tile - (tile - 1), 0), o * tile + tile - 1
        out_ref[o] = _bias_from_dist(dist, tab_ref, head0 + h, thresholds, lo, hi) * LOG2E


def _bias_att_tiles(tab, head0, n_heads, tile, thresholds):
    return pl.pallas_call(
        functools.partial(_bias_att_kernel, head0=head0, thresholds=thresholds),
        out_shape=jax.ShapeDtypeStruct((n_heads, N_OFFSETS, tile, tile), F32),
        grid=(n_heads,),
        in_specs=[pl.BlockSpec(memory_space=pltpu.SMEM)],
        out_specs=pl.BlockSpec((None, N_OFFSETS, tile, tile), lambda h: (h, 0, 0, 0)),
        compiler_params=_cparams(1),
        name="bias_att_tiles",
    )(tab)


def _bias_dil_kernel(tab_ref, out_ref, *, head0, thresholds):
    h = pl.program_id(0)
    g = h // C_HPG
    dil = jnp.where(g == 0, C_GROUPS[0][1], jnp.where(g == 1, C_GROUPS[1][1], C_GROUPS[2][1]))
    i = lax.broadcasted_iota(I32, (C_BAND, 2 * C_BAND), 0)
    j = lax.broadcasted_iota(I32, (C_BAND, 2 * C_BAND), 1)
    dist = jnp.maximum((i - j + C_BAND) * dil, 0)
    out_ref[...] = _bias_from_dist(dist, tab_ref, head0 + h, thresholds)


def _bias_dil_tiles(tab, head0, thresholds):
    return pl.pallas_call(
        functools.partial(_bias_dil_kernel, head0=head0, thresholds=thresholds),
        out_shape=jax.ShapeDtypeStruct((C_HEADS, C_BAND, 2 * C_BAND), F32),
        grid=(C_HEADS,),
        in_specs=[pl.BlockSpec(memory_space=pltpu.SMEM)],
        out_specs=pl.BlockSpec((None, C_BAND, 2 * C_BAND), lambda h: (h, 0, 0)),
        compiler_params=_cparams(1),
        name="bias_dil_tiles",
    )(tab)


def _mod_kernel(c_ref, w_ref, b_ref, out_ref):
    c = c_ref[...]
    ca = (c * jax.nn.sigmoid(c)).astype(MM_DTYPE)
    out_ref[...] = jnp.dot(ca, w_ref[...].astype(MM_DTYPE), preferred_element_type=F32) + b_ref[...]


def _modulation(c, w_ada, b_ada):
    depth, d, wid = w_ada.shape
    bsz = c.shape[0]
    tn = MOD_COLS
    assert wid % tn == 0
    return pl.pallas_call(
        _mod_kernel,
        out_shape=jax.ShapeDtypeStruct((depth, bsz, wid), F32),
        grid=(depth, wid // tn),
        in_specs=[pl.BlockSpec((bsz, d), lambda l, j: (0, 0)),
                  pl.BlockSpec((None, d, tn), lambda l, j: (l, 0, j)),
                  pl.BlockSpec((None, 1, tn), lambda l, j: (l, 0, j))],
        out_specs=pl.BlockSpec((None, bsz, tn), lambda l, j: (l, 0, j)),
        compiler_params=_cparams(2),
        name="adaln_modulation",
    )(c, w_ada, b_ada.reshape(depth, 1, wid))


def _rms(x):
    return x * lax.rsqrt(jnp.mean(x * x, axis=-1, keepdims=True) + EPS)


def _in_kernel(x_ref, mod_ref, g1_ref, gkv_ref,
               wT_aq, wT_iq, wT_iw, wT_bq, wT_bv, w_ik, w_kv, w_bk, w_c,
               o_aqT, o_iqT, o_iwT, o_bqT, o_bvT, o_ik, o_kv, o_kvT, o_bk, o_c0, o_c1, o_c2,
               c_scr):
    T = ATT_TILE
    tm = x_ref.shape[0]
    h = _rms(x_ref[...]) * g1_ref[...]
    h = h * (1.0 + mod_ref[1:2, :]) + mod_ref[0:1, :]
    hb = h.astype(MM_DTYPE)

    def mm(w):
        return jnp.dot(hb, w[...], preferred_element_type=F32)

    def mm_t(wT):
        return lax.dot_general(wT[...], hb, (((1,), (1,)), ((), ())), preferred_element_type=F32)

    o_aqT[...] = mm_t(wT_aq).astype(o_aqT.dtype)
    o_iqT[...] = mm_t(wT_iq).astype(o_iqT.dtype)
    o_bqT[...] = (mm_t(wT_bq) * (ATTN_SCALE * LOG2E)).astype(o_bqT.dtype)
    o_iwT[...] = (mm_t(wT_iw) * IDX_SCALE)[:IDX_HEADS]
    bvT = mm_t(wT_bv).astype(o_bvT.dtype)
    kv = _rms(mm(w_kv)) * gkv_ref[...]
    kvT = kv.T.astype(o_kvT.dtype)
    for j in range(tm // T):
        o_bvT[j] = bvT[:, j * T:(j + 1) * T]
        o_kvT[j] = kvT[:, j * T:(j + 1) * T]
    o_kv[...] = kv.astype(o_kv.dtype)
    o_ik[...] = mm(w_ik).astype(o_ik.dtype)
    o_bk[...] = mm(w_bk).astype(o_bk.dtype)

    yc = mm(w_c)
    n_chunk = yc.shape[1] // LANES
    for j in range(n_chunk):
        c_scr[j] = yc[:, j * LANES:(j + 1) * LANES]
    per_group = n_chunk // len(C_GROUPS)
    for g, o_c in enumerate((o_c0, o_c1, o_c2)):
        dil = C_GROUPS[g][1]
        for r in range(dil):
            for jj in range(per_group):
                o_c[r, :, jj * LANES:(jj + 1) * LANES] = c_scr[
                    g * per_group + jj, pl.ds(r, tm // dil, stride=dil), :].astype(o_c.dtype)


def _in_proj(x2d, mod_l, g1, gkv, ws, bsz, seq):
    n, d = x2d.shape
    tm = ROW_TILE
    T = ATT_TILE
    per_b = seq // tm
    nk = seq // T
    hd = A_HEADS * HEAD_DIM
    bw = B_HEADS * 2 * HEAD_DIM
    gw = 3 * C_HPG * HEAD_DIM
    in_specs = [pl.BlockSpec((tm, d), lambda i: (i, 0)),
                pl.BlockSpec((None, 6, d), lambda i: (i // per_b, 0, 0)),
                _const_spec((1, d)), _const_spec((1, A_LATENT))]
    in_specs += [_const_spec(w.shape) for w in ws]

    def tspec(rows):
        return pl.BlockSpec((None, rows, tm), lambda i: (i // per_b, 0, i % per_b))

    def tile_tspec(rows):
        return pl.BlockSpec((None, tm // T, rows, T), lambda i: (i // per_b, i % per_b, 0, 0))

    def rspec(wd):
        return pl.BlockSpec((tm, wd), lambda i: (i, 0))

    def cspec(dil):
        return pl.BlockSpec((None, dil, tm // dil, gw), lambda i: (i // per_b, 0, i % per_b, 0))

    sds = jax.ShapeDtypeStruct
    out_specs = [tspec(hd), tspec(IDX_HEADS * IDX_DIM), tspec(IDX_HEADS), tspec(bw),
                 tile_tspec(bw), rspec(IDX_DIM), rspec(A_LATENT), tile_tspec(A_LATENT), rspec(bw)]
    out_shape = [sds((bsz, hd, seq), MM_DTYPE), sds((bsz, IDX_HEADS * IDX_DIM, seq), MM_DTYPE),
                 sds((bsz, IDX_HEADS, seq), F32), sds((bsz, bw, seq), MM_DTYPE),
                 sds((bsz, nk, bw, T), MM_DTYPE), sds((n, IDX_DIM), MM_DTYPE),
                 sds((n, A_LATENT), MM_DTYPE), sds((bsz, nk, A_LATENT, T), MM_DTYPE),
                 sds((n, bw), MM_DTYPE)]
    for _, dil in C_GROUPS:
        out_specs.append(cspec(dil))
        out_shape.append(sds((bsz, dil, seq // dil, gw), MM_DTYPE))
    return pl.pallas_call(
        _in_kernel, out_shape=out_shape, grid=(n // tm,),
        in_specs=in_specs, out_specs=out_specs,
        scratch_shapes=[pltpu.VMEM((3 * gw // LANES, tm, LANES), F32)],
        compiler_params=_cparams(1), name="in_proj",
    )(x2d, mod_l, g1, gkv, *ws)


def _softmax_step(s, m_old):
    m_new = jnp.maximum(m_old, jnp.max(s, axis=0, keepdims=True))
    p = jnp.exp2(s - m_new)
    alpha = jnp.exp2(m_old - m_new)
    return p.astype(MM_DTYPE), alpha, m_new


def _initial_max(n_chains, tile):
    return tuple(jnp.full((1, tile), NEG, F32) for _ in range(n_chains))


SUM_ROWS = 16


def _with_ones_row(vT):
    row = lax.broadcasted_iota(I32, (SUM_ROWS, vT.shape[1]), 0)
    ones = jnp.where(row == 0, 1.0, 0.0).astype(vT.dtype)
    return jnp.concatenate([vT, ones], axis=0)


def _bit_transpose32(words):
    a = list(words)
    j, mask = 16, 0x0000FFFF
    while j:
        k = 0
        while k < 32:
            t = (a[k] ^ lax.shift_right_logical(a[k + j], jnp.int32(j))) & jnp.int32(mask)
            a[k] = a[k] ^ t
            a[k + j] = a[k + j] ^ lax.shift_left(t, jnp.int32(j))
            k = (k + j + 1) & ~j
        j >>= 1
        mask = (mask ^ (mask << j)) & 0xFFFFFFFF
    return a


def _dsa_kernel(aqT_ref, iqT_ref, iwT_ref, kidx_ref, kv_ref, kvT_ref, bias_ref, wukT_ref, wuvT_ref,
                out_ref, keys_ref, planes_ref, qlat_ref, acc_ref, s_ref, oT_ref, *, topk, idx_bits):
    T = ATT_TILE
    qi = pl.program_id(1)
    nk = qi + 1

    for h in range(A_HEADS):
        q = jnp.dot(wukT_ref[h], aqT_ref[h * HEAD_DIM:(h + 1) * HEAD_DIM, :],
                    preferred_element_type=F32) * (ATTN_SCALE * LOG2E)
        qlat_ref[h] = q.astype(qlat_ref.dtype)

    s_loc = lax.broadcasted_iota(I32, (T, T), 0)
    t_loc = lax.broadcasted_iota(I32, (T, T), 1)

    def score_tile(kj, diagonal):
        kt = kidx_ref[kj]
        acc = jnp.zeros((T, T), F32)
        for h in range(IDX_HEADS):
            s = jnp.dot(kt, iqT_ref[h * IDX_DIM:(h + 1) * IDX_DIM, :], preferred_element_type=F32)
            acc = acc + jnp.maximum(s, 0.0) * iwT_ref[h:h + 1, :]
        bits = lax.bitcast_convert_type(acc, I32)
        key = jnp.where(bits < 0, bits ^ jnp.int32(0x7FFFFFFF), bits)
        if diagonal:
            key = jnp.where(s_loc <= t_loc, key, jnp.int32(INT_MIN))
        keys_ref[kj] = key

    def off_diagonal(kj, carry):
        score_tile(kj, False)
        return carry

    lax.fori_loop(0, qi, off_diagonal, 0)
    score_tile(qi, True)

    def slice_tile(kj, carry):
        for g in range(T // LANES):
            cols = slice(g * LANES, (g + 1) * LANES)
            planes = _bit_transpose32(
                [keys_ref[kj, SUBLANES * i:SUBLANES * (i + 1), cols] ^ jnp.int32(INT_MIN)
                 for i in range(KEY_BITS)])
            for b in range(KEY_BITS):
                planes_ref[kj, b, :, cols] = planes[b]
        return carry

    lax.fori_loop(0, nk, slice_tile, 0)

    n_tiles = planes_ref.shape[0]

    @pl.when(qi == 0)
    def _():
        def clear_planes(kj, carry):
            for b in range(KEY_BITS):
                planes_ref[kj, b] = jnp.zeros((SUBLANES, T), I32)
            return carry

        lax.fori_loop(1, n_tiles, clear_planes, 0)

    one = jnp.int32(1)
    nil = jnp.int32(0)
    zero = jnp.zeros((1, T), I32)

    n_rows = n_tiles * SUBLANES
    tile_of_row = lax.broadcasted_iota(I32, (n_rows, T), 0) // SUBLANES
    tied0 = jnp.where(tile_of_row < nk, jnp.int32(-1), nil)

    def bit_step(b, carry):
        tied, n_gt, kth_u = carry
        ones = tied & planes_ref[:, b].reshape(n_rows, T)
        n1 = jnp.sum(lax.population_count(ones), axis=0, keepdims=True)
        take = (n_gt + n1) >= topk
        tied = jnp.where(take, ones, tied ^ ones)
        n_gt = jnp.where(take, n_gt, n_gt + n1)
        kth_u = jnp.where(take, kth_u | lax.shift_left(one, jnp.int32(KEY_BITS - 1) - b), kth_u)
        return tied, n_gt, kth_u

    tied, n_gt, kth_u = lax.fori_loop(0, KEY_BITS, bit_step, (tied0, zero, zero))
    n_eq = jnp.sum(lax.population_count(tied), axis=0, keepdims=True)
    n_ge = jnp.where(kth_u == nil, n_gt, n_gt + n_eq)
    kth = jnp.maximum(kth_u ^ jnp.int32(INT_MIN), jnp.int32(INT_MIN + 1))

    def count(hit_fn):
        def body(kj, c):
            hit = hit_fn(keys_ref[kj], kj)
            return c + jnp.sum(hit.reshape(T // SUBLANES, SUBLANES, T), axis=0)
        c = lax.fori_loop(0, nk, body, jnp.zeros((SUBLANES, T), I32))
        return jnp.sum(c, axis=0, keepdims=True)

    @pl.when(jnp.max(n_ge) > topk)
    def _():
        need = topk - n_gt

        def pos_step(i, cut):
            cand = cut | jnp.left_shift(one, jnp.int32(idx_bits - 1) - i)
            c = count(lambda kk, kj: jnp.where(
                kk == kth, jnp.where((kj * T + s_loc) < cand, one, nil), nil))
            return jnp.where(c < need, cand, cut)

        cut = lax.fori_loop(0, idx_bits, pos_step, zero)

        def demote(kj, carry):
            kk = keys_ref[kj]
            lowered = jnp.where((kj * T + s_loc) > cut, kth - one, kk)
            keys_ref[kj] = jnp.where(kk == kth, lowered, kk)
            return carry

        lax.fori_loop(0, nk, demote, 0)

    acc_ref[...] = jnp.zeros(acc_ref.shape, F32)

    def logits(kj, h):
        off = jnp.minimum(qi - kj, N_OFFSETS - 1)
        return bias_ref[h, off] + jnp.dot(kv_ref[kj], qlat_ref[h], preferred_element_type=F32)

    for h in range(A_HEADS):
        s_ref[h] = logits(0, h)

    def attend(kj, ms):
        unselected = jnp.where(keys_ref[kj] >= kth, 0.0, NEG)
        kvTt = _with_ones_row(kvT_ref[kj])
        kj_next = jnp.minimum(kj + 1, nk - 1)
        new_m = []
        for h in range(A_HEADS):
            s = s_ref[h] + unselected
            p, alpha, m_new = _softmax_step(s, ms[h])
            s_ref[h] = logits(kj_next, h)
            acc_ref[h] = alpha * acc_ref[h] + jnp.dot(kvTt, p, preferred_element_type=F32)
            new_m.append(m_new)
        return tuple(new_m)

    lax.fori_loop(0, nk, attend, _initial_max(A_HEADS, T))

    for h in range(A_HEADS):
        o_lat = (acc_ref[h, 0:A_LATENT, :] / acc_ref[h, A_LATENT:A_LATENT + 1, :]).astype(MM_DTYPE)
        oT_ref[h * HEAD_DIM:(h + 1) * HEAD_DIM, :] = jnp.dot(
            wuvT_ref[h], o_lat, preferred_element_type=F32)
    out_ref[...] = oT_ref[...].T.astype(out_ref.dtype)


def _dsa(aqT, iqT, iwT, kidx, kv, kvT, bias_a, wukT, wuvT):
    bsz, _, seq = aqT.shape
    T = ATT_TILE
    nk = seq // T
    topk = min(TOPK_MAX, seq // 4)
    idx_bits = int(math.log2(seq))
    assert 2 ** idx_bits == seq
    assert T == KEY_BITS * SUBLANES
    qspec = lambda rows: pl.BlockSpec((None, rows, T), lambda b, i: (b, 0, i))
    kspec = lambda a, c: pl.BlockSpec((None, nk, a, c), lambda b, i: (b, 0, 0, 0))
    return pl.pallas_call(
        functools.partial(_dsa_kernel, topk=topk, idx_bits=idx_bits),
        out_shape=jax.ShapeDtypeStruct((bsz, seq, A_HEADS * HEAD_DIM), MM_DTYPE),
        grid=(bsz, nk),
        in_specs=[qspec(A_HEADS * HEAD_DIM), qspec(IDX_HEADS * IDX_DIM), qspec(IDX_HEADS),
                  kspec(T, IDX_DIM), kspec(T, A_LATENT), kspec(A_LATENT, T),
                  _const_spec(bias_a.shape), _const_spec(wukT.shape), _const_spec(wuvT.shape)],
        out_specs=pl.BlockSpec((None, T, A_HEADS * HEAD_DIM), lambda b, i: (b, i, 0)),
        scratch_shapes=[pltpu.VMEM((nk, T, T), I32),
                        pltpu.VMEM((nk, KEY_BITS + 1, SUBLANES, T), I32),
                        pltpu.VMEM((A_HEADS, A_LATENT, T), MM_DTYPE),
                        pltpu.VMEM((A_HEADS, A_LATENT + SUM_ROWS, T), F32),
                        pltpu.VMEM((A_HEADS, T, T), F32),
                        pltpu.VMEM((A_HEADS * HEAD_DIM, T), F32)],
        compiler_params=_cparams(2), name="dsa_attention",
    )(aqT, iqT, iwT, kidx, kv, kvT, bias_a, wukT, wuvT)


def _diff_kernel(qT_ref, k_ref, vT_ref, bias_ref, lam_ref, gsub_ref, out_ref,
                 qz_ref, acc_ref, s_ref, oT_ref, *, lam_init):
    T = ATT_TILE
    dv = 2 * HEAD_DIM
    n_chain = 2 * B_HEADS
    qi = pl.program_id(1)
    half = lax.broadcasted_iota(I32, (dv, T), 0) < HEAD_DIM
    for h in range(B_HEADS):
        q = qT_ref[h * dv:(h + 1) * dv, :].astype(F32)
        qz_ref[2 * h] = jnp.where(half, q, 0.0).astype(MM_DTYPE)
        qz_ref[2 * h + 1] = jnp.where(half, 0.0, q).astype(MM_DTYPE)
    s_loc = lax.broadcasted_iota(I32, (T, T), 0)
    t_loc = lax.broadcasted_iota(I32, (T, T), 1)
    acc_ref[...] = jnp.zeros(acc_ref.shape, F32)

    def logits(kj, c):
        h = c // 2
        off = jnp.minimum(qi - kj, N_OFFSETS - 1)
        return bias_ref[h, off] + jnp.dot(k_ref[kj, :, h * dv:(h + 1) * dv], qz_ref[c],
                                          preferred_element_type=F32)

    for c in range(n_chain):
        s_ref[c] = logits(0, c)

    def step(kj, ms, diagonal):
        new_m = []
        for c in range(n_chain):
            h = c // 2
            s = s_ref[c]
            if diagonal:
                s = jnp.where(s_loc <= t_loc, s, NEG)
            p, alpha, m_new = _softmax_step(s, ms[c])
            if not diagonal:
                s_ref[c] = logits(kj + 1, c)
            vT = _with_ones_row(vT_ref[kj, h * dv:(h + 1) * dv, :])
            acc_ref[c] = alpha * acc_ref[c] + jnp.dot(vT, p, preferred_element_type=F32)
            new_m.append(m_new)
        return tuple(new_m)

    ms = lax.fori_loop(0, qi, lambda kj, cr: step(kj, cr, False), _initial_max(n_chain, T))
    step(qi, ms, True)

    lr = lam_ref[...]
    lam = (jnp.exp(jnp.sum(lr[0:1, :] * lr[1:2, :], axis=1, keepdims=True))
           - jnp.exp(jnp.sum(lr[2:3, :] * lr[3:4, :], axis=1, keepdims=True)) + lam_init)

    def normalised(c):
        return acc_ref[c, 0:dv, :] / acc_ref[c, dv:dv + 1, :]

    for h in range(B_HEADS):
        attn = normalised(2 * h) - lam * normalised(2 * h + 1)
        y = attn * lax.rsqrt(jnp.mean(attn * attn, axis=0, keepdims=True) + EPS)
        oT_ref[h * dv:(h + 1) * dv, :] = y * gsub_ref[...] * (1.0 - lam_init)
    out_ref[...] = oT_ref[...].T.astype(out_ref.dtype)


def _diff(bqT, bk, bvT, bias_b, lam_rows, gsub, lam_init):
    bsz, _, seq = bqT.shape
    T = ATT_TILE
    nk = seq // T
    dv = 2 * HEAD_DIM
    qspec = pl.BlockSpec((None, B_HEADS * dv, T), lambda b, i: (b, 0, i))
    return pl.pallas_call(
        functools.partial(_diff_kernel, lam_init=lam_init),
        out_shape=jax.ShapeDtypeStruct((bsz, seq, B_HEADS * dv), MM_DTYPE),
        grid=(bsz, nk),
        in_specs=[qspec,
                  pl.BlockSpec((None, nk, T, B_HEADS * dv), lambda b, i: (b, 0, 0, 0)),
                  pl.BlockSpec((None, nk, B_HEADS * dv, T), lambda b, i: (b, 0, 0, 0)),
                  _const_spec(bias_b.shape), _const_spec((4, HEAD_DIM)), _const_spec((dv, 1))],
        out_specs=pl.BlockSpec((None, T, B_HEADS * dv), lambda b, i: (b, i, 0)),
        scratch_shapes=[pltpu.VMEM((2 * B_HEADS, dv, T), MM_DTYPE),
                        pltpu.VMEM((2 * B_HEADS, dv + SUM_ROWS, T), F32),
                        pltpu.VMEM((2 * B_HEADS, T, T), F32),
                        pltpu.VMEM((B_HEADS * dv, T), F32)],
        compiler_params=_cparams(2), name="diff_attention",
    )(bqT, bk, bvT, bias_b, lam_rows, gsub)


def _dil_kernel(cur_ref, halo_ref, bias_ref, out_ref, lse_ref, *, tq):
    n = C_BAND
    wid = C_HPG * HEAD_DIM
    halo_lo = jnp.where(pl.program_id(2) == 0, jnp.int32(n), jnp.int32(0))
    i = lax.broadcasted_iota(I32, (n, 2 * n), 0)
    j = lax.broadcasted_iota(I32, (n, 2 * n), 1)
    lane_head = lax.broadcasted_iota(I32, (n, wid), 1) // HEAD_DIM
    in_head = [lane_head == h for h in range(C_HPG)]
    band = jnp.where(j >= i, jnp.where(j <= i + n, 0.0, NEG), NEG)
    band0 = jnp.where(j >= jnp.maximum(i, halo_lo), jnp.where(j <= i + n, 0.0, NEG), NEG)
    bias = [bias_ref[h] + band for h in range(C_HPG)]
    bias0 = [bias_ref[h] + band0 for h in range(C_HPG)]

    def band_rows(r, c, lo, hi):
        if c == 0:
            return jnp.concatenate([halo_ref[r, :, lo:hi], cur_ref[r, 0:n, lo:hi]], axis=0)
        return cur_ref[r, (c - 1) * n:(c + 1) * n, lo:hi]

    blocks = [(r, c) for r in range(cur_ref.shape[0]) for c in range(tq // n)]
    logits = []
    for r, c in blocks:
        q = cur_ref[r, c * n:(c + 1) * n, 0:wid].astype(F32) * ATTN_SCALE
        keys = band_rows(r, c, wid, 2 * wid)
        for h in range(C_HPG):
            qh = jnp.where(in_head[h], q, 0.0).astype(MM_DTYPE)
            s = lax.dot_general(qh, keys, (((1,), (1,)), ((), ())), preferred_element_type=F32)
            logits.append(s + (bias0 if c == 0 else bias)[h])
    for k, (r, c) in enumerate(blocks):
        vals = band_rows(r, c, 2 * wid, 3 * wid)
        out = jnp.zeros((n, wid), F32)
        lse = jnp.zeros((n, wid), F32)
        for h in range(C_HPG):
            s = logits[k * C_HPG + h]
            m = jnp.max(s, axis=1, keepdims=True)
            p = jnp.exp(s - m)
            den = jnp.sum(p, axis=1, keepdims=True)
            o = jnp.dot(p.astype(MM_DTYPE), vals, preferred_element_type=F32) * (1.0 / den)
            out = jnp.where(in_head[h], o, out)
            lse = jnp.where(in_head[h], m + jnp.log(den), lse)
        out_ref[r, c * n:(c + 1) * n, :] = out
        lse_ref[r, c * n:(c + 1) * n, :] = lse


def _dilated_group(cg, bias_g, g):
    bsz, dil, m, gw = cg.shape
    wid = C_HPG * HEAD_DIM
    n = C_BAND
    assert m % n == 0 and gw == 3 * wid
    tq = min(m, 512)
    n_res = max(1, min(dil, 512 // tq))
    cur = pl.BlockSpec((None, n_res, tq, gw), lambda b, r, i: (b, r, i, 0))
    halo = pl.BlockSpec((None, n_res, n, gw),
                        lambda b, r, i: (b, r, jnp.maximum(i * (tq // n) - 1, 0), 0))
    outspec = pl.BlockSpec((None, n_res, tq, wid), lambda b, r, i: (b, r, i, 0))
    return pl.pallas_call(
        functools.partial(_dil_kernel, tq=tq),
        out_shape=[jax.ShapeDtypeStruct((bsz, dil, m, wid), F32)] * 2,
        grid=(bsz, dil // n_res, m // tq),
        in_specs=[cur, halo, pl.BlockSpec((C_HPG, n, 2 * n), lambda b, r, i: (0, 0, 0))],
        out_specs=[outspec, outspec],
        compiler_params=_cparams(3), name=f"dilated_group{g}",
    )(cg, cg, bias_g)


def _merge_ffn_kernel(x_ref, mod_ref, g1_ref, g2_ref, gf_ref, oa_ref, ob_ref,
                      c0_ref, c1_ref, c2_ref, s0_ref, s1_ref, s2_ref,
                      wza, wzb, wzc, wba, wbb, wbc, wo, wgu, wd, out_ref, tok_ref, *, final_norm):
    x = x_ref[...]
    tm = x.shape[0]
    h = _rms(x) * g1_ref[...]
    h = h * (1.0 + mod_ref[1:2, :]) + mod_ref[0:1, :]
    hb = h.astype(MM_DTYPE)

    def token_order(k, ref):
        dil, _, w = ref.shape
        if dil == 1:
            return ref[0]
        n_chunk = w // LANES
        for r in range(dil):
            for j in range(n_chunk):
                tok_ref[k * n_chunk + j, pl.ds(r, tm // dil, stride=dil), :] = ref[
                    r, :, j * LANES:(j + 1) * LANES]
        return jnp.concatenate([tok_ref[k * n_chunk + j] for j in range(n_chunk)], axis=1)

    s0, s1, s2 = s0_ref[0], token_order(0, s1_ref), token_order(1, s2_ref)
    c0, c1, c2 = c0_ref[0], token_order(2, c1_ref), token_order(3, c2_ref)
    mx = jnp.maximum(jnp.maximum(s0, s1), s2)
    e0, e1, e2 = jnp.exp(s0 - mx), jnp.exp(s1 - mx), jnp.exp(s2 - mx)
    oc = (e0 * c0 + e1 * c1 + e2 * c2) / (e0 + e1 + e2)

    def gated(wz, o, wb):
        z = jnp.dot(hb, wz[...], preferred_element_type=F32)
        return jax.nn.sigmoid(z) * jnp.dot(o, wb[...], preferred_element_type=F32)

    merged = (gated(wza, oa_ref[...], wba) + gated(wzb, ob_ref[...], wbb)
              + gated(wzc, oc.astype(MM_DTYPE), wbc))
    y = jnp.dot(merged.astype(MM_DTYPE), wo[...], preferred_element_type=F32)
    x = x + mod_ref[2:3, :] * y

    h = _rms(x) * g2_ref[...]
    h = h * (1.0 + mod_ref[4:5, :]) + mod_ref[3:4, :]
    hb = h.astype(MM_DTYPE)
    acc = jnp.zeros(x.shape, F32)
    for c in range(D_FF // FFN_CHUNK):
        cols = slice(c * FFN_CHUNK, (c + 1) * FFN_CHUNK)
        fg = jnp.dot(hb, wgu[:, cols], preferred_element_type=F32)
        fu = jnp.dot(hb, wgu[:, D_FF + c * FFN_CHUNK:D_FF + (c + 1) * FFN_CHUNK],
                     preferred_element_type=F32)
        act = (fg * jax.nn.sigmoid(fg) * fu).astype(MM_DTYPE)
        acc = acc + jnp.dot(act, wd[cols, :], preferred_element_type=F32)
    y = x + mod_ref[5:6, :] * acc
    if final_norm:
        y = _rms(y) * gf_ref[...]
    out_ref[...] = y


def _merge_ffn(x2d, mod_l, g1, g2, gf, oa, ob, ocs, lses, ws, seq, final_norm):
    n, d = x2d.shape
    tm = ROW_TILE
    per_b = seq // tm
    row = lambda wd: pl.BlockSpec((tm, wd), lambda i: (i, 0))
    vec = _const_spec((1, d))
    wid = C_HPG * HEAD_DIM
    res = [pl.BlockSpec((None, dil, tm // dil, wid), lambda i: (i // per_b, 0, i % per_b, 0))
           for _, dil in C_GROUPS]
    in_specs = [row(d), pl.BlockSpec((None, 6, d), lambda i: (i // per_b, 0, 0)), vec, vec, vec,
                row(oa.shape[1]), row(ob.shape[1])] + res + res
    in_specs += [_const_spec(w.shape) for w in ws]
    return pl.pallas_call(
        functools.partial(_merge_ffn_kernel, final_norm=final_norm),
        out_shape=jax.ShapeDtypeStruct((n, d), F32), grid=(n // tm,),
        in_specs=in_specs, out_specs=row(d),
        scratch_shapes=[pltpu.VMEM((4 * wid // LANES, tm, LANES), F32)],
        compiler_params=_cparams(1), name="merge_ffn",
    )(x2d, mod_l, g1, g2, gf, oa, ob, *ocs, *lses, *ws)


def kernel(x, c, w_ada, b_ada, g_norm1, w_in, w_uk, w_uv, g_kv, lam_q1, lam_k1, lam_q2, lam_k2,
           g_subln, w_branch_a, w_branch_b, w_branch_c, w_out, g_norm2, w_gate_up, w_down,
           rel_bias, g_final):
    bsz, seq, d = x.shape
    depth = w_ada.shape[0]
    T = ATT_TILE
    nk = seq // T
    assert d == D_MODEL and seq % T == 0 and seq % ROW_TILE == 0
    n = bsz * seq
    cast = lambda w: w.astype(MM_DTYPE)

    thresholds = _bucket_thresholds(seq + 2 * C_BAND * C_GROUPS[-1][1])
    assert seq <= N_OFFSETS * T or (N_OFFSETS - 2) * T + 1 >= thresholds[-1]
    tab = rel_bias.reshape(-1)
    bias_a = _bias_att_tiles(tab, 0, A_HEADS, T, thresholds)
    bias_b = _bias_att_tiles(tab, A_HEADS, B_HEADS, T, thresholds)
    bias_c = _bias_dil_tiles(tab, A_HEADS + B_HEADS, thresholds)

    mod = _modulation(c, w_ada, b_ada).reshape(depth, bsz, 6, d)

    splits = (A_HEADS * HEAD_DIM, A_LATENT, IDX_HEADS * IDX_DIM, IDX_DIM, IDX_HEADS,
              B_HEADS * 2 * HEAD_DIM, B_HEADS * 2 * HEAD_DIM, B_HEADS * 2 * HEAD_DIM,
              C_HEADS * HEAD_DIM, C_HEADS * HEAD_DIM, C_HEADS * HEAD_DIM, d, d, d)
    offs = np.concatenate([[0], np.cumsum(splits)])
    seg = lambda w, k: w[:, int(offs[k]):int(offs[k + 1])]

    x2d = x.reshape(n, d)
    for l in range(depth):
        wl = w_in[l]
        castT = lambda w: w.T.astype(MM_DTYPE)
        wT_iw = jnp.pad(seg(wl, 4).T, ((0, 16 - IDX_HEADS), (0, 0)))
        wid = C_HPG * HEAD_DIM
        w_c = jnp.concatenate([seg(wl, k)[:, g * wid:(g + 1) * wid]
                               for g in range(len(C_GROUPS)) for k in (8, 9, 10)], axis=1)
        ws_in = [castT(seg(wl, 0)), castT(seg(wl, 2)), cast(wT_iw), castT(seg(wl, 5)), castT(seg(wl, 7)),
                 cast(seg(wl, 3)), cast(seg(wl, 1)), cast(seg(wl, 6)), cast(w_c)]
        g1 = g_norm1[l].reshape(1, d)
        (aqT, iqT, iwT, bqT, bvT, ik, kv, kvT, bk, cg0, cg1, cg2) = _in_proj(
            x2d, mod[l], g1, g_kv[l].reshape(1, A_LATENT), ws_in, bsz, seq)

        o_a = _dsa(aqT, iqT, iwT,
                   ik.reshape(bsz, nk, T, IDX_DIM), kv.reshape(bsz, nk, T, A_LATENT), kvT,
                   bias_a, cast(w_uk[l].transpose(0, 2, 1)), cast(w_uv[l].transpose(0, 2, 1)))
        o_a = o_a.reshape(n, -1)

        lam_init = 0.8 - 0.6 * math.exp(-0.3 * l)
        lam_rows = jnp.stack([lam_q1[l], lam_k1[l], lam_q2[l], lam_k2[l]])
        dv = 2 * HEAD_DIM
        o_b = _diff(bqT, bk.reshape(bsz, nk, T, B_HEADS * dv), bvT, bias_b, lam_rows,
                    g_subln[l].reshape(dv, 1), lam_init)
        o_b = o_b.reshape(n, -1)

        ocs, lses = [], []
        for g, (cg, (window, dil)) in enumerate(zip((cg0, cg1, cg2), C_GROUPS)):
            assert window // dil == C_BAND
            o, s = _dilated_group(cg, bias_c[g * C_HPG:(g + 1) * C_HPG], g)
            ocs.append(o)
            lses.append(s)

        ws_out = [cast(seg(wl, 11)), cast(seg(wl, 12)), cast(seg(wl, 13)),
                  cast(w_branch_a[l]), cast(w_branch_b[l]), cast(w_branch_c[l]), cast(w_out[l]),
                  cast(w_gate_up[l]), cast(w_down[l])]
        x2d = _merge_ffn(x2d, mod[l], g1, g_norm2[l].reshape(1, d), g_final.reshape(1, d),
                         o_a, o_b, ocs, lses, ws_out, seq, final_norm=(l == depth - 1))
    return x2d.reshape(bsz, seq, d)
```

```python
import functools
import math

import numpy as np
import jax
import jax.numpy as jnp
from jax import lax
from jax.experimental import pallas as pl
from jax.experimental.pallas import tpu as pltpu

D_MODEL = 1024
HEAD_DIM = 64
ATTN_SCALE = HEAD_DIM ** -0.5
LOG2E = math.log2(math.e)
A_HEADS = 8
A_LATENT = 128
IDX_HEADS = 8
IDX_DIM = 64
IDX_SCALE = (IDX_HEADS * IDX_DIM) ** -0.5
TOPK_MAX = 256
B_HEADS = 4
C_GROUPS = ((128, 1), (512, 4), (2048, 16))
C_HPG = 4
C_HEADS = C_HPG * len(C_GROUPS)
N_BUCKETS = 32
MAX_DISTANCE = 2048
N_BIAS_HEADS = A_HEADS + B_HEADS + C_HEADS
D_FF = -(-8 * D_MODEL // (3 * 256)) * 256
EPS = 1e-6

MM_DTYPE = jnp.bfloat16
F32 = jnp.float32
I32 = jnp.int32

ATT_TILE = 256
N_OFFSETS = 8
C_BAND = 128
ROW_TILE = 512
FFN_CHUNK = 256
MOD_COLS = 1536
NEG = -1e30
INT_MIN = -2 ** 31
LANES = 128
SUBLANES = 8
KEY_BITS = 32
VMEM_LIMIT = 56 * 1024 * 1024


def _cparams(n_axes, vmem=VMEM_LIMIT):
    return pltpu.CompilerParams(dimension_semantics=("arbitrary",) * n_axes,
                                vmem_limit_bytes=vmem)


def _const_spec(shape):
    nd = len(shape)
    return pl.BlockSpec(shape, lambda *_: (0,) * nd, pipeline_mode=pl.Buffered(1))


def _bucket_thresholds(max_dist):
    n = np.arange(max_dist + 1)
    max_exact = N_BUCKETS // 2
    nf = np.maximum(n, 1).astype(np.float32)
    large = max_exact + (np.log(nf / np.float32(max_exact))
                         / np.float32(math.log(MAX_DISTANCE / max_exact))
                         * np.float32(N_BUCKETS - max_exact)).astype(np.int32)
    large = np.minimum(large, N_BUCKETS - 1)
    bucket = np.where(n < max_exact, n, large)
    assert np.all(np.diff(bucket) >= 0)
    thr = []
    for k in range(1, N_BUCKETS):
        idx = np.nonzero(bucket >= k)[0]
        thr.append(int(idx[0]) if idx.size else None)
    return thr


def _bias_from_dist(dist, tab_ref, col, thresholds, lo=0, hi=None):
    reached = [k for k, thr in enumerate(thresholds, start=1) if thr is not None]
    base = max([0] + [k for k in reached if thresholds[k - 1] <= lo])
    b = jnp.full(dist.shape, tab_ref[base * N_BIAS_HEADS + col], F32)
    for k in reached:
        thr = thresholds[k - 1]
        if thr > lo and (hi is None or thr <= hi):
            b = jnp.where(dist >= thr, tab_ref[k * N_BIAS_HEADS + col], b)
    return b


def _bias_att_kernel(tab_ref, out_ref, *, head0, thresholds):
    h = pl.program_id(0)
    n_off, tile, _ = out_ref.shape
    row = lax.broadcasted_iota(I32, (tile, tile), 0)
    colq = lax.broadcasted_iota(I32, (tile, tile), 1)
    for o in range(n_off):
        dist = jnp.maximum(o * tile + colq - row, 0)
        lo, hi = max(o * tile - (tile - 1), 0), o * tile + tile - 1
        out_ref[o] = _bias_from_dist(dist, tab_ref, head0 + h, thresholds, lo, hi) * LOG2E


def _bias_att_tiles(tab, head0, n_heads, tile, thresholds):
    return pl.pallas_call(
        functools.partial(_bias_att_kernel, head0=head0, thresholds=thresholds),
        out_shape=jax.ShapeDtypeStruct((n_heads, N_OFFSETS, tile, tile), F32),
        grid=(n_heads,),
        in_specs=[pl.BlockSpec(memory_space=pltpu.SMEM)],
        out_specs=pl.BlockSpec((None, N_OFFSETS, tile, tile), lambda h: (h, 0, 0, 0)),
        compiler_params=_cparams(1),
        name="bias_att_tiles",
    )(tab)


def _bias_dil_kernel(tab_ref, out_ref, *, head0, thresholds):
    h = pl.program_id(0)
    g = h // C_HPG
    dil = jnp.where(g == 0, C_GROUPS[0][1], jnp.where(g == 1, C_GROUPS[1][1], C_GROUPS[2][1]))
    i = lax.broadcasted_iota(I32, (C_BAND, 2 * C_BAND), 0)
    j = lax.broadcasted_iota(I32, (C_BAND, 2 * C_BAND), 1)
    dist = jnp.maximum((i - j + C_BAND) * dil, 0)
    out_ref[...] = _bias_from_dist(dist, tab_ref, head0 + h, thresholds)


def _bias_dil_tiles(tab, head0, thresholds):
    return pl.pallas_call(
        functools.partial(_bias_dil_kernel, head0=head0, thresholds=thresholds),
        out_shape=jax.ShapeDtypeStruct((C_HEADS, C_BAND, 2 * C_BAND), F32),
        grid=(C_HEADS,),
        in_specs=[pl.BlockSpec(memory_space=pltpu.SMEM)],
        out_specs=pl.BlockSpec((None, C_BAND, 2 * C_BAND), lambda h: (h, 0, 0)),
        compiler_params=_cparams(1),
        name="bias_dil_tiles",
    )(tab)


def _mod_kernel(c_ref, w_ref, b_ref, out_ref):
    c = c_ref[...]
    ca = (c * jax.nn.sigmoid(c)).astype(MM_DTYPE)
    out_ref[...] = jnp.dot(ca, w_ref[...].astype(MM_DTYPE), preferred_element_type=F32) + b_ref[...]


def _modulation(c, w_ada, b_ada):
    depth, d, wid = w_ada.shape
    bsz = c.shape[0]
    tn = MOD_COLS
    assert wid % tn == 0
    return pl.pallas_call(
        _mod_kernel,
        out_shape=jax.ShapeDtypeStruct((depth, bsz, wid), F32),
        grid=(depth, wid // tn),
        in_specs=[pl.BlockSpec((bsz, d), lambda l, j: (0, 0)),
                  pl.BlockSpec((None, d, tn), lambda l, j: (l, 0, j)),
                  pl.BlockSpec((None, 1, tn), lambda l, j: (l, 0, j))],
        out_specs=pl.BlockSpec((None, bsz, tn), lambda l, j: (l, 0, j)),
        compiler_params=_cparams(2),
        name="adaln_modulation",
    )(c, w_ada, b_ada.reshape(depth, 1, wid))


def _rms(x):
    return x * lax.rsqrt(jnp.mean(x * x, axis=-1, keepdims=True) + EPS)


def _in_kernel(x_ref, mod_ref, g1_ref, gkv_ref,
               wT_aq, wT_iq, wT_iw, wT_bq, wT_bv, w_ik, w_kv, w_bk, w_c,
               o_aqT, o_iqT, o_iwT, o_bqT, o_bvT, o_ik, o_kv, o_kvT, o_bk, o_c0, o_c1, o_c2,
               c_scr):
    T = ATT_TILE
    tm = x_ref.shape[0]
    h = _rms(x_ref[...]) * g1_ref[...]
    h = h * (1.0 + mod_ref[1:2, :]) + mod_ref[0:1, :]
    hb = h.astype(MM_DTYPE)

    def mm(w):
        return jnp.dot(hb, w[...], preferred_element_type=F32)

    def mm_t(wT):
        return lax.dot_general(wT[...], hb, (((1,), (1,)), ((), ())), preferred_element_type=F32)

    o_aqT[...] = mm_t(wT_aq).astype(o_aqT.dtype)
    o_iqT[...] = mm_t(wT_iq).astype(o_iqT.dtype)
    o_bqT[...] = (mm_t(wT_bq) * (ATTN_SCALE * LOG2E)).astype(o_bqT.dtype)
    o_iwT[...] = (mm_t(wT_iw) * IDX_SCALE)[:IDX_HEADS]
    bvT = mm_t(wT_bv).astype(o_bvT.dtype)
    kv = _rms(mm(w_kv)) * gkv_ref[...]
    kvT = kv.T.astype(o_kvT.dtype)
    for j in range(tm // T):
        o_bvT[j] = bvT[:, j * T:(j + 1) * T]
        o_kvT[j] = kvT[:, j * T:(j + 1) * T]
    o_kv[...] = kv.astype(o_kv.dtype)
    o_ik[...] = mm(w_ik).astype(o_ik.dtype)
    o_bk[...] = mm(w_bk).astype(o_bk.dtype)

    yc = mm(w_c)
    n_chunk = yc.shape[1] // LANES
    for j in range(n_chunk):
        c_scr[j] = yc[:, j * LANES:(j + 1) * LANES]
    per_group = n_chunk // len(C_GROUPS)
    for g, o_c in enumerate((o_c0, o_c1, o_c2)):
        dil = C_GROUPS[g][1]
        for r in range(dil):
            for jj in range(per_group):
                o_c[r, :, jj * LANES:(jj + 1) * LANES] = c_scr[
                    g * per_group + jj, pl.ds(r, tm // dil, stride=dil), :].astype(o_c.dtype)


def _in_proj(x2d, mod_l, g1, gkv, ws, bsz, seq):
    n, d = x2d.shape
    tm = ROW_TILE
    T = ATT_TILE
    per_b = seq // tm
    nk = seq // T
    hd = A_HEADS * HEAD_DIM
    bw = B_HEADS * 2 * HEAD_DIM
    gw = 3 * C_HPG * HEAD_DIM
    in_specs = [pl.BlockSpec((tm, d), lambda i: (i, 0)),
                pl.BlockSpec((None, 6, d), lambda i: (i // per_b, 0, 0)),
                _const_spec((1, d)), _const_spec((1, A_LATENT))]
    in_specs += [_const_spec(w.shape) for w in ws]

    def tspec(rows):
        return pl.BlockSpec((None, rows, tm), lambda i: (i // per_b, 0, i % per_b))

    def tile_tspec(rows):
        return pl.BlockSpec((None, tm // T, rows, T), lambda i: (i // per_b, i % per_b, 0, 0))

    def rspec(wd):
        return pl.BlockSpec((tm, wd), lambda i: (i, 0))

    def cspec(dil):
        return pl.BlockSpec((None, dil, tm // dil, gw), lambda i: (i // per_b, 0, i % per_b, 0))

    sds = jax.ShapeDtypeStruct
    out_specs = [tspec(hd), tspec(IDX_HEADS * IDX_DIM), tspec(IDX_HEADS), tspec(bw),
                 tile_tspec(bw), rspec(IDX_DIM), rspec(A_LATENT), tile_tspec(A_LATENT), rspec(bw)]
    out_shape = [sds((bsz, hd, seq), MM_DTYPE), sds((bsz, IDX_HEADS * IDX_DIM, seq), MM_DTYPE),
                 sds((bsz, IDX_HEADS, seq), F32), sds((bsz, bw, seq), MM_DTYPE),
                 sds((bsz, nk, bw, T), MM_DTYPE), sds((n, IDX_DIM), MM_DTYPE),
                 sds((n, A_LATENT), MM_DTYPE), sds((bsz, nk, A_LATENT, T), MM_DTYPE),
                 sds((n, bw), MM_DTYPE)]
    for _, dil in C_GROUPS:
        out_specs.append(cspec(dil))
        out_shape.append(sds((bsz, dil, seq // dil, gw), MM_DTYPE))
    return pl.pallas_call(
        _in_kernel, out_shape=out_shape, grid=(n // tm,),
        in_specs=in_specs, out_specs=out_specs,
        scratch_shapes=[pltpu.VMEM((3 * gw // LANES, tm, LANES), F32)],
        compiler_params=_cparams(1), name="in_proj",
    )(x2d, mod_l, g1, gkv, *ws)


def _softmax_step(s, m_old):
    m_new = jnp.maximum(m_old, jnp.max(s, axis=0, keepdims=True))
    p = jnp.exp2(s - m_new)
    alpha = jnp.exp2(m_old - m_new)
    return p.astype(MM_DTYPE), alpha, m_new


def _initial_max(n_chains, tile):
    return tuple(jnp.full((1, tile), NEG, F32) for _ in range(n_chains))


SUM_ROWS = 16


def _with_ones_row(vT):
    row = lax.broadcasted_iota(I32, (SUM_ROWS, vT.shape[1]), 0)
    ones = jnp.where(row == 0, 1.0, 0.0).astype(vT.dtype)
    return jnp.concatenate([vT, ones], axis=0)


def _bit_transpose32(words):
    a = list(words)
    j, mask = 16, 0x0000FFFF
    while j:
        k = 0
        while k < 32:
            t = (a[k] ^ lax.shift_right_logical(a[k + j], jnp.int32(j))) & jnp.int32(mask)
            a[k] = a[k] ^ t
            a[k + j] = a[k + j] ^ lax.shift_left(t, jnp.int32(j))
            k = (k + j + 1) & ~j
        j >>= 1
        mask = (mask ^ (mask << j)) & 0xFFFFFFFF
    return a


def _dsa_kernel(aqT_ref, iqT_ref, iwT_ref, kidx_ref, kv_ref, kvT_ref, bias_ref, wukT_ref, wuvT_ref,
                out_ref, keys_ref, planes_ref, qlat_ref, acc_ref, s_ref, oT_ref, *, topk, idx_bits):
    T = ATT_TILE
    qi = pl.program_id(1)
    nk = qi + 1

    for h in range(A_HEADS):
        q = jnp.dot(wukT_ref[h], aqT_ref[h * HEAD_DIM:(h + 1) * HEAD_DIM, :],
                    preferred_element_type=F32) * (ATTN_SCALE * LOG2E)
        qlat_ref[h] = q.astype(qlat_ref.dtype)

    s_loc = lax.broadcasted_iota(I32, (T, T), 0)
    t_loc = lax.broadcasted_iota(I32, (T, T), 1)

    def score_tile(kj, diagonal):
        kt = kidx_ref[kj]
        acc = jnp.zeros((T, T), F32)
        for h in range(IDX_HEADS):
            s = jnp.dot(kt, iqT_ref[h * IDX_DIM:(h + 1) * IDX_DIM, :], preferred_element_type=F32)
            acc = acc + jnp.maximum(s, 0.0) * iwT_ref[h:h + 1, :]
        bits = lax.bitcast_convert_type(acc, I32)
        key = jnp.where(bits < 0, bits ^ jnp.int32(0x7FFFFFFF), bits)
        if diagonal:
            key = jnp.where(s_loc <= t_loc, key, jnp.int32(INT_MIN))
        keys_ref[kj] = key
        planes = _bit_transpose32([key[SUBLANES * i:SUBLANES * (i + 1), :] ^ jnp.int32(INT_MIN)
                                   for i in range(KEY_BITS)])
        for b in range(KEY_BITS):
            planes_ref[kj, b] = planes[b]

    def off_diagonal(kj, carry):
        score_tile(kj, False)
        return carry

    lax.fori_loop(0, qi, off_diagonal, 0)
    score_tile(qi, True)

    n_tiles = planes_ref.shape[0]

    @pl.when(qi == 0)
    def _():
        def clear_planes(kj, carry):
            for b in range(KEY_BITS):
                planes_ref[kj, b] = jnp.zeros((SUBLANES, T), I32)
            return carry

        lax.fori_loop(1, n_tiles, clear_planes, 0)

    one = jnp.int32(1)
    nil = jnp.int32(0)
    zero = jnp.zeros((1, T), I32)

    n_rows = n_tiles * SUBLANES
    tile_of_row = lax.broadcasted_iota(I32, (n_rows, T), 0) // SUBLANES
    tied0 = jnp.where(tile_of_row < nk, jnp.int32(-1), nil)

    def bit_step(b, carry):
        tied, n_gt, kth_u = carry
        ones = tied & planes_ref[:, b].reshape(n_rows, T)
        n1 = jnp.sum(lax.population_count(ones), axis=0, keepdims=True)
        take = (n_gt + n1) >= topk
        tied = jnp.where(take, ones, tied ^ ones)
        n_gt = jnp.where(take, n_gt, n_gt + n1)
        kth_u = jnp.where(take, kth_u | lax.shift_left(one, jnp.int32(KEY_BITS - 1) - b), kth_u)
        return tied, n_gt, kth_u

    tied, n_gt, kth_u = lax.fori_loop(0, KEY_BITS, bit_step, (tied0, zero, zero))
    n_eq = jnp.sum(lax.population_count(tied), axis=0, keepdims=True)
    n_ge = jnp.where(kth_u == nil, n_gt, n_gt + n_eq)
    kth = jnp.maximum(kth_u ^ jnp.int32(INT_MIN), jnp.int32(INT_MIN + 1))

    @pl.when(jnp.max(n_ge) > topk)
    def _():
        need = topk - n_gt
        r_bits = SUBLANES.bit_length() - 1
        t_bits = T.bit_length() - 1
        word_row = lax.broadcasted_iota(I32, (n_rows, T), 0)
        word_tile = lax.shift_right_logical(word_row, jnp.int32(r_bits))
        word_r = word_row & jnp.int32(SUBLANES - 1)
        tied_count = lax.population_count(tied)

        def tied_before(pos):
            p_tile = lax.shift_right_logical(pos, jnp.int32(t_bits))
            p_i = lax.shift_right_logical(pos, jnp.int32(r_bits)) & jnp.int32(KEY_BITS - 1)
            p_r = pos & jnp.int32(SUBLANES - 1)
            above = jnp.where(p_i == nil, nil,
                              lax.shift_left(jnp.full_like(p_i, -1), jnp.int32(KEY_BITS) - p_i))
            at_i = lax.shift_right_logical(
                tied, jnp.broadcast_to(jnp.int32(KEY_BITS - 1) - p_i, tied.shape)) & one
            inside = lax.population_count(tied & above) + jnp.where(word_r < p_r, at_i, nil)
            hit = jnp.where(word_tile < p_tile, tied_count,
                            jnp.where(word_tile == p_tile, inside, nil))
            return jnp.sum(hit, axis=0, keepdims=True)

        def pos_step(i, cut):
            cand = cut | jnp.left_shift(one, jnp.int32(idx_bits - 1) - i)
            return jnp.where(tied_before(cand) < need, cand, cut)

        cut = lax.fori_loop(0, idx_bits, pos_step, zero)

        def demote(kj, carry):
            kk = keys_ref[kj]
            lowered = jnp.where((kj * T + s_loc) > cut, kth - one, kk)
            keys_ref[kj] = jnp.where(kk == kth, lowered, kk)
            return carry

        lax.fori_loop(0, nk, demote, 0)

    acc_ref[...] = jnp.zeros(acc_ref.shape, F32)

    def logits(kj, h):
        off = jnp.minimum(qi - kj, N_OFFSETS - 1)
        return bias_ref[h, off] + jnp.dot(kv_ref[kj], qlat_ref[h], preferred_element_type=F32)

    for h in range(A_HEADS):
        s_ref[h] = logits(0, h)

    def attend(kj, ms):
        unselected = jnp.where(keys_ref[kj] >= kth, 0.0, NEG)
        kvTt = _with_ones_row(kvT_ref[kj])
        kj_next = jnp.minimum(kj + 1, nk - 1)
        new_m = []
        for h in range(A_HEADS):
            s = s_ref[h] + unselected
            p, alpha, m_new = _softmax_step(s, ms[h])
            s_ref[h] = logits(kj_next, h)
            acc_ref[h] = alpha * acc_ref[h] + jnp.dot(kvTt, p, preferred_element_type=F32)
            new_m.append(m_new)
        return tuple(new_m)

    lax.fori_loop(0, nk, attend, _initial_max(A_HEADS, T))

    for h in range(A_HEADS):
        o_lat = (acc_ref[h, 0:A_LATENT, :] / acc_ref[h, A_LATENT:A_LATENT + 1, :]).astype(MM_DTYPE)
        oT_ref[h * HEAD_DIM:(h + 1) * HEAD_DIM, :] = jnp.dot(
            wuvT_ref[h], o_lat, preferred_element_type=F32)
    out_ref[...] = oT_ref[...].T.astype(out_ref.dtype)


def _dsa(aqT, iqT, iwT, kidx, kv, kvT, bias_a, wukT, wuvT):
    bsz, _, seq = aqT.shape
    T = ATT_TILE
    nk = seq // T
    topk = min(TOPK_MAX, seq // 4)
    idx_bits = int(math.log2(seq))
    assert 2 ** idx_bits == seq
    assert T == KEY_BITS * SUBLANES
    qspec = lambda rows: pl.BlockSpec((None, rows, T), lambda b, i: (b, 0, i))
    kspec = lambda a, c: pl.BlockSpec((None, nk, a, c), lambda b, i: (b, 0, 0, 0))
    return pl.pallas_call(
        functools.partial(_dsa_kernel, topk=topk, idx_bits=idx_bits),
        out_shape=jax.ShapeDtypeStruct((bsz, seq, A_HEADS * HEAD_DIM), MM_DTYPE),
        grid=(bsz, nk),
        in_specs=[qspec(A_HEADS * HEAD_DIM), qspec(IDX_HEADS * IDX_DIM), qspec(IDX_HEADS),
                  kspec(T, IDX_DIM), kspec(T, A_LATENT), kspec(A_LATENT, T),
                  _const_spec(bias_a.shape), _const_spec(wukT.shape), _const_spec(wuvT.shape)],
        out_specs=pl.BlockSpec((None, T, A_HEADS * HEAD_DIM), lambda b, i: (b, i, 0)),
        scratch_shapes=[pltpu.VMEM((nk, T, T), I32),
                        pltpu.VMEM((nk, KEY_BITS + 1, SUBLANES, T), I32),
                        pltpu.VMEM((A_HEADS, A_LATENT, T), MM_DTYPE),
                        pltpu.VMEM((A_HEADS, A_LATENT + SUM_ROWS, T), F32),
                        pltpu.VMEM((A_HEADS, T, T), F32),
                        pltpu.VMEM((A_HEADS * HEAD_DIM, T), F32)],
        compiler_params=_cparams(2), name="dsa_attention",
    )(aqT, iqT, iwT, kidx, kv, kvT, bias_a, wukT, wuvT)


def _diff_kernel(qT_ref, k_ref, vT_ref, bias_ref, lam_ref, gsub_ref, out_ref,
                 qz_ref, acc_ref, s_ref, oT_ref, *, lam_init):
    T = ATT_TILE
    dv = 2 * HEAD_DIM
    n_chain = 2 * B_HEADS
    qi = pl.program_id(1)
    half = lax.broadcasted_iota(I32, (dv, T), 0) < HEAD_DIM
    for h in range(B_HEADS):
        q = qT_ref[h * dv:(h + 1) * dv, :].astype(F32)
        qz_ref[2 * h] = jnp.where(half, q, 0.0).astype(MM_DTYPE)
        qz_ref[2 * h + 1] = jnp.where(half, 0.0, q).astype(MM_DTYPE)
    s_loc = lax.broadcasted_iota(I32, (T, T), 0)
    t_loc = lax.broadcasted_iota(I32, (T, T), 1)
    acc_ref[...] = jnp.zeros(acc_ref.shape, F32)

    def logits(kj, c):
        h = c // 2
        off = jnp.minimum(qi - kj, N_OFFSETS - 1)
        return bias_ref[h, off] + jnp.dot(k_ref[kj, :, h * dv:(h + 1) * dv], qz_ref[c],
                                          preferred_element_type=F32)

    for c in range(n_chain):
        s_ref[c] = logits(0, c)

    def step(kj, ms, diagonal):
        new_m = []
        for c in range(n_chain):
            h = c // 2
            s = s_ref[c]
            if diagonal:
                s = jnp.where(s_loc <= t_loc, s, NEG)
            p, alpha, m_new = _softmax_step(s, ms[c])
            if not diagonal:
                s_ref[c] = logits(kj + 1, c)
            vT = _with_ones_row(vT_ref[kj, h * dv:(h + 1) * dv, :])
            acc_ref[c] = alpha * acc_ref[c] + jnp.dot(vT, p, preferred_element_type=F32)
            new_m.append(m_new)
        return tuple(new_m)

    ms = lax.fori_loop(0, qi, lambda kj, cr: step(kj, cr, False), _initial_max(n_chain, T))
    step(qi, ms, True)

    lr = lam_ref[...]
    lam = (jnp.exp(jnp.sum(lr[0:1, :] * lr[1:2, :], axis=1, keepdims=True))
           - jnp.exp(jnp.sum(lr[2:3, :] * lr[3:4, :], axis=1, keepdims=True)) + lam_init)

    def normalised(c):
        return acc_ref[c, 0:dv, :] / acc_ref[c, dv:dv + 1, :]

    for h in range(B_HEADS):
        attn = normalised(2 * h) - lam * normalised(2 * h + 1)
        y = attn * lax.rsqrt(jnp.mean(attn * attn, axis=0, keepdims=True) + EPS)
        oT_ref[h * dv:(h + 1) * dv, :] = y * gsub_ref[...] * (1.0 - lam_init)
    out_ref[...] = oT_ref[...].T.astype(out_ref.dtype)


def _diff(bqT, bk, bvT, bias_b, lam_rows, gsub, lam_init):
    bsz, _, seq = bqT.shape
    T = ATT_TILE
    nk = seq // T
    dv = 2 * HEAD_DIM
    qspec = pl.BlockSpec((None, B_HEADS * dv, T), lambda b, i: (b, 0, i))
    return pl.pallas_call(
        functools.partial(_diff_kernel, lam_init=lam_init),
        out_shape=jax.ShapeDtypeStruct((bsz, seq, B_HEADS * dv), MM_DTYPE),
        grid=(bsz, nk),
        in_specs=[qspec,
                  pl.BlockSpec((None, nk, T, B_HEADS * dv), lambda b, i: (b, 0, 0, 0)),
                  pl.BlockSpec((None, nk, B_HEADS * dv, T), lambda b, i: (b, 0, 0, 0)),
                  _const_spec(bias_b.shape), _const_spec((4, HEAD_DIM)), _const_spec((dv, 1))],
        out_specs=pl.BlockSpec((None, T, B_HEADS * dv), lambda b, i: (b, i, 0)),
        scratch_shapes=[pltpu.VMEM((2 * B_HEADS, dv, T), MM_DTYPE),
                        pltpu.VMEM((2 * B_HEADS, dv + SUM_ROWS, T), F32),
                        pltpu.VMEM((2 * B_HEADS, T, T), F32),
                        pltpu.VMEM((B_HEADS * dv, T), F32)],
        compiler_params=_cparams(2), name="diff_attention",
    )(bqT, bk, bvT, bias_b, lam_rows, gsub)


def _dil_kernel(cur_ref, halo_ref, bias_ref, out_ref, lse_ref, *, tq):
    n = C_BAND
    wid = C_HPG * HEAD_DIM
    halo_lo = jnp.where(pl.program_id(2) == 0, jnp.int32(n), jnp.int32(0))
    i = lax.broadcasted_iota(I32, (n, 2 * n), 0)
    j = lax.broadcasted_iota(I32, (n, 2 * n), 1)
    lane_head = lax.broadcasted_iota(I32, (n, wid), 1) // HEAD_DIM
    in_head = [lane_head == h for h in range(C_HPG)]
    band = jnp.where(j >= i, jnp.where(j <= i + n, 0.0, NEG), NEG)
    band0 = jnp.where(j >= jnp.maximum(i, halo_lo), jnp.where(j <= i + n, 0.0, NEG), NEG)
    bias = [bias_ref[h] + band for h in range(C_HPG)]
    bias0 = [bias_ref[h] + band0 for h in range(C_HPG)]

    def band_rows(r, c, lo, hi):
        if c == 0:
            return jnp.concatenate([halo_ref[r, :, lo:hi], cur_ref[r, 0:n, lo:hi]], axis=0)
        return cur_ref[r, (c - 1) * n:(c + 1) * n, lo:hi]

    blocks = [(r, c) for r in range(cur_ref.shape[0]) for c in range(tq // n)]
    logits = []
    for r, c in blocks:
        q = cur_ref[r, c * n:(c + 1) * n, 0:wid].astype(F32) * ATTN_SCALE
        keys = band_rows(r, c, wid, 2 * wid)
        for h in range(C_HPG):
            qh = jnp.where(in_head[h], q, 0.0).astype(MM_DTYPE)
            s = lax.dot_general(qh, keys, (((1,), (1,)), ((), ())), preferred_element_type=F32)
            logits.append(s + (bias0 if c == 0 else bias)[h])
    for k, (r, c) in enumerate(blocks):
        vals = band_rows(r, c, 2 * wid, 3 * wid)
        out = jnp.zeros((n, wid), F32)
        lse = jnp.zeros((n, wid), F32)
        for h in range(C_HPG):
            s = logits[k * C_HPG + h]
            m = jnp.max(s, axis=1, keepdims=True)
            p = jnp.exp(s - m)
            den = jnp.sum(p, axis=1, keepdims=True)
            o = jnp.dot(p.astype(MM_DTYPE), vals, preferred_element_type=F32) * (1.0 / den)
            out = jnp.where(in_head[h], o, out)
            lse = jnp.where(in_head[h], m + jnp.log(den), lse)
        out_ref[r, c * n:(c + 1) * n, :] = out
        lse_ref[r, c * n:(c + 1) * n, :] = lse


def _dilated_group(cg, bias_g, g):
    bsz, dil, m, gw = cg.shape
    wid = C_HPG * HEAD_DIM
    n = C_BAND
    assert m % n == 0 and gw == 3 * wid
    tq = min(m, 512)
    n_res = max(1, min(dil, 512 // tq))
    cur = pl.BlockSpec((None, n_res, tq, gw), lambda b, r, i: (b, r, i, 0))
    halo = pl.BlockSpec((None, n_res, n, gw),
                        lambda b, r, i: (b, r, jnp.maximum(i * (tq // n) - 1, 0), 0))
    outspec = pl.BlockSpec((None, n_res, tq, wid), lambda b, r, i: (b, r, i, 0))
    return pl.pallas_call(
        functools.partial(_dil_kernel, tq=tq),
        out_shape=[jax.ShapeDtypeStruct((bsz, dil, m, wid), F32)] * 2,
        grid=(bsz, dil // n_res, m // tq),
        in_specs=[cur, halo, pl.BlockSpec((C_HPG, n, 2 * n), lambda b, r, i: (0, 0, 0))],
        out_specs=[outspec, outspec],
        compiler_params=_cparams(3), name=f"dilated_group{g}",
    )(cg, cg, bias_g)


def _merge_ffn_kernel(x_ref, mod_ref, g1_ref, g2_ref, gf_ref, oa_ref, ob_ref,
                      c0_ref, c1_ref, c2_ref, s0_ref, s1_ref, s2_ref,
                      wza, wzb, wzc, wba, wbb, wbc, wo, wgu, wd, out_ref, tok_ref, *, final_norm):
    x = x_ref[...]
    tm = x.shape[0]
    h = _rms(x) * g1_ref[...]
    h = h * (1.0 + mod_ref[1:2, :]) + mod_ref[0:1, :]
    hb = h.astype(MM_DTYPE)

    def token_order(k, ref):
        dil, _, w = ref.shape
        if dil == 1:
            return ref[0]
        n_chunk = w // LANES
        for r in range(dil):
            for j in range(n_chunk):
                tok_ref[k * n_chunk + j, pl.ds(r, tm // dil, stride=dil), :] = ref[
                    r, :, j * LANES:(j + 1) * LANES]
        return jnp.concatenate([tok_ref[k * n_chunk + j] for j in range(n_chunk)], axis=1)

    s0, s1, s2 = s0_ref[0], token_order(0, s1_ref), token_order(1, s2_ref)
    c0, c1, c2 = c0_ref[0], token_order(2, c1_ref), token_order(3, c2_ref)
    mx = jnp.maximum(jnp.maximum(s0, s1), s2)
    e0, e1, e2 = jnp.exp(s0 - mx), jnp.exp(s1 - mx), jnp.exp(s2 - mx)
    oc = (e0 * c0 + e1 * c1 + e2 * c2) / (e0 + e1 + e2)

    def gated(wz, o, wb):
        z = jnp.dot(hb, wz[...], preferred_element_type=F32)
        return jax.nn.sigmoid(z) * jnp.dot(o, wb[...], preferred_element_type=F32)

    merged = (gated(wza, oa_ref[...], wba) + gated(wzb, ob_ref[...], wbb)
              + gated(wzc, oc.astype(MM_DTYPE), wbc))
    y = jnp.dot(merged.astype(MM_DTYPE), wo[...], preferred_element_type=F32)
    x = x + mod_ref[2:3, :] * y

    h = _rms(x) * g2_ref[...]
    h = h * (1.0 + mod_ref[4:5, :]) + mod_ref[3:4, :]
    hb = h.astype(MM_DTYPE)
    acc = jnp.zeros(x.shape, F32)
    for c in range(D_FF // FFN_CHUNK):
        cols = slice(c * FFN_CHUNK, (c + 1) * FFN_CHUNK)
        fg = jnp.dot(hb, wgu[:, cols], preferred_element_type=F32)
        fu = jnp.dot(hb, wgu[:, D_FF + c * FFN_CHUNK:D_FF + (c + 1) * FFN_CHUNK],
                     preferred_element_type=F32)
        act = (fg * jax.nn.sigmoid(fg) * fu).astype(MM_DTYPE)
        acc = acc + jnp.dot(act, wd[cols, :], preferred_element_type=F32)
    y = x + mod_ref[5:6, :] * acc
    if final_norm:
        y = _rms(y) * gf_ref[...]
    out_ref[...] = y


def _merge_ffn(x2d, mod_l, g1, g2, gf, oa, ob, ocs, lses, ws, seq, final_norm):
    n, d = x2d.shape
    tm = ROW_TILE
    per_b = seq // tm
    row = lambda wd: pl.BlockSpec((tm, wd), lambda i: (i, 0))
    vec = _const_spec((1, d))
    wid = C_HPG * HEAD_DIM
    res = [pl.BlockSpec((None, dil, tm // dil, wid), lambda i: (i // per_b, 0, i % per_b, 0))
           for _, dil in C_GROUPS]
    in_specs = [row(d), pl.BlockSpec((None, 6, d), lambda i: (i // per_b, 0, 0)), vec, vec, vec,
                row(oa.shape[1]), row(ob.shape[1])] + res + res
    in_specs += [_const_spec(w.shape) for w in ws]
    return pl.pallas_call(
        functools.partial(_merge_ffn_kernel, final_norm=final_norm),
        out_shape=jax.ShapeDtypeStruct((n, d), F32), grid=(n // tm,),
        in_specs=in_specs, out_specs=row(d),
        scratch_shapes=[pltpu.VMEM((4 * wid // LANES, tm, LANES), F32)],
        compiler_params=_cparams(1), name="merge_ffn",
    )(x2d, mod_l, g1, g2, gf, oa, ob, *ocs, *lses, *ws)


def kernel(x, c, w_ada, b_ada, g_norm1, w_in, w_uk, w_uv, g_kv, lam_q1, lam_k1, lam_q2, lam_k2,
           g_subln, w_branch_a, w_branch_b, w_branch_c, w_out, g_norm2, w_gate_up, w_down,
           rel_bias, g_final):
    bsz, seq, d = x.shape
    depth = w_ada.shape[0]
    T = ATT_TILE
    nk = seq // T
    assert d == D_MODEL and seq % T == 0 and seq % ROW_TILE == 0
    n = bsz * seq
    cast = lambda w: w.astype(MM_DTYPE)

    thresholds = _bucket_thresholds(seq + 2 * C_BAND * C_GROUPS[-1][1])
    assert seq <= N_OFFSETS * T or (N_OFFSETS - 2) * T + 1 >= thresholds[-1]
    tab = rel_bias.reshape(-1)
    bias_a = _bias_att_tiles(tab, 0, A_HEADS, T, thresholds)
    bias_b = _bias_att_tiles(tab, A_HEADS, B_HEADS, T, thresholds)
    bias_c = _bias_dil_tiles(tab, A_HEADS + B_HEADS, thresholds)

    mod = _modulation(c, w_ada, b_ada).reshape(depth, bsz, 6, d)

    splits = (A_HEADS * HEAD_DIM, A_LATENT, IDX_HEADS * IDX_DIM, IDX_DIM, IDX_HEADS,
              B_HEADS * 2 * HEAD_DIM, B_HEADS * 2 * HEAD_DIM, B_HEADS * 2 * HEAD_DIM,
              C_HEADS * HEAD_DIM, C_HEADS * HEAD_DIM, C_HEADS * HEAD_DIM, d, d, d)
    offs = np.concatenate([[0], np.cumsum(splits)])
    seg = lambda w, k: w[:, int(offs[k]):int(offs[k + 1])]

    x2d = x.reshape(n, d)
    for l in range(depth):
        wl = w_in[l]
        castT = lambda w: w.T.astype(MM_DTYPE)
        wT_iw = jnp.pad(seg(wl, 4).T, ((0, 16 - IDX_HEADS), (0, 0)))
        wid = C_HPG * HEAD_DIM
        w_c = jnp.concatenate([seg(wl, k)[:, g * wid:(g + 1) * wid]
                               for g in range(len(C_GROUPS)) for k in (8, 9, 10)], axis=1)
        ws_in = [castT(seg(wl, 0)), castT(seg(wl, 2)), cast(wT_iw), castT(seg(wl, 5)), castT(seg(wl, 7)),
                 cast(seg(wl, 3)), cast(seg(wl, 1)), cast(seg(wl, 6)), cast(w_c)]
        g1 = g_norm1[l].reshape(1, d)
        (aqT, iqT, iwT, bqT, bvT, ik, kv, kvT, bk, cg0, cg1, cg2) = _in_proj(
            x2d, mod[l], g1, g_kv[l].reshape(1, A_LATENT), ws_in, bsz, seq)

        o_a = _dsa(aqT, iqT, iwT,
                   ik.reshape(bsz, nk, T, IDX_DIM), kv.reshape(bsz, nk, T, A_LATENT), kvT,
                   bias_a, cast(w_uk[l].transpose(0, 2, 1)), cast(w_uv[l].transpose(0, 2, 1)))
        o_a = o_a.reshape(n, -1)

        lam_init = 0.8 - 0.6 * math.exp(-0.3 * l)
        lam_rows = jnp.stack([lam_q1[l], lam_k1[l], lam_q2[l], lam_k2[l]])
        dv = 2 * HEAD_DIM
        o_b = _diff(bqT, bk.reshape(bsz, nk, T, B_HEADS * dv), bvT, bias_b, lam_rows,
                    g_subln[l].reshape(dv, 1), lam_init)
        o_b = o_b.reshape(n, -1)

        ocs, lses = [], []
        for g, (cg, (window, dil)) in enumerate(zip((cg0, cg1, cg2), C_GROUPS)):
            assert window // dil == C_BAND
            o, s = _dilated_group(cg, bias_c[g * C_HPG:(g + 1) * C_HPG], g)
            ocs.append(o)
            lses.append(s)

        ws_out = [cast(seg(wl, 11)), cast(seg(wl, 12)), cast(seg(wl, 13)),
                  cast(w_branch_a[l]), cast(w_branch_b[l]), cast(w_branch_c[l]), cast(w_out[l]),
                  cast(w_gate_up[l]), cast(w_down[l])]
        x2d = _merge_ffn(x2d, mod[l], g1, g_norm2[l].reshape(1, d), g_final.reshape(1, d),
                         o_a, o_b, ocs, lses, ws_out, seq, final_norm=(l == depth - 1))
    return x2d.reshape(bsz, seq, d)
```

```python
import functools
import math

import numpy as np
import jax
import jax.numpy as jnp
from jax import lax
from jax.experimental import pallas as pl
from jax.experimental.pallas import tpu as pltpu

D_MODEL = 1024
HEAD_DIM = 64
ATTN_SCALE = HEAD_DIM ** -0.5
LOG2E = math.log2(math.e)
A_HEADS = 8
A_LATENT = 128
IDX_HEADS = 8
IDX_DIM = 64
IDX_SCALE = (IDX_HEADS * IDX_DIM) ** -0.5
TOPK_MAX = 256
B_HEADS = 4
C_GROUPS = ((128, 1), (512, 4), (2048, 16))
C_HPG = 4
C_HEADS = C_HPG * len(C_GROUPS)
N_BUCKETS = 32
MAX_DISTANCE = 2048
N_BIAS_HEADS = A_HEADS + B_HEADS + C_HEADS
D_FF = -(-8 * D_MODEL // (3 * 256)) * 256
EPS = 1e-6

MM_DTYPE = jnp.bfloat16
F32 = jnp.float32
I32 = jnp.int32

ATT_TILE = 256
N_OFFSETS = 8
C_BAND = 128
ROW_TILE = 512
FFN_CHUNK = 256
MOD_COLS = 1536
NEG = -1e30
INT_MIN = -2 ** 31
LANES = 128
SUBLANES = 8
KEY_BITS = 32
VMEM_LIMIT = 56 * 1024 * 1024


def _cparams(n_axes, vmem=VMEM_LIMIT):
    return pltpu.CompilerParams(dimension_semantics=("arbitrary",) * n_axes,
                                vmem_limit_bytes=vmem)


def _const_spec(shape):
    nd = len(shape)
    return pl.BlockSpec(shape, lambda *_: (0,) * nd, pipeline_mode=pl.Buffered(1))


def _bucket_thresholds(max_dist):
    n = np.arange(max_dist + 1)
    max_exact = N_BUCKETS // 2
    nf = np.maximum(n, 1).astype(np.float32)
    large = max_exact + (np.log(nf / np.float32(max_exact))
                         / np.float32(math.log(MAX_DISTANCE / max_exact))
                         * np.float32(N_BUCKETS - max_exact)).astype(np.int32)
    large = np.minimum(large, N_BUCKETS - 1)
    bucket = np.where(n < max_exact, n, large)
    assert np.all(np.diff(bucket) >= 0)
    thr = []
    for k in range(1, N_BUCKETS):
        idx = np.nonzero(bucket >= k)[0]
        thr.append(int(idx[0]) if idx.size else None)
    return thr


def _bias_from_dist(dist, tab_ref, col, thresholds, lo=0, hi=None):
    reached = [k for k, thr in enumerate(thresholds, start=1) if thr is not None]
    base = max([0] + [k for k in reached if thresholds[k - 1] <= lo])
    b = jnp.full(dist.shape, tab_ref[base * N_BIAS_HEADS + col], F32)
    for k in reached:
        thr = thresholds[k - 1]
        if thr > lo and (hi is None or thr <= hi):
            b = jnp.where(dist >= thr, tab_ref[k * N_BIAS_HEADS + col], b)
    return b


def _bias_att_kernel(tab_ref, out_ref, *, head0, thresholds):
    h = pl.program_id(0)
    n_off, tile, _ = out_ref.shape
    row = lax.broadcasted_iota(I32, (tile, tile), 0)
    colq = lax.broadcasted_iota(I32, (tile, tile), 1)
    for o in range(n_off):
        dist = jnp.maximum(o * tile + colq - row, 0)
        lo, hi = max(o * tile - (tile - 1), 0), o * tile + tile - 1
        b = _bias_from_dist(dist, tab_ref, head0 + h, thresholds, lo, hi) * LOG2E
        if o == 0:
            b = jnp.where(row <= colq, b, NEG)
        out_ref[o] = b


def _bias_att_tiles(tab, head0, n_heads, tile, thresholds):
    return pl.pallas_call(
        functools.partial(_bias_att_kernel, head0=head0, thresholds=thresholds),
        out_shape=jax.ShapeDtypeStruct((n_heads, N_OFFSETS, tile, tile), F32),
        grid=(n_heads,),
        in_specs=[pl.BlockSpec(memory_space=pltpu.SMEM)],
        out_specs=pl.BlockSpec((None, N_OFFSETS, tile, tile), lambda h: (h, 0, 0, 0)),
        compiler_params=_cparams(1),
        name="bias_att_tiles",
    )(tab)


def _bias_dil_kernel(tab_ref, out_ref, *, head0, thresholds):
    h = pl.program_id(0)
    g = h // C_HPG
    dil = jnp.where(g == 0, C_GROUPS[0][1], jnp.where(g == 1, C_GROUPS[1][1], C_GROUPS[2][1]))
    i = lax.broadcasted_iota(I32, (C_BAND, 2 * C_BAND), 0)
    j = lax.broadcasted_iota(I32, (C_BAND, 2 * C_BAND), 1)
    dist = jnp.maximum((i - j + C_BAND) * dil, 0)
    out_ref[...] = _bias_from_dist(dist, tab_ref, head0 + h, thresholds)


def _bias_dil_tiles(tab, head0, thresholds):
    return pl.pallas_call(
        functools.partial(_bias_dil_kernel, head0=head0, thresholds=thresholds),
        out_shape=jax.ShapeDtypeStruct((C_HEADS, C_BAND, 2 * C_BAND), F32),
        grid=(C_HEADS,),
        in_specs=[pl.BlockSpec(memory_space=pltpu.SMEM)],
        out_specs=pl.BlockSpec((None, C_BAND, 2 * C_BAND), lambda h: (h, 0, 0)),
        compiler_params=_cparams(1),
        name="bias_dil_tiles",
    )(tab)


def _mod_kernel(c_ref, w_ref, b_ref, out_ref):
    c = c_ref[...]
    ca = (c * jax.nn.sigmoid(c)).astype(MM_DTYPE)
    out_ref[...] = jnp.dot(ca, w_ref[...].astype(MM_DTYPE), preferred_element_type=F32) + b_ref[...]


def _modulation(c, w_ada, b_ada):
    depth, d, wid = w_ada.shape
    bsz = c.shape[0]
    tn = MOD_COLS
    assert wid % tn == 0
    return pl.pallas_call(
        _mod_kernel,
        out_shape=jax.ShapeDtypeStruct((depth, bsz, wid), F32),
        grid=(depth, wid // tn),
        in_specs=[pl.BlockSpec((bsz, d), lambda l, j: (0, 0)),
                  pl.BlockSpec((None, d, tn), lambda l, j: (l, 0, j)),
                  pl.BlockSpec((None, 1, tn), lambda l, j: (l, 0, j))],
        out_specs=pl.BlockSpec((None, bsz, tn), lambda l, j: (l, 0, j)),
        compiler_params=_cparams(2),
        name="adaln_modulation",
    )(c, w_ada, b_ada.reshape(depth, 1, wid))


def _rms(x):
    return x * lax.rsqrt(jnp.mean(x * x, axis=-1, keepdims=True) + EPS)


def _in_kernel(x_ref, mod_ref, g1_ref, gkv_ref,
               wT_aq, wT_iq, wT_iw, wT_bq, wT_bv, w_ik, w_kv, w_bk, w_c,
               o_aqT, o_iqT, o_iwT, o_bqT, o_bvT, o_ik, o_kv, o_kvT, o_bk, o_c0, o_c1, o_c2,
               c_scr):
    T = ATT_TILE
    tm = x_ref.shape[0]
    h = _rms(x_ref[...]) * g1_ref[...]
    h = h * (1.0 + mod_ref[1:2, :]) + mod_ref[0:1, :]
    hb = h.astype(MM_DTYPE)

    def mm(w):
        return jnp.dot(hb, w[...], preferred_element_type=F32)

    def mm_t(wT):
        return lax.dot_general(wT[...], hb, (((1,), (1,)), ((), ())), preferred_element_type=F32)

    o_aqT[...] = mm_t(wT_aq).astype(o_aqT.dtype)
    o_iqT[...] = mm_t(wT_iq).astype(o_iqT.dtype)
    o_bqT[...] = (mm_t(wT_bq) * (ATTN_SCALE * LOG2E)).astype(o_bqT.dtype)
    o_iwT[...] = (mm_t(wT_iw) * IDX_SCALE)[:IDX_HEADS]
    bvT = mm_t(wT_bv).astype(o_bvT.dtype)
    kv = _rms(mm(w_kv)) * gkv_ref[...]
    kvT = kv.T.astype(o_kvT.dtype)
    for j in range(tm // T):
        o_bvT[j] = bvT[:, j * T:(j + 1) * T]
        o_kvT[j] = kvT[:, j * T:(j + 1) * T]
    o_kv[...] = kv.astype(o_kv.dtype)
    o_ik[...] = mm(w_ik).astype(o_ik.dtype)
    o_bk[...] = mm(w_bk).astype(o_bk.dtype)

    yc = mm(w_c)
    n_chunk = yc.shape[1] // LANES
    for j in range(n_chunk):
        c_scr[j] = yc[:, j * LANES:(j + 1) * LANES]
    per_group = n_chunk // len(C_GROUPS)
    for g, o_c in enumerate((o_c0, o_c1, o_c2)):
        dil = C_GROUPS[g][1]
        for r in range(dil):
            for jj in range(per_group):
                o_c[r, :, jj * LANES:(jj + 1) * LANES] = c_scr[
                    g * per_group + jj, pl.ds(r, tm // dil, stride=dil), :].astype(o_c.dtype)


def _in_proj(x2d, mod_l, g1, gkv, ws, bsz, seq):
    n, d = x2d.shape
    tm = ROW_TILE
    T = ATT_TILE
    per_b = seq // tm
    nk = seq // T
    hd = A_HEADS * HEAD_DIM
    bw = B_HEADS * 2 * HEAD_DIM
    gw = 3 * C_HPG * HEAD_DIM
    in_specs = [pl.BlockSpec((tm, d), lambda i: (i, 0)),
                pl.BlockSpec((None, 6, d), lambda i: (i // per_b, 0, 0)),
                _const_spec((1, d)), _const_spec((1, A_LATENT))]
    in_specs += [_const_spec(w.shape) for w in ws]

    def tspec(rows):
        return pl.BlockSpec((None, rows, tm), lambda i: (i // per_b, 0, i % per_b))

    def tile_tspec(rows):
        return pl.BlockSpec((None, tm // T, rows, T), lambda i: (i // per_b, i % per_b, 0, 0))

    def rspec(wd):
        return pl.BlockSpec((tm, wd), lambda i: (i, 0))

    def cspec(dil):
        return pl.BlockSpec((None, dil, tm // dil, gw), lambda i: (i // per_b, 0, i % per_b, 0))

    sds = jax.ShapeDtypeStruct
    out_specs = [tspec(hd), tspec(IDX_HEADS * IDX_DIM), tspec(IDX_HEADS), tspec(bw),
                 tile_tspec(bw), rspec(IDX_DIM), rspec(A_LATENT), tile_tspec(A_LATENT), rspec(bw)]
    out_shape = [sds((bsz, hd, seq), MM_DTYPE), sds((bsz, IDX_HEADS * IDX_DIM, seq), MM_DTYPE),
                 sds((bsz, IDX_HEADS, seq), F32), sds((bsz, bw, seq), MM_DTYPE),
                 sds((bsz, nk, bw, T), MM_DTYPE), sds((n, IDX_DIM), MM_DTYPE),
                 sds((n, A_LATENT), MM_DTYPE), sds((bsz, nk, A_LATENT, T), MM_DTYPE),
                 sds((n, bw), MM_DTYPE)]
    for _, dil in C_GROUPS:
        out_specs.append(cspec(dil))
        out_shape.append(sds((bsz, dil, seq // dil, gw), MM_DTYPE))
    return pl.pallas_call(
        _in_kernel, out_shape=out_shape, grid=(n // tm,),
        in_specs=in_specs, out_specs=out_specs,
        scratch_shapes=[pltpu.VMEM((3 * gw // LANES, tm, LANES), F32)],
        compiler_params=_cparams(1), name="in_proj",
    )(x2d, mod_l, g1, gkv, *ws)


def _initial_max(n_chains, tile):
    return tuple(jnp.full((1, tile), NEG, F32) for _ in range(n_chains))


def _loop_by_pairs(n, body, init):
    pairs = lax.shift_right_logical(n, jnp.int32(1))
    carry = lax.fori_loop(0, pairs, lambda i, c: body(2 * i + 1, body(2 * i, c)), init)
    return lax.fori_loop(2 * pairs, n, body, carry)


SUM_ROWS = 16


def _with_ones_row(vT):
    row = lax.broadcasted_iota(I32, (SUM_ROWS, vT.shape[1]), 0)
    ones = jnp.where(row == 0, 1.0, 0.0).astype(vT.dtype)
    return jnp.concatenate([vT, ones], axis=0)


def _bit_transpose32(words):
    a = list(words)
    j, mask = 16, 0x0000FFFF
    while j:
        k = 0
        while k < 32:
            t = (a[k] ^ lax.shift_right_logical(a[k + j], jnp.int32(j))) & jnp.int32(mask)
            a[k] = a[k] ^ t
            a[k + j] = a[k + j] ^ lax.shift_left(t, jnp.int32(j))
            k = (k + j + 1) & ~j
        j >>= 1
        mask = (mask ^ (mask << j)) & 0xFFFFFFFF
    return a


def _dsa_kernel(aqT_ref, iqT_ref, iwT_ref, kidx_ref, kv_ref, kvT_ref, bias_ref, wukT_ref, wuvT_ref,
                out_ref, keys_ref, planes_ref, qlat_ref, acc_ref, s_ref, oT_ref, *, topk, idx_bits):
    T = ATT_TILE
    qi = pl.program_id(1)
    nk = qi + 1

    for h in range(A_HEADS):
        q = jnp.dot(wukT_ref[h], aqT_ref[h * HEAD_DIM:(h + 1) * HEAD_DIM, :],
                    preferred_element_type=F32) * (ATTN_SCALE * LOG2E)
        qlat_ref[h] = q.astype(qlat_ref.dtype)

    s_loc = lax.broadcasted_iota(I32, (T, T), 0)
    t_loc = lax.broadcasted_iota(I32, (T, T), 1)

    def score_tile(kj, diagonal):
        kt = kidx_ref[kj]
        acc = jnp.zeros((T, T), F32)
        for h in range(IDX_HEADS):
            s = jnp.dot(kt, iqT_ref[h * IDX_DIM:(h + 1) * IDX_DIM, :], preferred_element_type=F32)
            acc = acc + jnp.maximum(s, 0.0) * iwT_ref[h:h + 1, :]
        bits = lax.bitcast_convert_type(acc, I32)
        key = jnp.where(bits < 0, bits ^ jnp.int32(0x7FFFFFFF), bits)
        if diagonal:
            key = jnp.where(s_loc <= t_loc, key, jnp.int32(INT_MIN))
        keys_ref[kj] = key
        planes = _bit_transpose32([key[SUBLANES * i:SUBLANES * (i + 1), :] ^ jnp.int32(INT_MIN)
                                   for i in range(KEY_BITS)])
        for b in range(KEY_BITS):
            planes_ref[kj, b] = planes[b]

    def off_diagonal(kj, carry):
        score_tile(kj, False)
        return carry

    lax.fori_loop(0, qi, off_diagonal, 0)
    score_tile(qi, True)

    n_tiles = planes_ref.shape[0]

    @pl.when(qi == 0)
    def _():
        def clear_planes(kj, carry):
            for b in range(KEY_BITS):
                planes_ref[kj, b] = jnp.zeros((SUBLANES, T), I32)
            return carry

        lax.fori_loop(1, n_tiles, clear_planes, 0)

    one = jnp.int32(1)
    nil = jnp.int32(0)
    zero = jnp.zeros((1, T), I32)

    n_rows = n_tiles * SUBLANES
    tile_of_row = lax.broadcasted_iota(I32, (n_rows, T), 0) // SUBLANES
    tied0 = jnp.where(tile_of_row < nk, jnp.int32(-1), nil)

    def bit_step(b, carry):
        tied, n_gt, kth_u = carry
        ones = tied & planes_ref[:, b].reshape(n_rows, T)
        n1 = jnp.sum(lax.population_count(ones), axis=0, keepdims=True)
        take = (n_gt + n1) >= topk
        tied = jnp.where(take, ones, tied ^ ones)
        n_gt = jnp.where(take, n_gt, n_gt + n1)
        kth_u = jnp.where(take, kth_u | lax.shift_left(one, jnp.int32(KEY_BITS - 1) - b), kth_u)
        return tied, n_gt, kth_u

    tied, n_gt, kth_u = lax.fori_loop(0, KEY_BITS, bit_step, (tied0, zero, zero))
    n_eq = jnp.sum(lax.population_count(tied), axis=0, keepdims=True)
    n_ge = jnp.where(kth_u == nil, n_gt, n_gt + n_eq)
    kth = jnp.maximum(kth_u ^ jnp.int32(INT_MIN), jnp.int32(INT_MIN + 1))

    @pl.when(jnp.max(n_ge) > topk)
    def _():
        need = topk - n_gt
        r_bits = SUBLANES.bit_length() - 1
        t_bits = T.bit_length() - 1
        word_row = lax.broadcasted_iota(I32, (n_rows, T), 0)
        word_tile = lax.shift_right_logical(word_row, jnp.int32(r_bits))
        word_r = word_row & jnp.int32(SUBLANES - 1)
        tied_count = lax.population_count(tied)

        def tied_before(pos):
            p_tile = lax.shift_right_logical(pos, jnp.int32(t_bits))
            p_i = lax.shift_right_logical(pos, jnp.int32(r_bits)) & jnp.int32(KEY_BITS - 1)
            p_r = pos & jnp.int32(SUBLANES - 1)
            above = jnp.where(p_i == nil, nil,
                              lax.shift_left(jnp.full_like(p_i, -1), jnp.int32(KEY_BITS) - p_i))
            at_i = lax.shift_right_logical(
                tied, jnp.broadcast_to(jnp.int32(KEY_BITS - 1) - p_i, tied.shape)) & one
            inside = lax.population_count(tied & above) + jnp.where(word_r < p_r, at_i, nil)
            hit = jnp.where(word_tile < p_tile, tied_count,
                            jnp.where(word_tile == p_tile, inside, nil))
            return jnp.sum(hit, axis=0, keepdims=True)

        def pos_step(i, cut):
            cand = cut | jnp.left_shift(one, jnp.int32(idx_bits - 1) - i)
            return jnp.where(tied_before(cand) < need, cand, cut)

        cut = lax.fori_loop(0, idx_bits, pos_step, zero)

        def demote(kj, carry):
            kk = keys_ref[kj]
            lowered = jnp.where((kj * T + s_loc) > cut, kth - one, kk)
            keys_ref[kj] = jnp.where(kk == kth, lowered, kk)
            return carry

        lax.fori_loop(0, nk, demote, 0)

    acc_ref[...] = jnp.zeros(acc_ref.shape, F32)

    def stage_logits(kj):
        unselected = jnp.where(keys_ref[kj] >= kth, 0.0, NEG)
        off = jnp.minimum(qi - kj, N_OFFSETS - 1)
        kvt = kv_ref[kj]

        def one_head(h):
            s = bias_ref[h, off] + jnp.dot(kvt, qlat_ref[h], preferred_element_type=F32)
            s = s + unselected
            s_ref[h] = s
            return jnp.max(s, axis=0, keepdims=True)

        return one_head

    first = stage_logits(0)
    tile_max0 = tuple(first(h) for h in range(A_HEADS))

    def attend(kj, carry):
        ms, tile_max = carry
        kvTt = _with_ones_row(kvT_ref[kj])
        stage_next = stage_logits(jnp.minimum(kj + 1, nk - 1))
        new_m, new_max = [], []
        for h in range(A_HEADS):
            m_new = jnp.maximum(ms[h], tile_max[h])
            p = jnp.exp2(s_ref[h] - m_new).astype(MM_DTYPE)
            alpha = jnp.exp2(ms[h] - m_new)
            new_max.append(stage_next(h))
            acc_ref[h] = alpha * acc_ref[h] + jnp.dot(kvTt, p, preferred_element_type=F32)
            new_m.append(m_new)
        return tuple(new_m), tuple(new_max)

    lax.fori_loop(0, nk, attend, (_initial_max(A_HEADS, T), tile_max0))

    for h in range(A_HEADS):
        o_lat = (acc_ref[h, 0:A_LATENT, :] / acc_ref[h, A_LATENT:A_LATENT + 1, :]).astype(MM_DTYPE)
        oT_ref[h * HEAD_DIM:(h + 1) * HEAD_DIM, :] = jnp.dot(
            wuvT_ref[h], o_lat, preferred_element_type=F32)
    out_ref[...] = oT_ref[...].T.astype(out_ref.dtype)


def _dsa(aqT, iqT, iwT, kidx, kv, kvT, bias_a, wukT, wuvT):
    bsz, _, seq = aqT.shape
    T = ATT_TILE
    nk = seq // T
    topk = min(TOPK_MAX, seq // 4)
    idx_bits = int(math.log2(seq))
    assert 2 ** idx_bits == seq
    assert T == KEY_BITS * SUBLANES
    qspec = lambda rows: pl.BlockSpec((None, rows, T), lambda b, i: (b, 0, i))
    kspec = lambda a, c: pl.BlockSpec((None, nk, a, c), lambda b, i: (b, 0, 0, 0))
    return pl.pallas_call(
        functools.partial(_dsa_kernel, topk=topk, idx_bits=idx_bits),
        out_shape=jax.ShapeDtypeStruct((bsz, seq, A_HEADS * HEAD_DIM), MM_DTYPE),
        grid=(bsz, nk),
        in_specs=[qspec(A_HEADS * HEAD_DIM), qspec(IDX_HEADS * IDX_DIM), qspec(IDX_HEADS),
                  kspec(T, IDX_DIM), kspec(T, A_LATENT), kspec(A_LATENT, T),
                  _const_spec(bias_a.shape), _const_spec(wukT.shape), _const_spec(wuvT.shape)],
        out_specs=pl.BlockSpec((None, T, A_HEADS * HEAD_DIM), lambda b, i: (b, i, 0)),
        scratch_shapes=[pltpu.VMEM((nk, T, T), I32),
                        pltpu.VMEM((nk, KEY_BITS + 1, SUBLANES, T), I32),
                        pltpu.VMEM((A_HEADS, A_LATENT, T), MM_DTYPE),
                        pltpu.VMEM((A_HEADS, A_LATENT + SUM_ROWS, T), F32),
                        pltpu.VMEM((A_HEADS, T, T), F32),
                        pltpu.VMEM((A_HEADS * HEAD_DIM, T), F32)],
        compiler_params=_cparams(2), name="dsa_attention",
    )(aqT, iqT, iwT, kidx, kv, kvT, bias_a, wukT, wuvT)


def _diff_kernel(qT_ref, k_ref, vT_ref, bias_ref, lam_ref, gsub_ref, out_ref,
                 qz_ref, acc_ref, s_ref, oT_ref, *, lam_init):
    T = ATT_TILE
    dv = 2 * HEAD_DIM
    n_chain = 2 * B_HEADS
    qi = pl.program_id(1)
    half = lax.broadcasted_iota(I32, (dv, T), 0) < HEAD_DIM
    for h in range(B_HEADS):
        q = qT_ref[h * dv:(h + 1) * dv, :].astype(F32)
        qz_ref[2 * h] = jnp.where(half, q, 0.0).astype(MM_DTYPE)
        qz_ref[2 * h + 1] = jnp.where(half, 0.0, q).astype(MM_DTYPE)
    acc_ref[...] = jnp.zeros(acc_ref.shape, F32)

    def stage_logits(kj, c):
        h = c // 2
        off = jnp.minimum(qi - kj, N_OFFSETS - 1)
        s = bias_ref[h, off] + jnp.dot(k_ref[kj, :, h * dv:(h + 1) * dv], qz_ref[c],
                                       preferred_element_type=F32)
        s_ref[c] = s
        return jnp.max(s, axis=0, keepdims=True)

    tile_max0 = tuple(stage_logits(0, c) for c in range(n_chain))

    def step(kj, carry, last):
        ms, tile_max = carry
        new_m, new_max = [], []
        for c in range(n_chain):
            h = c // 2
            m_new = jnp.maximum(ms[c], tile_max[c])
            p = jnp.exp2(s_ref[c] - m_new).astype(MM_DTYPE)
            alpha = jnp.exp2(ms[c] - m_new)
            if not last:
                new_max.append(stage_logits(kj + 1, c))
            vT = _with_ones_row(vT_ref[kj, h * dv:(h + 1) * dv, :])
            acc_ref[c] = alpha * acc_ref[c] + jnp.dot(vT, p, preferred_element_type=F32)
            new_m.append(m_new)
        return tuple(new_m), tuple(new_max)

    carry = _loop_by_pairs(qi, lambda kj, cr: step(kj, cr, False),
                           (_initial_max(n_chain, T), tile_max0))
    step(qi, carry, True)

    lr = lam_ref[...]
    lam = (jnp.exp(jnp.sum(lr[0:1, :] * lr[1:2, :], axis=1, keepdims=True))
           - jnp.exp(jnp.sum(lr[2:3, :] * lr[3:4, :], axis=1, keepdims=True)) + lam_init)

    def normalised(c):
        return acc_ref[c, 0:dv, :] / acc_ref[c, dv:dv + 1, :]

    for h in range(B_HEADS):
        attn = normalised(2 * h) - lam * normalised(2 * h + 1)
        y = attn * lax.rsqrt(jnp.mean(attn * attn, axis=0, keepdims=True) + EPS)
        oT_ref[h * dv:(h + 1) * dv, :] = y * gsub_ref[...] * (1.0 - lam_init)
    out_ref[...] = oT_ref[...].T.astype(out_ref.dtype)


def _diff(bqT, bk, bvT, bias_b, lam_rows, gsub, lam_init):
    bsz, _, seq = bqT.shape
    T = ATT_TILE
    nk = seq // T
    dv = 2 * HEAD_DIM
    qspec = pl.BlockSpec((None, B_HEADS * dv, T), lambda b, i: (b, 0, i))
    return pl.pallas_call(
        functools.partial(_diff_kernel, lam_init=lam_init),
        out_shape=jax.ShapeDtypeStruct((bsz, seq, B_HEADS * dv), MM_DTYPE),
        grid=(bsz, nk),
        in_specs=[qspec,
                  pl.BlockSpec((None, nk, T, B_HEADS * dv), lambda b, i: (b, 0, 0, 0)),
                  pl.BlockSpec((None, nk, B_HEADS * dv, T), lambda b, i: (b, 0, 0, 0)),
                  _const_spec(bias_b.shape), _const_spec((4, HEAD_DIM)), _const_spec((dv, 1))],
        out_specs=pl.BlockSpec((None, T, B_HEADS * dv), lambda b, i: (b, i, 0)),
        scratch_shapes=[pltpu.VMEM((2 * B_HEADS, dv, T), MM_DTYPE),
                        pltpu.VMEM((2 * B_HEADS, dv + SUM_ROWS, T), F32),
                        pltpu.VMEM((2 * B_HEADS, T, T), F32),
                        pltpu.VMEM((B_HEADS * dv, T), F32)],
        compiler_params=_cparams(2), name="diff_attention",
    )(bqT, bk, bvT, bias_b, lam_rows, gsub)


def _dil_kernel(cur_ref, halo_ref, bias_ref, out_ref, lse_ref, *, tq):
    n = C_BAND
    wid = C_HPG * HEAD_DIM
    halo_lo = jnp.where(pl.program_id(2) == 0, jnp.int32(n), jnp.int32(0))
    i = lax.broadcasted_iota(I32, (n, 2 * n), 0)
    j = lax.broadcasted_iota(I32, (n, 2 * n), 1)
    lane_head = lax.broadcasted_iota(I32, (n, wid), 1) // HEAD_DIM
    in_head = [lane_head == h for h in range(C_HPG)]
    band = jnp.where(j >= i, jnp.where(j <= i + n, 0.0, NEG), NEG)
    band0 = jnp.where(j >= jnp.maximum(i, halo_lo), jnp.where(j <= i + n, 0.0, NEG), NEG)
    bias = [bias_ref[h] + band for h in range(C_HPG)]
    bias0 = [bias_ref[h] + band0 for h in range(C_HPG)]

    def band_rows(r, c, lo, hi):
        if c == 0:
            return jnp.concatenate([halo_ref[r, :, lo:hi], cur_ref[r, 0:n, lo:hi]], axis=0)
        return cur_ref[r, (c - 1) * n:(c + 1) * n, lo:hi]

    blocks = [(r, c) for r in range(cur_ref.shape[0]) for c in range(tq // n)]
    logits = []
    for r, c in blocks:
        q = cur_ref[r, c * n:(c + 1) * n, 0:wid].astype(F32) * ATTN_SCALE
        keys = band_rows(r, c, wid, 2 * wid)
        for h in range(C_HPG):
            qh = jnp.where(in_head[h], q, 0.0).astype(MM_DTYPE)
            s = lax.dot_general(qh, keys, (((1,), (1,)), ((), ())), preferred_element_type=F32)
            logits.append(s + (bias0 if c == 0 else bias)[h])
    for k, (r, c) in enumerate(blocks):
        vals = band_rows(r, c, 2 * wid, 3 * wid)
        out = jnp.zeros((n, wid), F32)
        lse = jnp.zeros((n, wid), F32)
        for h in range(C_HPG):
            s = logits[k * C_HPG + h]
            m = jnp.max(s, axis=1, keepdims=True)
            p = jnp.exp(s - m)
            den = jnp.sum(p, axis=1, keepdims=True)
            o = jnp.dot(p.astype(MM_DTYPE), vals, preferred_element_type=F32) * (1.0 / den)
            out = jnp.where(in_head[h], o, out)
            lse = jnp.where(in_head[h], m + jnp.log(den), lse)
        out_ref[r, c * n:(c + 1) * n, :] = out
        lse_ref[r, c * n:(c + 1) * n, :] = lse


def _dilated_group(cg, bias_g, g):
    bsz, dil, m, gw = cg.shape
    wid = C_HPG * HEAD_DIM
    n = C_BAND
    assert m % n == 0 and gw == 3 * wid
    tq = min(m, 512)
    n_res = max(1, min(dil, 512 // tq))
    cur = pl.BlockSpec((None, n_res, tq, gw), lambda b, r, i: (b, r, i, 0))
    halo = pl.BlockSpec((None, n_res, n, gw),
                        lambda b, r, i: (b, r, jnp.maximum(i * (tq // n) - 1, 0), 0))
    outspec = pl.BlockSpec((None, n_res, tq, wid), lambda b, r, i: (b, r, i, 0))
    return pl.pallas_call(
        functools.partial(_dil_kernel, tq=tq),
        out_shape=[jax.ShapeDtypeStruct((bsz, dil, m, wid), F32)] * 2,
        grid=(bsz, dil // n_res, m // tq),
        in_specs=[cur, halo, pl.BlockSpec((C_HPG, n, 2 * n), lambda b, r, i: (0, 0, 0))],
        out_specs=[outspec, outspec],
        compiler_params=_cparams(3), name=f"dilated_group{g}",
    )(cg, cg, bias_g)


def _merge_ffn_kernel(x_ref, mod_ref, g1_ref, g2_ref, gf_ref, oa_ref, ob_ref,
                      c0_ref, c1_ref, c2_ref, s0_ref, s1_ref, s2_ref,
                      wza, wzb, wzc, wba, wbb, wbc, wo, wgu, wd, out_ref, tok_ref, *, final_norm):
    x = x_ref[...]
    tm = x.shape[0]
    h = _rms(x) * g1_ref[...]
    h = h * (1.0 + mod_ref[1:2, :]) + mod_ref[0:1, :]
    hb = h.astype(MM_DTYPE)

    def token_order(k, ref):
        dil, _, w = ref.shape
        if dil == 1:
            return ref[0]
        n_chunk = w // LANES
        for r in range(dil):
            for j in range(n_chunk):
                tok_ref[k * n_chunk + j, pl.ds(r, tm // dil, stride=dil), :] = ref[
                    r, :, j * LANES:(j + 1) * LANES]
        return jnp.concatenate([tok_ref[k * n_chunk + j] for j in range(n_chunk)], axis=1)

    s0, s1, s2 = s0_ref[0], token_order(0, s1_ref), token_order(1, s2_ref)
    c0, c1, c2 = c0_ref[0], token_order(2, c1_ref), token_order(3, c2_ref)
    mx = jnp.maximum(jnp.maximum(s0, s1), s2)
    e0, e1, e2 = jnp.exp(s0 - mx), jnp.exp(s1 - mx), jnp.exp(s2 - mx)
    oc = (e0 * c0 + e1 * c1 + e2 * c2) / (e0 + e1 + e2)

    def gated(wz, o, wb):
        z = jnp.dot(hb, wz[...], preferred_element_type=F32)
        return jax.nn.sigmoid(z) * jnp.dot(o, wb[...], preferred_element_type=F32)

    merged = (gated(wza, oa_ref[...], wba) + gated(wzb, ob_ref[...], wbb)
              + gated(wzc, oc.astype(MM_DTYPE), wbc))
    y = jnp.dot(merged.astype(MM_DTYPE), wo[...], preferred_element_type=F32)
    x = x + mod_ref[2:3, :] * y

    h = _rms(x) * g2_ref[...]
    h = h * (1.0 + mod_ref[4:5, :]) + mod_ref[3:4, :]
    hb = h.astype(MM_DTYPE)
    acc = jnp.zeros(x.shape, F32)
    for c in range(D_FF // FFN_CHUNK):
        cols = slice(c * FFN_CHUNK, (c + 1) * FFN_CHUNK)
        fg = jnp.dot(hb, wgu[:, cols], preferred_element_type=F32)
        fu = jnp.dot(hb, wgu[:, D_FF + c * FFN_CHUNK:D_FF + (c + 1) * FFN_CHUNK],
                     preferred_element_type=F32)
        act = (fg * jax.nn.sigmoid(fg) * fu).astype(MM_DTYPE)
        acc = acc + jnp.dot(act, wd[cols, :], preferred_element_type=F32)
    y = x + mod_ref[5:6, :] * acc
    if final_norm:
        y = _rms(y) * gf_ref[...]
    out_ref[...] = y


def _merge_ffn(x2d, mod_l, g1, g2, gf, oa, ob, ocs, lses, ws, seq, final_norm):
    n, d = x2d.shape
    tm = ROW_TILE
    per_b = seq // tm
    row = lambda wd: pl.BlockSpec((tm, wd), lambda i: (i, 0))
    vec = _const_spec((1, d))
    wid = C_HPG * HEAD_DIM
    res = [pl.BlockSpec((None, dil, tm // dil, wid), lambda i: (i // per_b, 0, i % per_b, 0))
           for _, dil in C_GROUPS]
    in_specs = [row(d), pl.BlockSpec((None, 6, d), lambda i: (i // per_b, 0, 0)), vec, vec, vec,
                row(oa.shape[1]), row(ob.shape[1])] + res + res
    in_specs += [_const_spec(w.shape) for w in ws]
    return pl.pallas_call(
        functools.partial(_merge_ffn_kernel, final_norm=final_norm),
        out_shape=jax.ShapeDtypeStruct((n, d), F32), grid=(n // tm,),
        in_specs=in_specs, out_specs=row(d),
        scratch_shapes=[pltpu.VMEM((4 * wid // LANES, tm, LANES), F32)],
        compiler_params=_cparams(1), name="merge_ffn",
    )(x2d, mod_l, g1, g2, gf, oa, ob, *ocs, *lses, *ws)


def kernel(x, c, w_ada, b_ada, g_norm1, w_in, w_uk, w_uv, g_kv, lam_q1, lam_k1, lam_q2, lam_k2,
           g_subln, w_branch_a, w_branch_b, w_branch_c, w_out, g_norm2, w_gate_up, w_down,
           rel_bias, g_final):
    bsz, seq, d = x.shape
    depth = w_ada.shape[0]
    T = ATT_TILE
    nk = seq // T
    assert d == D_MODEL and seq % T == 0 and seq % ROW_TILE == 0
    n = bsz * seq
    cast = lambda w: w.astype(MM_DTYPE)

    thresholds = _bucket_thresholds(seq + 2 * C_BAND * C_GROUPS[-1][1])
    assert seq <= N_OFFSETS * T or (N_OFFSETS - 2) * T + 1 >= thresholds[-1]
    tab = rel_bias.reshape(-1)
    bias_a = _bias_att_tiles(tab, 0, A_HEADS, T, thresholds)
    bias_b = _bias_att_tiles(tab, A_HEADS, B_HEADS, T, thresholds)
    bias_c = _bias_dil_tiles(tab, A_HEADS + B_HEADS, thresholds)

    mod = _modulation(c, w_ada, b_ada).reshape(depth, bsz, 6, d)

    splits = (A_HEADS * HEAD_DIM, A_LATENT, IDX_HEADS * IDX_DIM, IDX_DIM, IDX_HEADS,
              B_HEADS * 2 * HEAD_DIM, B_HEADS * 2 * HEAD_DIM, B_HEADS * 2 * HEAD_DIM,
              C_HEADS * HEAD_DIM, C_HEADS * HEAD_DIM, C_HEADS * HEAD_DIM, d, d, d)
    offs = np.concatenate([[0], np.cumsum(splits)])
    seg = lambda w, k: w[:, int(offs[k]):int(offs[k + 1])]

    x2d = x.reshape(n, d)
    for l in range(depth):
        wl = w_in[l]
        castT = lambda w: w.T.astype(MM_DTYPE)
        wT_iw = jnp.pad(seg(wl, 4).T, ((0, 16 - IDX_HEADS), (0, 0)))
        wid = C_HPG * HEAD_DIM
        w_c = jnp.concatenate([seg(wl, k)[:, g * wid:(g + 1) * wid]
                               for g in range(len(C_GROUPS)) for k in (8, 9, 10)], axis=1)
        ws_in = [castT(seg(wl, 0)), castT(seg(wl, 2)), cast(wT_iw), castT(seg(wl, 5)), castT(seg(wl, 7)),
                 cast(seg(wl, 3)), cast(seg(wl, 1)), cast(seg(wl, 6)), cast(w_c)]
        g1 = g_norm1[l].reshape(1, d)
        (aqT, iqT, iwT, bqT, bvT, ik, kv, kvT, bk, cg0, cg1, cg2) = _in_proj(
            x2d, mod[l], g1, g_kv[l].reshape(1, A_LATENT), ws_in, bsz, seq)

        o_a = _dsa(aqT, iqT, iwT,
                   ik.reshape(bsz, nk, T, IDX_DIM), kv.reshape(bsz, nk, T, A_LATENT), kvT,
                   bias_a, cast(w_uk[l].transpose(0, 2, 1)), cast(w_uv[l].transpose(0, 2, 1)))
        o_a = o_a.reshape(n, -1)

        lam_init = 0.8 - 0.6 * math.exp(-0.3 * l)
        lam_rows = jnp.stack([lam_q1[l], lam_k1[l], lam_q2[l], lam_k2[l]])
        dv = 2 * HEAD_DIM
        o_b = _diff(bqT, bk.reshape(bsz, nk, T, B_HEADS * dv), bvT, bias_b, lam_rows,
                    g_subln[l].reshape(dv, 1), lam_init)
        o_b = o_b.reshape(n, -1)

        ocs, lses = [], []
        for g, (cg, (window, dil)) in enumerate(zip((cg0, cg1, cg2), C_GROUPS)):
            assert window // dil == C_BAND
            o, s = _dilated_group(cg, bias_c[g * C_HPG:(g + 1) * C_HPG], g)
            ocs.append(o)
            lses.append(s)

        ws_out = [cast(seg(wl, 11)), cast(seg(wl, 12)), cast(seg(wl, 13)),
                  cast(w_branch_a[l]), cast(w_branch_b[l]), cast(w_branch_c[l]), cast(w_out[l]),
                  cast(w_gate_up[l]), cast(w_down[l])]
        x2d = _merge_ffn(x2d, mod[l], g1, g_norm2[l].reshape(1, d), g_final.reshape(1, d),
                         o_a, o_b, ocs, lses, ws_out, seq, final_norm=(l == depth - 1))
    return x2d.reshape(bsz, seq, d)
```

```python
import functools
import math

import numpy as np
import jax
import jax.numpy as jnp
from jax import lax
from jax.experimental import pallas as pl
from jax.experimental.pallas import tpu as pltpu

D_MODEL = 1024
HEAD_DIM = 64
ATTN_SCALE = HEAD_DIM ** -0.5
LOG2E = math.log2(math.e)
A_HEADS = 8
A_LATENT = 128
IDX_HEADS = 8
IDX_DIM = 64
IDX_SCALE = (IDX_HEADS * IDX_DIM) ** -0.5
TOPK_MAX = 256
B_HEADS = 4
C_GROUPS = ((128, 1), (512, 4), (2048, 16))
C_HPG = 4
C_HEADS = C_HPG * len(C_GROUPS)
N_BUCKETS = 32
MAX_DISTANCE = 2048
N_BIAS_HEADS = A_HEADS + B_HEADS + C_HEADS
D_FF = -(-8 * D_MODEL // (3 * 256)) * 256
EPS = 1e-6

MM_DTYPE = jnp.bfloat16
F32 = jnp.float32
I32 = jnp.int32

ATT_TILE = 256
N_OFFSETS = 8
C_BAND = 128
ROW_TILE = 512
FFN_CHUNK = 256
MOD_COLS = 1536
NEG = -1e30
INT_MIN = -2 ** 31
LANES = 128
SUBLANES = 8
KEY_BITS = 32
VMEM_LIMIT = 56 * 1024 * 1024


def _cparams(n_axes, vmem=VMEM_LIMIT):
    return pltpu.CompilerParams(dimension_semantics=("arbitrary",) * n_axes,
                                vmem_limit_bytes=vmem)


def _const_spec(shape):
    nd = len(shape)
    return pl.BlockSpec(shape, lambda *_: (0,) * nd, pipeline_mode=pl.Buffered(1))


def _bucket_thresholds(max_dist):
    n = np.arange(max_dist + 1)
    max_exact = N_BUCKETS // 2
    nf = np.maximum(n, 1).astype(np.float32)
    large = max_exact + (np.log(nf / np.float32(max_exact))
                         / np.float32(math.log(MAX_DISTANCE / max_exact))
                         * np.float32(N_BUCKETS - max_exact)).astype(np.int32)
    large = np.minimum(large, N_BUCKETS - 1)
    bucket = np.where(n < max_exact, n, large)
    assert np.all(np.diff(bucket) >= 0)
    thr = []
    for k in range(1, N_BUCKETS):
        idx = np.nonzero(bucket >= k)[0]
        thr.append(int(idx[0]) if idx.size else None)
    return thr


def _bias_from_dist(dist, tab_ref, col, thresholds, lo=0, hi=None):
    reached = [k for k, thr in enumerate(thresholds, start=1) if thr is not None]
    base = max([0] + [k for k in reached if thresholds[k - 1] <= lo])
    b = jnp.full(dist.shape, tab_ref[base * N_BIAS_HEADS + col], F32)
    for k in reached:
        thr = thresholds[k - 1]
        if thr > lo and (hi is None or thr <= hi):
            b = jnp.where(dist >= thr, tab_ref[k * N_BIAS_HEADS + col], b)
    return b


def _bias_att_kernel(tab_ref, out_ref, *, head0, thresholds):
    h = pl.program_id(0)
    n_off, tile, _ = out_ref.shape
    row = lax.broadcasted_iota(I32, (tile, tile), 0)
    colq = lax.broadcasted_iota(I32, (tile, tile), 1)
    for o in range(n_off):
        dist = jnp.maximum(o * tile + colq - row, 0)
        lo, hi = max(o * tile - (tile - 1), 0), o * tile + tile - 1
        b = _bias_from_dist(dist, tab_ref, head0 + h, thresholds, lo, hi) * LOG2E
        if o == 0:
            b = jnp.where(row <= colq, b, NEG)
        out_ref[o] = b


def _bias_att_tiles(tab, head0, n_heads, tile, thresholds):
    return pl.pallas_call(
        functools.partial(_bias_att_kernel, head0=head0, thresholds=thresholds),
        out_shape=jax.ShapeDtypeStruct((n_heads, N_OFFSETS, tile, tile), F32),
        grid=(n_heads,),
        in_specs=[pl.BlockSpec(memory_space=pltpu.SMEM)],
        out_specs=pl.BlockSpec((None, N_OFFSETS, tile, tile), lambda h: (h, 0, 0, 0)),
        compiler_params=_cparams(1),
        name="bias_att_tiles",
    )(tab)


def _bias_dil_kernel(tab_ref, out_ref, *, head0, thresholds):
    h = pl.program_id(0)
    g = h // C_HPG
    dil = jnp.where(g == 0, C_GROUPS[0][1], jnp.where(g == 1, C_GROUPS[1][1], C_GROUPS[2][1]))
    i = lax.broadcasted_iota(I32, (C_BAND, 2 * C_BAND), 0)
    j = lax.broadcasted_iota(I32, (C_BAND, 2 * C_BAND), 1)
    dist = jnp.maximum((i - j + C_BAND) * dil, 0)
    out_ref[...] = _bias_from_dist(dist, tab_ref, head0 + h, thresholds)


def _bias_dil_tiles(tab, head0, thresholds):
    return pl.pallas_call(
        functools.partial(_bias_dil_kernel, head0=head0, thresholds=thresholds),
        out_shape=jax.ShapeDtypeStruct((C_HEADS, C_BAND, 2 * C_BAND), F32),
        grid=(C_HEADS,),
        in_specs=[pl.BlockSpec(memory_space=pltpu.SMEM)],
        out_specs=pl.BlockSpec((None, C_BAND, 2 * C_BAND), lambda h: (h, 0, 0)),
        compiler_params=_cparams(1),
        name="bias_dil_tiles",
    )(tab)


def _mod_kernel(c_ref, w_ref, b_ref, out_ref):
    c = c_ref[...]
    ca = (c * jax.nn.sigmoid(c)).astype(MM_DTYPE)
    out_ref[...] = jnp.dot(ca, w_ref[...].astype(MM_DTYPE), preferred_element_type=F32) + b_ref[...]


def _modulation(c, w_ada, b_ada):
    depth, d, wid = w_ada.shape
    bsz = c.shape[0]
    tn = MOD_COLS
    assert wid % tn == 0
    return pl.pallas_call(
        _mod_kernel,
        out_shape=jax.ShapeDtypeStruct((depth, bsz, wid), F32),
        grid=(depth, wid // tn),
        in_specs=[pl.BlockSpec((bsz, d), lambda l, j: (0, 0)),
                  pl.BlockSpec((None, d, tn), lambda l, j: (l, 0, j)),
                  pl.BlockSpec((None, 1, tn), lambda l, j: (l, 0, j))],
        out_specs=pl.BlockSpec((None, bsz, tn), lambda l, j: (l, 0, j)),
        compiler_params=_cparams(2),
        name="adaln_modulation",
    )(c, w_ada, b_ada.reshape(depth, 1, wid))


def _rms(x):
    return x * lax.rsqrt(jnp.mean(x * x, axis=-1, keepdims=True) + EPS)


def _in_kernel(x_ref, mod_ref, g1_ref, gkv_ref,
               wT_aq, wT_iq, wT_iw, wT_bq, wT_bv, w_ik, w_kv, w_bk, w_c,
               o_aqT, o_iqT, o_iwT, o_bqT, o_bvT, o_ik, o_kv, o_kvT, o_bk, o_c0, o_c1, o_c2,
               c_scr):
    T = ATT_TILE
    tm = x_ref.shape[0]
    h = _rms(x_ref[...]) * g1_ref[...]
    h = h * (1.0 + mod_ref[1:2, :]) + mod_ref[0:1, :]
    hb = h.astype(MM_DTYPE)

    def mm(w):
        return jnp.dot(hb, w[...], preferred_element_type=F32)

    def mm_t(wT):
        return lax.dot_general(wT[...], hb, (((1,), (1,)), ((), ())), preferred_element_type=F32)

    o_aqT[...] = mm_t(wT_aq).astype(o_aqT.dtype)
    o_iqT[...] = mm_t(wT_iq).astype(o_iqT.dtype)
    o_bqT[...] = (mm_t(wT_bq) * (ATTN_SCALE * LOG2E)).astype(o_bqT.dtype)
    o_iwT[...] = (mm_t(wT_iw) * IDX_SCALE)[:IDX_HEADS]
    bvT = mm_t(wT_bv).astype(o_bvT.dtype)
    kv = _rms(mm(w_kv)) * gkv_ref[...]
    kvT = kv.T.astype(o_kvT.dtype)
    for j in range(tm // T):
        o_bvT[j] = bvT[:, j * T:(j + 1) * T]
        o_kvT[j] = kvT[:, j * T:(j + 1) * T]
    o_kv[...] = kv.astype(o_kv.dtype)
    o_ik[...] = mm(w_ik).astype(o_ik.dtype)
    o_bk[...] = mm(w_bk).astype(o_bk.dtype)

    yc = mm(w_c)
    n_chunk = yc.shape[1] // LANES
    for j in range(n_chunk):
        c_scr[j] = yc[:, j * LANES:(j + 1) * LANES]
    per_group = n_chunk // len(C_GROUPS)
    for g, o_c in enumerate((o_c0, o_c1, o_c2)):
        dil = C_GROUPS[g][1]
        for r in range(dil):
            for jj in range(per_group):
                o_c[r, :, jj * LANES:(jj + 1) * LANES] = c_scr[
                    g * per_group + jj, pl.ds(r, tm // dil, stride=dil), :].astype(o_c.dtype)


def _in_proj(x2d, mod_l, g1, gkv, ws, bsz, seq):
    n, d = x2d.shape
    tm = ROW_TILE
    T = ATT_TILE
    per_b = seq // tm
    nk = seq // T
    hd = A_HEADS * HEAD_DIM
    bw = B_HEADS * 2 * HEAD_DIM
    gw = 3 * C_HPG * HEAD_DIM
    in_specs = [pl.BlockSpec((tm, d), lambda i: (i, 0)),
                pl.BlockSpec((None, 6, d), lambda i: (i // per_b, 0, 0)),
                _const_spec((1, d)), _const_spec((1, A_LATENT))]
    in_specs += [_const_spec(w.shape) for w in ws]

    def tspec(rows):
        return pl.BlockSpec((None, rows, tm), lambda i: (i // per_b, 0, i % per_b))

    def tile_tspec(rows):
        return pl.BlockSpec((None, tm // T, rows, T), lambda i: (i // per_b, i % per_b, 0, 0))

    def rspec(wd):
        return pl.BlockSpec((tm, wd), lambda i: (i, 0))

    def cspec(dil):
        return pl.BlockSpec((None, dil, tm // dil, gw), lambda i: (i // per_b, 0, i % per_b, 0))

    sds = jax.ShapeDtypeStruct
    out_specs = [tspec(hd), tspec(IDX_HEADS * IDX_DIM), tspec(IDX_HEADS), tspec(bw),
                 tile_tspec(bw), rspec(IDX_DIM), rspec(A_LATENT), tile_tspec(A_LATENT), rspec(bw)]
    out_shape = [sds((bsz, hd, seq), MM_DTYPE), sds((bsz, IDX_HEADS * IDX_DIM, seq), MM_DTYPE),
                 sds((bsz, IDX_HEADS, seq), F32), sds((bsz, bw, seq), MM_DTYPE),
                 sds((bsz, nk, bw, T), MM_DTYPE), sds((n, IDX_DIM), MM_DTYPE),
                 sds((n, A_LATENT), MM_DTYPE), sds((bsz, nk, A_LATENT, T), MM_DTYPE),
                 sds((n, bw), MM_DTYPE)]
    for _, dil in C_GROUPS:
        out_specs.append(cspec(dil))
        out_shape.append(sds((bsz, dil, seq // dil, gw), MM_DTYPE))
    return pl.pallas_call(
        _in_kernel, out_shape=out_shape, grid=(n // tm,),
        in_specs=in_specs, out_specs=out_specs,
        scratch_shapes=[pltpu.VMEM((3 * gw // LANES, tm, LANES), F32)],
        compiler_params=_cparams(1), name="in_proj",
    )(x2d, mod_l, g1, gkv, *ws)


def _initial_max(n_chains, tile):
    return tuple(jnp.full((1, tile), NEG, F32) for _ in range(n_chains))


def _loop_by_pairs(n, body, init):
    pairs = lax.shift_right_logical(n, jnp.int32(1))
    carry = lax.fori_loop(0, pairs, lambda i, c: body(2 * i + 1, body(2 * i, c)), init)
    return lax.fori_loop(2 * pairs, n, body, carry)


SUM_ROWS = 16


def _with_ones_row(vT):
    row = lax.broadcasted_iota(I32, (SUM_ROWS, vT.shape[1]), 0)
    ones = jnp.where(row == 0, 1.0, 0.0).astype(vT.dtype)
    return jnp.concatenate([vT, ones], axis=0)


def _bit_transpose32(words):
    a = list(words)
    j, mask = 16, 0x0000FFFF
    while j:
        k = 0
        while k < 32:
            t = (a[k] ^ lax.shift_right_logical(a[k + j], jnp.int32(j))) & jnp.int32(mask)
            a[k] = a[k] ^ t
            a[k + j] = a[k + j] ^ lax.shift_left(t, jnp.int32(j))
            k = (k + j + 1) & ~j
        j >>= 1
        mask = (mask ^ (mask << j)) & 0xFFFFFFFF
    return a


def _dsa_kernel(aqT_ref, iqT_ref, iwT_ref, kidx_ref, kv_ref, kvT_ref, bias_ref, wukT_ref, wuvT_ref,
                out_ref, keys_ref, planes_ref, qlat_ref, acc_ref, s_ref, oT_ref, *, topk, idx_bits):
    T = ATT_TILE
    qi = pl.program_id(1)
    nk = qi + 1

    for h in range(A_HEADS):
        q = jnp.dot(wukT_ref[h], aqT_ref[h * HEAD_DIM:(h + 1) * HEAD_DIM, :],
                    preferred_element_type=F32) * (ATTN_SCALE * LOG2E)
        qlat_ref[h] = q.astype(qlat_ref.dtype)

    s_loc = lax.broadcasted_iota(I32, (T, T), 0)
    t_loc = lax.broadcasted_iota(I32, (T, T), 1)

    def score_tile(kj, diagonal):
        kt = kidx_ref[kj]
        acc = jnp.zeros((T, T), F32)
        for h in range(IDX_HEADS):
            s = jnp.dot(kt, iqT_ref[h * IDX_DIM:(h + 1) * IDX_DIM, :], preferred_element_type=F32)
            acc = acc + jnp.maximum(s, 0.0) * iwT_ref[h:h + 1, :]
        bits = lax.bitcast_convert_type(acc, I32)
        key = jnp.where(bits < 0, bits ^ jnp.int32(0x7FFFFFFF), bits)
        if diagonal:
            key = jnp.where(s_loc <= t_loc, key, jnp.int32(INT_MIN))
        keys_ref[kj] = key
        planes = _bit_transpose32([key[SUBLANES * i:SUBLANES * (i + 1), :] ^ jnp.int32(INT_MIN)
                                   for i in range(KEY_BITS)])
        for b in range(KEY_BITS):
            planes_ref[kj, b] = planes[b]

    def off_diagonal(kj, carry):
        score_tile(kj, False)
        return carry

    _loop_by_pairs(qi, off_diagonal, 0)
    score_tile(qi, True)

    n_tiles = planes_ref.shape[0]

    @pl.when(qi == 0)
    def _():
        def clear_planes(kj, carry):
            for b in range(KEY_BITS):
                planes_ref[kj, b] = jnp.zeros((SUBLANES, T), I32)
            return carry

        lax.fori_loop(1, n_tiles, clear_planes, 0)

    one = jnp.int32(1)
    nil = jnp.int32(0)
    zero = jnp.zeros((1, T), I32)

    n_rows = n_tiles * SUBLANES
    tile_of_row = lax.broadcasted_iota(I32, (n_rows, T), 0) // SUBLANES
    tied0 = jnp.where(tile_of_row < nk, jnp.int32(-1), nil)

    def bit_step(b, carry):
        tied, n_gt, kth_u = carry
        ones = tied & planes_ref[:, b].reshape(n_rows, T)
        n1 = jnp.sum(lax.population_count(ones), axis=0, keepdims=True)
        take = (n_gt + n1) >= topk
        tied = jnp.where(take, ones, tied ^ ones)
        n_gt = jnp.where(take, n_gt, n_gt + n1)
        kth_u = jnp.where(take, kth_u | lax.shift_left(one, jnp.int32(KEY_BITS - 1) - b), kth_u)
        return tied, n_gt, kth_u

    tied, n_gt, kth_u = lax.fori_loop(0, KEY_BITS, bit_step, (tied0, zero, zero))
    n_eq = jnp.sum(lax.population_count(tied), axis=0, keepdims=True)
    n_ge = jnp.where(kth_u == nil, n_gt, n_gt + n_eq)
    kth = jnp.maximum(kth_u ^ jnp.int32(INT_MIN), jnp.int32(INT_MIN + 1))

    @pl.when(jnp.max(n_ge) > topk)
    def _():
        need = topk - n_gt
        r_bits = SUBLANES.bit_length() - 1
        t_bits = T.bit_length() - 1
        word_row = lax.broadcasted_iota(I32, (n_rows, T), 0)
        word_tile = lax.shift_right_logical(word_row, jnp.int32(r_bits))
        word_r = word_row & jnp.int32(SUBLANES - 1)
        tied_count = lax.population_count(tied)

        def tied_before(pos):
            p_tile = lax.shift_right_logical(pos, jnp.int32(t_bits))
            p_i = lax.shift_right_logical(pos, jnp.int32(r_bits)) & jnp.int32(KEY_BITS - 1)
            p_r = pos & jnp.int32(SUBLANES - 1)
            above = jnp.where(p_i == nil, nil,
                              lax.shift_left(jnp.full_like(p_i, -1), jnp.int32(KEY_BITS) - p_i))
            at_i = lax.shift_right_logical(
                tied, jnp.broadcast_to(jnp.int32(KEY_BITS - 1) - p_i, tied.shape)) & one
            inside = lax.population_count(tied & above) + jnp.where(word_r < p_r, at_i, nil)
            hit = jnp.where(word_tile < p_tile, tied_count,
                            jnp.where(word_tile == p_tile, inside, nil))
            return jnp.sum(hit, axis=0, keepdims=True)

        def pos_step(i, cut):
            cand = cut | jnp.left_shift(one, jnp.int32(idx_bits - 1) - i)
            return jnp.where(tied_before(cand) < need, cand, cut)

        cut = lax.fori_loop(0, idx_bits, pos_step, zero)

        def demote(kj, carry):
            kk = keys_ref[kj]
            lowered = jnp.where((kj * T + s_loc) > cut, kth - one, kk)
            keys_ref[kj] = jnp.where(kk == kth, lowered, kk)
            return carry

        lax.fori_loop(0, nk, demote, 0)

    acc_ref[...] = jnp.zeros(acc_ref.shape, F32)

    def stage_logits(kj):
        unselected = jnp.where(keys_ref[kj] >= kth, 0.0, NEG)
        off = jnp.minimum(qi - kj, N_OFFSETS - 1)
        kvt = kv_ref[kj]

        def one_head(h):
            s = bias_ref[h, off] + jnp.dot(kvt, qlat_ref[h], preferred_element_type=F32)
            s = s + unselected
            s_ref[h] = s
            return jnp.max(s, axis=0, keepdims=True)

        return one_head

    first = stage_logits(0)
    tile_max0 = tuple(first(h) for h in range(A_HEADS))

    def attend(kj, carry):
        ms, tile_max = carry
        kvTt = _with_ones_row(kvT_ref[kj])
        stage_next = stage_logits(jnp.minimum(kj + 1, nk - 1))
        new_m, new_max = [], []
        for h in range(A_HEADS):
            m_new = jnp.maximum(ms[h], tile_max[h])
            p = jnp.exp2(s_ref[h] - m_new).astype(MM_DTYPE)
            alpha = jnp.exp2(ms[h] - m_new)
            new_max.append(stage_next(h))
            acc_ref[h] = alpha * acc_ref[h] + jnp.dot(kvTt, p, preferred_element_type=F32)
            new_m.append(m_new)
        return tuple(new_m), tuple(new_max)

    _loop_by_pairs(nk, attend, (_initial_max(A_HEADS, T), tile_max0))

    for h in range(A_HEADS):
        o_lat = (acc_ref[h, 0:A_LATENT, :] / acc_ref[h, A_LATENT:A_LATENT + 1, :]).astype(MM_DTYPE)
        oT_ref[h * HEAD_DIM:(h + 1) * HEAD_DIM, :] = jnp.dot(
            wuvT_ref[h], o_lat, preferred_element_type=F32)
    out_ref[...] = oT_ref[...].T.astype(out_ref.dtype)


def _dsa(aqT, iqT, iwT, kidx, kv, kvT, bias_a, wukT, wuvT):
    bsz, _, seq = aqT.shape
    T = ATT_TILE
    nk = seq // T
    topk = min(TOPK_MAX, seq // 4)
    idx_bits = int(math.log2(seq))
    assert 2 ** idx_bits == seq
    assert T == KEY_BITS * SUBLANES
    qspec = lambda rows: pl.BlockSpec((None, rows, T), lambda b, i: (b, 0, i))
    kspec = lambda a, c: pl.BlockSpec((None, nk, a, c), lambda b, i: (b, 0, 0, 0))
    return pl.pallas_call(
        functools.partial(_dsa_kernel, topk=topk, idx_bits=idx_bits),
        out_shape=jax.ShapeDtypeStruct((bsz, seq, A_HEADS * HEAD_DIM), MM_DTYPE),
        grid=(bsz, nk),
        in_specs=[qspec(A_HEADS * HEAD_DIM), qspec(IDX_HEADS * IDX_DIM), qspec(IDX_HEADS),
                  kspec(T, IDX_DIM), kspec(T, A_LATENT), kspec(A_LATENT, T),
                  _const_spec(bias_a.shape), _const_spec(wukT.shape), _const_spec(wuvT.shape)],
        out_specs=pl.BlockSpec((None, T, A_HEADS * HEAD_DIM), lambda b, i: (b, i, 0)),
        scratch_shapes=[pltpu.VMEM((nk, T, T), I32),
                        pltpu.VMEM((nk, KEY_BITS + 1, SUBLANES, T), I32),
                        pltpu.VMEM((A_HEADS, A_LATENT, T), MM_DTYPE),
                        pltpu.VMEM((A_HEADS, A_LATENT + SUM_ROWS, T), F32),
                        pltpu.VMEM((A_HEADS, T, T), F32),
                        pltpu.VMEM((A_HEADS * HEAD_DIM, T), F32)],
        compiler_params=_cparams(2), name="dsa_attention",
    )(aqT, iqT, iwT, kidx, kv, kvT, bias_a, wukT, wuvT)


def _diff_kernel(qT_ref, k_ref, vT_ref, bias_ref, lam_ref, gsub_ref, out_ref,
                 qz_ref, acc_ref, s_ref, oT_ref, *, lam_init):
    T = ATT_TILE
    dv = 2 * HEAD_DIM
    n_chain = 2 * B_HEADS
    qi = pl.program_id(1)
    half = lax.broadcasted_iota(I32, (dv, T), 0) < HEAD_DIM
    for h in range(B_HEADS):
        q = qT_ref[h * dv:(h + 1) * dv, :].astype(F32)
        qz_ref[2 * h] = jnp.where(half, q, 0.0).astype(MM_DTYPE)
        qz_ref[2 * h + 1] = jnp.where(half, 0.0, q).astype(MM_DTYPE)
    acc_ref[...] = jnp.zeros(acc_ref.shape, F32)

    def stage_logits(kj, c):
        h = c // 2
        off = jnp.minimum(qi - kj, N_OFFSETS - 1)
        s = bias_ref[h, off] + jnp.dot(k_ref[kj, :, h * dv:(h + 1) * dv], qz_ref[c],
                                       preferred_element_type=F32)
        s_ref[c] = s
        return jnp.max(s, axis=0, keepdims=True)

    tile_max0 = tuple(stage_logits(0, c) for c in range(n_chain))

    def step(kj, carry, last):
        ms, tile_max = carry
        new_m, new_max = [], []
        for c in range(n_chain):
            h = c // 2
            m_new = jnp.maximum(ms[c], tile_max[c])
            p = jnp.exp2(s_ref[c] - m_new).astype(MM_DTYPE)
            alpha = jnp.exp2(ms[c] - m_new)
            if not last:
                new_max.append(stage_logits(kj + 1, c))
            vT = _with_ones_row(vT_ref[kj, h * dv:(h + 1) * dv, :])
            acc_ref[c] = alpha * acc_ref[c] + jnp.dot(vT, p, preferred_element_type=F32)
            new_m.append(m_new)
        return tuple(new_m), tuple(new_max)

    carry = _loop_by_pairs(qi, lambda kj, cr: step(kj, cr, False),
                           (_initial_max(n_chain, T), tile_max0))
    step(qi, carry, True)

    lr = lam_ref[...]
    lam = (jnp.exp(jnp.sum(lr[0:1, :] * lr[1:2, :], axis=1, keepdims=True))
           - jnp.exp(jnp.sum(lr[2:3, :] * lr[3:4, :], axis=1, keepdims=True)) + lam_init)

    def normalised(c):
        return acc_ref[c, 0:dv, :] / acc_ref[c, dv:dv + 1, :]

    for h in range(B_HEADS):
        attn = normalised(2 * h) - lam * normalised(2 * h + 1)
        y = attn * lax.rsqrt(jnp.mean(attn * attn, axis=0, keepdims=True) + EPS)
        oT_ref[h * dv:(h + 1) * dv, :] = y * gsub_ref[...] * (1.0 - lam_init)
    out_ref[...] = oT_ref[...].T.astype(out_ref.dtype)


def _diff(bqT, bk, bvT, bias_b, lam_rows, gsub, lam_init):
    bsz, _, seq = bqT.shape
    T = ATT_TILE
    nk = seq // T
    dv = 2 * HEAD_DIM
    qspec = pl.BlockSpec((None, B_HEADS * dv, T), lambda b, i: (b, 0, i))
    return pl.pallas_call(
        functools.partial(_diff_kernel, lam_init=lam_init),
        out_shape=jax.ShapeDtypeStruct((bsz, seq, B_HEADS * dv), MM_DTYPE),
        grid=(bsz, nk),
        in_specs=[qspec,
                  pl.BlockSpec((None, nk, T, B_HEADS * dv), lambda b, i: (b, 0, 0, 0)),
                  pl.BlockSpec((None, nk, B_HEADS * dv, T), lambda b, i: (b, 0, 0, 0)),
                  _const_spec(bias_b.shape), _const_spec((4, HEAD_DIM)), _const_spec((dv, 1))],
        out_specs=pl.BlockSpec((None, T, B_HEADS * dv), lambda b, i: (b, i, 0)),
        scratch_shapes=[pltpu.VMEM((2 * B_HEADS, dv, T), MM_DTYPE),
                        pltpu.VMEM((2 * B_HEADS, dv + SUM_ROWS, T), F32),
                        pltpu.VMEM((2 * B_HEADS, T, T), F32),
                        pltpu.VMEM((B_HEADS * dv, T), F32)],
        compiler_params=_cparams(2), name="diff_attention",
    )(bqT, bk, bvT, bias_b, lam_rows, gsub)


def _dil_kernel(cur_ref, halo_ref, bias_ref, out_ref, lse_ref, *, tq):
    n = C_BAND
    wid = C_HPG * HEAD_DIM
    halo_lo = jnp.where(pl.program_id(2) == 0, jnp.int32(n), jnp.int32(0))
    i = lax.broadcasted_iota(I32, (n, 2 * n), 0)
    j = lax.broadcasted_iota(I32, (n, 2 * n), 1)
    lane_head = lax.broadcasted_iota(I32, (n, wid), 1) // HEAD_DIM
    in_head = [lane_head == h for h in range(C_HPG)]
    band = jnp.where(j >= i, jnp.where(j <= i + n, 0.0, NEG), NEG)
    band0 = jnp.where(j >= jnp.maximum(i, halo_lo), jnp.where(j <= i + n, 0.0, NEG), NEG)
    bias = [bias_ref[h] + band for h in range(C_HPG)]
    bias0 = [bias_ref[h] + band0 for h in range(C_HPG)]

    def band_rows(r, c, lo, hi):
        if c == 0:
            return jnp.concatenate([halo_ref[r, :, lo:hi], cur_ref[r, 0:n, lo:hi]], axis=0)
        return cur_ref[r, (c - 1) * n:(c + 1) * n, lo:hi]

    blocks = [(r, c) for r in range(cur_ref.shape[0]) for c in range(tq // n)]
    logits = []
    for r, c in blocks:
        q = cur_ref[r, c * n:(c + 1) * n, 0:wid].astype(F32) * ATTN_SCALE
        keys = band_rows(r, c, wid, 2 * wid)
        for h in range(C_HPG):
            qh = jnp.where(in_head[h], q, 0.0).astype(MM_DTYPE)
            s = lax.dot_general(qh, keys, (((1,), (1,)), ((), ())), preferred_element_type=F32)
            logits.append(s + (bias0 if c == 0 else bias)[h])
    for k, (r, c) in enumerate(blocks):
        vals = band_rows(r, c, 2 * wid, 3 * wid)
        out = jnp.zeros((n, wid), F32)
        lse = jnp.zeros((n, wid), F32)
        for h in range(C_HPG):
            s = logits[k * C_HPG + h]
            m = jnp.max(s, axis=1, keepdims=True)
            p = jnp.exp(s - m)
            den = jnp.sum(p, axis=1, keepdims=True)
            o = jnp.dot(p.astype(MM_DTYPE), vals, preferred_element_type=F32) * (1.0 / den)
            out = jnp.where(in_head[h], o, out)
            lse = jnp.where(in_head[h], m + jnp.log(den), lse)
        out_ref[r, c * n:(c + 1) * n, :] = out
        lse_ref[r, c * n:(c + 1) * n, :] = lse


def _dilated_group(cg, bias_g, g):
    bsz, dil, m, gw = cg.shape
    wid = C_HPG * HEAD_DIM
    n = C_BAND
    assert m % n == 0 and gw == 3 * wid
    tq = min(m, 512)
    n_res = max(1, min(dil, 512 // tq))
    cur = pl.BlockSpec((None, n_res, tq, gw), lambda b, r, i: (b, r, i, 0))
    halo = pl.BlockSpec((None, n_res, n, gw),
                        lambda b, r, i: (b, r, jnp.maximum(i * (tq // n) - 1, 0), 0))
    outspec = pl.BlockSpec((None, n_res, tq, wid), lambda b, r, i: (b, r, i, 0))
    return pl.pallas_call(
        functools.partial(_dil_kernel, tq=tq),
        out_shape=[jax.ShapeDtypeStruct((bsz, dil, m, wid), F32)] * 2,
        grid=(bsz, dil // n_res, m // tq),
        in_specs=[cur, halo, pl.BlockSpec((C_HPG, n, 2 * n), lambda b, r, i: (0, 0, 0))],
        out_specs=[outspec, outspec],
        compiler_params=_cparams(3), name=f"dilated_group{g}",
    )(cg, cg, bias_g)


def _merge_ffn_kernel(x_ref, mod_ref, g1_ref, g2_ref, gf_ref, oa_ref, ob_ref,
                      c0_ref, c1_ref, c2_ref, s0_ref, s1_ref, s2_ref,
                      wza, wzb, wzc, wba, wbb, wbc, wo, wgu, wd, out_ref, tok_ref, *, final_norm):
    x = x_ref[...]
    tm = x.shape[0]
    h = _rms(x) * g1_ref[...]
    h = h * (1.0 + mod_ref[1:2, :]) + mod_ref[0:1, :]
    hb = h.astype(MM_DTYPE)

    def token_order(k, ref):
        dil, _, w = ref.shape
        if dil == 1:
            return ref[0]
        n_chunk = w // LANES
        for r in range(dil):
            for j in range(n_chunk):
                tok_ref[k * n_chunk + j, pl.ds(r, tm // dil, stride=dil), :] = ref[
                    r, :, j * LANES:(j + 1) * LANES]
        return jnp.concatenate([tok_ref[k * n_chunk + j] for j in range(n_chunk)], axis=1)

    s0, s1, s2 = s0_ref[0], token_order(0, s1_ref), token_order(1, s2_ref)
    c0, c1, c2 = c0_ref[0], token_order(2, c1_ref), token_order(3, c2_ref)
    mx = jnp.maximum(jnp.maximum(s0, s1), s2)
    e0, e1, e2 = jnp.exp(s0 - mx), jnp.exp(s1 - mx), jnp.exp(s2 - mx)
    oc = (e0 * c0 + e1 * c1 + e2 * c2) / (e0 + e1 + e2)

    def gated(wz, o, wb):
        z = jnp.dot(hb, wz[...], preferred_element_type=F32)
        return jax.nn.sigmoid(z) * jnp.dot(o, wb[...], preferred_element_type=F32)

    merged = (gated(wza, oa_ref[...], wba) + gated(wzb, ob_ref[...], wbb)
              + gated(wzc, oc.astype(MM_DTYPE), wbc))
    y = jnp.dot(merged.astype(MM_DTYPE), wo[...], preferred_element_type=F32)
    x = x + mod_ref[2:3, :] * y

    h = _rms(x) * g2_ref[...]
    h = h * (1.0 + mod_ref[4:5, :]) + mod_ref[3:4, :]
    hb = h.astype(MM_DTYPE)
    acc = jnp.zeros(x.shape, F32)
    for c in range(D_FF // FFN_CHUNK):
        cols = slice(c * FFN_CHUNK, (c + 1) * FFN_CHUNK)
        fg = jnp.dot(hb, wgu[:, cols], preferred_element_type=F32)
        fu = jnp.dot(hb, wgu[:, D_FF + c * FFN_CHUNK:D_FF + (c + 1) * FFN_CHUNK],
                     preferred_element_type=F32)
        act = (fg * jax.nn.sigmoid(fg) * fu).astype(MM_DTYPE)
        acc = acc + jnp.dot(act, wd[cols, :], preferred_element_type=F32)
    y = x + mod_ref[5:6, :] * acc
    if final_norm:
        y = _rms(y) * gf_ref[...]
    out_ref[...] = y


def _merge_ffn(x2d, mod_l, g1, g2, gf, oa, ob, ocs, lses, ws, seq, final_norm):
    n, d = x2d.shape
    tm = ROW_TILE
    per_b = seq // tm
    row = lambda wd: pl.BlockSpec((tm, wd), lambda i: (i, 0))
    vec = _const_spec((1, d))
    wid = C_HPG * HEAD_DIM
    res = [pl.BlockSpec((None, dil, tm // dil, wid), lambda i: (i // per_b, 0, i % per_b, 0))
           for _, dil in C_GROUPS]
    in_specs = [row(d), pl.BlockSpec((None, 6, d), lambda i: (i // per_b, 0, 0)), vec, vec, vec,
                row(oa.shape[1]), row(ob.shape[1])] + res + res
    in_specs += [_const_spec(w.shape) for w in ws]
    return pl.pallas_call(
        functools.partial(_merge_ffn_kernel, final_norm=final_norm),
        out_shape=jax.ShapeDtypeStruct((n, d), F32), grid=(n // tm,),
        in_specs=in_specs, out_specs=row(d),
        scratch_shapes=[pltpu.VMEM((4 * wid // LANES, tm, LANES), F32)],
        compiler_params=_cparams(1), name="merge_ffn",
    )(x2d, mod_l, g1, g2, gf, oa, ob, *ocs, *lses, *ws)


def kernel(x, c, w_ada, b_ada, g_norm1, w_in, w_uk, w_uv, g_kv, lam_q1, lam_k1, lam_q2, lam_k2,
           g_subln, w_branch_a, w_branch_b, w_branch_c, w_out, g_norm2, w_gate_up, w_down,
           rel_bias, g_final):
    bsz, seq, d = x.shape
    depth = w_ada.shape[0]
    T = ATT_TILE
    nk = seq // T
    assert d == D_MODEL and seq % T == 0 and seq % ROW_TILE == 0
    n = bsz * seq
    cast = lambda w: w.astype(MM_DTYPE)

    thresholds = _bucket_thresholds(seq + 2 * C_BAND * C_GROUPS[-1][1])
    assert seq <= N_OFFSETS * T or (N_OFFSETS - 2) * T + 1 >= thresholds[-1]
    tab = rel_bias.reshape(-1)
    bias_a = _bias_att_tiles(tab, 0, A_HEADS, T, thresholds)
    bias_b = _bias_att_tiles(tab, A_HEADS, B_HEADS, T, thresholds)
    bias_c = _bias_dil_tiles(tab, A_HEADS + B_HEADS, thresholds)

    mod = _modulation(c, w_ada, b_ada).reshape(depth, bsz, 6, d)

    splits = (A_HEADS * HEAD_DIM, A_LATENT, IDX_HEADS * IDX_DIM, IDX_DIM, IDX_HEADS,
              B_HEADS * 2 * HEAD_DIM, B_HEADS * 2 * HEAD_DIM, B_HEADS * 2 * HEAD_DIM,
              C_HEADS * HEAD_DIM, C_HEADS * HEAD_DIM, C_HEADS * HEAD_DIM, d, d, d)
    offs = np.concatenate([[0], np.cumsum(splits)])
    seg = lambda w, k: w[:, int(offs[k]):int(offs[k + 1])]

    x2d = x.reshape(n, d)
    for l in range(depth):
        wl = w_in[l]
        castT = lambda w: w.T.astype(MM_DTYPE)
        wT_iw = jnp.pad(seg(wl, 4).T, ((0, 16 - IDX_HEADS), (0, 0)))
        wid = C_HPG * HEAD_DIM
        w_c = jnp.concatenate([seg(wl, k)[:, g * wid:(g + 1) * wid]
                               for g in range(len(C_GROUPS)) for k in (8, 9, 10)], axis=1)
        ws_in = [castT(seg(wl, 0)), castT(seg(wl, 2)), cast(wT_iw), castT(seg(wl, 5)), castT(seg(wl, 7)),
                 cast(seg(wl, 3)), cast(seg(wl, 1)), cast(seg(wl, 6)), cast(w_c)]
        g1 = g_norm1[l].reshape(1, d)
        (aqT, iqT, iwT, bqT, bvT, ik, kv, kvT, bk, cg0, cg1, cg2) = _in_proj(
            x2d, mod[l], g1, g_kv[l].reshape(1, A_LATENT), ws_in, bsz, seq)

        o_a = _dsa(aqT, iqT, iwT,
                   ik.reshape(bsz, nk, T, IDX_DIM), kv.reshape(bsz, nk, T, A_LATENT), kvT,
                   bias_a, cast(w_uk[l].transpose(0, 2, 1)), cast(w_uv[l].transpose(0, 2, 1)))
        o_a = o_a.reshape(n, -1)

        lam_init = 0.8 - 0.6 * math.exp(-0.3 * l)
        lam_rows = jnp.stack([lam_q1[l], lam_k1[l], lam_q2[l], lam_k2[l]])
        dv = 2 * HEAD_DIM
        o_b = _diff(bqT, bk.reshape(bsz, nk, T, B_HEADS * dv), bvT, bias_b, lam_rows,
                    g_subln[l].reshape(dv, 1), lam_init)
        o_b = o_b.reshape(n, -1)

        ocs, lses = [], []
        for g, (cg, (window, dil)) in enumerate(zip((cg0, cg1, cg2), C_GROUPS)):
            assert window // dil == C_BAND
            o, s = _dilated_group(cg, bias_c[g * C_HPG:(g + 1) * C_HPG], g)
            ocs.append(o)
            lses.append(s)

        ws_out = [cast(seg(wl, 11)), cast(seg(wl, 12)), cast(seg(wl, 13)),
                  cast(w_branch_a[l]), cast(w_branch_b[l]), cast(w_branch_c[l]), cast(w_out[l]),
                  cast(w_gate_up[l]), cast(w_down[l])]
        x2d = _merge_ffn(x2d, mod[l], g1, g_norm2[l].reshape(1, d), g_final.reshape(1, d),
                         o_a, o_b, ocs, lses, ws_out, seq, final_norm=(l == depth - 1))
    return x2d.reshape(bsz, seq, d)
```

```python
import functools
import math

import numpy as np
import jax
import jax.numpy as jnp
from jax import lax
from jax.experimental import pallas as pl
from jax.experimental.pallas import tpu as pltpu

D_MODEL = 1024
HEAD_DIM = 64
ATTN_SCALE = HEAD_DIM ** -0.5
LOG2E = math.log2(math.e)
A_HEADS = 8
A_LATENT = 128
IDX_HEADS = 8
IDX_DIM = 64
IDX_SCALE = (IDX_HEADS * IDX_DIM) ** -0.5
TOPK_MAX = 256
B_HEADS = 4
C_GROUPS = ((128, 1), (512, 4), (2048, 16))
C_HPG = 4
C_HEADS = C_HPG * len(C_GROUPS)
N_BUCKETS = 32
MAX_DISTANCE = 2048
N_BIAS_HEADS = A_HEADS + B_HEADS + C_HEADS
D_FF = -(-8 * D_MODEL // (3 * 256)) * 256
EPS = 1e-6

MM_DTYPE = jnp.bfloat16
F32 = jnp.float32
I32 = jnp.int32

ATT_TILE = 256
N_OFFSETS = 8
C_BAND = 128
ROW_TILE = 512
FFN_CHUNK = 256
MOD_COLS = 1536
NEG = -1e30
INT_MIN = -2 ** 31
LANES = 128
SUBLANES = 8
KEY_BITS = 32
VMEM_LIMIT = 56 * 1024 * 1024


def _cparams(n_axes, vmem=VMEM_LIMIT):
    return pltpu.CompilerParams(dimension_semantics=("arbitrary",) * n_axes,
                                vmem_limit_bytes=vmem)


def _const_spec(shape):
    nd = len(shape)
    return pl.BlockSpec(shape, lambda *_: (0,) * nd, pipeline_mode=pl.Buffered(1))


def _bucket_thresholds(max_dist):
    n = np.arange(max_dist + 1)
    max_exact = N_BUCKETS // 2
    nf = np.maximum(n, 1).astype(np.float32)
    large = max_exact + (np.log(nf / np.float32(max_exact))
                         / np.float32(math.log(MAX_DISTANCE / max_exact))
                         * np.float32(N_BUCKETS - max_exact)).astype(np.int32)
    large = np.minimum(large, N_BUCKETS - 1)
    bucket = np.where(n < max_exact, n, large)
    assert np.all(np.diff(bucket) >= 0)
    thr = []
    for k in range(1, N_BUCKETS):
        idx = np.nonzero(bucket >= k)[0]
        thr.append(int(idx[0]) if idx.size else None)
    return thr


def _bias_from_dist(dist, tab_ref, col, thresholds, lo=0, hi=None):
    reached = [k for k, thr in enumerate(thresholds, start=1) if thr is not None]
    base = max([0] + [k for k in reached if thresholds[k - 1] <= lo])
    b = jnp.full(dist.shape, tab_ref[base * N_BIAS_HEADS + col], F32)
    for k in reached:
        thr = thresholds[k - 1]
        if thr > lo and (hi is None or thr <= hi):
            b = jnp.where(dist >= thr, tab_ref[k * N_BIAS_HEADS + col], b)
    return b


def _bias_att_kernel(tab_ref, out_ref, *, head0, thresholds):
    h = pl.program_id(0)
    n_off, tile, _ = out_ref.shape
    row = lax.broadcasted_iota(I32, (tile, tile), 0)
    colq = lax.broadcasted_iota(I32, (tile, tile), 1)
    for o in range(n_off):
        dist = jnp.maximum(o * tile + colq - row, 0)
        lo, hi = max(o * tile - (tile - 1), 0), o * tile + tile - 1
        b = _bias_from_dist(dist, tab_ref, head0 + h, thresholds, lo, hi) * LOG2E
        if o == 0:
            b = jnp.where(row <= colq, b, NEG)
        out_ref[o] = b


def _bias_att_tiles(tab, head0, n_heads, tile, thresholds):
    return pl.pallas_call(
        functools.partial(_bias_att_kernel, head0=head0, thresholds=thresholds),
        out_shape=jax.ShapeDtypeStruct((n_heads, N_OFFSETS, tile, tile), F32),
        grid=(n_heads,),
        in_specs=[pl.BlockSpec(memory_space=pltpu.SMEM)],
        out_specs=pl.BlockSpec((None, N_OFFSETS, tile, tile), lambda h: (h, 0, 0, 0)),
        compiler_params=_cparams(1),
        name="bias_att_tiles",
    )(tab)


def _bias_dil_kernel(tab_ref, out_ref, *, head0, thresholds):
    h = pl.program_id(0)
    g = h // C_HPG
    dil = jnp.where(g == 0, C_GROUPS[0][1], jnp.where(g == 1, C_GROUPS[1][1], C_GROUPS[2][1]))
    i = lax.broadcasted_iota(I32, (C_BAND, 2 * C_BAND), 0)
    j = lax.broadcasted_iota(I32, (C_BAND, 2 * C_BAND), 1)
    dist = jnp.maximum((i - j + C_BAND) * dil, 0)
    out_ref[...] = _bias_from_dist(dist, tab_ref, head0 + h, thresholds)


def _bias_dil_tiles(tab, head0, thresholds):
    return pl.pallas_call(
        functools.partial(_bias_dil_kernel, head0=head0, thresholds=thresholds),
        out_shape=jax.ShapeDtypeStruct((C_HEADS, C_BAND, 2 * C_BAND), F32),
        grid=(C_HEADS,),
        in_specs=[pl.BlockSpec(memory_space=pltpu.SMEM)],
        out_specs=pl.BlockSpec((None, C_BAND, 2 * C_BAND), lambda h: (h, 0, 0)),
        compiler_params=_cparams(1),
        name="bias_dil_tiles",
    )(tab)


def _mod_kernel(c_ref, w_ref, b_ref, out_ref):
    c = c_ref[...]
    ca = (c * jax.nn.sigmoid(c)).astype(MM_DTYPE)
    out_ref[...] = jnp.dot(ca, w_ref[...].astype(MM_DTYPE), preferred_element_type=F32) + b_ref[...]


def _modulation(c, w_ada, b_ada):
    depth, d, wid = w_ada.shape
    bsz = c.shape[0]
    tn = MOD_COLS
    assert wid % tn == 0
    return pl.pallas_call(
        _mod_kernel,
        out_shape=jax.ShapeDtypeStruct((depth, bsz, wid), F32),
        grid=(depth, wid // tn),
        in_specs=[pl.BlockSpec((bsz, d), lambda l, j: (0, 0)),
                  pl.BlockSpec((None, d, tn), lambda l, j: (l, 0, j)),
                  pl.BlockSpec((None, 1, tn), lambda l, j: (l, 0, j))],
        out_specs=pl.BlockSpec((None, bsz, tn), lambda l, j: (l, 0, j)),
        compiler_params=_cparams(2),
        name="adaln_modulation",
    )(c, w_ada, b_ada.reshape(depth, 1, wid))


def _rms(x):
    return x * lax.rsqrt(jnp.mean(x * x, axis=-1, keepdims=True) + EPS)


def _in_kernel(x_ref, mod_ref, g1_ref, gkv_ref,
               wT_aq, wT_iq, wT_iw, wT_bq, wT_bv, w_ik, w_kv, w_bk, w_c,
               o_aqT, o_iqT, o_iwT, o_bqT, o_bvT, o_ik, o_kv, o_kvT, o_bk, o_c0, o_c1, o_c2,
               c_scr):
    T = ATT_TILE
    tm = x_ref.shape[0]
    h = _rms(x_ref[...]) * g1_ref[...]
    h = h * (1.0 + mod_ref[1:2, :]) + mod_ref[0:1, :]
    hb = h.astype(MM_DTYPE)

    def mm(w):
        return jnp.dot(hb, w[...], preferred_element_type=F32)

    def mm_t(wT):
        return lax.dot_general(wT[...], hb, (((1,), (1,)), ((), ())), preferred_element_type=F32)

    o_aqT[...] = mm_t(wT_aq).astype(o_aqT.dtype)
    o_iqT[...] = mm_t(wT_iq).astype(o_iqT.dtype)
    o_bqT[...] = (mm_t(wT_bq) * (ATTN_SCALE * LOG2E)).astype(o_bqT.dtype)
    o_iwT[...] = (mm_t(wT_iw) * IDX_SCALE)[:IDX_HEADS]
    bvT = mm_t(wT_bv).astype(o_bvT.dtype)
    kv = _rms(mm(w_kv)) * gkv_ref[...]
    kvT = kv.T.astype(o_kvT.dtype)
    for j in range(tm // T):
        o_bvT[j] = bvT[:, j * T:(j + 1) * T]
        o_kvT[j] = kvT[:, j * T:(j + 1) * T]
    o_kv[...] = kv.astype(o_kv.dtype)
    o_ik[...] = mm(w_ik).astype(o_ik.dtype)
    o_bk[...] = mm(w_bk).astype(o_bk.dtype)

    yc = mm(w_c)
    n_chunk = yc.shape[1] // LANES
    for j in range(n_chunk):
        c_scr[j] = yc[:, j * LANES:(j + 1) * LANES]
    per_group = n_chunk // len(C_GROUPS)
    for g, o_c in enumerate((o_c0, o_c1, o_c2)):
        dil = C_GROUPS[g][1]
        for r in range(dil):
            for jj in range(per_group):
                o_c[r, :, jj * LANES:(jj + 1) * LANES] = c_scr[
                    g * per_group + jj, pl.ds(r, tm // dil, stride=dil), :].astype(o_c.dtype)


def _in_proj(x2d, mod_l, g1, gkv, ws, bsz, seq):
    n, d = x2d.shape
    tm = ROW_TILE
    T = ATT_TILE
    per_b = seq // tm
    nk = seq // T
    hd = A_HEADS * HEAD_DIM
    bw = B_HEADS * 2 * HEAD_DIM
    gw = 3 * C_HPG * HEAD_DIM
    in_specs = [pl.BlockSpec((tm, d), lambda i: (i, 0)),
                pl.BlockSpec((None, 6, d), lambda i: (i // per_b, 0, 0)),
                _const_spec((1, d)), _const_spec((1, A_LATENT))]
    in_specs += [_const_spec(w.shape) for w in ws]

    def tspec(rows):
        return pl.BlockSpec((None, rows, tm), lambda i: (i // per_b, 0, i % per_b))

    def tile_tspec(rows):
        return pl.BlockSpec((None, tm // T, rows, T), lambda i: (i // per_b, i % per_b, 0, 0))

    def rspec(wd):
        return pl.BlockSpec((tm, wd), lambda i: (i, 0))

    def cspec(dil):
        return pl.BlockSpec((None, dil, tm // dil, gw), lambda i: (i // per_b, 0, i % per_b, 0))

    sds = jax.ShapeDtypeStruct
    out_specs = [tspec(hd), tspec(IDX_HEADS * IDX_DIM), tspec(IDX_HEADS), tspec(bw),
                 tile_tspec(bw), rspec(IDX_DIM), rspec(A_LATENT), tile_tspec(A_LATENT), rspec(bw)]
    out_shape = [sds((bsz, hd, seq), MM_DTYPE), sds((bsz, IDX_HEADS * IDX_DIM, seq), MM_DTYPE),
                 sds((bsz, IDX_HEADS, seq), F32), sds((bsz, bw, seq), MM_DTYPE),
                 sds((bsz, nk, bw, T), MM_DTYPE), sds((n, IDX_DIM), MM_DTYPE),
                 sds((n, A_LATENT), MM_DTYPE), sds((bsz, nk, A_LATENT, T), MM_DTYPE),
                 sds((n, bw), MM_DTYPE)]
    for _, dil in C_GROUPS:
        out_specs.append(cspec(dil))
        out_shape.append(sds((bsz, dil, seq // dil, gw), MM_DTYPE))
    return pl.pallas_call(
        _in_kernel, out_shape=out_shape, grid=(n // tm,),
        in_specs=in_specs, out_specs=out_specs,
        scratch_shapes=[pltpu.VMEM((3 * gw // LANES, tm, LANES), F32)],
        compiler_params=_cparams(1), name="in_proj",
    )(x2d, mod_l, g1, gkv, *ws)


def _initial_max(n_chains, tile):
    return tuple(jnp.full((1, tile), NEG, F32) for _ in range(n_chains))


def _loop_grouped(n, body, init, log2_group):
    carry, done = init, jnp.int32(0)
    for lg in range(log2_group, -1, -1):
        group = 1 << lg
        trips = lax.shift_right_logical(n - done, jnp.int32(lg))

        def grouped(i, c, group=group, done=done):
            for j in range(group):
                c = body(done + group * i + j, c)
            return c

        carry = lax.fori_loop(0, trips, grouped, carry)
        done = done + group * trips
    return carry


SUM_ROWS = 16


def _with_ones_row(vT):
    row = lax.broadcasted_iota(I32, (SUM_ROWS, vT.shape[1]), 0)
    ones = jnp.where(row == 0, 1.0, 0.0).astype(vT.dtype)
    return jnp.concatenate([vT, ones], axis=0)


def _bit_transpose32(words):
    a = list(words)
    j, mask = 16, 0x0000FFFF
    while j:
        k = 0
        while k < 32:
            t = (a[k] ^ lax.shift_right_logical(a[k + j], jnp.int32(j))) & jnp.int32(mask)
            a[k] = a[k] ^ t
            a[k + j] = a[k + j] ^ lax.shift_left(t, jnp.int32(j))
            k = (k + j + 1) & ~j
        j >>= 1
        mask = (mask ^ (mask << j)) & 0xFFFFFFFF
    return a


def _dsa_kernel(aqT_ref, iqT_ref, iwT_ref, kidx_ref, kv_ref, kvT_ref, bias_ref, wukT_ref, wuvT_ref,
                out_ref, keys_ref, planes_ref, qlat_ref, acc_ref, s_ref, oT_ref, *, topk, idx_bits):
    T = ATT_TILE
    qi = pl.program_id(1)
    nk = qi + 1

    for h in range(A_HEADS):
        q = jnp.dot(wukT_ref[h], aqT_ref[h * HEAD_DIM:(h + 1) * HEAD_DIM, :],
                    preferred_element_type=F32) * (ATTN_SCALE * LOG2E)
        qlat_ref[h] = q.astype(qlat_ref.dtype)

    s_loc = lax.broadcasted_iota(I32, (T, T), 0)
    t_loc = lax.broadcasted_iota(I32, (T, T), 1)

    def score_tile(kj, diagonal):
        kt = kidx_ref[kj]
        acc = jnp.zeros((T, T), F32)
        for h in range(IDX_HEADS):
            s = jnp.dot(kt, iqT_ref[h * IDX_DIM:(h + 1) * IDX_DIM, :], preferred_element_type=F32)
            acc = acc + jnp.maximum(s, 0.0) * iwT_ref[h:h + 1, :]
        bits = lax.bitcast_convert_type(acc, I32)
        key = jnp.where(bits < 0, bits ^ jnp.int32(0x7FFFFFFF), bits)
        if diagonal:
            key = jnp.where(s_loc <= t_loc, key, jnp.int32(INT_MIN))
        keys_ref[kj] = key
        planes = _bit_transpose32([key[SUBLANES * i:SUBLANES * (i + 1), :] ^ jnp.int32(INT_MIN)
                                   for i in range(KEY_BITS)])
        for b in range(KEY_BITS):
            planes_ref[kj, b] = planes[b]

    def off_diagonal(kj, carry):
        score_tile(kj, False)
        return carry

    _loop_grouped(qi, off_diagonal, 0, log2_group=2)
    score_tile(qi, True)

    n_tiles = planes_ref.shape[0]

    @pl.when(qi == 0)
    def _():
        def clear_planes(kj, carry):
            for b in range(KEY_BITS):
                planes_ref[kj, b] = jnp.zeros((SUBLANES, T), I32)
            return carry

        lax.fori_loop(1, n_tiles, clear_planes, 0)

    one = jnp.int32(1)
    nil = jnp.int32(0)
    zero = jnp.zeros((1, T), I32)

    n_rows = n_tiles * SUBLANES
    tile_of_row = lax.broadcasted_iota(I32, (n_rows, T), 0) // SUBLANES
    tied0 = jnp.where(tile_of_row < nk, jnp.int32(-1), nil)

    def bit_step(b, carry):
        tied, n_gt, kth_u = carry
        ones = tied & planes_ref[:, b].reshape(n_rows, T)
        n1 = jnp.sum(lax.population_count(ones), axis=0, keepdims=True)
        take = (n_gt + n1) >= topk
        tied = jnp.where(take, ones, tied ^ ones)
        n_gt = jnp.where(take, n_gt, n_gt + n1)
        kth_u = jnp.where(take, kth_u | lax.shift_left(one, jnp.int32(KEY_BITS - 1) - b), kth_u)
        return tied, n_gt, kth_u

    tied, n_gt, kth_u = lax.fori_loop(0, KEY_BITS, bit_step, (tied0, zero, zero))
    n_eq = jnp.sum(lax.population_count(tied), axis=0, keepdims=True)
    n_ge = jnp.where(kth_u == nil, n_gt, n_gt + n_eq)
    kth = jnp.maximum(kth_u ^ jnp.int32(INT_MIN), jnp.int32(INT_MIN + 1))

    @pl.when(jnp.max(n_ge) > topk)
    def _():
        need = topk - n_gt
        r_bits = SUBLANES.bit_length() - 1
        t_bits = T.bit_length() - 1
        word_row = lax.broadcasted_iota(I32, (n_rows, T), 0)
        word_tile = lax.shift_right_logical(word_row, jnp.int32(r_bits))
        word_r = word_row & jnp.int32(SUBLANES - 1)
        tied_count = lax.population_count(tied)

        def tied_before(pos):
            p_tile = lax.shift_right_logical(pos, jnp.int32(t_bits))
            p_i = lax.shift_right_logical(pos, jnp.int32(r_bits)) & jnp.int32(KEY_BITS - 1)
            p_r = pos & jnp.int32(SUBLANES - 1)
            above = jnp.where(p_i == nil, nil,
                              lax.shift_left(jnp.full_like(p_i, -1), jnp.int32(KEY_BITS) - p_i))
            at_i = lax.shift_right_logical(
                tied, jnp.broadcast_to(jnp.int32(KEY_BITS - 1) - p_i, tied.shape)) & one
            inside = lax.population_count(tied & above) + jnp.where(word_r < p_r, at_i, nil)
            hit = jnp.where(word_tile < p_tile, tied_count,
                            jnp.where(word_tile == p_tile, inside, nil))
            return jnp.sum(hit, axis=0, keepdims=True)

        def pos_step(i, cut):
            cand = cut | jnp.left_shift(one, jnp.int32(idx_bits - 1) - i)
            return jnp.where(tied_before(cand) < need, cand, cut)

        cut = lax.fori_loop(0, idx_bits, pos_step, zero)

        def demote(kj, carry):
            kk = keys_ref[kj]
            lowered = jnp.where((kj * T + s_loc) > cut, kth - one, kk)
            keys_ref[kj] = jnp.where(kk == kth, lowered, kk)
            return carry

        lax.fori_loop(0, nk, demote, 0)

    acc_ref[...] = jnp.zeros(acc_ref.shape, F32)

    def stage_logits(kj):
        unselected = jnp.where(keys_ref[kj] >= kth, 0.0, NEG)
        off = jnp.minimum(qi - kj, N_OFFSETS - 1)
        kvt = kv_ref[kj]

        def one_head(h):
            s = bias_ref[h, off] + jnp.dot(kvt, qlat_ref[h], preferred_element_type=F32)
            s = s + unselected
            s_ref[h] = s
            return jnp.max(s, axis=0, keepdims=True)

        return one_head

    first = stage_logits(0)
    tile_max0 = tuple(first(h) for h in range(A_HEADS))

    def attend(kj, carry):
        ms, tile_max = carry
        kvTt = _with_ones_row(kvT_ref[kj])
        stage_next = stage_logits(jnp.minimum(kj + 1, nk - 1))
        new_m, new_max = [], []
        for h in range(A_HEADS):
            m_new = jnp.maximum(ms[h], tile_max[h])
            p = jnp.exp2(s_ref[h] - m_new).astype(MM_DTYPE)
            alpha = jnp.exp2(ms[h] - m_new)
            new_max.append(stage_next(h))
            acc_ref[h] = alpha * acc_ref[h] + jnp.dot(kvTt, p, preferred_element_type=F32)
            new_m.append(m_new)
        return tuple(new_m), tuple(new_max)

    _loop_grouped(nk, attend, (_initial_max(A_HEADS, T), tile_max0), log2_group=1)

    for h in range(A_HEADS):
        o_lat = (acc_ref[h, 0:A_LATENT, :] / acc_ref[h, A_LATENT:A_LATENT + 1, :]).astype(MM_DTYPE)
        oT_ref[h * HEAD_DIM:(h + 1) * HEAD_DIM, :] = jnp.dot(
            wuvT_ref[h], o_lat, preferred_element_type=F32)
    out_ref[...] = oT_ref[...].T.astype(out_ref.dtype)


def _dsa(aqT, iqT, iwT, kidx, kv, kvT, bias_a, wukT, wuvT):
    bsz, _, seq = aqT.shape
    T = ATT_TILE
    nk = seq // T
    topk = min(TOPK_MAX, seq // 4)
    idx_bits = int(math.log2(seq))
    assert 2 ** idx_bits == seq
    assert T == KEY_BITS * SUBLANES
    qspec = lambda rows: pl.BlockSpec((None, rows, T), lambda b, i: (b, 0, i))
    kspec = lambda a, c: pl.BlockSpec((None, nk, a, c), lambda b, i: (b, 0, 0, 0))
    return pl.pallas_call(
        functools.partial(_dsa_kernel, topk=topk, idx_bits=idx_bits),
        out_shape=jax.ShapeDtypeStruct((bsz, seq, A_HEADS * HEAD_DIM), MM_DTYPE),
        grid=(bsz, nk),
        in_specs=[qspec(A_HEADS * HEAD_DIM), qspec(IDX_HEADS * IDX_DIM), qspec(IDX_HEADS),
                  kspec(T, IDX_DIM), kspec(T, A_LATENT), kspec(A_LATENT, T),
                  _const_spec(bias_a.shape), _const_spec(wukT.shape), _const_spec(wuvT.shape)],
        out_specs=pl.BlockSpec((None, T, A_HEADS * HEAD_DIM), lambda b, i: (b, i, 0)),
        scratch_shapes=[pltpu.VMEM((nk, T, T), I32),
                        pltpu.VMEM((nk, KEY_BITS + 1, SUBLANES, T), I32),
                        pltpu.VMEM((A_HEADS, A_LATENT, T), MM_DTYPE),
                        pltpu.VMEM((A_HEADS, A_LATENT + SUM_ROWS, T), F32),
                        pltpu.VMEM((A_HEADS, T, T), F32),
                        pltpu.VMEM((A_HEADS * HEAD_DIM, T), F32)],
        compiler_params=_cparams(2), name="dsa_attention",
    )(aqT, iqT, iwT, kidx, kv, kvT, bias_a, wukT, wuvT)


def _diff_kernel(qT_ref, k_ref, vT_ref, bias_ref, lam_ref, gsub_ref, out_ref,
                 qz_ref, acc_ref, s_ref, oT_ref, *, lam_init):
    T = ATT_TILE
    dv = 2 * HEAD_DIM
    n_chain = 2 * B_HEADS
    qi = pl.program_id(1)
    half = lax.broadcasted_iota(I32, (dv, T), 0) < HEAD_DIM
    for h in range(B_HEADS):
        q = qT_ref[h * dv:(h + 1) * dv, :].astype(F32)
        qz_ref[2 * h] = jnp.where(half, q, 0.0).astype(MM_DTYPE)
        qz_ref[2 * h + 1] = jnp.where(half, 0.0, q).astype(MM_DTYPE)
    acc_ref[...] = jnp.zeros(acc_ref.shape, F32)

    def stage_logits(kj, c):
        h = c // 2
        off = jnp.minimum(qi - kj, N_OFFSETS - 1)
        s = bias_ref[h, off] + jnp.dot(k_ref[kj, :, h * dv:(h + 1) * dv], qz_ref[c],
                                       preferred_element_type=F32)
        s_ref[c] = s
        return jnp.max(s, axis=0, keepdims=True)

    tile_max0 = tuple(stage_logits(0, c) for c in range(n_chain))

    def step(kj, carry, last):
        ms, tile_max = carry
        new_m, new_max = [], []
        for c in range(n_chain):
            h = c // 2
            m_new = jnp.maximum(ms[c], tile_max[c])
            p = jnp.exp2(s_ref[c] - m_new).astype(MM_DTYPE)
            alpha = jnp.exp2(ms[c] - m_new)
            if not last:
                new_max.append(stage_logits(kj + 1, c))
            vT = _with_ones_row(vT_ref[kj, h * dv:(h + 1) * dv, :])
            acc_ref[c] = alpha * acc_ref[c] + jnp.dot(vT, p, preferred_element_type=F32)
            new_m.append(m_new)
        return tuple(new_m), tuple(new_max)

    carry = _loop_grouped(qi, lambda kj, cr: step(kj, cr, False),
                          (_initial_max(n_chain, T), tile_max0), log2_group=2)
    step(qi, carry, True)

    lr = lam_ref[...]
    lam = (jnp.exp(jnp.sum(lr[0:1, :] * lr[1:2, :], axis=1, keepdims=True))
           - jnp.exp(jnp.sum(lr[2:3, :] * lr[3:4, :], axis=1, keepdims=True)) + lam_init)

    def normalised(c):
        return acc_ref[c, 0:dv, :] / acc_ref[c, dv:dv + 1, :]

    for h in range(B_HEADS):
        attn = normalised(2 * h) - lam * normalised(2 * h + 1)
        y = attn * lax.rsqrt(jnp.mean(attn * attn, axis=0, keepdims=True) + EPS)
        oT_ref[h * dv:(h + 1) * dv, :] = y * gsub_ref[...] * (1.0 - lam_init)
    out_ref[...] = oT_ref[...].T.astype(out_ref.dtype)


def _diff(bqT, bk, bvT, bias_b, lam_rows, gsub, lam_init):
    bsz, _, seq = bqT.shape
    T = ATT_TILE
    nk = seq // T
    dv = 2 * HEAD_DIM
    qspec = pl.BlockSpec((None, B_HEADS * dv, T), lambda b, i: (b, 0, i))
    return pl.pallas_call(
        functools.partial(_diff_kernel, lam_init=lam_init),
        out_shape=jax.ShapeDtypeStruct((bsz, seq, B_HEADS * dv), MM_DTYPE),
        grid=(bsz, nk),
        in_specs=[qspec,
                  pl.BlockSpec((None, nk, T, B_HEADS * dv), lambda b, i: (b, 0, 0, 0)),
                  pl.BlockSpec((None, nk, B_HEADS * dv, T), lambda b, i: (b, 0, 0, 0)),
                  _const_spec(bias_b.shape), _const_spec((4, HEAD_DIM)), _const_spec((dv, 1))],
        out_specs=pl.BlockSpec((None, T, B_HEADS * dv), lambda b, i: (b, i, 0)),
        scratch_shapes=[pltpu.VMEM((2 * B_HEADS, dv, T), MM_DTYPE),
                        pltpu.VMEM((2 * B_HEADS, dv + SUM_ROWS, T), F32),
                        pltpu.VMEM((2 * B_HEADS, T, T), F32),
                        pltpu.VMEM((B_HEADS * dv, T), F32)],
        compiler_params=_cparams(2), name="diff_attention",
    )(bqT, bk, bvT, bias_b, lam_rows, gsub)


def _dil_kernel(cur_ref, halo_ref, bias_ref, out_ref, lse_ref, *, tq):
    n = C_BAND
    wid = C_HPG * HEAD_DIM
    halo_lo = jnp.where(pl.program_id(2) == 0, jnp.int32(n), jnp.int32(0))
    i = lax.broadcasted_iota(I32, (n, 2 * n), 0)
    j = lax.broadcasted_iota(I32, (n, 2 * n), 1)
    lane_head = lax.broadcasted_iota(I32, (n, wid), 1) // HEAD_DIM
    in_head = [lane_head == h for h in range(C_HPG)]
    band = jnp.where(j >= i, jnp.where(j <= i + n, 0.0, NEG), NEG)
    band0 = jnp.where(j >= jnp.maximum(i, halo_lo), jnp.where(j <= i + n, 0.0, NEG), NEG)
    bias = [bias_ref[h] + band for h in range(C_HPG)]
    bias0 = [bias_ref[h] + band0 for h in range(C_HPG)]

    def band_rows(r, c, lo, hi):
        if c == 0:
            return jnp.concatenate([halo_ref[r, :, lo:hi], cur_ref[r, 0:n, lo:hi]], axis=0)
        return cur_ref[r, (c - 1) * n:(c + 1) * n, lo:hi]

    blocks = [(r, c) for r in range(cur_ref.shape[0]) for c in range(tq // n)]
    logits = []
    for r, c in blocks:
        q = cur_ref[r, c * n:(c + 1) * n, 0:wid].astype(F32) * ATTN_SCALE
        keys = band_rows(r, c, wid, 2 * wid)
        for h in range(C_HPG):
            qh = jnp.where(in_head[h], q, 0.0).astype(MM_DTYPE)
            s = lax.dot_general(qh, keys, (((1,), (1,)), ((), ())), preferred_element_type=F32)
            logits.append(s + (bias0 if c == 0 else bias)[h])
    for k, (r, c) in enumerate(blocks):
        vals = band_rows(r, c, 2 * wid, 3 * wid)
        out = jnp.zeros((n, wid), F32)
        lse = jnp.zeros((n, wid), F32)
        for h in range(C_HPG):
            s = logits[k * C_HPG + h]
            m = jnp.max(s, axis=1, keepdims=True)
            p = jnp.exp(s - m)
            den = jnp.sum(p, axis=1, keepdims=True)
            o = jnp.dot(p.astype(MM_DTYPE), vals, preferred_element_type=F32) * (1.0 / den)
            out = jnp.where(in_head[h], o, out)
            lse = jnp.where(in_head[h], m + jnp.log(den), lse)
        out_ref[r, c * n:(c + 1) * n, :] = out
        lse_ref[r, c * n:(c + 1) * n, :] = lse


def _dilated_group(cg, bias_g, g):
    bsz, dil, m, gw = cg.shape
    wid = C_HPG * HEAD_DIM
    n = C_BAND
    assert m % n == 0 and gw == 3 * wid
    tq = min(m, 512)
    n_res = max(1, min(dil, 512 // tq))
    cur = pl.BlockSpec((None, n_res, tq, gw), lambda b, r, i: (b, r, i, 0))
    halo = pl.BlockSpec((None, n_res, n, gw),
                        lambda b, r, i: (b, r, jnp.maximum(i * (tq // n) - 1, 0), 0))
    outspec = pl.BlockSpec((None, n_res, tq, wid), lambda b, r, i: (b, r, i, 0))
    return pl.pallas_call(
        functools.partial(_dil_kernel, tq=tq),
        out_shape=[jax.ShapeDtypeStruct((bsz, dil, m, wid), F32)] * 2,
        grid=(bsz, dil // n_res, m // tq),
        in_specs=[cur, halo, pl.BlockSpec((C_HPG, n, 2 * n), lambda b, r, i: (0, 0, 0))],
        out_specs=[outspec, outspec],
        compiler_params=_cparams(3), name=f"dilated_group{g}",
    )(cg, cg, bias_g)


def _merge_ffn_kernel(x_ref, mod_ref, g1_ref, g2_ref, gf_ref, oa_ref, ob_ref,
                      c0_ref, c1_ref, c2_ref, s0_ref, s1_ref, s2_ref,
                      wza, wzb, wzc, wba, wbb, wbc, wo, wgu, wd, out_ref, tok_ref, *, final_norm):
    x = x_ref[...]
    tm = x.shape[0]
    h = _rms(x) * g1_ref[...]
    h = h * (1.0 + mod_ref[1:2, :]) + mod_ref[0:1, :]
    hb = h.astype(MM_DTYPE)

    def token_order(k, ref):
        dil, _, w = ref.shape
        if dil == 1:
            return ref[0]
        n_chunk = w // LANES
        for r in range(dil):
            for j in range(n_chunk):
                tok_ref[k * n_chunk + j, pl.ds(r, tm // dil, stride=dil), :] = ref[
                    r, :, j * LANES:(j + 1) * LANES]
        return jnp.concatenate([tok_ref[k * n_chunk + j] for j in range(n_chunk)], axis=1)

    s0, s1, s2 = s0_ref[0], token_order(0, s1_ref), token_order(1, s2_ref)
    c0, c1, c2 = c0_ref[0], token_order(2, c1_ref), token_order(3, c2_ref)
    mx = jnp.maximum(jnp.maximum(s0, s1), s2)
    e0, e1, e2 = jnp.exp(s0 - mx), jnp.exp(s1 - mx), jnp.exp(s2 - mx)
    oc = (e0 * c0 + e1 * c1 + e2 * c2) / (e0 + e1 + e2)

    def gated(wz, o, wb):
        z = jnp.dot(hb, wz[...], preferred_element_type=F32)
        return jax.nn.sigmoid(z) * jnp.dot(o, wb[...], preferred_element_type=F32)

    merged = (gated(wza, oa_ref[...], wba) + gated(wzb, ob_ref[...], wbb)
              + gated(wzc, oc.astype(MM_DTYPE), wbc))
    y = jnp.dot(merged.astype(MM_DTYPE), wo[...], preferred_element_type=F32)
    x = x + mod_ref[2:3, :] * y

    h = _rms(x) * g2_ref[...]
    h = h * (1.0 + mod_ref[4:5, :]) + mod_ref[3:4, :]
    hb = h.astype(MM_DTYPE)
    acc = jnp.zeros(x.shape, F32)
    for c in range(D_FF // FFN_CHUNK):
        cols = slice(c * FFN_CHUNK, (c + 1) * FFN_CHUNK)
        fg = jnp.dot(hb, wgu[:, cols], preferred_element_type=F32)
        fu = jnp.dot(hb, wgu[:, D_FF + c * FFN_CHUNK:D_FF + (c + 1) * FFN_CHUNK],
                     preferred_element_type=F32)
        act = (fg * jax.nn.sigmoid(fg) * fu).astype(MM_DTYPE)
        acc = acc + jnp.dot(act, wd[cols, :], preferred_element_type=F32)
    y = x + mod_ref[5:6, :] * acc
    if final_norm:
        y = _rms(y) * gf_ref[...]
    out_ref[...] = y


def _merge_ffn(x2d, mod_l, g1, g2, gf, oa, ob, ocs, lses, ws, seq, final_norm):
    n, d = x2d.shape
    tm = ROW_TILE
    per_b = seq // tm
    row = lambda wd: pl.BlockSpec((tm, wd), lambda i: (i, 0))
    vec = _const_spec((1, d))
    wid = C_HPG * HEAD_DIM
    res = [pl.BlockSpec((None, dil, tm // dil, wid), lambda i: (i // per_b, 0, i % per_b, 0))
           for _, dil in C_GROUPS]
    in_specs = [row(d), pl.BlockSpec((None, 6, d), lambda i: (i // per_b, 0, 0)), vec, vec, vec,
                row(oa.shape[1]), row(ob.shape[1])] + res + res
    in_specs += [_const_spec(w.shape) for w in ws]
    return pl.pallas_call(
        functools.partial(_merge_ffn_kernel, final_norm=final_norm),
        out_shape=jax.ShapeDtypeStruct((n, d), F32), grid=(n // tm,),
        in_specs=in_specs, out_specs=row(d),
        scratch_shapes=[pltpu.VMEM((4 * wid // LANES, tm, LANES), F32)],
        compiler_params=_cparams(1), name="merge_ffn",
    )(x2d, mod_l, g1, g2, gf, oa, ob, *ocs, *lses, *ws)


def kernel(x, c, w_ada, b_ada, g_norm1, w_in, w_uk, w_uv, g_kv, lam_q1, lam_k1, lam_q2, lam_k2,
           g_subln, w_branch_a, w_branch_b, w_branch_c, w_out, g_norm2, w_gate_up, w_down,
           rel_bias, g_final):
    bsz, seq, d = x.shape
    depth = w_ada.shape[0]
    T = ATT_TILE
    nk = seq // T
    assert d == D_MODEL and seq % T == 0 and seq % ROW_TILE == 0
    n = bsz * seq
    cast = lambda w: w.astype(MM_DTYPE)

    thresholds = _bucket_thresholds(seq + 2 * C_BAND * C_GROUPS[-1][1])
    assert seq <= N_OFFSETS * T or (N_OFFSETS - 2) * T + 1 >= thresholds[-1]
    tab = rel_bias.reshape(-1)
    bias_a = _bias_att_tiles(tab, 0, A_HEADS, T, thresholds)
    bias_b = _bias_att_tiles(tab, A_HEADS, B_HEADS, T, thresholds)
    bias_c = _bias_dil_tiles(tab, A_HEADS + B_HEADS, thresholds)

    mod = _modulation(c, w_ada, b_ada).reshape(depth, bsz, 6, d)

    splits = (A_HEADS * HEAD_DIM, A_LATENT, IDX_HEADS * IDX_DIM, IDX_DIM, IDX_HEADS,
              B_HEADS * 2 * HEAD_DIM, B_HEADS * 2 * HEAD_DIM, B_HEADS * 2 * HEAD_DIM,
              C_HEADS * HEAD_DIM, C_HEADS * HEAD_DIM, C_HEADS * HEAD_DIM, d, d, d)
    offs = np.concatenate([[0], np.cumsum(splits)])
    seg = lambda w, k: w[:, int(offs[k]):int(offs[k + 1])]

    x2d = x.reshape(n, d)
    for l in range(depth):
        wl = w_in[l]
        castT = lambda w: w.T.astype(MM_DTYPE)
        wT_iw = jnp.pad(seg(wl, 4).T, ((0, 16 - IDX_HEADS), (0, 0)))
        wid = C_HPG * HEAD_DIM
        w_c = jnp.concatenate([seg(wl, k)[:, g * wid:(g + 1) * wid]
                               for g in range(len(C_GROUPS)) for k in (8, 9, 10)], axis=1)
        ws_in = [castT(seg(wl, 0)), castT(seg(wl, 2)), cast(wT_iw), castT(seg(wl, 5)), castT(seg(wl, 7)),
                 cast(seg(wl, 3)), cast(seg(wl, 1)), cast(seg(wl, 6)), cast(w_c)]
        g1 = g_norm1[l].reshape(1, d)
        (aqT, iqT, iwT, bqT, bvT, ik, kv, kvT, bk, cg0, cg1, cg2) = _in_proj(
            x2d, mod[l], g1, g_kv[l].reshape(1, A_LATENT), ws_in, bsz, seq)

        o_a = _dsa(aqT, iqT, iwT,
                   ik.reshape(bsz, nk, T, IDX_DIM), kv.reshape(bsz, nk, T, A_LATENT), kvT,
                   bias_a, cast(w_uk[l].transpose(0, 2, 1)), cast(w_uv[l].transpose(0, 2, 1)))
        o_a = o_a.reshape(n, -1)

        lam_init = 0.8 - 0.6 * math.exp(-0.3 * l)
        lam_rows = jnp.stack([lam_q1[l], lam_k1[l], lam_q2[l], lam_k2[l]])
        dv = 2 * HEAD_DIM
        o_b = _diff(bqT, bk.reshape(bsz, nk, T, B_HEADS * dv), bvT, bias_b, lam_rows,
                    g_subln[l].reshape(dv, 1), lam_init)
        o_b = o_b.reshape(n, -1)

        ocs, lses = [], []
        for g, (cg, (window, dil)) in enumerate(zip((cg0, cg1, cg2), C_GROUPS)):
            assert window // dil == C_BAND
            o, s = _dilated_group(cg, bias_c[g * C_HPG:(g + 1) * C_HPG], g)
            ocs.append(o)
            lses.append(s)

        ws_out = [cast(seg(wl, 11)), cast(seg(wl, 12)), cast(seg(wl, 13)),
                  cast(w_branch_a[l]), cast(w_branch_b[l]), cast(w_branch_c[l]), cast(w_out[l]),
                  cast(w_gate_up[l]), cast(w_down[l])]
        x2d = _merge_ffn(x2d, mod[l], g1, g_norm2[l].reshape(1, d), g_final.reshape(1, d),
                         o_a, o_b, ocs, lses, ws_out, seq, final_norm=(l == depth - 1))
    return x2d.reshape(bsz, seq, d)
```

```python
import functools
import math

import numpy as np
import jax
import jax.numpy as jnp
from jax import lax
from jax.experimental import pallas as pl
from jax.experimental.pallas import tpu as pltpu

D_MODEL = 1024
HEAD_DIM = 64
ATTN_SCALE = HEAD_DIM ** -0.5
LOG2E = math.log2(math.e)
A_HEADS = 8
A_LATENT = 128
IDX_HEADS = 8
IDX_DIM = 64
IDX_SCALE = (IDX_HEADS * IDX_DIM) ** -0.5
TOPK_MAX = 256
B_HEADS = 4
C_GROUPS = ((128, 1), (512, 4), (2048, 16))
C_HPG = 4
C_HEADS = C_HPG * len(C_GROUPS)
N_BUCKETS = 32
MAX_DISTANCE = 2048
N_BIAS_HEADS = A_HEADS + B_HEADS + C_HEADS
D_FF = -(-8 * D_MODEL // (3 * 256)) * 256
EPS = 1e-6

MM_DTYPE = jnp.bfloat16
F32 = jnp.float32
I32 = jnp.int32

ATT_TILE = 256
N_OFFSETS = 8
C_BAND = 128
ROW_TILE = 512
FFN_CHUNK = 256
MOD_COLS = 1536
NEG = -1e30
INT_MIN = -2 ** 31
LANES = 128
SUBLANES = 8
KEY_BITS = 32
VMEM_LIMIT = 56 * 1024 * 1024


def _cparams(n_axes, vmem=VMEM_LIMIT):
    return pltpu.CompilerParams(dimension_semantics=("arbitrary",) * n_axes,
                                vmem_limit_bytes=vmem)


def _const_spec(shape):
    nd = len(shape)
    return pl.BlockSpec(shape, lambda *_: (0,) * nd, pipeline_mode=pl.Buffered(1))


def _bucket_thresholds(max_dist):
    n = np.arange(max_dist + 1)
    max_exact = N_BUCKETS // 2
    nf = np.maximum(n, 1).astype(np.float32)
    large = max_exact + (np.log(nf / np.float32(max_exact))
                         / np.float32(math.log(MAX_DISTANCE / max_exact))
                         * np.float32(N_BUCKETS - max_exact)).astype(np.int32)
    large = np.minimum(large, N_BUCKETS - 1)
    bucket = np.where(n < max_exact, n, large)
    assert np.all(np.diff(bucket) >= 0)
    thr = []
    for k in range(1, N_BUCKETS):
        idx = np.nonzero(bucket >= k)[0]
        thr.append(int(idx[0]) if idx.size else None)
    return thr


def _bias_from_dist(dist, tab_ref, col, thresholds, lo=0, hi=None):
    reached = [k for k, thr in enumerate(thresholds, start=1) if thr is not None]
    base = max([0] + [k for k in reached if thresholds[k - 1] <= lo])
    b = jnp.full(dist.shape, tab_ref[base * N_BIAS_HEADS + col], F32)
    for k in reached:
        thr = thresholds[k - 1]
        if thr > lo and (hi is None or thr <= hi):
            b = jnp.where(dist >= thr, tab_ref[k * N_BIAS_HEADS + col], b)
    return b


def _bias_att_kernel(tab_ref, out_ref, *, head0, thresholds):
    h = pl.program_id(0)
    n_off, tile, _ = out_ref.shape
    row = lax.broadcasted_iota(I32, (tile, tile), 0)
    colq = lax.broadcasted_iota(I32, (tile, tile), 1)
    for o in range(n_off):
        dist = jnp.maximum(o * tile + colq - row, 0)
        lo, hi = max(o * tile - (tile - 1), 0), o * tile + tile - 1
        b = _bias_from_dist(dist, tab_ref, head0 + h, thresholds, lo, hi) * LOG2E
        if o == 0:
            b = jnp.where(row <= colq, b, NEG)
        out_ref[o] = b


def _bias_att_tiles(tab, head0, n_heads, tile, thresholds):
    return pl.pallas_call(
        functools.partial(_bias_att_kernel, head0=head0, thresholds=thresholds),
        out_shape=jax.ShapeDtypeStruct((n_heads, N_OFFSETS, tile, tile), F32),
        grid=(n_heads,),
        in_specs=[pl.BlockSpec(memory_space=pltpu.SMEM)],
        out_specs=pl.BlockSpec((None, N_OFFSETS, tile, tile), lambda h: (h, 0, 0, 0)),
        compiler_params=_cparams(1),
        name="bias_att_tiles",
    )(tab)


def _bias_dil_kernel(tab_ref, out_ref, *, head0, thresholds):
    h = pl.program_id(0)
    g = h // C_HPG
    dil = jnp.where(g == 0, C_GROUPS[0][1], jnp.where(g == 1, C_GROUPS[1][1], C_GROUPS[2][1]))
    i = lax.broadcasted_iota(I32, (C_BAND, 2 * C_BAND), 0)
    j = lax.broadcasted_iota(I32, (C_BAND, 2 * C_BAND), 1)
    dist = jnp.maximum((i - j + C_BAND) * dil, 0)
    out_ref[...] = _bias_from_dist(dist, tab_ref, head0 + h, thresholds)


def _bias_dil_tiles(tab, head0, thresholds):
    return pl.pallas_call(
        functools.partial(_bias_dil_kernel, head0=head0, thresholds=thresholds),
        out_shape=jax.ShapeDtypeStruct((C_HEADS, C_BAND, 2 * C_BAND), F32),
        grid=(C_HEADS,),
        in_specs=[pl.BlockSpec(memory_space=pltpu.SMEM)],
        out_specs=pl.BlockSpec((None, C_BAND, 2 * C_BAND), lambda h: (h, 0, 0)),
        compiler_params=_cparams(1),
        name="bias_dil_tiles",
    )(tab)


def _mod_kernel(c_ref, w_ref, b_ref, out_ref):
    c = c_ref[...]
    ca = (c * jax.nn.sigmoid(c)).astype(MM_DTYPE)
    out_ref[...] = jnp.dot(ca, w_ref[...].astype(MM_DTYPE), preferred_element_type=F32) + b_ref[...]


def _modulation(c, w_ada, b_ada):
    depth, d, wid = w_ada.shape
    bsz = c.shape[0]
    tn = MOD_COLS
    assert wid % tn == 0
    return pl.pallas_call(
        _mod_kernel,
        out_shape=jax.ShapeDtypeStruct((depth, bsz, wid), F32),
        grid=(depth, wid // tn),
        in_specs=[pl.BlockSpec((bsz, d), lambda l, j: (0, 0)),
                  pl.BlockSpec((None, d, tn), lambda l, j: (l, 0, j)),
                  pl.BlockSpec((None, 1, tn), lambda l, j: (l, 0, j))],
        out_specs=pl.BlockSpec((None, bsz, tn), lambda l, j: (l, 0, j)),
        compiler_params=_cparams(2),
        name="adaln_modulation",
    )(c, w_ada, b_ada.reshape(depth, 1, wid))


def _rms(x):
    return x * lax.rsqrt(jnp.mean(x * x, axis=-1, keepdims=True) + EPS)


def _in_kernel(x_ref, mod_ref, g1_ref, gkv_ref,
               wT_aq, wT_iq, wT_iw, wT_bq, wT_bv, w_ik, w_kv, w_bk, w_c,
               o_aqT, o_iqT, o_iwT, o_bqT, o_bvT, o_ik, o_kv, o_kvT, o_bk, o_c0, o_c1, o_c2,
               c_scr):
    T = ATT_TILE
    tm = x_ref.shape[0]
    h = _rms(x_ref[...]) * g1_ref[...]
    h = h * (1.0 + mod_ref[1:2, :]) + mod_ref[0:1, :]
    hb = h.astype(MM_DTYPE)

    def mm(w):
        return jnp.dot(hb, w[...], preferred_element_type=F32)

    def mm_t(wT):
        return lax.dot_general(wT[...], hb, (((1,), (1,)), ((), ())), preferred_element_type=F32)

    o_aqT[...] = mm_t(wT_aq).astype(o_aqT.dtype)
    o_iqT[...] = mm_t(wT_iq).astype(o_iqT.dtype)
    o_bqT[...] = (mm_t(wT_bq) * (ATTN_SCALE * LOG2E)).astype(o_bqT.dtype)
    o_iwT[...] = (mm_t(wT_iw) * IDX_SCALE)[:IDX_HEADS]
    bvT = mm_t(wT_bv).astype(o_bvT.dtype)
    kv = _rms(mm(w_kv)) * gkv_ref[...]
    kvT = kv.T.astype(o_kvT.dtype)
    for j in range(tm // T):
        o_bvT[j] = bvT[:, j * T:(j + 1) * T]
        o_kvT[j] = kvT[:, j * T:(j + 1) * T]
    o_kv[...] = kv.astype(o_kv.dtype)
    o_ik[...] = mm(w_ik).astype(o_ik.dtype)
    o_bk[...] = mm(w_bk).astype(o_bk.dtype)

    yc = mm(w_c)
    n_chunk = yc.shape[1] // LANES
    for j in range(n_chunk):
        c_scr[j] = yc[:, j * LANES:(j + 1) * LANES]
    per_group = n_chunk // len(C_GROUPS)
    for g, o_c in enumerate((o_c0, o_c1, o_c2)):
        dil = C_GROUPS[g][1]
        for r in range(dil):
            for jj in range(per_group):
                o_c[r, :, jj * LANES:(jj + 1) * LANES] = c_scr[
                    g * per_group + jj, pl.ds(r, tm // dil, stride=dil), :].astype(o_c.dtype)


def _in_proj(x2d, mod_l, g1, gkv, ws, bsz, seq):
    n, d = x2d.shape
    tm = ROW_TILE
    T = ATT_TILE
    per_b = seq // tm
    nk = seq // T
    hd = A_HEADS * HEAD_DIM
    bw = B_HEADS * 2 * HEAD_DIM
    gw = 3 * C_HPG * HEAD_DIM
    in_specs = [pl.BlockSpec((tm, d), lambda i: (i, 0)),
                pl.BlockSpec((None, 6, d), lambda i: (i // per_b, 0, 0)),
                _const_spec((1, d)), _const_spec((1, A_LATENT))]
    in_specs += [_const_spec(w.shape) for w in ws]

    def tspec(rows):
        return pl.BlockSpec((None, rows, tm), lambda i: (i // per_b, 0, i % per_b))

    def tile_tspec(rows):
        return pl.BlockSpec((None, tm // T, rows, T), lambda i: (i // per_b, i % per_b, 0, 0))

    def rspec(wd):
        return pl.BlockSpec((tm, wd), lambda i: (i, 0))

    def cspec(dil):
        return pl.BlockSpec((None, dil, tm // dil, gw), lambda i: (i // per_b, 0, i % per_b, 0))

    sds = jax.ShapeDtypeStruct
    out_specs = [tspec(hd), tspec(IDX_HEADS * IDX_DIM), tspec(IDX_HEADS), tspec(bw),
                 tile_tspec(bw), rspec(IDX_DIM), rspec(A_LATENT), tile_tspec(A_LATENT), rspec(bw)]
    out_shape = [sds((bsz, hd, seq), MM_DTYPE), sds((bsz, IDX_HEADS * IDX_DIM, seq), MM_DTYPE),
                 sds((bsz, IDX_HEADS, seq), F32), sds((bsz, bw, seq), MM_DTYPE),
                 sds((bsz, nk, bw, T), MM_DTYPE), sds((n, IDX_DIM), MM_DTYPE),
                 sds((n, A_LATENT), MM_DTYPE), sds((bsz, nk, A_LATENT, T), MM_DTYPE),
                 sds((n, bw), MM_DTYPE)]
    for _, dil in C_GROUPS:
        out_specs.append(cspec(dil))
        out_shape.append(sds((bsz, dil, seq // dil, gw), MM_DTYPE))
    return pl.pallas_call(
        _in_kernel, out_shape=out_shape, grid=(n // tm,),
        in_specs=in_specs, out_specs=out_specs,
        scratch_shapes=[pltpu.VMEM((3 * gw // LANES, tm, LANES), F32)],
        compiler_params=_cparams(1), name="in_proj",
    )(x2d, mod_l, g1, gkv, *ws)


def _initial_max(n_chains, tile):
    return tuple(jnp.full((1, tile), NEG, F32) for _ in range(n_chains))


def _loop_grouped(n, body, init, log2_group):
    carry, done = init, jnp.int32(0)
    for lg in range(log2_group, -1, -1):
        group = 1 << lg
        trips = lax.shift_right_logical(n - done, jnp.int32(lg))

        def grouped(i, c, group=group, done=done):
            for j in range(group):
                c = body(done + group * i + j, c)
            return c

        carry = lax.fori_loop(0, trips, grouped, carry)
        done = done + group * trips
    return carry


SUM_ROWS = 16


def _with_ones_row(vT):
    row = lax.broadcasted_iota(I32, (SUM_ROWS, vT.shape[1]), 0)
    ones = jnp.where(row == 0, 1.0, 0.0).astype(vT.dtype)
    return jnp.concatenate([vT, ones], axis=0)


def _bit_transpose32(words):
    a = list(words)
    j, mask = 16, 0x0000FFFF
    while j:
        k = 0
        while k < 32:
            t = (a[k] ^ lax.shift_right_logical(a[k + j], jnp.int32(j))) & jnp.int32(mask)
            a[k] = a[k] ^ t
            a[k + j] = a[k + j] ^ lax.shift_left(t, jnp.int32(j))
            k = (k + j + 1) & ~j
        j >>= 1
        mask = (mask ^ (mask << j)) & 0xFFFFFFFF
    return a


def _dsa_kernel(aqT_ref, iqT_ref, iwT_ref, kidx_ref, kv_ref, kvT_ref, bias_ref, wukT_ref, wuvT_ref,
                out_ref, keys_ref, planes_ref, qlat_ref, acc_ref, s_ref, oT_ref, *, topk):
    T = ATT_TILE
    qi = pl.program_id(1)
    nk = qi + 1

    for h in range(A_HEADS):
        q = jnp.dot(wukT_ref[h], aqT_ref[h * HEAD_DIM:(h + 1) * HEAD_DIM, :],
                    preferred_element_type=F32) * (ATTN_SCALE * LOG2E)
        qlat_ref[h] = q.astype(qlat_ref.dtype)

    s_loc = lax.broadcasted_iota(I32, (T, T), 0)
    t_loc = lax.broadcasted_iota(I32, (T, T), 1)

    def score_tile(kj, diagonal):
        kt = kidx_ref[kj]
        acc = jnp.zeros((T, T), F32)
        for h in range(IDX_HEADS):
            s = jnp.dot(kt, iqT_ref[h * IDX_DIM:(h + 1) * IDX_DIM, :], preferred_element_type=F32)
            acc = acc + jnp.maximum(s, 0.0) * iwT_ref[h:h + 1, :]
        bits = lax.bitcast_convert_type(acc, I32)
        key = jnp.where(bits < 0, bits ^ jnp.int32(0x7FFFFFFF), bits)
        if diagonal:
            key = jnp.where(s_loc <= t_loc, key, jnp.int32(INT_MIN))
        keys_ref[kj] = key
        planes = _bit_transpose32([key[SUBLANES * i:SUBLANES * (i + 1), :] ^ jnp.int32(INT_MIN)
                                   for i in range(KEY_BITS)])
        for b in range(KEY_BITS):
            planes_ref[kj, b] = planes[b]

    def off_diagonal(kj, carry):
        score_tile(kj, False)
        return carry

    _loop_grouped(qi, off_diagonal, 0, log2_group=2)
    score_tile(qi, True)

    n_tiles = planes_ref.shape[0]

    @pl.when(qi == 0)
    def _():
        def clear_planes(kj, carry):
            for b in range(KEY_BITS):
                planes_ref[kj, b] = jnp.zeros((SUBLANES, T), I32)
            return carry

        lax.fori_loop(1, n_tiles, clear_planes, 0)

    one = jnp.int32(1)
    nil = jnp.int32(0)
    zero = jnp.zeros((1, T), I32)

    n_rows = n_tiles * SUBLANES
    tile_of_row = lax.broadcasted_iota(I32, (n_rows, T), 0) // SUBLANES
    tied0 = jnp.where(tile_of_row < nk, jnp.int32(-1), nil)

    def bit_step(b, carry):
        tied, n_gt, kth_u = carry
        ones = tied & planes_ref[:, b].reshape(n_rows, T)
        n1 = jnp.sum(lax.population_count(ones), axis=0, keepdims=True)
        take = (n_gt + n1) >= topk
        tied = jnp.where(take, ones, tied ^ ones)
        n_gt = jnp.where(take, n_gt, n_gt + n1)
        kth_u = jnp.where(take, kth_u | lax.shift_left(one, jnp.int32(KEY_BITS - 1) - b), kth_u)
        return tied, n_gt, kth_u

    tied, n_gt, kth_u = lax.fori_loop(0, KEY_BITS, bit_step, (tied0, zero, zero))
    n_eq = jnp.sum(lax.population_count(tied), axis=0, keepdims=True)
    n_ge = jnp.where(kth_u == nil, n_gt, n_gt + n_eq)
    kth = jnp.maximum(kth_u ^ jnp.int32(INT_MIN), jnp.int32(INT_MIN + 1))

    @pl.when(jnp.max(n_ge) > topk)
    def _():
        need = topk - n_gt
        r_bits = SUBLANES.bit_length() - 1

        before, cut_tile, rank = zero, zero, need
        words = jnp.zeros((SUBLANES, T), I32)
        for kj in range(n_tiles):
            tile_words = tied[kj * SUBLANES:(kj + 1) * SUBLANES, :]
            after = before + jnp.sum(lax.population_count(tile_words), axis=0, keepdims=True)
            here = jnp.where(before < need, jnp.where(after >= need, one, nil), nil) == one
            cut_tile = jnp.where(here, jnp.int32(kj), cut_tile)
            rank = jnp.where(here, need - before, rank)
            words = jnp.where(here, tile_words, words)
            before = after

        word_r = lax.broadcasted_iota(I32, (SUBLANES, T), 0)
        local = zero
        for bit in reversed(range(T.bit_length() - 1)):
            cand = local | jnp.int32(1 << bit)
            p_i = lax.shift_right_logical(cand, jnp.int32(r_bits))
            p_r = cand & jnp.int32(SUBLANES - 1)
            above = jnp.where(p_i == nil, nil,
                              lax.shift_left(jnp.full_like(p_i, -1), jnp.int32(KEY_BITS) - p_i))
            at_i = lax.shift_right_logical(
                words, jnp.broadcast_to(jnp.int32(KEY_BITS - 1) - p_i, words.shape)) & one
            below = lax.population_count(words & above) + jnp.where(word_r < p_r, at_i, nil)
            local = jnp.where(jnp.sum(below, axis=0, keepdims=True) < rank, cand, local)
        cut = cut_tile * T + local

        def demote(kj, carry):
            kk = keys_ref[kj]
            lowered = jnp.where((kj * T + s_loc) > cut, kth - one, kk)
            keys_ref[kj] = jnp.where(kk == kth, lowered, kk)
            return carry

        lax.fori_loop(0, nk, demote, 0)

    acc_ref[...] = jnp.zeros(acc_ref.shape, F32)

    def stage_logits(kj):
        unselected = jnp.where(keys_ref[kj] >= kth, 0.0, NEG)
        off = jnp.minimum(qi - kj, N_OFFSETS - 1)
        kvt = kv_ref[kj]

        def one_head(h):
            s = bias_ref[h, off] + jnp.dot(kvt, qlat_ref[h], preferred_element_type=F32)
            s = s + unselected
            s_ref[h] = s
            return jnp.max(s, axis=0, keepdims=True)

        return one_head

    first = stage_logits(0)
    tile_max0 = tuple(first(h) for h in range(A_HEADS))

    def attend(kj, carry):
        ms, tile_max = carry
        kvTt = _with_ones_row(kvT_ref[kj])
        stage_next = stage_logits(jnp.minimum(kj + 1, nk - 1))
        new_m, new_max = [], []
        for h in range(A_HEADS):
            m_new = jnp.maximum(ms[h], tile_max[h])
            p = jnp.exp2(s_ref[h] - m_new).astype(MM_DTYPE)
            alpha = jnp.exp2(ms[h] - m_new)
            new_max.append(stage_next(h))
            acc_ref[h] = alpha * acc_ref[h] + jnp.dot(kvTt, p, preferred_element_type=F32)
            new_m.append(m_new)
        return tuple(new_m), tuple(new_max)

    _loop_grouped(nk, attend, (_initial_max(A_HEADS, T), tile_max0), log2_group=1)

    for h in range(A_HEADS):
        o_lat = (acc_ref[h, 0:A_LATENT, :] / acc_ref[h, A_LATENT:A_LATENT + 1, :]).astype(MM_DTYPE)
        oT_ref[h * HEAD_DIM:(h + 1) * HEAD_DIM, :] = jnp.dot(
            wuvT_ref[h], o_lat, preferred_element_type=F32)
    out_ref[...] = oT_ref[...].T.astype(out_ref.dtype)


def _dsa(aqT, iqT, iwT, kidx, kv, kvT, bias_a, wukT, wuvT):
    bsz, _, seq = aqT.shape
    T = ATT_TILE
    nk = seq // T
    topk = min(TOPK_MAX, seq // 4)
    assert T == KEY_BITS * SUBLANES
    qspec = lambda rows: pl.BlockSpec((None, rows, T), lambda b, i: (b, 0, i))
    kspec = lambda a, c: pl.BlockSpec((None, nk, a, c), lambda b, i: (b, 0, 0, 0))
    return pl.pallas_call(
        functools.partial(_dsa_kernel, topk=topk),
        out_shape=jax.ShapeDtypeStruct((bsz, seq, A_HEADS * HEAD_DIM), MM_DTYPE),
        grid=(bsz, nk),
        in_specs=[qspec(A_HEADS * HEAD_DIM), qspec(IDX_HEADS * IDX_DIM), qspec(IDX_HEADS),
                  kspec(T, IDX_DIM), kspec(T, A_LATENT), kspec(A_LATENT, T),
                  _const_spec(bias_a.shape), _const_spec(wukT.shape), _const_spec(wuvT.shape)],
        out_specs=pl.BlockSpec((None, T, A_HEADS * HEAD_DIM), lambda b, i: (b, i, 0)),
        scratch_shapes=[pltpu.VMEM((nk, T, T), I32),
                        pltpu.VMEM((nk, KEY_BITS + 1, SUBLANES, T), I32),
                        pltpu.VMEM((A_HEADS, A_LATENT, T), MM_DTYPE),
                        pltpu.VMEM((A_HEADS, A_LATENT + SUM_ROWS, T), F32),
                        pltpu.VMEM((A_HEADS, T, T), F32),
                        pltpu.VMEM((A_HEADS * HEAD_DIM, T), F32)],
        compiler_params=_cparams(2), name="dsa_attention",
    )(aqT, iqT, iwT, kidx, kv, kvT, bias_a, wukT, wuvT)


def _diff_kernel(qT_ref, k_ref, vT_ref, bias_ref, lam_ref, gsub_ref, out_ref,
                 qz_ref, acc_ref, s_ref, oT_ref, *, lam_init):
    T = ATT_TILE
    dv = 2 * HEAD_DIM
    n_chain = 2 * B_HEADS
    qi = pl.program_id(1)
    half = lax.broadcasted_iota(I32, (dv, T), 0) < HEAD_DIM
    for h in range(B_HEADS):
        q = qT_ref[h * dv:(h + 1) * dv, :].astype(F32)
        qz_ref[2 * h] = jnp.where(half, q, 0.0).astype(MM_DTYPE)
        qz_ref[2 * h + 1] = jnp.where(half, 0.0, q).astype(MM_DTYPE)
    acc_ref[...] = jnp.zeros(acc_ref.shape, F32)

    def stage_logits(kj, c):
        h = c // 2
        off = jnp.minimum(qi - kj, N_OFFSETS - 1)
        s = bias_ref[h, off] + jnp.dot(k_ref[kj, :, h * dv:(h + 1) * dv], qz_ref[c],
                                       preferred_element_type=F32)
        s_ref[c] = s
        return jnp.max(s, axis=0, keepdims=True)

    tile_max0 = tuple(stage_logits(0, c) for c in range(n_chain))

    def step(kj, carry, last):
        ms, tile_max = carry
        new_m, new_max = [], []
        for c in range(n_chain):
            h = c // 2
            m_new = jnp.maximum(ms[c], tile_max[c])
            p = jnp.exp2(s_ref[c] - m_new).astype(MM_DTYPE)
            alpha = jnp.exp2(ms[c] - m_new)
            if not last:
                new_max.append(stage_logits(kj + 1, c))
            vT = _with_ones_row(vT_ref[kj, h * dv:(h + 1) * dv, :])
            acc_ref[c] = alpha * acc_ref[c] + jnp.dot(vT, p, preferred_element_type=F32)
            new_m.append(m_new)
        return tuple(new_m), tuple(new_max)

    carry = _loop_grouped(qi, lambda kj, cr: step(kj, cr, False),
                          (_initial_max(n_chain, T), tile_max0), log2_group=2)
    step(qi, carry, True)

    lr = lam_ref[...]
    lam = (jnp.exp(jnp.sum(lr[0:1, :] * lr[1:2, :], axis=1, keepdims=True))
           - jnp.exp(jnp.sum(lr[2:3, :] * lr[3:4, :], axis=1, keepdims=True)) + lam_init)

    def normalised(c):
        return acc_ref[c, 0:dv, :] / acc_ref[c, dv:dv + 1, :]

    for h in range(B_HEADS):
        attn = normalised(2 * h) - lam * normalised(2 * h + 1)
        y = attn * lax.rsqrt(jnp.mean(attn * attn, axis=0, keepdims=True) + EPS)
        oT_ref[h * dv:(h + 1) * dv, :] = y * gsub_ref[...] * (1.0 - lam_init)
    out_ref[...] = oT_ref[...].T.astype(out_ref.dtype)


def _diff(bqT, bk, bvT, bias_b, lam_rows, gsub, lam_init):
    bsz, _, seq = bqT.shape
    T = ATT_TILE
    nk = seq // T
    dv = 2 * HEAD_DIM
    qspec = pl.BlockSpec((None, B_HEADS * dv, T), lambda b, i: (b, 0, i))
    return pl.pallas_call(
        functools.partial(_diff_kernel, lam_init=lam_init),
        out_shape=jax.ShapeDtypeStruct((bsz, seq, B_HEADS * dv), MM_DTYPE),
        grid=(bsz, nk),
        in_specs=[qspec,
                  pl.BlockSpec((None, nk, T, B_HEADS * dv), lambda b, i: (b, 0, 0, 0)),
                  pl.BlockSpec((None, nk, B_HEADS * dv, T), lambda b, i: (b, 0, 0, 0)),
                  _const_spec(bias_b.shape), _const_spec((4, HEAD_DIM)), _const_spec((dv, 1))],
        out_specs=pl.BlockSpec((None, T, B_HEADS * dv), lambda b, i: (b, i, 0)),
        scratch_shapes=[pltpu.VMEM((2 * B_HEADS, dv, T), MM_DTYPE),
                        pltpu.VMEM((2 * B_HEADS, dv + SUM_ROWS, T), F32),
                        pltpu.VMEM((2 * B_HEADS, T, T), F32),
                        pltpu.VMEM((B_HEADS * dv, T), F32)],
        compiler_params=_cparams(2), name="diff_attention",
    )(bqT, bk, bvT, bias_b, lam_rows, gsub)


def _dil_kernel(cur_ref, halo_ref, bias_ref, out_ref, lse_ref, *, tq):
    n = C_BAND
    wid = C_HPG * HEAD_DIM
    halo_lo = jnp.where(pl.program_id(2) == 0, jnp.int32(n), jnp.int32(0))
    i = lax.broadcasted_iota(I32, (n, 2 * n), 0)
    j = lax.broadcasted_iota(I32, (n, 2 * n), 1)
    lane_head = lax.broadcasted_iota(I32, (n, wid), 1) // HEAD_DIM
    in_head = [lane_head == h for h in range(C_HPG)]
    band = jnp.where(j >= i, jnp.where(j <= i + n, 0.0, NEG), NEG)
    band0 = jnp.where(j >= jnp.maximum(i, halo_lo), jnp.where(j <= i + n, 0.0, NEG), NEG)
    bias = [bias_ref[h] + band for h in range(C_HPG)]
    bias0 = [bias_ref[h] + band0 for h in range(C_HPG)]

    def band_rows(r, c, lo, hi):
        if c == 0:
            return jnp.concatenate([halo_ref[r, :, lo:hi], cur_ref[r, 0:n, lo:hi]], axis=0)
        return cur_ref[r, (c - 1) * n:(c + 1) * n, lo:hi]

    blocks = [(r, c) for r in range(cur_ref.shape[0]) for c in range(tq // n)]
    logits = []
    for r, c in blocks:
        q = cur_ref[r, c * n:(c + 1) * n, 0:wid].astype(F32) * ATTN_SCALE
        keys = band_rows(r, c, wid, 2 * wid)
        for h in range(C_HPG):
            qh = jnp.where(in_head[h], q, 0.0).astype(MM_DTYPE)
            s = lax.dot_general(qh, keys, (((1,), (1,)), ((), ())), preferred_element_type=F32)
            logits.append(s + (bias0 if c == 0 else bias)[h])
    for k, (r, c) in enumerate(blocks):
        vals = band_rows(r, c, 2 * wid, 3 * wid)
        out = jnp.zeros((n, wid), F32)
        lse = jnp.zeros((n, wid), F32)
        for h in range(C_HPG):
            s = logits[k * C_HPG + h]
            m = jnp.max(s, axis=1, keepdims=True)
            p = jnp.exp(s - m)
            den = jnp.sum(p, axis=1, keepdims=True)
            o = jnp.dot(p.astype(MM_DTYPE), vals, preferred_element_type=F32) * (1.0 / den)
            out = jnp.where(in_head[h], o, out)
            lse = jnp.where(in_head[h], m + jnp.log(den), lse)
        out_ref[r, c * n:(c + 1) * n, :] = out
        lse_ref[r, c * n:(c + 1) * n, :] = lse


def _dilated_group(cg, bias_g, g):
    bsz, dil, m, gw = cg.shape
    wid = C_HPG * HEAD_DIM
    n = C_BAND
    assert m % n == 0 and gw == 3 * wid
    tq = min(m, 512)
    n_res = max(1, min(dil, 512 // tq))
    cur = pl.BlockSpec((None, n_res, tq, gw), lambda b, r, i: (b, r, i, 0))
    halo = pl.BlockSpec((None, n_res, n, gw),
                        lambda b, r, i: (b, r, jnp.maximum(i * (tq // n) - 1, 0), 0))
    outspec = pl.BlockSpec((None, n_res, tq, wid), lambda b, r, i: (b, r, i, 0))
    return pl.pallas_call(
        functools.partial(_dil_kernel, tq=tq),
        out_shape=[jax.ShapeDtypeStruct((bsz, dil, m, wid), F32)] * 2,
        grid=(bsz, dil // n_res, m // tq),
        in_specs=[cur, halo, pl.BlockSpec((C_HPG, n, 2 * n), lambda b, r, i: (0, 0, 0))],
        out_specs=[outspec, outspec],
        compiler_params=_cparams(3), name=f"dilated_group{g}",
    )(cg, cg, bias_g)


def _merge_ffn_kernel(x_ref, mod_ref, g1_ref, g2_ref, gf_ref, oa_ref, ob_ref,
                      c0_ref, c1_ref, c2_ref, s0_ref, s1_ref, s2_ref,
                      wza, wzb, wzc, wba, wbb, wbc, wo, wgu, wd, out_ref, tok_ref, *, final_norm):
    x = x_ref[...]
    tm = x.shape[0]
    h = _rms(x) * g1_ref[...]
    h = h * (1.0 + mod_ref[1:2, :]) + mod_ref[0:1, :]
    hb = h.astype(MM_DTYPE)

    def token_order(k, ref):
        dil, _, w = ref.shape
        if dil == 1:
            return ref[0]
        n_chunk = w // LANES
        for r in range(dil):
            for j in range(n_chunk):
                tok_ref[k * n_chunk + j, pl.ds(r, tm // dil, stride=dil), :] = ref[
                    r, :, j * LANES:(j + 1) * LANES]
        return jnp.concatenate([tok_ref[k * n_chunk + j] for j in range(n_chunk)], axis=1)

    s0, s1, s2 = s0_ref[0], token_order(0, s1_ref), token_order(1, s2_ref)
    c0, c1, c2 = c0_ref[0], token_order(2, c1_ref), token_order(3, c2_ref)
    mx = jnp.maximum(jnp.maximum(s0, s1), s2)
    e0, e1, e2 = jnp.exp(s0 - mx), jnp.exp(s1 - mx), jnp.exp(s2 - mx)
    oc = (e0 * c0 + e1 * c1 + e2 * c2) / (e0 + e1 + e2)

    def gated(wz, o, wb):
        z = jnp.dot(hb, wz[...], preferred_element_type=F32)
        return jax.nn.sigmoid(z) * jnp.dot(o, wb[...], preferred_element_type=F32)

    merged = (gated(wza, oa_ref[...], wba) + gated(wzb, ob_ref[...], wbb)
              + gated(wzc, oc.astype(MM_DTYPE), wbc))
    y = jnp.dot(merged.astype(MM_DTYPE), wo[...], preferred_element_type=F32)
    x = x + mod_ref[2:3, :] * y

    h = _rms(x) * g2_ref[...]
    h = h * (1.0 + mod_ref[4:5, :]) + mod_ref[3:4, :]
    hb = h.astype(MM_DTYPE)
    acc = jnp.zeros(x.shape, F32)
    for c in range(D_FF // FFN_CHUNK):
        cols = slice(c * FFN_CHUNK, (c + 1) * FFN_CHUNK)
        fg = jnp.dot(hb, wgu[:, cols], preferred_element_type=F32)
        fu = jnp.dot(hb, wgu[:, D_FF + c * FFN_CHUNK:D_FF + (c + 1) * FFN_CHUNK],
                     preferred_element_type=F32)
        act = (fg * jax.nn.sigmoid(fg) * fu).astype(MM_DTYPE)
        acc = acc + jnp.dot(act, wd[cols, :], preferred_element_type=F32)
    y = x + mod_ref[5:6, :] * acc
    if final_norm:
        y = _rms(y) * gf_ref[...]
    out_ref[...] = y


def _merge_ffn(x2d, mod_l, g1, g2, gf, oa, ob, ocs, lses, ws, seq, final_norm):
    n, d = x2d.shape
    tm = ROW_TILE
    per_b = seq // tm
    row = lambda wd: pl.BlockSpec((tm, wd), lambda i: (i, 0))
    vec = _const_spec((1, d))
    wid = C_HPG * HEAD_DIM
    res = [pl.BlockSpec((None, dil, tm // dil, wid), lambda i: (i // per_b, 0, i % per_b, 0))
           for _, dil in C_GROUPS]
    in_specs = [row(d), pl.BlockSpec((None, 6, d), lambda i: (i // per_b, 0, 0)), vec, vec, vec,
                row(oa.shape[1]), row(ob.shape[1])] + res + res
    in_specs += [_const_spec(w.shape) for w in ws]
    return pl.pallas_call(
        functools.partial(_merge_ffn_kernel, final_norm=final_norm),
        out_shape=jax.ShapeDtypeStruct((n, d), F32), grid=(n // tm,),
        in_specs=in_specs, out_specs=row(d),
        scratch_shapes=[pltpu.VMEM((4 * wid // LANES, tm, LANES), F32)],
        compiler_params=_cparams(1), name="merge_ffn",
    )(x2d, mod_l, g1, g2, gf, oa, ob, *ocs, *lses, *ws)


def kernel(x, c, w_ada, b_ada, g_norm1, w_in, w_uk, w_uv, g_kv, lam_q1, lam_k1, lam_q2, lam_k2,
           g_subln, w_branch_a, w_branch_b, w_branch_c, w_out, g_norm2, w_gate_up, w_down,
           rel_bias, g_final):
    bsz, seq, d = x.shape
    depth = w_ada.shape[0]
    T = ATT_TILE
    nk = seq // T
    assert d == D_MODEL and seq % T == 0 and seq % ROW_TILE == 0
    n = bsz * seq
    cast = lambda w: w.astype(MM_DTYPE)

    thresholds = _bucket_thresholds(seq + 2 * C_BAND * C_GROUPS[-1][1])
    assert seq <= N_OFFSETS * T or (N_OFFSETS - 2) * T + 1 >= thresholds[-1]
    tab = rel_bias.reshape(-1)
    bias_a = _bias_att_tiles(tab, 0, A_HEADS, T, thresholds)
    bias_b = _bias_att_tiles(tab, A_HEADS, B_HEADS, T, thresholds)
    bias_c = _bias_dil_tiles(tab, A_HEADS + B_HEADS, thresholds)

    mod = _modulation(c, w_ada, b_ada).reshape(depth, bsz, 6, d)

    splits = (A_HEADS * HEAD_DIM, A_LATENT, IDX_HEADS * IDX_DIM, IDX_DIM, IDX_HEADS,
              B_HEADS * 2 * HEAD_DIM, B_HEADS * 2 * HEAD_DIM, B_HEADS * 2 * HEAD_DIM,
              C_HEADS * HEAD_DIM, C_HEADS * HEAD_DIM, C_HEADS * HEAD_DIM, d, d, d)
    offs = np.concatenate([[0], np.cumsum(splits)])
    seg = lambda w, k: w[:, int(offs[k]):int(offs[k + 1])]

    x2d = x.reshape(n, d)
    for l in range(depth):
        wl = w_in[l]
        castT = lambda w: w.T.astype(MM_DTYPE)
        wT_iw = jnp.pad(seg(wl, 4).T, ((0, 16 - IDX_HEADS), (0, 0)))
        wid = C_HPG * HEAD_DIM
        w_c = jnp.concatenate([seg(wl, k)[:, g * wid:(g + 1) * wid]
                               for g in range(len(C_GROUPS)) for k in (8, 9, 10)], axis=1)
        ws_in = [castT(seg(wl, 0)), castT(seg(wl, 2)), cast(wT_iw), castT(seg(wl, 5)), castT(seg(wl, 7)),
                 cast(seg(wl, 3)), cast(seg(wl, 1)), cast(seg(wl, 6)), cast(w_c)]
        g1 = g_norm1[l].reshape(1, d)
        (aqT, iqT, iwT, bqT, bvT, ik, kv, kvT, bk, cg0, cg1, cg2) = _in_proj(
            x2d, mod[l], g1, g_kv[l].reshape(1, A_LATENT), ws_in, bsz, seq)

        o_a = _dsa(aqT, iqT, iwT,
                   ik.reshape(bsz, nk, T, IDX_DIM), kv.reshape(bsz, nk, T, A_LATENT), kvT,
                   bias_a, cast(w_uk[l].transpose(0, 2, 1)), cast(w_uv[l].transpose(0, 2, 1)))
        o_a = o_a.reshape(n, -1)

        lam_init = 0.8 - 0.6 * math.exp(-0.3 * l)
        lam_rows = jnp.stack([lam_q1[l], lam_k1[l], lam_q2[l], lam_k2[l]])
        dv = 2 * HEAD_DIM
        o_b = _diff(bqT, bk.reshape(bsz, nk, T, B_HEADS * dv), bvT, bias_b, lam_rows,
                    g_subln[l].reshape(dv, 1), lam_init)
        o_b = o_b.reshape(n, -1)

        ocs, lses = [], []
        for g, (cg, (window, dil)) in enumerate(zip((cg0, cg1, cg2), C_GROUPS)):
            assert window // dil == C_BAND
            o, s = _dilated_group(cg, bias_c[g * C_HPG:(g + 1) * C_HPG], g)
            ocs.append(o)
            lses.append(s)

        ws_out = [cast(seg(wl, 11)), cast(seg(wl, 12)), cast(seg(wl, 13)),
                  cast(w_branch_a[l]), cast(w_branch_b[l]), cast(w_branch_c[l]), cast(w_out[l]),
                  cast(w_gate_up[l]), cast(w_down[l])]
        x2d = _merge_ffn(x2d, mod[l], g1, g_norm2[l].reshape(1, d), g_final.reshape(1, d),
                         o_a, o_b, ocs, lses, ws_out, seq, final_norm=(l == depth - 1))
    return x2d.reshape(bsz, seq, d)
```

```python
import functools
import math

import numpy as np
import jax
import jax.numpy as jnp
from jax import lax
from jax.experimental import pallas as pl
from jax.experimental.pallas import tpu as pltpu

D_MODEL = 1024
HEAD_DIM = 64
ATTN_SCALE = HEAD_DIM ** -0.5
LOG2E = math.log2(math.e)
A_HEADS = 8
A_LATENT = 128
IDX_HEADS = 8
IDX_DIM = 64
IDX_SCALE = (IDX_HEADS * IDX_DIM) ** -0.5
TOPK_MAX = 256
B_HEADS = 4
C_GROUPS = ((128, 1), (512, 4), (2048, 16))
C_HPG = 4
C_HEADS = C_HPG * len(C_GROUPS)
N_BUCKETS = 32
MAX_DISTANCE = 2048
N_BIAS_HEADS = A_HEADS + B_HEADS + C_HEADS
D_FF = -(-8 * D_MODEL // (3 * 256)) * 256
EPS = 1e-6

MM_DTYPE = jnp.bfloat16
F32 = jnp.float32
I32 = jnp.int32

ATT_TILE = 256
N_OFFSETS = 8
C_BAND = 128
ROW_TILE = 512
IN_ROW_TILE = 1024
FFN_CHUNK = 256
MOD_COLS = 1536
NEG = -1e30
INT_MIN = -2 ** 31
LANES = 128
SUBLANES = 8
KEY_BITS = 32
VMEM_LIMIT = 56 * 1024 * 1024


def _cparams(n_axes, vmem=VMEM_LIMIT):
    return pltpu.CompilerParams(dimension_semantics=("arbitrary",) * n_axes,
                                vmem_limit_bytes=vmem)


def _const_spec(shape):
    nd = len(shape)
    return pl.BlockSpec(shape, lambda *_: (0,) * nd, pipeline_mode=pl.Buffered(1))


def _bucket_thresholds(max_dist):
    n = np.arange(max_dist + 1)
    max_exact = N_BUCKETS // 2
    nf = np.maximum(n, 1).astype(np.float32)
    large = max_exact + (np.log(nf / np.float32(max_exact))
                         / np.float32(math.log(MAX_DISTANCE / max_exact))
                         * np.float32(N_BUCKETS - max_exact)).astype(np.int32)
    large = np.minimum(large, N_BUCKETS - 1)
    bucket = np.where(n < max_exact, n, large)
    assert np.all(np.diff(bucket) >= 0)
    thr = []
    for k in range(1, N_BUCKETS):
        idx = np.nonzero(bucket >= k)[0]
        thr.append(int(idx[0]) if idx.size else None)
    return thr


def _bias_from_dist(dist, tab_ref, col, thresholds, lo=0, hi=None):
    reached = [k for k, thr in enumerate(thresholds, start=1) if thr is not None]
    base = max([0] + [k for k in reached if thresholds[k - 1] <= lo])
    b = jnp.full(dist.shape, tab_ref[base * N_BIAS_HEADS + col], F32)
    for k in reached:
        thr = thresholds[k - 1]
        if thr > lo and (hi is None or thr <= hi):
            b = jnp.where(dist >= thr, tab_ref[k * N_BIAS_HEADS + col], b)
    return b


def _bias_att_kernel(tab_ref, out_ref, *, head0, thresholds):
    h = pl.program_id(0)
    n_off, tile, _ = out_ref.shape
    row = lax.broadcasted_iota(I32, (tile, tile), 0)
    colq = lax.broadcasted_iota(I32, (tile, tile), 1)
    for o in range(n_off):
        dist = jnp.maximum(o * tile + colq - row, 0)
        lo, hi = max(o * tile - (tile - 1), 0), o * tile + tile - 1
        b = _bias_from_dist(dist, tab_ref, head0 + h, thresholds, lo, hi) * LOG2E
        if o == 0:
            b = jnp.where(row <= colq, b, NEG)
        out_ref[o] = b


def _bias_att_tiles(tab, head0, n_heads, tile, thresholds):
    return pl.pallas_call(
        functools.partial(_bias_att_kernel, head0=head0, thresholds=thresholds),
        out_shape=jax.ShapeDtypeStruct((n_heads, N_OFFSETS, tile, tile), F32),
        grid=(n_heads,),
        in_specs=[pl.BlockSpec(memory_space=pltpu.SMEM)],
        out_specs=pl.BlockSpec((None, N_OFFSETS, tile, tile), lambda h: (h, 0, 0, 0)),
        compiler_params=_cparams(1),
        name="bias_att_tiles",
    )(tab)


def _bias_dil_kernel(tab_ref, out_ref, *, head0, thresholds):
    h = pl.program_id(0)
    g = h // C_HPG
    dil = jnp.where(g == 0, C_GROUPS[0][1], jnp.where(g == 1, C_GROUPS[1][1], C_GROUPS[2][1]))
    i = lax.broadcasted_iota(I32, (C_BAND, 2 * C_BAND), 0)
    j = lax.broadcasted_iota(I32, (C_BAND, 2 * C_BAND), 1)
    dist = jnp.maximum((i - j + C_BAND) * dil, 0)
    out_ref[...] = _bias_from_dist(dist, tab_ref, head0 + h, thresholds)


def _bias_dil_tiles(tab, head0, thresholds):
    return pl.pallas_call(
        functools.partial(_bias_dil_kernel, head0=head0, thresholds=thresholds),
        out_shape=jax.ShapeDtypeStruct((C_HEADS, C_BAND, 2 * C_BAND), F32),
        grid=(C_HEADS,),
        in_specs=[pl.BlockSpec(memory_space=pltpu.SMEM)],
        out_specs=pl.BlockSpec((None, C_BAND, 2 * C_BAND), lambda h: (h, 0, 0)),
        compiler_params=_cparams(1),
        name="bias_dil_tiles",
    )(tab)


def _mod_kernel(c_ref, w_ref, b_ref, out_ref):
    c = c_ref[...]
    ca = (c * jax.nn.sigmoid(c)).astype(MM_DTYPE)
    out_ref[...] = jnp.dot(ca, w_ref[...].astype(MM_DTYPE), preferred_element_type=F32) + b_ref[...]


def _modulation(c, w_ada, b_ada):
    depth, d, wid = w_ada.shape
    bsz = c.shape[0]
    tn = MOD_COLS
    assert wid % tn == 0
    return pl.pallas_call(
        _mod_kernel,
        out_shape=jax.ShapeDtypeStruct((depth, bsz, wid), F32),
        grid=(depth, wid // tn),
        in_specs=[pl.BlockSpec((bsz, d), lambda l, j: (0, 0)),
                  pl.BlockSpec((None, d, tn), lambda l, j: (l, 0, j)),
                  pl.BlockSpec((None, 1, tn), lambda l, j: (l, 0, j))],
        out_specs=pl.BlockSpec((None, bsz, tn), lambda l, j: (l, 0, j)),
        compiler_params=_cparams(2),
        name="adaln_modulation",
    )(c, w_ada, b_ada.reshape(depth, 1, wid))


def _rms(x):
    return x * lax.rsqrt(jnp.mean(x * x, axis=-1, keepdims=True) + EPS)


IN_ROWS_T = (("aq", A_HEADS * HEAD_DIM), ("iq", IDX_HEADS * IDX_DIM), ("bq", B_HEADS * 2 * HEAD_DIM),
             ("bv", B_HEADS * 2 * HEAD_DIM), ("iw", 16))
IN_COLS = (("kv", A_LATENT), ("bk", B_HEADS * 2 * HEAD_DIM), ("c", 3 * C_HEADS * HEAD_DIM),
           ("ik", IDX_DIM))


def _segments(layout):
    out, start = {}, 0
    for name, width in layout:
        out[name] = slice(start, start + width)
        start += width
    return out


def _in_kernel(x_ref, mod_ref, g1_ref, gkv_ref, wT_ref, w_ref,
               o_aqT, o_iqT, o_iwT, o_bqT, o_bvT, o_ik, o_kv, o_kvT, o_bk, o_c0, o_c1, o_c2,
               c_scr):
    T = ATT_TILE
    tm = x_ref.shape[0]
    h = _rms(x_ref[...]) * g1_ref[...]
    h = h * (1.0 + mod_ref[1:2, :]) + mod_ref[0:1, :]
    hb = h.astype(MM_DTYPE)
    rows_t, cols = _segments(IN_ROWS_T), _segments(IN_COLS)

    def mm(name):
        return jnp.dot(hb, w_ref[:, cols[name]], preferred_element_type=F32)

    def mm_t(name):
        return lax.dot_general(wT_ref[rows_t[name], :], hb, (((1,), (1,)), ((), ())),
                               preferred_element_type=F32)


    o_aqT[...] = mm_t("aq").astype(o_aqT.dtype)
    o_iqT[...] = mm_t("iq").astype(o_iqT.dtype)
    o_bqT[...] = (mm_t("bq") * (ATTN_SCALE * LOG2E)).astype(o_bqT.dtype)
    o_iwT[...] = (mm_t("iw") * IDX_SCALE)[:IDX_HEADS]
    bvT = mm_t("bv").astype(o_bvT.dtype)
    kv = _rms(mm("kv")) * gkv_ref[...]
    kvT = kv.T.astype(o_kvT.dtype)
    for j in range(tm // T):
        o_bvT[j] = bvT[:, j * T:(j + 1) * T]
        o_kvT[j] = kvT[:, j * T:(j + 1) * T]
    o_kv[...] = kv.astype(o_kv.dtype)
    o_ik[...] = mm("ik").astype(o_ik.dtype)
    o_bk[...] = mm("bk").astype(o_bk.dtype)

    yc = mm("c")
    n_chunk = yc.shape[1] // LANES
    for j in range(n_chunk):
        c_scr[j] = yc[:, j * LANES:(j + 1) * LANES]
    per_group = n_chunk // len(C_GROUPS)
    for g, o_c in enumerate((o_c0, o_c1, o_c2)):
        dil = C_GROUPS[g][1]
        for r in range(dil):
            for jj in range(per_group):
                o_c[r, :, jj * LANES:(jj + 1) * LANES] = c_scr[
                    g * per_group + jj, pl.ds(r, tm // dil, stride=dil), :].astype(o_c.dtype)


def _in_proj(x2d, mod_l, g1, gkv, ws, bsz, seq):
    n, d = x2d.shape
    tm = IN_ROW_TILE
    T = ATT_TILE
    per_b = seq // tm
    nk = seq // T
    hd = A_HEADS * HEAD_DIM
    bw = B_HEADS * 2 * HEAD_DIM
    gw = 3 * C_HPG * HEAD_DIM
    in_specs = [pl.BlockSpec((tm, d), lambda i: (i, 0)),
                pl.BlockSpec((None, 6, d), lambda i: (i // per_b, 0, 0)),
                _const_spec((1, d)), _const_spec((1, A_LATENT))]
    in_specs += [_const_spec(w.shape) for w in ws]

    def tspec(rows):
        return pl.BlockSpec((None, rows, tm), lambda i: (i // per_b, 0, i % per_b))

    def tile_tspec(rows):
        return pl.BlockSpec((None, tm // T, rows, T), lambda i: (i // per_b, i % per_b, 0, 0))

    def rspec(wd):
        return pl.BlockSpec((tm, wd), lambda i: (i, 0))

    def cspec(dil):
        return pl.BlockSpec((None, dil, tm // dil, gw), lambda i: (i // per_b, 0, i % per_b, 0))

    sds = jax.ShapeDtypeStruct
    out_specs = [tspec(hd), tspec(IDX_HEADS * IDX_DIM), tspec(IDX_HEADS), tspec(bw),
                 tile_tspec(bw), rspec(IDX_DIM), rspec(A_LATENT), tile_tspec(A_LATENT), rspec(bw)]
    out_shape = [sds((bsz, hd, seq), MM_DTYPE), sds((bsz, IDX_HEADS * IDX_DIM, seq), MM_DTYPE),
                 sds((bsz, IDX_HEADS, seq), F32), sds((bsz, bw, seq), MM_DTYPE),
                 sds((bsz, nk, bw, T), MM_DTYPE), sds((n, IDX_DIM), MM_DTYPE),
                 sds((n, A_LATENT), MM_DTYPE), sds((bsz, nk, A_LATENT, T), MM_DTYPE),
                 sds((n, bw), MM_DTYPE)]
    for _, dil in C_GROUPS:
        out_specs.append(cspec(dil))
        out_shape.append(sds((bsz, dil, seq // dil, gw), MM_DTYPE))
    return pl.pallas_call(
        _in_kernel, out_shape=out_shape, grid=(n // tm,),
        in_specs=in_specs, out_specs=out_specs,
        scratch_shapes=[pltpu.VMEM((3 * gw // LANES, tm, LANES), F32)],
        compiler_params=_cparams(1), name="in_proj",
    )(x2d, mod_l, g1, gkv, *ws)


def _initial_max(n_chains, tile):
    return tuple(jnp.full((1, tile), NEG, F32) for _ in range(n_chains))


def _loop_grouped(n, body, init, log2_group):
    carry, done = init, jnp.int32(0)
    for lg in range(log2_group, -1, -1):
        group = 1 << lg
        trips = lax.shift_right_logical(n - done, jnp.int32(lg))

        def grouped(i, c, group=group, done=done):
            for j in range(group):
                c = body(done + group * i + j, c)
            return c

        carry = lax.fori_loop(0, trips, grouped, carry)
        done = done + group * trips
    return carry


SUM_ROWS = 16


def _with_ones_row(vT):
    row = lax.broadcasted_iota(I32, (SUM_ROWS, vT.shape[1]), 0)
    ones = jnp.where(row == 0, 1.0, 0.0).astype(vT.dtype)
    return jnp.concatenate([vT, ones], axis=0)


def _bit_transpose32(words):
    a = list(words)
    j, mask = 16, 0x0000FFFF
    while j:
        k = 0
        while k < 32:
            t = (a[k] ^ lax.shift_right_logical(a[k + j], jnp.int32(j))) & jnp.int32(mask)
            a[k] = a[k] ^ t
            a[k + j] = a[k + j] ^ lax.shift_left(t, jnp.int32(j))
            k = (k + j + 1) & ~j
        j >>= 1
        mask = (mask ^ (mask << j)) & 0xFFFFFFFF
    return a


def _dsa_kernel(aqT_ref, iqT_ref, iwT_ref, kidx_ref, kv_ref, kvT_ref, bias_ref, wukT_ref, wuvT_ref,
                out_ref, keys_ref, planes_ref, qlat_ref, acc_ref, s_ref, oT_ref, *, topk):
    T = ATT_TILE
    qi = pl.program_id(1)
    nk = qi + 1

    for h in range(A_HEADS):
        q = jnp.dot(wukT_ref[h], aqT_ref[h * HEAD_DIM:(h + 1) * HEAD_DIM, :],
                    preferred_element_type=F32) * (ATTN_SCALE * LOG2E)
        qlat_ref[h] = q.astype(qlat_ref.dtype)

    s_loc = lax.broadcasted_iota(I32, (T, T), 0)
    t_loc = lax.broadcasted_iota(I32, (T, T), 1)

    def score_tile(kj, diagonal):
        kt = kidx_ref[kj]
        acc = jnp.zeros((T, T), F32)
        for h in range(IDX_HEADS):
            s = jnp.dot(kt, iqT_ref[h * IDX_DIM:(h + 1) * IDX_DIM, :], preferred_element_type=F32)
            acc = acc + jnp.maximum(s, 0.0) * iwT_ref[h:h + 1, :]
        bits = lax.bitcast_convert_type(acc, I32)
        key = jnp.where(bits < 0, bits ^ jnp.int32(0x7FFFFFFF), bits)
        if diagonal:
            key = jnp.where(s_loc <= t_loc, key, jnp.int32(INT_MIN))
        keys_ref[kj] = key
        planes = _bit_transpose32([key[SUBLANES * i:SUBLANES * (i + 1), :] ^ jnp.int32(INT_MIN)
                                   for i in range(KEY_BITS)])
        for b in range(KEY_BITS):
            planes_ref[kj, b] = planes[b]

    def off_diagonal(kj, carry):
        score_tile(kj, False)
        return carry

    _loop_grouped(qi, off_diagonal, 0, log2_group=2)
    score_tile(qi, True)

    n_tiles = planes_ref.shape[0]

    @pl.when(qi == 0)
    def _():
        def clear_planes(kj, carry):
            for b in range(KEY_BITS):
                planes_ref[kj, b] = jnp.zeros((SUBLANES, T), I32)
            return carry

        lax.fori_loop(1, n_tiles, clear_planes, 0)

    one = jnp.int32(1)
    nil = jnp.int32(0)
    zero = jnp.zeros((1, T), I32)

    n_rows = n_tiles * SUBLANES
    tile_of_row = lax.broadcasted_iota(I32, (n_rows, T), 0) // SUBLANES
    tied0 = jnp.where(tile_of_row < nk, jnp.int32(-1), nil)

    def bit_step(b, carry):
        tied, n_gt, kth_u = carry
        ones = tied & planes_ref[:, b].reshape(n_rows, T)
        n1 = jnp.sum(lax.population_count(ones), axis=0, keepdims=True)
        take = (n_gt + n1) >= topk
        tied = jnp.where(take, ones, tied ^ ones)
        n_gt = jnp.where(take, n_gt, n_gt + n1)
        kth_u = jnp.where(take, kth_u | lax.shift_left(one, jnp.int32(KEY_BITS - 1) - b), kth_u)
        return tied, n_gt, kth_u

    tied, n_gt, kth_u = lax.fori_loop(0, KEY_BITS, bit_step, (tied0, zero, zero))
    n_eq = jnp.sum(lax.population_count(tied), axis=0, keepdims=True)
    n_ge = jnp.where(kth_u == nil, n_gt, n_gt + n_eq)
    kth = jnp.maximum(kth_u ^ jnp.int32(INT_MIN), jnp.int32(INT_MIN + 1))

    @pl.when(jnp.max(n_ge) > topk)
    def _():
        need = topk - n_gt
        r_bits = SUBLANES.bit_length() - 1

        before, cut_tile, rank = zero, zero, need
        words = jnp.zeros((SUBLANES, T), I32)
        for kj in range(n_tiles):
            tile_words = tied[kj * SUBLANES:(kj + 1) * SUBLANES, :]
            after = before + jnp.sum(lax.population_count(tile_words), axis=0, keepdims=True)
            here = jnp.where(before < need, jnp.where(after >= need, one, nil), nil) == one
            cut_tile = jnp.where(here, jnp.int32(kj), cut_tile)
            rank = jnp.where(here, need - before, rank)
            words = jnp.where(here, tile_words, words)
            before = after

        word_r = lax.broadcasted_iota(I32, (SUBLANES, T), 0)
        local = zero
        for bit in reversed(range(T.bit_length() - 1)):
            cand = local | jnp.int32(1 << bit)
            p_i = lax.shift_right_logical(cand, jnp.int32(r_bits))
            p_r = cand & jnp.int32(SUBLANES - 1)
            above = jnp.where(p_i == nil, nil,
                              lax.shift_left(jnp.full_like(p_i, -1), jnp.int32(KEY_BITS) - p_i))
            at_i = lax.shift_right_logical(
                words, jnp.broadcast_to(jnp.int32(KEY_BITS - 1) - p_i, words.shape)) & one
            below = lax.population_count(words & above) + jnp.where(word_r < p_r, at_i, nil)
            local = jnp.where(jnp.sum(below, axis=0, keepdims=True) < rank, cand, local)
        cut = cut_tile * T + local

        def demote(kj, carry):
            kk = keys_ref[kj]
            lowered = jnp.where((kj * T + s_loc) > cut, kth - one, kk)
            keys_ref[kj] = jnp.where(kk == kth, lowered, kk)
            return carry

        lax.fori_loop(0, nk, demote, 0)

    acc_ref[...] = jnp.zeros(acc_ref.shape, F32)

    def stage_logits(kj):
        unselected = jnp.where(keys_ref[kj] >= kth, 0.0, NEG)
        off = jnp.minimum(qi - kj, N_OFFSETS - 1)
        kvt = kv_ref[kj]

        def one_head(h):
            s = bias_ref[h, off] + jnp.dot(kvt, qlat_ref[h], preferred_element_type=F32)
            s = s + unselected
            s_ref[h] = s
            return jnp.max(s, axis=0, keepdims=True)

        return one_head

    first = stage_logits(0)
    tile_max0 = tuple(first(h) for h in range(A_HEADS))

    def attend(kj, carry):
        ms, tile_max = carry
        kvTt = _with_ones_row(kvT_ref[kj])
        stage_next = stage_logits(jnp.minimum(kj + 1, nk - 1))
        new_m, new_max = [], []
        for h in range(A_HEADS):
            m_new = jnp.maximum(ms[h], tile_max[h])
            p = jnp.exp2(s_ref[h] - m_new).astype(MM_DTYPE)
            alpha = jnp.exp2(ms[h] - m_new)
            new_max.append(stage_next(h))
            acc_ref[h] = alpha * acc_ref[h] + jnp.dot(kvTt, p, preferred_element_type=F32)
            new_m.append(m_new)
        return tuple(new_m), tuple(new_max)

    _loop_grouped(nk, attend, (_initial_max(A_HEADS, T), tile_max0), log2_group=1)

    for h in range(A_HEADS):
        o_lat = (acc_ref[h, 0:A_LATENT, :] / acc_ref[h, A_LATENT:A_LATENT + 1, :]).astype(MM_DTYPE)
        oT_ref[h * HEAD_DIM:(h + 1) * HEAD_DIM, :] = jnp.dot(
            wuvT_ref[h], o_lat, preferred_element_type=F32)
    out_ref[...] = oT_ref[...].T.astype(out_ref.dtype)


def _dsa(aqT, iqT, iwT, kidx, kv, kvT, bias_a, wukT, wuvT):
    bsz, _, seq = aqT.shape
    T = ATT_TILE
    nk = seq // T
    topk = min(TOPK_MAX, seq // 4)
    assert T == KEY_BITS * SUBLANES
    qspec = lambda rows: pl.BlockSpec((None, rows, T), lambda b, i: (b, 0, i))
    kspec = lambda a, c: pl.BlockSpec((None, nk, a, c), lambda b, i: (b, 0, 0, 0))
    return pl.pallas_call(
        functools.partial(_dsa_kernel, topk=topk),
        out_shape=jax.ShapeDtypeStruct((bsz, seq, A_HEADS * HEAD_DIM), MM_DTYPE),
        grid=(bsz, nk),
        in_specs=[qspec(A_HEADS * HEAD_DIM), qspec(IDX_HEADS * IDX_DIM), qspec(IDX_HEADS),
                  kspec(T, IDX_DIM), kspec(T, A_LATENT), kspec(A_LATENT, T),
                  _const_spec(bias_a.shape), _const_spec(wukT.shape), _const_spec(wuvT.shape)],
        out_specs=pl.BlockSpec((None, T, A_HEADS * HEAD_DIM), lambda b, i: (b, i, 0)),
        scratch_shapes=[pltpu.VMEM((nk, T, T), I32),
                        pltpu.VMEM((nk, KEY_BITS + 1, SUBLANES, T), I32),
                        pltpu.VMEM((A_HEADS, A_LATENT, T), MM_DTYPE),
                        pltpu.VMEM((A_HEADS, A_LATENT + SUM_ROWS, T), F32),
                        pltpu.VMEM((A_HEADS, T, T), F32),
                        pltpu.VMEM((A_HEADS * HEAD_DIM, T), F32)],
        compiler_params=_cparams(2), name="dsa_attention",
    )(aqT, iqT, iwT, kidx, kv, kvT, bias_a, wukT, wuvT)


def _diff_kernel(qT_ref, k_ref, vT_ref, bias_ref, lam_ref, gsub_ref, out_ref,
                 qz_ref, acc_ref, s_ref, oT_ref, *, lam_init):
    T = ATT_TILE
    dv = 2 * HEAD_DIM
    n_chain = 2 * B_HEADS
    qi = pl.program_id(1)
    half = lax.broadcasted_iota(I32, (dv, T), 0) < HEAD_DIM
    for h in range(B_HEADS):
        q = qT_ref[h * dv:(h + 1) * dv, :].astype(F32)
        qz_ref[2 * h] = jnp.where(half, q, 0.0).astype(MM_DTYPE)
        qz_ref[2 * h + 1] = jnp.where(half, 0.0, q).astype(MM_DTYPE)
    acc_ref[...] = jnp.zeros(acc_ref.shape, F32)

    def stage_logits(kj, c):
        h = c // 2
        off = jnp.minimum(qi - kj, N_OFFSETS - 1)
        s = bias_ref[h, off] + jnp.dot(k_ref[kj, :, h * dv:(h + 1) * dv], qz_ref[c],
                                       preferred_element_type=F32)
        s_ref[c] = s
        return jnp.max(s, axis=0, keepdims=True)

    tile_max0 = tuple(stage_logits(0, c) for c in range(n_chain))

    def step(kj, carry, last):
        ms, tile_max = carry
        new_m, new_max = [], []
        for c in range(n_chain):
            h = c // 2
            m_new = jnp.maximum(ms[c], tile_max[c])
            p = jnp.exp2(s_ref[c] - m_new).astype(MM_DTYPE)
            alpha = jnp.exp2(ms[c] - m_new)
            if not last:
                new_max.append(stage_logits(kj + 1, c))
            vT = _with_ones_row(vT_ref[kj, h * dv:(h + 1) * dv, :])
            acc_ref[c] = alpha * acc_ref[c] + jnp.dot(vT, p, preferred_element_type=F32)
            new_m.append(m_new)
        return tuple(new_m), tuple(new_max)

    carry = _loop_grouped(qi, lambda kj, cr: step(kj, cr, False),
                          (_initial_max(n_chain, T), tile_max0), log2_group=2)
    step(qi, carry, True)

    lr = lam_ref[...]
    lam = (jnp.exp(jnp.sum(lr[0:1, :] * lr[1:2, :], axis=1, keepdims=True))
           - jnp.exp(jnp.sum(lr[2:3, :] * lr[3:4, :], axis=1, keepdims=True)) + lam_init)

    def normalised(c):
        return acc_ref[c, 0:dv, :] / acc_ref[c, dv:dv + 1, :]

    for h in range(B_HEADS):
        attn = normalised(2 * h) - lam * normalised(2 * h + 1)
        y = attn * lax.rsqrt(jnp.mean(attn * attn, axis=0, keepdims=True) + EPS)
        oT_ref[h * dv:(h + 1) * dv, :] = y * gsub_ref[...] * (1.0 - lam_init)
    out_ref[...] = oT_ref[...].T.astype(out_ref.dtype)


def _diff(bqT, bk, bvT, bias_b, lam_rows, gsub, lam_init):
    bsz, _, seq = bqT.shape
    T = ATT_TILE
    nk = seq // T
    dv = 2 * HEAD_DIM
    qspec = pl.BlockSpec((None, B_HEADS * dv, T), lambda b, i: (b, 0, i))
    return pl.pallas_call(
        functools.partial(_diff_kernel, lam_init=lam_init),
        out_shape=jax.ShapeDtypeStruct((bsz, seq, B_HEADS * dv), MM_DTYPE),
        grid=(bsz, nk),
        in_specs=[qspec,
                  pl.BlockSpec((None, nk, T, B_HEADS * dv), lambda b, i: (b, 0, 0, 0)),
                  pl.BlockSpec((None, nk, B_HEADS * dv, T), lambda b, i: (b, 0, 0, 0)),
                  _const_spec(bias_b.shape), _const_spec((4, HEAD_DIM)), _const_spec((dv, 1))],
        out_specs=pl.BlockSpec((None, T, B_HEADS * dv), lambda b, i: (b, i, 0)),
        scratch_shapes=[pltpu.VMEM((2 * B_HEADS, dv, T), MM_DTYPE),
                        pltpu.VMEM((2 * B_HEADS, dv + SUM_ROWS, T), F32),
                        pltpu.VMEM((2 * B_HEADS, T, T), F32),
                        pltpu.VMEM((B_HEADS * dv, T), F32)],
        compiler_params=_cparams(2), name="diff_attention",
    )(bqT, bk, bvT, bias_b, lam_rows, gsub)


def _dil_kernel(cur_ref, halo_ref, bias_ref, out_ref, lse_ref, *, tq):
    n = C_BAND
    wid = C_HPG * HEAD_DIM
    halo_lo = jnp.where(pl.program_id(2) == 0, jnp.int32(n), jnp.int32(0))
    i = lax.broadcasted_iota(I32, (n, 2 * n), 0)
    j = lax.broadcasted_iota(I32, (n, 2 * n), 1)
    lane_head = lax.broadcasted_iota(I32, (n, wid), 1) // HEAD_DIM
    in_head = [lane_head == h for h in range(C_HPG)]
    band = jnp.where(j >= i, jnp.where(j <= i + n, 0.0, NEG), NEG)
    band0 = jnp.where(j >= jnp.maximum(i, halo_lo), jnp.where(j <= i + n, 0.0, NEG), NEG)
    bias = [bias_ref[h] + band for h in range(C_HPG)]
    bias0 = [bias_ref[h] + band0 for h in range(C_HPG)]

    def band_rows(r, c, lo, hi):
        if c == 0:
            return jnp.concatenate([halo_ref[r, :, lo:hi], cur_ref[r, 0:n, lo:hi]], axis=0)
        return cur_ref[r, (c - 1) * n:(c + 1) * n, lo:hi]

    blocks = [(r, c) for r in range(cur_ref.shape[0]) for c in range(tq // n)]
    logits = []
    for r, c in blocks:
        q = cur_ref[r, c * n:(c + 1) * n, 0:wid].astype(F32) * ATTN_SCALE
        keys = band_rows(r, c, wid, 2 * wid)
        for h in range(C_HPG):
            qh = jnp.where(in_head[h], q, 0.0).astype(MM_DTYPE)
            s = lax.dot_general(qh, keys, (((1,), (1,)), ((), ())), preferred_element_type=F32)
            logits.append(s + (bias0 if c == 0 else bias)[h])
    for k, (r, c) in enumerate(blocks):
        vals = band_rows(r, c, 2 * wid, 3 * wid)
        out = jnp.zeros((n, wid), F32)
        lse = jnp.zeros((n, wid), F32)
        for h in range(C_HPG):
            s = logits[k * C_HPG + h]
            m = jnp.max(s, axis=1, keepdims=True)
            p = jnp.exp(s - m)
            den = jnp.sum(p, axis=1, keepdims=True)
            o = jnp.dot(p.astype(MM_DTYPE), vals, preferred_element_type=F32) * (1.0 / den)
            out = jnp.where(in_head[h], o, out)
            lse = jnp.where(in_head[h], m + jnp.log(den), lse)
        out_ref[r, c * n:(c + 1) * n, :] = out
        lse_ref[r, c * n:(c + 1) * n, :] = lse


def _dilated_group(cg, bias_g, g):
    bsz, dil, m, gw = cg.shape
    wid = C_HPG * HEAD_DIM
    n = C_BAND
    assert m % n == 0 and gw == 3 * wid
    tq = min(m, 512)
    n_res = max(1, min(dil, 512 // tq))
    cur = pl.BlockSpec((None, n_res, tq, gw), lambda b, r, i: (b, r, i, 0))
    halo = pl.BlockSpec((None, n_res, n, gw),
                        lambda b, r, i: (b, r, jnp.maximum(i * (tq // n) - 1, 0), 0))
    outspec = pl.BlockSpec((None, n_res, tq, wid), lambda b, r, i: (b, r, i, 0))
    return pl.pallas_call(
        functools.partial(_dil_kernel, tq=tq),
        out_shape=[jax.ShapeDtypeStruct((bsz, dil, m, wid), F32)] * 2,
        grid=(bsz, dil // n_res, m // tq),
        in_specs=[cur, halo, pl.BlockSpec((C_HPG, n, 2 * n), lambda b, r, i: (0, 0, 0))],
        out_specs=[outspec, outspec],
        compiler_params=_cparams(3), name=f"dilated_group{g}",
    )(cg, cg, bias_g)


def _merge_ffn_kernel(x_ref, mod_ref, g1_ref, g2_ref, gf_ref, oa_ref, ob_ref,
                      c0_ref, c1_ref, c2_ref, s0_ref, s1_ref, s2_ref,
                      wz, wba, wbb, wbc, wo, wgu, wd, out_ref, tok_ref, *, final_norm):
    x = x_ref[...]
    tm = x.shape[0]
    h = _rms(x) * g1_ref[...]
    h = h * (1.0 + mod_ref[1:2, :]) + mod_ref[0:1, :]
    hb = h.astype(MM_DTYPE)

    def token_order(k, ref):
        dil, _, w = ref.shape
        if dil == 1:
            return ref[0]
        n_chunk = w // LANES
        for r in range(dil):
            for j in range(n_chunk):
                tok_ref[k * n_chunk + j, pl.ds(r, tm // dil, stride=dil), :] = ref[
                    r, :, j * LANES:(j + 1) * LANES]
        return jnp.concatenate([tok_ref[k * n_chunk + j] for j in range(n_chunk)], axis=1)

    s0, s1, s2 = s0_ref[0], token_order(0, s1_ref), token_order(1, s2_ref)
    c0, c1, c2 = c0_ref[0], token_order(2, c1_ref), token_order(3, c2_ref)
    mx = jnp.maximum(jnp.maximum(s0, s1), s2)
    e0, e1, e2 = jnp.exp(s0 - mx), jnp.exp(s1 - mx), jnp.exp(s2 - mx)
    oc = (e0 * c0 + e1 * c1 + e2 * c2) / (e0 + e1 + e2)

    d = x.shape[1]

    def gated(k, o, wb):
        z = jnp.dot(hb, wz[:, k * d:(k + 1) * d], preferred_element_type=F32)
        return jax.nn.sigmoid(z) * jnp.dot(o, wb[...], preferred_element_type=F32)

    merged = (gated(0, oa_ref[...], wba) + gated(1, ob_ref[...], wbb)
              + gated(2, oc.astype(MM_DTYPE), wbc))
    y = jnp.dot(merged.astype(MM_DTYPE), wo[...], preferred_element_type=F32)
    x = x + mod_ref[2:3, :] * y

    h = _rms(x) * g2_ref[...]
    h = h * (1.0 + mod_ref[4:5, :]) + mod_ref[3:4, :]
    hb = h.astype(MM_DTYPE)
    acc = jnp.zeros(x.shape, F32)
    for c in range(D_FF // FFN_CHUNK):
        cols = slice(c * FFN_CHUNK, (c + 1) * FFN_CHUNK)
        fg = jnp.dot(hb, wgu[:, cols], preferred_element_type=F32)
        fu = jnp.dot(hb, wgu[:, D_FF + c * FFN_CHUNK:D_FF + (c + 1) * FFN_CHUNK],
                     preferred_element_type=F32)
        act = (fg * jax.nn.sigmoid(fg) * fu).astype(MM_DTYPE)
        acc = acc + jnp.dot(act, wd[cols, :], preferred_element_type=F32)
    y = x + mod_ref[5:6, :] * acc
    if final_norm:
        y = _rms(y) * gf_ref[...]
    out_ref[...] = y


def _merge_ffn(x2d, mod_l, g1, g2, gf, oa, ob, ocs, lses, ws, seq, final_norm):
    n, d = x2d.shape
    tm = ROW_TILE
    per_b = seq // tm
    row = lambda wd: pl.BlockSpec((tm, wd), lambda i: (i, 0))
    vec = _const_spec((1, d))
    wid = C_HPG * HEAD_DIM
    res = [pl.BlockSpec((None, dil, tm // dil, wid), lambda i: (i // per_b, 0, i % per_b, 0))
           for _, dil in C_GROUPS]
    in_specs = [row(d), pl.BlockSpec((None, 6, d), lambda i: (i // per_b, 0, 0)), vec, vec, vec,
                row(oa.shape[1]), row(ob.shape[1])] + res + res
    in_specs += [_const_spec(w.shape) for w in ws]
    return pl.pallas_call(
        functools.partial(_merge_ffn_kernel, final_norm=final_norm),
        out_shape=jax.ShapeDtypeStruct((n, d), F32), grid=(n // tm,),
        in_specs=in_specs, out_specs=row(d),
        scratch_shapes=[pltpu.VMEM((4 * wid // LANES, tm, LANES), F32)],
        compiler_params=_cparams(1), name="merge_ffn",
    )(x2d, mod_l, g1, g2, gf, oa, ob, *ocs, *lses, *ws)


def kernel(x, c, w_ada, b_ada, g_norm1, w_in, w_uk, w_uv, g_kv, lam_q1, lam_k1, lam_q2, lam_k2,
           g_subln, w_branch_a, w_branch_b, w_branch_c, w_out, g_norm2, w_gate_up, w_down,
           rel_bias, g_final):
    bsz, seq, d = x.shape
    depth = w_ada.shape[0]
    T = ATT_TILE
    nk = seq // T
    assert d == D_MODEL and seq % T == 0 and seq % ROW_TILE == 0 and seq % IN_ROW_TILE == 0
    n = bsz * seq
    cast = lambda w: w.astype(MM_DTYPE)

    thresholds = _bucket_thresholds(seq + 2 * C_BAND * C_GROUPS[-1][1])
    assert seq <= N_OFFSETS * T or (N_OFFSETS - 2) * T + 1 >= thresholds[-1]
    tab = rel_bias.reshape(-1)
    bias_a = _bias_att_tiles(tab, 0, A_HEADS, T, thresholds)
    bias_b = _bias_att_tiles(tab, A_HEADS, B_HEADS, T, thresholds)
    bias_c = _bias_dil_tiles(tab, A_HEADS + B_HEADS, thresholds)

    mod = _modulation(c, w_ada, b_ada).reshape(depth, bsz, 6, d)

    splits = (A_HEADS * HEAD_DIM, A_LATENT, IDX_HEADS * IDX_DIM, IDX_DIM, IDX_HEADS,
              B_HEADS * 2 * HEAD_DIM, B_HEADS * 2 * HEAD_DIM, B_HEADS * 2 * HEAD_DIM,
              C_HEADS * HEAD_DIM, C_HEADS * HEAD_DIM, C_HEADS * HEAD_DIM, d, d, d)
    offs = np.concatenate([[0], np.cumsum(splits)])
    seg = lambda w, k: w[:, int(offs[k]):int(offs[k + 1])]

    x2d = x.reshape(n, d)
    for l in range(depth):
        wl = w_in[l]
        wid = C_HPG * HEAD_DIM
        w_iw = jnp.pad(seg(wl, 4), ((0, 0), (0, 16 - IDX_HEADS)))
        wT_all = cast(jnp.concatenate([seg(wl, 0), seg(wl, 2), seg(wl, 5), seg(wl, 7), w_iw], axis=1).T)
        w_all = cast(jnp.concatenate(
            [seg(wl, 1), seg(wl, 6)]
            + [seg(wl, k)[:, g * wid:(g + 1) * wid] for g in range(len(C_GROUPS)) for k in (8, 9, 10)]
            + [seg(wl, 3)], axis=1))
        ws_in = [wT_all, w_all]
        g1 = g_norm1[l].reshape(1, d)
        (aqT, iqT, iwT, bqT, bvT, ik, kv, kvT, bk, cg0, cg1, cg2) = _in_proj(
            x2d, mod[l], g1, g_kv[l].reshape(1, A_LATENT), ws_in, bsz, seq)

        o_a = _dsa(aqT, iqT, iwT,
                   ik.reshape(bsz, nk, T, IDX_DIM), kv.reshape(bsz, nk, T, A_LATENT), kvT,
                   bias_a, cast(w_uk[l].transpose(0, 2, 1)), cast(w_uv[l].transpose(0, 2, 1)))
        o_a = o_a.reshape(n, -1)

        lam_init = 0.8 - 0.6 * math.exp(-0.3 * l)
        lam_rows = jnp.stack([lam_q1[l], lam_k1[l], lam_q2[l], lam_k2[l]])
        dv = 2 * HEAD_DIM
        o_b = _diff(bqT, bk.reshape(bsz, nk, T, B_HEADS * dv), bvT, bias_b, lam_rows,
                    g_subln[l].reshape(dv, 1), lam_init)
        o_b = o_b.reshape(n, -1)

        ocs, lses = [], []
        for g, (cg, (window, dil)) in enumerate(zip((cg0, cg1, cg2), C_GROUPS)):
            assert window // dil == C_BAND
            o, s = _dilated_group(cg, bias_c[g * C_HPG:(g + 1) * C_HPG], g)
            ocs.append(o)
            lses.append(s)

        ws_out = [cast(wl[:, int(offs[11]):int(offs[14])]),
                  cast(w_branch_a[l]), cast(w_branch_b[l]), cast(w_branch_c[l]), cast(w_out[l]),
                  cast(w_gate_up[l]), cast(w_down[l])]
        x2d = _merge_ffn(x2d, mod[l], g1, g_norm2[l].reshape(1, d), g_final.reshape(1, d),
                         o_a, o_b, ocs, lses, ws_out, seq, final_norm=(l == depth - 1))
    return x2d.reshape(bsz, seq, d)
```

```python
import functools
import math

import numpy as np
import jax
import jax.numpy as jnp
from jax import lax
from jax.experimental import pallas as pl
from jax.experimental.pallas import tpu as pltpu

D_MODEL = 1024
HEAD_DIM = 64
ATTN_SCALE = HEAD_DIM ** -0.5
LOG2E = math.log2(math.e)
A_HEADS = 8
A_LATENT = 128
IDX_HEADS = 8
IDX_DIM = 64
IDX_SCALE = (IDX_HEADS * IDX_DIM) ** -0.5
TOPK_MAX = 256
B_HEADS = 4
C_GROUPS = ((128, 1), (512, 4), (2048, 16))
C_HPG = 4
C_HEADS = C_HPG * len(C_GROUPS)
N_BUCKETS = 32
MAX_DISTANCE = 2048
N_BIAS_HEADS = A_HEADS + B_HEADS + C_HEADS
D_FF = -(-8 * D_MODEL // (3 * 256)) * 256
EPS = 1e-6

MM_DTYPE = jnp.bfloat16
F32 = jnp.float32
I32 = jnp.int32

ATT_TILE = 256
N_OFFSETS = 8
C_BAND = 128
ROW_TILE = 512
IN_ROW_TILE = 1024
FFN_CHUNK = 256
MOD_COLS = 1536
NEG = -1e30
INT_MIN = -2 ** 31
LANES = 128
SUBLANES = 8
KEY_BITS = 32
VMEM_LIMIT = 56 * 1024 * 1024


def _cparams(n_axes, vmem=VMEM_LIMIT):
    return pltpu.CompilerParams(dimension_semantics=("arbitrary",) * n_axes,
                                vmem_limit_bytes=vmem)


def _const_spec(shape):
    nd = len(shape)
    return pl.BlockSpec(shape, lambda *_: (0,) * nd, pipeline_mode=pl.Buffered(1))


def _bucket_thresholds(max_dist):
    n = np.arange(max_dist + 1)
    max_exact = N_BUCKETS // 2
    nf = np.maximum(n, 1).astype(np.float32)
    large = max_exact + (np.log(nf / np.float32(max_exact))
                         / np.float32(math.log(MAX_DISTANCE / max_exact))
                         * np.float32(N_BUCKETS - max_exact)).astype(np.int32)
    large = np.minimum(large, N_BUCKETS - 1)
    bucket = np.where(n < max_exact, n, large)
    assert np.all(np.diff(bucket) >= 0)
    thr = []
    for k in range(1, N_BUCKETS):
        idx = np.nonzero(bucket >= k)[0]
        thr.append(int(idx[0]) if idx.size else None)
    return thr


def _bias_from_dist(dist, tab_ref, col, thresholds, lo=0, hi=None):
    reached = [k for k, thr in enumerate(thresholds, start=1) if thr is not None]
    base = max([0] + [k for k in reached if thresholds[k - 1] <= lo])
    b = jnp.full(dist.shape, tab_ref[base * N_BIAS_HEADS + col], F32)
    for k in reached:
        thr = thresholds[k - 1]
        if thr > lo and (hi is None or thr <= hi):
            b = jnp.where(dist >= thr, tab_ref[k * N_BIAS_HEADS + col], b)
    return b


def _bias_att_kernel(tab_ref, out_ref, *, head0, thresholds):
    h = pl.program_id(0)
    n_off, tile, _ = out_ref.shape
    row = lax.broadcasted_iota(I32, (tile, tile), 0)
    colq = lax.broadcasted_iota(I32, (tile, tile), 1)
    for o in range(n_off):
        dist = jnp.maximum(o * tile + colq - row, 0)
        lo, hi = max(o * tile - (tile - 1), 0), o * tile + tile - 1
        b = _bias_from_dist(dist, tab_ref, head0 + h, thresholds, lo, hi) * LOG2E
        if o == 0:
            b = jnp.where(row <= colq, b, NEG)
        out_ref[o] = b


def _bias_att_tiles(tab, head0, n_heads, tile, thresholds):
    return pl.pallas_call(
        functools.partial(_bias_att_kernel, head0=head0, thresholds=thresholds),
        out_shape=jax.ShapeDtypeStruct((n_heads, N_OFFSETS, tile, tile), F32),
        grid=(n_heads,),
        in_specs=[pl.BlockSpec(memory_space=pltpu.SMEM)],
        out_specs=pl.BlockSpec((None, N_OFFSETS, tile, tile), lambda h: (h, 0, 0, 0)),
        compiler_params=_cparams(1),
        name="bias_att_tiles",
    )(tab)


def _bias_dil_kernel(tab_ref, out_ref, *, head0, thresholds):
    h = pl.program_id(0)
    g = h // C_HPG
    dil = jnp.where(g == 0, C_GROUPS[0][1], jnp.where(g == 1, C_GROUPS[1][1], C_GROUPS[2][1]))
    i = lax.broadcasted_iota(I32, (C_BAND, 2 * C_BAND), 0)
    j = lax.broadcasted_iota(I32, (C_BAND, 2 * C_BAND), 1)
    dist = jnp.maximum((i - j + C_BAND) * dil, 0)
    out_ref[...] = _bias_from_dist(dist, tab_ref, head0 + h, thresholds)


def _bias_dil_tiles(tab, head0, thresholds):
    return pl.pallas_call(
        functools.partial(_bias_dil_kernel, head0=head0, thresholds=thresholds),
        out_shape=jax.ShapeDtypeStruct((C_HEADS, C_BAND, 2 * C_BAND), F32),
        grid=(C_HEADS,),
        in_specs=[pl.BlockSpec(memory_space=pltpu.SMEM)],
        out_specs=pl.BlockSpec((None, C_BAND, 2 * C_BAND), lambda h: (h, 0, 0)),
        compiler_params=_cparams(1),
        name="bias_dil_tiles",
    )(tab)


def _mod_kernel(c_ref, w_ref, b_ref, out_ref):
    c = c_ref[...]
    ca = (c * jax.nn.sigmoid(c)).astype(MM_DTYPE)
    out_ref[...] = jnp.dot(ca, w_ref[...].astype(MM_DTYPE), preferred_element_type=F32) + b_ref[...]


def _modulation(c, w_ada, b_ada):
    depth, d, wid = w_ada.shape
    bsz = c.shape[0]
    tn = MOD_COLS
    assert wid % tn == 0
    return pl.pallas_call(
        _mod_kernel,
        out_shape=jax.ShapeDtypeStruct((depth, bsz, wid), F32),
        grid=(depth, wid // tn),
        in_specs=[pl.BlockSpec((bsz, d), lambda l, j: (0, 0)),
                  pl.BlockSpec((None, d, tn), lambda l, j: (l, 0, j)),
                  pl.BlockSpec((None, 1, tn), lambda l, j: (l, 0, j))],
        out_specs=pl.BlockSpec((None, bsz, tn), lambda l, j: (l, 0, j)),
        compiler_params=_cparams(2),
        name="adaln_modulation",
    )(c, w_ada, b_ada.reshape(depth, 1, wid))


def _rms(x):
    return x * lax.rsqrt(jnp.mean(x * x, axis=-1, keepdims=True) + EPS)


IN_ROWS_T = (("aq", A_HEADS * HEAD_DIM), ("iq", IDX_HEADS * IDX_DIM), ("bq", B_HEADS * 2 * HEAD_DIM),
             ("bv", B_HEADS * 2 * HEAD_DIM), ("iw", 16))
IN_COLS = (("kv", A_LATENT), ("bk", B_HEADS * 2 * HEAD_DIM), ("c", 3 * C_HEADS * HEAD_DIM),
           ("ik", IDX_DIM))


def _segments(layout):
    out, start = {}, 0
    for name, width in layout:
        out[name] = slice(start, start + width)
        start += width
    return out


def _in_kernel(x_ref, mod_ref, g1_ref, gkv_ref, wT_ref, w_ref,
               o_aqT, o_iqT, o_iwT, o_bqT, o_bvT, o_ik, o_kv, o_kvT, o_bk, o_c0, o_c1, o_c2,
               c_scr):
    T = ATT_TILE
    tm = x_ref.shape[0]
    h = _rms(x_ref[...]) * g1_ref[...]
    h = h * (1.0 + mod_ref[1:2, :]) + mod_ref[0:1, :]
    hb = h.astype(MM_DTYPE)
    rows_t, cols = _segments(IN_ROWS_T), _segments(IN_COLS)

    def mm(name):
        return jnp.dot(hb, w_ref[:, cols[name]], preferred_element_type=F32)

    def mm_t(name):
        return lax.dot_general(wT_ref[rows_t[name], :], hb, (((1,), (1,)), ((), ())),
                               preferred_element_type=F32)


    o_aqT[...] = mm_t("aq").astype(o_aqT.dtype)
    o_iqT[...] = mm_t("iq").astype(o_iqT.dtype)
    o_bqT[...] = (mm_t("bq") * (ATTN_SCALE * LOG2E)).astype(o_bqT.dtype)
    o_iwT[...] = (mm_t("iw") * IDX_SCALE)[:IDX_HEADS]
    bvT = mm_t("bv").astype(o_bvT.dtype)
    kv = _rms(mm("kv")) * gkv_ref[...]
    kvT = kv.T.astype(o_kvT.dtype)
    for j in range(tm // T):
        o_bvT[j] = bvT[:, j * T:(j + 1) * T]
        o_kvT[j] = kvT[:, j * T:(j + 1) * T]
    o_kv[...] = kv.astype(o_kv.dtype)
    o_ik[...] = mm("ik").astype(o_ik.dtype)
    o_bk[...] = mm("bk").astype(o_bk.dtype)

    yc = mm("c")
    n_chunk = yc.shape[1] // LANES
    for j in range(n_chunk):
        c_scr[j] = yc[:, j * LANES:(j + 1) * LANES]
    per_group = n_chunk // len(C_GROUPS)
    for g, o_c in enumerate((o_c0, o_c1, o_c2)):
        dil = C_GROUPS[g][1]
        for r in range(dil):
            for jj in range(per_group):
                o_c[r, :, jj * LANES:(jj + 1) * LANES] = c_scr[
                    g * per_group + jj, pl.ds(r, tm // dil, stride=dil), :].astype(o_c.dtype)


def _in_proj(x2d, mod_l, g1, gkv, ws, bsz, seq):
    n, d = x2d.shape
    tm = IN_ROW_TILE
    T = ATT_TILE
    per_b = seq // tm
    nk = seq // T
    hd = A_HEADS * HEAD_DIM
    bw = B_HEADS * 2 * HEAD_DIM
    gw = 3 * C_HPG * HEAD_DIM
    in_specs = [pl.BlockSpec((tm, d), lambda i: (i, 0)),
                pl.BlockSpec((None, 6, d), lambda i: (i // per_b, 0, 0)),
                _const_spec((1, d)), _const_spec((1, A_LATENT))]
    in_specs += [_const_spec(w.shape) for w in ws]

    def tspec(rows):
        return pl.BlockSpec((None, rows, tm), lambda i: (i // per_b, 0, i % per_b))

    def tile_tspec(rows):
        return pl.BlockSpec((None, tm // T, rows, T), lambda i: (i // per_b, i % per_b, 0, 0))

    def rspec(wd):
        return pl.BlockSpec((tm, wd), lambda i: (i, 0))

    def cspec(dil):
        return pl.BlockSpec((None, dil, tm // dil, gw), lambda i: (i // per_b, 0, i % per_b, 0))

    sds = jax.ShapeDtypeStruct
    out_specs = [tspec(hd), tspec(IDX_HEADS * IDX_DIM), tspec(IDX_HEADS), tspec(bw),
                 tile_tspec(bw), rspec(IDX_DIM), rspec(A_LATENT), tile_tspec(A_LATENT), rspec(bw)]
    out_shape = [sds((bsz, hd, seq), MM_DTYPE), sds((bsz, IDX_HEADS * IDX_DIM, seq), MM_DTYPE),
                 sds((bsz, IDX_HEADS, seq), F32), sds((bsz, bw, seq), MM_DTYPE),
                 sds((bsz, nk, bw, T), MM_DTYPE), sds((n, IDX_DIM), MM_DTYPE),
                 sds((n, A_LATENT), MM_DTYPE), sds((bsz, nk, A_LATENT, T), MM_DTYPE),
                 sds((n, bw), MM_DTYPE)]
    for _, dil in C_GROUPS:
        out_specs.append(cspec(dil))
        out_shape.append(sds((bsz, dil, seq // dil, gw), MM_DTYPE))
    return pl.pallas_call(
        _in_kernel, out_shape=out_shape, grid=(n // tm,),
        in_specs=in_specs, out_specs=out_specs,
        scratch_shapes=[pltpu.VMEM((3 * gw // LANES, tm, LANES), F32)],
        compiler_params=_cparams(1), name="in_proj",
    )(x2d, mod_l, g1, gkv, *ws)


def _initial_max(n_chains, tile):
    return tuple(jnp.full((1, tile), NEG, F32) for _ in range(n_chains))


def _loop_grouped(n, body, init, log2_group):
    carry, done = init, jnp.int32(0)
    for lg in range(log2_group, -1, -1):
        group = 1 << lg
        trips = lax.shift_right_logical(n - done, jnp.int32(lg))

        def grouped(i, c, group=group, done=done):
            for j in range(group):
                c = body(done + group * i + j, c)
            return c

        carry = lax.fori_loop(0, trips, grouped, carry)
        done = done + group * trips
    return carry


SUM_ROWS = 16


def _with_ones_row(vT):
    row = lax.broadcasted_iota(I32, (SUM_ROWS, vT.shape[1]), 0)
    ones = jnp.where(row == 0, 1.0, 0.0).astype(vT.dtype)
    return jnp.concatenate([vT, ones], axis=0)


def _bit_transpose32(words):
    a = list(words)
    j, mask = 16, 0x0000FFFF
    while j:
        k = 0
        while k < 32:
            t = (a[k] ^ lax.shift_right_logical(a[k + j], jnp.int32(j))) & jnp.int32(mask)
            a[k] = a[k] ^ t
            a[k + j] = a[k + j] ^ lax.shift_left(t, jnp.int32(j))
            k = (k + j + 1) & ~j
        j >>= 1
        mask = (mask ^ (mask << j)) & 0xFFFFFFFF
    return a


def _dsa_kernel(aqT_ref, iqT_ref, iwT_ref, kidx_ref, kv_ref, kvT_ref, bias_ref, wukT_ref, wuvT_ref,
                out_ref, keys_ref, planes_ref, qlat_ref, acc_ref, s_ref, oT_ref, *, topk):
    T = ATT_TILE
    qi = pl.program_id(1)
    nk = qi + 1

    for h in range(A_HEADS):
        q = jnp.dot(wukT_ref[h], aqT_ref[h * HEAD_DIM:(h + 1) * HEAD_DIM, :],
                    preferred_element_type=F32) * (ATTN_SCALE * LOG2E)
        qlat_ref[h] = q.astype(qlat_ref.dtype)

    s_loc = lax.broadcasted_iota(I32, (T, T), 0)
    t_loc = lax.broadcasted_iota(I32, (T, T), 1)

    def score_tile(kj, diagonal):
        kt = kidx_ref[kj]
        acc = jnp.zeros((T, T), F32)
        for h in range(IDX_HEADS):
            s = jnp.dot(kt, iqT_ref[h * IDX_DIM:(h + 1) * IDX_DIM, :], preferred_element_type=F32)
            acc = acc + jnp.maximum(s, 0.0) * iwT_ref[h:h + 1, :]
        bits = lax.bitcast_convert_type(acc, I32)
        key = jnp.where(bits < 0, bits ^ jnp.int32(0x7FFFFFFF), bits)
        if diagonal:
            key = jnp.where(s_loc <= t_loc, key, jnp.int32(INT_MIN))
        keys_ref[kj] = key
        planes = _bit_transpose32([key[SUBLANES * i:SUBLANES * (i + 1), :] ^ jnp.int32(INT_MIN)
                                   for i in range(KEY_BITS)])
        for b in range(KEY_BITS):
            planes_ref[kj, b] = planes[b]

    def off_diagonal(kj, carry):
        score_tile(kj, False)
        return carry

    _loop_grouped(qi, off_diagonal, 0, log2_group=2)
    score_tile(qi, True)

    n_tiles = planes_ref.shape[0]

    @pl.when(qi == 0)
    def _():
        def clear_planes(kj, carry):
            for b in range(KEY_BITS):
                planes_ref[kj, b] = jnp.zeros((SUBLANES, T), I32)
            return carry

        lax.fori_loop(1, n_tiles, clear_planes, 0)

    one = jnp.int32(1)
    nil = jnp.int32(0)
    zero = jnp.zeros((1, T), I32)

    n_rows = n_tiles * SUBLANES
    tile_of_row = lax.broadcasted_iota(I32, (n_rows, T), 0) // SUBLANES
    tied0 = jnp.where(tile_of_row < nk, jnp.int32(-1), nil)

    def bit_step(b, carry):
        tied, n_gt, kth_u = carry
        ones = tied & planes_ref[:, b].reshape(n_rows, T)
        n1 = jnp.sum(lax.population_count(ones), axis=0, keepdims=True)
        take = (n_gt + n1) >= topk
        tied = jnp.where(take, ones, tied ^ ones)
        n_gt = jnp.where(take, n_gt, n_gt + n1)
        kth_u = jnp.where(take, kth_u | lax.shift_left(one, jnp.int32(KEY_BITS - 1) - b), kth_u)
        return tied, n_gt, kth_u

    tied, n_gt, kth_u = lax.fori_loop(0, KEY_BITS, bit_step, (tied0, zero, zero))
    n_eq = jnp.sum(lax.population_count(tied), axis=0, keepdims=True)
    n_ge = jnp.where(kth_u == nil, n_gt, n_gt + n_eq)
    kth = jnp.maximum(kth_u ^ jnp.int32(INT_MIN), jnp.int32(INT_MIN + 1))

    @pl.when(jnp.max(n_ge) > topk)
    def _():
        need = topk - n_gt
        r_bits = SUBLANES.bit_length() - 1

        before, cut_tile, rank = zero, zero, need
        words = jnp.zeros((SUBLANES, T), I32)
        for kj in range(n_tiles):
            tile_words = tied[kj * SUBLANES:(kj + 1) * SUBLANES, :]
            after = before + jnp.sum(lax.population_count(tile_words), axis=0, keepdims=True)
            here = jnp.where(before < need, jnp.where(after >= need, one, nil), nil) == one
            cut_tile = jnp.where(here, jnp.int32(kj), cut_tile)
            rank = jnp.where(here, need - before, rank)
            words = jnp.where(here, tile_words, words)
            before = after

        word_r = lax.broadcasted_iota(I32, (SUBLANES, T), 0)
        local = zero
        for bit in reversed(range(T.bit_length() - 1)):
            cand = local | jnp.int32(1 << bit)
            p_i = lax.shift_right_logical(cand, jnp.int32(r_bits))
            p_r = cand & jnp.int32(SUBLANES - 1)
            above = jnp.where(p_i == nil, nil,
                              lax.shift_left(jnp.full_like(p_i, -1), jnp.int32(KEY_BITS) - p_i))
            at_i = lax.shift_right_logical(
                words, jnp.broadcast_to(jnp.int32(KEY_BITS - 1) - p_i, words.shape)) & one
            below = lax.population_count(words & above) + jnp.where(word_r < p_r, at_i, nil)
            local = jnp.where(jnp.sum(below, axis=0, keepdims=True) < rank, cand, local)
        cut = cut_tile * T + local

        def demote(kj, carry):
            kk = keys_ref[kj]
            lowered = jnp.where((kj * T + s_loc) > cut, kth - one, kk)
            keys_ref[kj] = jnp.where(kk == kth, lowered, kk)
            return carry

        lax.fori_loop(0, nk, demote, 0)

    acc_ref[...] = jnp.zeros(acc_ref.shape, F32)

    def stage_logits(kj):
        unselected = jnp.where(keys_ref[kj] >= kth, 0.0, NEG)
        off = jnp.minimum(qi - kj, N_OFFSETS - 1)
        kvt = kv_ref[kj]

        def one_head(h):
            s = bias_ref[h, off] + jnp.dot(kvt, qlat_ref[h], preferred_element_type=F32)
            s = s + unselected
            s_ref[h] = s
            return jnp.max(s, axis=0, keepdims=True)

        return one_head

    first = stage_logits(0)
    tile_max0 = tuple(first(h) for h in range(A_HEADS))

    def attend(kj, carry):
        ms, tile_max = carry
        kvTt = _with_ones_row(kvT_ref[kj])
        stage_next = stage_logits(jnp.minimum(kj + 1, nk - 1))
        new_m, new_max = [], []
        for h in range(A_HEADS):
            m_new = jnp.maximum(ms[h], tile_max[h])
            p = jnp.exp2(s_ref[h] - m_new).astype(MM_DTYPE)
            alpha = jnp.exp2(ms[h] - m_new)
            new_max.append(stage_next(h))
            acc_ref[h] = alpha * acc_ref[h] + jnp.dot(kvTt, p, preferred_element_type=F32)
            new_m.append(m_new)
        return tuple(new_m), tuple(new_max)

    _loop_grouped(nk, attend, (_initial_max(A_HEADS, T), tile_max0), log2_group=1)

    for h in range(A_HEADS):
        o_lat = (acc_ref[h, 0:A_LATENT, :] / acc_ref[h, A_LATENT:A_LATENT + 1, :]).astype(MM_DTYPE)
        oT_ref[h * HEAD_DIM:(h + 1) * HEAD_DIM, :] = jnp.dot(
            wuvT_ref[h], o_lat, preferred_element_type=F32)
    out_ref[...] = oT_ref[...].T.astype(out_ref.dtype)


def _dsa(aqT, iqT, iwT, kidx, kv, kvT, bias_a, wukT, wuvT):
    bsz, _, seq = aqT.shape
    T = ATT_TILE
    nk = seq // T
    topk = min(TOPK_MAX, seq // 4)
    assert T == KEY_BITS * SUBLANES
    qspec = lambda rows: pl.BlockSpec((None, rows, T), lambda b, i: (b, 0, i))
    kspec = lambda a, c: pl.BlockSpec((None, nk, a, c), lambda b, i: (b, 0, 0, 0))
    return pl.pallas_call(
        functools.partial(_dsa_kernel, topk=topk),
        out_shape=jax.ShapeDtypeStruct((bsz, seq, A_HEADS * HEAD_DIM), MM_DTYPE),
        grid=(bsz, nk),
        in_specs=[qspec(A_HEADS * HEAD_DIM), qspec(IDX_HEADS * IDX_DIM), qspec(IDX_HEADS),
                  kspec(T, IDX_DIM), kspec(T, A_LATENT), kspec(A_LATENT, T),
                  _const_spec(bias_a.shape), _const_spec(wukT.shape), _const_spec(wuvT.shape)],
        out_specs=pl.BlockSpec((None, T, A_HEADS * HEAD_DIM), lambda b, i: (b, i, 0)),
        scratch_shapes=[pltpu.VMEM((nk, T, T), I32),
                        pltpu.VMEM((nk, KEY_BITS + 1, SUBLANES, T), I32),
                        pltpu.VMEM((A_HEADS, A_LATENT, T), MM_DTYPE),
                        pltpu.VMEM((A_HEADS, A_LATENT + SUM_ROWS, T), F32),
                        pltpu.VMEM((A_HEADS, T, T), F32),
                        pltpu.VMEM((A_HEADS * HEAD_DIM, T), F32)],
        compiler_params=_cparams(2), name="dsa_attention",
    )(aqT, iqT, iwT, kidx, kv, kvT, bias_a, wukT, wuvT)


def _diff_kernel(qT_ref, k_ref, vT_ref, bias_ref, lam_ref, gsub_ref, out_ref,
                 qz_ref, acc_ref, s_ref, oT_ref, *, lam_init):
    T = ATT_TILE
    dv = 2 * HEAD_DIM
    n_chain = 2 * B_HEADS
    qi = pl.program_id(1)
    half = lax.broadcasted_iota(I32, (dv, T), 0) < HEAD_DIM
    for h in range(B_HEADS):
        q = qT_ref[h * dv:(h + 1) * dv, :].astype(F32)
        qz_ref[2 * h] = jnp.where(half, q, 0.0).astype(MM_DTYPE)
        qz_ref[2 * h + 1] = jnp.where(half, 0.0, q).astype(MM_DTYPE)
    acc_ref[...] = jnp.zeros(acc_ref.shape, F32)

    def stage_logits(kj, c):
        h = c // 2
        off = jnp.minimum(qi - kj, N_OFFSETS - 1)
        s = bias_ref[h, off] + jnp.dot(k_ref[kj, :, h * dv:(h + 1) * dv], qz_ref[c],
                                       preferred_element_type=F32)
        s_ref[c] = s
        return jnp.max(s, axis=0, keepdims=True)

    tile_max0 = tuple(stage_logits(0, c) for c in range(n_chain))

    def step(kj, carry, last):
        ms, tile_max = carry
        new_m, new_max = [], []
        for c in range(n_chain):
            h = c // 2
            m_new = jnp.maximum(ms[c], tile_max[c])
            p = jnp.exp2(s_ref[c] - m_new).astype(MM_DTYPE)
            alpha = jnp.exp2(ms[c] - m_new)
            if not last:
                new_max.append(stage_logits(kj + 1, c))
            vT = _with_ones_row(vT_ref[kj, h * dv:(h + 1) * dv, :])
            acc_ref[c] = alpha * acc_ref[c] + jnp.dot(vT, p, preferred_element_type=F32)
            new_m.append(m_new)
        return tuple(new_m), tuple(new_max)

    carry = _loop_grouped(qi, lambda kj, cr: step(kj, cr, False),
                          (_initial_max(n_chain, T), tile_max0), log2_group=2)
    step(qi, carry, True)

    lr = lam_ref[...]
    lam = (jnp.exp(jnp.sum(lr[0:1, :] * lr[1:2, :], axis=1, keepdims=True))
           - jnp.exp(jnp.sum(lr[2:3, :] * lr[3:4, :], axis=1, keepdims=True)) + lam_init)

    def normalised(c):
        return acc_ref[c, 0:dv, :] / acc_ref[c, dv:dv + 1, :]

    for h in range(B_HEADS):
        attn = normalised(2 * h) - lam * normalised(2 * h + 1)
        y = attn * lax.rsqrt(jnp.mean(attn * attn, axis=0, keepdims=True) + EPS)
        oT_ref[h * dv:(h + 1) * dv, :] = y * gsub_ref[...] * (1.0 - lam_init)
    out_ref[...] = oT_ref[...].T.astype(out_ref.dtype)


def _diff(bqT, bk, bvT, bias_b, lam_rows, gsub, lam_init):
    bsz, _, seq = bqT.shape
    T = ATT_TILE
    nk = seq // T
    dv = 2 * HEAD_DIM
    qspec = pl.BlockSpec((None, B_HEADS * dv, T), lambda b, i: (b, 0, i))
    return pl.pallas_call(
        functools.partial(_diff_kernel, lam_init=lam_init),
        out_shape=jax.ShapeDtypeStruct((bsz, seq, B_HEADS * dv), MM_DTYPE),
        grid=(bsz, nk),
        in_specs=[qspec,
                  pl.BlockSpec((None, nk, T, B_HEADS * dv), lambda b, i: (b, 0, 0, 0)),
                  pl.BlockSpec((None, nk, B_HEADS * dv, T), lambda b, i: (b, 0, 0, 0)),
                  _const_spec(bias_b.shape), _const_spec((4, HEAD_DIM)), _const_spec((dv, 1))],
        out_specs=pl.BlockSpec((None, T, B_HEADS * dv), lambda b, i: (b, i, 0)),
        scratch_shapes=[pltpu.VMEM((2 * B_HEADS, dv, T), MM_DTYPE),
                        pltpu.VMEM((2 * B_HEADS, dv + SUM_ROWS, T), F32),
                        pltpu.VMEM((2 * B_HEADS, T, T), F32),
                        pltpu.VMEM((B_HEADS * dv, T), F32)],
        compiler_params=_cparams(2), name="diff_attention",
    )(bqT, bk, bvT, bias_b, lam_rows, gsub)


def _dil_kernel(cur_ref, halo_ref, bias_ref, out_ref, lse_ref, *, tq):
    n = C_BAND
    wid = C_HPG * HEAD_DIM
    halo_lo = jnp.where(pl.program_id(2) == 0, jnp.int32(n), jnp.int32(0))
    i = lax.broadcasted_iota(I32, (n, 2 * n), 0)
    j = lax.broadcasted_iota(I32, (n, 2 * n), 1)
    lane_head = lax.broadcasted_iota(I32, (n, wid), 1) // HEAD_DIM
    in_head = [lane_head == h for h in range(C_HPG)]
    band = jnp.where(j >= i, jnp.where(j <= i + n, 0.0, NEG), NEG)
    band0 = jnp.where(j >= jnp.maximum(i, halo_lo), jnp.where(j <= i + n, 0.0, NEG), NEG)
    bias = [bias_ref[h] + band for h in range(C_HPG)]
    bias0 = [bias_ref[h] + band0 for h in range(C_HPG)]

    def band_rows(r, c, lo, hi):
        if c == 0:
            return jnp.concatenate([halo_ref[r, :, lo:hi], cur_ref[r, 0:n, lo:hi]], axis=0)
        return cur_ref[r, (c - 1) * n:(c + 1) * n, lo:hi]

    blocks = [(r, c) for r in range(cur_ref.shape[0]) for c in range(tq // n)]
    logits = []
    for r, c in blocks:
        q = cur_ref[r, c * n:(c + 1) * n, 0:wid].astype(F32) * ATTN_SCALE
        keys = band_rows(r, c, wid, 2 * wid)
        for h in range(C_HPG):
            qh = jnp.where(in_head[h], q, 0.0).astype(MM_DTYPE)
            s = lax.dot_general(qh, keys, (((1,), (1,)), ((), ())), preferred_element_type=F32)
            logits.append(s + (bias0 if c == 0 else bias)[h])
    for k, (r, c) in enumerate(blocks):
        vals = band_rows(r, c, 2 * wid, 3 * wid)
        out = jnp.zeros((n, wid), F32)
        lse = jnp.zeros((n, wid), F32)
        for h in range(C_HPG):
            s = logits[k * C_HPG + h]
            m = jnp.max(s, axis=1, keepdims=True)
            p = jnp.exp(s - m)
            den = jnp.sum(p, axis=1, keepdims=True)
            o = jnp.dot(p.astype(MM_DTYPE), vals, preferred_element_type=F32) * (1.0 / den)
            out = jnp.where(in_head[h], o, out)
            lse = jnp.where(in_head[h], m + jnp.log(den), lse)
        out_ref[r, c * n:(c + 1) * n, :] = out
        lse_ref[r, c * n:(c + 1) * n, :] = lse


def _dilated_group(cg, bias_g, g):
    bsz, dil, m, gw = cg.shape
    wid = C_HPG * HEAD_DIM
    n = C_BAND
    assert m % n == 0 and gw == 3 * wid
    tq = min(m, 512)
    n_res = max(1, min(dil, 512 // tq))
    cur = pl.BlockSpec((None, n_res, tq, gw), lambda b, r, i: (b, r, i, 0))
    halo = pl.BlockSpec((None, n_res, n, gw),
                        lambda b, r, i: (b, r, jnp.maximum(i * (tq // n) - 1, 0), 0))
    outspec = pl.BlockSpec((None, n_res, tq, wid), lambda b, r, i: (b, r, i, 0))
    return pl.pallas_call(
        functools.partial(_dil_kernel, tq=tq),
        out_shape=[jax.ShapeDtypeStruct((bsz, dil, m, wid), F32)] * 2,
        grid=(bsz, dil // n_res, m // tq),
        in_specs=[cur, halo, pl.BlockSpec((C_HPG, n, 2 * n), lambda b, r, i: (0, 0, 0))],
        out_specs=[outspec, outspec],
        compiler_params=_cparams(3), name=f"dilated_group{g}",
    )(cg, cg, bias_g)


def _merge_ffn_kernel(x_ref, mod_ref, g1_ref, g2_ref, gf_ref, oa_ref, ob_ref,
                      c0_ref, c1_ref, c2_ref, s0_ref, s1_ref, s2_ref,
                      wz, wba, wbb, wbc, wo, wgu, wd, out_ref, tok_ref, *, final_norm):
    x = x_ref[...]
    tm = x.shape[0]
    h = _rms(x) * g1_ref[...]
    h = h * (1.0 + mod_ref[1:2, :]) + mod_ref[0:1, :]
    hb = h.astype(MM_DTYPE)

    def token_order(k, ref):
        dil, _, w = ref.shape
        if dil == 1:
            return ref[0]
        n_chunk = w // LANES
        for r in range(dil):
            for j in range(n_chunk):
                tok_ref[k * n_chunk + j, pl.ds(r, tm // dil, stride=dil), :] = ref[
                    r, :, j * LANES:(j + 1) * LANES]
        return jnp.concatenate([tok_ref[k * n_chunk + j] for j in range(n_chunk)], axis=1)

    s0, s1, s2 = s0_ref[0], token_order(0, s1_ref), token_order(1, s2_ref)
    c0, c1, c2 = c0_ref[0], token_order(2, c1_ref), token_order(3, c2_ref)
    mx = jnp.maximum(jnp.maximum(s0, s1), s2)
    e0, e1, e2 = jnp.exp(s0 - mx), jnp.exp(s1 - mx), jnp.exp(s2 - mx)
    oc = (e0 * c0 + e1 * c1 + e2 * c2) / (e0 + e1 + e2)

    d = x.shape[1]

    def gated(k, o, wb):
        z = jnp.dot(hb, wz[:, k * d:(k + 1) * d], preferred_element_type=F32)
        return jax.nn.sigmoid(z) * jnp.dot(o, wb[...], preferred_element_type=F32)

    merged = (gated(0, oa_ref[...], wba) + gated(1, ob_ref[...], wbb)
              + gated(2, oc.astype(MM_DTYPE), wbc))
    y = jnp.dot(merged.astype(MM_DTYPE), wo[...], preferred_element_type=F32)
    x = x + mod_ref[2:3, :] * y

    h = _rms(x) * g2_ref[...]
    h = h * (1.0 + mod_ref[4:5, :]) + mod_ref[3:4, :]
    hb = h.astype(MM_DTYPE)
    acc = jnp.zeros(x.shape, F32)
    for c in range(D_FF // FFN_CHUNK):
        cols = slice(c * FFN_CHUNK, (c + 1) * FFN_CHUNK)
        fg = jnp.dot(hb, wgu[:, cols], preferred_element_type=F32)
        fu = jnp.dot(hb, wgu[:, D_FF + c * FFN_CHUNK:D_FF + (c + 1) * FFN_CHUNK],
                     preferred_element_type=F32)
        act = (fg * jax.nn.sigmoid(fg) * fu).astype(MM_DTYPE)
        acc = acc + jnp.dot(act, wd[cols, :], preferred_element_type=F32)
    y = x + mod_ref[5:6, :] * acc
    if final_norm:
        y = _rms(y) * gf_ref[...]
    out_ref[...] = y


def _merge_ffn(x2d, mod_l, g1, g2, gf, oa, ob, ocs, lses, ws, seq, final_norm):
    n, d = x2d.shape
    tm = ROW_TILE
    per_b = seq // tm
    row = lambda wd: pl.BlockSpec((tm, wd), lambda i: (i, 0))
    vec = _const_spec((1, d))
    wid = C_HPG * HEAD_DIM
    res = [pl.BlockSpec((None, dil, tm // dil, wid), lambda i: (i // per_b, 0, i % per_b, 0))
           for _, dil in C_GROUPS]
    in_specs = [row(d), pl.BlockSpec((None, 6, d), lambda i: (i // per_b, 0, 0)), vec, vec, vec,
                row(oa.shape[1]), row(ob.shape[1])] + res + res
    in_specs += [_const_spec(w.shape) for w in ws]
    return pl.pallas_call(
        functools.partial(_merge_ffn_kernel, final_norm=final_norm),
        out_shape=jax.ShapeDtypeStruct((n, d), F32), grid=(n // tm,),
        in_specs=in_specs, out_specs=row(d),
        scratch_shapes=[pltpu.VMEM((4 * wid // LANES, tm, LANES), F32)],
        compiler_params=_cparams(1), name="merge_ffn",
    )(x2d, mod_l, g1, g2, gf, oa, ob, *ocs, *lses, *ws)


def kernel(x, c, w_ada, b_ada, g_norm1, w_in, w_uk, w_uv, g_kv, lam_q1, lam_k1, lam_q2, lam_k2,
           g_subln, w_branch_a, w_branch_b, w_branch_c, w_out, g_norm2, w_gate_up, w_down,
           rel_bias, g_final):
    bsz, seq, d = x.shape
    depth = w_ada.shape[0]
    T = ATT_TILE
    nk = seq // T
    assert d == D_MODEL and seq % T == 0 and seq % ROW_TILE == 0 and seq % IN_ROW_TILE == 0
    n = bsz * seq
    cast = lambda w: w.astype(MM_DTYPE)

    thresholds = _bucket_thresholds(seq + 2 * C_BAND * C_GROUPS[-1][1])
    assert seq <= N_OFFSETS * T or (N_OFFSETS - 2) * T + 1 >= thresholds[-1]
    tab = rel_bias.reshape(-1)
    bias_a = _bias_att_tiles(tab, 0, A_HEADS, T, thresholds)
    bias_b = _bias_att_tiles(tab, A_HEADS, B_HEADS, T, thresholds)
    bias_c = _bias_dil_tiles(tab, A_HEADS + B_HEADS, thresholds)

    mod = _modulation(c, w_ada, b_ada).reshape(depth, bsz, 6, d)

    splits = (A_HEADS * HEAD_DIM, A_LATENT, IDX_HEADS * IDX_DIM, IDX_DIM, IDX_HEADS,
              B_HEADS * 2 * HEAD_DIM, B_HEADS * 2 * HEAD_DIM, B_HEADS * 2 * HEAD_DIM,
              C_HEADS * HEAD_DIM, C_HEADS * HEAD_DIM, C_HEADS * HEAD_DIM, d, d, d)
    offs = np.concatenate([[0], np.cumsum(splits)])
    seg = lambda w, k: w[:, int(offs[k]):int(offs[k + 1])]

    x2d = x.reshape(n, d)
    for l in range(depth):
        wl = w_in[l]
        wid = C_HPG * HEAD_DIM
        n_grp = len(C_GROUPS)
        w_iw = jnp.pad(seg(wl, 4), ((0, 0), (0, 16 - IDX_HEADS)))
        wT_all = cast(jnp.concatenate([seg(wl, 0), seg(wl, 2), seg(wl, 5), seg(wl, 7), w_iw], axis=1).T)
        w_qkv = wl[:, int(offs[8]):int(offs[11])].reshape(d, 3, n_grp, wid)
        w_c = w_qkv.transpose(0, 2, 1, 3).reshape(d, 3 * n_grp * wid)
        w_all = cast(jnp.concatenate([seg(wl, 1), seg(wl, 6), w_c, seg(wl, 3)], axis=1))
        ws_in = [wT_all, w_all]
        g1 = g_norm1[l].reshape(1, d)
        (aqT, iqT, iwT, bqT, bvT, ik, kv, kvT, bk, cg0, cg1, cg2) = _in_proj(
            x2d, mod[l], g1, g_kv[l].reshape(1, A_LATENT), ws_in, bsz, seq)

        o_a = _dsa(aqT, iqT, iwT,
                   ik.reshape(bsz, nk, T, IDX_DIM), kv.reshape(bsz, nk, T, A_LATENT), kvT,
                   bias_a, cast(w_uk[l].transpose(0, 2, 1)), cast(w_uv[l].transpose(0, 2, 1)))
        o_a = o_a.reshape(n, -1)

        lam_init = 0.8 - 0.6 * math.exp(-0.3 * l)
        lam_rows = jnp.stack([lam_q1[l], lam_k1[l], lam_q2[l], lam_k2[l]])
        dv = 2 * HEAD_DIM
        o_b = _diff(bqT, bk.reshape(bsz, nk, T, B_HEADS * dv), bvT, bias_b, lam_rows,
                    g_subln[l].reshape(dv, 1), lam_init)
        o_b = o_b.reshape(n, -1)

        ocs, lses = [], []
        for g, (cg, (window, dil)) in enumerate(zip((cg0, cg1, cg2), C_GROUPS)):
            assert window // dil == C_BAND
            o, s = _dilated_group(cg, bias_c[g * C_HPG:(g + 1) * C_HPG], g)
            ocs.append(o)
            lses.append(s)

        ws_out = [cast(wl[:, int(offs[11]):int(offs[14])]),
                  cast(w_branch_a[l]), cast(w_branch_b[l]), cast(w_branch_c[l]), cast(w_out[l]),
                  cast(w_gate_up[l]), cast(w_down[l])]
        x2d = _merge_ffn(x2d, mod[l], g1, g_norm2[l].reshape(1, d), g_final.reshape(1, d),
                         o_a, o_b, ocs, lses, ws_out, seq, final_norm=(l == depth - 1))
    return x2d.reshape(bsz, seq, d)
```

```python
import functools
import math

import numpy as np
import jax
import jax.numpy as jnp
from jax import lax
from jax.experimental import pallas as pl
from jax.experimental.pallas import tpu as pltpu

D_MODEL = 1024
HEAD_DIM = 64
ATTN_SCALE = HEAD_DIM ** -0.5
LOG2E = math.log2(math.e)
A_HEADS = 8
A_LATENT = 128
IDX_HEADS = 8
IDX_DIM = 64
IDX_SCALE = (IDX_HEADS * IDX_DIM) ** -0.5
TOPK_MAX = 256
B_HEADS = 4
C_GROUPS = ((128, 1), (512, 4), (2048, 16))
C_HPG = 4
C_HEADS = C_HPG * len(C_GROUPS)
N_BUCKETS = 32
MAX_DISTANCE = 2048
N_BIAS_HEADS = A_HEADS + B_HEADS + C_HEADS
D_FF = -(-8 * D_MODEL // (3 * 256)) * 256
EPS = 1e-6

MM_DTYPE = jnp.bfloat16
F32 = jnp.float32
I32 = jnp.int32

ATT_TILE = 256
N_OFFSETS = 8
C_BAND = 128
ROW_TILE = 512
IN_ROW_TILE = 1024
FFN_CHUNK = 256
MOD_COLS = 1536
NEG = -1e30
INT_MIN = -2 ** 31
LANES = 128
SUBLANES = 8
KEY_BITS = 32
VMEM_LIMIT = 56 * 1024 * 1024


def _cparams(n_axes, vmem=VMEM_LIMIT):
    return pltpu.CompilerParams(dimension_semantics=("arbitrary",) * n_axes,
                                vmem_limit_bytes=vmem)


def _const_spec(shape):
    nd = len(shape)
    return pl.BlockSpec(shape, lambda *_: (0,) * nd, pipeline_mode=pl.Buffered(1))


def _bucket_thresholds(max_dist):
    n = np.arange(max_dist + 1)
    max_exact = N_BUCKETS // 2
    nf = np.maximum(n, 1).astype(np.float32)
    large = max_exact + (np.log(nf / np.float32(max_exact))
                         / np.float32(math.log(MAX_DISTANCE / max_exact))
                         * np.float32(N_BUCKETS - max_exact)).astype(np.int32)
    large = np.minimum(large, N_BUCKETS - 1)
    bucket = np.where(n < max_exact, n, large)
    assert np.all(np.diff(bucket) >= 0)
    thr = []
    for k in range(1, N_BUCKETS):
        idx = np.nonzero(bucket >= k)[0]
        thr.append(int(idx[0]) if idx.size else None)
    return thr


def _bias_from_dist(dist, tab_ref, col, thresholds, lo=0, hi=None):
    reached = [k for k, thr in enumerate(thresholds, start=1) if thr is not None]
    base = max([0] + [k for k in reached if thresholds[k - 1] <= lo])
    b = jnp.full(dist.shape, tab_ref[base * N_BIAS_HEADS + col], F32)
    for k in reached:
        thr = thresholds[k - 1]
        if thr > lo and (hi is None or thr <= hi):
            b = jnp.where(dist >= thr, tab_ref[k * N_BIAS_HEADS + col], b)
    return b


def _bias_att_kernel(tab_ref, out_ref, *, head0, thresholds):
    h = pl.program_id(0)
    n_off, tile, _ = out_ref.shape
    row = lax.broadcasted_iota(I32, (tile, tile), 0)
    colq = lax.broadcasted_iota(I32, (tile, tile), 1)
    for o in range(n_off):
        dist = jnp.maximum(o * tile + colq - row, 0)
        lo, hi = max(o * tile - (tile - 1), 0), o * tile + tile - 1
        b = _bias_from_dist(dist, tab_ref, head0 + h, thresholds, lo, hi) * LOG2E
        if o == 0:
            b = jnp.where(row <= colq, b, NEG)
        out_ref[o] = b


def _bias_att_tiles(tab, head0, n_heads, tile, thresholds):
    return pl.pallas_call(
        functools.partial(_bias_att_kernel, head0=head0, thresholds=thresholds),
        out_shape=jax.ShapeDtypeStruct((n_heads, N_OFFSETS, tile, tile), F32),
        grid=(n_heads,),
        in_specs=[pl.BlockSpec(memory_space=pltpu.SMEM)],
        out_specs=pl.BlockSpec((None, N_OFFSETS, tile, tile), lambda h: (h, 0, 0, 0)),
        compiler_params=_cparams(1),
        name="bias_att_tiles",
    )(tab)


def _bias_dil_kernel(tab_ref, out_ref, *, head0, thresholds):
    h = pl.program_id(0)
    g = h // C_HPG
    dil = jnp.where(g == 0, C_GROUPS[0][1], jnp.where(g == 1, C_GROUPS[1][1], C_GROUPS[2][1]))
    i = lax.broadcasted_iota(I32, (C_BAND, 2 * C_BAND), 0)
    j = lax.broadcasted_iota(I32, (C_BAND, 2 * C_BAND), 1)
    dist = jnp.maximum((i - j + C_BAND) * dil, 0)
    out_ref[...] = _bias_from_dist(dist, tab_ref, head0 + h, thresholds) * LOG2E


def _bias_dil_tiles(tab, head0, thresholds):
    return pl.pallas_call(
        functools.partial(_bias_dil_kernel, head0=head0, thresholds=thresholds),
        out_shape=jax.ShapeDtypeStruct((C_HEADS, C_BAND, 2 * C_BAND), F32),
        grid=(C_HEADS,),
        in_specs=[pl.BlockSpec(memory_space=pltpu.SMEM)],
        out_specs=pl.BlockSpec((None, C_BAND, 2 * C_BAND), lambda h: (h, 0, 0)),
        compiler_params=_cparams(1),
        name="bias_dil_tiles",
    )(tab)


def _mod_kernel(c_ref, w_ref, b_ref, out_ref):
    c = c_ref[...]
    ca = (c * jax.nn.sigmoid(c)).astype(MM_DTYPE)
    out_ref[...] = jnp.dot(ca, w_ref[...].astype(MM_DTYPE), preferred_element_type=F32) + b_ref[...]


def _modulation(c, w_ada, b_ada):
    depth, d, wid = w_ada.shape
    bsz = c.shape[0]
    tn = MOD_COLS
    assert wid % tn == 0
    return pl.pallas_call(
        _mod_kernel,
        out_shape=jax.ShapeDtypeStruct((depth, bsz, wid), F32),
        grid=(depth, wid // tn),
        in_specs=[pl.BlockSpec((bsz, d), lambda l, j: (0, 0)),
                  pl.BlockSpec((None, d, tn), lambda l, j: (l, 0, j)),
                  pl.BlockSpec((None, 1, tn), lambda l, j: (l, 0, j))],
        out_specs=pl.BlockSpec((None, bsz, tn), lambda l, j: (l, 0, j)),
        compiler_params=_cparams(2),
        name="adaln_modulation",
    )(c, w_ada, b_ada.reshape(depth, 1, wid))


def _rms(x):
    return x * lax.rsqrt(jnp.mean(x * x, axis=-1, keepdims=True) + EPS)


IN_ROWS_T = (("aq", A_HEADS * HEAD_DIM), ("iq", IDX_HEADS * IDX_DIM), ("bq", B_HEADS * 2 * HEAD_DIM),
             ("bv", B_HEADS * 2 * HEAD_DIM), ("iw", 16))
IN_COLS = (("kv", A_LATENT), ("bk", B_HEADS * 2 * HEAD_DIM), ("c", 3 * C_HEADS * HEAD_DIM),
           ("ik", IDX_DIM))


def _segments(layout):
    out, start = {}, 0
    for name, width in layout:
        out[name] = slice(start, start + width)
        start += width
    return out


def _in_kernel(x_ref, mod_ref, g1_ref, gkv_ref, wT_ref, w_ref,
               o_aqT, o_iqT, o_iwT, o_bqT, o_bvT, o_ik, o_kv, o_kvT, o_bk, o_c0, o_c1, o_c2,
               c_scr):
    T = ATT_TILE
    tm = x_ref.shape[0]
    h = _rms(x_ref[...]) * g1_ref[...]
    h = h * (1.0 + mod_ref[1:2, :]) + mod_ref[0:1, :]
    hb = h.astype(MM_DTYPE)
    rows_t, cols = _segments(IN_ROWS_T), _segments(IN_COLS)

    def mm(name):
        return jnp.dot(hb, w_ref[:, cols[name]], preferred_element_type=F32)

    def mm_t(name):
        return lax.dot_general(wT_ref[rows_t[name], :], hb, (((1,), (1,)), ((), ())),
                               preferred_element_type=F32)


    o_aqT[...] = mm_t("aq").astype(o_aqT.dtype)
    o_iqT[...] = mm_t("iq").astype(o_iqT.dtype)
    o_bqT[...] = (mm_t("bq") * (ATTN_SCALE * LOG2E)).astype(o_bqT.dtype)
    o_iwT[...] = (mm_t("iw") * IDX_SCALE)[:IDX_HEADS]
    bvT = mm_t("bv").astype(o_bvT.dtype)
    kv = _rms(mm("kv")) * gkv_ref[...]
    kvT = kv.T.astype(o_kvT.dtype)
    for j in range(tm // T):
        o_bvT[j] = bvT[:, j * T:(j + 1) * T]
        o_kvT[j] = kvT[:, j * T:(j + 1) * T]
    o_kv[...] = kv.astype(o_kv.dtype)
    o_ik[...] = mm("ik").astype(o_ik.dtype)
    o_bk[...] = mm("bk").astype(o_bk.dtype)

    yc = mm("c")
    n_chunk = yc.shape[1] // LANES
    for j in range(n_chunk):
        c_scr[j] = yc[:, j * LANES:(j + 1) * LANES]
    per_group = n_chunk // len(C_GROUPS)
    for g, o_c in enumerate((o_c0, o_c1, o_c2)):
        dil = C_GROUPS[g][1]
        for r in range(dil):
            for jj in range(per_group):
                o_c[r, :, jj * LANES:(jj + 1) * LANES] = c_scr[
                    g * per_group + jj, pl.ds(r, tm // dil, stride=dil), :].astype(o_c.dtype)


def _in_proj(x2d, mod_l, g1, gkv, ws, bsz, seq):
    n, d = x2d.shape
    tm = IN_ROW_TILE
    T = ATT_TILE
    per_b = seq // tm
    nk = seq // T
    hd = A_HEADS * HEAD_DIM
    bw = B_HEADS * 2 * HEAD_DIM
    gw = 3 * C_HPG * HEAD_DIM
    in_specs = [pl.BlockSpec((tm, d), lambda i: (i, 0)),
                pl.BlockSpec((None, 6, d), lambda i: (i // per_b, 0, 0)),
                _const_spec((1, d)), _const_spec((1, A_LATENT))]
    in_specs += [_const_spec(w.shape) for w in ws]

    def tspec(rows):
        return pl.BlockSpec((None, rows, tm), lambda i: (i // per_b, 0, i % per_b))

    def tile_tspec(rows):
        return pl.BlockSpec((None, tm // T, rows, T), lambda i: (i // per_b, i % per_b, 0, 0))

    def rspec(wd):
        return pl.BlockSpec((tm, wd), lambda i: (i, 0))

    def cspec(dil):
        return pl.BlockSpec((None, dil, tm // dil, gw), lambda i: (i // per_b, 0, i % per_b, 0))

    sds = jax.ShapeDtypeStruct
    out_specs = [tspec(hd), tspec(IDX_HEADS * IDX_DIM), tspec(IDX_HEADS), tspec(bw),
                 tile_tspec(bw), rspec(IDX_DIM), rspec(A_LATENT), tile_tspec(A_LATENT), rspec(bw)]
    out_shape = [sds((bsz, hd, seq), MM_DTYPE), sds((bsz, IDX_HEADS * IDX_DIM, seq), MM_DTYPE),
                 sds((bsz, IDX_HEADS, seq), F32), sds((bsz, bw, seq), MM_DTYPE),
                 sds((bsz, nk, bw, T), MM_DTYPE), sds((n, IDX_DIM), MM_DTYPE),
                 sds((n, A_LATENT), MM_DTYPE), sds((bsz, nk, A_LATENT, T), MM_DTYPE),
                 sds((n, bw), MM_DTYPE)]
    for _, dil in C_GROUPS:
        out_specs.append(cspec(dil))
        out_shape.append(sds((bsz, dil, seq // dil, gw), MM_DTYPE))
    return pl.pallas_call(
        _in_kernel, out_shape=out_shape, grid=(n // tm,),
        in_specs=in_specs, out_specs=out_specs,
        scratch_shapes=[pltpu.VMEM((3 * gw // LANES, tm, LANES), F32)],
        compiler_params=_cparams(1), name="in_proj",
    )(x2d, mod_l, g1, gkv, *ws)


def _initial_max(n_chains, tile):
    return tuple(jnp.full((1, tile), NEG, F32) for _ in range(n_chains))


def _loop_grouped(n, body, init, log2_group):
    carry, done = init, jnp.int32(0)
    for lg in range(log2_group, -1, -1):
        group = 1 << lg
        trips = lax.shift_right_logical(n - done, jnp.int32(lg))

        def grouped(i, c, group=group, done=done):
            for j in range(group):
                c = body(done + group * i + j, c)
            return c

        carry = lax.fori_loop(0, trips, grouped, carry)
        done = done + group * trips
    return carry


SUM_ROWS = 16


def _with_ones_row(vT):
    row = lax.broadcasted_iota(I32, (SUM_ROWS, vT.shape[1]), 0)
    ones = jnp.where(row == 0, 1.0, 0.0).astype(vT.dtype)
    return jnp.concatenate([vT, ones], axis=0)


def _bit_transpose32(words):
    a = list(words)
    j, mask = 16, 0x0000FFFF
    while j:
        k = 0
        while k < 32:
            t = (a[k] ^ lax.shift_right_logical(a[k + j], jnp.int32(j))) & jnp.int32(mask)
            a[k] = a[k] ^ t
            a[k + j] = a[k + j] ^ lax.shift_left(t, jnp.int32(j))
            k = (k + j + 1) & ~j
        j >>= 1
        mask = (mask ^ (mask << j)) & 0xFFFFFFFF
    return a


def _dsa_kernel(aqT_ref, iqT_ref, iwT_ref, kidx_ref, kv_ref, kvT_ref, bias_ref, wukT_ref, wuvT_ref,
                out_ref, keys_ref, planes_ref, qlat_ref, acc_ref, s_ref, oT_ref, *, topk):
    T = ATT_TILE
    qi = pl.program_id(1)
    nk = qi + 1

    for h in range(A_HEADS):
        q = jnp.dot(wukT_ref[h], aqT_ref[h * HEAD_DIM:(h + 1) * HEAD_DIM, :],
                    preferred_element_type=F32) * (ATTN_SCALE * LOG2E)
        qlat_ref[h] = q.astype(qlat_ref.dtype)

    s_loc = lax.broadcasted_iota(I32, (T, T), 0)
    t_loc = lax.broadcasted_iota(I32, (T, T), 1)

    def score_tile(kj, diagonal):
        kt = kidx_ref[kj]
        acc = jnp.zeros((T, T), F32)
        for h in range(IDX_HEADS):
            s = jnp.dot(kt, iqT_ref[h * IDX_DIM:(h + 1) * IDX_DIM, :], preferred_element_type=F32)
            acc = acc + jnp.maximum(s, 0.0) * iwT_ref[h:h + 1, :]
        bits = lax.bitcast_convert_type(acc, I32)
        key = jnp.where(bits < 0, bits ^ jnp.int32(0x7FFFFFFF), bits)
        if diagonal:
            key = jnp.where(s_loc <= t_loc, key, jnp.int32(INT_MIN))
        keys_ref[kj] = key
        planes = _bit_transpose32([key[SUBLANES * i:SUBLANES * (i + 1), :] ^ jnp.int32(INT_MIN)
                                   for i in range(KEY_BITS)])
        for b in range(KEY_BITS):
            planes_ref[kj, b] = planes[b]

    def off_diagonal(kj, carry):
        score_tile(kj, False)
        return carry

    _loop_grouped(qi, off_diagonal, 0, log2_group=2)
    score_tile(qi, True)

    n_tiles = planes_ref.shape[0]

    @pl.when(qi == 0)
    def _():
        def clear_planes(kj, carry):
            for b in range(KEY_BITS):
                planes_ref[kj, b] = jnp.zeros((SUBLANES, T), I32)
            return carry

        lax.fori_loop(1, n_tiles, clear_planes, 0)

    one = jnp.int32(1)
    nil = jnp.int32(0)
    zero = jnp.zeros((1, T), I32)

    n_rows = n_tiles * SUBLANES
    tile_of_row = lax.broadcasted_iota(I32, (n_rows, T), 0) // SUBLANES
    tied0 = jnp.where(tile_of_row < nk, jnp.int32(-1), nil)

    def bit_step(b, carry):
        tied, n_gt, kth_u = carry
        ones = tied & planes_ref[:, b].reshape(n_rows, T)
        n1 = jnp.sum(lax.population_count(ones), axis=0, keepdims=True)
        take = (n_gt + n1) >= topk
        tied = jnp.where(take, ones, tied ^ ones)
        n_gt = jnp.where(take, n_gt, n_gt + n1)
        kth_u = jnp.where(take, kth_u | lax.shift_left(one, jnp.int32(KEY_BITS - 1) - b), kth_u)
        return tied, n_gt, kth_u

    tied, n_gt, kth_u = lax.fori_loop(0, KEY_BITS, bit_step, (tied0, zero, zero))
    n_eq = jnp.sum(lax.population_count(tied), axis=0, keepdims=True)
    n_ge = jnp.where(kth_u == nil, n_gt, n_gt + n_eq)
    kth = jnp.maximum(kth_u ^ jnp.int32(INT_MIN), jnp.int32(INT_MIN + 1))

    @pl.when(jnp.max(n_ge) > topk)
    def _():
        need = topk - n_gt
        r_bits = SUBLANES.bit_length() - 1

        before, cut_tile, rank = zero, zero, need
        words = jnp.zeros((SUBLANES, T), I32)
        for kj in range(n_tiles):
            tile_words = tied[kj * SUBLANES:(kj + 1) * SUBLANES, :]
            after = before + jnp.sum(lax.population_count(tile_words), axis=0, keepdims=True)
            here = jnp.where(before < need, jnp.where(after >= need, one, nil), nil) == one
            cut_tile = jnp.where(here, jnp.int32(kj), cut_tile)
            rank = jnp.where(here, need - before, rank)
            words = jnp.where(here, tile_words, words)
            before = after

        word_r = lax.broadcasted_iota(I32, (SUBLANES, T), 0)
        local = zero
        for bit in reversed(range(T.bit_length() - 1)):
            cand = local | jnp.int32(1 << bit)
            p_i = lax.shift_right_logical(cand, jnp.int32(r_bits))
            p_r = cand & jnp.int32(SUBLANES - 1)
            above = jnp.where(p_i == nil, nil,
                              lax.shift_left(jnp.full_like(p_i, -1), jnp.int32(KEY_BITS) - p_i))
            at_i = lax.shift_right_logical(
                words, jnp.broadcast_to(jnp.int32(KEY_BITS - 1) - p_i, words.shape)) & one
            below = lax.population_count(words & above) + jnp.where(word_r < p_r, at_i, nil)
            local = jnp.where(jnp.sum(below, axis=0, keepdims=True) < rank, cand, local)
        cut = cut_tile * T + local

        def demote(kj, carry):
            kk = keys_ref[kj]
            lowered = jnp.where((kj * T + s_loc) > cut, kth - one, kk)
            keys_ref[kj] = jnp.where(kk == kth, lowered, kk)
            return carry

        lax.fori_loop(0, nk, demote, 0)

    acc_ref[...] = jnp.zeros(acc_ref.shape, F32)

    def stage_logits(kj):
        unselected = jnp.where(keys_ref[kj] >= kth, 0.0, NEG)
        off = jnp.minimum(qi - kj, N_OFFSETS - 1)
        kvt = kv_ref[kj]

        def one_head(h):
            s = bias_ref[h, off] + jnp.dot(kvt, qlat_ref[h], preferred_element_type=F32)
            s = s + unselected
            s_ref[h] = s
            return jnp.max(s, axis=0, keepdims=True)

        return one_head

    first = stage_logits(0)
    tile_max0 = tuple(first(h) for h in range(A_HEADS))

    def attend(kj, carry):
        ms, tile_max = carry
        kvTt = _with_ones_row(kvT_ref[kj])
        stage_next = stage_logits(jnp.minimum(kj + 1, nk - 1))
        new_m, new_max = [], []
        for h in range(A_HEADS):
            m_new = jnp.maximum(ms[h], tile_max[h])
            p = jnp.exp2(s_ref[h] - m_new).astype(MM_DTYPE)
            alpha = jnp.exp2(ms[h] - m_new)
            new_max.append(stage_next(h))
            acc_ref[h] = alpha * acc_ref[h] + jnp.dot(kvTt, p, preferred_element_type=F32)
            new_m.append(m_new)
        return tuple(new_m), tuple(new_max)

    _loop_grouped(nk, attend, (_initial_max(A_HEADS, T), tile_max0), log2_group=1)

    for h in range(A_HEADS):
        o_lat = (acc_ref[h, 0:A_LATENT, :] / acc_ref[h, A_LATENT:A_LATENT + 1, :]).astype(MM_DTYPE)
        oT_ref[h * HEAD_DIM:(h + 1) * HEAD_DIM, :] = jnp.dot(
            wuvT_ref[h], o_lat, preferred_element_type=F32)
    out_ref[...] = oT_ref[...].T.astype(out_ref.dtype)


def _dsa(aqT, iqT, iwT, kidx, kv, kvT, bias_a, wukT, wuvT):
    bsz, _, seq = aqT.shape
    T = ATT_TILE
    nk = seq // T
    topk = min(TOPK_MAX, seq // 4)
    assert T == KEY_BITS * SUBLANES
    qspec = lambda rows: pl.BlockSpec((None, rows, T), lambda b, i: (b, 0, i))
    kspec = lambda a, c: pl.BlockSpec((None, nk, a, c), lambda b, i: (b, 0, 0, 0))
    return pl.pallas_call(
        functools.partial(_dsa_kernel, topk=topk),
        out_shape=jax.ShapeDtypeStruct((bsz, seq, A_HEADS * HEAD_DIM), MM_DTYPE),
        grid=(bsz, nk),
        in_specs=[qspec(A_HEADS * HEAD_DIM), qspec(IDX_HEADS * IDX_DIM), qspec(IDX_HEADS),
                  kspec(T, IDX_DIM), kspec(T, A_LATENT), kspec(A_LATENT, T),
                  _const_spec(bias_a.shape), _const_spec(wukT.shape), _const_spec(wuvT.shape)],
        out_specs=pl.BlockSpec((None, T, A_HEADS * HEAD_DIM), lambda b, i: (b, i, 0)),
        scratch_shapes=[pltpu.VMEM((nk, T, T), I32),
                        pltpu.VMEM((nk, KEY_BITS + 1, SUBLANES, T), I32),
                        pltpu.VMEM((A_HEADS, A_LATENT, T), MM_DTYPE),
                        pltpu.VMEM((A_HEADS, A_LATENT + SUM_ROWS, T), F32),
                        pltpu.VMEM((A_HEADS, T, T), F32),
                        pltpu.VMEM((A_HEADS * HEAD_DIM, T), F32)],
        compiler_params=_cparams(2), name="dsa_attention",
    )(aqT, iqT, iwT, kidx, kv, kvT, bias_a, wukT, wuvT)


def _diff_kernel(qT_ref, k_ref, vT_ref, bias_ref, lam_ref, gsub_ref, out_ref,
                 qz_ref, acc_ref, s_ref, oT_ref, *, lam_init):
    T = ATT_TILE
    dv = 2 * HEAD_DIM
    n_chain = 2 * B_HEADS
    qi = pl.program_id(1)
    half = lax.broadcasted_iota(I32, (dv, T), 0) < HEAD_DIM
    for h in range(B_HEADS):
        q = qT_ref[h * dv:(h + 1) * dv, :].astype(F32)
        qz_ref[2 * h] = jnp.where(half, q, 0.0).astype(MM_DTYPE)
        qz_ref[2 * h + 1] = jnp.where(half, 0.0, q).astype(MM_DTYPE)
    acc_ref[...] = jnp.zeros(acc_ref.shape, F32)

    def stage_logits(kj, c):
        h = c // 2
        off = jnp.minimum(qi - kj, N_OFFSETS - 1)
        s = bias_ref[h, off] + jnp.dot(k_ref[kj, :, h * dv:(h + 1) * dv], qz_ref[c],
                                       preferred_element_type=F32)
        s_ref[c] = s
        return jnp.max(s, axis=0, keepdims=True)

    tile_max0 = tuple(stage_logits(0, c) for c in range(n_chain))

    def step(kj, carry, last):
        ms, tile_max = carry
        new_m, new_max = [], []
        for c in range(n_chain):
            h = c // 2
            m_new = jnp.maximum(ms[c], tile_max[c])
            p = jnp.exp2(s_ref[c] - m_new).astype(MM_DTYPE)
            alpha = jnp.exp2(ms[c] - m_new)
            if not last:
                new_max.append(stage_logits(kj + 1, c))
            vT = _with_ones_row(vT_ref[kj, h * dv:(h + 1) * dv, :])
            acc_ref[c] = alpha * acc_ref[c] + jnp.dot(vT, p, preferred_element_type=F32)
            new_m.append(m_new)
        return tuple(new_m), tuple(new_max)

    carry = _loop_grouped(qi, lambda kj, cr: step(kj, cr, False),
                          (_initial_max(n_chain, T), tile_max0), log2_group=2)
    step(qi, carry, True)

    lr = lam_ref[...]
    lam = (jnp.exp(jnp.sum(lr[0:1, :] * lr[1:2, :], axis=1, keepdims=True))
           - jnp.exp(jnp.sum(lr[2:3, :] * lr[3:4, :], axis=1, keepdims=True)) + lam_init)

    def normalised(c):
        return acc_ref[c, 0:dv, :] / acc_ref[c, dv:dv + 1, :]

    for h in range(B_HEADS):
        attn = normalised(2 * h) - lam * normalised(2 * h + 1)
        y = attn * lax.rsqrt(jnp.mean(attn * attn, axis=0, keepdims=True) + EPS)
        oT_ref[h * dv:(h + 1) * dv, :] = y * gsub_ref[...] * (1.0 - lam_init)
    out_ref[...] = oT_ref[...].T.astype(out_ref.dtype)


def _diff(bqT, bk, bvT, bias_b, lam_rows, gsub, lam_init):
    bsz, _, seq = bqT.shape
    T = ATT_TILE
    nk = seq // T
    dv = 2 * HEAD_DIM
    qspec = pl.BlockSpec((None, B_HEADS * dv, T), lambda b, i: (b, 0, i))
    return pl.pallas_call(
        functools.partial(_diff_kernel, lam_init=lam_init),
        out_shape=jax.ShapeDtypeStruct((bsz, seq, B_HEADS * dv), MM_DTYPE),
        grid=(bsz, nk),
        in_specs=[qspec,
                  pl.BlockSpec((None, nk, T, B_HEADS * dv), lambda b, i: (b, 0, 0, 0)),
                  pl.BlockSpec((None, nk, B_HEADS * dv, T), lambda b, i: (b, 0, 0, 0)),
                  _const_spec(bias_b.shape), _const_spec((4, HEAD_DIM)), _const_spec((dv, 1))],
        out_specs=pl.BlockSpec((None, T, B_HEADS * dv), lambda b, i: (b, i, 0)),
        scratch_shapes=[pltpu.VMEM((2 * B_HEADS, dv, T), MM_DTYPE),
                        pltpu.VMEM((2 * B_HEADS, dv + SUM_ROWS, T), F32),
                        pltpu.VMEM((2 * B_HEADS, T, T), F32),
                        pltpu.VMEM((B_HEADS * dv, T), F32)],
        compiler_params=_cparams(2), name="diff_attention",
    )(bqT, bk, bvT, bias_b, lam_rows, gsub)


def _dil_kernel(cur_ref, halo_ref, bias_ref, out_ref, lse_ref, *, tq):
    n = C_BAND
    wid = C_HPG * HEAD_DIM
    halo_lo = jnp.where(pl.program_id(2) == 0, jnp.int32(n), jnp.int32(0))
    i = lax.broadcasted_iota(I32, (n, 2 * n), 0)
    j = lax.broadcasted_iota(I32, (n, 2 * n), 1)
    lane_head = lax.broadcasted_iota(I32, (n, wid), 1) // HEAD_DIM
    in_head = [lane_head == h for h in range(C_HPG)]
    band = jnp.where(j >= i, jnp.where(j <= i + n, 0.0, NEG), NEG)
    band0 = jnp.where(j >= jnp.maximum(i, halo_lo), jnp.where(j <= i + n, 0.0, NEG), NEG)
    bias = [bias_ref[h] + band for h in range(C_HPG)]
    bias0 = [bias_ref[h] + band0 for h in range(C_HPG)]

    def band_rows(r, c, lo, hi):
        if c == 0:
            return jnp.concatenate([halo_ref[r, :, lo:hi], cur_ref[r, 0:n, lo:hi]], axis=0)
        return cur_ref[r, (c - 1) * n:(c + 1) * n, lo:hi]

    blocks = [(r, c) for r in range(cur_ref.shape[0]) for c in range(tq // n)]
    logits = []
    for r, c in blocks:
        q = cur_ref[r, c * n:(c + 1) * n, 0:wid].astype(F32) * (ATTN_SCALE * LOG2E)
        keys = band_rows(r, c, wid, 2 * wid)
        for h in range(C_HPG):
            qh = jnp.where(in_head[h], q, 0.0).astype(MM_DTYPE)
            s = lax.dot_general(qh, keys, (((1,), (1,)), ((), ())), preferred_element_type=F32)
            logits.append(s + (bias0 if c == 0 else bias)[h])
    for k, (r, c) in enumerate(blocks):
        vals = band_rows(r, c, 2 * wid, 3 * wid)
        out = jnp.zeros((n, wid), F32)
        lse = jnp.zeros((n, wid), F32)
        for h in range(C_HPG):
            s = logits[k * C_HPG + h]
            m = jnp.max(s, axis=1, keepdims=True)
            p = jnp.exp2(s - m)
            den = jnp.sum(p, axis=1, keepdims=True)
            o = jnp.dot(p.astype(MM_DTYPE), vals, preferred_element_type=F32) * (1.0 / den)
            out = jnp.where(in_head[h], o, out)
            lse = jnp.where(in_head[h], m + jnp.log2(den), lse)
        out_ref[r, c * n:(c + 1) * n, :] = out
        lse_ref[r, c * n:(c + 1) * n, :] = lse


def _dilated_group(cg, bias_g, g):
    bsz, dil, m, gw = cg.shape
    wid = C_HPG * HEAD_DIM
    n = C_BAND
    assert m % n == 0 and gw == 3 * wid
    tq = min(m, 512)
    n_res = max(1, min(dil, 512 // tq))
    cur = pl.BlockSpec((None, n_res, tq, gw), lambda b, r, i: (b, r, i, 0))
    halo = pl.BlockSpec((None, n_res, n, gw),
                        lambda b, r, i: (b, r, jnp.maximum(i * (tq // n) - 1, 0), 0))
    outspec = pl.BlockSpec((None, n_res, tq, wid), lambda b, r, i: (b, r, i, 0))
    return pl.pallas_call(
        functools.partial(_dil_kernel, tq=tq),
        out_shape=[jax.ShapeDtypeStruct((bsz, dil, m, wid), F32)] * 2,
        grid=(bsz, dil // n_res, m // tq),
        in_specs=[cur, halo, pl.BlockSpec((C_HPG, n, 2 * n), lambda b, r, i: (0, 0, 0))],
        out_specs=[outspec, outspec],
        compiler_params=_cparams(3), name=f"dilated_group{g}",
    )(cg, cg, bias_g)


def _merge_ffn_kernel(x_ref, mod_ref, g1_ref, g2_ref, gf_ref, oa_ref, ob_ref,
                      c0_ref, c1_ref, c2_ref, s0_ref, s1_ref, s2_ref,
                      wz, wba, wbb, wbc, wo, wgu, wd, out_ref, tok_ref, *, final_norm):
    x = x_ref[...]
    tm = x.shape[0]
    h = _rms(x) * g1_ref[...]
    h = h * (1.0 + mod_ref[1:2, :]) + mod_ref[0:1, :]
    hb = h.astype(MM_DTYPE)

    def token_order(k, ref):
        dil, _, w = ref.shape
        if dil == 1:
            return ref[0]
        n_chunk = w // LANES
        for r in range(dil):
            for j in range(n_chunk):
                tok_ref[k * n_chunk + j, pl.ds(r, tm // dil, stride=dil), :] = ref[
                    r, :, j * LANES:(j + 1) * LANES]
        return jnp.concatenate([tok_ref[k * n_chunk + j] for j in range(n_chunk)], axis=1)

    s0, s1, s2 = s0_ref[0], token_order(0, s1_ref), token_order(1, s2_ref)
    c0, c1, c2 = c0_ref[0], token_order(2, c1_ref), token_order(3, c2_ref)
    mx = jnp.maximum(jnp.maximum(s0, s1), s2)
    e0, e1, e2 = jnp.exp2(s0 - mx), jnp.exp2(s1 - mx), jnp.exp2(s2 - mx)
    oc = (e0 * c0 + e1 * c1 + e2 * c2) / (e0 + e1 + e2)

    d = x.shape[1]

    def gated(k, o, wb):
        z = jnp.dot(hb, wz[:, k * d:(k + 1) * d], preferred_element_type=F32)
        return jax.nn.sigmoid(z) * jnp.dot(o, wb[...], preferred_element_type=F32)

    merged = (gated(0, oa_ref[...], wba) + gated(1, ob_ref[...], wbb)
              + gated(2, oc.astype(MM_DTYPE), wbc))
    y = jnp.dot(merged.astype(MM_DTYPE), wo[...], preferred_element_type=F32)
    x = x + mod_ref[2:3, :] * y

    h = _rms(x) * g2_ref[...]
    h = h * (1.0 + mod_ref[4:5, :]) + mod_ref[3:4, :]
    hb = h.astype(MM_DTYPE)
    acc = jnp.zeros(x.shape, F32)
    for c in range(D_FF // FFN_CHUNK):
        cols = slice(c * FFN_CHUNK, (c + 1) * FFN_CHUNK)
        fg = jnp.dot(hb, wgu[:, cols], preferred_element_type=F32)
        fu = jnp.dot(hb, wgu[:, D_FF + c * FFN_CHUNK:D_FF + (c + 1) * FFN_CHUNK],
                     preferred_element_type=F32)
        act = (fg * jax.nn.sigmoid(fg) * fu).astype(MM_DTYPE)
        acc = acc + jnp.dot(act, wd[cols, :], preferred_element_type=F32)
    y = x + mod_ref[5:6, :] * acc
    if final_norm:
        y = _rms(y) * gf_ref[...]
    out_ref[...] = y


def _merge_ffn(x2d, mod_l, g1, g2, gf, oa, ob, ocs, lses, ws, seq, final_norm):
    n, d = x2d.shape
    tm = ROW_TILE
    per_b = seq // tm
    row = lambda wd: pl.BlockSpec((tm, wd), lambda i: (i, 0))
    vec = _const_spec((1, d))
    wid = C_HPG * HEAD_DIM
    res = [pl.BlockSpec((None, dil, tm // dil, wid), lambda i: (i // per_b, 0, i % per_b, 0))
           for _, dil in C_GROUPS]
    in_specs = [row(d), pl.BlockSpec((None, 6, d), lambda i: (i // per_b, 0, 0)), vec, vec, vec,
                row(oa.shape[1]), row(ob.shape[1])] + res + res
    in_specs += [_const_spec(w.shape) for w in ws]
    return pl.pallas_call(
        functools.partial(_merge_ffn_kernel, final_norm=final_norm),
        out_shape=jax.ShapeDtypeStruct((n, d), F32), grid=(n // tm,),
        in_specs=in_specs, out_specs=row(d),
        scratch_shapes=[pltpu.VMEM((4 * wid // LANES, tm, LANES), F32)],
        compiler_params=_cparams(1), name="merge_ffn",
    )(x2d, mod_l, g1, g2, gf, oa, ob, *ocs, *lses, *ws)


def kernel(x, c, w_ada, b_ada, g_norm1, w_in, w_uk, w_uv, g_kv, lam_q1, lam_k1, lam_q2, lam_k2,
           g_subln, w_branch_a, w_branch_b, w_branch_c, w_out, g_norm2, w_gate_up, w_down,
           rel_bias, g_final):
    bsz, seq, d = x.shape
    depth = w_ada.shape[0]
    T = ATT_TILE
    nk = seq // T
    assert d == D_MODEL and seq % T == 0 and seq % ROW_TILE == 0 and seq % IN_ROW_TILE == 0
    n = bsz * seq
    cast = lambda w: w.astype(MM_DTYPE)

    thresholds = _bucket_thresholds(seq + 2 * C_BAND * C_GROUPS[-1][1])
    assert seq <= N_OFFSETS * T or (N_OFFSETS - 2) * T + 1 >= thresholds[-1]
    tab = rel_bias.reshape(-1)
    bias_a = _bias_att_tiles(tab, 0, A_HEADS, T, thresholds)
    bias_b = _bias_att_tiles(tab, A_HEADS, B_HEADS, T, thresholds)
    bias_c = _bias_dil_tiles(tab, A_HEADS + B_HEADS, thresholds)

    mod = _modulation(c, w_ada, b_ada).reshape(depth, bsz, 6, d)

    splits = (A_HEADS * HEAD_DIM, A_LATENT, IDX_HEADS * IDX_DIM, IDX_DIM, IDX_HEADS,
              B_HEADS * 2 * HEAD_DIM, B_HEADS * 2 * HEAD_DIM, B_HEADS * 2 * HEAD_DIM,
              C_HEADS * HEAD_DIM, C_HEADS * HEAD_DIM, C_HEADS * HEAD_DIM, d, d, d)
    offs = np.concatenate([[0], np.cumsum(splits)])
    seg = lambda w, k: w[:, int(offs[k]):int(offs[k + 1])]

    x2d = x.reshape(n, d)
    for l in range(depth):
        wl = w_in[l]
        wid = C_HPG * HEAD_DIM
        w_iw = jnp.pad(seg(wl, 4), ((0, 0), (0, 16 - IDX_HEADS)))
        wT_all = cast(jnp.concatenate([seg(wl, 0), seg(wl, 2), seg(wl, 5), seg(wl, 7), w_iw], axis=1).T)
        w_all = cast(jnp.concatenate(
            [seg(wl, 1), seg(wl, 6)]
            + [seg(wl, k)[:, g * wid:(g + 1) * wid] for g in range(len(C_GROUPS)) for k in (8, 9, 10)]
            + [seg(wl, 3)], axis=1))
        ws_in = [wT_all, w_all]
        g1 = g_norm1[l].reshape(1, d)
        (aqT, iqT, iwT, bqT, bvT, ik, kv, kvT, bk, cg0, cg1, cg2) = _in_proj(
            x2d, mod[l], g1, g_kv[l].reshape(1, A_LATENT), ws_in, bsz, seq)

        o_a = _dsa(aqT, iqT, iwT,
                   ik.reshape(bsz, nk, T, IDX_DIM), kv.reshape(bsz, nk, T, A_LATENT), kvT,
                   bias_a, cast(w_uk[l].transpose(0, 2, 1)), cast(w_uv[l].transpose(0, 2, 1)))
        o_a = o_a.reshape(n, -1)

        lam_init = 0.8 - 0.6 * math.exp(-0.3 * l)
        lam_rows = jnp.stack([lam_q1[l], lam_k1[l], lam_q2[l], lam_k2[l]])
        dv = 2 * HEAD_DIM
        o_b = _diff(bqT, bk.reshape(bsz, nk, T, B_HEADS * dv), bvT, bias_b, lam_rows,
                    g_subln[l].reshape(dv, 1), lam_init)
        o_b = o_b.reshape(n, -1)

        ocs, lses = [], []
        for g, (cg, (window, dil)) in enumerate(zip((cg0, cg1, cg2), C_GROUPS)):
            assert window // dil == C_BAND
            o, s = _dilated_group(cg, bias_c[g * C_HPG:(g + 1) * C_HPG], g)
            ocs.append(o)
            lses.append(s)

        ws_out = [cast(wl[:, int(offs[11]):int(offs[14])]),
                  cast(w_branch_a[l]), cast(w_branch_b[l]), cast(w_branch_c[l]), cast(w_out[l]),
                  cast(w_gate_up[l]), cast(w_down[l])]
        x2d = _merge_ffn(x2d, mod[l], g1, g_norm2[l].reshape(1, d), g_final.reshape(1, d),
                         o_a, o_b, ocs, lses, ws_out, seq, final_norm=(l == depth - 1))
    return x2d.reshape(bsz, seq, d)
```

```python
import functools
import math

import numpy as np
import jax
import jax.numpy as jnp
from jax import lax
from jax.experimental import pallas as pl
from jax.experimental.pallas import tpu as pltpu

D_MODEL = 1024
HEAD_DIM = 64
ATTN_SCALE = HEAD_DIM ** -0.5
LOG2E = math.log2(math.e)
A_HEADS = 8
A_LATENT = 128
IDX_HEADS = 8
IDX_DIM = 64
IDX_SCALE = (IDX_HEADS * IDX_DIM) ** -0.5
TOPK_MAX = 256
B_HEADS = 4
C_GROUPS = ((128, 1), (512, 4), (2048, 16))
C_HPG = 4
C_HEADS = C_HPG * len(C_GROUPS)
N_BUCKETS = 32
MAX_DISTANCE = 2048
N_BIAS_HEADS = A_HEADS + B_HEADS + C_HEADS
D_FF = -(-8 * D_MODEL // (3 * 256)) * 256
EPS = 1e-6

MM_DTYPE = jnp.bfloat16
F32 = jnp.float32
I32 = jnp.int32

ATT_TILE = 256
N_OFFSETS = 8
C_BAND = 128
ROW_TILE = 512
IN_ROW_TILE = 1024
FFN_CHUNK = 256
MOD_COLS = 1536
NEG = -1e30
INT_MIN = -2 ** 31
LANES = 128
SUBLANES = 8
KEY_BITS = 32
SEARCH_TILES = 4
VMEM_LIMIT = 56 * 1024 * 1024


def _cparams(n_axes, vmem=VMEM_LIMIT):
    return pltpu.CompilerParams(dimension_semantics=("arbitrary",) * n_axes,
                                vmem_limit_bytes=vmem)


def _const_spec(shape):
    nd = len(shape)
    return pl.BlockSpec(shape, lambda *_: (0,) * nd, pipeline_mode=pl.Buffered(1))


def _bucket_thresholds(max_dist):
    n = np.arange(max_dist + 1)
    max_exact = N_BUCKETS // 2
    nf = np.maximum(n, 1).astype(np.float32)
    large = max_exact + (np.log(nf / np.float32(max_exact))
                         / np.float32(math.log(MAX_DISTANCE / max_exact))
                         * np.float32(N_BUCKETS - max_exact)).astype(np.int32)
    large = np.minimum(large, N_BUCKETS - 1)
    bucket = np.where(n < max_exact, n, large)
    assert np.all(np.diff(bucket) >= 0)
    thr = []
    for k in range(1, N_BUCKETS):
        idx = np.nonzero(bucket >= k)[0]
        thr.append(int(idx[0]) if idx.size else None)
    return thr


def _bias_from_dist(dist, tab_ref, col, thresholds, lo=0, hi=None):
    reached = [k for k, thr in enumerate(thresholds, start=1) if thr is not None]
    base = max([0] + [k for k in reached if thresholds[k - 1] <= lo])
    b = jnp.full(dist.shape, tab_ref[base * N_BIAS_HEADS + col], F32)
    for k in reached:
        thr = thresholds[k - 1]
        if thr > lo and (hi is None or thr <= hi):
            b = jnp.where(dist >= thr, tab_ref[k * N_BIAS_HEADS + col], b)
    return b


def _bias_att_kernel(tab_ref, out_ref, *, head0, thresholds):
    h = pl.program_id(0)
    n_off, tile, _ = out_ref.shape
    row = lax.broadcasted_iota(I32, (tile, tile), 0)
    colq = lax.broadcasted_iota(I32, (tile, tile), 1)
    for o in range(n_off):
        dist = jnp.maximum(o * tile + colq - row, 0)
        lo, hi = max(o * tile - (tile - 1), 0), o * tile + tile - 1
        b = _bias_from_dist(dist, tab_ref, head0 + h, thresholds, lo, hi) * LOG2E
        if o == 0:
            b = jnp.where(row <= colq, b, NEG)
        out_ref[o] = b


def _bias_att_tiles(tab, head0, n_heads, tile, thresholds):
    return pl.pallas_call(
        functools.partial(_bias_att_kernel, head0=head0, thresholds=thresholds),
        out_shape=jax.ShapeDtypeStruct((n_heads, N_OFFSETS, tile, tile), F32),
        grid=(n_heads,),
        in_specs=[pl.BlockSpec(memory_space=pltpu.SMEM)],
        out_specs=pl.BlockSpec((None, N_OFFSETS, tile, tile), lambda h: (h, 0, 0, 0)),
        compiler_params=_cparams(1),
        name="bias_att_tiles",
    )(tab)


def _bias_dil_kernel(tab_ref, out_ref, *, head0, thresholds):
    h = pl.program_id(0)
    g = h // C_HPG
    dil = jnp.where(g == 0, C_GROUPS[0][1], jnp.where(g == 1, C_GROUPS[1][1], C_GROUPS[2][1]))
    i = lax.broadcasted_iota(I32, (C_BAND, 2 * C_BAND), 0)
    j = lax.broadcasted_iota(I32, (C_BAND, 2 * C_BAND), 1)
    dist = jnp.maximum((i - j + C_BAND) * dil, 0)
    out_ref[...] = _bias_from_dist(dist, tab_ref, head0 + h, thresholds) * LOG2E


def _bias_dil_tiles(tab, head0, thresholds):
    return pl.pallas_call(
        functools.partial(_bias_dil_kernel, head0=head0, thresholds=thresholds),
        out_shape=jax.ShapeDtypeStruct((C_HEADS, C_BAND, 2 * C_BAND), F32),
        grid=(C_HEADS,),
        in_specs=[pl.BlockSpec(memory_space=pltpu.SMEM)],
        out_specs=pl.BlockSpec((None, C_BAND, 2 * C_BAND), lambda h: (h, 0, 0)),
        compiler_params=_cparams(1),
        name="bias_dil_tiles",
    )(tab)


def _mod_kernel(c_ref, w_ref, b_ref, out_ref):
    c = c_ref[...]
    ca = (c * jax.nn.sigmoid(c)).astype(MM_DTYPE)
    out_ref[...] = jnp.dot(ca, w_ref[...].astype(MM_DTYPE), preferred_element_type=F32) + b_ref[...]


def _modulation(c, w_ada, b_ada):
    depth, d, wid = w_ada.shape
    bsz = c.shape[0]
    tn = MOD_COLS
    assert wid % tn == 0
    return pl.pallas_call(
        _mod_kernel,
        out_shape=jax.ShapeDtypeStruct((depth, bsz, wid), F32),
        grid=(depth, wid // tn),
        in_specs=[pl.BlockSpec((bsz, d), lambda l, j: (0, 0)),
                  pl.BlockSpec((None, d, tn), lambda l, j: (l, 0, j)),
                  pl.BlockSpec((None, 1, tn), lambda l, j: (l, 0, j))],
        out_specs=pl.BlockSpec((None, bsz, tn), lambda l, j: (l, 0, j)),
        compiler_params=_cparams(2),
        name="adaln_modulation",
    )(c, w_ada, b_ada.reshape(depth, 1, wid))


def _rms(x):
    return x * lax.rsqrt(jnp.mean(x * x, axis=-1, keepdims=True) + EPS)


IN_ROWS_T = (("aq", A_HEADS * HEAD_DIM), ("iq", IDX_HEADS * IDX_DIM), ("bq", B_HEADS * 2 * HEAD_DIM),
             ("bv", B_HEADS * 2 * HEAD_DIM), ("iw", 16))
IN_COLS = (("kv", A_LATENT), ("bk", B_HEADS * 2 * HEAD_DIM), ("c", 3 * C_HEADS * HEAD_DIM),
           ("ik", IDX_DIM))


def _segments(layout):
    out, start = {}, 0
    for name, width in layout:
        out[name] = slice(start, start + width)
        start += width
    return out


def _in_kernel(x_ref, mod_ref, g1_ref, gkv_ref, wT_ref, w_ref,
               o_aqT, o_iqT, o_iwT, o_bqT, o_bvT, o_ik, o_kv, o_kvT, o_bk, o_c0, o_c1, o_c2,
               c_scr):
    T = ATT_TILE
    tm = x_ref.shape[0]
    h = _rms(x_ref[...]) * g1_ref[...]
    h = h * (1.0 + mod_ref[1:2, :]) + mod_ref[0:1, :]
    hb = h.astype(MM_DTYPE)
    rows_t, cols = _segments(IN_ROWS_T), _segments(IN_COLS)

    def mm(name):
        return jnp.dot(hb, w_ref[:, cols[name]], preferred_element_type=F32)

    def mm_t(name):
        return lax.dot_general(wT_ref[rows_t[name], :], hb, (((1,), (1,)), ((), ())),
                               preferred_element_type=F32)


    o_aqT[...] = mm_t("aq").astype(o_aqT.dtype)
    o_iqT[...] = mm_t("iq").astype(o_iqT.dtype)
    o_bqT[...] = (mm_t("bq") * (ATTN_SCALE * LOG2E)).astype(o_bqT.dtype)
    o_iwT[...] = (mm_t("iw") * IDX_SCALE)[:IDX_HEADS]
    bvT = mm_t("bv").astype(o_bvT.dtype)
    kv = _rms(mm("kv")) * gkv_ref[...]
    kvT = kv.T.astype(o_kvT.dtype)
    for j in range(tm // T):
        o_bvT[j] = bvT[:, j * T:(j + 1) * T]
        o_kvT[j] = kvT[:, j * T:(j + 1) * T]
    o_kv[...] = kv.astype(o_kv.dtype)
    o_ik[...] = mm("ik").astype(o_ik.dtype)
    o_bk[...] = mm("bk").astype(o_bk.dtype)

    yc = mm("c")
    n_chunk = yc.shape[1] // LANES
    for j in range(n_chunk):
        c_scr[j] = yc[:, j * LANES:(j + 1) * LANES]
    per_group = n_chunk // len(C_GROUPS)
    for g, o_c in enumerate((o_c0, o_c1, o_c2)):
        dil = C_GROUPS[g][1]
        for r in range(dil):
            for jj in range(per_group):
                o_c[r, :, jj * LANES:(jj + 1) * LANES] = c_scr[
                    g * per_group + jj, pl.ds(r, tm // dil, stride=dil), :].astype(o_c.dtype)


def _in_proj(x2d, mod_l, g1, gkv, ws, bsz, seq):
    n, d = x2d.shape
    tm = IN_ROW_TILE
    T = ATT_TILE
    per_b = seq // tm
    nk = seq // T
    hd = A_HEADS * HEAD_DIM
    bw = B_HEADS * 2 * HEAD_DIM
    gw = 3 * C_HPG * HEAD_DIM
    in_specs = [pl.BlockSpec((tm, d), lambda i: (i, 0)),
                pl.BlockSpec((None, 6, d), lambda i: (i // per_b, 0, 0)),
                _const_spec((1, d)), _const_spec((1, A_LATENT))]
    in_specs += [_const_spec(w.shape) for w in ws]

    def tspec(rows):
        return pl.BlockSpec((None, rows, tm), lambda i: (i // per_b, 0, i % per_b))

    def tile_tspec(rows):
        return pl.BlockSpec((None, tm // T, rows, T), lambda i: (i // per_b, i % per_b, 0, 0))

    def rspec(wd):
        return pl.BlockSpec((tm, wd), lambda i: (i, 0))

    def cspec(dil):
        return pl.BlockSpec((None, dil, tm // dil, gw), lambda i: (i // per_b, 0, i % per_b, 0))

    sds = jax.ShapeDtypeStruct
    out_specs = [tspec(hd), tspec(IDX_HEADS * IDX_DIM), tspec(IDX_HEADS), tspec(bw),
                 tile_tspec(bw), rspec(IDX_DIM), rspec(A_LATENT), tile_tspec(A_LATENT), rspec(bw)]
    out_shape = [sds((bsz, hd, seq), MM_DTYPE), sds((bsz, IDX_HEADS * IDX_DIM, seq), MM_DTYPE),
                 sds((bsz, IDX_HEADS, seq), F32), sds((bsz, bw, seq), MM_DTYPE),
                 sds((bsz, nk, bw, T), MM_DTYPE), sds((n, IDX_DIM), MM_DTYPE),
                 sds((n, A_LATENT), MM_DTYPE), sds((bsz, nk, A_LATENT, T), MM_DTYPE),
                 sds((n, bw), MM_DTYPE)]
    for _, dil in C_GROUPS:
        out_specs.append(cspec(dil))
        out_shape.append(sds((bsz, dil, seq // dil, gw), MM_DTYPE))
    return pl.pallas_call(
        _in_kernel, out_shape=out_shape, grid=(n // tm,),
        in_specs=in_specs, out_specs=out_specs,
        scratch_shapes=[pltpu.VMEM((3 * gw // LANES, tm, LANES), F32)],
        compiler_params=_cparams(1), name="in_proj",
    )(x2d, mod_l, g1, gkv, *ws)


def _initial_max(n_chains, tile):
    return tuple(jnp.full((1, tile), NEG, F32) for _ in range(n_chains))


def _loop_grouped(n, body, init, log2_group):
    carry, done = init, jnp.int32(0)
    for lg in range(log2_group, -1, -1):
        group = 1 << lg
        trips = lax.shift_right_logical(n - done, jnp.int32(lg))

        def grouped(i, c, group=group, done=done):
            for j in range(group):
                c = body(done + group * i + j, c)
            return c

        carry = lax.fori_loop(0, trips, grouped, carry)
        done = done + group * trips
    return carry


SUM_ROWS = 16


def _with_ones_row(vT):
    row = lax.broadcasted_iota(I32, (SUM_ROWS, vT.shape[1]), 0)
    ones = jnp.where(row == 0, 1.0, 0.0).astype(vT.dtype)
    return jnp.concatenate([vT, ones], axis=0)


def _bit_transpose32(words):
    a = list(words)
    j, mask = 16, 0x0000FFFF
    while j:
        k = 0
        while k < 32:
            t = (a[k] ^ lax.shift_right_logical(a[k + j], jnp.int32(j))) & jnp.int32(mask)
            a[k] = a[k] ^ t
            a[k + j] = a[k + j] ^ lax.shift_left(t, jnp.int32(j))
            k = (k + j + 1) & ~j
        j >>= 1
        mask = (mask ^ (mask << j)) & 0xFFFFFFFF
    return a


def _dsa_kernel(aqT_ref, iqT_ref, iwT_ref, kidx_ref, kv_ref, kvT_ref, bias_ref, wukT_ref, wuvT_ref,
                out_ref, keys_ref, planes_ref, found_ref, qlat_ref, acc_ref, s_ref, oT_ref, *, topk):
    T = ATT_TILE
    qi = pl.program_id(1)
    nk = qi + 1

    for h in range(A_HEADS):
        q = jnp.dot(wukT_ref[h], aqT_ref[h * HEAD_DIM:(h + 1) * HEAD_DIM, :],
                    preferred_element_type=F32) * (ATTN_SCALE * LOG2E)
        qlat_ref[h] = q.astype(qlat_ref.dtype)

    s_loc = lax.broadcasted_iota(I32, (T, T), 0)
    t_loc = lax.broadcasted_iota(I32, (T, T), 1)

    def score_tile(kj, diagonal):
        kt = kidx_ref[kj]
        acc = jnp.zeros((T, T), F32)
        for h in range(IDX_HEADS):
            s = jnp.dot(kt, iqT_ref[h * IDX_DIM:(h + 1) * IDX_DIM, :], preferred_element_type=F32)
            acc = acc + jnp.maximum(s, 0.0) * iwT_ref[h:h + 1, :]
        bits = lax.bitcast_convert_type(acc, I32)
        key = jnp.where(bits < 0, bits ^ jnp.int32(0x7FFFFFFF), bits)
        if diagonal:
            key = jnp.where(s_loc <= t_loc, key, jnp.int32(INT_MIN))
        keys_ref[kj] = key
        planes = _bit_transpose32([key[SUBLANES * i:SUBLANES * (i + 1), :] ^ jnp.int32(INT_MIN)
                                   for i in range(KEY_BITS)])
        for b in range(KEY_BITS):
            planes_ref[kj, b] = planes[b]

    def off_diagonal(kj, carry):
        score_tile(kj, False)
        return carry

    _loop_grouped(qi, off_diagonal, 0, log2_group=2)
    score_tile(qi, True)

    n_tiles = planes_ref.shape[0]

    @pl.when(qi == 0)
    def _():
        def clear_planes(kj, carry):
            for b in range(KEY_BITS):
                planes_ref[kj, b] = jnp.zeros((SUBLANES, T), I32)
            return carry

        lax.fori_loop(1, n_tiles, clear_planes, 0)

    one = jnp.int32(1)
    nil = jnp.int32(0)
    zero = jnp.zeros((1, T), I32)

    n_rows = n_tiles * SUBLANES

    def search(tiles):
        rows = tiles * SUBLANES
        tile_of_row = lax.broadcasted_iota(I32, (rows, T), 0) // SUBLANES
        tied0 = jnp.where(tile_of_row < nk, jnp.int32(-1), nil)

        def bit_step(b, carry):
            tied, n_gt, kth_u = carry
            ones = tied & planes_ref[0:tiles, b].reshape(rows, T)
            n1 = jnp.sum(lax.population_count(ones), axis=0, keepdims=True)
            take = (n_gt + n1) >= topk
            tied = jnp.where(take, ones, tied ^ ones)
            n_gt = jnp.where(take, n_gt, n_gt + n1)
            kth_u = jnp.where(take, kth_u | lax.shift_left(one, jnp.int32(KEY_BITS - 1) - b), kth_u)
            return tied, n_gt, kth_u

        tied, n_gt, kth_u = lax.fori_loop(0, KEY_BITS, bit_step, (tied0, zero, zero))
        found_ref[0:rows, :] = tied
        if rows < n_rows:
            found_ref[rows:n_rows, :] = jnp.zeros((n_rows - rows, T), I32)
        found_ref[n_rows:n_rows + SUBLANES, :] = jnp.broadcast_to(n_gt, (SUBLANES, T))
        found_ref[n_rows + SUBLANES:n_rows + 2 * SUBLANES, :] = jnp.broadcast_to(kth_u, (SUBLANES, T))

    covered = 0
    for tiles in list(range(SEARCH_TILES, n_tiles, SEARCH_TILES)) + [n_tiles]:
        pl.when(jnp.logical_and(nk > covered, nk <= tiles))(functools.partial(search, tiles))
        covered = tiles

    tied = found_ref[0:n_rows, :]
    n_gt = found_ref[n_rows:n_rows + 1, :]
    kth_u = found_ref[n_rows + SUBLANES:n_rows + SUBLANES + 1, :]
    n_eq = jnp.sum(lax.population_count(tied), axis=0, keepdims=True)
    n_ge = jnp.where(kth_u == nil, n_gt, n_gt + n_eq)
    kth = jnp.maximum(kth_u ^ jnp.int32(INT_MIN), jnp.int32(INT_MIN + 1))

    @pl.when(jnp.max(n_ge) > topk)
    def _():
        need = topk - n_gt
        r_bits = SUBLANES.bit_length() - 1

        before, cut_tile, rank = zero, zero, need
        words = jnp.zeros((SUBLANES, T), I32)
        for kj in range(n_tiles):
            tile_words = tied[kj * SUBLANES:(kj + 1) * SUBLANES, :]
            after = before + jnp.sum(lax.population_count(tile_words), axis=0, keepdims=True)
            here = jnp.where(before < need, jnp.where(after >= need, one, nil), nil) == one
            cut_tile = jnp.where(here, jnp.int32(kj), cut_tile)
            rank = jnp.where(here, need - before, rank)
            words = jnp.where(here, tile_words, words)
            before = after

        word_r = lax.broadcasted_iota(I32, (SUBLANES, T), 0)
        local = zero
        for bit in reversed(range(T.bit_length() - 1)):
            cand = local | jnp.int32(1 << bit)
            p_i = lax.shift_right_logical(cand, jnp.int32(r_bits))
            p_r = cand & jnp.int32(SUBLANES - 1)
            above = jnp.where(p_i == nil, nil,
                              lax.shift_left(jnp.full_like(p_i, -1), jnp.int32(KEY_BITS) - p_i))
            at_i = lax.shift_right_logical(
                words, jnp.broadcast_to(jnp.int32(KEY_BITS - 1) - p_i, words.shape)) & one
            below = lax.population_count(words & above) + jnp.where(word_r < p_r, at_i, nil)
            local = jnp.where(jnp.sum(below, axis=0, keepdims=True) < rank, cand, local)
        cut = cut_tile * T + local

        def demote(kj, carry):
            kk = keys_ref[kj]
            lowered = jnp.where((kj * T + s_loc) > cut, kth - one, kk)
            keys_ref[kj] = jnp.where(kk == kth, lowered, kk)
            return carry

        lax.fori_loop(0, nk, demote, 0)

    acc_ref[...] = jnp.zeros(acc_ref.shape, F32)

    def stage_logits(kj):
        unselected = jnp.where(keys_ref[kj] >= kth, 0.0, NEG)
        off = jnp.minimum(qi - kj, N_OFFSETS - 1)
        kvt = kv_ref[kj]

        def one_head(h):
            s = bias_ref[h, off] + jnp.dot(kvt, qlat_ref[h], preferred_element_type=F32)
            s = s + unselected
            s_ref[h] = s
            return jnp.max(s, axis=0, keepdims=True)

        return one_head

    first = stage_logits(0)
    tile_max0 = tuple(first(h) for h in range(A_HEADS))

    def attend(kj, carry):
        ms, tile_max = carry
        kvTt = _with_ones_row(kvT_ref[kj])
        stage_next = stage_logits(jnp.minimum(kj + 1, nk - 1))
        new_m, new_max = [], []
        for h in range(A_HEADS):
            m_new = jnp.maximum(ms[h], tile_max[h])
            p = jnp.exp2(s_ref[h] - m_new).astype(MM_DTYPE)
            alpha = jnp.exp2(ms[h] - m_new)
            new_max.append(stage_next(h))
            acc_ref[h] = alpha * acc_ref[h] + jnp.dot(kvTt, p, preferred_element_type=F32)
            new_m.append(m_new)
        return tuple(new_m), tuple(new_max)

    _loop_grouped(nk, attend, (_initial_max(A_HEADS, T), tile_max0), log2_group=1)

    for h in range(A_HEADS):
        o_lat = (acc_ref[h, 0:A_LATENT, :] / acc_ref[h, A_LATENT:A_LATENT + 1, :]).astype(MM_DTYPE)
        oT_ref[h * HEAD_DIM:(h + 1) * HEAD_DIM, :] = jnp.dot(
            wuvT_ref[h], o_lat, preferred_element_type=F32)
    out_ref[...] = oT_ref[...].T.astype(out_ref.dtype)


def _dsa(aqT, iqT, iwT, kidx, kv, kvT, bias_a, wukT, wuvT):
    bsz, _, seq = aqT.shape
    T = ATT_TILE
    nk = seq // T
    topk = min(TOPK_MAX, seq // 4)
    assert T == KEY_BITS * SUBLANES
    qspec = lambda rows: pl.BlockSpec((None, rows, T), lambda b, i: (b, 0, i))
    kspec = lambda a, c: pl.BlockSpec((None, nk, a, c), lambda b, i: (b, 0, 0, 0))
    return pl.pallas_call(
        functools.partial(_dsa_kernel, topk=topk),
        out_shape=jax.ShapeDtypeStruct((bsz, seq, A_HEADS * HEAD_DIM), MM_DTYPE),
        grid=(bsz, nk),
        in_specs=[qspec(A_HEADS * HEAD_DIM), qspec(IDX_HEADS * IDX_DIM), qspec(IDX_HEADS),
                  kspec(T, IDX_DIM), kspec(T, A_LATENT), kspec(A_LATENT, T),
                  _const_spec(bias_a.shape), _const_spec(wukT.shape), _const_spec(wuvT.shape)],
        out_specs=pl.BlockSpec((None, T, A_HEADS * HEAD_DIM), lambda b, i: (b, i, 0)),
        scratch_shapes=[pltpu.VMEM((nk, T, T), I32),
                        pltpu.VMEM((nk, KEY_BITS + 1, SUBLANES, T), I32),
                        pltpu.VMEM(((nk + 2) * SUBLANES, T), I32),
                        pltpu.VMEM((A_HEADS, A_LATENT, T), MM_DTYPE),
                        pltpu.VMEM((A_HEADS, A_LATENT + SUM_ROWS, T), F32),
                        pltpu.VMEM((A_HEADS, T, T), F32),
                        pltpu.VMEM((A_HEADS * HEAD_DIM, T), F32)],
        compiler_params=_cparams(2), name="dsa_attention",
    )(aqT, iqT, iwT, kidx, kv, kvT, bias_a, wukT, wuvT)


def _diff_kernel(qT_ref, k_ref, vT_ref, bias_ref, lam_ref, gsub_ref, out_ref,
                 qz_ref, acc_ref, s_ref, oT_ref, *, lam_init):
    T = ATT_TILE
    dv = 2 * HEAD_DIM
    n_chain = 2 * B_HEADS
    qi = pl.program_id(1)
    half = lax.broadcasted_iota(I32, (dv, T), 0) < HEAD_DIM
    for h in range(B_HEADS):
        q = qT_ref[h * dv:(h + 1) * dv, :].astype(F32)
        qz_ref[2 * h] = jnp.where(half, q, 0.0).astype(MM_DTYPE)
        qz_ref[2 * h + 1] = jnp.where(half, 0.0, q).astype(MM_DTYPE)
    acc_ref[...] = jnp.zeros(acc_ref.shape, F32)

    def stage_logits(kj, c):
        h = c // 2
        off = jnp.minimum(qi - kj, N_OFFSETS - 1)
        s = bias_ref[h, off] + jnp.dot(k_ref[kj, :, h * dv:(h + 1) * dv], qz_ref[c],
                                       preferred_element_type=F32)
        s_ref[c] = s
        return jnp.max(s, axis=0, keepdims=True)

    tile_max0 = tuple(stage_logits(0, c) for c in range(n_chain))

    def step(kj, carry, last):
        ms, tile_max = carry
        new_m, new_max = [], []
        for c in range(n_chain):
            h = c // 2
            m_new = jnp.maximum(ms[c], tile_max[c])
            p = jnp.exp2(s_ref[c] - m_new).astype(MM_DTYPE)
            alpha = jnp.exp2(ms[c] - m_new)
            if not last:
                new_max.append(stage_logits(kj + 1, c))
            vT = _with_ones_row(vT_ref[kj, h * dv:(h + 1) * dv, :])
            acc_ref[c] = alpha * acc_ref[c] + jnp.dot(vT, p, preferred_element_type=F32)
            new_m.append(m_new)
        return tuple(new_m), tuple(new_max)

    carry = _loop_grouped(qi, lambda kj, cr: step(kj, cr, False),
                          (_initial_max(n_chain, T), tile_max0), log2_group=2)
    step(qi, carry, True)

    lr = lam_ref[...]
    lam = (jnp.exp(jnp.sum(lr[0:1, :] * lr[1:2, :], axis=1, keepdims=True))
           - jnp.exp(jnp.sum(lr[2:3, :] * lr[3:4, :], axis=1, keepdims=True)) + lam_init)

    def normalised(c):
        return acc_ref[c, 0:dv, :] / acc_ref[c, dv:dv + 1, :]

    for h in range(B_HEADS):
        attn = normalised(2 * h) - lam * normalised(2 * h + 1)
        y = attn * lax.rsqrt(jnp.mean(attn * attn, axis=0, keepdims=True) + EPS)
        oT_ref[h * dv:(h + 1) * dv, :] = y * gsub_ref[...] * (1.0 - lam_init)
    out_ref[...] = oT_ref[...].T.astype(out_ref.dtype)


def _diff(bqT, bk, bvT, bias_b, lam_rows, gsub, lam_init):
    bsz, _, seq = bqT.shape
    T = ATT_TILE
    nk = seq // T
    dv = 2 * HEAD_DIM
    qspec = pl.BlockSpec((None, B_HEADS * dv, T), lambda b, i: (b, 0, i))
    return pl.pallas_call(
        functools.partial(_diff_kernel, lam_init=lam_init),
        out_shape=jax.ShapeDtypeStruct((bsz, seq, B_HEADS * dv), MM_DTYPE),
        grid=(bsz, nk),
        in_specs=[qspec,
                  pl.BlockSpec((None, nk, T, B_HEADS * dv), lambda b, i: (b, 0, 0, 0)),
                  pl.BlockSpec((None, nk, B_HEADS * dv, T), lambda b, i: (b, 0, 0, 0)),
                  _const_spec(bias_b.shape), _const_spec((4, HEAD_DIM)), _const_spec((dv, 1))],
        out_specs=pl.BlockSpec((None, T, B_HEADS * dv), lambda b, i: (b, i, 0)),
        scratch_shapes=[pltpu.VMEM((2 * B_HEADS, dv, T), MM_DTYPE),
                        pltpu.VMEM((2 * B_HEADS, dv + SUM_ROWS, T), F32),
                        pltpu.VMEM((2 * B_HEADS, T, T), F32),
                        pltpu.VMEM((B_HEADS * dv, T), F32)],
        compiler_params=_cparams(2), name="diff_attention",
    )(bqT, bk, bvT, bias_b, lam_rows, gsub)


def _dil_kernel(cur_ref, halo_ref, bias_ref, out_ref, lse_ref, *, tq):
    n = C_BAND
    wid = C_HPG * HEAD_DIM
    halo_lo = jnp.where(pl.program_id(2) == 0, jnp.int32(n), jnp.int32(0))
    i = lax.broadcasted_iota(I32, (n, 2 * n), 0)
    j = lax.broadcasted_iota(I32, (n, 2 * n), 1)
    lane_head = lax.broadcasted_iota(I32, (n, wid), 1) // HEAD_DIM
    in_head = [lane_head == h for h in range(C_HPG)]
    band = jnp.where(j >= i, jnp.where(j <= i + n, 0.0, NEG), NEG)
    band0 = jnp.where(j >= jnp.maximum(i, halo_lo), jnp.where(j <= i + n, 0.0, NEG), NEG)
    bias = [bias_ref[h] + band for h in range(C_HPG)]
    bias0 = [bias_ref[h] + band0 for h in range(C_HPG)]

    def band_rows(r, c, lo, hi):
        if c == 0:
            return jnp.concatenate([halo_ref[r, :, lo:hi], cur_ref[r, 0:n, lo:hi]], axis=0)
        return cur_ref[r, (c - 1) * n:(c + 1) * n, lo:hi]

    blocks = [(r, c) for r in range(cur_ref.shape[0]) for c in range(tq // n)]
    logits = []
    for r, c in blocks:
        q = cur_ref[r, c * n:(c + 1) * n, 0:wid].astype(F32) * (ATTN_SCALE * LOG2E)
        keys = band_rows(r, c, wid, 2 * wid)
        for h in range(C_HPG):
            qh = jnp.where(in_head[h], q, 0.0).astype(MM_DTYPE)
            s = lax.dot_general(qh, keys, (((1,), (1,)), ((), ())), preferred_element_type=F32)
            logits.append(s + (bias0 if c == 0 else bias)[h])
    for k, (r, c) in enumerate(blocks):
        vals = band_rows(r, c, 2 * wid, 3 * wid)
        out = jnp.zeros((n, wid), F32)
        lse = jnp.zeros((n, wid), F32)
        for h in range(C_HPG):
            s = logits[k * C_HPG + h]
            m = jnp.max(s, axis=1, keepdims=True)
            p = jnp.exp2(s - m)
            den = jnp.sum(p, axis=1, keepdims=True)
            o = jnp.dot(p.astype(MM_DTYPE), vals, preferred_element_type=F32) * (1.0 / den)
            out = jnp.where(in_head[h], o, out)
            lse = jnp.where(in_head[h], m + jnp.log2(den), lse)
        out_ref[r, c * n:(c + 1) * n, :] = out
        lse_ref[r, c * n:(c + 1) * n, :] = lse


def _dilated_group(cg, bias_g, g):
    bsz, dil, m, gw = cg.shape
    wid = C_HPG * HEAD_DIM
    n = C_BAND
    assert m % n == 0 and gw == 3 * wid
    tq = min(m, 512)
    n_res = max(1, min(dil, 512 // tq))
    cur = pl.BlockSpec((None, n_res, tq, gw), lambda b, r, i: (b, r, i, 0))
    halo = pl.BlockSpec((None, n_res, n, gw),
                        lambda b, r, i: (b, r, jnp.maximum(i * (tq // n) - 1, 0), 0))
    outspec = pl.BlockSpec((None, n_res, tq, wid), lambda b, r, i: (b, r, i, 0))
    return pl.pallas_call(
        functools.partial(_dil_kernel, tq=tq),
        out_shape=[jax.ShapeDtypeStruct((bsz, dil, m, wid), F32)] * 2,
        grid=(bsz, dil // n_res, m // tq),
        in_specs=[cur, halo, pl.BlockSpec((C_HPG, n, 2 * n), lambda b, r, i: (0, 0, 0))],
        out_specs=[outspec, outspec],
        compiler_params=_cparams(3), name=f"dilated_group{g}",
    )(cg, cg, bias_g)


def _merge_ffn_kernel(x_ref, mod_ref, g1_ref, g2_ref, gf_ref, oa_ref, ob_ref,
                      c0_ref, c1_ref, c2_ref, s0_ref, s1_ref, s2_ref,
                      wz, wba, wbb, wbc, wo, wgu, wd, out_ref, tok_ref, *, final_norm):
    x = x_ref[...]
    tm = x.shape[0]
    h = _rms(x) * g1_ref[...]
    h = h * (1.0 + mod_ref[1:2, :]) + mod_ref[0:1, :]
    hb = h.astype(MM_DTYPE)

    def token_order(k, ref):
        dil, _, w = ref.shape
        if dil == 1:
            return ref[0]
        n_chunk = w // LANES
        for r in range(dil):
            for j in range(n_chunk):
                tok_ref[k * n_chunk + j, pl.ds(r, tm // dil, stride=dil), :] = ref[
                    r, :, j * LANES:(j + 1) * LANES]
        return jnp.concatenate([tok_ref[k * n_chunk + j] for j in range(n_chunk)], axis=1)

    s0, s1, s2 = s0_ref[0], token_order(0, s1_ref), token_order(1, s2_ref)
    c0, c1, c2 = c0_ref[0], token_order(2, c1_ref), token_order(3, c2_ref)
    mx = jnp.maximum(jnp.maximum(s0, s1), s2)
    e0, e1, e2 = jnp.exp2(s0 - mx), jnp.exp2(s1 - mx), jnp.exp2(s2 - mx)
    oc = (e0 * c0 + e1 * c1 + e2 * c2) / (e0 + e1 + e2)

    d = x.shape[1]

    def gated(k, o, wb):
        z = jnp.dot(hb, wz[:, k * d:(k + 1) * d], preferred_element_type=F32)
        return jax.nn.sigmoid(z) * jnp.dot(o, wb[...], preferred_element_type=F32)

    merged = (gated(0, oa_ref[...], wba) + gated(1, ob_ref[...], wbb)
              + gated(2, oc.astype(MM_DTYPE), wbc))
    y = jnp.dot(merged.astype(MM_DTYPE), wo[...], preferred_element_type=F32)
    x = x + mod_ref[2:3, :] * y

    h = _rms(x) * g2_ref[...]
    h = h * (1.0 + mod_ref[4:5, :]) + mod_ref[3:4, :]
    hb = h.astype(MM_DTYPE)
    acc = jnp.zeros(x.shape, F32)
    for c in range(D_FF // FFN_CHUNK):
        cols = slice(c * FFN_CHUNK, (c + 1) * FFN_CHUNK)
        fg = jnp.dot(hb, wgu[:, cols], preferred_element_type=F32)
        fu = jnp.dot(hb, wgu[:, D_FF + c * FFN_CHUNK:D_FF + (c + 1) * FFN_CHUNK],
                     preferred_element_type=F32)
        act = (fg * jax.nn.sigmoid(fg) * fu).astype(MM_DTYPE)
        acc = acc + jnp.dot(act, wd[cols, :], preferred_element_type=F32)
    y = x + mod_ref[5:6, :] * acc
    if final_norm:
        y = _rms(y) * gf_ref[...]
    out_ref[...] = y


def _merge_ffn(x2d, mod_l, g1, g2, gf, oa, ob, ocs, lses, ws, seq, final_norm):
    n, d = x2d.shape
    tm = ROW_TILE
    per_b = seq // tm
    row = lambda wd: pl.BlockSpec((tm, wd), lambda i: (i, 0))
    vec = _const_spec((1, d))
    wid = C_HPG * HEAD_DIM
    res = [pl.BlockSpec((None, dil, tm // dil, wid), lambda i: (i // per_b, 0, i % per_b, 0))
           for _, dil in C_GROUPS]
    in_specs = [row(d), pl.BlockSpec((None, 6, d), lambda i: (i // per_b, 0, 0)), vec, vec, vec,
                row(oa.shape[1]), row(ob.shape[1])] + res + res
    in_specs += [_const_spec(w.shape) for w in ws]
    return pl.pallas_call(
        functools.partial(_merge_ffn_kernel, final_norm=final_norm),
        out_shape=jax.ShapeDtypeStruct((n, d), F32), grid=(n // tm,),
        in_specs=in_specs, out_specs=row(d),
        scratch_shapes=[pltpu.VMEM((4 * wid // LANES, tm, LANES), F32)],
        compiler_params=_cparams(1), name="merge_ffn",
    )(x2d, mod_l, g1, g2, gf, oa, ob, *ocs, *lses, *ws)


def kernel(x, c, w_ada, b_ada, g_norm1, w_in, w_uk, w_uv, g_kv, lam_q1, lam_k1, lam_q2, lam_k2,
           g_subln, w_branch_a, w_branch_b, w_branch_c, w_out, g_norm2, w_gate_up, w_down,
           rel_bias, g_final):
    bsz, seq, d = x.shape
    depth = w_ada.shape[0]
    T = ATT_TILE
    nk = seq // T
    assert d == D_MODEL and seq % T == 0 and seq % ROW_TILE == 0 and seq % IN_ROW_TILE == 0
    n = bsz * seq
    cast = lambda w: w.astype(MM_DTYPE)

    thresholds = _bucket_thresholds(seq + 2 * C_BAND * C_GROUPS[-1][1])
    assert seq <= N_OFFSETS * T or (N_OFFSETS - 2) * T + 1 >= thresholds[-1]
    tab = rel_bias.reshape(-1)
    bias_a = _bias_att_tiles(tab, 0, A_HEADS, T, thresholds)
    bias_b = _bias_att_tiles(tab, A_HEADS, B_HEADS, T, thresholds)
    bias_c = _bias_dil_tiles(tab, A_HEADS + B_HEADS, thresholds)

    mod = _modulation(c, w_ada, b_ada).reshape(depth, bsz, 6, d)

    splits = (A_HEADS * HEAD_DIM, A_LATENT, IDX_HEADS * IDX_DIM, IDX_DIM, IDX_HEADS,
              B_HEADS * 2 * HEAD_DIM, B_HEADS * 2 * HEAD_DIM, B_HEADS * 2 * HEAD_DIM,
              C_HEADS * HEAD_DIM, C_HEADS * HEAD_DIM, C_HEADS * HEAD_DIM, d, d, d)
    offs = np.concatenate([[0], np.cumsum(splits)])
    seg = lambda w, k: w[:, int(offs[k]):int(offs[k + 1])]

    x2d = x.reshape(n, d)
    for l in range(depth):
        wl = w_in[l]
        wid = C_HPG * HEAD_DIM
        w_iw = jnp.pad(seg(wl, 4), ((0, 0), (0, 16 - IDX_HEADS)))
        wT_all = cast(jnp.concatenate([seg(wl, 0), seg(wl, 2), seg(wl, 5), seg(wl, 7), w_iw], axis=1).T)
        w_all = cast(jnp.concatenate(
            [seg(wl, 1), seg(wl, 6)]
            + [seg(wl, k)[:, g * wid:(g + 1) * wid] for g in range(len(C_GROUPS)) for k in (8, 9, 10)]
            + [seg(wl, 3)], axis=1))
        ws_in = [wT_all, w_all]
        g1 = g_norm1[l].reshape(1, d)
        (aqT, iqT, iwT, bqT, bvT, ik, kv, kvT, bk, cg0, cg1, cg2) = _in_proj(
            x2d, mod[l], g1, g_kv[l].reshape(1, A_LATENT), ws_in, bsz, seq)

        o_a = _dsa(aqT, iqT, iwT,
                   ik.reshape(bsz, nk, T, IDX_DIM), kv.reshape(bsz, nk, T, A_LATENT), kvT,
                   bias_a, cast(w_uk[l].transpose(0, 2, 1)), cast(w_uv[l].transpose(0, 2, 1)))
        o_a = o_a.reshape(n, -1)

        lam_init = 0.8 - 0.6 * math.exp(-0.3 * l)
        lam_rows = jnp.stack([lam_q1[l], lam_k1[l], lam_q2[l], lam_k2[l]])
        dv = 2 * HEAD_DIM
        o_b = _diff(bqT, bk.reshape(bsz, nk, T, B_HEADS * dv), bvT, bias_b, lam_rows,
                    g_subln[l].reshape(dv, 1), lam_init)
        o_b = o_b.reshape(n, -1)

        ocs, lses = [], []
        for g, (cg, (window, dil)) in enumerate(zip((cg0, cg1, cg2), C_GROUPS)):
            assert window // dil == C_BAND
            o, s = _dilated_group(cg, bias_c[g * C_HPG:(g + 1) * C_HPG], g)
            ocs.append(o)
            lses.append(s)

        ws_out = [cast(wl[:, int(offs[11]):int(offs[14])]),
                  cast(w_branch_a[l]), cast(w_branch_b[l]), cast(w_branch_c[l]), cast(w_out[l]),
                  cast(w_gate_up[l]), cast(w_down[l])]
        x2d = _merge_ffn(x2d, mod[l], g1, g_norm2[l].reshape(1, d), g_final.reshape(1, d),
                         o_a, o_b, ocs, lses, ws_out, seq, final_norm=(l == depth - 1))
    return x2d.reshape(bsz, seq, d)
```

```python
import functools
import math

import numpy as np
import jax
import jax.numpy as jnp
from jax import lax
from jax.experimental import pallas as pl
from jax.experimental.pallas import tpu as pltpu

D_MODEL = 1024
HEAD_DIM = 64
ATTN_SCALE = HEAD_DIM ** -0.5
LOG2E = math.log2(math.e)
A_HEADS = 8
A_LATENT = 128
IDX_HEADS = 8
IDX_DIM = 64
IDX_SCALE = (IDX_HEADS * IDX_DIM) ** -0.5
TOPK_MAX = 256
B_HEADS = 4
C_GROUPS = ((128, 1), (512, 4), (2048, 16))
C_HPG = 4
C_HEADS = C_HPG * len(C_GROUPS)
N_BUCKETS = 32
MAX_DISTANCE = 2048
N_BIAS_HEADS = A_HEADS + B_HEADS + C_HEADS
D_FF = -(-8 * D_MODEL // (3 * 256)) * 256
EPS = 1e-6

MM_DTYPE = jnp.bfloat16
F32 = jnp.float32
I32 = jnp.int32

ATT_TILE = 256
N_OFFSETS = 8
C_BAND = 128
C_ROWS = 1024
ROW_TILE = 512
IN_ROW_TILE = 1024
FFN_CHUNK = 256
MOD_COLS = 1536
NEG = -1e30
INT_MIN = -2 ** 31
LANES = 128
SUBLANES = 8
KEY_BITS = 32
SEARCH_TILES = 4
VMEM_LIMIT = 56 * 1024 * 1024


def _cparams(n_axes, vmem=VMEM_LIMIT):
    return pltpu.CompilerParams(dimension_semantics=("arbitrary",) * n_axes,
                                vmem_limit_bytes=vmem)


def _const_spec(shape):
    nd = len(shape)
    return pl.BlockSpec(shape, lambda *_: (0,) * nd, pipeline_mode=pl.Buffered(1))


def _bucket_thresholds(max_dist):
    n = np.arange(max_dist + 1)
    max_exact = N_BUCKETS // 2
    nf = np.maximum(n, 1).astype(np.float32)
    large = max_exact + (np.log(nf / np.float32(max_exact))
                         / np.float32(math.log(MAX_DISTANCE / max_exact))
                         * np.float32(N_BUCKETS - max_exact)).astype(np.int32)
    large = np.minimum(large, N_BUCKETS - 1)
    bucket = np.where(n < max_exact, n, large)
    assert np.all(np.diff(bucket) >= 0)
    thr = []
    for k in range(1, N_BUCKETS):
        idx = np.nonzero(bucket >= k)[0]
        thr.append(int(idx[0]) if idx.size else None)
    return thr


def _bias_from_dist(dist, tab_ref, col, thresholds, lo=0, hi=None):
    reached = [k for k, thr in enumerate(thresholds, start=1) if thr is not None]
    base = max([0] + [k for k in reached if thresholds[k - 1] <= lo])
    b = jnp.full(dist.shape, tab_ref[base * N_BIAS_HEADS + col], F32)
    for k in reached:
        thr = thresholds[k - 1]
        if thr > lo and (hi is None or thr <= hi):
            b = jnp.where(dist >= thr, tab_ref[k * N_BIAS_HEADS + col], b)
    return b


def _bias_att_kernel(tab_ref, out_ref, *, head0, thresholds):
    h = pl.program_id(0)
    n_off, tile, _ = out_ref.shape
    row = lax.broadcasted_iota(I32, (tile, tile), 0)
    colq = lax.broadcasted_iota(I32, (tile, tile), 1)
    for o in range(n_off):
        dist = jnp.maximum(o * tile + colq - row, 0)
        lo, hi = max(o * tile - (tile - 1), 0), o * tile + tile - 1
        b = _bias_from_dist(dist, tab_ref, head0 + h, thresholds, lo, hi) * LOG2E
        if o == 0:
            b = jnp.where(row <= colq, b, NEG)
        out_ref[o] = b


def _bias_att_tiles(tab, head0, n_heads, tile, thresholds):
    return pl.pallas_call(
        functools.partial(_bias_att_kernel, head0=head0, thresholds=thresholds),
        out_shape=jax.ShapeDtypeStruct((n_heads, N_OFFSETS, tile, tile), F32),
        grid=(n_heads,),
        in_specs=[pl.BlockSpec(memory_space=pltpu.SMEM)],
        out_specs=pl.BlockSpec((None, N_OFFSETS, tile, tile), lambda h: (h, 0, 0, 0)),
        compiler_params=_cparams(1),
        name="bias_att_tiles",
    )(tab)


def _bias_dil_kernel(tab_ref, out_ref, *, head0, thresholds):
    h = pl.program_id(0)
    g = h // C_HPG
    dil = jnp.where(g == 0, C_GROUPS[0][1], jnp.where(g == 1, C_GROUPS[1][1], C_GROUPS[2][1]))
    i = lax.broadcasted_iota(I32, (C_BAND, 2 * C_BAND), 0)
    j = lax.broadcasted_iota(I32, (C_BAND, 2 * C_BAND), 1)
    dist = jnp.maximum((i - j + C_BAND) * dil, 0)
    out_ref[...] = _bias_from_dist(dist, tab_ref, head0 + h, thresholds) * LOG2E


def _bias_dil_tiles(tab, head0, thresholds):
    return pl.pallas_call(
        functools.partial(_bias_dil_kernel, head0=head0, thresholds=thresholds),
        out_shape=jax.ShapeDtypeStruct((C_HEADS, C_BAND, 2 * C_BAND), F32),
        grid=(C_HEADS,),
        in_specs=[pl.BlockSpec(memory_space=pltpu.SMEM)],
        out_specs=pl.BlockSpec((None, C_BAND, 2 * C_BAND), lambda h: (h, 0, 0)),
        compiler_params=_cparams(1),
        name="bias_dil_tiles",
    )(tab)


def _mod_kernel(c_ref, w_ref, b_ref, out_ref):
    c = c_ref[...]
    ca = (c * jax.nn.sigmoid(c)).astype(MM_DTYPE)
    out_ref[...] = jnp.dot(ca, w_ref[...].astype(MM_DTYPE), preferred_element_type=F32) + b_ref[...]


def _modulation(c, w_ada, b_ada):
    depth, d, wid = w_ada.shape
    bsz = c.shape[0]
    tn = MOD_COLS
    assert wid % tn == 0
    return pl.pallas_call(
        _mod_kernel,
        out_shape=jax.ShapeDtypeStruct((depth, bsz, wid), F32),
        grid=(depth, wid // tn),
        in_specs=[pl.BlockSpec((bsz, d), lambda l, j: (0, 0)),
                  pl.BlockSpec((None, d, tn), lambda l, j: (l, 0, j)),
                  pl.BlockSpec((None, 1, tn), lambda l, j: (l, 0, j))],
        out_specs=pl.BlockSpec((None, bsz, tn), lambda l, j: (l, 0, j)),
        compiler_params=_cparams(2),
        name="adaln_modulation",
    )(c, w_ada, b_ada.reshape(depth, 1, wid))


def _rms(x):
    return x * lax.rsqrt(jnp.mean(x * x, axis=-1, keepdims=True) + EPS)


IN_ROWS_T = (("aq", A_HEADS * HEAD_DIM), ("iq", IDX_HEADS * IDX_DIM), ("bq", B_HEADS * 2 * HEAD_DIM),
             ("bv", B_HEADS * 2 * HEAD_DIM), ("iw", 16))
IN_COLS = (("kv", A_LATENT), ("bk", B_HEADS * 2 * HEAD_DIM), ("c", 3 * C_HEADS * HEAD_DIM),
           ("ik", IDX_DIM))


def _segments(layout):
    out, start = {}, 0
    for name, width in layout:
        out[name] = slice(start, start + width)
        start += width
    return out


def _in_kernel(x_ref, mod_ref, g1_ref, gkv_ref, wT_ref, w_ref,
               o_aqT, o_iqT, o_iwT, o_bqT, o_bvT, o_ik, o_kv, o_kvT, o_bk, o_c0, o_c1, o_c2,
               c_scr):
    T = ATT_TILE
    tm = x_ref.shape[0]
    h = _rms(x_ref[...]) * g1_ref[...]
    h = h * (1.0 + mod_ref[1:2, :]) + mod_ref[0:1, :]
    hb = h.astype(MM_DTYPE)
    rows_t, cols = _segments(IN_ROWS_T), _segments(IN_COLS)

    def mm(name):
        return jnp.dot(hb, w_ref[:, cols[name]], preferred_element_type=F32)

    def mm_t(name):
        return lax.dot_general(wT_ref[rows_t[name], :], hb, (((1,), (1,)), ((), ())),
                               preferred_element_type=F32)


    o_aqT[...] = mm_t("aq").astype(o_aqT.dtype)
    o_iqT[...] = mm_t("iq").astype(o_iqT.dtype)
    o_bqT[...] = (mm_t("bq") * (ATTN_SCALE * LOG2E)).astype(o_bqT.dtype)
    o_iwT[...] = (mm_t("iw") * IDX_SCALE)[:IDX_HEADS]
    bvT = mm_t("bv").astype(o_bvT.dtype)
    kv = _rms(mm("kv")) * gkv_ref[...]
    kvT = kv.T.astype(o_kvT.dtype)
    for j in range(tm // T):
        o_bvT[j] = bvT[:, j * T:(j + 1) * T]
        o_kvT[j] = kvT[:, j * T:(j + 1) * T]
    o_kv[...] = kv.astype(o_kv.dtype)
    o_ik[...] = mm("ik").astype(o_ik.dtype)
    o_bk[...] = mm("bk").astype(o_bk.dtype)

    yc = mm("c")
    n_chunk = yc.shape[1] // LANES
    for j in range(n_chunk):
        c_scr[j] = yc[:, j * LANES:(j + 1) * LANES]
    per_group = n_chunk // len(C_GROUPS)
    for g, o_c in enumerate((o_c0, o_c1, o_c2)):
        dil = C_GROUPS[g][1]
        for r in range(dil):
            for jj in range(per_group):
                o_c[r, :, jj * LANES:(jj + 1) * LANES] = c_scr[
                    g * per_group + jj, pl.ds(r, tm // dil, stride=dil), :].astype(o_c.dtype)


def _in_proj(x2d, mod_l, g1, gkv, ws, bsz, seq):
    n, d = x2d.shape
    tm = IN_ROW_TILE
    T = ATT_TILE
    per_b = seq // tm
    nk = seq // T
    hd = A_HEADS * HEAD_DIM
    bw = B_HEADS * 2 * HEAD_DIM
    gw = 3 * C_HPG * HEAD_DIM
    in_specs = [pl.BlockSpec((tm, d), lambda i: (i, 0)),
                pl.BlockSpec((None, 6, d), lambda i: (i // per_b, 0, 0)),
                _const_spec((1, d)), _const_spec((1, A_LATENT))]
    in_specs += [_const_spec(w.shape) for w in ws]

    def tspec(rows):
        return pl.BlockSpec((None, rows, tm), lambda i: (i // per_b, 0, i % per_b))

    def tile_tspec(rows):
        return pl.BlockSpec((None, tm // T, rows, T), lambda i: (i // per_b, i % per_b, 0, 0))

    def rspec(wd):
        return pl.BlockSpec((tm, wd), lambda i: (i, 0))

    def cspec(dil):
        return pl.BlockSpec((None, dil, tm // dil, gw), lambda i: (i // per_b, 0, i % per_b, 0))

    sds = jax.ShapeDtypeStruct
    out_specs = [tspec(hd), tspec(IDX_HEADS * IDX_DIM), tspec(IDX_HEADS), tspec(bw),
                 tile_tspec(bw), rspec(IDX_DIM), rspec(A_LATENT), tile_tspec(A_LATENT), rspec(bw)]
    out_shape = [sds((bsz, hd, seq), MM_DTYPE), sds((bsz, IDX_HEADS * IDX_DIM, seq), MM_DTYPE),
                 sds((bsz, IDX_HEADS, seq), F32), sds((bsz, bw, seq), MM_DTYPE),
                 sds((bsz, nk, bw, T), MM_DTYPE), sds((n, IDX_DIM), MM_DTYPE),
                 sds((n, A_LATENT), MM_DTYPE), sds((bsz, nk, A_LATENT, T), MM_DTYPE),
                 sds((n, bw), MM_DTYPE)]
    for _, dil in C_GROUPS:
        out_specs.append(cspec(dil))
        out_shape.append(sds((bsz, dil, seq // dil, gw), MM_DTYPE))
    return pl.pallas_call(
        _in_kernel, out_shape=out_shape, grid=(n // tm,),
        in_specs=in_specs, out_specs=out_specs,
        scratch_shapes=[pltpu.VMEM((3 * gw // LANES, tm, LANES), F32)],
        compiler_params=_cparams(1), name="in_proj",
    )(x2d, mod_l, g1, gkv, *ws)


def _initial_max(n_chains, tile):
    return tuple(jnp.full((1, tile), NEG, F32) for _ in range(n_chains))


def _loop_grouped(n, body, init, log2_group):
    carry, done = init, jnp.int32(0)
    for lg in range(log2_group, -1, -1):
        group = 1 << lg
        trips = lax.shift_right_logical(n - done, jnp.int32(lg))

        def grouped(i, c, group=group, done=done):
            for j in range(group):
                c = body(done + group * i + j, c)
            return c

        carry = lax.fori_loop(0, trips, grouped, carry)
        done = done + group * trips
    return carry


SUM_ROWS = 16


def _with_ones_row(vT):
    row = lax.broadcasted_iota(I32, (SUM_ROWS, vT.shape[1]), 0)
    ones = jnp.where(row == 0, 1.0, 0.0).astype(vT.dtype)
    return jnp.concatenate([vT, ones], axis=0)


def _bit_transpose32(words):
    a = list(words)
    j, mask = 16, 0x0000FFFF
    while j:
        k = 0
        while k < 32:
            t = (a[k] ^ lax.shift_right_logical(a[k + j], jnp.int32(j))) & jnp.int32(mask)
            a[k] = a[k] ^ t
            a[k + j] = a[k + j] ^ lax.shift_left(t, jnp.int32(j))
            k = (k + j + 1) & ~j
        j >>= 1
        mask = (mask ^ (mask << j)) & 0xFFFFFFFF
    return a


def _dsa_kernel(aqT_ref, iqT_ref, iwT_ref, kidx_ref, kv_ref, kvT_ref, bias_ref, wukT_ref, wuvT_ref,
                out_ref, keys_ref, planes_ref, found_ref, qlat_ref, acc_ref, s_ref, oT_ref, *, topk):
    T = ATT_TILE
    qi = pl.program_id(1)
    nk = qi + 1

    for h in range(A_HEADS):
        q = jnp.dot(wukT_ref[h], aqT_ref[h * HEAD_DIM:(h + 1) * HEAD_DIM, :],
                    preferred_element_type=F32) * (ATTN_SCALE * LOG2E)
        qlat_ref[h] = q.astype(qlat_ref.dtype)

    s_loc = lax.broadcasted_iota(I32, (T, T), 0)
    t_loc = lax.broadcasted_iota(I32, (T, T), 1)

    def score_tile(kj, diagonal):
        kt = kidx_ref[kj]
        acc = jnp.zeros((T, T), F32)
        for h in range(IDX_HEADS):
            s = jnp.dot(kt, iqT_ref[h * IDX_DIM:(h + 1) * IDX_DIM, :], preferred_element_type=F32)
            acc = acc + jnp.maximum(s, 0.0) * iwT_ref[h:h + 1, :]
        bits = lax.bitcast_convert_type(acc, I32)
        key = jnp.where(bits < 0, bits ^ jnp.int32(0x7FFFFFFF), bits)
        if diagonal:
            key = jnp.where(s_loc <= t_loc, key, jnp.int32(INT_MIN))
        keys_ref[kj] = key
        planes = _bit_transpose32([key[SUBLANES * i:SUBLANES * (i + 1), :] ^ jnp.int32(INT_MIN)
                                   for i in range(KEY_BITS)])
        for b in range(KEY_BITS):
            planes_ref[kj, b] = planes[b]

    def off_diagonal(kj, carry):
        score_tile(kj, False)
        return carry

    _loop_grouped(qi, off_diagonal, 0, log2_group=2)
    score_tile(qi, True)

    n_tiles = planes_ref.shape[0]

    @pl.when(qi == 0)
    def _():
        def clear_planes(kj, carry):
            for b in range(KEY_BITS):
                planes_ref[kj, b] = jnp.zeros((SUBLANES, T), I32)
            return carry

        lax.fori_loop(1, n_tiles, clear_planes, 0)

    one = jnp.int32(1)
    nil = jnp.int32(0)
    zero = jnp.zeros((1, T), I32)

    n_rows = n_tiles * SUBLANES

    def search(tiles):
        rows = tiles * SUBLANES
        tile_of_row = lax.broadcasted_iota(I32, (rows, T), 0) // SUBLANES
        tied0 = jnp.where(tile_of_row < nk, jnp.int32(-1), nil)

        def bit_step(b, carry):
            tied, n_gt, kth_u = carry
            ones = tied & planes_ref[0:tiles, b].reshape(rows, T)
            n1 = jnp.sum(lax.population_count(ones), axis=0, keepdims=True)
            take = (n_gt + n1) >= topk
            tied = jnp.where(take, ones, tied ^ ones)
            n_gt = jnp.where(take, n_gt, n_gt + n1)
            kth_u = jnp.where(take, kth_u | lax.shift_left(one, jnp.int32(KEY_BITS - 1) - b), kth_u)
            return tied, n_gt, kth_u

        tied, n_gt, kth_u = lax.fori_loop(0, KEY_BITS, bit_step, (tied0, zero, zero))
        found_ref[0:rows, :] = tied
        if rows < n_rows:
            found_ref[rows:n_rows, :] = jnp.zeros((n_rows - rows, T), I32)
        found_ref[n_rows:n_rows + SUBLANES, :] = jnp.broadcast_to(n_gt, (SUBLANES, T))
        found_ref[n_rows + SUBLANES:n_rows + 2 * SUBLANES, :] = jnp.broadcast_to(kth_u, (SUBLANES, T))

    covered = 0
    for tiles in list(range(SEARCH_TILES, n_tiles, SEARCH_TILES)) + [n_tiles]:
        pl.when(jnp.logical_and(nk > covered, nk <= tiles))(functools.partial(search, tiles))
        covered = tiles

    tied = found_ref[0:n_rows, :]
    n_gt = found_ref[n_rows:n_rows + 1, :]
    kth_u = found_ref[n_rows + SUBLANES:n_rows + SUBLANES + 1, :]
    n_eq = jnp.sum(lax.population_count(tied), axis=0, keepdims=True)
    n_ge = jnp.where(kth_u == nil, n_gt, n_gt + n_eq)
    kth = jnp.maximum(kth_u ^ jnp.int32(INT_MIN), jnp.int32(INT_MIN + 1))

    @pl.when(jnp.max(n_ge) > topk)
    def _():
        need = topk - n_gt
        r_bits = SUBLANES.bit_length() - 1

        before, cut_tile, rank = zero, zero, need
        words = jnp.zeros((SUBLANES, T), I32)
        for kj in range(n_tiles):
            tile_words = tied[kj * SUBLANES:(kj + 1) * SUBLANES, :]
            after = before + jnp.sum(lax.population_count(tile_words), axis=0, keepdims=True)
            here = jnp.where(before < need, jnp.where(after >= need, one, nil), nil) == one
            cut_tile = jnp.where(here, jnp.int32(kj), cut_tile)
            rank = jnp.where(here, need - before, rank)
            words = jnp.where(here, tile_words, words)
            before = after

        word_r = lax.broadcasted_iota(I32, (SUBLANES, T), 0)
        local = zero
        for bit in reversed(range(T.bit_length() - 1)):
            cand = local | jnp.int32(1 << bit)
            p_i = lax.shift_right_logical(cand, jnp.int32(r_bits))
            p_r = cand & jnp.int32(SUBLANES - 1)
            above = jnp.where(p_i == nil, nil,
                              lax.shift_left(jnp.full_like(p_i, -1), jnp.int32(KEY_BITS) - p_i))
            at_i = lax.shift_right_logical(
                words, jnp.broadcast_to(jnp.int32(KEY_BITS - 1) - p_i, words.shape)) & one
            below = lax.population_count(words & above) + jnp.where(word_r < p_r, at_i, nil)
            local = jnp.where(jnp.sum(below, axis=0, keepdims=True) < rank, cand, local)
        cut = cut_tile * T + local

        def demote(kj, carry):
            kk = keys_ref[kj]
            lowered = jnp.where((kj * T + s_loc) > cut, kth - one, kk)
            keys_ref[kj] = jnp.where(kk == kth, lowered, kk)
            return carry

        lax.fori_loop(0, nk, demote, 0)

    acc_ref[...] = jnp.zeros(acc_ref.shape, F32)

    def stage_logits(kj):
        unselected = jnp.where(keys_ref[kj] >= kth, 0.0, NEG)
        off = jnp.minimum(qi - kj, N_OFFSETS - 1)
        kvt = kv_ref[kj]

        def one_head(h):
            s = bias_ref[h, off] + jnp.dot(kvt, qlat_ref[h], preferred_element_type=F32)
            s = s + unselected
            s_ref[h] = s
            return jnp.max(s, axis=0, keepdims=True)

        return one_head

    first = stage_logits(0)
    tile_max0 = tuple(first(h) for h in range(A_HEADS))

    def attend(kj, carry):
        ms, tile_max = carry
        kvTt = _with_ones_row(kvT_ref[kj])
        stage_next = stage_logits(jnp.minimum(kj + 1, nk - 1))
        new_m, new_max = [], []
        for h in range(A_HEADS):
            m_new = jnp.maximum(ms[h], tile_max[h])
            p = jnp.exp2(s_ref[h] - m_new).astype(MM_DTYPE)
            alpha = jnp.exp2(ms[h] - m_new)
            new_max.append(stage_next(h))
            acc_ref[h] = alpha * acc_ref[h] + jnp.dot(kvTt, p, preferred_element_type=F32)
            new_m.append(m_new)
        return tuple(new_m), tuple(new_max)

    _loop_grouped(nk, attend, (_initial_max(A_HEADS, T), tile_max0), log2_group=1)

    for h in range(A_HEADS):
        o_lat = (acc_ref[h, 0:A_LATENT, :] / acc_ref[h, A_LATENT:A_LATENT + 1, :]).astype(MM_DTYPE)
        oT_ref[h * HEAD_DIM:(h + 1) * HEAD_DIM, :] = jnp.dot(
            wuvT_ref[h], o_lat, preferred_element_type=F32)
    out_ref[...] = oT_ref[...].T.astype(out_ref.dtype)


def _dsa(aqT, iqT, iwT, kidx, kv, kvT, bias_a, wukT, wuvT):
    bsz, _, seq = aqT.shape
    T = ATT_TILE
    nk = seq // T
    topk = min(TOPK_MAX, seq // 4)
    assert T == KEY_BITS * SUBLANES
    qspec = lambda rows: pl.BlockSpec((None, rows, T), lambda b, i: (b, 0, i))
    kspec = lambda a, c: pl.BlockSpec((None, nk, a, c), lambda b, i: (b, 0, 0, 0))
    return pl.pallas_call(
        functools.partial(_dsa_kernel, topk=topk),
        out_shape=jax.ShapeDtypeStruct((bsz, seq, A_HEADS * HEAD_DIM), MM_DTYPE),
        grid=(bsz, nk),
        in_specs=[qspec(A_HEADS * HEAD_DIM), qspec(IDX_HEADS * IDX_DIM), qspec(IDX_HEADS),
                  kspec(T, IDX_DIM), kspec(T, A_LATENT), kspec(A_LATENT, T),
                  _const_spec(bias_a.shape), _const_spec(wukT.shape), _const_spec(wuvT.shape)],
        out_specs=pl.BlockSpec((None, T, A_HEADS * HEAD_DIM), lambda b, i: (b, i, 0)),
        scratch_shapes=[pltpu.VMEM((nk, T, T), I32),
                        pltpu.VMEM((nk, KEY_BITS + 1, SUBLANES, T), I32),
                        pltpu.VMEM(((nk + 2) * SUBLANES, T), I32),
                        pltpu.VMEM((A_HEADS, A_LATENT, T), MM_DTYPE),
                        pltpu.VMEM((A_HEADS, A_LATENT + SUM_ROWS, T), F32),
                        pltpu.VMEM((A_HEADS, T, T), F32),
                        pltpu.VMEM((A_HEADS * HEAD_DIM, T), F32)],
        compiler_params=_cparams(2), name="dsa_attention",
    )(aqT, iqT, iwT, kidx, kv, kvT, bias_a, wukT, wuvT)


def _diff_kernel(qT_ref, k_ref, vT_ref, bias_ref, lam_ref, gsub_ref, out_ref,
                 qz_ref, acc_ref, s_ref, oT_ref, *, lam_init):
    T = ATT_TILE
    dv = 2 * HEAD_DIM
    n_chain = 2 * B_HEADS
    qi = pl.program_id(1)
    half = lax.broadcasted_iota(I32, (dv, T), 0) < HEAD_DIM
    for h in range(B_HEADS):
        q = qT_ref[h * dv:(h + 1) * dv, :].astype(F32)
        qz_ref[2 * h] = jnp.where(half, q, 0.0).astype(MM_DTYPE)
        qz_ref[2 * h + 1] = jnp.where(half, 0.0, q).astype(MM_DTYPE)
    acc_ref[...] = jnp.zeros(acc_ref.shape, F32)

    def stage_logits(kj, c):
        h = c // 2
        off = jnp.minimum(qi - kj, N_OFFSETS - 1)
        s = bias_ref[h, off] + jnp.dot(k_ref[kj, :, h * dv:(h + 1) * dv], qz_ref[c],
                                       preferred_element_type=F32)
        s_ref[c] = s
        return jnp.max(s, axis=0, keepdims=True)

    tile_max0 = tuple(stage_logits(0, c) for c in range(n_chain))

    def step(kj, carry, last):
        ms, tile_max = carry
        new_m, new_max = [], []
        for c in range(n_chain):
            h = c // 2
            m_new = jnp.maximum(ms[c], tile_max[c])
            p = jnp.exp2(s_ref[c] - m_new).astype(MM_DTYPE)
            alpha = jnp.exp2(ms[c] - m_new)
            if not last:
                new_max.append(stage_logits(kj + 1, c))
            vT = _with_ones_row(vT_ref[kj, h * dv:(h + 1) * dv, :])
            acc_ref[c] = alpha * acc_ref[c] + jnp.dot(vT, p, preferred_element_type=F32)
            new_m.append(m_new)
        return tuple(new_m), tuple(new_max)

    carry = _loop_grouped(qi, lambda kj, cr: step(kj, cr, False),
                          (_initial_max(n_chain, T), tile_max0), log2_group=2)
    step(qi, carry, True)

    lr = lam_ref[...]
    lam = (jnp.exp(jnp.sum(lr[0:1, :] * lr[1:2, :], axis=1, keepdims=True))
           - jnp.exp(jnp.sum(lr[2:3, :] * lr[3:4, :], axis=1, keepdims=True)) + lam_init)

    def normalised(c):
        return acc_ref[c, 0:dv, :] / acc_ref[c, dv:dv + 1, :]

    for h in range(B_HEADS):
        attn = normalised(2 * h) - lam * normalised(2 * h + 1)
        y = attn * lax.rsqrt(jnp.mean(attn * attn, axis=0, keepdims=True) + EPS)
        oT_ref[h * dv:(h + 1) * dv, :] = y * gsub_ref[...] * (1.0 - lam_init)
    out_ref[...] = oT_ref[...].T.astype(out_ref.dtype)


def _diff(bqT, bk, bvT, bias_b, lam_rows, gsub, lam_init):
    bsz, _, seq = bqT.shape
    T = ATT_TILE
    nk = seq // T
    dv = 2 * HEAD_DIM
    qspec = pl.BlockSpec((None, B_HEADS * dv, T), lambda b, i: (b, 0, i))
    return pl.pallas_call(
        functools.partial(_diff_kernel, lam_init=lam_init),
        out_shape=jax.ShapeDtypeStruct((bsz, seq, B_HEADS * dv), MM_DTYPE),
        grid=(bsz, nk),
        in_specs=[qspec,
                  pl.BlockSpec((None, nk, T, B_HEADS * dv), lambda b, i: (b, 0, 0, 0)),
                  pl.BlockSpec((None, nk, B_HEADS * dv, T), lambda b, i: (b, 0, 0, 0)),
                  _const_spec(bias_b.shape), _const_spec((4, HEAD_DIM)), _const_spec((dv, 1))],
        out_specs=pl.BlockSpec((None, T, B_HEADS * dv), lambda b, i: (b, i, 0)),
        scratch_shapes=[pltpu.VMEM((2 * B_HEADS, dv, T), MM_DTYPE),
                        pltpu.VMEM((2 * B_HEADS, dv + SUM_ROWS, T), F32),
                        pltpu.VMEM((2 * B_HEADS, T, T), F32),
                        pltpu.VMEM((B_HEADS * dv, T), F32)],
        compiler_params=_cparams(2), name="diff_attention",
    )(bqT, bk, bvT, bias_b, lam_rows, gsub)


def _dil_kernel(cur_ref, halo_ref, bias_ref, out_ref, lse_ref, *, tq):
    n = C_BAND
    wid = C_HPG * HEAD_DIM
    halo_lo = jnp.where(pl.program_id(2) == 0, jnp.int32(n), jnp.int32(0))
    i = lax.broadcasted_iota(I32, (n, 2 * n), 0)
    j = lax.broadcasted_iota(I32, (n, 2 * n), 1)
    lane_head = lax.broadcasted_iota(I32, (n, wid), 1) // HEAD_DIM
    in_head = [lane_head == h for h in range(C_HPG)]
    band = jnp.where(j >= i, jnp.where(j <= i + n, 0.0, NEG), NEG)
    band0 = jnp.where(j >= jnp.maximum(i, halo_lo), jnp.where(j <= i + n, 0.0, NEG), NEG)
    bias = [bias_ref[h] + band for h in range(C_HPG)]
    bias0 = [bias_ref[h] + band0 for h in range(C_HPG)]

    def band_rows(r, c, lo, hi):
        if c == 0:
            return jnp.concatenate([halo_ref[r, :, lo:hi], cur_ref[r, 0:n, lo:hi]], axis=0)
        return cur_ref[r, (c - 1) * n:(c + 1) * n, lo:hi]

    blocks = [(r, c) for r in range(cur_ref.shape[0]) for c in range(tq // n)]
    logits = []
    for r, c in blocks:
        q = cur_ref[r, c * n:(c + 1) * n, 0:wid].astype(F32) * (ATTN_SCALE * LOG2E)
        keys = band_rows(r, c, wid, 2 * wid)
        for h in range(C_HPG):
            qh = jnp.where(in_head[h], q, 0.0).astype(MM_DTYPE)
            s = lax.dot_general(qh, keys, (((1,), (1,)), ((), ())), preferred_element_type=F32)
            logits.append(s + (bias0 if c == 0 else bias)[h])
    for k, (r, c) in enumerate(blocks):
        vals = band_rows(r, c, 2 * wid, 3 * wid)
        out = jnp.zeros((n, wid), F32)
        lse = jnp.zeros((n, wid), F32)
        for h in range(C_HPG):
            s = logits[k * C_HPG + h]
            m = jnp.max(s, axis=1, keepdims=True)
            p = jnp.exp2(s - m)
            den = jnp.sum(p, axis=1, keepdims=True)
            o = jnp.dot(p.astype(MM_DTYPE), vals, preferred_element_type=F32) * (1.0 / den)
            out = jnp.where(in_head[h], o, out)
            lse = jnp.where(in_head[h], m + jnp.log2(den), lse)
        out_ref[r, c * n:(c + 1) * n, :] = out
        lse_ref[r, c * n:(c + 1) * n, :] = lse


def _dilated_group(cg, bias_g, g):
    bsz, dil, m, gw = cg.shape
    wid = C_HPG * HEAD_DIM
    n = C_BAND
    assert m % n == 0 and gw == 3 * wid
    tq = min(m, C_ROWS)
    n_res = max(1, min(dil, C_ROWS // tq))
    cur = pl.BlockSpec((None, n_res, tq, gw), lambda b, r, i: (b, r, i, 0))
    halo = pl.BlockSpec((None, n_res, n, gw),
                        lambda b, r, i: (b, r, jnp.maximum(i * (tq // n) - 1, 0), 0))
    outspec = pl.BlockSpec((None, n_res, tq, wid), lambda b, r, i: (b, r, i, 0))
    return pl.pallas_call(
        functools.partial(_dil_kernel, tq=tq),
        out_shape=[jax.ShapeDtypeStruct((bsz, dil, m, wid), F32)] * 2,
        grid=(bsz, dil // n_res, m // tq),
        in_specs=[cur, halo, pl.BlockSpec((C_HPG, n, 2 * n), lambda b, r, i: (0, 0, 0))],
        out_specs=[outspec, outspec],
        compiler_params=_cparams(3), name=f"dilated_group{g}",
    )(cg, cg, bias_g)


def _merge_ffn_kernel(x_ref, mod_ref, g1_ref, g2_ref, gf_ref, oa_ref, ob_ref,
                      c0_ref, c1_ref, c2_ref, s0_ref, s1_ref, s2_ref,
                      wz, wba, wbb, wbc, wo, wgu, wd, out_ref, tok_ref, *, final_norm):
    x = x_ref[...]
    tm = x.shape[0]
    h = _rms(x) * g1_ref[...]
    h = h * (1.0 + mod_ref[1:2, :]) + mod_ref[0:1, :]
    hb = h.astype(MM_DTYPE)

    def token_order(k, ref):
        dil, _, w = ref.shape
        if dil == 1:
            return ref[0]
        n_chunk = w // LANES
        for r in range(dil):
            for j in range(n_chunk):
                tok_ref[k * n_chunk + j, pl.ds(r, tm // dil, stride=dil), :] = ref[
                    r, :, j * LANES:(j + 1) * LANES]
        return jnp.concatenate([tok_ref[k * n_chunk + j] for j in range(n_chunk)], axis=1)

    s0, s1, s2 = s0_ref[0], token_order(0, s1_ref), token_order(1, s2_ref)
    c0, c1, c2 = c0_ref[0], token_order(2, c1_ref), token_order(3, c2_ref)
    mx = jnp.maximum(jnp.maximum(s0, s1), s2)
    e0, e1, e2 = jnp.exp2(s0 - mx), jnp.exp2(s1 - mx), jnp.exp2(s2 - mx)
    oc = (e0 * c0 + e1 * c1 + e2 * c2) / (e0 + e1 + e2)

    d = x.shape[1]

    def gated(k, o, wb):
        z = jnp.dot(hb, wz[:, k * d:(k + 1) * d], preferred_element_type=F32)
        return jax.nn.sigmoid(z) * jnp.dot(o, wb[...], preferred_element_type=F32)

    merged = (gated(0, oa_ref[...], wba) + gated(1, ob_ref[...], wbb)
              + gated(2, oc.astype(MM_DTYPE), wbc))
    y = jnp.dot(merged.astype(MM_DTYPE), wo[...], preferred_element_type=F32)
    x = x + mod_ref[2:3, :] * y

    h = _rms(x) * g2_ref[...]
    h = h * (1.0 + mod_ref[4:5, :]) + mod_ref[3:4, :]
    hb = h.astype(MM_DTYPE)
    acc = jnp.zeros(x.shape, F32)
    for c in range(D_FF // FFN_CHUNK):
        cols = slice(c * FFN_CHUNK, (c + 1) * FFN_CHUNK)
        fg = jnp.dot(hb, wgu[:, cols], preferred_element_type=F32)
        fu = jnp.dot(hb, wgu[:, D_FF + c * FFN_CHUNK:D_FF + (c + 1) * FFN_CHUNK],
                     preferred_element_type=F32)
        act = (fg * jax.nn.sigmoid(fg) * fu).astype(MM_DTYPE)
        acc = acc + jnp.dot(act, wd[cols, :], preferred_element_type=F32)
    y = x + mod_ref[5:6, :] * acc
    if final_norm:
        y = _rms(y) * gf_ref[...]
    out_ref[...] = y


def _merge_ffn(x2d, mod_l, g1, g2, gf, oa, ob, ocs, lses, ws, seq, final_norm):
    n, d = x2d.shape
    tm = ROW_TILE
    per_b = seq // tm
    row = lambda wd: pl.BlockSpec((tm, wd), lambda i: (i, 0))
    vec = _const_spec((1, d))
    wid = C_HPG * HEAD_DIM
    res = [pl.BlockSpec((None, dil, tm // dil, wid), lambda i: (i // per_b, 0, i % per_b, 0))
           for _, dil in C_GROUPS]
    in_specs = [row(d), pl.BlockSpec((None, 6, d), lambda i: (i // per_b, 0, 0)), vec, vec, vec,
                row(oa.shape[1]), row(ob.shape[1])] + res + res
    in_specs += [_const_spec(w.shape) for w in ws]
    return pl.pallas_call(
        functools.partial(_merge_ffn_kernel, final_norm=final_norm),
        out_shape=jax.ShapeDtypeStruct((n, d), F32), grid=(n // tm,),
        in_specs=in_specs, out_specs=row(d),
        scratch_shapes=[pltpu.VMEM((4 * wid // LANES, tm, LANES), F32)],
        compiler_params=_cparams(1), name="merge_ffn",
    )(x2d, mod_l, g1, g2, gf, oa, ob, *ocs, *lses, *ws)


def kernel(x, c, w_ada, b_ada, g_norm1, w_in, w_uk, w_uv, g_kv, lam_q1, lam_k1, lam_q2, lam_k2,
           g_subln, w_branch_a, w_branch_b, w_branch_c, w_out, g_norm2, w_gate_up, w_down,
           rel_bias, g_final):
    bsz, seq, d = x.shape
    depth = w_ada.shape[0]
    T = ATT_TILE
    nk = seq // T
    assert d == D_MODEL and seq % T == 0 and seq % ROW_TILE == 0 and seq % IN_ROW_TILE == 0
    n = bsz * seq
    cast = lambda w: w.astype(MM_DTYPE)

    thresholds = _bucket_thresholds(seq + 2 * C_BAND * C_GROUPS[-1][1])
    assert seq <= N_OFFSETS * T or (N_OFFSETS - 2) * T + 1 >= thresholds[-1]
    tab = rel_bias.reshape(-1)
    bias_a = _bias_att_tiles(tab, 0, A_HEADS, T, thresholds)
    bias_b = _bias_att_tiles(tab, A_HEADS, B_HEADS, T, thresholds)
    bias_c = _bias_dil_tiles(tab, A_HEADS + B_HEADS, thresholds)

    mod = _modulation(c, w_ada, b_ada).reshape(depth, bsz, 6, d)

    splits = (A_HEADS * HEAD_DIM, A_LATENT, IDX_HEADS * IDX_DIM, IDX_DIM, IDX_HEADS,
              B_HEADS * 2 * HEAD_DIM, B_HEADS * 2 * HEAD_DIM, B_HEADS * 2 * HEAD_DIM,
              C_HEADS * HEAD_DIM, C_HEADS * HEAD_DIM, C_HEADS * HEAD_DIM, d, d, d)
    offs = np.concatenate([[0], np.cumsum(splits)])
    seg = lambda w, k: w[:, int(offs[k]):int(offs[k + 1])]

    x2d = x.reshape(n, d)
    for l in range(depth):
        wl = w_in[l]
        wid = C_HPG * HEAD_DIM
        w_iw = jnp.pad(seg(wl, 4), ((0, 0), (0, 16 - IDX_HEADS)))
        wT_all = cast(jnp.concatenate([seg(wl, 0), seg(wl, 2), seg(wl, 5), seg(wl, 7), w_iw], axis=1).T)
        w_all = cast(jnp.concatenate(
            [seg(wl, 1), seg(wl, 6)]
            + [seg(wl, k)[:, g * wid:(g + 1) * wid] for g in range(len(C_GROUPS)) for k in (8, 9, 10)]
            + [seg(wl, 3)], axis=1))
        ws_in = [wT_all, w_all]
        g1 = g_norm1[l].reshape(1, d)
        (aqT, iqT, iwT, bqT, bvT, ik, kv, kvT, bk, cg0, cg1, cg2) = _in_proj(
            x2d, mod[l], g1, g_kv[l].reshape(1, A_LATENT), ws_in, bsz, seq)

        o_a = _dsa(aqT, iqT, iwT,
                   ik.reshape(bsz, nk, T, IDX_DIM), kv.reshape(bsz, nk, T, A_LATENT), kvT,
                   bias_a, cast(w_uk[l].transpose(0, 2, 1)), cast(w_uv[l].transpose(0, 2, 1)))
        o_a = o_a.reshape(n, -1)

        lam_init = 0.8 - 0.6 * math.exp(-0.3 * l)
        lam_rows = jnp.stack([lam_q1[l], lam_k1[l], lam_q2[l], lam_k2[l]])
        dv = 2 * HEAD_DIM
        o_b = _diff(bqT, bk.reshape(bsz, nk, T, B_HEADS * dv), bvT, bias_b, lam_rows,
                    g_subln[l].reshape(dv, 1), lam_init)
        o_b = o_b.reshape(n, -1)

        ocs, lses = [], []
        for g, (cg, (window, dil)) in enumerate(zip((cg0, cg1, cg2), C_GROUPS)):
            assert window // dil == C_BAND
            o, s = _dilated_group(cg, bias_c[g * C_HPG:(g + 1) * C_HPG], g)
            ocs.append(o)
            lses.append(s)

        ws_out = [cast(wl[:, int(offs[11]):int(offs[14])]),
                  cast(w_branch_a[l]), cast(w_branch_b[l]), cast(w_branch_c[l]), cast(w_out[l]),
                  cast(w_gate_up[l]), cast(w_down[l])]
        x2d = _merge_ffn(x2d, mod[l], g1, g_norm2[l].reshape(1, d), g_final.reshape(1, d),
                         o_a, o_b, ocs, lses, ws_out, seq, final_norm=(l == depth - 1))
    return x2d.reshape(bsz, seq, d)
```

```python
import functools
import math

import numpy as np
import jax
import jax.numpy as jnp
from jax import lax
from jax.experimental import pallas as pl
from jax.experimental.pallas import tpu as pltpu

D_MODEL = 1024
HEAD_DIM = 64
ATTN_SCALE = HEAD_DIM ** -0.5
LOG2E = math.log2(math.e)
A_HEADS = 8
A_LATENT = 128
IDX_HEADS = 8
IDX_DIM = 64
IDX_SCALE = (IDX_HEADS * IDX_DIM) ** -0.5
TOPK_MAX = 256
B_HEADS = 4
C_GROUPS = ((128, 1), (512, 4), (2048, 16))
C_HPG = 4
C_HEADS = C_HPG * len(C_GROUPS)
N_BUCKETS = 32
MAX_DISTANCE = 2048
N_BIAS_HEADS = A_HEADS + B_HEADS + C_HEADS
D_FF = -(-8 * D_MODEL // (3 * 256)) * 256
EPS = 1e-6

MM_DTYPE = jnp.bfloat16
F32 = jnp.float32
I32 = jnp.int32

ATT_TILE = 256
N_OFFSETS = 8
C_BAND = 128
C_ROWS = 2048
ROW_TILE = 512
IN_ROW_TILE = 1024
FFN_CHUNK = 256
MOD_COLS = 1536
NEG = -1e30
INT_MIN = -2 ** 31
LANES = 128
SUBLANES = 8
KEY_BITS = 32
SEARCH_TILES = 4
VMEM_LIMIT = 56 * 1024 * 1024


def _cparams(n_axes, vmem=VMEM_LIMIT):
    return pltpu.CompilerParams(dimension_semantics=("arbitrary",) * n_axes,
                                vmem_limit_bytes=vmem)


def _const_spec(shape):
    nd = len(shape)
    return pl.BlockSpec(shape, lambda *_: (0,) * nd, pipeline_mode=pl.Buffered(1))


def _bucket_thresholds(max_dist):
    n = np.arange(max_dist + 1)
    max_exact = N_BUCKETS // 2
    nf = np.maximum(n, 1).astype(np.float32)
    large = max_exact + (np.log(nf / np.float32(max_exact))
                         / np.float32(math.log(MAX_DISTANCE / max_exact))
                         * np.float32(N_BUCKETS - max_exact)).astype(np.int32)
    large = np.minimum(large, N_BUCKETS - 1)
    bucket = np.where(n < max_exact, n, large)
    assert np.all(np.diff(bucket) >= 0)
    thr = []
    for k in range(1, N_BUCKETS):
        idx = np.nonzero(bucket >= k)[0]
        thr.append(int(idx[0]) if idx.size else None)
    return thr


def _bias_from_dist(dist, tab_ref, col, thresholds, lo=0, hi=None):
    reached = [k for k, thr in enumerate(thresholds, start=1) if thr is not None]
    base = max([0] + [k for k in reached if thresholds[k - 1] <= lo])
    b = jnp.full(dist.shape, tab_ref[base * N_BIAS_HEADS + col], F32)
    for k in reached:
        thr = thresholds[k - 1]
        if thr > lo and (hi is None or thr <= hi):
            b = jnp.where(dist >= thr, tab_ref[k * N_BIAS_HEADS + col], b)
    return b


def _bias_att_kernel(tab_ref, out_ref, *, head0, thresholds):
    h = pl.program_id(0)
    n_off, tile, _ = out_ref.shape
    row = lax.broadcasted_iota(I32, (tile, tile), 0)
    colq = lax.broadcasted_iota(I32, (tile, tile), 1)
    for o in range(n_off):
        dist = jnp.maximum(o * tile + colq - row, 0)
        lo, hi = max(o * tile - (tile - 1), 0), o * tile + tile - 1
        b = _bias_from_dist(dist, tab_ref, head0 + h, thresholds, lo, hi) * LOG2E
        if o == 0:
            b = jnp.where(row <= colq, b, NEG)
        out_ref[o] = b


def _bias_att_tiles(tab, head0, n_heads, tile, thresholds):
    return pl.pallas_call(
        functools.partial(_bias_att_kernel, head0=head0, thresholds=thresholds),
        out_shape=jax.ShapeDtypeStruct((n_heads, N_OFFSETS, tile, tile), F32),
        grid=(n_heads,),
        in_specs=[pl.BlockSpec(memory_space=pltpu.SMEM)],
        out_specs=pl.BlockSpec((None, N_OFFSETS, tile, tile), lambda h: (h, 0, 0, 0)),
        compiler_params=_cparams(1),
        name="bias_att_tiles",
    )(tab)


def _bias_dil_kernel(tab_ref, out_ref, *, head0, thresholds):
    h = pl.program_id(0)
    g = h // C_HPG
    dil = jnp.where(g == 0, C_GROUPS[0][1], jnp.where(g == 1, C_GROUPS[1][1], C_GROUPS[2][1]))
    i = lax.broadcasted_iota(I32, (C_BAND, 2 * C_BAND), 0)
    j = lax.broadcasted_iota(I32, (C_BAND, 2 * C_BAND), 1)
    dist = jnp.maximum((i - j + C_BAND) * dil, 0)
    out_ref[...] = _bias_from_dist(dist, tab_ref, head0 + h, thresholds) * LOG2E


def _bias_dil_tiles(tab, head0, thresholds):
    return pl.pallas_call(
        functools.partial(_bias_dil_kernel, head0=head0, thresholds=thresholds),
        out_shape=jax.ShapeDtypeStruct((C_HEADS, C_BAND, 2 * C_BAND), F32),
        grid=(C_HEADS,),
        in_specs=[pl.BlockSpec(memory_space=pltpu.SMEM)],
        out_specs=pl.BlockSpec((None, C_BAND, 2 * C_BAND), lambda h: (h, 0, 0)),
        compiler_params=_cparams(1),
        name="bias_dil_tiles",
    )(tab)


def _mod_kernel(c_ref, w_ref, b_ref, out_ref):
    c = c_ref[...]
    ca = (c * jax.nn.sigmoid(c)).astype(MM_DTYPE)
    out_ref[...] = jnp.dot(ca, w_ref[...].astype(MM_DTYPE), preferred_element_type=F32) + b_ref[...]


def _modulation(c, w_ada, b_ada):
    depth, d, wid = w_ada.shape
    bsz = c.shape[0]
    tn = MOD_COLS
    assert wid % tn == 0
    return pl.pallas_call(
        _mod_kernel,
        out_shape=jax.ShapeDtypeStruct((depth, bsz, wid), F32),
        grid=(depth, wid // tn),
        in_specs=[pl.BlockSpec((bsz, d), lambda l, j: (0, 0)),
                  pl.BlockSpec((None, d, tn), lambda l, j: (l, 0, j)),
                  pl.BlockSpec((None, 1, tn), lambda l, j: (l, 0, j))],
        out_specs=pl.BlockSpec((None, bsz, tn), lambda l, j: (l, 0, j)),
        compiler_params=_cparams(2),
        name="adaln_modulation",
    )(c, w_ada, b_ada.reshape(depth, 1, wid))


def _rms(x):
    return x * lax.rsqrt(jnp.mean(x * x, axis=-1, keepdims=True) + EPS)


IN_ROWS_T = (("aq", A_HEADS * HEAD_DIM), ("iq", IDX_HEADS * IDX_DIM), ("bq", B_HEADS * 2 * HEAD_DIM),
             ("bv", B_HEADS * 2 * HEAD_DIM), ("iw", 16))
IN_COLS = (("kv", A_LATENT), ("bk", B_HEADS * 2 * HEAD_DIM), ("c", 3 * C_HEADS * HEAD_DIM),
           ("ik", IDX_DIM))


def _segments(layout):
    out, start = {}, 0
    for name, width in layout:
        out[name] = slice(start, start + width)
        start += width
    return out


def _in_kernel(x_ref, mod_ref, g1_ref, gkv_ref, wT_ref, w_ref,
               o_aqT, o_iqT, o_iwT, o_bqT, o_bvT, o_ik, o_kv, o_kvT, o_bk, o_c0, o_c1, o_c2,
               c_scr):
    T = ATT_TILE
    tm = x_ref.shape[0]
    h = _rms(x_ref[...]) * g1_ref[...]
    h = h * (1.0 + mod_ref[1:2, :]) + mod_ref[0:1, :]
    hb = h.astype(MM_DTYPE)
    rows_t, cols = _segments(IN_ROWS_T), _segments(IN_COLS)

    def mm(name):
        return jnp.dot(hb, w_ref[:, cols[name]], preferred_element_type=F32)

    def mm_t(name):
        return lax.dot_general(wT_ref[rows_t[name], :], hb, (((1,), (1,)), ((), ())),
                               preferred_element_type=F32)


    o_aqT[...] = mm_t("aq").astype(o_aqT.dtype)
    o_iqT[...] = mm_t("iq").astype(o_iqT.dtype)
    o_bqT[...] = (mm_t("bq") * (ATTN_SCALE * LOG2E)).astype(o_bqT.dtype)
    o_iwT[...] = (mm_t("iw") * IDX_SCALE)[:IDX_HEADS]
    bvT = mm_t("bv").astype(o_bvT.dtype)
    kv = _rms(mm("kv")) * gkv_ref[...]
    kvT = kv.T.astype(o_kvT.dtype)
    for j in range(tm // T):
        o_bvT[j] = bvT[:, j * T:(j + 1) * T]
        o_kvT[j] = kvT[:, j * T:(j + 1) * T]
    o_kv[...] = kv.astype(o_kv.dtype)
    o_ik[...] = mm("ik").astype(o_ik.dtype)
    o_bk[...] = mm("bk").astype(o_bk.dtype)

    yc = mm("c")
    n_chunk = yc.shape[1] // LANES
    for j in range(n_chunk):
        c_scr[j] = yc[:, j * LANES:(j + 1) * LANES]
    per_group = n_chunk // len(C_GROUPS)
    for g, o_c in enumerate((o_c0, o_c1, o_c2)):
        dil = C_GROUPS[g][1]
        for r in range(dil):
            for jj in range(per_group):
                o_c[r, :, jj * LANES:(jj + 1) * LANES] = c_scr[
                    g * per_group + jj, pl.ds(r, tm // dil, stride=dil), :].astype(o_c.dtype)


def _in_proj(x2d, mod_l, g1, gkv, ws, bsz, seq):
    n, d = x2d.shape
    tm = IN_ROW_TILE
    T = ATT_TILE
    per_b = seq // tm
    nk = seq // T
    hd = A_HEADS * HEAD_DIM
    bw = B_HEADS * 2 * HEAD_DIM
    gw = 3 * C_HPG * HEAD_DIM
    in_specs = [pl.BlockSpec((tm, d), lambda i: (i, 0)),
                pl.BlockSpec((None, 6, d), lambda i: (i // per_b, 0, 0)),
                _const_spec((1, d)), _const_spec((1, A_LATENT))]
    in_specs += [_const_spec(w.shape) for w in ws]

    def tspec(rows):
        return pl.BlockSpec((None, rows, tm), lambda i: (i // per_b, 0, i % per_b))

    def tile_tspec(rows):
        return pl.BlockSpec((None, tm // T, rows, T), lambda i: (i // per_b, i % per_b, 0, 0))

    def rspec(wd):
        return pl.BlockSpec((tm, wd), lambda i: (i, 0))

    def cspec(dil):
        return pl.BlockSpec((None, dil, tm // dil, gw), lambda i: (i // per_b, 0, i % per_b, 0))

    sds = jax.ShapeDtypeStruct
    out_specs = [tspec(hd), tspec(IDX_HEADS * IDX_DIM), tspec(IDX_HEADS), tspec(bw),
                 tile_tspec(bw), rspec(IDX_DIM), rspec(A_LATENT), tile_tspec(A_LATENT), rspec(bw)]
    out_shape = [sds((bsz, hd, seq), MM_DTYPE), sds((bsz, IDX_HEADS * IDX_DIM, seq), MM_DTYPE),
                 sds((bsz, IDX_HEADS, seq), F32), sds((bsz, bw, seq), MM_DTYPE),
                 sds((bsz, nk, bw, T), MM_DTYPE), sds((n, IDX_DIM), MM_DTYPE),
                 sds((n, A_LATENT), MM_DTYPE), sds((bsz, nk, A_LATENT, T), MM_DTYPE),
                 sds((n, bw), MM_DTYPE)]
    for _, dil in C_GROUPS:
        out_specs.append(cspec(dil))
        out_shape.append(sds((bsz, dil, seq // dil, gw), MM_DTYPE))
    return pl.pallas_call(
        _in_kernel, out_shape=out_shape, grid=(n // tm,),
        in_specs=in_specs, out_specs=out_specs,
        scratch_shapes=[pltpu.VMEM((3 * gw // LANES, tm, LANES), F32)],
        compiler_params=_cparams(1), name="in_proj",
    )(x2d, mod_l, g1, gkv, *ws)


def _initial_max(n_chains, tile):
    return tuple(jnp.full((1, tile), NEG, F32) for _ in range(n_chains))


def _loop_grouped(n, body, init, log2_group):
    carry, done = init, jnp.int32(0)
    for lg in range(log2_group, -1, -1):
        group = 1 << lg
        trips = lax.shift_right_logical(n - done, jnp.int32(lg))

        def grouped(i, c, group=group, done=done):
            for j in range(group):
                c = body(done + group * i + j, c)
            return c

        carry = lax.fori_loop(0, trips, grouped, carry)
        done = done + group * trips
    return carry


SUM_ROWS = 16


def _with_ones_row(vT):
    row = lax.broadcasted_iota(I32, (SUM_ROWS, vT.shape[1]), 0)
    ones = jnp.where(row == 0, 1.0, 0.0).astype(vT.dtype)
    return jnp.concatenate([vT, ones], axis=0)


def _bit_transpose32(words):
    a = list(words)
    j, mask = 16, 0x0000FFFF
    while j:
        k = 0
        while k < 32:
            t = (a[k] ^ lax.shift_right_logical(a[k + j], jnp.int32(j))) & jnp.int32(mask)
            a[k] = a[k] ^ t
            a[k + j] = a[k + j] ^ lax.shift_left(t, jnp.int32(j))
            k = (k + j + 1) & ~j
        j >>= 1
        mask = (mask ^ (mask << j)) & 0xFFFFFFFF
    return a


def _dsa_kernel(aqT_ref, iqT_ref, iwT_ref, kidx_ref, kv_ref, kvT_ref, bias_ref, wukT_ref, wuvT_ref,
                out_ref, keys_ref, planes_ref, found_ref, qlat_ref, acc_ref, s_ref, oT_ref, *, topk):
    T = ATT_TILE
    qi = pl.program_id(1)
    nk = qi + 1

    for h in range(A_HEADS):
        q = jnp.dot(wukT_ref[h], aqT_ref[h * HEAD_DIM:(h + 1) * HEAD_DIM, :],
                    preferred_element_type=F32) * (ATTN_SCALE * LOG2E)
        qlat_ref[h] = q.astype(qlat_ref.dtype)

    s_loc = lax.broadcasted_iota(I32, (T, T), 0)
    t_loc = lax.broadcasted_iota(I32, (T, T), 1)

    def score_tile(kj, diagonal):
        kt = kidx_ref[kj]
        acc = jnp.zeros((T, T), F32)
        for h in range(IDX_HEADS):
            s = jnp.dot(kt, iqT_ref[h * IDX_DIM:(h + 1) * IDX_DIM, :], preferred_element_type=F32)
            acc = acc + jnp.maximum(s, 0.0) * iwT_ref[h:h + 1, :]
        bits = lax.bitcast_convert_type(acc, I32)
        key = jnp.where(bits < 0, bits ^ jnp.int32(0x7FFFFFFF), bits)
        if diagonal:
            key = jnp.where(s_loc <= t_loc, key, jnp.int32(INT_MIN))
        keys_ref[kj] = key
        planes = _bit_transpose32([key[SUBLANES * i:SUBLANES * (i + 1), :] ^ jnp.int32(INT_MIN)
                                   for i in range(KEY_BITS)])
        for b in range(KEY_BITS):
            planes_ref[kj, b] = planes[b]

    def off_diagonal(kj, carry):
        score_tile(kj, False)
        return carry

    _loop_grouped(qi, off_diagonal, 0, log2_group=2)
    score_tile(qi, True)

    n_tiles = planes_ref.shape[0]

    @pl.when(qi == 0)
    def _():
        def clear_planes(kj, carry):
            for b in range(KEY_BITS):
                planes_ref[kj, b] = jnp.zeros((SUBLANES, T), I32)
            return carry

        lax.fori_loop(1, n_tiles, clear_planes, 0)

    one = jnp.int32(1)
    nil = jnp.int32(0)
    zero = jnp.zeros((1, T), I32)

    n_rows = n_tiles * SUBLANES

    def search(tiles):
        rows = tiles * SUBLANES
        tile_of_row = lax.broadcasted_iota(I32, (rows, T), 0) // SUBLANES
        tied0 = jnp.where(tile_of_row < nk, jnp.int32(-1), nil)

        def bit_step(b, carry):
            tied, n_gt, kth_u = carry
            ones = tied & planes_ref[0:tiles, b].reshape(rows, T)
            n1 = jnp.sum(lax.population_count(ones), axis=0, keepdims=True)
            take = (n_gt + n1) >= topk
            tied = jnp.where(take, ones, tied ^ ones)
            n_gt = jnp.where(take, n_gt, n_gt + n1)
            kth_u = jnp.where(take, kth_u | lax.shift_left(one, jnp.int32(KEY_BITS - 1) - b), kth_u)
            return tied, n_gt, kth_u

        tied, n_gt, kth_u = lax.fori_loop(0, KEY_BITS, bit_step, (tied0, zero, zero))
        found_ref[0:rows, :] = tied
        if rows < n_rows:
            found_ref[rows:n_rows, :] = jnp.zeros((n_rows - rows, T), I32)
        found_ref[n_rows:n_rows + SUBLANES, :] = jnp.broadcast_to(n_gt, (SUBLANES, T))
        found_ref[n_rows + SUBLANES:n_rows + 2 * SUBLANES, :] = jnp.broadcast_to(kth_u, (SUBLANES, T))

    covered = 0
    for tiles in list(range(SEARCH_TILES, n_tiles, SEARCH_TILES)) + [n_tiles]:
        pl.when(jnp.logical_and(nk > covered, nk <= tiles))(functools.partial(search, tiles))
        covered = tiles

    tied = found_ref[0:n_rows, :]
    n_gt = found_ref[n_rows:n_rows + 1, :]
    kth_u = found_ref[n_rows + SUBLANES:n_rows + SUBLANES + 1, :]
    n_eq = jnp.sum(lax.population_count(tied), axis=0, keepdims=True)
    n_ge = jnp.where(kth_u == nil, n_gt, n_gt + n_eq)
    kth = jnp.maximum(kth_u ^ jnp.int32(INT_MIN), jnp.int32(INT_MIN + 1))

    @pl.when(jnp.max(n_ge) > topk)
    def _():
        need = topk - n_gt
        r_bits = SUBLANES.bit_length() - 1

        before, cut_tile, rank = zero, zero, need
        words = jnp.zeros((SUBLANES, T), I32)
        for kj in range(n_tiles):
            tile_words = tied[kj * SUBLANES:(kj + 1) * SUBLANES, :]
            after = before + jnp.sum(lax.population_count(tile_words), axis=0, keepdims=True)
            here = jnp.where(before < need, jnp.where(after >= need, one, nil), nil) == one
            cut_tile = jnp.where(here, jnp.int32(kj), cut_tile)
            rank = jnp.where(here, need - before, rank)
            words = jnp.where(here, tile_words, words)
            before = after

        word_r = lax.broadcasted_iota(I32, (SUBLANES, T), 0)
        local = zero
        for bit in reversed(range(T.bit_length() - 1)):
            cand = local | jnp.int32(1 << bit)
            p_i = lax.shift_right_logical(cand, jnp.int32(r_bits))
            p_r = cand & jnp.int32(SUBLANES - 1)
            above = jnp.where(p_i == nil, nil,
                              lax.shift_left(jnp.full_like(p_i, -1), jnp.int32(KEY_BITS) - p_i))
            at_i = lax.shift_right_logical(
                words, jnp.broadcast_to(jnp.int32(KEY_BITS - 1) - p_i, words.shape)) & one
            below = lax.population_count(words & above) + jnp.where(word_r < p_r, at_i, nil)
            local = jnp.where(jnp.sum(below, axis=0, keepdims=True) < rank, cand, local)
        cut = cut_tile * T + local

        def demote(kj, carry):
            kk = keys_ref[kj]
            lowered = jnp.where((kj * T + s_loc) > cut, kth - one, kk)
            keys_ref[kj] = jnp.where(kk == kth, lowered, kk)
            return carry

        lax.fori_loop(0, nk, demote, 0)

    acc_ref[...] = jnp.zeros(acc_ref.shape, F32)

    def stage_logits(kj):
        unselected = jnp.where(keys_ref[kj] >= kth, 0.0, NEG)
        off = jnp.minimum(qi - kj, N_OFFSETS - 1)
        kvt = kv_ref[kj]

        def one_head(h):
            s = bias_ref[h, off] + jnp.dot(kvt, qlat_ref[h], preferred_element_type=F32)
            s = s + unselected
            s_ref[h] = s
            return jnp.max(s, axis=0, keepdims=True)

        return one_head

    first = stage_logits(0)
    tile_max0 = tuple(first(h) for h in range(A_HEADS))

    def attend(kj, carry):
        ms, tile_max = carry
        kvTt = _with_ones_row(kvT_ref[kj])
        stage_next = stage_logits(jnp.minimum(kj + 1, nk - 1))
        new_m, new_max = [], []
        for h in range(A_HEADS):
            m_new = jnp.maximum(ms[h], tile_max[h])
            p = jnp.exp2(s_ref[h] - m_new).astype(MM_DTYPE)
            alpha = jnp.exp2(ms[h] - m_new)
            new_max.append(stage_next(h))
            acc_ref[h] = alpha * acc_ref[h] + jnp.dot(kvTt, p, preferred_element_type=F32)
            new_m.append(m_new)
        return tuple(new_m), tuple(new_max)

    _loop_grouped(nk, attend, (_initial_max(A_HEADS, T), tile_max0), log2_group=1)

    for h in range(A_HEADS):
        o_lat = (acc_ref[h, 0:A_LATENT, :] / acc_ref[h, A_LATENT:A_LATENT + 1, :]).astype(MM_DTYPE)
        oT_ref[h * HEAD_DIM:(h + 1) * HEAD_DIM, :] = jnp.dot(
            wuvT_ref[h], o_lat, preferred_element_type=F32)
    out_ref[...] = oT_ref[...].T.astype(out_ref.dtype)


def _dsa(aqT, iqT, iwT, kidx, kv, kvT, bias_a, wukT, wuvT):
    bsz, _, seq = aqT.shape
    T = ATT_TILE
    nk = seq // T
    topk = min(TOPK_MAX, seq // 4)
    assert T == KEY_BITS * SUBLANES
    qspec = lambda rows: pl.BlockSpec((None, rows, T), lambda b, i: (b, 0, i))
    kspec = lambda a, c: pl.BlockSpec((None, nk, a, c), lambda b, i: (b, 0, 0, 0))
    return pl.pallas_call(
        functools.partial(_dsa_kernel, topk=topk),
        out_shape=jax.ShapeDtypeStruct((bsz, seq, A_HEADS * HEAD_DIM), MM_DTYPE),
        grid=(bsz, nk),
        in_specs=[qspec(A_HEADS * HEAD_DIM), qspec(IDX_HEADS * IDX_DIM), qspec(IDX_HEADS),
                  kspec(T, IDX_DIM), kspec(T, A_LATENT), kspec(A_LATENT, T),
                  _const_spec(bias_a.shape), _const_spec(wukT.shape), _const_spec(wuvT.shape)],
        out_specs=pl.BlockSpec((None, T, A_HEADS * HEAD_DIM), lambda b, i: (b, i, 0)),
        scratch_shapes=[pltpu.VMEM((nk, T, T), I32),
                        pltpu.VMEM((nk, KEY_BITS + 1, SUBLANES, T), I32),
                        pltpu.VMEM(((nk + 2) * SUBLANES, T), I32),
                        pltpu.VMEM((A_HEADS, A_LATENT, T), MM_DTYPE),
                        pltpu.VMEM((A_HEADS, A_LATENT + SUM_ROWS, T), F32),
                        pltpu.VMEM((A_HEADS, T, T), F32),
                        pltpu.VMEM((A_HEADS * HEAD_DIM, T), F32)],
        compiler_params=_cparams(2), name="dsa_attention",
    )(aqT, iqT, iwT, kidx, kv, kvT, bias_a, wukT, wuvT)


def _diff_kernel(qT_ref, k_ref, vT_ref, bias_ref, lam_ref, gsub_ref, out_ref,
                 qz_ref, acc_ref, s_ref, oT_ref, *, lam_init):
    T = ATT_TILE
    dv = 2 * HEAD_DIM
    n_chain = 2 * B_HEADS
    qi = pl.program_id(1)
    half = lax.broadcasted_iota(I32, (dv, T), 0) < HEAD_DIM
    for h in range(B_HEADS):
        q = qT_ref[h * dv:(h + 1) * dv, :].astype(F32)
        qz_ref[2 * h] = jnp.where(half, q, 0.0).astype(MM_DTYPE)
        qz_ref[2 * h + 1] = jnp.where(half, 0.0, q).astype(MM_DTYPE)
    acc_ref[...] = jnp.zeros(acc_ref.shape, F32)

    def stage_logits(kj, c):
        h = c // 2
        off = jnp.minimum(qi - kj, N_OFFSETS - 1)
        s = bias_ref[h, off] + jnp.dot(k_ref[kj, :, h * dv:(h + 1) * dv], qz_ref[c],
                                       preferred_element_type=F32)
        s_ref[c] = s
        return jnp.max(s, axis=0, keepdims=True)

    tile_max0 = tuple(stage_logits(0, c) for c in range(n_chain))

    def step(kj, carry, last):
        ms, tile_max = carry
        new_m, new_max = [], []
        for c in range(n_chain):
            h = c // 2
            m_new = jnp.maximum(ms[c], tile_max[c])
            p = jnp.exp2(s_ref[c] - m_new).astype(MM_DTYPE)
            alpha = jnp.exp2(ms[c] - m_new)
            if not last:
                new_max.append(stage_logits(kj + 1, c))
            vT = _with_ones_row(vT_ref[kj, h * dv:(h + 1) * dv, :])
            acc_ref[c] = alpha * acc_ref[c] + jnp.dot(vT, p, preferred_element_type=F32)
            new_m.append(m_new)
        return tuple(new_m), tuple(new_max)

    carry = _loop_grouped(qi, lambda kj, cr: step(kj, cr, False),
                          (_initial_max(n_chain, T), tile_max0), log2_group=2)
    step(qi, carry, True)

    lr = lam_ref[...]
    lam = (jnp.exp(jnp.sum(lr[0:1, :] * lr[1:2, :], axis=1, keepdims=True))
           - jnp.exp(jnp.sum(lr[2:3, :] * lr[3:4, :], axis=1, keepdims=True)) + lam_init)

    def normalised(c):
        return acc_ref[c, 0:dv, :] / acc_ref[c, dv:dv + 1, :]

    for h in range(B_HEADS):
        attn = normalised(2 * h) - lam * normalised(2 * h + 1)
        y = attn * lax.rsqrt(jnp.mean(attn * attn, axis=0, keepdims=True) + EPS)
        oT_ref[h * dv:(h + 1) * dv, :] = y * gsub_ref[...] * (1.0 - lam_init)
    out_ref[...] = oT_ref[...].T.astype(out_ref.dtype)


def _diff(bqT, bk, bvT, bias_b, lam_rows, gsub, lam_init):
    bsz, _, seq = bqT.shape
    T = ATT_TILE
    nk = seq // T
    dv = 2 * HEAD_DIM
    qspec = pl.BlockSpec((None, B_HEADS * dv, T), lambda b, i: (b, 0, i))
    return pl.pallas_call(
        functools.partial(_diff_kernel, lam_init=lam_init),
        out_shape=jax.ShapeDtypeStruct((bsz, seq, B_HEADS * dv), MM_DTYPE),
        grid=(bsz, nk),
        in_specs=[qspec,
                  pl.BlockSpec((None, nk, T, B_HEADS * dv), lambda b, i: (b, 0, 0, 0)),
                  pl.BlockSpec((None, nk, B_HEADS * dv, T), lambda b, i: (b, 0, 0, 0)),
                  _const_spec(bias_b.shape), _const_spec((4, HEAD_DIM)), _const_spec((dv, 1))],
        out_specs=pl.BlockSpec((None, T, B_HEADS * dv), lambda b, i: (b, i, 0)),
        scratch_shapes=[pltpu.VMEM((2 * B_HEADS, dv, T), MM_DTYPE),
                        pltpu.VMEM((2 * B_HEADS, dv + SUM_ROWS, T), F32),
                        pltpu.VMEM((2 * B_HEADS, T, T), F32),
                        pltpu.VMEM((B_HEADS * dv, T), F32)],
        compiler_params=_cparams(2), name="diff_attention",
    )(bqT, bk, bvT, bias_b, lam_rows, gsub)


def _dil_kernel(cur_ref, halo_ref, bias_ref, out_ref, lse_ref, *, tq):
    n = C_BAND
    wid = C_HPG * HEAD_DIM
    halo_lo = jnp.where(pl.program_id(2) == 0, jnp.int32(n), jnp.int32(0))
    i = lax.broadcasted_iota(I32, (n, 2 * n), 0)
    j = lax.broadcasted_iota(I32, (n, 2 * n), 1)
    lane_head = lax.broadcasted_iota(I32, (n, wid), 1) // HEAD_DIM
    in_head = [lane_head == h for h in range(C_HPG)]
    band = jnp.where(j >= i, jnp.where(j <= i + n, 0.0, NEG), NEG)
    band0 = jnp.where(j >= jnp.maximum(i, halo_lo), jnp.where(j <= i + n, 0.0, NEG), NEG)
    bias = [bias_ref[h] + band for h in range(C_HPG)]
    bias0 = [bias_ref[h] + band0 for h in range(C_HPG)]

    def band_rows(r, c, lo, hi):
        if c == 0:
            return jnp.concatenate([halo_ref[r, :, lo:hi], cur_ref[r, 0:n, lo:hi]], axis=0)
        return cur_ref[r, (c - 1) * n:(c + 1) * n, lo:hi]

    blocks = [(r, c) for r in range(cur_ref.shape[0]) for c in range(tq // n)]
    logits = []
    for r, c in blocks:
        q = cur_ref[r, c * n:(c + 1) * n, 0:wid].astype(F32) * (ATTN_SCALE * LOG2E)
        keys = band_rows(r, c, wid, 2 * wid)
        for h in range(C_HPG):
            qh = jnp.where(in_head[h], q, 0.0).astype(MM_DTYPE)
            s = lax.dot_general(qh, keys, (((1,), (1,)), ((), ())), preferred_element_type=F32)
            logits.append(s + (bias0 if c == 0 else bias)[h])
    for k, (r, c) in enumerate(blocks):
        vals = band_rows(r, c, 2 * wid, 3 * wid)
        out = jnp.zeros((n, wid), F32)
        lse = jnp.zeros((n, wid), F32)
        for h in range(C_HPG):
            s = logits[k * C_HPG + h]
            m = jnp.max(s, axis=1, keepdims=True)
            p = jnp.exp2(s - m)
            den = jnp.sum(p, axis=1, keepdims=True)
            o = jnp.dot(p.astype(MM_DTYPE), vals, preferred_element_type=F32) * (1.0 / den)
            out = jnp.where(in_head[h], o, out)
            lse = jnp.where(in_head[h], m + jnp.log2(den), lse)
        out_ref[r, c * n:(c + 1) * n, :] = out
        lse_ref[r, c * n:(c + 1) * n, :] = lse


def _dilated_group(cg, bias_g, g):
    bsz, dil, m, gw = cg.shape
    wid = C_HPG * HEAD_DIM
    n = C_BAND
    assert m % n == 0 and gw == 3 * wid
    tq = min(m, C_ROWS)
    n_res = max(1, min(dil, C_ROWS // tq))
    cur = pl.BlockSpec((None, n_res, tq, gw), lambda b, r, i: (b, r, i, 0))
    halo = pl.BlockSpec((None, n_res, n, gw),
                        lambda b, r, i: (b, r, jnp.maximum(i * (tq // n) - 1, 0), 0))
    outspec = pl.BlockSpec((None, n_res, tq, wid), lambda b, r, i: (b, r, i, 0))
    return pl.pallas_call(
        functools.partial(_dil_kernel, tq=tq),
        out_shape=[jax.ShapeDtypeStruct((bsz, dil, m, wid), F32)] * 2,
        grid=(bsz, dil // n_res, m // tq),
        in_specs=[cur, halo, pl.BlockSpec((C_HPG, n, 2 * n), lambda b, r, i: (0, 0, 0))],
        out_specs=[outspec, outspec],
        compiler_params=_cparams(3), name=f"dilated_group{g}",
    )(cg, cg, bias_g)


def _merge_ffn_kernel(x_ref, mod_ref, g1_ref, g2_ref, gf_ref, oa_ref, ob_ref,
                      c0_ref, c1_ref, c2_ref, s0_ref, s1_ref, s2_ref,
                      wz, wba, wbb, wbc, wo, wgu, wd, out_ref, tok_ref, *, final_norm):
    x = x_ref[...]
    tm = x.shape[0]
    h = _rms(x) * g1_ref[...]
    h = h * (1.0 + mod_ref[1:2, :]) + mod_ref[0:1, :]
    hb = h.astype(MM_DTYPE)

    def token_order(k, ref):
        dil, _, w = ref.shape
        if dil == 1:
            return ref[0]
        n_chunk = w // LANES
        for r in range(dil):
            for j in range(n_chunk):
                tok_ref[k * n_chunk + j, pl.ds(r, tm // dil, stride=dil), :] = ref[
                    r, :, j * LANES:(j + 1) * LANES]
        return jnp.concatenate([tok_ref[k * n_chunk + j] for j in range(n_chunk)], axis=1)

    s0, s1, s2 = s0_ref[0], token_order(0, s1_ref), token_order(1, s2_ref)
    c0, c1, c2 = c0_ref[0], token_order(2, c1_ref), token_order(3, c2_ref)
    mx = jnp.maximum(jnp.maximum(s0, s1), s2)
    e0, e1, e2 = jnp.exp2(s0 - mx), jnp.exp2(s1 - mx), jnp.exp2(s2 - mx)
    oc = (e0 * c0 + e1 * c1 + e2 * c2) / (e0 + e1 + e2)

    d = x.shape[1]

    def gated(k, o, wb):
        z = jnp.dot(hb, wz[:, k * d:(k + 1) * d], preferred_element_type=F32)
        return jax.nn.sigmoid(z) * jnp.dot(o, wb[...], preferred_element_type=F32)

    merged = (gated(0, oa_ref[...], wba) + gated(1, ob_ref[...], wbb)
              + gated(2, oc.astype(MM_DTYPE), wbc))
    y = jnp.dot(merged.astype(MM_DTYPE), wo[...], preferred_element_type=F32)
    x = x + mod_ref[2:3, :] * y

    h = _rms(x) * g2_ref[...]
    h = h * (1.0 + mod_ref[4:5, :]) + mod_ref[3:4, :]
    hb = h.astype(MM_DTYPE)
    acc = jnp.zeros(x.shape, F32)
    for c in range(D_FF // FFN_CHUNK):
        cols = slice(c * FFN_CHUNK, (c + 1) * FFN_CHUNK)
        fg = jnp.dot(hb, wgu[:, cols], preferred_element_type=F32)
        fu = jnp.dot(hb, wgu[:, D_FF + c * FFN_CHUNK:D_FF + (c + 1) * FFN_CHUNK],
                     preferred_element_type=F32)
        act = (fg * jax.nn.sigmoid(fg) * fu).astype(MM_DTYPE)
        acc = acc + jnp.dot(act, wd[cols, :], preferred_element_type=F32)
    y = x + mod_ref[5:6, :] * acc
    if final_norm:
        y = _rms(y) * gf_ref[...]
    out_ref[...] = y


def _merge_ffn(x2d, mod_l, g1, g2, gf, oa, ob, ocs, lses, ws, seq, final_norm):
    n, d = x2d.shape
    tm = ROW_TILE
    per_b = seq // tm
    row = lambda wd: pl.BlockSpec((tm, wd), lambda i: (i, 0))
    vec = _const_spec((1, d))
    wid = C_HPG * HEAD_DIM
    res = [pl.BlockSpec((None, dil, tm // dil, wid), lambda i: (i // per_b, 0, i % per_b, 0))
           for _, dil in C_GROUPS]
    in_specs = [row(d), pl.BlockSpec((None, 6, d), lambda i: (i // per_b, 0, 0)), vec, vec, vec,
                row(oa.shape[1]), row(ob.shape[1])] + res + res
    in_specs += [_const_spec(w.shape) for w in ws]
    return pl.pallas_call(
        functools.partial(_merge_ffn_kernel, final_norm=final_norm),
        out_shape=jax.ShapeDtypeStruct((n, d), F32), grid=(n // tm,),
        in_specs=in_specs, out_specs=row(d),
        scratch_shapes=[pltpu.VMEM((4 * wid // LANES, tm, LANES), F32)],
        compiler_params=_cparams(1), name="merge_ffn",
    )(x2d, mod_l, g1, g2, gf, oa, ob, *ocs, *lses, *ws)


def kernel(x, c, w_ada, b_ada, g_norm1, w_in, w_uk, w_uv, g_kv, lam_q1, lam_k1, lam_q2, lam_k2,
           g_subln, w_branch_a, w_branch_b, w_branch_c, w_out, g_norm2, w_gate_up, w_down,
           rel_bias, g_final):
    bsz, seq, d = x.shape
    depth = w_ada.shape[0]
    T = ATT_TILE
    nk = seq // T
    assert d == D_MODEL and seq % T == 0 and seq % ROW_TILE == 0 and seq % IN_ROW_TILE == 0
    n = bsz * seq
    cast = lambda w: w.astype(MM_DTYPE)

    thresholds = _bucket_thresholds(seq + 2 * C_BAND * C_GROUPS[-1][1])
    assert seq <= N_OFFSETS * T or (N_OFFSETS - 2) * T + 1 >= thresholds[-1]
    tab = rel_bias.reshape(-1)
    bias_a = _bias_att_tiles(tab, 0, A_HEADS, T, thresholds)
    bias_b = _bias_att_tiles(tab, A_HEADS, B_HEADS, T, thresholds)
    bias_c = _bias_dil_tiles(tab, A_HEADS + B_HEADS, thresholds)

    mod = _modulation(c, w_ada, b_ada).reshape(depth, bsz, 6, d)

    splits = (A_HEADS * HEAD_DIM, A_LATENT, IDX_HEADS * IDX_DIM, IDX_DIM, IDX_HEADS,
              B_HEADS * 2 * HEAD_DIM, B_HEADS * 2 * HEAD_DIM, B_HEADS * 2 * HEAD_DIM,
              C_HEADS * HEAD_DIM, C_HEADS * HEAD_DIM, C_HEADS * HEAD_DIM, d, d, d)
    offs = np.concatenate([[0], np.cumsum(splits)])
    seg = lambda w, k: w[:, int(offs[k]):int(offs[k + 1])]

    x2d = x.reshape(n, d)
    for l in range(depth):
        wl = w_in[l]
        wid = C_HPG * HEAD_DIM
        w_iw = jnp.pad(seg(wl, 4), ((0, 0), (0, 16 - IDX_HEADS)))
        wT_all = cast(jnp.concatenate([seg(wl, 0), seg(wl, 2), seg(wl, 5), seg(wl, 7), w_iw], axis=1).T)
        w_all = cast(jnp.concatenate(
            [seg(wl, 1), seg(wl, 6)]
            + [seg(wl, k)[:, g * wid:(g + 1) * wid] for g in range(len(C_GROUPS)) for k in (8, 9, 10)]
            + [seg(wl, 3)], axis=1))
        ws_in = [wT_all, w_all]
        g1 = g_norm1[l].reshape(1, d)
        (aqT, iqT, iwT, bqT, bvT, ik, kv, kvT, bk, cg0, cg1, cg2) = _in_proj(
            x2d, mod[l], g1, g_kv[l].reshape(1, A_LATENT), ws_in, bsz, seq)

        o_a = _dsa(aqT, iqT, iwT,
                   ik.reshape(bsz, nk, T, IDX_DIM), kv.reshape(bsz, nk, T, A_LATENT), kvT,
                   bias_a, cast(w_uk[l].transpose(0, 2, 1)), cast(w_uv[l].transpose(0, 2, 1)))
        o_a = o_a.reshape(n, -1)

        lam_init = 0.8 - 0.6 * math.exp(-0.3 * l)
        lam_rows = jnp.stack([lam_q1[l], lam_k1[l], lam_q2[l], lam_k2[l]])
        dv = 2 * HEAD_DIM
        o_b = _diff(bqT, bk.reshape(bsz, nk, T, B_HEADS * dv), bvT, bias_b, lam_rows,
                    g_subln[l].reshape(dv, 1), lam_init)
        o_b = o_b.reshape(n, -1)

        ocs, lses = [], []
        for g, (cg, (window, dil)) in enumerate(zip((cg0, cg1, cg2), C_GROUPS)):
            assert window // dil == C_BAND
            o, s = _dilated_group(cg, bias_c[g * C_HPG:(g + 1) * C_HPG], g)
            ocs.append(o)
            lses.append(s)

        ws_out = [cast(wl[:, int(offs[11]):int(offs[14])]),
                  cast(w_branch_a[l]), cast(w_branch_b[l]), cast(w_branch_c[l]), cast(w_out[l]),
                  cast(w_gate_up[l]), cast(w_down[l])]
        x2d = _merge_ffn(x2d, mod[l], g1, g_norm2[l].reshape(1, d), g_final.reshape(1, d),
                         o_a, o_b, ocs, lses, ws_out, seq, final_norm=(l == depth - 1))
    return x2d.reshape(bsz, seq, d)
```

```python
import functools
import math

import numpy as np
import jax
import jax.numpy as jnp
from jax import lax
from jax.experimental import pallas as pl
from jax.experimental.pallas import tpu as pltpu

D_MODEL = 1024
HEAD_DIM = 64
ATTN_SCALE = HEAD_DIM ** -0.5
LOG2E = math.log2(math.e)
A_HEADS = 8
A_LATENT = 128
IDX_HEADS = 8
IDX_DIM = 64
IDX_SCALE = (IDX_HEADS * IDX_DIM) ** -0.5
TOPK_MAX = 256
B_HEADS = 4
C_GROUPS = ((128, 1), (512, 4), (2048, 16))
C_HPG = 4
C_HEADS = C_HPG * len(C_GROUPS)
N_BUCKETS = 32
MAX_DISTANCE = 2048
N_BIAS_HEADS = A_HEADS + B_HEADS + C_HEADS
D_FF = -(-8 * D_MODEL // (3 * 256)) * 256
EPS = 1e-6

MM_DTYPE = jnp.bfloat16
F32 = jnp.float32
I32 = jnp.int32

ATT_TILE = 256
N_OFFSETS = 8
C_BAND = 128
C_ROWS = 2048
ROW_TILE = 512
IN_ROW_TILE = 1024
FFN_CHUNK = 256
MOD_COLS = 1536
NEG = -1e30
INT_MIN = -2 ** 31
LANES = 128
SUBLANES = 8
KEY_BITS = 32
SEARCH_TILES = 4
VMEM_LIMIT = 56 * 1024 * 1024


def _cparams(n_axes, vmem=VMEM_LIMIT):
    return pltpu.CompilerParams(dimension_semantics=("arbitrary",) * n_axes,
                                vmem_limit_bytes=vmem)


def _const_spec(shape):
    nd = len(shape)
    return pl.BlockSpec(shape, lambda *_: (0,) * nd, pipeline_mode=pl.Buffered(1))


def _bucket_thresholds(max_dist):
    n = np.arange(max_dist + 1)
    max_exact = N_BUCKETS // 2
    nf = np.maximum(n, 1).astype(np.float32)
    large = max_exact + (np.log(nf / np.float32(max_exact))
                         / np.float32(math.log(MAX_DISTANCE / max_exact))
                         * np.float32(N_BUCKETS - max_exact)).astype(np.int32)
    large = np.minimum(large, N_BUCKETS - 1)
    bucket = np.where(n < max_exact, n, large)
    assert np.all(np.diff(bucket) >= 0)
    thr = []
    for k in range(1, N_BUCKETS):
        idx = np.nonzero(bucket >= k)[0]
        thr.append(int(idx[0]) if idx.size else None)
    return thr


def _bias_from_dist(dist, tab_ref, col, thresholds, lo=0, hi=None):
    reached = [k for k, thr in enumerate(thresholds, start=1) if thr is not None]
    base = max([0] + [k for k in reached if thresholds[k - 1] <= lo])
    b = jnp.full(dist.shape, tab_ref[base * N_BIAS_HEADS + col], F32)
    for k in reached:
        thr = thresholds[k - 1]
        if thr > lo and (hi is None or thr <= hi):
            b = jnp.where(dist >= thr, tab_ref[k * N_BIAS_HEADS + col], b)
    return b


def _bias_att_kernel(tab_ref, out_ref, *, head0, thresholds):
    h = pl.program_id(0)
    n_off, tile, _ = out_ref.shape
    row = lax.broadcasted_iota(I32, (tile, tile), 0)
    colq = lax.broadcasted_iota(I32, (tile, tile), 1)
    for o in range(n_off):
        dist = jnp.maximum(o * tile + colq - row, 0)
        lo, hi = max(o * tile - (tile - 1), 0), o * tile + tile - 1
        b = _bias_from_dist(dist, tab_ref, head0 + h, thresholds, lo, hi) * LOG2E
        if o == 0:
            b = jnp.where(row <= colq, b, NEG)
        out_ref[o] = b


def _bias_att_tiles(tab, head0, n_heads, tile, thresholds):
    return pl.pallas_call(
        functools.partial(_bias_att_kernel, head0=head0, thresholds=thresholds),
        out_shape=jax.ShapeDtypeStruct((n_heads, N_OFFSETS, tile, tile), F32),
        grid=(n_heads,),
        in_specs=[pl.BlockSpec(memory_space=pltpu.SMEM)],
        out_specs=pl.BlockSpec((None, N_OFFSETS, tile, tile), lambda h: (h, 0, 0, 0)),
        compiler_params=_cparams(1),
        name="bias_att_tiles",
    )(tab)


def _bias_dil_kernel(tab_ref, out_ref, *, head0, thresholds):
    h = pl.program_id(0)
    g = h // C_HPG
    dil = jnp.where(g == 0, C_GROUPS[0][1], jnp.where(g == 1, C_GROUPS[1][1], C_GROUPS[2][1]))
    i = lax.broadcasted_iota(I32, (C_BAND, 2 * C_BAND), 0)
    j = lax.broadcasted_iota(I32, (C_BAND, 2 * C_BAND), 1)
    dist = jnp.maximum((i - j + C_BAND) * dil, 0)
    out_ref[...] = _bias_from_dist(dist, tab_ref, head0 + h, thresholds) * LOG2E


def _bias_dil_tiles(tab, head0, thresholds):
    return pl.pallas_call(
        functools.partial(_bias_dil_kernel, head0=head0, thresholds=thresholds),
        out_shape=jax.ShapeDtypeStruct((C_HEADS, C_BAND, 2 * C_BAND), F32),
        grid=(C_HEADS,),
        in_specs=[pl.BlockSpec(memory_space=pltpu.SMEM)],
        out_specs=pl.BlockSpec((None, C_BAND, 2 * C_BAND), lambda h: (h, 0, 0)),
        compiler_params=_cparams(1),
        name="bias_dil_tiles",
    )(tab)


def _mod_kernel(c_ref, w_ref, b_ref, out_ref):
    c = c_ref[...]
    ca = (c * jax.nn.sigmoid(c)).astype(MM_DTYPE)
    out_ref[...] = jnp.dot(ca, w_ref[...].astype(MM_DTYPE), preferred_element_type=F32) + b_ref[...]


def _modulation(c, w_ada, b_ada):
    depth, d, wid = w_ada.shape
    bsz = c.shape[0]
    tn = MOD_COLS
    assert wid % tn == 0
    return pl.pallas_call(
        _mod_kernel,
        out_shape=jax.ShapeDtypeStruct((depth, bsz, wid), F32),
        grid=(depth, wid // tn),
        in_specs=[pl.BlockSpec((bsz, d), lambda l, j: (0, 0)),
                  pl.BlockSpec((None, d, tn), lambda l, j: (l, 0, j)),
                  pl.BlockSpec((None, 1, tn), lambda l, j: (l, 0, j))],
        out_specs=pl.BlockSpec((None, bsz, tn), lambda l, j: (l, 0, j)),
        compiler_params=_cparams(2),
        name="adaln_modulation",
    )(c, w_ada, b_ada.reshape(depth, 1, wid))


def _rms(x):
    return x * lax.rsqrt(jnp.mean(x * x, axis=-1, keepdims=True) + EPS)


IN_ROWS_T = (("aq", A_HEADS * HEAD_DIM), ("iq", IDX_HEADS * IDX_DIM), ("bq", B_HEADS * 2 * HEAD_DIM),
             ("bv", B_HEADS * 2 * HEAD_DIM), ("iw", 16))
IN_COLS = (("kv", A_LATENT), ("bk", B_HEADS * 2 * HEAD_DIM), ("c", 3 * C_HEADS * HEAD_DIM),
           ("ik", IDX_DIM))


def _segments(layout):
    out, start = {}, 0
    for name, width in layout:
        out[name] = slice(start, start + width)
        start += width
    return out


def _in_kernel(x_ref, mod_ref, g1_ref, gkv_ref, wT_ref, w_ref,
               o_aqT, o_iqT, o_iwT, o_bqT, o_bvT, o_ik, o_kv, o_kvT, o_bk, o_c0, o_c1, o_c2,
               c_scr):
    T = ATT_TILE
    tm = x_ref.shape[0]
    h = _rms(x_ref[...]) * g1_ref[...]
    h = h * (1.0 + mod_ref[1:2, :]) + mod_ref[0:1, :]
    hb = h.astype(MM_DTYPE)
    rows_t, cols = _segments(IN_ROWS_T), _segments(IN_COLS)

    def mm(name):
        return jnp.dot(hb, w_ref[:, cols[name]], preferred_element_type=F32)

    def mm_t(name):
        return lax.dot_general(wT_ref[rows_t[name], :], hb, (((1,), (1,)), ((), ())),
                               preferred_element_type=F32)


    o_aqT[...] = mm_t("aq").astype(o_aqT.dtype)
    o_iqT[...] = mm_t("iq").astype(o_iqT.dtype)
    o_bqT[...] = (mm_t("bq") * (ATTN_SCALE * LOG2E)).astype(o_bqT.dtype)
    o_iwT[...] = (mm_t("iw") * IDX_SCALE)[:IDX_HEADS]
    bvT = mm_t("bv").astype(o_bvT.dtype)
    kv = _rms(mm("kv")) * gkv_ref[...]
    kvT = kv.T.astype(o_kvT.dtype)
    for j in range(tm // T):
        o_bvT[j] = bvT[:, j * T:(j + 1) * T]
        o_kvT[j] = kvT[:, j * T:(j + 1) * T]
    o_kv[...] = kv.astype(o_kv.dtype)
    o_ik[...] = mm("ik").astype(o_ik.dtype)
    o_bk[...] = mm("bk").astype(o_bk.dtype)

    yc = mm("c")
    n_chunk = yc.shape[1] // LANES
    for j in range(n_chunk):
        c_scr[j] = yc[:, j * LANES:(j + 1) * LANES]
    per_group = n_chunk // len(C_GROUPS)
    for g, o_c in enumerate((o_c0, o_c1, o_c2)):
        dil = C_GROUPS[g][1]
        for r in range(dil):
            for jj in range(per_group):
                o_c[r, :, jj * LANES:(jj + 1) * LANES] = c_scr[
                    g * per_group + jj, pl.ds(r, tm // dil, stride=dil), :].astype(o_c.dtype)


def _in_proj(x2d, mod_l, g1, gkv, ws, bsz, seq):
    n, d = x2d.shape
    tm = IN_ROW_TILE
    T = ATT_TILE
    per_b = seq // tm
    nk = seq // T
    hd = A_HEADS * HEAD_DIM
    bw = B_HEADS * 2 * HEAD_DIM
    gw = 3 * C_HPG * HEAD_DIM
    in_specs = [pl.BlockSpec((tm, d), lambda i: (i, 0)),
                pl.BlockSpec((None, 6, d), lambda i: (i // per_b, 0, 0)),
                _const_spec((1, d)), _const_spec((1, A_LATENT))]
    in_specs += [_const_spec(w.shape) for w in ws]

    def tspec(rows):
        return pl.BlockSpec((None, rows, tm), lambda i: (i // per_b, 0, i % per_b))

    def tile_tspec(rows):
        return pl.BlockSpec((None, tm // T, rows, T), lambda i: (i // per_b, i % per_b, 0, 0))

    def rspec(wd):
        return pl.BlockSpec((tm, wd), lambda i: (i, 0))

    def cspec(dil):
        return pl.BlockSpec((None, dil, tm // dil, gw), lambda i: (i // per_b, 0, i % per_b, 0))

    sds = jax.ShapeDtypeStruct
    out_specs = [tspec(hd), tspec(IDX_HEADS * IDX_DIM), tspec(IDX_HEADS), tspec(bw),
                 tile_tspec(bw), rspec(IDX_DIM), rspec(A_LATENT), tile_tspec(A_LATENT), rspec(bw)]
    out_shape = [sds((bsz, hd, seq), MM_DTYPE), sds((bsz, IDX_HEADS * IDX_DIM, seq), MM_DTYPE),
                 sds((bsz, IDX_HEADS, seq), F32), sds((bsz, bw, seq), MM_DTYPE),
                 sds((bsz, nk, bw, T), MM_DTYPE), sds((n, IDX_DIM), MM_DTYPE),
                 sds((n, A_LATENT), MM_DTYPE), sds((bsz, nk, A_LATENT, T), MM_DTYPE),
                 sds((n, bw), MM_DTYPE)]
    for _, dil in C_GROUPS:
        out_specs.append(cspec(dil))
        out_shape.append(sds((bsz, dil, seq // dil, gw), MM_DTYPE))
    return pl.pallas_call(
        _in_kernel, out_shape=out_shape, grid=(n // tm,),
        in_specs=in_specs, out_specs=out_specs,
        scratch_shapes=[pltpu.VMEM((3 * gw // LANES, tm, LANES), F32)],
        compiler_params=_cparams(1), name="in_proj",
    )(x2d, mod_l, g1, gkv, *ws)


def _initial_max(n_chains, tile):
    return tuple(jnp.full((1, tile), NEG, F32) for _ in range(n_chains))


def _loop_grouped(n, body, init, log2_group, start=0):
    carry, done = init, jnp.int32(0) + start
    end = done + n
    for lg in range(log2_group, -1, -1):
        group = 1 << lg
        trips = lax.shift_right_logical(end - done, jnp.int32(lg))

        def grouped(i, c, group=group, done=done):
            for j in range(group):
                c = body(done + group * i + j, c)
            return c

        carry = lax.fori_loop(0, trips, grouped, carry)
        done = done + group * trips
    return carry


SUM_ROWS = 16


def _with_ones_row(vT):
    row = lax.broadcasted_iota(I32, (SUM_ROWS, vT.shape[1]), 0)
    ones = jnp.where(row == 0, 1.0, 0.0).astype(vT.dtype)
    return jnp.concatenate([vT, ones], axis=0)


def _bit_transpose32(words):
    a = list(words)
    j, mask = 16, 0x0000FFFF
    while j:
        k = 0
        while k < 32:
            t = (a[k] ^ lax.shift_right_logical(a[k + j], jnp.int32(j))) & jnp.int32(mask)
            a[k] = a[k] ^ t
            a[k + j] = a[k + j] ^ lax.shift_left(t, jnp.int32(j))
            k = (k + j + 1) & ~j
        j >>= 1
        mask = (mask ^ (mask << j)) & 0xFFFFFFFF
    return a


def _dsa_kernel(aqT_ref, iqT_ref, iwT_ref, kidx_ref, kv_ref, kvT_ref, bias_ref, wukT_ref, wuvT_ref,
                out_ref, keys_ref, planes_ref, found_ref, qlat_ref, acc_ref, s_ref, oT_ref,
                *, topk, far_bias):
    T = ATT_TILE
    qi = pl.program_id(1)
    nk = qi + 1

    for h in range(A_HEADS):
        q = jnp.dot(wukT_ref[h], aqT_ref[h * HEAD_DIM:(h + 1) * HEAD_DIM, :],
                    preferred_element_type=F32) * (ATTN_SCALE * LOG2E)
        qlat_ref[h] = q.astype(qlat_ref.dtype)

    s_loc = lax.broadcasted_iota(I32, (T, T), 0)
    t_loc = lax.broadcasted_iota(I32, (T, T), 1)

    def score_tile(kj, diagonal):
        kt = kidx_ref[kj]
        acc = jnp.zeros((T, T), F32)
        for h in range(IDX_HEADS):
            s = jnp.dot(kt, iqT_ref[h * IDX_DIM:(h + 1) * IDX_DIM, :], preferred_element_type=F32)
            acc = acc + jnp.maximum(s, 0.0) * iwT_ref[h:h + 1, :]
        bits = lax.bitcast_convert_type(acc, I32)
        key = jnp.where(bits < 0, bits ^ jnp.int32(0x7FFFFFFF), bits)
        if diagonal:
            key = jnp.where(s_loc <= t_loc, key, jnp.int32(INT_MIN))
        keys_ref[kj] = key
        planes = _bit_transpose32([key[SUBLANES * i:SUBLANES * (i + 1), :] ^ jnp.int32(INT_MIN)
                                   for i in range(KEY_BITS)])
        for b in range(KEY_BITS):
            planes_ref[kj, b] = planes[b]

    def off_diagonal(kj, carry):
        score_tile(kj, False)
        return carry

    _loop_grouped(qi, off_diagonal, 0, log2_group=2)
    score_tile(qi, True)

    n_tiles = planes_ref.shape[0]

    @pl.when(qi == 0)
    def _():
        def clear_planes(kj, carry):
            for b in range(KEY_BITS):
                planes_ref[kj, b] = jnp.zeros((SUBLANES, T), I32)
            return carry

        lax.fori_loop(1, n_tiles, clear_planes, 0)

    one = jnp.int32(1)
    nil = jnp.int32(0)
    zero = jnp.zeros((1, T), I32)

    n_rows = n_tiles * SUBLANES

    def search(tiles):
        rows = tiles * SUBLANES
        tile_of_row = lax.broadcasted_iota(I32, (rows, T), 0) // SUBLANES
        tied0 = jnp.where(tile_of_row < nk, jnp.int32(-1), nil)

        def bit_step(b, carry):
            tied, n_gt, kth_u = carry
            ones = tied & planes_ref[0:tiles, b].reshape(rows, T)
            n1 = jnp.sum(lax.population_count(ones), axis=0, keepdims=True)
            take = (n_gt + n1) >= topk
            tied = jnp.where(take, ones, tied ^ ones)
            n_gt = jnp.where(take, n_gt, n_gt + n1)
            kth_u = jnp.where(take, kth_u | lax.shift_left(one, jnp.int32(KEY_BITS - 1) - b), kth_u)
            return tied, n_gt, kth_u

        tied, n_gt, kth_u = lax.fori_loop(0, KEY_BITS, bit_step, (tied0, zero, zero))
        found_ref[0:rows, :] = tied
        if rows < n_rows:
            found_ref[rows:n_rows, :] = jnp.zeros((n_rows - rows, T), I32)
        found_ref[n_rows:n_rows + SUBLANES, :] = jnp.broadcast_to(n_gt, (SUBLANES, T))
        found_ref[n_rows + SUBLANES:n_rows + 2 * SUBLANES, :] = jnp.broadcast_to(kth_u, (SUBLANES, T))

    covered = 0
    for tiles in list(range(SEARCH_TILES, n_tiles, SEARCH_TILES)) + [n_tiles]:
        pl.when(jnp.logical_and(nk > covered, nk <= tiles))(functools.partial(search, tiles))
        covered = tiles

    tied = found_ref[0:n_rows, :]
    n_gt = found_ref[n_rows:n_rows + 1, :]
    kth_u = found_ref[n_rows + SUBLANES:n_rows + SUBLANES + 1, :]
    n_eq = jnp.sum(lax.population_count(tied), axis=0, keepdims=True)
    n_ge = jnp.where(kth_u == nil, n_gt, n_gt + n_eq)
    kth = jnp.maximum(kth_u ^ jnp.int32(INT_MIN), jnp.int32(INT_MIN + 1))

    @pl.when(jnp.max(n_ge) > topk)
    def _():
        need = topk - n_gt
        r_bits = SUBLANES.bit_length() - 1

        before, cut_tile, rank = zero, zero, need
        words = jnp.zeros((SUBLANES, T), I32)
        for kj in range(n_tiles):
            tile_words = tied[kj * SUBLANES:(kj + 1) * SUBLANES, :]
            after = before + jnp.sum(lax.population_count(tile_words), axis=0, keepdims=True)
            here = jnp.where(before < need, jnp.where(after >= need, one, nil), nil) == one
            cut_tile = jnp.where(here, jnp.int32(kj), cut_tile)
            rank = jnp.where(here, need - before, rank)
            words = jnp.where(here, tile_words, words)
            before = after

        word_r = lax.broadcasted_iota(I32, (SUBLANES, T), 0)
        local = zero
        for bit in reversed(range(T.bit_length() - 1)):
            cand = local | jnp.int32(1 << bit)
            p_i = lax.shift_right_logical(cand, jnp.int32(r_bits))
            p_r = cand & jnp.int32(SUBLANES - 1)
            above = jnp.where(p_i == nil, nil,
                              lax.shift_left(jnp.full_like(p_i, -1), jnp.int32(KEY_BITS) - p_i))
            at_i = lax.shift_right_logical(
                words, jnp.broadcast_to(jnp.int32(KEY_BITS - 1) - p_i, words.shape)) & one
            below = lax.population_count(words & above) + jnp.where(word_r < p_r, at_i, nil)
            local = jnp.where(jnp.sum(below, axis=0, keepdims=True) < rank, cand, local)
        cut = cut_tile * T + local

        def demote(kj, carry):
            kk = keys_ref[kj]
            lowered = jnp.where((kj * T + s_loc) > cut, kth - one, kk)
            keys_ref[kj] = jnp.where(kk == kth, lowered, kk)
            return carry

        lax.fori_loop(0, nk, demote, 0)

    acc_ref[...] = jnp.zeros(acc_ref.shape, F32)

    def stage_logits(kj, far=False):
        unselected = jnp.where(keys_ref[kj] >= kth, 0.0, NEG)
        off = jnp.minimum(qi - kj, N_OFFSETS - 1)
        kvt = kv_ref[kj]

        def one_head(h):
            if far:
                start = jnp.broadcast_to(bias_ref[h, N_OFFSETS - 1, 0:1, 0:1], (T, T))
            else:
                start = bias_ref[h, off]
            s = start + jnp.dot(kvt, qlat_ref[h], preferred_element_type=F32)
            s = s + unselected
            s_ref[h] = s
            return jnp.max(s, axis=0, keepdims=True)

        return one_head

    first = stage_logits(0)
    tile_max0 = tuple(first(h) for h in range(A_HEADS))

    def attend(kj, carry, far_next):
        ms, tile_max = carry
        kvTt = _with_ones_row(kvT_ref[kj])
        stage_next = stage_logits(jnp.minimum(kj + 1, nk - 1), far_next)
        new_m, new_max = [], []
        for h in range(A_HEADS):
            m_new = jnp.maximum(ms[h], tile_max[h])
            p = jnp.exp2(s_ref[h] - m_new).astype(MM_DTYPE)
            alpha = jnp.exp2(ms[h] - m_new)
            new_max.append(stage_next(h))
            acc_ref[h] = alpha * acc_ref[h] + jnp.dot(kvTt, p, preferred_element_type=F32)
            new_m.append(m_new)
        return tuple(new_m), tuple(new_max)

    n_far = jnp.maximum(qi - (N_OFFSETS - 1), 0) if far_bias else jnp.int32(0)
    carry = _loop_grouped(n_far, lambda kj, c: attend(kj, c, True),
                          (_initial_max(A_HEADS, T), tile_max0), log2_group=1)
    _loop_grouped(nk - n_far, lambda kj, c: attend(kj, c, False), carry, log2_group=1, start=n_far)

    for h in range(A_HEADS):
        o_lat = (acc_ref[h, 0:A_LATENT, :] / acc_ref[h, A_LATENT:A_LATENT + 1, :]).astype(MM_DTYPE)
        oT_ref[h * HEAD_DIM:(h + 1) * HEAD_DIM, :] = jnp.dot(
            wuvT_ref[h], o_lat, preferred_element_type=F32)
    out_ref[...] = oT_ref[...].T.astype(out_ref.dtype)


def _dsa(aqT, iqT, iwT, kidx, kv, kvT, bias_a, wukT, wuvT, far_bias):
    bsz, _, seq = aqT.shape
    T = ATT_TILE
    nk = seq // T
    topk = min(TOPK_MAX, seq // 4)
    assert T == KEY_BITS * SUBLANES
    qspec = lambda rows: pl.BlockSpec((None, rows, T), lambda b, i: (b, 0, i))
    kspec = lambda a, c: pl.BlockSpec((None, nk, a, c), lambda b, i: (b, 0, 0, 0))
    return pl.pallas_call(
        functools.partial(_dsa_kernel, topk=topk, far_bias=far_bias),
        out_shape=jax.ShapeDtypeStruct((bsz, seq, A_HEADS * HEAD_DIM), MM_DTYPE),
        grid=(bsz, nk),
        in_specs=[qspec(A_HEADS * HEAD_DIM), qspec(IDX_HEADS * IDX_DIM), qspec(IDX_HEADS),
                  kspec(T, IDX_DIM), kspec(T, A_LATENT), kspec(A_LATENT, T),
                  _const_spec(bias_a.shape), _const_spec(wukT.shape), _const_spec(wuvT.shape)],
        out_specs=pl.BlockSpec((None, T, A_HEADS * HEAD_DIM), lambda b, i: (b, i, 0)),
        scratch_shapes=[pltpu.VMEM((nk, T, T), I32),
                        pltpu.VMEM((nk, KEY_BITS + 1, SUBLANES, T), I32),
                        pltpu.VMEM(((nk + 2) * SUBLANES, T), I32),
                        pltpu.VMEM((A_HEADS, A_LATENT, T), MM_DTYPE),
                        pltpu.VMEM((A_HEADS, A_LATENT + SUM_ROWS, T), F32),
                        pltpu.VMEM((A_HEADS, T, T), F32),
                        pltpu.VMEM((A_HEADS * HEAD_DIM, T), F32)],
        compiler_params=_cparams(2), name="dsa_attention",
    )(aqT, iqT, iwT, kidx, kv, kvT, bias_a, wukT, wuvT)


def _diff_kernel(qT_ref, k_ref, vT_ref, bias_ref, lam_ref, gsub_ref, out_ref,
                 qz_ref, acc_ref, s_ref, oT_ref, *, lam_init):
    T = ATT_TILE
    dv = 2 * HEAD_DIM
    n_chain = 2 * B_HEADS
    qi = pl.program_id(1)
    half = lax.broadcasted_iota(I32, (dv, T), 0) < HEAD_DIM
    for h in range(B_HEADS):
        q = qT_ref[h * dv:(h + 1) * dv, :].astype(F32)
        qz_ref[2 * h] = jnp.where(half, q, 0.0).astype(MM_DTYPE)
        qz_ref[2 * h + 1] = jnp.where(half, 0.0, q).astype(MM_DTYPE)
    acc_ref[...] = jnp.zeros(acc_ref.shape, F32)

    def stage_logits(kj, c):
        h = c // 2
        off = jnp.minimum(qi - kj, N_OFFSETS - 1)
        s = bias_ref[h, off] + jnp.dot(k_ref[kj, :, h * dv:(h + 1) * dv], qz_ref[c],
                                       preferred_element_type=F32)
        s_ref[c] = s
        return jnp.max(s, axis=0, keepdims=True)

    tile_max0 = tuple(stage_logits(0, c) for c in range(n_chain))

    def step(kj, carry, last):
        ms, tile_max = carry
        new_m, new_max = [], []
        for c in range(n_chain):
            h = c // 2
            m_new = jnp.maximum(ms[c], tile_max[c])
            p = jnp.exp2(s_ref[c] - m_new).astype(MM_DTYPE)
            alpha = jnp.exp2(ms[c] - m_new)
            if not last:
                new_max.append(stage_logits(kj + 1, c))
            vT = _with_ones_row(vT_ref[kj, h * dv:(h + 1) * dv, :])
            acc_ref[c] = alpha * acc_ref[c] + jnp.dot(vT, p, preferred_element_type=F32)
            new_m.append(m_new)
        return tuple(new_m), tuple(new_max)

    carry = _loop_grouped(qi, lambda kj, cr: step(kj, cr, False),
                          (_initial_max(n_chain, T), tile_max0), log2_group=2)
    step(qi, carry, True)

    lr = lam_ref[...]
    lam = (jnp.exp(jnp.sum(lr[0:1, :] * lr[1:2, :], axis=1, keepdims=True))
           - jnp.exp(jnp.sum(lr[2:3, :] * lr[3:4, :], axis=1, keepdims=True)) + lam_init)

    def normalised(c):
        return acc_ref[c, 0:dv, :] / acc_ref[c, dv:dv + 1, :]

    for h in range(B_HEADS):
        attn = normalised(2 * h) - lam * normalised(2 * h + 1)
        y = attn * lax.rsqrt(jnp.mean(attn * attn, axis=0, keepdims=True) + EPS)
        oT_ref[h * dv:(h + 1) * dv, :] = y * gsub_ref[...] * (1.0 - lam_init)
    out_ref[...] = oT_ref[...].T.astype(out_ref.dtype)


def _diff(bqT, bk, bvT, bias_b, lam_rows, gsub, lam_init):
    bsz, _, seq = bqT.shape
    T = ATT_TILE
    nk = seq // T
    dv = 2 * HEAD_DIM
    qspec = pl.BlockSpec((None, B_HEADS * dv, T), lambda b, i: (b, 0, i))
    return pl.pallas_call(
        functools.partial(_diff_kernel, lam_init=lam_init),
        out_shape=jax.ShapeDtypeStruct((bsz, seq, B_HEADS * dv), MM_DTYPE),
        grid=(bsz, nk),
        in_specs=[qspec,
                  pl.BlockSpec((None, nk, T, B_HEADS * dv), lambda b, i: (b, 0, 0, 0)),
                  pl.BlockSpec((None, nk, B_HEADS * dv, T), lambda b, i: (b, 0, 0, 0)),
                  _const_spec(bias_b.shape), _const_spec((4, HEAD_DIM)), _const_spec((dv, 1))],
        out_specs=pl.BlockSpec((None, T, B_HEADS * dv), lambda b, i: (b, i, 0)),
        scratch_shapes=[pltpu.VMEM((2 * B_HEADS, dv, T), MM_DTYPE),
                        pltpu.VMEM((2 * B_HEADS, dv + SUM_ROWS, T), F32),
                        pltpu.VMEM((2 * B_HEADS, T, T), F32),
                        pltpu.VMEM((B_HEADS * dv, T), F32)],
        compiler_params=_cparams(2), name="diff_attention",
    )(bqT, bk, bvT, bias_b, lam_rows, gsub)


def _dil_kernel(cur_ref, halo_ref, bias_ref, out_ref, lse_ref, *, tq):
    n = C_BAND
    wid = C_HPG * HEAD_DIM
    halo_lo = jnp.where(pl.program_id(2) == 0, jnp.int32(n), jnp.int32(0))
    i = lax.broadcasted_iota(I32, (n, 2 * n), 0)
    j = lax.broadcasted_iota(I32, (n, 2 * n), 1)
    lane_head = lax.broadcasted_iota(I32, (n, wid), 1) // HEAD_DIM
    in_head = [lane_head == h for h in range(C_HPG)]
    band = jnp.where(j >= i, jnp.where(j <= i + n, 0.0, NEG), NEG)
    band0 = jnp.where(j >= jnp.maximum(i, halo_lo), jnp.where(j <= i + n, 0.0, NEG), NEG)
    bias = [bias_ref[h] + band for h in range(C_HPG)]
    bias0 = [bias_ref[h] + band0 for h in range(C_HPG)]

    def band_rows(r, c, lo, hi):
        if c == 0:
            return jnp.concatenate([halo_ref[r, :, lo:hi], cur_ref[r, 0:n, lo:hi]], axis=0)
        return cur_ref[r, (c - 1) * n:(c + 1) * n, lo:hi]

    blocks = [(r, c) for r in range(cur_ref.shape[0]) for c in range(tq // n)]
    logits = []
    for r, c in blocks:
        q = cur_ref[r, c * n:(c + 1) * n, 0:wid].astype(F32) * (ATTN_SCALE * LOG2E)
        keys = band_rows(r, c, wid, 2 * wid)
        for h in range(C_HPG):
            qh = jnp.where(in_head[h], q, 0.0).astype(MM_DTYPE)
            s = lax.dot_general(qh, keys, (((1,), (1,)), ((), ())), preferred_element_type=F32)
            logits.append(s + (bias0 if c == 0 else bias)[h])
    for k, (r, c) in enumerate(blocks):
        vals = band_rows(r, c, 2 * wid, 3 * wid)
        out = jnp.zeros((n, wid), F32)
        lse = jnp.zeros((n, wid), F32)
        for h in range(C_HPG):
            s = logits[k * C_HPG + h]
            m = jnp.max(s, axis=1, keepdims=True)
            p = jnp.exp2(s - m)
            den = jnp.sum(p, axis=1, keepdims=True)
            o = jnp.dot(p.astype(MM_DTYPE), vals, preferred_element_type=F32) * (1.0 / den)
            out = jnp.where(in_head[h], o, out)
            lse = jnp.where(in_head[h], m + jnp.log2(den), lse)
        out_ref[r, c * n:(c + 1) * n, :] = out
        lse_ref[r, c * n:(c + 1) * n, :] = lse


def _dilated_group(cg, bias_g, g):
    bsz, dil, m, gw = cg.shape
    wid = C_HPG * HEAD_DIM
    n = C_BAND
    assert m % n == 0 and gw == 3 * wid
    tq = min(m, C_ROWS)
    n_res = max(1, min(dil, C_ROWS // tq))
    cur = pl.BlockSpec((None, n_res, tq, gw), lambda b, r, i: (b, r, i, 0))
    halo = pl.BlockSpec((None, n_res, n, gw),
                        lambda b, r, i: (b, r, jnp.maximum(i * (tq // n) - 1, 0), 0))
    outspec = pl.BlockSpec((None, n_res, tq, wid), lambda b, r, i: (b, r, i, 0))
    return pl.pallas_call(
        functools.partial(_dil_kernel, tq=tq),
        out_shape=[jax.ShapeDtypeStruct((bsz, dil, m, wid), F32)] * 2,
        grid=(bsz, dil // n_res, m // tq),
        in_specs=[cur, halo, pl.BlockSpec((C_HPG, n, 2 * n), lambda b, r, i: (0, 0, 0))],
        out_specs=[outspec, outspec],
        compiler_params=_cparams(3), name=f"dilated_group{g}",
    )(cg, cg, bias_g)


def _merge_ffn_kernel(x_ref, mod_ref, g1_ref, g2_ref, gf_ref, oa_ref, ob_ref,
                      c0_ref, c1_ref, c2_ref, s0_ref, s1_ref, s2_ref,
                      wz, wba, wbb, wbc, wo, wgu, wd, out_ref, tok_ref, *, final_norm):
    x = x_ref[...]
    tm = x.shape[0]
    h = _rms(x) * g1_ref[...]
    h = h * (1.0 + mod_ref[1:2, :]) + mod_ref[0:1, :]
    hb = h.astype(MM_DTYPE)

    def token_order(k, ref):
        dil, _, w = ref.shape
        if dil == 1:
            return ref[0]
        n_chunk = w // LANES
        for r in range(dil):
            for j in range(n_chunk):
                tok_ref[k * n_chunk + j, pl.ds(r, tm // dil, stride=dil), :] = ref[
                    r, :, j * LANES:(j + 1) * LANES]
        return jnp.concatenate([tok_ref[k * n_chunk + j] for j in range(n_chunk)], axis=1)

    s0, s1, s2 = s0_ref[0], token_order(0, s1_ref), token_order(1, s2_ref)
    c0, c1, c2 = c0_ref[0], token_order(2, c1_ref), token_order(3, c2_ref)
    mx = jnp.maximum(jnp.maximum(s0, s1), s2)
    e0, e1, e2 = jnp.exp2(s0 - mx), jnp.exp2(s1 - mx), jnp.exp2(s2 - mx)
    oc = (e0 * c0 + e1 * c1 + e2 * c2) / (e0 + e1 + e2)

    d = x.shape[1]

    def gated(k, o, wb):
        z = jnp.dot(hb, wz[:, k * d:(k + 1) * d], preferred_element_type=F32)
        return jax.nn.sigmoid(z) * jnp.dot(o, wb[...], preferred_element_type=F32)

    merged = (gated(0, oa_ref[...], wba) + gated(1, ob_ref[...], wbb)
              + gated(2, oc.astype(MM_DTYPE), wbc))
    y = jnp.dot(merged.astype(MM_DTYPE), wo[...], preferred_element_type=F32)
    x = x + mod_ref[2:3, :] * y

    h = _rms(x) * g2_ref[...]
    h = h * (1.0 + mod_ref[4:5, :]) + mod_ref[3:4, :]
    hb = h.astype(MM_DTYPE)
    acc = jnp.zeros(x.shape, F32)
    for c in range(D_FF // FFN_CHUNK):
        cols = slice(c * FFN_CHUNK, (c + 1) * FFN_CHUNK)
        fg = jnp.dot(hb, wgu[:, cols], preferred_element_type=F32)
        fu = jnp.dot(hb, wgu[:, D_FF + c * FFN_CHUNK:D_FF + (c + 1) * FFN_CHUNK],
                     preferred_element_type=F32)
        act = (fg * jax.nn.sigmoid(fg) * fu).astype(MM_DTYPE)
        acc = acc + jnp.dot(act, wd[cols, :], preferred_element_type=F32)
    y = x + mod_ref[5:6, :] * acc
    if final_norm:
        y = _rms(y) * gf_ref[...]
    out_ref[...] = y


def _merge_ffn(x2d, mod_l, g1, g2, gf, oa, ob, ocs, lses, ws, seq, final_norm):
    n, d = x2d.shape
    tm = ROW_TILE
    per_b = seq // tm
    row = lambda wd: pl.BlockSpec((tm, wd), lambda i: (i, 0))
    vec = _const_spec((1, d))
    wid = C_HPG * HEAD_DIM
    res = [pl.BlockSpec((None, dil, tm // dil, wid), lambda i: (i // per_b, 0, i % per_b, 0))
           for _, dil in C_GROUPS]
    in_specs = [row(d), pl.BlockSpec((None, 6, d), lambda i: (i // per_b, 0, 0)), vec, vec, vec,
                row(oa.shape[1]), row(ob.shape[1])] + res + res
    in_specs += [_const_spec(w.shape) for w in ws]
    return pl.pallas_call(
        functools.partial(_merge_ffn_kernel, final_norm=final_norm),
        out_shape=jax.ShapeDtypeStruct((n, d), F32), grid=(n // tm,),
        in_specs=in_specs, out_specs=row(d),
        scratch_shapes=[pltpu.VMEM((4 * wid // LANES, tm, LANES), F32)],
        compiler_params=_cparams(1), name="merge_ffn",
    )(x2d, mod_l, g1, g2, gf, oa, ob, *ocs, *lses, *ws)


def kernel(x, c, w_ada, b_ada, g_norm1, w_in, w_uk, w_uv, g_kv, lam_q1, lam_k1, lam_q2, lam_k2,
           g_subln, w_branch_a, w_branch_b, w_branch_c, w_out, g_norm2, w_gate_up, w_down,
           rel_bias, g_final):
    bsz, seq, d = x.shape
    depth = w_ada.shape[0]
    T = ATT_TILE
    nk = seq // T
    assert d == D_MODEL and seq % T == 0 and seq % ROW_TILE == 0 and seq % IN_ROW_TILE == 0
    n = bsz * seq
    cast = lambda w: w.astype(MM_DTYPE)

    thresholds = _bucket_thresholds(seq + 2 * C_BAND * C_GROUPS[-1][1])
    assert seq <= N_OFFSETS * T or (N_OFFSETS - 2) * T + 1 >= thresholds[-1]
    tab = rel_bias.reshape(-1)
    bias_a = _bias_att_tiles(tab, 0, A_HEADS, T, thresholds)
    bias_b = _bias_att_tiles(tab, A_HEADS, B_HEADS, T, thresholds)
    bias_c = _bias_dil_tiles(tab, A_HEADS + B_HEADS, thresholds)

    mod = _modulation(c, w_ada, b_ada).reshape(depth, bsz, 6, d)

    splits = (A_HEADS * HEAD_DIM, A_LATENT, IDX_HEADS * IDX_DIM, IDX_DIM, IDX_HEADS,
              B_HEADS * 2 * HEAD_DIM, B_HEADS * 2 * HEAD_DIM, B_HEADS * 2 * HEAD_DIM,
              C_HEADS * HEAD_DIM, C_HEADS * HEAD_DIM, C_HEADS * HEAD_DIM, d, d, d)
    offs = np.concatenate([[0], np.cumsum(splits)])
    seg = lambda w, k: w[:, int(offs[k]):int(offs[k + 1])]

    x2d = x.reshape(n, d)
    for l in range(depth):
        wl = w_in[l]
        wid = C_HPG * HEAD_DIM
        w_iw = jnp.pad(seg(wl, 4), ((0, 0), (0, 16 - IDX_HEADS)))
        wT_all = cast(jnp.concatenate([seg(wl, 0), seg(wl, 2), seg(wl, 5), seg(wl, 7), w_iw], axis=1).T)
        w_all = cast(jnp.concatenate(
            [seg(wl, 1), seg(wl, 6)]
            + [seg(wl, k)[:, g * wid:(g + 1) * wid] for g in range(len(C_GROUPS)) for k in (8, 9, 10)]
            + [seg(wl, 3)], axis=1))
        ws_in = [wT_all, w_all]
        g1 = g_norm1[l].reshape(1, d)
        (aqT, iqT, iwT, bqT, bvT, ik, kv, kvT, bk, cg0, cg1, cg2) = _in_proj(
            x2d, mod[l], g1, g_kv[l].reshape(1, A_LATENT), ws_in, bsz, seq)

        o_a = _dsa(aqT, iqT, iwT,
                   ik.reshape(bsz, nk, T, IDX_DIM), kv.reshape(bsz, nk, T, A_LATENT), kvT,
                   bias_a, cast(w_uk[l].transpose(0, 2, 1)), cast(w_uv[l].transpose(0, 2, 1)),
                   far_bias=(N_OFFSETS - 2) * T + 1 >= thresholds[-1])
        o_a = o_a.reshape(n, -1)

        lam_init = 0.8 - 0.6 * math.exp(-0.3 * l)
        lam_rows = jnp.stack([lam_q1[l], lam_k1[l], lam_q2[l], lam_k2[l]])
        dv = 2 * HEAD_DIM
        o_b = _diff(bqT, bk.reshape(bsz, nk, T, B_HEADS * dv), bvT, bias_b, lam_rows,
                    g_subln[l].reshape(dv, 1), lam_init)
        o_b = o_b.reshape(n, -1)

        ocs, lses = [], []
        for g, (cg, (window, dil)) in enumerate(zip((cg0, cg1, cg2), C_GROUPS)):
            assert window // dil == C_BAND
            o, s = _dilated_group(cg, bias_c[g * C_HPG:(g + 1) * C_HPG], g)
            ocs.append(o)
            lses.append(s)

        ws_out = [cast(wl[:, int(offs[11]):int(offs[14])]),
                  cast(w_branch_a[l]), cast(w_branch_b[l]), cast(w_branch_c[l]), cast(w_out[l]),
                  cast(w_gate_up[l]), cast(w_down[l])]
        x2d = _merge_ffn(x2d, mod[l], g1, g_norm2[l].reshape(1, d), g_final.reshape(1, d),
                         o_a, o_b, ocs, lses, ws_out, seq, final_norm=(l == depth - 1))
    return x2d.reshape(bsz, seq, d)
```

```python
import functools
import math

import numpy as np
import jax
import jax.numpy as jnp
from jax import lax
from jax.experimental import pallas as pl
from jax.experimental.pallas import tpu as pltpu

D_MODEL = 1024
HEAD_DIM = 64
ATTN_SCALE = HEAD_DIM ** -0.5
LOG2E = math.log2(math.e)
A_HEADS = 8
A_LATENT = 128
IDX_HEADS = 8
IDX_DIM = 64
IDX_SCALE = (IDX_HEADS * IDX_DIM) ** -0.5
TOPK_MAX = 256
B_HEADS = 4
C_GROUPS = ((128, 1), (512, 4), (2048, 16))
C_HPG = 4
C_HEADS = C_HPG * len(C_GROUPS)
N_BUCKETS = 32
MAX_DISTANCE = 2048
N_BIAS_HEADS = A_HEADS + B_HEADS + C_HEADS
D_FF = -(-8 * D_MODEL // (3 * 256)) * 256
EPS = 1e-6

MM_DTYPE = jnp.bfloat16
F32 = jnp.float32
I32 = jnp.int32

ATT_TILE = 256
N_OFFSETS = 8
C_BAND = 128
C_ROWS = 2048
ROW_TILE = 512
IN_ROW_TILE = 1024
FFN_CHUNK = 256
MOD_COLS = 1536
NEG = -1e30
INT_MIN = -2 ** 31
LANES = 128
SUBLANES = 8
KEY_BITS = 32
SEARCH_TILES = 4
VMEM_LIMIT = 56 * 1024 * 1024


def _cparams(n_axes, vmem=VMEM_LIMIT):
    return pltpu.CompilerParams(dimension_semantics=("arbitrary",) * n_axes,
                                vmem_limit_bytes=vmem)


def _const_spec(shape):
    nd = len(shape)
    return pl.BlockSpec(shape, lambda *_: (0,) * nd, pipeline_mode=pl.Buffered(1))


def _bucket_thresholds(max_dist):
    n = np.arange(max_dist + 1)
    max_exact = N_BUCKETS // 2
    nf = np.maximum(n, 1).astype(np.float32)
    large = max_exact + (np.log(nf / np.float32(max_exact))
                         / np.float32(math.log(MAX_DISTANCE / max_exact))
                         * np.float32(N_BUCKETS - max_exact)).astype(np.int32)
    large = np.minimum(large, N_BUCKETS - 1)
    bucket = np.where(n < max_exact, n, large)
    assert np.all(np.diff(bucket) >= 0)
    thr = []
    for k in range(1, N_BUCKETS):
        idx = np.nonzero(bucket >= k)[0]
        thr.append(int(idx[0]) if idx.size else None)
    return thr


def _bias_from_dist(dist, tab_ref, col, thresholds, lo=0, hi=None):
    reached = [k for k, thr in enumerate(thresholds, start=1) if thr is not None]
    base = max([0] + [k for k in reached if thresholds[k - 1] <= lo])
    b = jnp.full(dist.shape, tab_ref[base * N_BIAS_HEADS + col], F32)
    for k in reached:
        thr = thresholds[k - 1]
        if thr > lo and (hi is None or thr <= hi):
            b = jnp.where(dist >= thr, tab_ref[k * N_BIAS_HEADS + col], b)
    return b


def _bias_att_kernel(tab_ref, out_ref, *, head0, thresholds):
    h = pl.program_id(0)
    n_off, tile, _ = out_ref.shape
    row = lax.broadcasted_iota(I32, (tile, tile), 0)
    colq = lax.broadcasted_iota(I32, (tile, tile), 1)
    for o in range(n_off):
        dist = jnp.maximum(o * tile + colq - row, 0)
        lo, hi = max(o * tile - (tile - 1), 0), o * tile + tile - 1
        b = _bias_from_dist(dist, tab_ref, head0 + h, thresholds, lo, hi) * LOG2E
        if o == 0:
            b = jnp.where(row <= colq, b, NEG)
        out_ref[o] = b


def _bias_att_tiles(tab, head0, n_heads, tile, thresholds):
    return pl.pallas_call(
        functools.partial(_bias_att_kernel, head0=head0, thresholds=thresholds),
        out_shape=jax.ShapeDtypeStruct((n_heads, N_OFFSETS, tile, tile), F32),
        grid=(n_heads,),
        in_specs=[pl.BlockSpec(memory_space=pltpu.SMEM)],
        out_specs=pl.BlockSpec((None, N_OFFSETS, tile, tile), lambda h: (h, 0, 0, 0)),
        compiler_params=_cparams(1),
        name="bias_att_tiles",
    )(tab)


def _bias_dil_kernel(tab_ref, out_ref, *, head0, thresholds):
    h = pl.program_id(0)
    g = h // C_HPG
    dil = jnp.where(g == 0, C_GROUPS[0][1], jnp.where(g == 1, C_GROUPS[1][1], C_GROUPS[2][1]))
    i = lax.broadcasted_iota(I32, (C_BAND, 2 * C_BAND), 0)
    j = lax.broadcasted_iota(I32, (C_BAND, 2 * C_BAND), 1)
    dist = jnp.maximum((i - j + C_BAND) * dil, 0)
    out_ref[...] = _bias_from_dist(dist, tab_ref, head0 + h, thresholds) * LOG2E


def _bias_dil_tiles(tab, head0, thresholds):
    return pl.pallas_call(
        functools.partial(_bias_dil_kernel, head0=head0, thresholds=thresholds),
        out_shape=jax.ShapeDtypeStruct((C_HEADS, C_BAND, 2 * C_BAND), F32),
        grid=(C_HEADS,),
        in_specs=[pl.BlockSpec(memory_space=pltpu.SMEM)],
        out_specs=pl.BlockSpec((None, C_BAND, 2 * C_BAND), lambda h: (h, 0, 0)),
        compiler_params=_cparams(1),
        name="bias_dil_tiles",
    )(tab)


def _mod_kernel(c_ref, w_ref, b_ref, out_ref):
    c = c_ref[...]
    ca = (c * jax.nn.sigmoid(c)).astype(MM_DTYPE)
    out_ref[...] = jnp.dot(ca, w_ref[...].astype(MM_DTYPE), preferred_element_type=F32) + b_ref[...]


def _modulation(c, w_ada, b_ada):
    depth, d, wid = w_ada.shape
    bsz = c.shape[0]
    tn = MOD_COLS
    assert wid % tn == 0
    return pl.pallas_call(
        _mod_kernel,
        out_shape=jax.ShapeDtypeStruct((depth, bsz, wid), F32),
        grid=(depth, wid // tn),
        in_specs=[pl.BlockSpec((bsz, d), lambda l, j: (0, 0)),
                  pl.BlockSpec((None, d, tn), lambda l, j: (l, 0, j)),
                  pl.BlockSpec((None, 1, tn), lambda l, j: (l, 0, j))],
        out_specs=pl.BlockSpec((None, bsz, tn), lambda l, j: (l, 0, j)),
        compiler_params=_cparams(2),
        name="adaln_modulation",
    )(c, w_ada, b_ada.reshape(depth, 1, wid))


def _rms(x):
    return x * lax.rsqrt(jnp.mean(x * x, axis=-1, keepdims=True) + EPS)


IN_ROWS_T = (("aq", A_HEADS * HEAD_DIM), ("iq", IDX_HEADS * IDX_DIM), ("bq", B_HEADS * 2 * HEAD_DIM),
             ("bv", B_HEADS * 2 * HEAD_DIM), ("iw", 16))
IN_COLS = (("kv", A_LATENT), ("bk", B_HEADS * 2 * HEAD_DIM), ("c", 3 * C_HEADS * HEAD_DIM),
           ("ik", IDX_DIM))


def _segments(layout):
    out, start = {}, 0
    for name, width in layout:
        out[name] = slice(start, start + width)
        start += width
    return out


def _in_kernel(x_ref, mod_ref, g1_ref, gkv_ref, wT_ref, w_ref,
               o_aqT, o_iqT, o_iwT, o_bqT, o_bvT, o_ik, o_kv, o_kvT, o_bk, o_c0, o_c1, o_c2,
               c_scr):
    T = ATT_TILE
    tm = x_ref.shape[0]
    h = _rms(x_ref[...]) * g1_ref[...]
    h = h * (1.0 + mod_ref[1:2, :]) + mod_ref[0:1, :]
    hb = h.astype(MM_DTYPE)
    rows_t, cols = _segments(IN_ROWS_T), _segments(IN_COLS)

    def mm(name):
        return jnp.dot(hb, w_ref[:, cols[name]], preferred_element_type=F32)

    def mm_t(name):
        return lax.dot_general(wT_ref[rows_t[name], :], hb, (((1,), (1,)), ((), ())),
                               preferred_element_type=F32)


    o_aqT[...] = mm_t("aq").astype(o_aqT.dtype)
    o_iqT[...] = mm_t("iq").astype(o_iqT.dtype)
    o_bqT[...] = (mm_t("bq") * (ATTN_SCALE * LOG2E)).astype(o_bqT.dtype)
    o_iwT[...] = (mm_t("iw") * IDX_SCALE)[:IDX_HEADS]
    bvT = mm_t("bv").astype(o_bvT.dtype)
    kv = _rms(mm("kv")) * gkv_ref[...]
    kvT = kv.T.astype(o_kvT.dtype)
    for j in range(tm // T):
        o_bvT[j] = bvT[:, j * T:(j + 1) * T]
        o_kvT[j] = kvT[:, j * T:(j + 1) * T]
    o_kv[...] = kv.astype(o_kv.dtype)
    o_ik[...] = mm("ik").astype(o_ik.dtype)
    o_bk[...] = mm("bk").astype(o_bk.dtype)

    yc = mm("c")
    n_chunk = yc.shape[1] // LANES
    for j in range(n_chunk):
        c_scr[j] = yc[:, j * LANES:(j + 1) * LANES]
    per_group = n_chunk // len(C_GROUPS)
    for g, o_c in enumerate((o_c0, o_c1, o_c2)):
        dil = C_GROUPS[g][1]
        for r in range(dil):
            for jj in range(per_group):
                o_c[r, :, jj * LANES:(jj + 1) * LANES] = c_scr[
                    g * per_group + jj, pl.ds(r, tm // dil, stride=dil), :].astype(o_c.dtype)


def _in_proj(x2d, mod_l, g1, gkv, ws, bsz, seq):
    n, d = x2d.shape
    tm = IN_ROW_TILE
    T = ATT_TILE
    per_b = seq // tm
    nk = seq // T
    hd = A_HEADS * HEAD_DIM
    bw = B_HEADS * 2 * HEAD_DIM
    gw = 3 * C_HPG * HEAD_DIM
    in_specs = [pl.BlockSpec((tm, d), lambda i: (i, 0)),
                pl.BlockSpec((None, 6, d), lambda i: (i // per_b, 0, 0)),
                _const_spec((1, d)), _const_spec((1, A_LATENT))]
    in_specs += [_const_spec(w.shape) for w in ws]

    def tspec(rows):
        return pl.BlockSpec((None, rows, tm), lambda i: (i // per_b, 0, i % per_b))

    def tile_tspec(rows):
        return pl.BlockSpec((None, tm // T, rows, T), lambda i: (i // per_b, i % per_b, 0, 0))

    def rspec(wd):
        return pl.BlockSpec((tm, wd), lambda i: (i, 0))

    def cspec(dil):
        return pl.BlockSpec((None, dil, tm // dil, gw), lambda i: (i // per_b, 0, i % per_b, 0))

    sds = jax.ShapeDtypeStruct
    out_specs = [tspec(hd), tspec(IDX_HEADS * IDX_DIM), tspec(IDX_HEADS), tspec(bw),
                 tile_tspec(bw), rspec(IDX_DIM), rspec(A_LATENT), tile_tspec(A_LATENT), rspec(bw)]
    out_shape = [sds((bsz, hd, seq), MM_DTYPE), sds((bsz, IDX_HEADS * IDX_DIM, seq), MM_DTYPE),
                 sds((bsz, IDX_HEADS, seq), F32), sds((bsz, bw, seq), MM_DTYPE),
                 sds((bsz, nk, bw, T), MM_DTYPE), sds((n, IDX_DIM), MM_DTYPE),
                 sds((n, A_LATENT), MM_DTYPE), sds((bsz, nk, A_LATENT, T), MM_DTYPE),
                 sds((n, bw), MM_DTYPE)]
    for _, dil in C_GROUPS:
        out_specs.append(cspec(dil))
        out_shape.append(sds((bsz, dil, seq // dil, gw), MM_DTYPE))
    return pl.pallas_call(
        _in_kernel, out_shape=out_shape, grid=(n // tm,),
        in_specs=in_specs, out_specs=out_specs,
        scratch_shapes=[pltpu.VMEM((3 * gw // LANES, tm, LANES), F32)],
        compiler_params=_cparams(1), name="in_proj",
    )(x2d, mod_l, g1, gkv, *ws)


def _initial_max(n_chains, tile):
    return tuple(jnp.full((1, tile), NEG, F32) for _ in range(n_chains))


def _loop_grouped(n, body, init, log2_group, start=0):
    carry, done = init, jnp.int32(0) + start
    end = done + n
    for lg in range(log2_group, -1, -1):
        group = 1 << lg
        trips = lax.shift_right_logical(end - done, jnp.int32(lg))

        def grouped(i, c, group=group, done=done):
            for j in range(group):
                c = body(done + group * i + j, c)
            return c

        carry = lax.fori_loop(0, trips, grouped, carry)
        done = done + group * trips
    return carry


SUM_ROWS = 16


def _with_ones_row(vT):
    row = lax.broadcasted_iota(I32, (SUM_ROWS, vT.shape[1]), 0)
    ones = jnp.where(row == 0, 1.0, 0.0).astype(vT.dtype)
    return jnp.concatenate([vT, ones], axis=0)


def _bit_transpose32(words):
    a = list(words)
    j, mask = 16, 0x0000FFFF
    while j:
        k = 0
        while k < 32:
            t = (a[k] ^ lax.shift_right_logical(a[k + j], jnp.int32(j))) & jnp.int32(mask)
            a[k] = a[k] ^ t
            a[k + j] = a[k + j] ^ lax.shift_left(t, jnp.int32(j))
            k = (k + j + 1) & ~j
        j >>= 1
        mask = (mask ^ (mask << j)) & 0xFFFFFFFF
    return a


def _dsa_kernel(aqT_ref, iqT_ref, iwT_ref, kidx_ref, kv_ref, kvT_ref, bias_ref, wukT_ref, wuvT_ref,
                out_ref, keys_ref, planes_ref, found_ref, qlat_ref, acc_ref, s_ref, oT_ref,
                *, topk, far_bias):
    T = ATT_TILE
    qi = pl.program_id(1)
    nk = qi + 1

    for h in range(A_HEADS):
        q = jnp.dot(wukT_ref[h], aqT_ref[h * HEAD_DIM:(h + 1) * HEAD_DIM, :],
                    preferred_element_type=F32) * (ATTN_SCALE * LOG2E)
        qlat_ref[h] = q.astype(qlat_ref.dtype)

    s_loc = lax.broadcasted_iota(I32, (T, T), 0)
    t_loc = lax.broadcasted_iota(I32, (T, T), 1)

    def score_tile(kj, diagonal):
        kt = kidx_ref[kj]
        acc = jnp.zeros((T, T), F32)
        for h in range(IDX_HEADS):
            s = jnp.dot(kt, iqT_ref[h * IDX_DIM:(h + 1) * IDX_DIM, :], preferred_element_type=F32)
            acc = acc + jnp.maximum(s, 0.0) * iwT_ref[h:h + 1, :]
        bits = lax.bitcast_convert_type(acc, I32)
        key = jnp.where(bits < 0, bits ^ jnp.int32(0x7FFFFFFF), bits)
        if diagonal:
            key = jnp.where(s_loc <= t_loc, key, jnp.int32(INT_MIN))
        keys_ref[kj] = key
        planes = _bit_transpose32([key[SUBLANES * i:SUBLANES * (i + 1), :] ^ jnp.int32(INT_MIN)
                                   for i in range(KEY_BITS)])
        for b in range(KEY_BITS):
            planes_ref[kj, b] = planes[b]

    def off_diagonal(kj, carry):
        score_tile(kj, False)
        return carry

    _loop_grouped(qi, off_diagonal, 0, log2_group=2)
    score_tile(qi, True)

    n_tiles = planes_ref.shape[0]

    @pl.when(qi == 0)
    def _():
        def clear_planes(kj, carry):
            for b in range(KEY_BITS):
                planes_ref[kj, b] = jnp.zeros((SUBLANES, T), I32)
            return carry

        lax.fori_loop(1, n_tiles, clear_planes, 0)

    one = jnp.int32(1)
    nil = jnp.int32(0)
    zero = jnp.zeros((1, T), I32)

    n_rows = n_tiles * SUBLANES

    def search(tiles):
        rows = tiles * SUBLANES
        tile_of_row = lax.broadcasted_iota(I32, (rows, T), 0) // SUBLANES
        tied0 = jnp.where(tile_of_row < nk, jnp.int32(-1), nil)

        def bit_step(b, carry):
            tied, n_gt, kth_u = carry
            ones = tied & planes_ref[0:tiles, b].reshape(rows, T)
            n1 = jnp.sum(lax.population_count(ones), axis=0, keepdims=True)
            take = (n_gt + n1) >= topk
            tied = jnp.where(take, ones, tied ^ ones)
            n_gt = jnp.where(take, n_gt, n_gt + n1)
            kth_u = jnp.where(take, kth_u | lax.shift_left(one, jnp.int32(KEY_BITS - 1) - b), kth_u)
            return tied, n_gt, kth_u

        tied, n_gt, kth_u = lax.fori_loop(0, KEY_BITS, bit_step, (tied0, zero, zero))
        found_ref[0:rows, :] = tied
        if rows < n_rows:
            found_ref[rows:n_rows, :] = jnp.zeros((n_rows - rows, T), I32)
        found_ref[n_rows:n_rows + SUBLANES, :] = jnp.broadcast_to(n_gt, (SUBLANES, T))
        found_ref[n_rows + SUBLANES:n_rows + 2 * SUBLANES, :] = jnp.broadcast_to(kth_u, (SUBLANES, T))

    covered = 0
    for tiles in list(range(SEARCH_TILES, n_tiles, SEARCH_TILES)) + [n_tiles]:
        pl.when(jnp.logical_and(nk > covered, nk <= tiles))(functools.partial(search, tiles))
        covered = tiles

    tied = found_ref[0:n_rows, :]
    n_gt = found_ref[n_rows:n_rows + 1, :]
    kth_u = found_ref[n_rows + SUBLANES:n_rows + SUBLANES + 1, :]
    n_eq = jnp.sum(lax.population_count(tied), axis=0, keepdims=True)
    n_ge = jnp.where(kth_u == nil, n_gt, n_gt + n_eq)
    kth = jnp.maximum(kth_u ^ jnp.int32(INT_MIN), jnp.int32(INT_MIN + 1))

    @pl.when(jnp.max(n_ge) > topk)
    def _():
        need = topk - n_gt
        r_bits = SUBLANES.bit_length() - 1

        before, cut_tile, rank = zero, zero, need
        words = jnp.zeros((SUBLANES, T), I32)
        for kj in range(n_tiles):
            tile_words = tied[kj * SUBLANES:(kj + 1) * SUBLANES, :]
            after = before + jnp.sum(lax.population_count(tile_words), axis=0, keepdims=True)
            here = jnp.where(before < need, jnp.where(after >= need, one, nil), nil) == one
            cut_tile = jnp.where(here, jnp.int32(kj), cut_tile)
            rank = jnp.where(here, need - before, rank)
            words = jnp.where(here, tile_words, words)
            before = after

        word_r = lax.broadcasted_iota(I32, (SUBLANES, T), 0)
        local = zero
        for bit in reversed(range(T.bit_length() - 1)):
            cand = local | jnp.int32(1 << bit)
            p_i = lax.shift_right_logical(cand, jnp.int32(r_bits))
            p_r = cand & jnp.int32(SUBLANES - 1)
            above = jnp.where(p_i == nil, nil,
                              lax.shift_left(jnp.full_like(p_i, -1), jnp.int32(KEY_BITS) - p_i))
            at_i = lax.shift_right_logical(
                words, jnp.broadcast_to(jnp.int32(KEY_BITS - 1) - p_i, words.shape)) & one
            below = lax.population_count(words & above) + jnp.where(word_r < p_r, at_i, nil)
            local = jnp.where(jnp.sum(below, axis=0, keepdims=True) < rank, cand, local)
        cut = cut_tile * T + local

        def demote(kj, carry):
            kk = keys_ref[kj]
            lowered = jnp.where((kj * T + s_loc) > cut, kth - one, kk)
            keys_ref[kj] = jnp.where(kk == kth, lowered, kk)
            return carry

        lax.fori_loop(0, nk, demote, 0)

    acc_ref[...] = jnp.zeros(acc_ref.shape, F32)

    def stage_logits(kj, far=False):
        unselected = jnp.where(keys_ref[kj] >= kth, 0.0, NEG)
        off = jnp.minimum(qi - kj, N_OFFSETS - 1)
        kvt = kv_ref[kj]

        def one_head(h):
            if far:
                start = jnp.broadcast_to(bias_ref[h, N_OFFSETS - 1, 0:1, 0:1], (T, T))
            else:
                start = bias_ref[h, off]
            s = start + jnp.dot(kvt, qlat_ref[h], preferred_element_type=F32)
            s = s + unselected
            s_ref[h] = s
            return jnp.max(s, axis=0, keepdims=True)

        return one_head

    first = stage_logits(0)
    tile_max0 = tuple(first(h) for h in range(A_HEADS))

    def attend(kj, carry, far_next):
        ms, tile_max = carry
        kvTt = _with_ones_row(kvT_ref[kj])
        stage_next = stage_logits(jnp.minimum(kj + 1, nk - 1), far_next)
        new_m, new_max = [], []
        for h in range(A_HEADS):
            m_new = jnp.maximum(ms[h], tile_max[h])
            p = jnp.exp2(s_ref[h] - m_new).astype(MM_DTYPE)
            alpha = jnp.exp2(ms[h] - m_new)
            new_max.append(stage_next(h))
            acc_ref[h] = alpha * acc_ref[h] + jnp.dot(kvTt, p, preferred_element_type=F32)
            new_m.append(m_new)
        return tuple(new_m), tuple(new_max)

    n_far = jnp.maximum(qi - (N_OFFSETS - 1), 0) if far_bias else jnp.int32(0)
    carry = _loop_grouped(n_far, lambda kj, c: attend(kj, c, True),
                          (_initial_max(A_HEADS, T), tile_max0), log2_group=1)
    _loop_grouped(nk - n_far, lambda kj, c: attend(kj, c, False), carry, log2_group=1, start=n_far)

    for h in range(A_HEADS):
        o_lat = (acc_ref[h, 0:A_LATENT, :] / acc_ref[h, A_LATENT:A_LATENT + 1, :]).astype(MM_DTYPE)
        oT_ref[h * HEAD_DIM:(h + 1) * HEAD_DIM, :] = jnp.dot(
            wuvT_ref[h], o_lat, preferred_element_type=F32)
    out_ref[...] = oT_ref[...].T.astype(out_ref.dtype)


def _dsa(aqT, iqT, iwT, kidx, kv, kvT, bias_a, wukT, wuvT, far_bias):
    bsz, _, seq = aqT.shape
    T = ATT_TILE
    nk = seq // T
    topk = min(TOPK_MAX, seq // 4)
    assert T == KEY_BITS * SUBLANES
    qspec = lambda rows: pl.BlockSpec((None, rows, T), lambda b, i: (b, 0, i))
    kspec = lambda a, c: pl.BlockSpec((None, nk, a, c), lambda b, i: (b, 0, 0, 0))
    return pl.pallas_call(
        functools.partial(_dsa_kernel, topk=topk, far_bias=far_bias),
        out_shape=jax.ShapeDtypeStruct((bsz, seq, A_HEADS * HEAD_DIM), MM_DTYPE),
        grid=(bsz, nk),
        in_specs=[qspec(A_HEADS * HEAD_DIM), qspec(IDX_HEADS * IDX_DIM), qspec(IDX_HEADS),
                  kspec(T, IDX_DIM), kspec(T, A_LATENT), kspec(A_LATENT, T),
                  _const_spec(bias_a.shape), _const_spec(wukT.shape), _const_spec(wuvT.shape)],
        out_specs=pl.BlockSpec((None, T, A_HEADS * HEAD_DIM), lambda b, i: (b, i, 0)),
        scratch_shapes=[pltpu.VMEM((nk, T, T), I32),
                        pltpu.VMEM((nk, KEY_BITS + 1, SUBLANES, T), I32),
                        pltpu.VMEM(((nk + 2) * SUBLANES, T), I32),
                        pltpu.VMEM((A_HEADS, A_LATENT, T), MM_DTYPE),
                        pltpu.VMEM((A_HEADS, A_LATENT + SUM_ROWS, T), F32),
                        pltpu.VMEM((A_HEADS, T, T), F32),
                        pltpu.VMEM((A_HEADS * HEAD_DIM, T), F32)],
        compiler_params=_cparams(2), name="dsa_attention",
    )(aqT, iqT, iwT, kidx, kv, kvT, bias_a, wukT, wuvT)


def _diff_kernel(qT_ref, k_ref, vT_ref, bias_ref, lam_ref, gsub_ref, out_ref,
                 qz_ref, acc_ref, s_ref, oT_ref, *, lam_init, far_bias):
    T = ATT_TILE
    dv = 2 * HEAD_DIM
    n_chain = 2 * B_HEADS
    qi = pl.program_id(1)
    half = lax.broadcasted_iota(I32, (dv, T), 0) < HEAD_DIM
    for h in range(B_HEADS):
        q = qT_ref[h * dv:(h + 1) * dv, :].astype(F32)
        qz_ref[2 * h] = jnp.where(half, q, 0.0).astype(MM_DTYPE)
        qz_ref[2 * h + 1] = jnp.where(half, 0.0, q).astype(MM_DTYPE)
    acc_ref[...] = jnp.zeros(acc_ref.shape, F32)

    def stage_logits(kj, c, far=False):
        h = c // 2
        if far:
            start = jnp.broadcast_to(bias_ref[h, N_OFFSETS - 1, 0:1, 0:1], (T, T))
        else:
            start = bias_ref[h, jnp.minimum(qi - kj, N_OFFSETS - 1)]
        s = start + jnp.dot(k_ref[kj, :, h * dv:(h + 1) * dv], qz_ref[c],
                            preferred_element_type=F32)
        s_ref[c] = s
        return jnp.max(s, axis=0, keepdims=True)

    tile_max0 = tuple(stage_logits(0, c) for c in range(n_chain))

    def step(kj, carry, last, far_next=False):
        ms, tile_max = carry
        new_m, new_max = [], []
        for c in range(n_chain):
            h = c // 2
            m_new = jnp.maximum(ms[c], tile_max[c])
            p = jnp.exp2(s_ref[c] - m_new).astype(MM_DTYPE)
            alpha = jnp.exp2(ms[c] - m_new)
            if not last:
                new_max.append(stage_logits(kj + 1, c, far_next))
            vT = _with_ones_row(vT_ref[kj, h * dv:(h + 1) * dv, :])
            acc_ref[c] = alpha * acc_ref[c] + jnp.dot(vT, p, preferred_element_type=F32)
            new_m.append(m_new)
        return tuple(new_m), tuple(new_max)

    n_far = jnp.maximum(qi - (N_OFFSETS - 1), 0) if far_bias else jnp.int32(0)
    carry = _loop_grouped(n_far, lambda kj, cr: step(kj, cr, False, True),
                          (_initial_max(n_chain, T), tile_max0), log2_group=2)
    carry = _loop_grouped(qi - n_far, lambda kj, cr: step(kj, cr, False), carry,
                          log2_group=2, start=n_far)
    step(qi, carry, True)

    lr = lam_ref[...]
    lam = (jnp.exp(jnp.sum(lr[0:1, :] * lr[1:2, :], axis=1, keepdims=True))
           - jnp.exp(jnp.sum(lr[2:3, :] * lr[3:4, :], axis=1, keepdims=True)) + lam_init)

    def normalised(c):
        return acc_ref[c, 0:dv, :] / acc_ref[c, dv:dv + 1, :]

    for h in range(B_HEADS):
        attn = normalised(2 * h) - lam * normalised(2 * h + 1)
        y = attn * lax.rsqrt(jnp.mean(attn * attn, axis=0, keepdims=True) + EPS)
        oT_ref[h * dv:(h + 1) * dv, :] = y * gsub_ref[...] * (1.0 - lam_init)
    out_ref[...] = oT_ref[...].T.astype(out_ref.dtype)


def _diff(bqT, bk, bvT, bias_b, lam_rows, gsub, lam_init, far_bias):
    bsz, _, seq = bqT.shape
    T = ATT_TILE
    nk = seq // T
    dv = 2 * HEAD_DIM
    qspec = pl.BlockSpec((None, B_HEADS * dv, T), lambda b, i: (b, 0, i))
    return pl.pallas_call(
        functools.partial(_diff_kernel, lam_init=lam_init, far_bias=far_bias),
        out_shape=jax.ShapeDtypeStruct((bsz, seq, B_HEADS * dv), MM_DTYPE),
        grid=(bsz, nk),
        in_specs=[qspec,
                  pl.BlockSpec((None, nk, T, B_HEADS * dv), lambda b, i: (b, 0, 0, 0)),
                  pl.BlockSpec((None, nk, B_HEADS * dv, T), lambda b, i: (b, 0, 0, 0)),
                  _const_spec(bias_b.shape), _const_spec((4, HEAD_DIM)), _const_spec((dv, 1))],
        out_specs=pl.BlockSpec((None, T, B_HEADS * dv), lambda b, i: (b, i, 0)),
        scratch_shapes=[pltpu.VMEM((2 * B_HEADS, dv, T), MM_DTYPE),
                        pltpu.VMEM((2 * B_HEADS, dv + SUM_ROWS, T), F32),
                        pltpu.VMEM((2 * B_HEADS, T, T), F32),
                        pltpu.VMEM((B_HEADS * dv, T), F32)],
        compiler_params=_cparams(2), name="diff_attention",
    )(bqT, bk, bvT, bias_b, lam_rows, gsub)


def _dil_kernel(cur_ref, halo_ref, bias_ref, out_ref, lse_ref, *, tq):
    n = C_BAND
    wid = C_HPG * HEAD_DIM
    halo_lo = jnp.where(pl.program_id(2) == 0, jnp.int32(n), jnp.int32(0))
    i = lax.broadcasted_iota(I32, (n, 2 * n), 0)
    j = lax.broadcasted_iota(I32, (n, 2 * n), 1)
    lane_head = lax.broadcasted_iota(I32, (n, wid), 1) // HEAD_DIM
    in_head = [lane_head == h for h in range(C_HPG)]
    band = jnp.where(j >= i, jnp.where(j <= i + n, 0.0, NEG), NEG)
    band0 = jnp.where(j >= jnp.maximum(i, halo_lo), jnp.where(j <= i + n, 0.0, NEG), NEG)
    bias = [bias_ref[h] + band for h in range(C_HPG)]
    bias0 = [bias_ref[h] + band0 for h in range(C_HPG)]

    def band_rows(r, c, lo, hi):
        if c == 0:
            return jnp.concatenate([halo_ref[r, :, lo:hi], cur_ref[r, 0:n, lo:hi]], axis=0)
        return cur_ref[r, (c - 1) * n:(c + 1) * n, lo:hi]

    blocks = [(r, c) for r in range(cur_ref.shape[0]) for c in range(tq // n)]
    logits = []
    for r, c in blocks:
        q = cur_ref[r, c * n:(c + 1) * n, 0:wid].astype(F32) * (ATTN_SCALE * LOG2E)
        keys = band_rows(r, c, wid, 2 * wid)
        for h in range(C_HPG):
            qh = jnp.where(in_head[h], q, 0.0).astype(MM_DTYPE)
            s = lax.dot_general(qh, keys, (((1,), (1,)), ((), ())), preferred_element_type=F32)
            logits.append(s + (bias0 if c == 0 else bias)[h])
    for k, (r, c) in enumerate(blocks):
        vals = band_rows(r, c, 2 * wid, 3 * wid)
        out = jnp.zeros((n, wid), F32)
        lse = jnp.zeros((n, wid), F32)
        for h in range(C_HPG):
            s = logits[k * C_HPG + h]
            m = jnp.max(s, axis=1, keepdims=True)
            p = jnp.exp2(s - m)
            den = jnp.sum(p, axis=1, keepdims=True)
            o = jnp.dot(p.astype(MM_DTYPE), vals, preferred_element_type=F32) * (1.0 / den)
            out = jnp.where(in_head[h], o, out)
            lse = jnp.where(in_head[h], m + jnp.log2(den), lse)
        out_ref[r, c * n:(c + 1) * n, :] = out
        lse_ref[r, c * n:(c + 1) * n, :] = lse


def _dilated_group(cg, bias_g, g):
    bsz, dil, m, gw = cg.shape
    wid = C_HPG * HEAD_DIM
    n = C_BAND
    assert m % n == 0 and gw == 3 * wid
    tq = min(m, C_ROWS)
    n_res = max(1, min(dil, C_ROWS // tq))
    cur = pl.BlockSpec((None, n_res, tq, gw), lambda b, r, i: (b, r, i, 0))
    halo = pl.BlockSpec((None, n_res, n, gw),
                        lambda b, r, i: (b, r, jnp.maximum(i * (tq // n) - 1, 0), 0))
    outspec = pl.BlockSpec((None, n_res, tq, wid), lambda b, r, i: (b, r, i, 0))
    return pl.pallas_call(
        functools.partial(_dil_kernel, tq=tq),
        out_shape=[jax.ShapeDtypeStruct((bsz, dil, m, wid), F32)] * 2,
        grid=(bsz, dil // n_res, m // tq),
        in_specs=[cur, halo, pl.BlockSpec((C_HPG, n, 2 * n), lambda b, r, i: (0, 0, 0))],
        out_specs=[outspec, outspec],
        compiler_params=_cparams(3), name=f"dilated_group{g}",
    )(cg, cg, bias_g)


def _merge_ffn_kernel(x_ref, mod_ref, g1_ref, g2_ref, gf_ref, oa_ref, ob_ref,
                      c0_ref, c1_ref, c2_ref, s0_ref, s1_ref, s2_ref,
                      wz, wba, wbb, wbc, wo, wgu, wd, out_ref, tok_ref, *, final_norm):
    x = x_ref[...]
    tm = x.shape[0]
    h = _rms(x) * g1_ref[...]
    h = h * (1.0 + mod_ref[1:2, :]) + mod_ref[0:1, :]
    hb = h.astype(MM_DTYPE)

    def token_order(k, ref):
        dil, _, w = ref.shape
        if dil == 1:
            return ref[0]
        n_chunk = w // LANES
        for r in range(dil):
            for j in range(n_chunk):
                tok_ref[k * n_chunk + j, pl.ds(r, tm // dil, stride=dil), :] = ref[
                    r, :, j * LANES:(j + 1) * LANES]
        return jnp.concatenate([tok_ref[k * n_chunk + j] for j in range(n_chunk)], axis=1)

    s0, s1, s2 = s0_ref[0], token_order(0, s1_ref), token_order(1, s2_ref)
    c0, c1, c2 = c0_ref[0], token_order(2, c1_ref), token_order(3, c2_ref)
    mx = jnp.maximum(jnp.maximum(s0, s1), s2)
    e0, e1, e2 = jnp.exp2(s0 - mx), jnp.exp2(s1 - mx), jnp.exp2(s2 - mx)
    oc = (e0 * c0 + e1 * c1 + e2 * c2) / (e0 + e1 + e2)

    d = x.shape[1]

    def gated(k, o, wb):
        z = jnp.dot(hb, wz[:, k * d:(k + 1) * d], preferred_element_type=F32)
        return jax.nn.sigmoid(z) * jnp.dot(o, wb[...], preferred_element_type=F32)

    merged = (gated(0, oa_ref[...], wba) + gated(1, ob_ref[...], wbb)
              + gated(2, oc.astype(MM_DTYPE), wbc))
    y = jnp.dot(merged.astype(MM_DTYPE), wo[...], preferred_element_type=F32)
    x = x + mod_ref[2:3, :] * y

    h = _rms(x) * g2_ref[...]
    h = h * (1.0 + mod_ref[4:5, :]) + mod_ref[3:4, :]
    hb = h.astype(MM_DTYPE)
    acc = jnp.zeros(x.shape, F32)
    for c in range(D_FF // FFN_CHUNK):
        cols = slice(c * FFN_CHUNK, (c + 1) * FFN_CHUNK)
        fg = jnp.dot(hb, wgu[:, cols], preferred_element_type=F32)
        fu = jnp.dot(hb, wgu[:, D_FF + c * FFN_CHUNK:D_FF + (c + 1) * FFN_CHUNK],
                     preferred_element_type=F32)
        act = (fg * jax.nn.sigmoid(fg) * fu).astype(MM_DTYPE)
        acc = acc + jnp.dot(act, wd[cols, :], preferred_element_type=F32)
    y = x + mod_ref[5:6, :] * acc
    if final_norm:
        y = _rms(y) * gf_ref[...]
    out_ref[...] = y


def _merge_ffn(x2d, mod_l, g1, g2, gf, oa, ob, ocs, lses, ws, seq, final_norm):
    n, d = x2d.shape
    tm = ROW_TILE
    per_b = seq // tm
    row = lambda wd: pl.BlockSpec((tm, wd), lambda i: (i, 0))
    vec = _const_spec((1, d))
    wid = C_HPG * HEAD_DIM
    res = [pl.BlockSpec((None, dil, tm // dil, wid), lambda i: (i // per_b, 0, i % per_b, 0))
           for _, dil in C_GROUPS]
    in_specs = [row(d), pl.BlockSpec((None, 6, d), lambda i: (i // per_b, 0, 0)), vec, vec, vec,
                row(oa.shape[1]), row(ob.shape[1])] + res + res
    in_specs += [_const_spec(w.shape) for w in ws]
    return pl.pallas_call(
        functools.partial(_merge_ffn_kernel, final_norm=final_norm),
        out_shape=jax.ShapeDtypeStruct((n, d), F32), grid=(n // tm,),
        in_specs=in_specs, out_specs=row(d),
        scratch_shapes=[pltpu.VMEM((4 * wid // LANES, tm, LANES), F32)],
        compiler_params=_cparams(1), name="merge_ffn",
    )(x2d, mod_l, g1, g2, gf, oa, ob, *ocs, *lses, *ws)


def kernel(x, c, w_ada, b_ada, g_norm1, w_in, w_uk, w_uv, g_kv, lam_q1, lam_k1, lam_q2, lam_k2,
           g_subln, w_branch_a, w_branch_b, w_branch_c, w_out, g_norm2, w_gate_up, w_down,
           rel_bias, g_final):
    bsz, seq, d = x.shape
    depth = w_ada.shape[0]
    T = ATT_TILE
    nk = seq // T
    assert d == D_MODEL and seq % T == 0 and seq % ROW_TILE == 0 and seq % IN_ROW_TILE == 0
    n = bsz * seq
    cast = lambda w: w.astype(MM_DTYPE)

    thresholds = _bucket_thresholds(seq + 2 * C_BAND * C_GROUPS[-1][1])
    assert seq <= N_OFFSETS * T or (N_OFFSETS - 2) * T + 1 >= thresholds[-1]
    tab = rel_bias.reshape(-1)
    bias_a = _bias_att_tiles(tab, 0, A_HEADS, T, thresholds)
    bias_b = _bias_att_tiles(tab, A_HEADS, B_HEADS, T, thresholds)
    bias_c = _bias_dil_tiles(tab, A_HEADS + B_HEADS, thresholds)

    mod = _modulation(c, w_ada, b_ada).reshape(depth, bsz, 6, d)

    splits = (A_HEADS * HEAD_DIM, A_LATENT, IDX_HEADS * IDX_DIM, IDX_DIM, IDX_HEADS,
              B_HEADS * 2 * HEAD_DIM, B_HEADS * 2 * HEAD_DIM, B_HEADS * 2 * HEAD_DIM,
              C_HEADS * HEAD_DIM, C_HEADS * HEAD_DIM, C_HEADS * HEAD_DIM, d, d, d)
    offs = np.concatenate([[0], np.cumsum(splits)])
    seg = lambda w, k: w[:, int(offs[k]):int(offs[k + 1])]

    x2d = x.reshape(n, d)
    for l in range(depth):
        wl = w_in[l]
        wid = C_HPG * HEAD_DIM
        w_iw = jnp.pad(seg(wl, 4), ((0, 0), (0, 16 - IDX_HEADS)))
        wT_all = cast(jnp.concatenate([seg(wl, 0), seg(wl, 2), seg(wl, 5), seg(wl, 7), w_iw], axis=1).T)
        w_all = cast(jnp.concatenate(
            [seg(wl, 1), seg(wl, 6)]
            + [seg(wl, k)[:, g * wid:(g + 1) * wid] for g in range(len(C_GROUPS)) for k in (8, 9, 10)]
            + [seg(wl, 3)], axis=1))
        ws_in = [wT_all, w_all]
        g1 = g_norm1[l].reshape(1, d)
        (aqT, iqT, iwT, bqT, bvT, ik, kv, kvT, bk, cg0, cg1, cg2) = _in_proj(
            x2d, mod[l], g1, g_kv[l].reshape(1, A_LATENT), ws_in, bsz, seq)

        o_a = _dsa(aqT, iqT, iwT,
                   ik.reshape(bsz, nk, T, IDX_DIM), kv.reshape(bsz, nk, T, A_LATENT), kvT,
                   bias_a, cast(w_uk[l].transpose(0, 2, 1)), cast(w_uv[l].transpose(0, 2, 1)),
                   far_bias=(N_OFFSETS - 2) * T + 1 >= thresholds[-1])
        o_a = o_a.reshape(n, -1)

        lam_init = 0.8 - 0.6 * math.exp(-0.3 * l)
        lam_rows = jnp.stack([lam_q1[l], lam_k1[l], lam_q2[l], lam_k2[l]])
        dv = 2 * HEAD_DIM
        o_b = _diff(bqT, bk.reshape(bsz, nk, T, B_HEADS * dv), bvT, bias_b, lam_rows,
                    g_subln[l].reshape(dv, 1), lam_init,
                    far_bias=(N_OFFSETS - 2) * T + 1 >= thresholds[-1])
        o_b = o_b.reshape(n, -1)

        ocs, lses = [], []
        for g, (cg, (window, dil)) in enumerate(zip((cg0, cg1, cg2), C_GROUPS)):
            assert window // dil == C_BAND
            o, s = _dilated_group(cg, bias_c[g * C_HPG:(g + 1) * C_HPG], g)
            ocs.append(o)
            lses.append(s)

        ws_out = [cast(wl[:, int(offs[11]):int(offs[14])]),
                  cast(w_branch_a[l]), cast(w_branch_b[l]), cast(w_branch_c[l]), cast(w_out[l]),
                  cast(w_gate_up[l]), cast(w_down[l])]
        x2d = _merge_ffn(x2d, mod[l], g1, g_norm2[l].reshape(1, d), g_final.reshape(1, d),
                         o_a, o_b, ocs, lses, ws_out, seq, final_norm=(l == depth - 1))
    return x2d.reshape(bsz, seq, d)
```
